```python
import jax, jax.numpy as jnp
from jax import lax
import numpy as np


D_MODEL = 2048
BATCH = 4
SEQ = 2048
DEPTH = 1

MEM_LEN = 256
HEAD_DIM = 128
NSA_HEADS = 8
NSA_KV_HEADS = 2
NSA_GROUP = NSA_HEADS // NSA_KV_HEADS
NSA_WIDTH = NSA_HEADS * HEAD_DIM
KV_WIDTH = NSA_KV_HEADS * HEAD_DIM
N_BRANCH = 3
CONV_WIDTH = D_MODEL - NSA_WIDTH
CONV_KERNEL = 31
CMP_LEN = 32
CMP_STRIDE = 16
SEL_BLK = 64
N_SEL = 16
WINDOW = 512
Q_BLOCK = 128
SEL_Q_CHUNK = 64
MEM_HEADS = 4
MEM_WIDTH = MEM_HEADS * HEAD_DIM
FFN_HIDDEN = -(-8 * D_MODEL // (3 * 256)) * 256
IN_SIZES = (NSA_WIDTH, KV_WIDTH, KV_WIDTH, KV_WIDTH, KV_WIDTH, KV_WIDTH, KV_WIDTH, N_BRANCH * NSA_HEADS, 2 * CONV_WIDTH)
IN_WIDTH = sum(IN_SIZES)
NEG_INF = -1e30
FORCE_BONUS = 1e3

kernel_name = 'hymba_nsa_conformer_hybrid_block'


def rms_norm(x, g, eps=1e-6):
    xf = x.astype(jnp.float32)
    y = xf * lax.rsqrt(jnp.mean(xf * xf, axis=-1, keepdims=True) + eps)
    return (y * g.astype(jnp.float32)).astype(x.dtype)


def layer_norm(x, g, b, eps=1e-5):
    xf = x.astype(jnp.float32)
    mu = jnp.mean(xf, axis=-1, keepdims=True)
    var = jnp.mean(jnp.square(xf - mu), axis=-1, keepdims=True)
    y = (xf - mu) * lax.rsqrt(var + eps) * g.astype(jnp.float32) + b.astype(jnp.float32)
    return y.astype(x.dtype)


def masked_softmax(s, mask, axis=-1):
    p = jax.nn.softmax(jnp.where(mask, s, NEG_INF), axis=axis)
    return p * mask


def alibi_slopes(n):
    return jnp.exp2(-8.0 * (jnp.arange(n, dtype=jnp.float32) + 1.0) / n)


def compress(kv, pos_emb, w1, w2):
    B, T, G, dh = kv.shape
    n_cmp = (T - CMP_LEN) // CMP_STRIDE + 1
    idx = jnp.arange(n_cmp)[:, None] * CMP_STRIDE + jnp.arange(CMP_LEN)[None, :]
    blocks = kv[:, idx] + pos_emb[:, None, :]
    blocks = blocks.transpose(0, 1, 3, 2, 4).reshape(B, n_cmp, G, CMP_LEN * dh)
    return jax.nn.silu(blocks @ w1) @ w2


def compressed_attention(q, k, v, slopes):
    B, T, G, Hg, dh = q.shape
    n_cmp = k.shape[1]
    s = jnp.einsum('btghd,bcgd->bghtc', q, k).astype(jnp.float32) * dh ** -0.5
    t = jnp.arange(T)
    start = jnp.arange(n_cmp) * CMP_STRIDE
    mid = start.astype(jnp.float32) + (CMP_LEN - 1) / 2.0
    dist = t[:, None].astype(jnp.float32) - mid[None, :]
    mask = (start[None, :] + CMP_LEN - 1) <= t[:, None]
    s = s - slopes[None, :, :, None, None] * dist
    p = masked_softmax(s, mask)
    o = jnp.einsum('bghtc,bcgd->btghd', p.astype(v.dtype), v)
    return o, p


def select_blocks(p_cmp, T):
    n_cmp = p_cmp.shape[-1]
    n_blk = T // SEL_BLK
    n_sel = min(N_SEL, n_blk)
    cs = jnp.arange(n_cmp)[:, None] * CMP_STRIDE
    bs = jnp.arange(n_blk)[None, :] * SEL_BLK
    overlap = jnp.clip(jnp.minimum(cs + CMP_LEN, bs + SEL_BLK) - jnp.maximum(cs, bs), 0, None).astype(jnp.float32) / CMP_LEN
    imp = jnp.einsum('bghtc,cs->bgts', p_cmp, overlap)
    t = jnp.arange(T)[:, None]
    blk = jnp.arange(n_blk)[None, :]
    cur = t // SEL_BLK
    valid = blk * SEL_BLK <= t
    forced = (blk == 0) | (blk == cur) | (blk == cur - 1)
    score = jnp.where(valid, imp + FORCE_BONUS * forced, NEG_INF)
    _, idx = lax.top_k(score, n_sel)
    return idx


def selected_attention(q, k, v, idx, slopes):
    B, T, G, Hg, dh = q.shape
    n_blk = T // SEL_BLK
    n_sel = idx.shape[-1]
    n_ch = T // SEL_Q_CHUNK
    kb = k.reshape(B, n_blk, SEL_BLK, G, dh).transpose(0, 3, 1, 2, 4)
    vb = v.reshape(B, n_blk, SEL_BLK, G, dh).transpose(0, 3, 1, 2, 4)
    q_ch = q.reshape(B, n_ch, SEL_Q_CHUNK, G, Hg, dh).transpose(1, 0, 2, 3, 4, 5)
    i_ch = idx.reshape(B, G, n_ch, SEL_Q_CHUNK, n_sel).transpose(2, 0, 1, 3, 4)
    t_ch = jnp.arange(T).reshape(n_ch, SEL_Q_CHUNK)
    bi = jnp.arange(B)[:, None, None, None]
    gi = jnp.arange(G)[None, :, None, None]
    offs = jnp.arange(SEL_BLK)

    def chunk(args):
        qc, ic, tc = args
        ks = kb[bi, gi, ic]
        vs = vb[bi, gi, ic]
        s = jnp.einsum('bqghd,bgqnkd->bghqnk', qc, ks).astype(jnp.float32) * dh ** -0.5
        pos = ic[..., None] * SEL_BLK + offs
        dist = (tc[:, None, None] - pos)[:, :, None]
        s = s - slopes[None, :, :, None, None, None] * dist.astype(jnp.float32)
        p = masked_softmax(s, dist >= 0, axis=(-2, -1))
        return jnp.einsum('bghqnk,bgqnkd->bqghd', p.astype(vs.dtype), vs)

    o = lax.map(chunk, (q_ch, i_ch, t_ch))
    return o.transpose(1, 0, 2, 3, 4, 5).reshape(B, T, G, Hg, dh)


def window_attention(q, k, v, slopes):
    B, T, G, Hg, dh = q.shape
    nb = T // Q_BLOCK
    span = WINDOW + Q_BLOCK
    pad = ((0, 0), (WINDOW, 0), (0, 0), (0, 0))
    kp = jnp.pad(k, pad)
    vp = jnp.pad(v, pad)
    idx = jnp.arange(nb)[:, None] * Q_BLOCK + jnp.arange(span)[None, :]
    kw = kp[:, idx]
    vw = vp[:, idx]
    qb = q.reshape(B, nb, Q_BLOCK, G, Hg, dh)
    s = jnp.einsum('bnqghd,bnkgd->bnghqk', qb, kw).astype(jnp.float32) * dh ** -0.5
    t = jnp.arange(nb)[:, None] * Q_BLOCK + jnp.arange(Q_BLOCK)[None, :]
    spos = idx - WINDOW
    dist = t[:, :, None] - spos[:, None, :]
    mask = (dist >= 0) & (dist < WINDOW) & (spos[:, None, :] >= 0)
    s = s - slopes[None, None, :, :, None, None] * dist[None, :, None, None].astype(jnp.float32)
    p = masked_softmax(s, mask[None, :, None, None])
    o = jnp.einsum('bnghqk,bnkgd->bnqghd', p.astype(vw.dtype), vw)
    return o.reshape(B, T, G, Hg, dh)


def conformer_conv(u, conv_w, conv_b, ln_g, ln_b):
    a, b = jnp.split(u, 2, axis=-1)
    h = a * jax.nn.sigmoid(b)
    C = h.shape[-1]
    h = lax.conv_general_dilated(h, conv_w[:, None, :].astype(h.dtype), (1,), [(CONV_KERNEL - 1, 0)],
                                 dimension_numbers=('NWC', 'WIO', 'NWC'), feature_group_count=C) + conv_b
    return jax.nn.silu(layer_norm(h, ln_g, ln_b))


def memory_cross_attention(hq, hm, w_mq, w_mk, w_mv, mq_norm, mk_norm, w_mo):
    B, T, _ = hq.shape
    M = hm.shape[1]
    q = rms_norm((hq @ w_mq).reshape(B, T, MEM_HEADS, HEAD_DIM), mq_norm)
    k = rms_norm((hm @ w_mk).reshape(B, M, MEM_HEADS, HEAD_DIM), mk_norm)
    v = (hm @ w_mv).reshape(B, M, MEM_HEADS, HEAD_DIM)
    s = jnp.einsum('bthd,bmhd->bhtm', q, k).astype(jnp.float32) * HEAD_DIM ** -0.5
    p = jax.nn.softmax(s, axis=-1)
    o = jnp.einsum('bhtm,bmhd->bthd', p.astype(v.dtype), v).reshape(B, T, MEM_WIDTH)
    return o @ w_mo


def swiglu(h, w_gate, w_up, w_down):
    return (jax.nn.silu(h @ w_gate) * (h @ w_up)) @ w_down


def setup_inputs(seed: int = 0) -> dict:
    key = jax.random.key(seed)
    ks = jax.random.split(key, 40)
    f32 = jnp.float32

    def nrm(k, shape, scale):
        return jax.random.normal(k, shape, f32) * scale

    def gain(k, n):
        return 1.0 + 0.01 * jax.random.normal(k, (DEPTH, n), f32)

    L, dh = DEPTH, HEAD_DIM
    return {
        'x': nrm(ks[0], (BATCH, SEQ, D_MODEL), 1.0),
        'mem': nrm(ks[1], (BATCH, MEM_LEN, D_MODEL), 1.0),
        'norm_mix': gain(ks[2], D_MODEL),
        'w_in': nrm(ks[3], (L, D_MODEL, IN_WIDTH), D_MODEL ** -0.5),
        'gate_b': nrm(ks[4], (L, N_BRANCH * NSA_HEADS), 0.01),
        'q_norm': gain(ks[5], dh),
        'k_norm_cmp': gain(ks[6], dh),
        'k_norm_slc': gain(ks[7], dh),
        'k_norm_win': gain(ks[8], dh),
        'cmp_pos_k': nrm(ks[9], (L, CMP_LEN, dh), 0.1),
        'cmp_pos_v': nrm(ks[10], (L, CMP_LEN, dh), 0.1),
        'cmp_k_w1': nrm(ks[11], (L, CMP_LEN * dh, dh), (CMP_LEN * dh) ** -0.5),
        'cmp_k_w2': nrm(ks[12], (L, dh, dh), dh ** -0.5),
        'cmp_v_w1': nrm(ks[13], (L, CMP_LEN * dh, dh), (CMP_LEN * dh) ** -0.5),
        'cmp_v_w2': nrm(ks[14], (L, dh, dh), dh ** -0.5),
        'conv_w': nrm(ks[15], (L, CONV_KERNEL, CONV_WIDTH), CONV_KERNEL ** -0.5),
        'conv_b': nrm(ks[16], (L, CONV_WIDTH), 0.01),
        'conv_ln_g': gain(ks[17], CONV_WIDTH),
        'conv_ln_b': nrm(ks[18], (L, CONV_WIDTH), 0.01),
        'w_out': nrm(ks[19], (L, NSA_WIDTH + CONV_WIDTH, D_MODEL), (NSA_WIDTH + CONV_WIDTH) ** -0.5),
        'norm_mem_q': gain(ks[20], D_MODEL),
        'norm_mem_kv': gain(ks[21], D_MODEL),
        'w_mq': nrm(ks[22], (L, D_MODEL, MEM_WIDTH), D_MODEL ** -0.5),
        'w_mk': nrm(ks[23], (L, D_MODEL, MEM_WIDTH), D_MODEL ** -0.5),
        'w_mv': nrm(ks[24], (L, D_MODEL, MEM_WIDTH), D_MODEL ** -0.5),
        'mq_norm': gain(ks[25], dh),
        'mk_norm': gain(ks[26], dh),
        'w_mo': nrm(ks[27], (L, MEM_WIDTH, D_MODEL), MEM_WIDTH ** -0.5),
        'norm_ffn': gain(ks[28], D_MODEL),
        'w_gate': nrm(ks[29], (L, D_MODEL, FFN_HIDDEN), D_MODEL ** -0.5),
        'w_up': nrm(ks[30], (L, D_MODEL, FFN_HIDDEN), D_MODEL ** -0.5),
        'w_down': nrm(ks[31], (L, FFN_HIDDEN, D_MODEL), FFN_HIDDEN ** -0.5),
    }


def reference(x, mem, norm_mix, w_in, gate_b, q_norm, k_norm_cmp, k_norm_slc, k_norm_win,
              cmp_pos_k, cmp_pos_v, cmp_k_w1, cmp_k_w2, cmp_v_w1, cmp_v_w2,
              conv_w, conv_b, conv_ln_g, conv_ln_b, w_out,
              norm_mem_q, norm_mem_kv, w_mq, w_mk, w_mv, mq_norm, mk_norm, w_mo,
              norm_ffn, w_gate, w_up, w_down):
    B, T, _ = x.shape
    G, Hg, dh = NSA_KV_HEADS, NSA_GROUP, HEAD_DIM
    slopes = alibi_slopes(NSA_HEADS).reshape(G, Hg)
    splits = np.cumsum(IN_SIZES)[:-1].tolist()
    for l in range(DEPTH):
        h = rms_norm(x, norm_mix[l])
        z = h @ w_in[l]
        q, kc, vc, ksl, vsl, kw, vw, g, u = jnp.split(z, splits, axis=-1)
        q = rms_norm(q.reshape(B, T, NSA_HEADS, dh), q_norm[l]).reshape(B, T, G, Hg, dh)
        kv_shape = (B, T, G, dh)
        k_cmp = rms_norm(compress(kc.reshape(kv_shape), cmp_pos_k[l], cmp_k_w1[l], cmp_k_w2[l]), k_norm_cmp[l])
        v_cmp = compress(vc.reshape(kv_shape), cmp_pos_v[l], cmp_v_w1[l], cmp_v_w2[l])
        o_cmp, p_cmp = compressed_attention(q, k_cmp, v_cmp, slopes)
        sel_idx = select_blocks(p_cmp, T)
        o_slc = selected_attention(q, rms_norm(ksl.reshape(kv_shape), k_norm_slc[l]), vsl.reshape(kv_shape), sel_idx, slopes)
        o_win = window_attention(q, rms_norm(kw.reshape(kv_shape), k_norm_win[l]), vw.reshape(kv_shape), slopes)
        gates = jax.nn.sigmoid(g + gate_b[l]).reshape(B, T, G, Hg, N_BRANCH)
        o_nsa = (gates[..., 0:1] * o_cmp + gates[..., 1:2] * o_slc + gates[..., 2:3] * o_win).reshape(B, T, NSA_WIDTH)
        o_conv = conformer_conv(u, conv_w[l], conv_b[l], conv_ln_g[l], conv_ln_b[l])
        x = x + jnp.concatenate([o_nsa, o_conv], axis=-1) @ w_out[l]
        x = x + memory_cross_attention(rms_norm(x, norm_mem_q[l]), rms_norm(mem, norm_mem_kv[l]),
                                       w_mq[l], w_mk[l], w_mv[l], mq_norm[l], mk_norm[l], w_mo[l])
        x = x + swiglu(rms_norm(x, norm_ffn[l]), w_gate[l], w_up[l], w_down[l])
    return x
```

```python
import functools

import numpy as np
import jax
import jax.numpy as jnp
from jax import lax
from jax.experimental import pallas as pl
from jax.experimental.pallas import tpu as pltpu

F32 = jnp.float32
BF16 = jnp.bfloat16

HEAD_DIM = 128
NSA_HEADS = 8
NSA_KV_HEADS = 2
NSA_GROUP = NSA_HEADS // NSA_KV_HEADS
NSA_WIDTH = NSA_HEADS * HEAD_DIM
KV_WIDTH = NSA_KV_HEADS * HEAD_DIM
N_BRANCH = 3
CONV_KERNEL = 31
CMP_LEN = 32
CMP_STRIDE = 16
SEL_BLK = 64
N_SEL = 16
WINDOW = 512
MEM_HEADS = 4
NEG_INF = -1e30
FORCE_BONUS = 1e3
RMS_EPS = 1e-6
LN_EPS = 1e-5

LANES = 128
VMEM_LIMIT_BYTES = 56 * 1024 * 1024

Q_TILE = 128
SEL_CHUNK = 256
WIN_CHUNK = 128
CONV_HALO = 32


def _cparams(*sem):
    return pltpu.CompilerParams(dimension_semantics=sem, vmem_limit_bytes=VMEM_LIMIT_BYTES)


def _rms(x, g, eps=RMS_EPS):
    ms = jnp.mean(x * x, axis=-1, keepdims=True)
    return x * lax.rsqrt(ms + eps) * g


def _silu(x):
    return x * jax.nn.sigmoid(x)


def _norm_to_scratch(x_ref, g_ref, xn_ref):
    @pl.when(pl.program_id(1) == 0)
    def _():
        xn_ref[...] = _rms(x_ref[...], g_ref[...]).astype(xn_ref.dtype)


def _norm_mm_body(x_ref, g_ref, w_ref, o_ref, xn_ref):
    _norm_to_scratch(x_ref, g_ref, xn_ref)
    o_ref[...] = jnp.dot(xn_ref[...], w_ref[...], preferred_element_type=F32).astype(o_ref.dtype)


def _norm_mm_gate_body(x_ref, g_ref, w_ref, wg_ref, o_ref, og_ref, xn_ref):
    _norm_to_scratch(x_ref, g_ref, xn_ref)

    @pl.when(pl.program_id(1) == 0)
    def _():
        og_ref[...] = jnp.dot(xn_ref[...], wg_ref[...], preferred_element_type=F32)

    o_ref[...] = jnp.dot(xn_ref[...], w_ref[...], preferred_element_type=F32).astype(o_ref.dtype)


def _norm_swiglu_body(x_ref, g_ref, wg_ref, wu_ref, o_ref, xn_ref):
    _norm_to_scratch(x_ref, g_ref, xn_ref)
    xn = xn_ref[...]
    a = jnp.dot(xn, wg_ref[...], preferred_element_type=F32)
    b = jnp.dot(xn, wu_ref[...], preferred_element_type=F32)
    o_ref[...] = (_silu(a) * b).astype(o_ref.dtype)


def norm_matmul(x, g, w, *, tm, tn, out_dtype=F32):
    M, K = x.shape
    N = w.shape[1]
    return pl.pallas_call(
        _norm_mm_body,
        grid=(M // tm, N // tn),
        in_specs=[pl.BlockSpec((tm, K), lambda i, j: (i, 0)),
                  pl.BlockSpec((1, K), lambda i, j: (0, 0)),
                  pl.BlockSpec((K, tn), lambda i, j: (0, j))],
        out_specs=pl.BlockSpec((tm, tn), lambda i, j: (i, j)),
        out_shape=jax.ShapeDtypeStruct((M, N), out_dtype),
        scratch_shapes=[pltpu.VMEM((tm, K), BF16)],
        compiler_params=_cparams("parallel", "arbitrary"),
        name="norm_matmul",
    )(x, g, w)


def norm_matmul_gate(x, g, w, wg, *, tm, tn):
    M, K = x.shape
    N = w.shape[1]
    NG = wg.shape[1]
    return pl.pallas_call(
        _norm_mm_gate_body,
        grid=(M // tm, N // tn),
        in_specs=[pl.BlockSpec((tm, K), lambda i, j: (i, 0)),
                  pl.BlockSpec((1, K), lambda i, j: (0, 0)),
                  pl.BlockSpec((K, tn), lambda i, j: (0, j)),
                  pl.BlockSpec((K, NG), lambda i, j: (0, 0))],
        out_specs=[pl.BlockSpec((tm, tn), lambda i, j: (i, j)),
                   pl.BlockSpec((tm, NG), lambda i, j: (i, 0))],
        out_shape=[jax.ShapeDtypeStruct((M, N), F32),
                   jax.ShapeDtypeStruct((M, NG), F32)],
        scratch_shapes=[pltpu.VMEM((tm, K), BF16)],
        compiler_params=_cparams("parallel", "arbitrary"),
        name="in_proj",
    )(x, g, w, wg)


def norm_swiglu(x, g, w_gate, w_up, *, tm, tn):
    M, K = x.shape
    N = w_gate.shape[1]
    return pl.pallas_call(
        _norm_swiglu_body,
        grid=(M // tm, N // tn),
        in_specs=[pl.BlockSpec((tm, K), lambda i, j: (i, 0)),
                  pl.BlockSpec((1, K), lambda i, j: (0, 0)),
                  pl.BlockSpec((K, tn), lambda i, j: (0, j)),
                  pl.BlockSpec((K, tn), lambda i, j: (0, j))],
        out_specs=pl.BlockSpec((tm, tn), lambda i, j: (i, j)),
        out_shape=jax.ShapeDtypeStruct((M, N), BF16),
        scratch_shapes=[pltpu.VMEM((tm, K), BF16)],
        compiler_params=_cparams("parallel", "arbitrary"),
        name="ffn_up",
    )(x, g, w_gate, w_up)


def _mm_res_body(a_ref, w_ref, r_ref, o_ref):
    o_ref[...] = r_ref[...] + jnp.dot(a_ref[...], w_ref[...], preferred_element_type=F32)


def matmul_residual(a, w, res, *, tm, tn):
    M, K = a.shape
    N = w.shape[1]
    return pl.pallas_call(
        _mm_res_body,
        grid=(M // tm, N // tn),
        in_specs=[pl.BlockSpec((tm, K), lambda i, j: (i, 0)),
                  pl.BlockSpec((K, tn), lambda i, j: (0, j)),
                  pl.BlockSpec((tm, tn), lambda i, j: (i, j))],
        out_specs=pl.BlockSpec((tm, tn), lambda i, j: (i, j)),
        out_shape=jax.ShapeDtypeStruct((M, N), F32),
        compiler_params=_cparams("parallel", "arbitrary"),
        name="matmul_residual",
    )(a, w, res)


def _mm2_res_body(a_ref, b_ref, wa_ref, wb_ref, r_ref, o_ref):
    acc = jnp.dot(a_ref[...], wa_ref[...], preferred_element_type=F32)
    acc = acc + jnp.dot(b_ref[...], wb_ref[...], preferred_element_type=F32)
    o_ref[...] = r_ref[...] + acc


def concat_matmul_residual(a, b, w, res, *, tm, tn):
    M, Ka = a.shape
    Kb = b.shape[1]
    assert Ka == Kb and w.shape[0] == Ka + Kb
    N = w.shape[1]
    return pl.pallas_call(
        _mm2_res_body,
        grid=(M // tm, N // tn),
        in_specs=[pl.BlockSpec((tm, Ka), lambda i, j: (i, 0)),
                  pl.BlockSpec((tm, Kb), lambda i, j: (i, 0)),
                  pl.BlockSpec((Ka, tn), lambda i, j: (0, j)),
                  pl.BlockSpec((Kb, tn), lambda i, j: (1, j)),
                  pl.BlockSpec((tm, tn), lambda i, j: (i, j))],
        out_specs=pl.BlockSpec((tm, tn), lambda i, j: (i, j)),
        out_shape=jax.ShapeDtypeStruct((M, N), F32),
        compiler_params=_cparams("parallel", "arbitrary"),
        name="out_proj",
    )(a, b, w, w, res)


def _kv_prep_body(c_ref, s_ref, w_ref, kns_ref, knw_ref,
                  kc_o, vc_o, ks_o, vsT_o, kw_o, vwT_o, *, tt):
    dh = HEAD_DIM
    for g in range(NSA_KV_HEADS):
        k_cols = slice(g * dh, (g + 1) * dh)
        v_cols = slice(KV_WIDTH + g * dh, KV_WIDTH + (g + 1) * dh)
        kc_o[0, g] = c_ref[:, k_cols]
        vc_o[0, g] = c_ref[:, v_cols]
        ks_o[0, g] = _rms(s_ref[:, k_cols], kns_ref[...]).astype(BF16)
        kw_o[0, g] = _rms(w_ref[:, k_cols], knw_ref[...]).astype(BF16)
        vsT = s_ref[:, v_cols].T.astype(BF16)
        for c in range(tt // SEL_CHUNK):
            vsT_o[0, g, c] = vsT[:, c * SEL_CHUNK:(c + 1) * SEL_CHUNK]
        vwT = w_ref[:, v_cols].T.astype(BF16)
        for c in range(tt // WIN_CHUNK):
            vwT_o[0, g, c] = vwT[:, c * WIN_CHUNK:(c + 1) * WIN_CHUNK]


def kv_prep(z, kn_slc, kn_win, *, B, T, col0, tt=256):
    G, dh = NSA_KV_HEADS, HEAD_DIM
    nt = T // tt
    cb = col0 // (2 * KV_WIDTH)
    row = lambda b, t: b * nt + t
    return pl.pallas_call(
        functools.partial(_kv_prep_body, tt=tt),
        grid=(B, nt),
        in_specs=[pl.BlockSpec((tt, 2 * KV_WIDTH), lambda b, t: (row(b, t), cb)),
                  pl.BlockSpec((tt, 2 * KV_WIDTH), lambda b, t: (row(b, t), cb + 1)),
                  pl.BlockSpec((tt, 2 * KV_WIDTH), lambda b, t: (row(b, t), cb + 2)),
                  pl.BlockSpec((1, dh), lambda b, t: (0, 0)),
                  pl.BlockSpec((1, dh), lambda b, t: (0, 0))],
        out_specs=[pl.BlockSpec((1, G, tt, dh), lambda b, t: (b, 0, t, 0)),
                   pl.BlockSpec((1, G, tt, dh), lambda b, t: (b, 0, t, 0)),
                   pl.BlockSpec((1, G, tt, dh), lambda b, t: (b, 0, t, 0)),
                   pl.BlockSpec((1, G, tt // SEL_CHUNK, dh, SEL_CHUNK), lambda b, t: (b, 0, t, 0, 0)),
                   pl.BlockSpec((1, G, tt, dh), lambda b, t: (b, 0, t, 0)),
                   pl.BlockSpec((1, G, tt // WIN_CHUNK, dh, WIN_CHUNK), lambda b, t: (b, 0, t, 0, 0))],
        out_shape=[jax.ShapeDtypeStruct((B, G, T, dh), F32),
                   jax.ShapeDtypeStruct((B, G, T, dh), F32),
                   jax.ShapeDtypeStruct((B, G, T, dh), BF16),
                   jax.ShapeDtypeStruct((B, G, T // SEL_CHUNK, dh, SEL_CHUNK), BF16),
                   jax.ShapeDtypeStruct((B, G, T, dh), BF16),
                   jax.ShapeDtypeStruct((B, G, T // WIN_CHUNK, dh, WIN_CHUNK), BF16)],
        compiler_params=_cparams("parallel", "parallel"),
        name="kv_prep",
    )(z, z, z, kn_slc, kn_win)


def _compress_body(xk_ref, xv_ref, pk_ref, pv_ref, w1k_ref, w2k_ref, w1v_ref, w2v_ref, kn_ref,
                   kc_o, vcT_o):
    def phi(x_ref, p_ref, w1_ref, w2_ref):
        x = x_ref[0, 0]
        first = jnp.dot((x + p_ref[0:1, :]).astype(BF16), w1_ref[0], preferred_element_type=F32)
        second = jnp.dot((x + p_ref[1:2, :]).astype(BF16), w1_ref[1], preferred_element_type=F32)
        n = second.shape[0]
        h = first + pltpu.roll(second, n - 1, 0)
        return jnp.dot(_silu(h).astype(BF16), w2_ref[...], preferred_element_type=F32)

    kc_o[0, 0] = _rms(phi(xk_ref, pk_ref, w1k_ref, w2k_ref), kn_ref[...]).astype(BF16)
    vcT_o[0, 0] = phi(xv_ref, pv_ref, w1v_ref, w2v_ref).T.astype(BF16)


def compress_kv(kc, vc, pos_k, pos_v, w1k, w2k, w1v, w2v, kn_cmp):
    B, G, T, dh = kc.shape
    assert CMP_LEN == 2 * CMP_STRIDE
    nc = T // CMP_STRIDE
    half = CMP_STRIDE * dh
    xk = kc.reshape(B, G, nc, half)
    xv = vc.reshape(B, G, nc, half)
    x_spec = pl.BlockSpec((1, 1, nc, half), lambda b, g: (b, g, 0, 0))
    full = lambda shape: pl.BlockSpec(shape, lambda b, g: (0,) * len(shape))
    o_spec = pl.BlockSpec((1, 1, nc, dh), lambda b, g: (b, g, 0, 0))
    return pl.pallas_call(
        _compress_body,
        grid=(B, G),
        in_specs=[x_spec, x_spec, full((2, half)), full((2, half)),
                  full((2, half, dh)), full((dh, dh)), full((2, half, dh)), full((dh, dh)),
                  full((1, dh))],
        out_specs=[o_spec, pl.BlockSpec((1, 1, dh, nc), lambda b, g: (b, g, 0, 0))],
        out_shape=[jax.ShapeDtypeStruct((B, G, nc, dh), BF16),
                   jax.ShapeDtypeStruct((B, G, dh, nc), BF16)],
        compiler_params=_cparams("parallel", "parallel"),
        name="compress_kv",
    )(xk, xv, pos_k.reshape(2, half), pos_v.reshape(2, half),
      w1k.reshape(2, half, dh), w2k, w1v.reshape(2, half, dh), w2v, kn_cmp)


def _nsa_body(q_ref, gl_ref, gb_ref, qn_ref, slope_ref, kc_ref, vcT_ref,
              ks_ref, vsT_ref, kw_ref, vwT_ref, ovT_ref, blk_ref, o_ref, score_ref, gate_ref):
    dh, hg, tq = HEAD_DIM, NSA_GROUP, Q_TILE
    cols = hg * tq
    qi = pl.program_id(2)
    t0 = qi * tq
    slope = slope_ref[0]
    contract_last = (((1,), (1,)), ((), ()))

    scale = dh ** -0.5
    q_rows = [(_rms(q_ref[:, h * dh:(h + 1) * dh], qn_ref[...]) * scale).astype(BF16)
              for h in range(hg)]
    q_all = jnp.concatenate(q_rows, axis=0)

    def t_of(shape):
        lane = lax.broadcasted_iota(jnp.int32, shape, 1)
        return t0 + (lane & (tq - 1))

    def softmax_step(s, mask, vT, m, l, acc):
        s = jnp.where(mask, s, NEG_INF)
        m_new = jnp.maximum(m, jnp.max(s, axis=0, keepdims=True))
        alpha = jnp.exp(m - m_new)
        p = jnp.where(mask, jnp.exp(s - m_new), 0.0)
        l = alpha * l + jnp.sum(p, axis=0, keepdims=True)
        acc = alpha * acc + jnp.dot(vT, p.astype(BF16), preferred_element_type=F32)
        return m_new, l, acc, p

    def finish(l, acc):
        return acc * jnp.where(l > 0.0, 1.0 / l, 0.0)

    init = (jnp.full((1, cols), NEG_INF, F32), jnp.zeros((1, cols), F32),
            jnp.zeros((dh, cols), F32))

    nc = kc_ref.shape[2]
    s = lax.dot_general(kc_ref[0, 0], q_all, contract_last, preferred_element_type=F32)
    c_idx = lax.broadcasted_iota(jnp.int32, (nc, cols), 0)
    t_c = t_of((nc, cols))
    start = c_idx * CMP_STRIDE
    dist = t_c.astype(F32) - (start.astype(F32) + (CMP_LEN - 1) / 2.0)
    mask = (start + (CMP_LEN - 1)) <= t_c
    _, l_c, acc_c, p_c = softmax_step(s - slope * dist, mask, vcT_ref[0, 0], *init)
    inv_c = jnp.where(l_c > 0.0, 1.0 / l_c, 0.0)
    o_cmp = acc_c * inv_c
    p_c = p_c * inv_c

    p_sum = p_c[:, 0:tq]
    for h in range(1, hg):
        p_sum = p_sum + p_c[:, h * tq:(h + 1) * tq]
    hi = p_sum.astype(BF16)
    rem = p_sum - hi.astype(F32)
    mid = rem.astype(BF16)
    lo = (rem - mid.astype(F32)).astype(BF16)
    ovT = ovT_ref[...]
    imp = (jnp.dot(ovT, hi, preferred_element_type=F32)
           + jnp.dot(ovT, mid, preferred_element_type=F32)
           + jnp.dot(ovT, lo, preferred_element_type=F32))
    n_blk = imp.shape[0]
    blk = lax.broadcasted_iota(jnp.int32, (n_blk, tq), 0)
    t_b = t_of((n_blk, tq))
    cur = jnp.right_shift(t_b, SEL_BLK.bit_length() - 1)
    valid = blk * SEL_BLK <= t_b
    forced = (blk == 0) | (blk == cur) | (blk == cur - 1)
    score = jnp.where(valid, imp + FORCE_BONUS * forced.astype(F32), NEG_INF)
    score_ref[...] = score
    rank = jnp.zeros((n_blk, tq), F32)
    for j in range(n_blk):
        sj = score_ref[j:j + 1, :]
        beats = (sj > score) | ((sj == score) & (blk > j))
        rank = rank + beats.astype(F32)
    sel = (rank < float(min(N_SEL, n_blk))).astype(BF16)
    sel_all = jnp.concatenate([sel] * hg, axis=1)

    ck = SEL_CHUNK
    d0 = (lax.broadcasted_iota(jnp.int32, (ck, cols), 1) & (tq - 1)) \
        - lax.broadcasted_iota(jnp.int32, (ck, cols), 0)

    def sel_step(c, carry):
        base = pl.multiple_of(c * ck, ck)
        k = ks_ref[0, 0, pl.ds(base, ck), :]
        s = lax.dot_general(k, q_all, contract_last, preferred_element_type=F32)
        chosen = jnp.dot(blk_ref[pl.ds(base, ck), :], sel_all, preferred_element_type=F32)
        dist = d0 + (t0 - base)
        mask = (dist >= 0) & (chosen > 0.5)
        m, l, acc, _ = softmax_step(s - slope * dist.astype(F32), mask, vsT_ref[0, 0, c], *carry)
        return m, l, acc

    n_sel_chunks = (t0 + tq - 1) // ck + 1
    _, l_s, acc_s = lax.fori_loop(0, n_sel_chunks, sel_step, init)
    o_slc = finish(l_s, acc_s)

    cw = WIN_CHUNK
    d0w = (lax.broadcasted_iota(jnp.int32, (cw, cols), 1) & (tq - 1)) \
        - lax.broadcasted_iota(jnp.int32, (cw, cols), 0)

    def win_step(c, carry):
        base = pl.multiple_of(c * cw, cw)
        k = kw_ref[0, 0, pl.ds(base, cw), :]
        s = lax.dot_general(k, q_all, contract_last, preferred_element_type=F32)
        dist = d0w + (t0 - base)
        mask = (dist >= 0) & (dist < WINDOW)
        m, l, acc, _ = softmax_step(s - slope * dist.astype(F32), mask, vwT_ref[0, 0, c], *carry)
        return m, l, acc

    first_chunk = jnp.maximum(t0 - WINDOW, 0) // cw
    last_chunk = (t0 + tq - 1) // cw
    _, l_w, acc_w = lax.fori_loop(first_chunk, last_chunk + 1, win_step, init)
    o_win = finish(l_w, acc_w)

    gate_ref[...] = jax.nn.sigmoid(gl_ref[...] + gb_ref[...]).T
    for h in range(hg):
        cs = slice(h * tq, (h + 1) * tq)
        r = h * N_BRANCH
        o_h = (gate_ref[r:r + 1, :] * o_cmp[:, cs] + gate_ref[r + 1:r + 2, :] * o_slc[:, cs]
               + gate_ref[r + 2:r + 3, :] * o_win[:, cs])
        o_ref[:, h * dh:(h + 1) * dh] = o_h.T.astype(o_ref.dtype)


def nsa_attention(z, gl, gate_b, q_norm, kcmp, vcmpT, ks, vsT, kw, vwT, *, B, T):
    G, hg, dh, tq = NSA_KV_HEADS, NSA_GROUP, HEAD_DIM, Q_TILE
    nq = T // tq
    nc = kcmp.shape[2]
    n_blk = T // SEL_BLK
    cols = hg * tq
    assert SEL_CHUNK % SEL_BLK == 0 and T % SEL_CHUNK == 0 and gl.shape[1] == G * LANES

    slopes = np.exp2(-8.0 * (np.arange(NSA_HEADS, dtype=np.float64) + 1.0) / NSA_HEADS)
    slope_rows = np.repeat(slopes.reshape(G, hg), tq, axis=1).reshape(G, 1, cols).astype(np.float32)
    cs = np.arange(nc)[:, None] * CMP_STRIDE
    bs = np.arange(n_blk)[None, :] * SEL_BLK
    overlap = np.clip(np.minimum(cs + CMP_LEN, bs + SEL_BLK) - np.maximum(cs, bs), 0, None) / CMP_LEN
    overlap[(T - CMP_LEN) // CMP_STRIDE + 1:] = 0.0
    ovT = jnp.asarray(overlap.T, dtype=BF16)
    blk_of_key = (np.arange(T)[:, None] // SEL_BLK == np.arange(n_blk)[None, :])
    blk_onehot = jnp.asarray(blk_of_key, dtype=BF16)

    row = lambda b, qi: b * nq + qi
    per_bg = lambda shape: pl.BlockSpec((1, 1) + shape, lambda b, g, qi: (b, g) + (0,) * len(shape))
    const = lambda shape: pl.BlockSpec(shape, lambda b, g, qi: (0,) * len(shape))
    return pl.pallas_call(
        _nsa_body,
        grid=(B, G, nq),
        in_specs=[pl.BlockSpec((tq, hg * dh), lambda b, g, qi: (row(b, qi), g)),
                  pl.BlockSpec((tq, LANES), lambda b, g, qi: (row(b, qi), g)),
                  pl.BlockSpec((1, LANES), lambda b, g, qi: (0, g)),
                  const((1, dh)),
                  pl.BlockSpec((1, 1, cols), lambda b, g, qi: (g, 0, 0)),
                  per_bg((nc, dh)), per_bg((dh, nc)),
                  per_bg((T, dh)), per_bg((T // SEL_CHUNK, dh, SEL_CHUNK)),
                  per_bg((T, dh)), per_bg((T // WIN_CHUNK, dh, WIN_CHUNK)),
                  const((n_blk, nc)), const((T, n_blk))],
        out_specs=pl.BlockSpec((tq, hg * dh), lambda b, g, qi: (row(b, qi), g)),
        out_shape=jax.ShapeDtypeStruct((B * T, NSA_WIDTH), BF16),
        scratch_shapes=[pltpu.VMEM((n_blk, tq), F32), pltpu.VMEM((LANES, tq), F32)],
        compiler_params=_cparams("parallel", "parallel", "arbitrary"),
        name="nsa_attention",
    )(z, gl, gate_b, q_norm, jnp.asarray(slope_rows), kcmp, vcmpT, ks, vsT, kw, vwT, ovT, blk_onehot)


def _conv_body(a_ref, b_ref, ah_ref, bh_ref, w_ref, cb_ref, lg_ref, lb_ref, o_ref, h_ref, y_ref, *, tt):
    halo = ah_ref[...] * jax.nn.sigmoid(bh_ref[...])
    h_ref[0:CONV_HALO, :] = jnp.where(pl.program_id(1) == 0, 0.0, halo)
    h_ref[CONV_HALO:CONV_HALO + tt, :] = a_ref[...] * jax.nn.sigmoid(b_ref[...])
    C = a_ref.shape[1]
    lead = CONV_HALO - (CONV_KERNEL - 1)
    for s in range(C // LANES):
        cs = slice(s * LANES, (s + 1) * LANES)
        acc = jnp.zeros((tt, LANES), F32)
        for k in range(CONV_KERNEL):
            acc = acc + w_ref[k:k + 1, cs] * h_ref[lead + k:lead + k + tt, cs]
        y_ref[:, cs] = acc + cb_ref[:, cs]
    y = y_ref[...]
    mu = jnp.mean(y, axis=-1, keepdims=True)
    var = jnp.mean(jnp.square(y - mu), axis=-1, keepdims=True)
    yn = (y - mu) * lax.rsqrt(var + LN_EPS) * lg_ref[...] + lb_ref[...]
    o_ref[...] = _silu(yn).astype(o_ref.dtype)


def conformer_conv(z, conv_w, conv_b, ln_g, ln_b, *, B, T, col0, C, tt=128):
    nt = T // tt
    ca = col0 // C
    hb = tt // CONV_HALO
    row = lambda b, t: b * nt + t
    halo_row = lambda b, t: jnp.maximum(row(b, t) * hb - 1, 0)
    const = lambda shape: pl.BlockSpec(shape, lambda b, t: (0,) * len(shape))
    return pl.pallas_call(
        functools.partial(_conv_body, tt=tt),
        grid=(B, nt),
        in_specs=[pl.BlockSpec((tt, C), lambda b, t: (row(b, t), ca)),
                  pl.BlockSpec((tt, C), lambda b, t: (row(b, t), ca + 1)),
                  pl.BlockSpec((CONV_HALO, C), lambda b, t: (halo_row(b, t), ca)),
                  pl.BlockSpec((CONV_HALO, C), lambda b, t: (halo_row(b, t), ca + 1)),
                  const((CONV_KERNEL, C)), const((1, C)), const((1, C)), const((1, C))],
        out_specs=pl.BlockSpec((tt, C), lambda b, t: (row(b, t), 0)),
        out_shape=jax.ShapeDtypeStruct((B * T, C), BF16),
        scratch_shapes=[pltpu.VMEM((CONV_HALO + tt, C), F32), pltpu.VMEM((tt, C), F32)],
        compiler_params=_cparams("parallel", "parallel"),
        name="conformer_conv",
    )(z, z, z, z, conv_w, conv_b, ln_g, ln_b)


def _mem_attn_body(q_ref, kv_ref, qn_ref, kn_ref, wo_ref, x_ref, o_ref):
    dh = HEAD_DIM
    width = MEM_HEADS * dh
    scale = dh ** -0.5
    outs = []
    for h in range(MEM_HEADS):
        cs = slice(h * dh, (h + 1) * dh)
        q = (_rms(q_ref[:, cs], qn_ref[...]) * scale).astype(BF16)
        k = _rms(kv_ref[:, cs], kn_ref[...]).astype(BF16)
        v = kv_ref[:, width + h * dh:width + (h + 1) * dh].astype(BF16)
        s = lax.dot_general(q, k, (((1,), (1,)), ((), ())), preferred_element_type=F32)
        m = jnp.max(s, axis=-1, keepdims=True)
        p = jnp.exp(s - m)
        p = p / jnp.sum(p, axis=-1, keepdims=True)
        outs.append(jnp.dot(p.astype(BF16), v, preferred_element_type=F32).astype(BF16))
    o = jnp.concatenate(outs, axis=1)
    o_ref[...] = x_ref[...] + jnp.dot(o, wo_ref[...], preferred_element_type=F32)


def mem_attention(qm, kvm, mq_norm, mk_norm, w_mo, x, *, B, T, M, tt=256):
    D = x.shape[1]
    width = MEM_HEADS * HEAD_DIM
    nt = T // tt
    row = lambda b, t: b * nt + t
    const = lambda shape: pl.BlockSpec(shape, lambda b, t: (0,) * len(shape))
    return pl.pallas_call(
        _mem_attn_body,
        grid=(B, nt),
        in_specs=[pl.BlockSpec((tt, width), lambda b, t: (row(b, t), 0)),
                  pl.BlockSpec((M, 2 * width), lambda b, t: (b, 0)),
                  const((1, HEAD_DIM)), const((1, HEAD_DIM)), const((width, D)),
                  pl.BlockSpec((tt, D), lambda b, t: (row(b, t), 0))],
        out_specs=pl.BlockSpec((tt, D), lambda b, t: (row(b, t), 0)),
        out_shape=jax.ShapeDtypeStruct((B * T, D), F32),
        compiler_params=_cparams("parallel", "parallel"),
        name="mem_attention",
    )(qm, kvm, mq_norm, mk_norm, w_mo, x)


def kernel(x, mem, norm_mix, w_in, gate_b, q_norm, k_norm_cmp, k_norm_slc, k_norm_win, cmp_pos_k, cmp_pos_v, cmp_k_w1, cmp_k_w2, cmp_v_w1, cmp_v_w2, conv_w, conv_b, conv_ln_g, conv_ln_b, w_out, norm_mem_q, norm_mem_kv, w_mq, w_mk, w_mv, mq_norm, mk_norm, w_mo, norm_ffn, w_gate, w_up, w_down):
    B, T, D = x.shape
    M = mem.shape[1]
    depth = w_in.shape[0]
    G, hg = NSA_KV_HEADS, NSA_GROUP
    C = D - NSA_WIDTH
    n_gate = N_BRANCH * NSA_HEADS
    kv_all = 6 * KV_WIDTH
    assert w_in.shape[2] == NSA_WIDTH + kv_all + n_gate + 2 * C
    assert C == NSA_WIDTH, "column blocking below assumes equal NSA and conv widths"

    row1 = lambda v: v.reshape(1, -1)
    xf = x.reshape(B * T, D)
    memf = mem.reshape(B * M, D)
    u0 = NSA_WIDTH + kv_all + n_gate
    for l in range(depth):
        w = w_in[l]
        w_main = jnp.concatenate([w[:, :NSA_WIDTH], w[:, u0:], w[:, NSA_WIDTH:NSA_WIDTH + kv_all]],
                                 axis=1).astype(BF16)
        wg = w[:, NSA_WIDTH + kv_all:u0].reshape(D, G, hg * N_BRANCH)
        wg = jnp.pad(wg, ((0, 0), (0, 0), (0, LANES - hg * N_BRANCH))).reshape(D, G * LANES).astype(BF16)
        gb = jnp.pad(gate_b[l].reshape(G, hg * N_BRANCH), ((0, 0), (0, LANES - hg * N_BRANCH))).reshape(1, G * LANES)

        z, gl = norm_matmul_gate(xf, row1(norm_mix[l]), w_main, wg, tm=1024, tn=512)
        kv_col0 = NSA_WIDTH + 2 * C
        kc, vc, ks, vsT, kw, vwT = kv_prep(z, row1(k_norm_slc[l]), row1(k_norm_win[l]), B=B, T=T, col0=kv_col0)
        kcmp, vcmpT = compress_kv(kc, vc, cmp_pos_k[l], cmp_pos_v[l],
                                  cmp_k_w1[l].astype(BF16), cmp_k_w2[l].astype(BF16),
                                  cmp_v_w1[l].astype(BF16), cmp_v_w2[l].astype(BF16), row1(k_norm_cmp[l]))
        o_nsa = nsa_attention(z, gl, gb, row1(q_norm[l]), kcmp, vcmpT, ks, vsT, kw, vwT, B=B, T=T)
        o_conv = conformer_conv(z, conv_w[l], row1(conv_b[l]), row1(conv_ln_g[l]), row1(conv_ln_b[l]),
                                B=B, T=T, col0=NSA_WIDTH, C=C)
        xf = concat_matmul_residual(o_nsa, o_conv, w_out[l].astype(BF16), xf, tm=1024, tn=512)

        qm = norm_matmul(xf, row1(norm_mem_q[l]), w_mq[l].astype(BF16), tm=1024, tn=512)
        w_mkv = jnp.concatenate([w_mk[l], w_mv[l]], axis=1).astype(BF16)
        kvm = norm_matmul(memf, row1(norm_mem_kv[l]), w_mkv, tm=B * M, tn=512)
        xf = mem_attention(qm, kvm, row1(mq_norm[l]), row1(mk_norm[l]), w_mo[l].astype(BF16), xf, B=B, T=T, M=M)

        hidden = norm_swiglu(xf, row1(norm_ffn[l]), w_gate[l].astype(BF16), w_up[l].astype(BF16), tm=1024, tn=512)
        xf = matmul_residual(hidden, w_down[l].astype(BF16), xf, tm=512, tn=512)
    return xf.reshape(B, T, D)
```

```python
import functools

import numpy as np
import jax
import jax.numpy as jnp
from jax import lax
from jax.experimental import pallas as pl
from jax.experimental.pallas import tpu as pltpu

F32 = jnp.float32
BF16 = jnp.bfloat16

HEAD_DIM = 128
NSA_HEADS = 8
NSA_KV_HEADS = 2
NSA_GROUP = NSA_HEADS // NSA_KV_HEADS
NSA_WIDTH = NSA_HEADS * HEAD_DIM
KV_WIDTH = NSA_KV_HEADS * HEAD_DIM
N_BRANCH = 3
CONV_KERNEL = 31
CMP_LEN = 32
CMP_STRIDE = 16
SEL_BLK = 64
N_SEL = 16
WINDOW = 512
MEM_HEADS = 4
NEG_INF = -1e30
FORCE_BONUS = 1e3
RMS_EPS = 1e-6
LN_EPS = 1e-5

LANES = 128
VMEM_LIMIT_BYTES = 56 * 1024 * 1024

Q_TILE = 128
SEL_CHUNK = 256
WIN_CHUNK = 128
CONV_HALO = 32


def _cparams(*sem):
    return pltpu.CompilerParams(dimension_semantics=sem, vmem_limit_bytes=VMEM_LIMIT_BYTES)


def _rms(x, g, eps=RMS_EPS):
    ms = jnp.mean(x * x, axis=-1, keepdims=True)
    return x * lax.rsqrt(ms + eps) * g


def _silu(x):
    return x * jax.nn.sigmoid(x)


def _norm_to_scratch(x_ref, g_ref, xn_ref):
    @pl.when(pl.program_id(1) == 0)
    def _():
        xn_ref[...] = _rms(x_ref[...], g_ref[...]).astype(xn_ref.dtype)


def _norm_mm_body(x_ref, g_ref, w_ref, o_ref, xn_ref):
    _norm_to_scratch(x_ref, g_ref, xn_ref)
    o_ref[...] = jnp.dot(xn_ref[...], w_ref[...], preferred_element_type=F32).astype(o_ref.dtype)


def _norm_mm_gate_body(x_ref, g_ref, w_ref, wg_ref, o_ref, og_ref, xn_ref):
    _norm_to_scratch(x_ref, g_ref, xn_ref)

    @pl.when(pl.program_id(1) == 0)
    def _():
        og_ref[...] = jnp.dot(xn_ref[...], wg_ref[...], preferred_element_type=F32)

    o_ref[...] = jnp.dot(xn_ref[...], w_ref[...], preferred_element_type=F32).astype(o_ref.dtype)


def _norm_swiglu_body(x_ref, g_ref, wg_ref, wu_ref, o_ref, xn_ref):
    _norm_to_scratch(x_ref, g_ref, xn_ref)
    xn = xn_ref[...]
    a = jnp.dot(xn, wg_ref[...], preferred_element_type=F32)
    b = jnp.dot(xn, wu_ref[...], preferred_element_type=F32)
    o_ref[...] = (_silu(a) * b).astype(o_ref.dtype)


def norm_matmul(x, g, w, *, tm, tn, out_dtype=F32):
    M, K = x.shape
    N = w.shape[1]
    return pl.pallas_call(
        _norm_mm_body,
        grid=(M // tm, N // tn),
        in_specs=[pl.BlockSpec((tm, K), lambda i, j: (i, 0)),
                  pl.BlockSpec((1, K), lambda i, j: (0, 0)),
                  pl.BlockSpec((K, tn), lambda i, j: (0, j))],
        out_specs=pl.BlockSpec((tm, tn), lambda i, j: (i, j)),
        out_shape=jax.ShapeDtypeStruct((M, N), out_dtype),
        scratch_shapes=[pltpu.VMEM((tm, K), BF16)],
        compiler_params=_cparams("parallel", "arbitrary"),
        name="norm_matmul",
    )(x, g, w)


def norm_matmul_gate(x, g, w, wg, *, tm, tn):
    M, K = x.shape
    N = w.shape[1]
    NG = wg.shape[1]
    return pl.pallas_call(
        _norm_mm_gate_body,
        grid=(M // tm, N // tn),
        in_specs=[pl.BlockSpec((tm, K), lambda i, j: (i, 0)),
                  pl.BlockSpec((1, K), lambda i, j: (0, 0)),
                  pl.BlockSpec((K, tn), lambda i, j: (0, j)),
                  pl.BlockSpec((K, NG), lambda i, j: (0, 0))],
        out_specs=[pl.BlockSpec((tm, tn), lambda i, j: (i, j)),
                   pl.BlockSpec((tm, NG), lambda i, j: (i, 0))],
        out_shape=[jax.ShapeDtypeStruct((M, N), F32),
                   jax.ShapeDtypeStruct((M, NG), F32)],
        scratch_shapes=[pltpu.VMEM((tm, K), BF16)],
        compiler_params=_cparams("parallel", "arbitrary"),
        name="in_proj",
    )(x, g, w, wg)


def norm_swiglu(x, g, w_gate, w_up, *, tm, tn):
    M, K = x.shape
    N = w_gate.shape[1]
    return pl.pallas_call(
        _norm_swiglu_body,
        grid=(M // tm, N // tn),
        in_specs=[pl.BlockSpec((tm, K), lambda i, j: (i, 0)),
                  pl.BlockSpec((1, K), lambda i, j: (0, 0)),
                  pl.BlockSpec((K, tn), lambda i, j: (0, j)),
                  pl.BlockSpec((K, tn), lambda i, j: (0, j))],
        out_specs=pl.BlockSpec((tm, tn), lambda i, j: (i, j)),
        out_shape=jax.ShapeDtypeStruct((M, N), BF16),
        scratch_shapes=[pltpu.VMEM((tm, K), BF16)],
        compiler_params=_cparams("parallel", "arbitrary"),
        name="ffn_up",
    )(x, g, w_gate, w_up)


def _mm_res_body(a_ref, w_ref, r_ref, o_ref):
    o_ref[...] = r_ref[...] + jnp.dot(a_ref[...], w_ref[...], preferred_element_type=F32)


def matmul_residual(a, w, res, *, tm, tn):
    M, K = a.shape
    N = w.shape[1]
    return pl.pallas_call(
        _mm_res_body,
        grid=(M // tm, N // tn),
        in_specs=[pl.BlockSpec((tm, K), lambda i, j: (i, 0)),
                  pl.BlockSpec((K, tn), lambda i, j: (0, j)),
                  pl.BlockSpec((tm, tn), lambda i, j: (i, j))],
        out_specs=pl.BlockSpec((tm, tn), lambda i, j: (i, j)),
        out_shape=jax.ShapeDtypeStruct((M, N), F32),
        compiler_params=_cparams("parallel", "arbitrary"),
        name="matmul_residual",
    )(a, w, res)


def _mm2_res_body(a_ref, b_ref, wa_ref, wb_ref, r_ref, o_ref):
    acc = jnp.dot(a_ref[...], wa_ref[...], preferred_element_type=F32)
    acc = acc + jnp.dot(b_ref[...], wb_ref[...], preferred_element_type=F32)
    o_ref[...] = r_ref[...] + acc


def concat_matmul_residual(a, b, w, res, *, tm, tn):
    M, Ka = a.shape
    Kb = b.shape[1]
    assert Ka == Kb and w.shape[0] == Ka + Kb
    N = w.shape[1]
    return pl.pallas_call(
        _mm2_res_body,
        grid=(M // tm, N // tn),
        in_specs=[pl.BlockSpec((tm, Ka), lambda i, j: (i, 0)),
                  pl.BlockSpec((tm, Kb), lambda i, j: (i, 0)),
                  pl.BlockSpec((Ka, tn), lambda i, j: (0, j)),
                  pl.BlockSpec((Kb, tn), lambda i, j: (1, j)),
                  pl.BlockSpec((tm, tn), lambda i, j: (i, j))],
        out_specs=pl.BlockSpec((tm, tn), lambda i, j: (i, j)),
        out_shape=jax.ShapeDtypeStruct((M, N), F32),
        compiler_params=_cparams("parallel", "arbitrary"),
        name="out_proj",
    )(a, b, w, w, res)


AUG_WIDTH = LANES


def _position_lanes(pos_hi, pos_lo, onehot, n_blk, shape):
    lane = lax.broadcasted_iota(jnp.int32, shape, 1)
    extra = jnp.where(lane == n_blk, pos_hi, jnp.where(lane == n_blk + 1, pos_lo, 0.0))
    if onehot is not None:
        extra = jnp.where(lane < n_blk, onehot, extra)
    return extra.astype(BF16)


def _kv_prep_body(c_ref, s_ref, w_ref, kns_ref, knw_ref,
                  kc_o, vc_o, ks_o, vsT_o, kw_o, vwT_o, *, tt, n_blk):
    dh = HEAD_DIM
    shape = (tt, AUG_WIDTH)
    pos = pl.program_id(1) * tt + lax.broadcasted_iota(jnp.int32, shape, 0)
    lane = lax.broadcasted_iota(jnp.int32, shape, 1)
    blk = jnp.right_shift(pos, SEL_BLK.bit_length() - 1)
    pos_hi = (blk * SEL_BLK).astype(F32)
    pos_lo = (pos & (SEL_BLK - 1)).astype(F32)
    aug_sel = _position_lanes(pos_hi, pos_lo, (blk == lane).astype(F32), n_blk, shape)
    aug_win = _position_lanes(pos_hi, pos_lo, None, n_blk, shape)
    for g in range(NSA_KV_HEADS):
        k_cols = slice(g * dh, (g + 1) * dh)
        v_cols = slice(KV_WIDTH + g * dh, KV_WIDTH + (g + 1) * dh)
        kc_o[0, g] = c_ref[:, k_cols]
        vc_o[0, g] = c_ref[:, v_cols]
        ks_o[0, g, :, 0:dh] = _rms(s_ref[:, k_cols], kns_ref[...]).astype(BF16)
        ks_o[0, g, :, dh:dh + AUG_WIDTH] = aug_sel
        kw_o[0, g, :, 0:dh] = _rms(w_ref[:, k_cols], knw_ref[...]).astype(BF16)
        kw_o[0, g, :, dh:dh + AUG_WIDTH] = aug_win
        vsT = s_ref[:, v_cols].T.astype(BF16)
        for c in range(tt // SEL_CHUNK):
            vsT_o[0, g, c] = vsT[:, c * SEL_CHUNK:(c + 1) * SEL_CHUNK]
        vwT = w_ref[:, v_cols].T.astype(BF16)
        for c in range(tt // WIN_CHUNK):
            vwT_o[0, g, c] = vwT[:, c * WIN_CHUNK:(c + 1) * WIN_CHUNK]


def kv_prep(z, kn_slc, kn_win, *, B, T, col0, tt=256):
    G, dh = NSA_KV_HEADS, HEAD_DIM
    nt = T // tt
    n_blk = T // SEL_BLK
    assert n_blk + 2 <= AUG_WIDTH
    cb = col0 // (2 * KV_WIDTH)
    row = lambda b, t: b * nt + t
    kdim = dh + AUG_WIDTH
    return pl.pallas_call(
        functools.partial(_kv_prep_body, tt=tt, n_blk=n_blk),
        grid=(B, nt),
        in_specs=[pl.BlockSpec((tt, 2 * KV_WIDTH), lambda b, t: (row(b, t), cb)),
                  pl.BlockSpec((tt, 2 * KV_WIDTH), lambda b, t: (row(b, t), cb + 1)),
                  pl.BlockSpec((tt, 2 * KV_WIDTH), lambda b, t: (row(b, t), cb + 2)),
                  pl.BlockSpec((1, dh), lambda b, t: (0, 0)),
                  pl.BlockSpec((1, dh), lambda b, t: (0, 0))],
        out_specs=[pl.BlockSpec((1, G, tt, dh), lambda b, t: (b, 0, t, 0)),
                   pl.BlockSpec((1, G, tt, dh), lambda b, t: (b, 0, t, 0)),
                   pl.BlockSpec((1, G, tt, kdim), lambda b, t: (b, 0, t, 0)),
                   pl.BlockSpec((1, G, tt // SEL_CHUNK, dh, SEL_CHUNK), lambda b, t: (b, 0, t, 0, 0)),
                   pl.BlockSpec((1, G, tt, kdim), lambda b, t: (b, 0, t, 0)),
                   pl.BlockSpec((1, G, tt // WIN_CHUNK, dh, WIN_CHUNK), lambda b, t: (b, 0, t, 0, 0))],
        out_shape=[jax.ShapeDtypeStruct((B, G, T, dh), F32),
                   jax.ShapeDtypeStruct((B, G, T, dh), F32),
                   jax.ShapeDtypeStruct((B, G, T, kdim), BF16),
                   jax.ShapeDtypeStruct((B, G, T // SEL_CHUNK, dh, SEL_CHUNK), BF16),
                   jax.ShapeDtypeStruct((B, G, T, kdim), BF16),
                   jax.ShapeDtypeStruct((B, G, T // WIN_CHUNK, dh, WIN_CHUNK), BF16)],
        compiler_params=_cparams("parallel", "parallel"),
        name="kv_prep",
    )(z, z, z, kn_slc, kn_win)


def _compress_body(xk_ref, xv_ref, pk_ref, pv_ref, w1k_ref, w2k_ref, w1v_ref, w2v_ref, kn_ref,
                   kc_o, vcT_o, *, n_blk):
    dh = HEAD_DIM

    def phi(x_ref, p_ref, w1_ref, w2_ref):
        x = x_ref[0, 0]
        first = jnp.dot((x + p_ref[0:1, :]).astype(BF16), w1_ref[0], preferred_element_type=F32)
        second = jnp.dot((x + p_ref[1:2, :]).astype(BF16), w1_ref[1], preferred_element_type=F32)
        n = second.shape[0]
        h = first + pltpu.roll(second, n - 1, 0)
        return jnp.dot(_silu(h).astype(BF16), w2_ref[...], preferred_element_type=F32)

    k = _rms(phi(xk_ref, pk_ref, w1k_ref, w2k_ref), kn_ref[...])
    nc = k.shape[0]
    shape = (nc, AUG_WIDTH)
    start = (lax.broadcasted_iota(jnp.int32, shape, 0) * CMP_STRIDE).astype(F32)
    centre = jnp.full(shape, (CMP_LEN - 1) / 2.0, F32)
    kc_o[0, 0, :, 0:dh] = k.astype(BF16)
    kc_o[0, 0, :, dh:dh + AUG_WIDTH] = _position_lanes(start, centre, None, n_blk, shape)
    vcT_o[0, 0] = phi(xv_ref, pv_ref, w1v_ref, w2v_ref).T.astype(BF16)


def compress_kv(kc, vc, pos_k, pos_v, w1k, w2k, w1v, w2v, kn_cmp, *, n_blk):
    B, G, T, dh = kc.shape
    assert CMP_LEN == 2 * CMP_STRIDE
    nc = T // CMP_STRIDE
    half = CMP_STRIDE * dh
    xk = kc.reshape(B, G, nc, half)
    xv = vc.reshape(B, G, nc, half)
    x_spec = pl.BlockSpec((1, 1, nc, half), lambda b, g: (b, g, 0, 0))
    full = lambda shape: pl.BlockSpec(shape, lambda b, g: (0,) * len(shape))
    return pl.pallas_call(
        functools.partial(_compress_body, n_blk=n_blk),
        grid=(B, G),
        in_specs=[x_spec, x_spec, full((2, half)), full((2, half)),
                  full((2, half, dh)), full((dh, dh)), full((2, half, dh)), full((dh, dh)),
                  full((1, dh))],
        out_specs=[pl.BlockSpec((1, 1, nc, dh + AUG_WIDTH), lambda b, g: (b, g, 0, 0)),
                   pl.BlockSpec((1, 1, dh, nc), lambda b, g: (b, g, 0, 0))],
        out_shape=[jax.ShapeDtypeStruct((B, G, nc, dh + AUG_WIDTH), BF16),
                   jax.ShapeDtypeStruct((B, G, dh, nc), BF16)],
        compiler_params=_cparams("parallel", "parallel"),
        name="compress_kv",
    )(xk, xv, pos_k.reshape(2, half), pos_v.reshape(2, half),
      w1k.reshape(2, half, dh), w2k, w1v.reshape(2, half, dh), w2v, kn_cmp)


def _nsa_body(q_ref, gl_ref, gb_ref, qn_ref, lanes_ref, kc_ref, vcT_ref,
              ks_ref, vsT_ref, kw_ref, vwT_ref, ovT_ref, o_ref, score_ref, gate_ref):
    dh, hg, tq = HEAD_DIM, NSA_GROUP, Q_TILE
    cols = hg * tq
    qi = pl.program_id(2)
    t0 = qi * tq
    contract_last = (((1,), (1,)), ((), ()))
    n_blk = ovT_ref.shape[0]

    scale = dh ** -0.5
    q_heads = [(_rms(q_ref[:, h * dh:(h + 1) * dh], qn_ref[...]) * scale).astype(BF16)
               for h in range(hg)]

    def stack_queries(aug_heads):
        return jnp.concatenate([jnp.concatenate([q_heads[h], aug_heads[h]], axis=1)
                                for h in range(hg)], axis=0)

    def softmax_part(s, m, l):
        m_new = jnp.maximum(m, jnp.max(s, axis=0, keepdims=True))
        alpha = jnp.exp(m - m_new)
        p = jnp.exp(s - m_new)
        l = alpha * l + jnp.sum(p, axis=0, keepdims=True)
        return m_new, l, alpha, p.astype(BF16)

    def softmax_step(s, vT, carry):
        m, l, acc = carry
        m, l, alpha, p = softmax_part(s, m, l)
        return m, l, alpha * acc + jnp.dot(vT, p, preferred_element_type=F32)

    init = (jnp.full((1, cols), NEG_INF, F32), jnp.zeros((1, cols), F32),
            jnp.zeros((dh, cols), F32))

    nc = kc_ref.shape[2]
    q_cmp = stack_queries([jnp.broadcast_to(lanes_ref[0, hg + h:hg + h + 1, :], (tq, AUG_WIDTH)).astype(BF16)
                           for h in range(hg)])
    s = lax.dot_general(kc_ref[0, 0], q_cmp, contract_last, preferred_element_type=F32)
    c_idx = lax.broadcasted_iota(jnp.int32, (nc, cols), 0)
    t_c = t0 + (lax.broadcasted_iota(jnp.int32, (nc, cols), 1) & (tq - 1))
    mask = (c_idx * CMP_STRIDE + (CMP_LEN - 1)) <= t_c
    s = jnp.where(mask, s, NEG_INF)
    m_c = jnp.max(s, axis=0, keepdims=True)
    p_c = jnp.where(mask, jnp.exp(s - m_c), 0.0)
    l_c = jnp.sum(p_c, axis=0, keepdims=True)
    p_c = p_c * jnp.where(l_c > 0.0, 1.0 / l_c, 0.0)
    o_cmp = jnp.dot(vcT_ref[0, 0], p_c.astype(BF16), preferred_element_type=F32)

    p_sum = p_c[:, 0:tq]
    for h in range(1, hg):
        p_sum = p_sum + p_c[:, h * tq:(h + 1) * tq]
    hi = p_sum.astype(BF16)
    rem = p_sum - hi.astype(F32)
    mid = rem.astype(BF16)
    lo = (rem - mid.astype(F32)).astype(BF16)
    ovT = ovT_ref[...]
    imp = (jnp.dot(ovT, hi, preferred_element_type=F32)
           + jnp.dot(ovT, mid, preferred_element_type=F32)
           + jnp.dot(ovT, lo, preferred_element_type=F32))
    blk = lax.broadcasted_iota(jnp.int32, (n_blk, tq), 0)
    t_b = t0 + lax.broadcasted_iota(jnp.int32, (n_blk, tq), 1)
    cur = jnp.right_shift(t_b, SEL_BLK.bit_length() - 1)
    valid = blk * SEL_BLK <= t_b
    forced = (blk == 0) | (blk == cur) | (blk == cur - 1)
    score = jnp.where(valid, imp + FORCE_BONUS * forced.astype(F32), NEG_INF)
    score_ref[...] = score
    rank = jnp.zeros((n_blk, tq), F32)
    for j in range(n_blk):
        sj = score_ref[j:j + 1, :]
        beats = (sj > score) | ((sj == score) & (blk > j))
        rank = rank + beats.astype(F32)
    block_bias = jnp.where(rank < float(min(N_SEL, n_blk)), 0.0, NEG_INF)

    ones_rows = (lax.broadcasted_iota(jnp.int32, (8, tq), 0) < 2).astype(F32)
    extra_t = jnp.concatenate([block_bias, ones_rows,
                               jnp.zeros((AUG_WIDTH - n_blk - 8, tq), F32)], axis=0).T
    q_all = stack_queries([(extra_t * lanes_ref[0, h:h + 1, :]).astype(BF16) for h in range(hg)])

    ck = SEL_CHUNK
    key_minus_query = (lax.broadcasted_iota(jnp.int32, (ck, cols), 0)
                       - (lax.broadcasted_iota(jnp.int32, (ck, cols), 1) & (tq - 1)))

    def sel_scores(c):
        k = ks_ref[0, 0, pl.ds(pl.multiple_of(c * ck, ck), ck), :]
        return lax.dot_general(k, q_all, contract_last, preferred_element_type=F32)

    def sel_step(c, carry):
        m, l, acc, s_cur, p_prev, alpha_prev = carry
        s_next = sel_scores(c + 1)
        acc = alpha_prev * acc + jnp.dot(vsT_ref[0, 0, jnp.maximum(c - 1, 0)], p_prev,
                                         preferred_element_type=F32)
        m, l, alpha, p = softmax_part(s_cur, m, l)
        return m, l, acc, s_next, p, alpha

    diag = t0 // ck
    m, l, acc, s_cur, p_prev, alpha_prev = lax.fori_loop(
        0, diag, sel_step,
        init + (sel_scores(0), jnp.zeros((ck, cols), BF16), jnp.ones((1, cols), F32)))
    acc = alpha_prev * acc + jnp.dot(vsT_ref[0, 0, jnp.maximum(diag - 1, 0)], p_prev,
                                     preferred_element_type=F32)
    s = jnp.where(key_minus_query <= t0 - diag * ck, s_cur, NEG_INF)
    _, l_s, acc_s = softmax_step(s, vsT_ref[0, 0, diag], (m, l, acc))
    o_slc = acc_s * (1.0 / l_s)

    cw = WIN_CHUNK
    assert cw == tq and WINDOW % cw == 0
    n_back = WINDOW // cw
    kmq = key_minus_query[0:cw, :]

    def win_scores(c):
        k = kw_ref[0, 0, pl.ds(pl.multiple_of(c * cw, cw), cw), :]
        return lax.dot_general(k, q_all, contract_last, preferred_element_type=F32)

    carry = softmax_step(jnp.where(kmq <= 0, win_scores(qi), NEG_INF), vwT_ref[0, 0, qi], init)
    for j in range(1, n_back):
        c = jnp.maximum(qi - j, 0)
        s = win_scores(c) + jnp.where(qi >= j, 0.0, NEG_INF)
        carry = softmax_step(s, vwT_ref[0, 0, c], carry)
    edge = jnp.maximum(qi - n_back, 0)
    threshold = jnp.where(qi >= n_back, 0, cw)
    s = jnp.where(kmq > threshold, win_scores(edge), NEG_INF)
    _, l_w, acc_w = softmax_step(s, vwT_ref[0, 0, edge], carry)
    o_win = acc_w * (1.0 / l_w)

    gate_ref[...] = jax.nn.sigmoid(gl_ref[...] + gb_ref[...]).T
    for h in range(hg):
        cs = slice(h * tq, (h + 1) * tq)
        r = h * N_BRANCH
        o_h = (gate_ref[r:r + 1, :] * o_cmp[:, cs] + gate_ref[r + 1:r + 2, :] * o_slc[:, cs]
               + gate_ref[r + 2:r + 3, :] * o_win[:, cs])
        o_ref[:, h * dh:(h + 1) * dh] = o_h.T.astype(o_ref.dtype)


def nsa_attention(z, gl, gate_b, q_norm, kcmp, vcmpT, ks, vsT, kw, vwT, *, B, T):
    G, hg, dh, tq = NSA_KV_HEADS, NSA_GROUP, HEAD_DIM, Q_TILE
    nq = T // tq
    nc = kcmp.shape[2]
    n_blk = T // SEL_BLK
    kdim = dh + AUG_WIDTH
    assert SEL_CHUNK % SEL_BLK == 0 and T % SEL_CHUNK == 0 and gl.shape[1] == G * LANES
    assert SEL_CHUNK % tq == 0 and n_blk % 8 == 0 and n_blk + 8 <= AUG_WIDTH

    slopes = np.exp2(-8.0 * (np.arange(NSA_HEADS, dtype=np.float64) + 1.0) / NSA_HEADS).reshape(G, hg)
    lanes = np.zeros((G, 2 * hg, AUG_WIDTH), np.float32)
    lanes[:, :hg, :n_blk] = 1.0
    lanes[:, :hg, n_blk:n_blk + 2] = slopes[:, :, None]
    lanes[:, hg:, n_blk:n_blk + 2] = slopes[:, :, None]
    cs = np.arange(nc)[:, None] * CMP_STRIDE
    bs = np.arange(n_blk)[None, :] * SEL_BLK
    overlap = np.clip(np.minimum(cs + CMP_LEN, bs + SEL_BLK) - np.maximum(cs, bs), 0, None) / CMP_LEN
    overlap[(T - CMP_LEN) // CMP_STRIDE + 1:] = 0.0
    ovT = jnp.asarray(overlap.T, dtype=BF16)

    row = lambda b, qi: b * nq + qi
    per_bg = lambda shape: pl.BlockSpec((1, 1) + shape, lambda b, g, qi: (b, g) + (0,) * len(shape))
    const = lambda shape: pl.BlockSpec(shape, lambda b, g, qi: (0,) * len(shape))
    return pl.pallas_call(
        _nsa_body,
        grid=(B, G, nq),
        in_specs=[pl.BlockSpec((tq, hg * dh), lambda b, g, qi: (row(b, qi), g)),
                  pl.BlockSpec((tq, LANES), lambda b, g, qi: (row(b, qi), g)),
                  pl.BlockSpec((1, LANES), lambda b, g, qi: (0, g)),
                  const((1, dh)),
                  pl.BlockSpec((1, 2 * hg, AUG_WIDTH), lambda b, g, qi: (g, 0, 0)),
                  per_bg((nc, kdim)), per_bg((dh, nc)),
                  per_bg((T, kdim)), per_bg((T // SEL_CHUNK, dh, SEL_CHUNK)),
                  per_bg((T, kdim)), per_bg((T // WIN_CHUNK, dh, WIN_CHUNK)),
                  const((n_blk, nc))],
        out_specs=pl.BlockSpec((tq, hg * dh), lambda b, g, qi: (row(b, qi), g)),
        out_shape=jax.ShapeDtypeStruct((B * T, NSA_WIDTH), BF16),
        scratch_shapes=[pltpu.VMEM((n_blk, tq), F32), pltpu.VMEM((LANES, tq), F32)],
        compiler_params=_cparams("parallel", "parallel", "arbitrary"),
        name="nsa_attention",
    )(z, gl, gate_b, q_norm, jnp.asarray(lanes), kcmp, vcmpT, ks, vsT, kw, vwT, ovT)


def _conv_body(a_ref, b_ref, ah_ref, bh_ref, w_ref, cb_ref, lg_ref, lb_ref, o_ref, h_ref, y_ref, *, tt):
    halo = ah_ref[...] * jax.nn.sigmoid(bh_ref[...])
    h_ref[0:CONV_HALO, :] = jnp.where(pl.program_id(1) == 0, 0.0, halo)
    h_ref[CONV_HALO:CONV_HALO + tt, :] = a_ref[...] * jax.nn.sigmoid(b_ref[...])
    C = a_ref.shape[1]
    lead = CONV_HALO - (CONV_KERNEL - 1)
    for s in range(C // LANES):
        cs = slice(s * LANES, (s + 1) * LANES)
        acc = jnp.zeros((tt, LANES), F32)
        for k in range(CONV_KERNEL):
            acc = acc + w_ref[k:k + 1, cs] * h_ref[lead + k:lead + k + tt, cs]
        y_ref[:, cs] = acc + cb_ref[:, cs]
    y = y_ref[...]
    mu = jnp.mean(y, axis=-1, keepdims=True)
    var = jnp.mean(jnp.square(y - mu), axis=-1, keepdims=True)
    yn = (y - mu) * lax.rsqrt(var + LN_EPS) * lg_ref[...] + lb_ref[...]
    o_ref[...] = _silu(yn).astype(o_ref.dtype)


def conformer_conv(z, conv_w, conv_b, ln_g, ln_b, *, B, T, col0, C, tt=128):
    nt = T // tt
    ca = col0 // C
    hb = tt // CONV_HALO
    row = lambda b, t: b * nt + t
    halo_row = lambda b, t: jnp.maximum(row(b, t) * hb - 1, 0)
    const = lambda shape: pl.BlockSpec(shape, lambda b, t: (0,) * len(shape))
    return pl.pallas_call(
        functools.partial(_conv_body, tt=tt),
        grid=(B, nt),
        in_specs=[pl.BlockSpec((tt, C), lambda b, t: (row(b, t), ca)),
                  pl.BlockSpec((tt, C), lambda b, t: (row(b, t), ca + 1)),
                  pl.BlockSpec((CONV_HALO, C), lambda b, t: (halo_row(b, t), ca)),
                  pl.BlockSpec((CONV_HALO, C), lambda b, t: (halo_row(b, t), ca + 1)),
                  const((CONV_KERNEL, C)), const((1, C)), const((1, C)), const((1, C))],
        out_specs=pl.BlockSpec((tt, C), lambda b, t: (row(b, t), 0)),
        out_shape=jax.ShapeDtypeStruct((B * T, C), BF16),
        scratch_shapes=[pltpu.VMEM((CONV_HALO + tt, C), F32), pltpu.VMEM((tt, C), F32)],
        compiler_params=_cparams("parallel", "parallel"),
        name="conformer_conv",
    )(z, z, z, z, conv_w, conv_b, ln_g, ln_b)


def _mem_attn_body(q_ref, kv_ref, qn_ref, kn_ref, wo_ref, x_ref, o_ref):
    dh = HEAD_DIM
    width = MEM_HEADS * dh
    scale = dh ** -0.5
    outs = []
    for h in range(MEM_HEADS):
        cs = slice(h * dh, (h + 1) * dh)
        q = (_rms(q_ref[:, cs], qn_ref[...]) * scale).astype(BF16)
        k = _rms(kv_ref[:, cs], kn_ref[...]).astype(BF16)
        v = kv_ref[:, width + h * dh:width + (h + 1) * dh].astype(BF16)
        s = lax.dot_general(q, k, (((1,), (1,)), ((), ())), preferred_element_type=F32)
        m = jnp.max(s, axis=-1, keepdims=True)
        p = jnp.exp(s - m)
        p = p / jnp.sum(p, axis=-1, keepdims=True)
        outs.append(jnp.dot(p.astype(BF16), v, preferred_element_type=F32).astype(BF16))
    o = jnp.concatenate(outs, axis=1)
    o_ref[...] = x_ref[...] + jnp.dot(o, wo_ref[...], preferred_element_type=F32)


def mem_attention(qm, kvm, mq_norm, mk_norm, w_mo, x, *, B, T, M, tt=256):
    D = x.shape[1]
    width = MEM_HEADS * HEAD_DIM
    nt = T // tt
    row = lambda b, t: b * nt + t
    const = lambda shape: pl.BlockSpec(shape, lambda b, t: (0,) * len(shape))
    return pl.pallas_call(
        _mem_attn_body,
        grid=(B, nt),
        in_specs=[pl.BlockSpec((tt, width), lambda b, t: (row(b, t), 0)),
                  pl.BlockSpec((M, 2 * width), lambda b, t: (b, 0)),
                  const((1, HEAD_DIM)), const((1, HEAD_DIM)), const((width, D)),
                  pl.BlockSpec((tt, D), lambda b, t: (row(b, t), 0))],
        out_specs=pl.BlockSpec((tt, D), lambda b, t: (row(b, t), 0)),
        out_shape=jax.ShapeDtypeStruct((B * T, D), F32),
        compiler_params=_cparams("parallel", "parallel"),
        name="mem_attention",
    )(qm, kvm, mq_norm, mk_norm, w_mo, x)


def kernel(x, mem, norm_mix, w_in, gate_b, q_norm, k_norm_cmp, k_norm_slc, k_norm_win, cmp_pos_k, cmp_pos_v, cmp_k_w1, cmp_k_w2, cmp_v_w1, cmp_v_w2, conv_w, conv_b, conv_ln_g, conv_ln_b, w_out, norm_mem_q, norm_mem_kv, w_mq, w_mk, w_mv, mq_norm, mk_norm, w_mo, norm_ffn, w_gate, w_up, w_down):
    B, T, D = x.shape
    M = mem.shape[1]
    depth = w_in.shape[0]
    G, hg = NSA_KV_HEADS, NSA_GROUP
    C = D - NSA_WIDTH
    n_gate = N_BRANCH * NSA_HEADS
    kv_all = 6 * KV_WIDTH
    assert w_in.shape[2] == NSA_WIDTH + kv_all + n_gate + 2 * C
    assert C == NSA_WIDTH, "column blocking below assumes equal NSA and conv widths"

    row1 = lambda v: v.reshape(1, -1)
    xf = x.reshape(B * T, D)
    memf = mem.reshape(B * M, D)
    u0 = NSA_WIDTH + kv_all + n_gate
    for l in range(depth):
        w = w_in[l]
        w_main = jnp.concatenate([w[:, :NSA_WIDTH], w[:, u0:], w[:, NSA_WIDTH:NSA_WIDTH + kv_all]],
                                 axis=1).astype(BF16)
        wg = w[:, NSA_WIDTH + kv_all:u0].reshape(D, G, hg * N_BRANCH)
        wg = jnp.pad(wg, ((0, 0), (0, 0), (0, LANES - hg * N_BRANCH))).reshape(D, G * LANES).astype(BF16)
        gb = jnp.pad(gate_b[l].reshape(G, hg * N_BRANCH), ((0, 0), (0, LANES - hg * N_BRANCH))).reshape(1, G * LANES)

        z, gl = norm_matmul_gate(xf, row1(norm_mix[l]), w_main, wg, tm=1024, tn=512)
        kv_col0 = NSA_WIDTH + 2 * C
        kc, vc, ks, vsT, kw, vwT = kv_prep(z, row1(k_norm_slc[l]), row1(k_norm_win[l]), B=B, T=T, col0=kv_col0)
        kcmp, vcmpT = compress_kv(kc, vc, cmp_pos_k[l], cmp_pos_v[l],
                                  cmp_k_w1[l].astype(BF16), cmp_k_w2[l].astype(BF16),
                                  cmp_v_w1[l].astype(BF16), cmp_v_w2[l].astype(BF16), row1(k_norm_cmp[l]),
                                  n_blk=T // SEL_BLK)
        o_nsa = nsa_attention(z, gl, gb, row1(q_norm[l]), kcmp, vcmpT, ks, vsT, kw, vwT, B=B, T=T)
        o_conv = conformer_conv(z, conv_w[l], row1(conv_b[l]), row1(conv_ln_g[l]), row1(conv_ln_b[l]),
                                B=B, T=T, col0=NSA_WIDTH, C=C)
        xf = concat_matmul_residual(o_nsa, o_conv, w_out[l].astype(BF16), xf, tm=1024, tn=512)

        qm = norm_matmul(xf, row1(norm_mem_q[l]), w_mq[l].astype(BF16), tm=1024, tn=512)
        w_mkv = jnp.concatenate([w_mk[l], w_mv[l]], axis=1).astype(BF16)
        kvm = norm_matmul(memf, row1(norm_mem_kv[l]), w_mkv, tm=B * M, tn=512)
        xf = mem_attention(qm, kvm, row1(mq_norm[l]), row1(mk_norm[l]), w_mo[l].astype(BF16), xf, B=B, T=T, M=M)

        hidden = norm_swiglu(xf, row1(norm_ffn[l]), w_gate[l].astype(BF16), w_up[l].astype(BF16), tm=1024, tn=512)
        xf = matmul_residual(hidden, w_down[l].astype(BF16), xf, tm=512, tn=512)
    return xf.reshape(B, T, D)
```

```python
import functools

import numpy as np
import jax
import jax.numpy as jnp
from jax import lax
from jax.experimental import pallas as pl
from jax.experimental.pallas import tpu as pltpu

F32 = jnp.float32
BF16 = jnp.bfloat16

HEAD_DIM = 128
NSA_HEADS = 8
NSA_KV_HEADS = 2
NSA_GROUP = NSA_HEADS // NSA_KV_HEADS
NSA_WIDTH = NSA_HEADS * HEAD_DIM
KV_WIDTH = NSA_KV_HEADS * HEAD_DIM
N_BRANCH = 3
CONV_KERNEL = 31
CMP_LEN = 32
CMP_STRIDE = 16
SEL_BLK = 64
N_SEL = 16
WINDOW = 512
MEM_HEADS = 4
NEG_INF = -1e30
FORCE_BONUS = 1e3
RMS_EPS = 1e-6
LN_EPS = 1e-5

LANES = 128
VMEM_LIMIT_BYTES = 56 * 1024 * 1024

Q_TILE = 256
SEL_CHUNK = 256
WIN_CHUNK = 256
NSA_COLS_PER_DOT = 512
CONV_HALO = 32


def _cparams(*sem):
    return pltpu.CompilerParams(dimension_semantics=sem, vmem_limit_bytes=VMEM_LIMIT_BYTES)


def _rms(x, g, eps=RMS_EPS):
    ms = jnp.mean(x * x, axis=-1, keepdims=True)
    return x * lax.rsqrt(ms + eps) * g


def _silu(x):
    return x * jax.nn.sigmoid(x)


def _norm_to_scratch(x_ref, g_ref, xn_ref):
    @pl.when(pl.program_id(1) == 0)
    def _():
        xn_ref[...] = _rms(x_ref[...], g_ref[...]).astype(xn_ref.dtype)


def _norm_mm_body(x_ref, g_ref, w_ref, o_ref, xn_ref):
    _norm_to_scratch(x_ref, g_ref, xn_ref)
    o_ref[...] = jnp.dot(xn_ref[...], w_ref[...], preferred_element_type=F32).astype(o_ref.dtype)


def _norm_mm_gate_body(x_ref, g_ref, w_ref, wg_ref, o_ref, og_ref, xn_ref):
    _norm_to_scratch(x_ref, g_ref, xn_ref)

    @pl.when(pl.program_id(1) == 0)
    def _():
        og_ref[...] = jnp.dot(xn_ref[...], wg_ref[...], preferred_element_type=F32)

    o_ref[...] = jnp.dot(xn_ref[...], w_ref[...], preferred_element_type=F32).astype(o_ref.dtype)


def _norm_swiglu_body(x_ref, g_ref, wg_ref, wu_ref, o_ref, xn_ref):
    _norm_to_scratch(x_ref, g_ref, xn_ref)
    xn = xn_ref[...]
    a = jnp.dot(xn, wg_ref[...], preferred_element_type=F32)
    b = jnp.dot(xn, wu_ref[...], preferred_element_type=F32)
    o_ref[...] = (_silu(a) * b).astype(o_ref.dtype)


def norm_matmul(x, g, w, *, tm, tn, out_dtype=F32):
    M, K = x.shape
    N = w.shape[1]
    return pl.pallas_call(
        _norm_mm_body,
        grid=(M // tm, N // tn),
        in_specs=[pl.BlockSpec((tm, K), lambda i, j: (i, 0)),
                  pl.BlockSpec((1, K), lambda i, j: (0, 0)),
                  pl.BlockSpec((K, tn), lambda i, j: (0, j))],
        out_specs=pl.BlockSpec((tm, tn), lambda i, j: (i, j)),
        out_shape=jax.ShapeDtypeStruct((M, N), out_dtype),
        scratch_shapes=[pltpu.VMEM((tm, K), BF16)],
        compiler_params=_cparams("parallel", "arbitrary"),
        name="norm_matmul",
    )(x, g, w)


def norm_matmul_gate(x, g, w, wg, *, tm, tn):
    M, K = x.shape
    N = w.shape[1]
    NG = wg.shape[1]
    return pl.pallas_call(
        _norm_mm_gate_body,
        grid=(M // tm, N // tn),
        in_specs=[pl.BlockSpec((tm, K), lambda i, j: (i, 0)),
                  pl.BlockSpec((1, K), lambda i, j: (0, 0)),
                  pl.BlockSpec((K, tn), lambda i, j: (0, j)),
                  pl.BlockSpec((K, NG), lambda i, j: (0, 0))],
        out_specs=[pl.BlockSpec((tm, tn), lambda i, j: (i, j)),
                   pl.BlockSpec((tm, NG), lambda i, j: (i, 0))],
        out_shape=[jax.ShapeDtypeStruct((M, N), F32),
                   jax.ShapeDtypeStruct((M, NG), F32)],
        scratch_shapes=[pltpu.VMEM((tm, K), BF16)],
        compiler_params=_cparams("parallel", "arbitrary"),
        name="in_proj",
    )(x, g, w, wg)


def norm_swiglu(x, g, w_gate, w_up, *, tm, tn):
    M, K = x.shape
    N = w_gate.shape[1]
    return pl.pallas_call(
        _norm_swiglu_body,
        grid=(M // tm, N // tn),
        in_specs=[pl.BlockSpec((tm, K), lambda i, j: (i, 0)),
                  pl.BlockSpec((1, K), lambda i, j: (0, 0)),
                  pl.BlockSpec((K, tn), lambda i, j: (0, j)),
                  pl.BlockSpec((K, tn), lambda i, j: (0, j))],
        out_specs=pl.BlockSpec((tm, tn), lambda i, j: (i, j)),
        out_shape=jax.ShapeDtypeStruct((M, N), BF16),
        scratch_shapes=[pltpu.VMEM((tm, K), BF16)],
        compiler_params=_cparams("parallel", "arbitrary"),
        name="ffn_up",
    )(x, g, w_gate, w_up)


def _mm_res_body(a_ref, w_ref, r_ref, o_ref):
    o_ref[...] = r_ref[...] + jnp.dot(a_ref[...], w_ref[...], preferred_element_type=F32)


def matmul_residual(a, w, res, *, tm, tn):
    M, K = a.shape
    N = w.shape[1]
    return pl.pallas_call(
        _mm_res_body,
        grid=(M // tm, N // tn),
        in_specs=[pl.BlockSpec((tm, K), lambda i, j: (i, 0)),
                  pl.BlockSpec((K, tn), lambda i, j: (0, j)),
                  pl.BlockSpec((tm, tn), lambda i, j: (i, j))],
        out_specs=pl.BlockSpec((tm, tn), lambda i, j: (i, j)),
        out_shape=jax.ShapeDtypeStruct((M, N), F32),
        compiler_params=_cparams("parallel", "arbitrary"),
        name="matmul_residual",
    )(a, w, res)


def _mm2_res_body(a_ref, b_ref, wa_ref, wb_ref, r_ref, o_ref):
    acc = jnp.dot(a_ref[...], wa_ref[...], preferred_element_type=F32)
    acc = acc + jnp.dot(b_ref[...], wb_ref[...], preferred_element_type=F32)
    o_ref[...] = r_ref[...] + acc


def concat_matmul_residual(a, b, w, res, *, tm, tn):
    M, Ka = a.shape
    Kb = b.shape[1]
    assert Ka == Kb and w.shape[0] == Ka + Kb
    N = w.shape[1]
    return pl.pallas_call(
        _mm2_res_body,
        grid=(M // tm, N // tn),
        in_specs=[pl.BlockSpec((tm, Ka), lambda i, j: (i, 0)),
                  pl.BlockSpec((tm, Kb), lambda i, j: (i, 0)),
                  pl.BlockSpec((Ka, tn), lambda i, j: (0, j)),
                  pl.BlockSpec((Kb, tn), lambda i, j: (1, j)),
                  pl.BlockSpec((tm, tn), lambda i, j: (i, j))],
        out_specs=pl.BlockSpec((tm, tn), lambda i, j: (i, j)),
        out_shape=jax.ShapeDtypeStruct((M, N), F32),
        compiler_params=_cparams("parallel", "arbitrary"),
        name="out_proj",
    )(a, b, w, w, res)


AUG_WIDTH = LANES


def _position_lanes(pos_hi, pos_lo, onehot, n_blk, shape):
    lane = lax.broadcasted_iota(jnp.int32, shape, 1)
    extra = jnp.where(lane == n_blk, pos_hi, jnp.where(lane == n_blk + 1, pos_lo, 0.0))
    if onehot is not None:
        extra = jnp.where(lane < n_blk, onehot, extra)
    return extra.astype(BF16)


def _kv_prep_body(c_ref, s_ref, w_ref, kns_ref, knw_ref,
                  kc_o, vc_o, ks_o, vsT_o, kw_o, vwT_o, *, tt, n_blk):
    dh = HEAD_DIM
    shape = (tt, AUG_WIDTH)
    pos = pl.program_id(1) * tt + lax.broadcasted_iota(jnp.int32, shape, 0)
    lane = lax.broadcasted_iota(jnp.int32, shape, 1)
    blk = jnp.right_shift(pos, SEL_BLK.bit_length() - 1)
    pos_hi = (blk * SEL_BLK).astype(F32)
    pos_lo = (pos & (SEL_BLK - 1)).astype(F32)
    aug_sel = _position_lanes(pos_hi, pos_lo, (blk == lane).astype(F32), n_blk, shape)
    aug_win = _position_lanes(pos_hi, pos_lo, None, n_blk, shape)
    for g in range(NSA_KV_HEADS):
        k_cols = slice(g * dh, (g + 1) * dh)
        v_cols = slice(KV_WIDTH + g * dh, KV_WIDTH + (g + 1) * dh)
        kc_o[0, g] = c_ref[:, k_cols]
        vc_o[0, g] = c_ref[:, v_cols]
        ks_o[0, g, :, 0:dh] = _rms(s_ref[:, k_cols], kns_ref[...]).astype(BF16)
        ks_o[0, g, :, dh:dh + AUG_WIDTH] = aug_sel
        kw_o[0, g, :, 0:dh] = _rms(w_ref[:, k_cols], knw_ref[...]).astype(BF16)
        kw_o[0, g, :, dh:dh + AUG_WIDTH] = aug_win
        vsT = s_ref[:, v_cols].T.astype(BF16)
        for c in range(tt // SEL_CHUNK):
            vsT_o[0, g, c] = vsT[:, c * SEL_CHUNK:(c + 1) * SEL_CHUNK]
        vwT = w_ref[:, v_cols].T.astype(BF16)
        for c in range(tt // WIN_CHUNK):
            vwT_o[0, g, c] = vwT[:, c * WIN_CHUNK:(c + 1) * WIN_CHUNK]


def kv_prep(z, kn_slc, kn_win, *, B, T, col0, tt=256):
    G, dh = NSA_KV_HEADS, HEAD_DIM
    nt = T // tt
    n_blk = T // SEL_BLK
    assert n_blk + 2 <= AUG_WIDTH
    cb = col0 // (2 * KV_WIDTH)
    row = lambda b, t: b * nt + t
    kdim = dh + AUG_WIDTH
    return pl.pallas_call(
        functools.partial(_kv_prep_body, tt=tt, n_blk=n_blk),
        grid=(B, nt),
        in_specs=[pl.BlockSpec((tt, 2 * KV_WIDTH), lambda b, t: (row(b, t), cb)),
                  pl.BlockSpec((tt, 2 * KV_WIDTH), lambda b, t: (row(b, t), cb + 1)),
                  pl.BlockSpec((tt, 2 * KV_WIDTH), lambda b, t: (row(b, t), cb + 2)),
                  pl.BlockSpec((1, dh), lambda b, t: (0, 0)),
                  pl.BlockSpec((1, dh), lambda b, t: (0, 0))],
        out_specs=[pl.BlockSpec((1, G, tt, dh), lambda b, t: (b, 0, t, 0)),
                   pl.BlockSpec((1, G, tt, dh), lambda b, t: (b, 0, t, 0)),
                   pl.BlockSpec((1, G, tt, kdim), lambda b, t: (b, 0, t, 0)),
                   pl.BlockSpec((1, G, tt // SEL_CHUNK, dh, SEL_CHUNK), lambda b, t: (b, 0, t, 0, 0)),
                   pl.BlockSpec((1, G, tt, kdim), lambda b, t: (b, 0, t, 0)),
                   pl.BlockSpec((1, G, tt // WIN_CHUNK, dh, WIN_CHUNK), lambda b, t: (b, 0, t, 0, 0))],
        out_shape=[jax.ShapeDtypeStruct((B, G, T, dh), F32),
                   jax.ShapeDtypeStruct((B, G, T, dh), F32),
                   jax.ShapeDtypeStruct((B, G, T, kdim), BF16),
                   jax.ShapeDtypeStruct((B, G, T // SEL_CHUNK, dh, SEL_CHUNK), BF16),
                   jax.ShapeDtypeStruct((B, G, T, kdim), BF16),
                   jax.ShapeDtypeStruct((B, G, T // WIN_CHUNK, dh, WIN_CHUNK), BF16)],
        compiler_params=_cparams("parallel", "parallel"),
        name="kv_prep",
    )(z, z, z, kn_slc, kn_win)


def _compress_body(xk_ref, xv_ref, pk_ref, pv_ref, w1k_ref, w2k_ref, w1v_ref, w2v_ref, kn_ref,
                   kc_o, vcT_o, *, n_blk):
    dh = HEAD_DIM

    def phi(x_ref, p_ref, w1_ref, w2_ref):
        x = x_ref[0, 0]
        first = jnp.dot((x + p_ref[0:1, :]).astype(BF16), w1_ref[0], preferred_element_type=F32)
        second = jnp.dot((x + p_ref[1:2, :]).astype(BF16), w1_ref[1], preferred_element_type=F32)
        n = second.shape[0]
        h = first + pltpu.roll(second, n - 1, 0)
        return jnp.dot(_silu(h).astype(BF16), w2_ref[...], preferred_element_type=F32)

    k = _rms(phi(xk_ref, pk_ref, w1k_ref, w2k_ref), kn_ref[...])
    nc = k.shape[0]
    shape = (nc, AUG_WIDTH)
    start = (lax.broadcasted_iota(jnp.int32, shape, 0) * CMP_STRIDE).astype(F32)
    centre = jnp.full(shape, (CMP_LEN - 1) / 2.0, F32)
    kc_o[0, 0, :, 0:dh] = k.astype(BF16)
    kc_o[0, 0, :, dh:dh + AUG_WIDTH] = _position_lanes(start, centre, None, n_blk, shape)
    vcT_o[0, 0] = phi(xv_ref, pv_ref, w1v_ref, w2v_ref).T.astype(BF16)


def compress_kv(kc, vc, pos_k, pos_v, w1k, w2k, w1v, w2v, kn_cmp, *, n_blk):
    B, G, T, dh = kc.shape
    assert CMP_LEN == 2 * CMP_STRIDE
    nc = T // CMP_STRIDE
    half = CMP_STRIDE * dh
    xk = kc.reshape(B, G, nc, half)
    xv = vc.reshape(B, G, nc, half)
    x_spec = pl.BlockSpec((1, 1, nc, half), lambda b, g: (b, g, 0, 0))
    full = lambda shape: pl.BlockSpec(shape, lambda b, g: (0,) * len(shape))
    return pl.pallas_call(
        functools.partial(_compress_body, n_blk=n_blk),
        grid=(B, G),
        in_specs=[x_spec, x_spec, full((2, half)), full((2, half)),
                  full((2, half, dh)), full((dh, dh)), full((2, half, dh)), full((dh, dh)),
                  full((1, dh))],
        out_specs=[pl.BlockSpec((1, 1, nc, dh + AUG_WIDTH), lambda b, g: (b, g, 0, 0)),
                   pl.BlockSpec((1, 1, dh, nc), lambda b, g: (b, g, 0, 0))],
        out_shape=[jax.ShapeDtypeStruct((B, G, nc, dh + AUG_WIDTH), BF16),
                   jax.ShapeDtypeStruct((B, G, dh, nc), BF16)],
        compiler_params=_cparams("parallel", "parallel"),
        name="compress_kv",
    )(xk, xv, pos_k.reshape(2, half), pos_v.reshape(2, half),
      w1k.reshape(2, half, dh), w2k, w1v.reshape(2, half, dh), w2v, kn_cmp)


def _nsa_body(q_ref, gl_ref, gb_ref, qn_ref, lanes_ref, kc_ref, vcT_ref,
              ks_ref, vsT_ref, kw_ref, vwT_ref, ovT_ref, kmq_ref, o_ref,
              score_ref, gate_ref, qc_ref, qs_ref, m_ref, l_ref, acc_ref, part_ref):
    dh, hg, tq = HEAD_DIM, NSA_GROUP, Q_TILE
    cols = hg * tq
    qi = pl.program_id(2)
    t0 = qi * tq
    contract_last = (((1,), (1,)), ((), ()))
    n_blk = ovT_ref.shape[0]

    scale = dh ** -0.5
    q_heads = [(_rms(q_ref[:, h * dh:(h + 1) * dh], qn_ref[...]) * scale).astype(BF16)
               for h in range(hg)]

    def stack_queries(aug_heads):
        return jnp.concatenate([jnp.concatenate([q_heads[h], aug_heads[h]], axis=1)
                                for h in range(hg)], axis=0)

    hc = kmq_ref.shape[1]
    assert cols % hc == 0 and hc % tq == 0

    def reset(b):
        m_ref[b] = jnp.full((1, cols), NEG_INF, F32)
        l_ref[b] = jnp.zeros((1, cols), F32)
        acc_ref[b] = jnp.zeros((dh, cols), F32)

    def attend(b, queries_ref, parts):
        groups = [slice(i * hc, (i + 1) * hc) for i in range(cols // hc)]
        scores = [[lax.dot_general(k, queries_ref[cs, :], contract_last, preferred_element_type=F32)
                   for cs in groups] for k, _, _, _ in parts]
        for (k, v_pieces, mask, bias), part_scores in zip(parts, scores):
            for cs, s in zip(groups, part_scores):
                if mask is not None:
                    s = jnp.where(mask(), s, NEG_INF)
                if bias is not None:
                    s = s + bias
                m_old = m_ref[b, :, cs]
                m_new = jnp.maximum(m_old, jnp.max(s, axis=0, keepdims=True))
                alpha = jnp.exp(m_old - m_new)
                p = jnp.exp(s - m_new)
                m_ref[b, :, cs] = m_new
                l_ref[b, :, cs] = alpha * l_ref[b, :, cs] + jnp.sum(p, axis=0, keepdims=True)
                p = p.astype(BF16)
                pv = None
                for i, vT in enumerate(v_pieces):
                    n = vT.shape[1]
                    d = jnp.dot(vT, p[i * n:(i + 1) * n, :], preferred_element_type=F32)
                    pv = d if pv is None else pv + d
                acc_ref[b, :, cs] = alpha * acc_ref[b, :, cs] + pv

    def result(b):
        return acc_ref[b] * (1.0 / l_ref[b])

    nc = kc_ref.shape[2]
    qc_ref[...] = stack_queries([jnp.broadcast_to(lanes_ref[0, hg + h:hg + h + 1, :], (tq, AUG_WIDTH)).astype(BF16)
                                 for h in range(hg)])
    s = lax.dot_general(kc_ref[0, 0], qc_ref[...], contract_last, preferred_element_type=F32)
    c_idx = lax.broadcasted_iota(jnp.int32, (nc, cols), 0)
    t_c = t0 + (lax.broadcasted_iota(jnp.int32, (nc, cols), 1) & (tq - 1))
    mask = (c_idx * CMP_STRIDE + (CMP_LEN - 1)) <= t_c
    s = jnp.where(mask, s, NEG_INF)
    m_c = jnp.max(s, axis=0, keepdims=True)
    p_c = jnp.where(mask, jnp.exp(s - m_c), 0.0)
    l_c = jnp.sum(p_c, axis=0, keepdims=True)
    p_c = p_c * jnp.where(l_c > 0.0, 1.0 / l_c, 0.0)
    o_cmp = jnp.dot(vcT_ref[0, 0], p_c.astype(BF16), preferred_element_type=F32)

    cw = WIN_CHUNK
    assert cw == tq and WINDOW % cw == 0 and cw == kmq_ref.shape[0]
    n_back = WINDOW // cw
    WIN, SEL = 0, 1

    def win_part(c, mask=None, bias=None):
        k = kw_ref[0, 0, pl.ds(pl.multiple_of(c * cw, cw), cw), :]
        return k, [vwT_ref[0, 0, c]], mask, bias

    threshold = jnp.where(qi >= n_back, 0, cw)
    reset(WIN)
    attend(WIN, qc_ref,
           [win_part(qi, mask=lambda: kmq_ref[...] <= 0)]
           + [win_part(jnp.maximum(qi - j, 0), bias=jnp.where(qi >= j, 0.0, NEG_INF))
              for j in range(1, n_back)]
           + [win_part(jnp.maximum(qi - n_back, 0), mask=lambda: kmq_ref[...] > threshold)])

    gate_ref[...] = jax.nn.sigmoid(gl_ref[...] + gb_ref[...]).T

    def gate(h, branch):
        r = h * N_BRANCH + branch
        return gate_ref[r:r + 1, :]

    o_win = result(WIN)
    for h in range(hg):
        cs = slice(h * tq, (h + 1) * tq)
        part_ref[:, cs] = gate(h, 0) * o_cmp[:, cs] + gate(h, 2) * o_win[:, cs]

    p_sum = p_c[:, 0:tq]
    for h in range(1, hg):
        p_sum = p_sum + p_c[:, h * tq:(h + 1) * tq]
    hi = p_sum.astype(BF16)
    rem = p_sum - hi.astype(F32)
    mid = rem.astype(BF16)
    lo = (rem - mid.astype(F32)).astype(BF16)
    ovT = ovT_ref[...]
    imp = (jnp.dot(ovT, hi, preferred_element_type=F32)
           + jnp.dot(ovT, mid, preferred_element_type=F32)
           + jnp.dot(ovT, lo, preferred_element_type=F32))
    blk = lax.broadcasted_iota(jnp.int32, (n_blk, tq), 0)
    t_b = t0 + lax.broadcasted_iota(jnp.int32, (n_blk, tq), 1)
    cur = jnp.right_shift(t_b, SEL_BLK.bit_length() - 1)
    valid = blk * SEL_BLK <= t_b
    forced = (blk == 0) | (blk == cur) | (blk == cur - 1)
    score = jnp.where(valid, imp + FORCE_BONUS * forced.astype(F32), NEG_INF)
    score_ref[...] = score
    rank = jnp.zeros((n_blk, tq), F32)
    for j in range(n_blk):
        sj = score_ref[j:j + 1, :]
        beats = (sj > score) | ((sj == score) & (blk > j))
        rank = rank + beats.astype(F32)
    block_bias = jnp.where(rank < float(min(N_SEL, n_blk)), 0.0, NEG_INF)

    ones_rows = (lax.broadcasted_iota(jnp.int32, (8, tq), 0) < 2).astype(F32)
    extra_t = jnp.concatenate([block_bias, ones_rows,
                               jnp.zeros((AUG_WIDTH - n_blk - 8, tq), F32)], axis=0).T
    qs_ref[...] = stack_queries([(extra_t * lanes_ref[0, h:h + 1, :]).astype(BF16) for h in range(hg)])

    ck = SEL_CHUNK
    assert ck == tq and ck == kmq_ref.shape[0]

    def sel_part(c, n=1, mask=None, bias=None):
        k = ks_ref[0, 0, pl.ds(pl.multiple_of(c * ck, ck), n * ck), :]
        return k, [vsT_ref[0, 0, c + i] for i in range(n)], mask, bias

    reset(SEL)

    @pl.loop(0, qi // 2)
    def _(i):
        attend(SEL, qs_ref, [sel_part(2 * i, n=2)])

    odd = qi % 2 == 1
    attend(SEL, qs_ref,
           [sel_part(qi, mask=lambda: kmq_ref[...] <= 0),
            sel_part(jnp.maximum(qi - 1, 0), bias=jnp.where(odd, 0.0, NEG_INF))])

    o_slc = result(SEL)
    for h in range(hg):
        cs = slice(h * tq, (h + 1) * tq)
        o_h = part_ref[:, cs] + gate(h, 1) * o_slc[:, cs]
        o_ref[:, h * dh:(h + 1) * dh] = o_h.T.astype(o_ref.dtype)


def nsa_attention(z, gl, gate_b, q_norm, kcmp, vcmpT, ks, vsT, kw, vwT, *, B, T):
    G, hg, dh, tq = NSA_KV_HEADS, NSA_GROUP, HEAD_DIM, Q_TILE
    nq = T // tq
    nc = kcmp.shape[2]
    n_blk = T // SEL_BLK
    kdim = dh + AUG_WIDTH
    cols = hg * tq
    assert SEL_CHUNK == WIN_CHUNK == tq
    assert SEL_CHUNK % SEL_BLK == 0 and T % SEL_CHUNK == 0 and gl.shape[1] == G * LANES
    assert SEL_CHUNK % tq == 0 and n_blk % 8 == 0 and n_blk + 8 <= AUG_WIDTH

    slopes = np.exp2(-8.0 * (np.arange(NSA_HEADS, dtype=np.float64) + 1.0) / NSA_HEADS).reshape(G, hg)
    lanes = np.zeros((G, 2 * hg, AUG_WIDTH), np.float32)
    lanes[:, :hg, :n_blk] = 1.0
    lanes[:, :hg, n_blk:n_blk + 2] = slopes[:, :, None]
    lanes[:, hg:, n_blk:n_blk + 2] = slopes[:, :, None]
    cs = np.arange(nc)[:, None] * CMP_STRIDE
    bs = np.arange(n_blk)[None, :] * SEL_BLK
    overlap = np.clip(np.minimum(cs + CMP_LEN, bs + SEL_BLK) - np.maximum(cs, bs), 0, None) / CMP_LEN
    overlap[(T - CMP_LEN) // CMP_STRIDE + 1:] = 0.0
    ovT = jnp.asarray(overlap.T, dtype=BF16)
    hc = NSA_COLS_PER_DOT
    kmq = jnp.asarray(np.arange(SEL_CHUNK)[:, None] - (np.arange(hc)[None, :] % tq), dtype=jnp.int32)

    row = lambda b, qi: b * nq + qi
    per_bg = lambda shape: pl.BlockSpec((1, 1) + shape, lambda b, g, qi: (b, g) + (0,) * len(shape))
    const = lambda shape: pl.BlockSpec(shape, lambda b, g, qi: (0,) * len(shape))
    return pl.pallas_call(
        _nsa_body,
        grid=(B, G, nq),
        in_specs=[pl.BlockSpec((tq, hg * dh), lambda b, g, qi: (row(b, qi), g)),
                  pl.BlockSpec((tq, LANES), lambda b, g, qi: (row(b, qi), g)),
                  pl.BlockSpec((1, LANES), lambda b, g, qi: (0, g)),
                  const((1, dh)),
                  pl.BlockSpec((1, 2 * hg, AUG_WIDTH), lambda b, g, qi: (g, 0, 0)),
                  per_bg((nc, kdim)), per_bg((dh, nc)),
                  per_bg((T, kdim)), per_bg((T // SEL_CHUNK, dh, SEL_CHUNK)),
                  per_bg((T, kdim)), per_bg((T // WIN_CHUNK, dh, WIN_CHUNK)),
                  const((n_blk, nc)), const((SEL_CHUNK, hc))],
        out_specs=pl.BlockSpec((tq, hg * dh), lambda b, g, qi: (row(b, qi), g)),
        out_shape=jax.ShapeDtypeStruct((B * T, NSA_WIDTH), BF16),
        scratch_shapes=[pltpu.VMEM((n_blk, tq), F32), pltpu.VMEM((LANES, tq), F32),
                        pltpu.VMEM((cols, kdim), BF16), pltpu.VMEM((cols, kdim), BF16),
                        pltpu.VMEM((2, 1, cols), F32), pltpu.VMEM((2, 1, cols), F32),
                        pltpu.VMEM((2, dh, cols), F32), pltpu.VMEM((dh, cols), F32)],
        compiler_params=_cparams("parallel", "parallel", "arbitrary"),
        name="nsa_attention",
    )(z, gl, gate_b, q_norm, jnp.asarray(lanes), kcmp, vcmpT, ks, vsT, kw, vwT, ovT, kmq)


def _conv_body(a_ref, b_ref, ah_ref, bh_ref, w_ref, cb_ref, lg_ref, lb_ref, o_ref, h_ref, y_ref, *, tt):
    halo = ah_ref[...] * jax.nn.sigmoid(bh_ref[...])
    h_ref[0:CONV_HALO, :] = jnp.where(pl.program_id(1) == 0, 0.0, halo)
    h_ref[CONV_HALO:CONV_HALO + tt, :] = a_ref[...] * jax.nn.sigmoid(b_ref[...])
    C = a_ref.shape[1]
    lead = CONV_HALO - (CONV_KERNEL - 1)
    for s in range(C // LANES):
        cs = slice(s * LANES, (s + 1) * LANES)
        acc = jnp.zeros((tt, LANES), F32)
        for k in range(CONV_KERNEL):
            acc = acc + w_ref[k:k + 1, cs] * h_ref[lead + k:lead + k + tt, cs]
        y_ref[:, cs] = acc + cb_ref[:, cs]
    y = y_ref[...]
    mu = jnp.mean(y, axis=-1, keepdims=True)
    var = jnp.mean(jnp.square(y - mu), axis=-1, keepdims=True)
    yn = (y - mu) * lax.rsqrt(var + LN_EPS) * lg_ref[...] + lb_ref[...]
    o_ref[...] = _silu(yn).astype(o_ref.dtype)


def conformer_conv(z, conv_w, conv_b, ln_g, ln_b, *, B, T, col0, C, tt=128):
    nt = T // tt
    ca = col0 // C
    hb = tt // CONV_HALO
    row = lambda b, t: b * nt + t
    halo_row = lambda b, t: jnp.maximum(row(b, t) * hb - 1, 0)
    const = lambda shape: pl.BlockSpec(shape, lambda b, t: (0,) * len(shape))
    return pl.pallas_call(
        functools.partial(_conv_body, tt=tt),
        grid=(B, nt),
        in_specs=[pl.BlockSpec((tt, C), lambda b, t: (row(b, t), ca)),
                  pl.BlockSpec((tt, C), lambda b, t: (row(b, t), ca + 1)),
                  pl.BlockSpec((CONV_HALO, C), lambda b, t: (halo_row(b, t), ca)),
                  pl.BlockSpec((CONV_HALO, C), lambda b, t: (halo_row(b, t), ca + 1)),
                  const((CONV_KERNEL, C)), const((1, C)), const((1, C)), const((1, C))],
        out_specs=pl.BlockSpec((tt, C), lambda b, t: (row(b, t), 0)),
        out_shape=jax.ShapeDtypeStruct((B * T, C), BF16),
        scratch_shapes=[pltpu.VMEM((CONV_HALO + tt, C), F32), pltpu.VMEM((tt, C), F32)],
        compiler_params=_cparams("parallel", "parallel"),
        name="conformer_conv",
    )(z, z, z, z, conv_w, conv_b, ln_g, ln_b)


def _mem_attn_body(q_ref, kv_ref, qn_ref, kn_ref, wo_ref, x_ref, o_ref):
    dh = HEAD_DIM
    width = MEM_HEADS * dh
    scale = dh ** -0.5
    outs = []
    for h in range(MEM_HEADS):
        cs = slice(h * dh, (h + 1) * dh)
        q = (_rms(q_ref[:, cs], qn_ref[...]) * scale).astype(BF16)
        k = _rms(kv_ref[:, cs], kn_ref[...]).astype(BF16)
        v = kv_ref[:, width + h * dh:width + (h + 1) * dh].astype(BF16)
        s = lax.dot_general(q, k, (((1,), (1,)), ((), ())), preferred_element_type=F32)
        m = jnp.max(s, axis=-1, keepdims=True)
        p = jnp.exp(s - m)
        p = p / jnp.sum(p, axis=-1, keepdims=True)
        outs.append(jnp.dot(p.astype(BF16), v, preferred_element_type=F32).astype(BF16))
    o = jnp.concatenate(outs, axis=1)
    o_ref[...] = x_ref[...] + jnp.dot(o, wo_ref[...], preferred_element_type=F32)


def mem_attention(qm, kvm, mq_norm, mk_norm, w_mo, x, *, B, T, M, tt=256):
    D = x.shape[1]
    width = MEM_HEADS * HEAD_DIM
    nt = T // tt
    row = lambda b, t: b * nt + t
    const = lambda shape: pl.BlockSpec(shape, lambda b, t: (0,) * len(shape))
    return pl.pallas_call(
        _mem_attn_body,
        grid=(B, nt),
        in_specs=[pl.BlockSpec((tt, width), lambda b, t: (row(b, t), 0)),
                  pl.BlockSpec((M, 2 * width), lambda b, t: (b, 0)),
                  const((1, HEAD_DIM)), const((1, HEAD_DIM)), const((width, D)),
                  pl.BlockSpec((tt, D), lambda b, t: (row(b, t), 0))],
        out_specs=pl.BlockSpec((tt, D), lambda b, t: (row(b, t), 0)),
        out_shape=jax.ShapeDtypeStruct((B * T, D), F32),
        compiler_params=_cparams("parallel", "parallel"),
        name="mem_attention",
    )(qm, kvm, mq_norm, mk_norm, w_mo, x)


def kernel(x, mem, norm_mix, w_in, gate_b, q_norm, k_norm_cmp, k_norm_slc, k_norm_win, cmp_pos_k, cmp_pos_v, cmp_k_w1, cmp_k_w2, cmp_v_w1, cmp_v_w2, conv_w, conv_b, conv_ln_g, conv_ln_b, w_out, norm_mem_q, norm_mem_kv, w_mq, w_mk, w_mv, mq_norm, mk_norm, w_mo, norm_ffn, w_gate, w_up, w_down):
    B, T, D = x.shape
    M = mem.shape[1]
    depth = w_in.shape[0]
    G, hg = NSA_KV_HEADS, NSA_GROUP
    C = D - NSA_WIDTH
    n_gate = N_BRANCH * NSA_HEADS
    kv_all = 6 * KV_WIDTH
    assert w_in.shape[2] == NSA_WIDTH + kv_all + n_gate + 2 * C
    assert C == NSA_WIDTH, "column blocking below assumes equal NSA and conv widths"

    row1 = lambda v: v.reshape(1, -1)
    xf = x.reshape(B * T, D)
    memf = mem.reshape(B * M, D)
    u0 = NSA_WIDTH + kv_all + n_gate
    for l in range(depth):
        w = w_in[l]
        w_main = jnp.concatenate([w[:, :NSA_WIDTH], w[:, u0:], w[:, NSA_WIDTH:NSA_WIDTH + kv_all]],
                                 axis=1).astype(BF16)
        wg = w[:, NSA_WIDTH + kv_all:u0].reshape(D, G, hg * N_BRANCH)
        wg = jnp.pad(wg, ((0, 0), (0, 0), (0, LANES - hg * N_BRANCH))).reshape(D, G * LANES).astype(BF16)
        gb = jnp.pad(gate_b[l].reshape(G, hg * N_BRANCH), ((0, 0), (0, LANES - hg * N_BRANCH))).reshape(1, G * LANES)

        z, gl = norm_matmul_gate(xf, row1(norm_mix[l]), w_main, wg, tm=1024, tn=512)
        kv_col0 = NSA_WIDTH + 2 * C
        kc, vc, ks, vsT, kw, vwT = kv_prep(z, row1(k_norm_slc[l]), row1(k_norm_win[l]), B=B, T=T, col0=kv_col0)
        kcmp, vcmpT = compress_kv(kc, vc, cmp_pos_k[l], cmp_pos_v[l],
                                  cmp_k_w1[l].astype(BF16), cmp_k_w2[l].astype(BF16),
                                  cmp_v_w1[l].astype(BF16), cmp_v_w2[l].astype(BF16), row1(k_norm_cmp[l]),
                                  n_blk=T // SEL_BLK)
        o_nsa = nsa_attention(z, gl, gb, row1(q_norm[l]), kcmp, vcmpT, ks, vsT, kw, vwT, B=B, T=T)
        o_conv = conformer_conv(z, conv_w[l], row1(conv_b[l]), row1(conv_ln_g[l]), row1(conv_ln_b[l]),
                                B=B, T=T, col0=NSA_WIDTH, C=C)
        xf = concat_matmul_residual(o_nsa, o_conv, w_out[l].astype(BF16), xf, tm=1024, tn=512)

        qm = norm_matmul(xf, row1(norm_mem_q[l]), w_mq[l].astype(BF16), tm=1024, tn=512)
        w_mkv = jnp.concatenate([w_mk[l], w_mv[l]], axis=1).astype(BF16)
        kvm = norm_matmul(memf, row1(norm_mem_kv[l]), w_mkv, tm=B * M, tn=512)
        xf = mem_attention(qm, kvm, row1(mq_norm[l]), row1(mk_norm[l]), w_mo[l].astype(BF16), xf, B=B, T=T, M=M)

        hidden = norm_swiglu(xf, row1(norm_ffn[l]), w_gate[l].astype(BF16), w_up[l].astype(BF16), tm=1024, tn=512)
        xf = matmul_residual(hidden, w_down[l].astype(BF16), xf, tm=1024, tn=512)
    return xf.reshape(B, T, D)
```

```python
import functools

import numpy as np
import jax
import jax.numpy as jnp
from jax import lax
from jax.experimental import pallas as pl
from jax.experimental.pallas import tpu as pltpu

F32 = jnp.float32
BF16 = jnp.bfloat16

HEAD_DIM = 128
NSA_HEADS = 8
NSA_KV_HEADS = 2
NSA_GROUP = NSA_HEADS // NSA_KV_HEADS
NSA_WIDTH = NSA_HEADS * HEAD_DIM
KV_WIDTH = NSA_KV_HEADS * HEAD_DIM
N_BRANCH = 3
CONV_KERNEL = 31
CMP_LEN = 32
CMP_STRIDE = 16
SEL_BLK = 64
N_SEL = 16
WINDOW = 512
MEM_HEADS = 4
NEG_INF = -1e30
FORCE_BONUS = 1e3
RMS_EPS = 1e-6
LN_EPS = 1e-5

LANES = 128
SUBLANES = 8
VMEM_LIMIT_BYTES = 56 * 1024 * 1024

Q_TILE = 256
SEL_CHUNK = 256
WIN_CHUNK = 256
NSA_COLS_PER_DOT = 512
CONV_HALO = 32


def _cparams(*sem):
    return pltpu.CompilerParams(dimension_semantics=sem, vmem_limit_bytes=VMEM_LIMIT_BYTES)


def _rms(x, g, eps=RMS_EPS):
    ms = jnp.mean(x * x, axis=-1, keepdims=True)
    return x * lax.rsqrt(ms + eps) * g


def _silu(x):
    return x * jax.nn.sigmoid(x)


def _norm_to_scratch(x_ref, g_ref, xn_ref):
    @pl.when(pl.program_id(1) == 0)
    def _():
        xn_ref[...] = _rms(x_ref[...], g_ref[...]).astype(xn_ref.dtype)


def _norm_mm_body(x_ref, g_ref, w_ref, o_ref, xn_ref):
    _norm_to_scratch(x_ref, g_ref, xn_ref)
    o_ref[...] = jnp.dot(xn_ref[...], w_ref[...], preferred_element_type=F32).astype(o_ref.dtype)


def _in_proj_body(x_ref, g_ref, wa_ref, wb_ref, wg_ref, o_ref, og_ref, xn_ref, *, b_lo, b_hi):
    _norm_to_scratch(x_ref, g_ref, xn_ref)
    j = pl.program_id(1)

    @pl.when(j == 0)
    def _():
        og_ref[...] = jnp.dot(xn_ref[...], wg_ref[...], preferred_element_type=F32)

    from_b = (j >= b_lo) & (j < b_hi)

    @pl.when(from_b)
    def _():
        o_ref[...] = jnp.dot(xn_ref[...], wb_ref[...], preferred_element_type=F32)

    @pl.when(jnp.logical_not(from_b))
    def _():
        o_ref[...] = jnp.dot(xn_ref[...], wa_ref[...].astype(BF16), preferred_element_type=F32)


def _swiglu_body(x_ref, wg_ref, wu_ref, o_ref):
    x = x_ref[...]
    a = jnp.dot(x, wg_ref[...].astype(BF16), preferred_element_type=F32)
    b = jnp.dot(x, wu_ref[...].astype(BF16), preferred_element_type=F32)
    o_ref[...] = (_silu(a) * b).astype(o_ref.dtype)


def norm_matmul(x, g, w, *, tm, tn, out_dtype=F32):
    M, K = x.shape
    N = w.shape[1]
    return pl.pallas_call(
        _norm_mm_body,
        grid=(M // tm, N // tn),
        in_specs=[pl.BlockSpec((tm, K), lambda i, j: (i, 0)),
                  pl.BlockSpec((1, K), lambda i, j: (0, 0)),
                  pl.BlockSpec((K, tn), lambda i, j: (0, j))],
        out_specs=pl.BlockSpec((tm, tn), lambda i, j: (i, j)),
        out_shape=jax.ShapeDtypeStruct((M, N), out_dtype),
        scratch_shapes=[pltpu.VMEM((tm, K), BF16)],
        compiler_params=_cparams("parallel", "arbitrary"),
        name="norm_matmul",
    )(x, g, w)


def in_projection(x, g, w_in, w_glu, wg, *, n_q, n_kv, tm, tn):
    M, K = x.shape
    NG = wg.shape[1]
    assert n_q % tn == 0 and n_kv % tn == 0 and w_glu.shape[1] % tn == 0
    q_blocks, glu_blocks, kv_blocks = n_q // tn, w_glu.shape[1] // tn, n_kv // tn
    b_lo, b_hi = q_blocks, q_blocks + glu_blocks
    N = n_q + w_glu.shape[1] + n_kv

    def a_block(i, j):
        return (0, jnp.where(j < b_lo, j, jnp.where(j < b_hi, b_lo - 1, j - glu_blocks)))

    def b_block(i, j):
        return (0, jnp.clip(j - b_lo, 0, glu_blocks - 1))

    return pl.pallas_call(
        functools.partial(_in_proj_body, b_lo=b_lo, b_hi=b_hi),
        grid=(M // tm, N // tn),
        in_specs=[pl.BlockSpec((tm, K), lambda i, j: (i, 0)),
                  pl.BlockSpec((1, K), lambda i, j: (0, 0)),
                  pl.BlockSpec((K, tn), a_block),
                  pl.BlockSpec((K, tn), b_block),
                  pl.BlockSpec((K, NG), lambda i, j: (0, 0))],
        out_specs=[pl.BlockSpec((tm, tn), lambda i, j: (i, j)),
                   pl.BlockSpec((tm, NG), lambda i, j: (i, 0))],
        out_shape=[jax.ShapeDtypeStruct((M, N), F32),
                   jax.ShapeDtypeStruct((M, NG), F32)],
        scratch_shapes=[pltpu.VMEM((tm, K), BF16)],
        compiler_params=_cparams("parallel", "arbitrary"),
        name="in_proj",
    )(x, g, w_in, w_glu, wg)


def swiglu_up(xn, w_gate, w_up, *, tm, tn):
    M, K = xn.shape
    N = w_gate.shape[1]
    return pl.pallas_call(
        _swiglu_body,
        grid=(M // tm, N // tn),
        in_specs=[pl.BlockSpec((tm, K), lambda i, j: (i, 0)),
                  pl.BlockSpec((K, tn), lambda i, j: (0, j)),
                  pl.BlockSpec((K, tn), lambda i, j: (0, j))],
        out_specs=pl.BlockSpec((tm, tn), lambda i, j: (i, j)),
        out_shape=jax.ShapeDtypeStruct((M, N), BF16),
        compiler_params=_cparams("parallel", "arbitrary"),
        name="ffn_up",
    )(xn, w_gate, w_up)


def _mm_res_body(a_ref, w_ref, r_ref, o_ref):
    o_ref[...] = r_ref[...] + jnp.dot(a_ref[...], w_ref[...], preferred_element_type=F32)


def matmul_residual(a, w, res, *, tm, tn):
    M, K = a.shape
    N = w.shape[1]
    return pl.pallas_call(
        _mm_res_body,
        grid=(M // tm, N // tn),
        in_specs=[pl.BlockSpec((tm, K), lambda i, j: (i, 0)),
                  pl.BlockSpec((K, tn), lambda i, j: (0, j)),
                  pl.BlockSpec((tm, tn), lambda i, j: (i, j))],
        out_specs=pl.BlockSpec((tm, tn), lambda i, j: (i, j)),
        out_shape=jax.ShapeDtypeStruct((M, N), F32),
        compiler_params=_cparams("parallel", "arbitrary"),
        name="matmul_residual",
    )(a, w, res)


def _mm2_res_body(a_ref, b_ref, wa_ref, wb_ref, r_ref, o_ref):
    acc = jnp.dot(a_ref[...], wa_ref[...].astype(BF16), preferred_element_type=F32)
    acc = acc + jnp.dot(b_ref[...], wb_ref[...].astype(BF16), preferred_element_type=F32)
    o_ref[...] = r_ref[...] + acc


def concat_matmul_residual(a, b, w, res, *, tm, tn):
    M, Ka = a.shape
    Kb = b.shape[1]
    assert Ka == Kb and w.shape[0] == Ka + Kb
    N = w.shape[1]
    return pl.pallas_call(
        _mm2_res_body,
        grid=(M // tm, N // tn),
        in_specs=[pl.BlockSpec((tm, Ka), lambda i, j: (i, 0)),
                  pl.BlockSpec((tm, Kb), lambda i, j: (i, 0)),
                  pl.BlockSpec((Ka, tn), lambda i, j: (0, j)),
                  pl.BlockSpec((Kb, tn), lambda i, j: (1, j)),
                  pl.BlockSpec((tm, tn), lambda i, j: (i, j))],
        out_specs=pl.BlockSpec((tm, tn), lambda i, j: (i, j)),
        out_shape=jax.ShapeDtypeStruct((M, N), F32),
        compiler_params=_cparams("parallel", "arbitrary"),
        name="out_proj",
    )(a, b, w, w, res)


AUG_WIDTH = LANES


def _position_lanes(pos_hi, pos_lo, onehot, n_blk, shape):
    lane = lax.broadcasted_iota(jnp.int32, shape, 1)
    extra = jnp.where(lane == n_blk, pos_hi, jnp.where(lane == n_blk + 1, pos_lo, 0.0))
    if onehot is not None:
        extra = jnp.where(lane < n_blk, onehot, extra)
    return extra.astype(BF16)


def _kv_prep_body(c_ref, s_ref, w_ref, kns_ref, knw_ref,
                  kc_o, vc_o, ks_o, vsT_o, kw_o, vwT_o, *, tt, n_blk):
    dh = HEAD_DIM
    shape = (tt, AUG_WIDTH)
    pos = pl.program_id(1) * tt + lax.broadcasted_iota(jnp.int32, shape, 0)
    lane = lax.broadcasted_iota(jnp.int32, shape, 1)
    blk = jnp.right_shift(pos, SEL_BLK.bit_length() - 1)
    pos_hi = (blk * SEL_BLK).astype(F32)
    pos_lo = (pos & (SEL_BLK - 1)).astype(F32)
    aug_sel = _position_lanes(pos_hi, pos_lo, (blk == lane).astype(F32), n_blk, shape)
    aug_win = _position_lanes(pos_hi, pos_lo, None, n_blk, shape)
    for g in range(NSA_KV_HEADS):
        k_cols = slice(g * dh, (g + 1) * dh)
        v_cols = slice(KV_WIDTH + g * dh, KV_WIDTH + (g + 1) * dh)
        kc_o[0, g] = c_ref[:, k_cols]
        vc_o[0, g] = c_ref[:, v_cols]
        ks_o[0, g, :, 0:dh] = _rms(s_ref[:, k_cols], kns_ref[...]).astype(BF16)
        ks_o[0, g, :, dh:dh + AUG_WIDTH] = aug_sel
        kw_o[0, g, :, 0:dh] = _rms(w_ref[:, k_cols], knw_ref[...]).astype(BF16)
        kw_o[0, g, :, dh:dh + AUG_WIDTH] = aug_win
        vsT = s_ref[:, v_cols].T.astype(BF16)
        for c in range(tt // SEL_CHUNK):
            vsT_o[0, g, c] = vsT[:, c * SEL_CHUNK:(c + 1) * SEL_CHUNK]
        vwT = w_ref[:, v_cols].T.astype(BF16)
        for c in range(tt // WIN_CHUNK):
            vwT_o[0, g, c] = vwT[:, c * WIN_CHUNK:(c + 1) * WIN_CHUNK]


def kv_prep(z, kn_slc, kn_win, *, B, T, col0, tt=256):
    G, dh = NSA_KV_HEADS, HEAD_DIM
    nt = T // tt
    n_blk = T // SEL_BLK
    assert n_blk + 2 <= AUG_WIDTH
    cb = col0 // (2 * KV_WIDTH)
    row = lambda b, t: b * nt + t
    kdim = dh + AUG_WIDTH
    return pl.pallas_call(
        functools.partial(_kv_prep_body, tt=tt, n_blk=n_blk),
        grid=(B, nt),
        in_specs=[pl.BlockSpec((tt, 2 * KV_WIDTH), lambda b, t: (row(b, t), cb)),
                  pl.BlockSpec((tt, 2 * KV_WIDTH), lambda b, t: (row(b, t), cb + 1)),
                  pl.BlockSpec((tt, 2 * KV_WIDTH), lambda b, t: (row(b, t), cb + 2)),
                  pl.BlockSpec((1, dh), lambda b, t: (0, 0)),
                  pl.BlockSpec((1, dh), lambda b, t: (0, 0))],
        out_specs=[pl.BlockSpec((1, G, tt, dh), lambda b, t: (b, 0, t, 0)),
                   pl.BlockSpec((1, G, tt, dh), lambda b, t: (b, 0, t, 0)),
                   pl.BlockSpec((1, G, tt, kdim), lambda b, t: (b, 0, t, 0)),
                   pl.BlockSpec((1, G, tt // SEL_CHUNK, dh, SEL_CHUNK), lambda b, t: (b, 0, t, 0, 0)),
                   pl.BlockSpec((1, G, tt, kdim), lambda b, t: (b, 0, t, 0)),
                   pl.BlockSpec((1, G, tt // WIN_CHUNK, dh, WIN_CHUNK), lambda b, t: (b, 0, t, 0, 0))],
        out_shape=[jax.ShapeDtypeStruct((B, G, T, dh), F32),
                   jax.ShapeDtypeStruct((B, G, T, dh), F32),
                   jax.ShapeDtypeStruct((B, G, T, kdim), BF16),
                   jax.ShapeDtypeStruct((B, G, T // SEL_CHUNK, dh, SEL_CHUNK), BF16),
                   jax.ShapeDtypeStruct((B, G, T, kdim), BF16),
                   jax.ShapeDtypeStruct((B, G, T // WIN_CHUNK, dh, WIN_CHUNK), BF16)],
        compiler_params=_cparams("parallel", "parallel"),
        name="kv_prep",
    )(z, z, z, kn_slc, kn_win)


def _compress_body(xk_ref, xv_ref, pk_ref, pv_ref, w1k_ref, w2k_ref, w1v_ref, w2v_ref, kn_ref,
                   kc_o, vcT_o, *, n_blk):
    dh = HEAD_DIM

    def phi(x_ref, p_ref, w1_ref, w2_ref):
        x = x_ref[0, 0]
        first = jnp.dot((x + p_ref[0:1, :]).astype(BF16), w1_ref[0], preferred_element_type=F32)
        second = jnp.dot((x + p_ref[1:2, :]).astype(BF16), w1_ref[1], preferred_element_type=F32)
        n = second.shape[0]
        h = first + pltpu.roll(second, n - 1, 0)
        return jnp.dot(_silu(h).astype(BF16), w2_ref[...], preferred_element_type=F32)

    k = _rms(phi(xk_ref, pk_ref, w1k_ref, w2k_ref), kn_ref[...])
    nc = k.shape[0]
    shape = (nc, AUG_WIDTH)
    start = (lax.broadcasted_iota(jnp.int32, shape, 0) * CMP_STRIDE).astype(F32)
    centre = jnp.full(shape, (CMP_LEN - 1) / 2.0, F32)
    kc_o[0, 0, :, 0:dh] = k.astype(BF16)
    kc_o[0, 0, :, dh:dh + AUG_WIDTH] = _position_lanes(start, centre, None, n_blk, shape)
    vcT_o[0, 0] = phi(xv_ref, pv_ref, w1v_ref, w2v_ref).T.astype(BF16)


def compress_kv(kc, vc, pos_k, pos_v, w1k, w2k, w1v, w2v, kn_cmp, *, n_blk):
    B, G, T, dh = kc.shape
    assert CMP_LEN == 2 * CMP_STRIDE
    nc = T // CMP_STRIDE
    half = CMP_STRIDE * dh
    xk = kc.reshape(B, G, nc, half)
    xv = vc.reshape(B, G, nc, half)
    x_spec = pl.BlockSpec((1, 1, nc, half), lambda b, g: (b, g, 0, 0))
    full = lambda shape: pl.BlockSpec(shape, lambda b, g: (0,) * len(shape))
    return pl.pallas_call(
        functools.partial(_compress_body, n_blk=n_blk),
        grid=(B, G),
        in_specs=[x_spec, x_spec, full((2, half)), full((2, half)),
                  full((2, half, dh)), full((dh, dh)), full((2, half, dh)), full((dh, dh)),
                  full((1, dh))],
        out_specs=[pl.BlockSpec((1, 1, nc, dh + AUG_WIDTH), lambda b, g: (b, g, 0, 0)),
                   pl.BlockSpec((1, 1, dh, nc), lambda b, g: (b, g, 0, 0))],
        out_shape=[jax.ShapeDtypeStruct((B, G, nc, dh + AUG_WIDTH), BF16),
                   jax.ShapeDtypeStruct((B, G, dh, nc), BF16)],
        compiler_params=_cparams("parallel", "parallel"),
        name="compress_kv",
    )(xk, xv, pos_k.reshape(2, half), pos_v.reshape(2, half),
      w1k.reshape(2, half, dh), w2k, w1v.reshape(2, half, dh), w2v, kn_cmp)


def _nsa_body(q_ref, gl_ref, gb_ref, qn_ref, lanes_ref, kc_ref, vcT_ref,
              ks_ref, vsT_ref, kw_ref, vwT_ref, ovT_ref, kmq_ref, o_ref,
              score_ref, gate_ref, qc_ref, qs_ref, m_ref, l_ref, acc_ref, part_ref):
    dh, hg, tq = HEAD_DIM, NSA_GROUP, Q_TILE
    cols = hg * tq
    qi = pl.program_id(2)
    t0 = qi * tq
    contract_last = (((1,), (1,)), ((), ()))
    n_blk = ovT_ref.shape[0]

    scale = dh ** -0.5
    q_heads = [(_rms(q_ref[:, h * dh:(h + 1) * dh], qn_ref[...]) * scale).astype(BF16)
               for h in range(hg)]

    def stack_queries(aug_heads):
        return jnp.concatenate([jnp.concatenate([q_heads[h], aug_heads[h]], axis=1)
                                for h in range(hg)], axis=0)

    hc = kmq_ref.shape[1]
    assert cols % hc == 0 and hc % tq == 0

    def reset(b):
        m_ref[b] = jnp.full((1, cols), NEG_INF, F32)
        l_ref[b] = jnp.zeros((1, cols), F32)
        acc_ref[b] = jnp.zeros((dh, cols), F32)

    def attend(b, queries_ref, parts):
        groups = [slice(i * hc, (i + 1) * hc) for i in range(cols // hc)]
        scores = [[lax.dot_general(k, queries_ref[cs, :], contract_last, preferred_element_type=F32)
                   for cs in groups] for k, _, _, _ in parts]
        for (k, v_pieces, mask, bias), part_scores in zip(parts, scores):
            for cs, s in zip(groups, part_scores):
                if mask is not None:
                    s = jnp.where(mask(), s, NEG_INF)
                if bias is not None:
                    s = s + bias
                m_old = m_ref[b, :, cs]
                m_new = jnp.maximum(m_old, jnp.max(s, axis=0, keepdims=True))
                alpha = jnp.exp(m_old - m_new)
                p = jnp.exp(s - m_new)
                m_ref[b, :, cs] = m_new
                l_ref[b, :, cs] = alpha * l_ref[b, :, cs] + jnp.sum(p, axis=0, keepdims=True)
                p = p.astype(BF16)
                pv = None
                for i, vT in enumerate(v_pieces):
                    n = vT.shape[1]
                    d = jnp.dot(vT, p[i * n:(i + 1) * n, :], preferred_element_type=F32)
                    pv = d if pv is None else pv + d
                acc_ref[b, :, cs] = alpha * acc_ref[b, :, cs] + pv

    def result(b):
        return acc_ref[b] * (1.0 / l_ref[b])

    nc = kc_ref.shape[2]
    qc_ref[...] = stack_queries([jnp.broadcast_to(lanes_ref[0, hg + h:hg + h + 1, :], (tq, AUG_WIDTH)).astype(BF16)
                                 for h in range(hg)])
    s = lax.dot_general(kc_ref[0, 0], qc_ref[...], contract_last, preferred_element_type=F32)
    c_idx = lax.broadcasted_iota(jnp.int32, (nc, cols), 0)
    t_c = t0 + (lax.broadcasted_iota(jnp.int32, (nc, cols), 1) & (tq - 1))
    mask = (c_idx * CMP_STRIDE + (CMP_LEN - 1)) <= t_c
    s = jnp.where(mask, s, NEG_INF)
    m_c = jnp.max(s, axis=0, keepdims=True)
    p_c = jnp.where(mask, jnp.exp(s - m_c), 0.0)
    l_c = jnp.sum(p_c, axis=0, keepdims=True)
    p_c = p_c * jnp.where(l_c > 0.0, 1.0 / l_c, 0.0)
    o_cmp = jnp.dot(vcT_ref[0, 0], p_c.astype(BF16), preferred_element_type=F32)

    cw = WIN_CHUNK
    assert cw == tq and WINDOW % cw == 0 and cw == kmq_ref.shape[0]
    n_back = WINDOW // cw
    WIN, SEL = 0, 1

    def win_part(c, mask=None, bias=None):
        k = kw_ref[0, 0, pl.ds(pl.multiple_of(c * cw, cw), cw), :]
        return k, [vwT_ref[0, 0, c]], mask, bias

    threshold = jnp.where(qi >= n_back, 0, cw)
    reset(WIN)
    attend(WIN, qc_ref,
           [win_part(qi, mask=lambda: kmq_ref[...] <= 0)]
           + [win_part(jnp.maximum(qi - j, 0), bias=jnp.where(qi >= j, 0.0, NEG_INF))
              for j in range(1, n_back)]
           + [win_part(jnp.maximum(qi - n_back, 0), mask=lambda: kmq_ref[...] > threshold)])

    gate_ref[...] = jax.nn.sigmoid(gl_ref[...] + gb_ref[...]).T

    def gate(h, branch):
        r = h * N_BRANCH + branch
        return gate_ref[r:r + 1, :]

    o_win = result(WIN)
    for h in range(hg):
        cs = slice(h * tq, (h + 1) * tq)
        part_ref[:, cs] = gate(h, 0) * o_cmp[:, cs] + gate(h, 2) * o_win[:, cs]

    p_sum = p_c[:, 0:tq]
    for h in range(1, hg):
        p_sum = p_sum + p_c[:, h * tq:(h + 1) * tq]
    hi = p_sum.astype(BF16)
    rem = p_sum - hi.astype(F32)
    mid = rem.astype(BF16)
    lo = (rem - mid.astype(F32)).astype(BF16)
    ovT = ovT_ref[...]
    imp = (jnp.dot(ovT, hi, preferred_element_type=F32)
           + jnp.dot(ovT, mid, preferred_element_type=F32)
           + jnp.dot(ovT, lo, preferred_element_type=F32))
    blk = lax.broadcasted_iota(jnp.int32, (n_blk, tq), 0)
    t_b = t0 + lax.broadcasted_iota(jnp.int32, (n_blk, tq), 1)
    cur = jnp.right_shift(t_b, SEL_BLK.bit_length() - 1)
    valid = blk * SEL_BLK <= t_b
    forced = (blk == 0) | (blk == cur) | (blk == cur - 1)
    score = jnp.where(valid, imp + FORCE_BONUS * forced.astype(F32), NEG_INF)
    score_ref[...] = score
    rank = jnp.zeros((n_blk, tq), F32)
    for j in range(n_blk):
        sj = score_ref[j:j + 1, :]
        beats = (sj > score) | ((sj == score) & (blk > j))
        rank = rank + beats.astype(F32)
    block_bias = jnp.where(rank < float(min(N_SEL, n_blk)), 0.0, NEG_INF)

    ones_rows = (lax.broadcasted_iota(jnp.int32, (8, tq), 0) < 2).astype(F32)
    extra_t = jnp.concatenate([block_bias, ones_rows,
                               jnp.zeros((AUG_WIDTH - n_blk - 8, tq), F32)], axis=0).T
    qs_ref[...] = stack_queries([(extra_t * lanes_ref[0, h:h + 1, :]).astype(BF16) for h in range(hg)])

    ck = SEL_CHUNK
    assert ck == tq and ck == kmq_ref.shape[0]

    def sel_part(c, n=1, mask=None, bias=None):
        k = ks_ref[0, 0, pl.ds(pl.multiple_of(c * ck, ck), n * ck), :]
        return k, [vsT_ref[0, 0, c + i] for i in range(n)], mask, bias

    reset(SEL)

    @pl.loop(0, qi // 2)
    def _(i):
        attend(SEL, qs_ref, [sel_part(2 * i, n=2)])

    odd = qi % 2 == 1
    attend(SEL, qs_ref,
           [sel_part(qi, mask=lambda: kmq_ref[...] <= 0),
            sel_part(jnp.maximum(qi - 1, 0), bias=jnp.where(odd, 0.0, NEG_INF))])

    o_slc = result(SEL)
    for h in range(hg):
        cs = slice(h * tq, (h + 1) * tq)
        o_h = part_ref[:, cs] + gate(h, 1) * o_slc[:, cs]
        o_ref[:, h * dh:(h + 1) * dh] = o_h.T.astype(o_ref.dtype)


def nsa_attention(z, gl, gate_b, q_norm, kcmp, vcmpT, ks, vsT, kw, vwT, *, B, T):
    G, hg, dh, tq = NSA_KV_HEADS, NSA_GROUP, HEAD_DIM, Q_TILE
    nq = T // tq
    nc = kcmp.shape[2]
    n_blk = T // SEL_BLK
    kdim = dh + AUG_WIDTH
    cols = hg * tq
    assert SEL_CHUNK == WIN_CHUNK == tq
    assert SEL_CHUNK % SEL_BLK == 0 and T % SEL_CHUNK == 0 and gl.shape[1] == G * LANES
    assert SEL_CHUNK % tq == 0 and n_blk % 8 == 0 and n_blk + 8 <= AUG_WIDTH

    slopes = np.exp2(-8.0 * (np.arange(NSA_HEADS, dtype=np.float64) + 1.0) / NSA_HEADS).reshape(G, hg)
    lanes = np.zeros((G, 2 * hg, AUG_WIDTH), np.float32)
    lanes[:, :hg, :n_blk] = 1.0
    lanes[:, :hg, n_blk:n_blk + 2] = slopes[:, :, None]
    lanes[:, hg:, n_blk:n_blk + 2] = slopes[:, :, None]
    cs = np.arange(nc)[:, None] * CMP_STRIDE
    bs = np.arange(n_blk)[None, :] * SEL_BLK
    overlap = np.clip(np.minimum(cs + CMP_LEN, bs + SEL_BLK) - np.maximum(cs, bs), 0, None) / CMP_LEN
    overlap[(T - CMP_LEN) // CMP_STRIDE + 1:] = 0.0
    ovT = jnp.asarray(overlap.T, dtype=BF16)
    hc = NSA_COLS_PER_DOT
    kmq = jnp.asarray(np.arange(SEL_CHUNK)[:, None] - (np.arange(hc)[None, :] % tq), dtype=jnp.int32)

    row = lambda b, qi: b * nq + qi
    per_bg = lambda shape: pl.BlockSpec((1, 1) + shape, lambda b, g, qi: (b, g) + (0,) * len(shape))
    const = lambda shape: pl.BlockSpec(shape, lambda b, g, qi: (0,) * len(shape))
    return pl.pallas_call(
        _nsa_body,
        grid=(B, G, nq),
        in_specs=[pl.BlockSpec((tq, hg * dh), lambda b, g, qi: (row(b, qi), g)),
                  pl.BlockSpec((tq, LANES), lambda b, g, qi: (row(b, qi), g)),
                  pl.BlockSpec((1, LANES), lambda b, g, qi: (0, g)),
                  const((1, dh)),
                  pl.BlockSpec((1, 2 * hg, AUG_WIDTH), lambda b, g, qi: (g, 0, 0)),
                  per_bg((nc, kdim)), per_bg((dh, nc)),
                  per_bg((T, kdim)), per_bg((T // SEL_CHUNK, dh, SEL_CHUNK)),
                  per_bg((T, kdim)), per_bg((T // WIN_CHUNK, dh, WIN_CHUNK)),
                  const((n_blk, nc)), const((SEL_CHUNK, hc))],
        out_specs=pl.BlockSpec((tq, hg * dh), lambda b, g, qi: (row(b, qi), g)),
        out_shape=jax.ShapeDtypeStruct((B * T, NSA_WIDTH), BF16),
        scratch_shapes=[pltpu.VMEM((n_blk, tq), F32), pltpu.VMEM((LANES, tq), F32),
                        pltpu.VMEM((cols, kdim), BF16), pltpu.VMEM((cols, kdim), BF16),
                        pltpu.VMEM((2, 1, cols), F32), pltpu.VMEM((2, 1, cols), F32),
                        pltpu.VMEM((2, dh, cols), F32), pltpu.VMEM((dh, cols), F32)],
        compiler_params=_cparams("parallel", "parallel", "arbitrary"),
        name="nsa_attention",
    )(z, gl, gate_b, q_norm, jnp.asarray(lanes), kcmp, vcmpT, ks, vsT, kw, vwT, ovT, kmq)


def _conv_body(a_ref, b_ref, ah_ref, bh_ref, w_ref, cb_ref, lg_ref, lb_ref, o_ref,
               h_ref, g_ref, y_ref, *, tt):
    C = a_ref.shape[1]
    halo = ah_ref[...] * jax.nn.sigmoid(bh_ref[...])
    h_ref[0:CONV_HALO, :] = jnp.where(pl.program_id(1) == 0, 0.0, halo)
    h_ref[CONV_HALO:CONV_HALO + tt, :] = a_ref[...] * jax.nn.sigmoid(b_ref[...])
    h_ref[CONV_HALO + tt:CONV_HALO + tt + SUBLANES, :] = jnp.zeros((SUBLANES, C), F32)
    lead = CONV_HALO - (CONV_KERNEL - 1)
    rows = tt + SUBLANES
    for s in range(C // LANES):
        cs = slice(s * LANES, (s + 1) * LANES)
        acc = None
        for r in range(SUBLANES):
            group = None
            for o in range(lead, lead + CONV_KERNEL):
                if o % SUBLANES == r:
                    term = w_ref[o - lead:o - lead + 1, cs] * h_ref[o - r:o - r + rows, cs]
                    group = term if group is None else group + term
            if r == 0:
                piece = group[0:tt, :]
            else:
                g_ref[r] = group
                piece = g_ref[r, r:r + tt, :]
            acc = piece if acc is None else acc + piece
        y_ref[:, cs] = acc + cb_ref[:, cs]
    y = y_ref[...]
    mu = jnp.mean(y, axis=-1, keepdims=True)
    var = jnp.mean(jnp.square(y - mu), axis=-1, keepdims=True)
    yn = (y - mu) * lax.rsqrt(var + LN_EPS) * lg_ref[...] + lb_ref[...]
    o_ref[...] = _silu(yn).astype(o_ref.dtype)


def conformer_conv(z, conv_w, conv_b, ln_g, ln_b, *, B, T, col0, C, tt=128):
    nt = T // tt
    ca = col0 // C
    hb = tt // CONV_HALO
    row = lambda b, t: b * nt + t
    halo_row = lambda b, t: jnp.maximum(row(b, t) * hb - 1, 0)
    const = lambda shape: pl.BlockSpec(shape, lambda b, t: (0,) * len(shape))
    return pl.pallas_call(
        functools.partial(_conv_body, tt=tt),
        grid=(B, nt),
        in_specs=[pl.BlockSpec((tt, C), lambda b, t: (row(b, t), ca)),
                  pl.BlockSpec((tt, C), lambda b, t: (row(b, t), ca + 1)),
                  pl.BlockSpec((CONV_HALO, C), lambda b, t: (halo_row(b, t), ca)),
                  pl.BlockSpec((CONV_HALO, C), lambda b, t: (halo_row(b, t), ca + 1)),
                  const((CONV_KERNEL, C)), const((1, C)), const((1, C)), const((1, C))],
        out_specs=pl.BlockSpec((tt, C), lambda b, t: (row(b, t), 0)),
        out_shape=jax.ShapeDtypeStruct((B * T, C), BF16),
        scratch_shapes=[pltpu.VMEM((CONV_HALO + tt + SUBLANES, C), F32),
                        pltpu.VMEM((SUBLANES, tt + SUBLANES, LANES), F32),
                        pltpu.VMEM((tt, C), F32)],
        compiler_params=_cparams("parallel", "parallel"),
        name="conformer_conv",
    )(z, z, z, z, conv_w, conv_b, ln_g, ln_b)


def _mem_attn_body(q_ref, kv_ref, qn_ref, kn_ref, wo_ref, x_ref, gn_ref, o_ref, on_ref):
    dh = HEAD_DIM
    width = MEM_HEADS * dh
    scale = dh ** -0.5
    outs = []
    for h in range(MEM_HEADS):
        cs = slice(h * dh, (h + 1) * dh)
        q = (_rms(q_ref[:, cs], qn_ref[...]) * scale).astype(BF16)
        k = _rms(kv_ref[:, cs], kn_ref[...]).astype(BF16)
        v = kv_ref[:, width + h * dh:width + (h + 1) * dh].astype(BF16)
        s = lax.dot_general(q, k, (((1,), (1,)), ((), ())), preferred_element_type=F32)
        m = jnp.max(s, axis=-1, keepdims=True)
        p = jnp.exp(s - m)
        p = p / jnp.sum(p, axis=-1, keepdims=True)
        outs.append(jnp.dot(p.astype(BF16), v, preferred_element_type=F32).astype(BF16))
    o = jnp.concatenate(outs, axis=1)
    y = x_ref[...] + jnp.dot(o, wo_ref[...], preferred_element_type=F32)
    o_ref[...] = y
    on_ref[...] = _rms(y, gn_ref[...]).astype(on_ref.dtype)


def mem_attention(qm, kvm, mq_norm, mk_norm, w_mo, x, next_norm, *, B, T, M, tt=256):
    D = x.shape[1]
    width = MEM_HEADS * HEAD_DIM
    nt = T // tt
    row = lambda b, t: b * nt + t
    const = lambda shape: pl.BlockSpec(shape, lambda b, t: (0,) * len(shape))
    return pl.pallas_call(
        _mem_attn_body,
        grid=(B, nt),
        in_specs=[pl.BlockSpec((tt, width), lambda b, t: (row(b, t), 0)),
                  pl.BlockSpec((M, 2 * width), lambda b, t: (b, 0)),
                  const((1, HEAD_DIM)), const((1, HEAD_DIM)), const((width, D)),
                  pl.BlockSpec((tt, D), lambda b, t: (row(b, t), 0)),
                  const((1, D))],
        out_specs=[pl.BlockSpec((tt, D), lambda b, t: (row(b, t), 0)),
                   pl.BlockSpec((tt, D), lambda b, t: (row(b, t), 0))],
        out_shape=[jax.ShapeDtypeStruct((B * T, D), F32),
                   jax.ShapeDtypeStruct((B * T, D), BF16)],
        compiler_params=_cparams("parallel", "parallel"),
        name="mem_attention",
    )(qm, kvm, mq_norm, mk_norm, w_mo, x, next_norm)


def kernel(x, mem, norm_mix, w_in, gate_b, q_norm, k_norm_cmp, k_norm_slc, k_norm_win, cmp_pos_k, cmp_pos_v, cmp_k_w1, cmp_k_w2, cmp_v_w1, cmp_v_w2, conv_w, conv_b, conv_ln_g, conv_ln_b, w_out, norm_mem_q, norm_mem_kv, w_mq, w_mk, w_mv, mq_norm, mk_norm, w_mo, norm_ffn, w_gate, w_up, w_down):
    B, T, D = x.shape
    M = mem.shape[1]
    depth = w_in.shape[0]
    G, hg = NSA_KV_HEADS, NSA_GROUP
    C = D - NSA_WIDTH
    n_gate = N_BRANCH * NSA_HEADS
    kv_all = 6 * KV_WIDTH
    assert w_in.shape[2] == NSA_WIDTH + kv_all + n_gate + 2 * C
    assert C == NSA_WIDTH, "column blocking below assumes equal NSA and conv widths"

    row1 = lambda v: v.reshape(1, -1)
    xf = x.reshape(B * T, D)
    memf = mem.reshape(B * M, D)
    u0 = NSA_WIDTH + kv_all + n_gate
    for l in range(depth):
        w = w_in[l]
        w_glu = w[:, u0:].astype(BF16)
        wg = w[:, NSA_WIDTH + kv_all:u0].reshape(D, G, hg * N_BRANCH)
        wg = jnp.pad(wg, ((0, 0), (0, 0), (0, LANES - hg * N_BRANCH))).reshape(D, G * LANES).astype(BF16)
        gb = jnp.pad(gate_b[l].reshape(G, hg * N_BRANCH), ((0, 0), (0, LANES - hg * N_BRANCH))).reshape(1, G * LANES)

        z, gl = in_projection(xf, row1(norm_mix[l]), w, w_glu, wg, n_q=NSA_WIDTH, n_kv=kv_all, tm=1024, tn=512)
        kv_col0 = NSA_WIDTH + 2 * C
        kc, vc, ks, vsT, kw, vwT = kv_prep(z, row1(k_norm_slc[l]), row1(k_norm_win[l]), B=B, T=T, col0=kv_col0)
        kcmp, vcmpT = compress_kv(kc, vc, cmp_pos_k[l], cmp_pos_v[l],
                                  cmp_k_w1[l].astype(BF16), cmp_k_w2[l].astype(BF16),
                                  cmp_v_w1[l].astype(BF16), cmp_v_w2[l].astype(BF16), row1(k_norm_cmp[l]),
                                  n_blk=T // SEL_BLK)
        o_nsa = nsa_attention(z, gl, gb, row1(q_norm[l]), kcmp, vcmpT, ks, vsT, kw, vwT, B=B, T=T)
        o_conv = conformer_conv(z, conv_w[l], row1(conv_b[l]), row1(conv_ln_g[l]), row1(conv_ln_b[l]),
                                B=B, T=T, col0=NSA_WIDTH, C=C)
        xf = concat_matmul_residual(o_nsa, o_conv, w_out[l], xf, tm=1024, tn=512)

        qm = norm_matmul(xf, row1(norm_mem_q[l]), w_mq[l].astype(BF16), tm=1024, tn=512)
        w_mkv = jnp.concatenate([w_mk[l], w_mv[l]], axis=1).astype(BF16)
        kvm = norm_matmul(memf, row1(norm_mem_kv[l]), w_mkv, tm=B * M, tn=512)
        xf, xn = mem_attention(qm, kvm, row1(mq_norm[l]), row1(mk_norm[l]), w_mo[l].astype(BF16), xf,
                               row1(norm_ffn[l]), B=B, T=T, M=M)

        hidden = swiglu_up(xn, w_gate[l], w_up[l], tm=1024, tn=512)
        xf = matmul_residual(hidden, w_down[l].astype(BF16), xf, tm=1024, tn=512)
    return xf.reshape(B, T, D)
```

```python
import functools

import ml_dtypes
import numpy as np
import jax
import jax.numpy as jnp
from jax import lax
from jax.experimental import pallas as pl
from jax.experimental.pallas import tpu as pltpu

F32 = jnp.float32
BF16 = jnp.bfloat16

HEAD_DIM = 128
NSA_HEADS = 8
NSA_KV_HEADS = 2
NSA_GROUP = NSA_HEADS // NSA_KV_HEADS
NSA_WIDTH = NSA_HEADS * HEAD_DIM
KV_WIDTH = NSA_KV_HEADS * HEAD_DIM
N_BRANCH = 3
CONV_KERNEL = 31
CMP_LEN = 32
CMP_STRIDE = 16
SEL_BLK = 64
N_SEL = 16
WINDOW = 512
MEM_HEADS = 4
NEG_INF = -1e30
FORCE_BONUS = 1e3
RMS_EPS = 1e-6
LN_EPS = 1e-5

LANES = 128
SUBLANES = 8
VMEM_LIMIT_BYTES = 56 * 1024 * 1024

Q_TILE = 256
SEL_CHUNK = 256
WIN_CHUNK = 256
NSA_COLS_PER_DOT = 512
CONV_HALO = 32


def _cparams(*sem):
    return pltpu.CompilerParams(dimension_semantics=sem, vmem_limit_bytes=VMEM_LIMIT_BYTES)


def _rms(x, g, eps=RMS_EPS):
    ms = jnp.mean(x * x, axis=-1, keepdims=True)
    return x * lax.rsqrt(ms + eps) * g


def _silu(x):
    return x * jax.nn.sigmoid(x)


def _norm_to_scratch(x_ref, g_ref, xn_ref):
    @pl.when(pl.program_id(1) == 0)
    def _():
        xn_ref[...] = _rms(x_ref[...], g_ref[...]).astype(xn_ref.dtype)


def _norm_mm_body(x_ref, g_ref, w_ref, o_ref, xn_ref):
    _norm_to_scratch(x_ref, g_ref, xn_ref)
    o_ref[...] = jnp.dot(xn_ref[...], w_ref[...], preferred_element_type=F32).astype(o_ref.dtype)


def _in_proj_body(x_ref, g_ref, wa_ref, wb_ref, wg_ref, o_ref, og_ref, xn_ref, *, b_lo, b_hi):
    _norm_to_scratch(x_ref, g_ref, xn_ref)
    j = pl.program_id(1)

    @pl.when(j == 0)
    def _():
        og_ref[...] = jnp.dot(xn_ref[...], wg_ref[...].astype(BF16), preferred_element_type=F32)

    from_b = (j >= b_lo) & (j < b_hi)

    @pl.when(from_b)
    def _():
        o_ref[...] = jnp.dot(xn_ref[...], wb_ref[...].astype(BF16), preferred_element_type=F32)

    @pl.when(jnp.logical_not(from_b))
    def _():
        o_ref[...] = jnp.dot(xn_ref[...], wa_ref[...].astype(BF16), preferred_element_type=F32)


def _swiglu_body(x_ref, wg_ref, wu_ref, o_ref):
    x = x_ref[...]
    a = jnp.dot(x, wg_ref[...].astype(BF16), preferred_element_type=F32)
    b = jnp.dot(x, wu_ref[...].astype(BF16), preferred_element_type=F32)
    o_ref[...] = (_silu(a) * b).astype(o_ref.dtype)


def norm_matmul(x, g, w, *, tm, tn, out_dtype=F32):
    M, K = x.shape
    N = w.shape[1]
    return pl.pallas_call(
        _norm_mm_body,
        grid=(M // tm, N // tn),
        in_specs=[pl.BlockSpec((tm, K), lambda i, j: (i, 0)),
                  pl.BlockSpec((1, K), lambda i, j: (0, 0)),
                  pl.BlockSpec((K, tn), lambda i, j: (0, j))],
        out_specs=pl.BlockSpec((tm, tn), lambda i, j: (i, j)),
        out_shape=jax.ShapeDtypeStruct((M, N), out_dtype),
        scratch_shapes=[pltpu.VMEM((tm, K), BF16)],
        compiler_params=_cparams("parallel", "arbitrary"),
        name="norm_matmul",
    )(x, g, w)


def in_projection(x, g, w_in, w_glu, wg, *, n_q, n_kv, tm, tn):
    M, K = x.shape
    NG = wg.shape[1]
    assert n_q % tn == 0 and n_kv % tn == 0 and w_glu.shape[1] % tn == 0
    q_blocks, glu_blocks, kv_blocks = n_q // tn, w_glu.shape[1] // tn, n_kv // tn
    b_lo, b_hi = q_blocks, q_blocks + glu_blocks
    N = n_q + w_glu.shape[1] + n_kv

    def a_block(i, j):
        return (0, jnp.where(j < b_lo, j, jnp.where(j < b_hi, b_lo - 1, j - glu_blocks)))

    def b_block(i, j):
        return (0, jnp.clip(j - b_lo, 0, glu_blocks - 1))

    return pl.pallas_call(
        functools.partial(_in_proj_body, b_lo=b_lo, b_hi=b_hi),
        grid=(M // tm, N // tn),
        in_specs=[pl.BlockSpec((tm, K), lambda i, j: (i, 0)),
                  pl.BlockSpec((1, K), lambda i, j: (0, 0)),
                  pl.BlockSpec((K, tn), a_block),
                  pl.BlockSpec((K, tn), b_block),
                  pl.BlockSpec((K, NG), lambda i, j: (0, 0))],
        out_specs=[pl.BlockSpec((tm, tn), lambda i, j: (i, j)),
                   pl.BlockSpec((tm, NG), lambda i, j: (i, 0))],
        out_shape=[jax.ShapeDtypeStruct((M, N), F32),
                   jax.ShapeDtypeStruct((M, NG), F32)],
        scratch_shapes=[pltpu.VMEM((tm, K), BF16)],
        compiler_params=_cparams("parallel", "arbitrary"),
        name="in_proj",
    )(x, g, w_in, w_glu, wg)


def swiglu_up(xn, w_gate, w_up, *, tm, tn):
    M, K = xn.shape
    N = w_gate.shape[1]
    return pl.pallas_call(
        _swiglu_body,
        grid=(M // tm, N // tn),
        in_specs=[pl.BlockSpec((tm, K), lambda i, j: (i, 0)),
                  pl.BlockSpec((K, tn), lambda i, j: (0, j)),
                  pl.BlockSpec((K, tn), lambda i, j: (0, j))],
        out_specs=pl.BlockSpec((tm, tn), lambda i, j: (i, j)),
        out_shape=jax.ShapeDtypeStruct((M, N), BF16),
        compiler_params=_cparams("parallel", "arbitrary"),
        name="ffn_up",
    )(xn, w_gate, w_up)


def _mm_res_body(a_ref, w_ref, r_ref, o_ref):
    o_ref[...] = r_ref[...] + jnp.dot(a_ref[...], w_ref[...], preferred_element_type=F32)


def matmul_residual(a, w, res, *, tm, tn):
    M, K = a.shape
    N = w.shape[1]
    return pl.pallas_call(
        _mm_res_body,
        grid=(M // tm, N // tn),
        in_specs=[pl.BlockSpec((tm, K), lambda i, j: (i, 0)),
                  pl.BlockSpec((K, tn), lambda i, j: (0, j)),
                  pl.BlockSpec((tm, tn), lambda i, j: (i, j))],
        out_specs=pl.BlockSpec((tm, tn), lambda i, j: (i, j)),
        out_shape=jax.ShapeDtypeStruct((M, N), F32),
        compiler_params=_cparams("parallel", "arbitrary"),
        name="matmul_residual",
    )(a, w, res)


def _mm2_res_body(a_ref, b_ref, wa_ref, wb_ref, r_ref, o_ref):
    acc = jnp.dot(a_ref[...], wa_ref[...].astype(BF16), preferred_element_type=F32)
    acc = acc + jnp.dot(b_ref[...], wb_ref[...].astype(BF16), preferred_element_type=F32)
    o_ref[...] = r_ref[...] + acc


def concat_matmul_residual(a, b, w, res, *, tm, tn):
    M, Ka = a.shape
    Kb = b.shape[1]
    assert Ka == Kb and w.shape[0] == Ka + Kb
    N = w.shape[1]
    return pl.pallas_call(
        _mm2_res_body,
        grid=(M // tm, N // tn),
        in_specs=[pl.BlockSpec((tm, Ka), lambda i, j: (i, 0)),
                  pl.BlockSpec((tm, Kb), lambda i, j: (i, 0)),
                  pl.BlockSpec((Ka, tn), lambda i, j: (0, j)),
                  pl.BlockSpec((Kb, tn), lambda i, j: (1, j)),
                  pl.BlockSpec((tm, tn), lambda i, j: (i, j))],
        out_specs=pl.BlockSpec((tm, tn), lambda i, j: (i, j)),
        out_shape=jax.ShapeDtypeStruct((M, N), F32),
        compiler_params=_cparams("parallel", "arbitrary"),
        name="out_proj",
    )(a, b, w, w, res)


AUG_WIDTH = LANES
SLOPE_PIECES = 4
LOG2E = float(np.log2(np.e))


def _bf16_pieces(x, n):
    out, rem = [], np.asarray(x, np.float64)
    for _ in range(n):
        piece = rem.astype(ml_dtypes.bfloat16).astype(np.float64)
        out.append(piece.astype(np.float32))
        rem = rem - piece
    return out


def _position_lanes(pos_hi, pos_lo, onehot, n_blk, shape):
    lane = lax.broadcasted_iota(jnp.int32, shape, 1)
    p = SLOPE_PIECES
    extra = jnp.where((lane >= n_blk) & (lane < n_blk + p), pos_hi,
                      jnp.where((lane >= n_blk + p) & (lane < n_blk + 2 * p), pos_lo, 0.0))
    if onehot is not None:
        extra = jnp.where(lane < n_blk, onehot, extra)
    return extra.astype(BF16)


def _kv_prep_body(c_ref, s_ref, w_ref, kns_ref, knw_ref,
                  kc_o, vc_o, ks_o, vsT_o, kw_o, vwT_o, *, tt, n_blk):
    dh = HEAD_DIM
    shape = (tt, AUG_WIDTH)
    pos = pl.program_id(1) * tt + lax.broadcasted_iota(jnp.int32, shape, 0)
    lane = lax.broadcasted_iota(jnp.int32, shape, 1)
    blk = jnp.right_shift(pos, SEL_BLK.bit_length() - 1)
    pos_hi = (blk * SEL_BLK).astype(F32)
    pos_lo = (pos & (SEL_BLK - 1)).astype(F32)
    aug_sel = _position_lanes(pos_hi, pos_lo, (blk == lane).astype(F32), n_blk, shape)
    aug_win = _position_lanes(pos_hi, pos_lo, None, n_blk, shape)
    for g in range(NSA_KV_HEADS):
        k_cols = slice(g * dh, (g + 1) * dh)
        v_cols = slice(KV_WIDTH + g * dh, KV_WIDTH + (g + 1) * dh)
        kc_o[0, g] = c_ref[:, k_cols]
        vc_o[0, g] = c_ref[:, v_cols]
        ks_o[0, g, :, 0:dh] = _rms(s_ref[:, k_cols], kns_ref[...]).astype(BF16)
        ks_o[0, g, :, dh:dh + AUG_WIDTH] = aug_sel
        kw_o[0, g, :, 0:dh] = _rms(w_ref[:, k_cols], knw_ref[...]).astype(BF16)
        kw_o[0, g, :, dh:dh + AUG_WIDTH] = aug_win
        vsT = s_ref[:, v_cols].T.astype(BF16)
        for c in range(tt // SEL_CHUNK):
            vsT_o[0, g, c] = vsT[:, c * SEL_CHUNK:(c + 1) * SEL_CHUNK]
        vwT = w_ref[:, v_cols].T.astype(BF16)
        for c in range(tt // WIN_CHUNK):
            vwT_o[0, g, c] = vwT[:, c * WIN_CHUNK:(c + 1) * WIN_CHUNK]


def kv_prep(z, kn_slc, kn_win, *, B, T, col0, tt=256):
    G, dh = NSA_KV_HEADS, HEAD_DIM
    nt = T // tt
    n_blk = T // SEL_BLK
    assert n_blk + 2 * SLOPE_PIECES <= AUG_WIDTH
    cb = col0 // (2 * KV_WIDTH)
    row = lambda b, t: b * nt + t
    kdim = dh + AUG_WIDTH
    return pl.pallas_call(
        functools.partial(_kv_prep_body, tt=tt, n_blk=n_blk),
        grid=(B, nt),
        in_specs=[pl.BlockSpec((tt, 2 * KV_WIDTH), lambda b, t: (row(b, t), cb)),
                  pl.BlockSpec((tt, 2 * KV_WIDTH), lambda b, t: (row(b, t), cb + 1)),
                  pl.BlockSpec((tt, 2 * KV_WIDTH), lambda b, t: (row(b, t), cb + 2)),
                  pl.BlockSpec((1, dh), lambda b, t: (0, 0)),
                  pl.BlockSpec((1, dh), lambda b, t: (0, 0))],
        out_specs=[pl.BlockSpec((1, G, tt, dh), lambda b, t: (b, 0, t, 0)),
                   pl.BlockSpec((1, G, tt, dh), lambda b, t: (b, 0, t, 0)),
                   pl.BlockSpec((1, G, tt, kdim), lambda b, t: (b, 0, t, 0)),
                   pl.BlockSpec((1, G, tt // SEL_CHUNK, dh, SEL_CHUNK), lambda b, t: (b, 0, t, 0, 0)),
                   pl.BlockSpec((1, G, tt, kdim), lambda b, t: (b, 0, t, 0)),
                   pl.BlockSpec((1, G, tt // WIN_CHUNK, dh, WIN_CHUNK), lambda b, t: (b, 0, t, 0, 0))],
        out_shape=[jax.ShapeDtypeStruct((B, G, T, dh), F32),
                   jax.ShapeDtypeStruct((B, G, T, dh), F32),
                   jax.ShapeDtypeStruct((B, G, T, kdim), BF16),
                   jax.ShapeDtypeStruct((B, G, T // SEL_CHUNK, dh, SEL_CHUNK), BF16),
                   jax.ShapeDtypeStruct((B, G, T, kdim), BF16),
                   jax.ShapeDtypeStruct((B, G, T // WIN_CHUNK, dh, WIN_CHUNK), BF16)],
        compiler_params=_cparams("parallel", "parallel"),
        name="kv_prep",
    )(z, z, z, kn_slc, kn_win)


def _compress_body(xk_ref, xv_ref, pk_ref, pv_ref, w1k_ref, w2k_ref, w1v_ref, w2v_ref, kn_ref,
                   kc_o, vcT_o, *, n_blk):
    dh = HEAD_DIM

    def phi(x_ref, p_ref, w1_ref, w2_ref):
        x = x_ref[0, 0]
        first = jnp.dot((x + p_ref[0:1, :]).astype(BF16), w1_ref[0], preferred_element_type=F32)
        second = jnp.dot((x + p_ref[1:2, :]).astype(BF16), w1_ref[1], preferred_element_type=F32)
        n = second.shape[0]
        h = first + pltpu.roll(second, n - 1, 0)
        return jnp.dot(_silu(h).astype(BF16), w2_ref[...], preferred_element_type=F32)

    k = _rms(phi(xk_ref, pk_ref, w1k_ref, w2k_ref), kn_ref[...])
    nc = k.shape[0]
    shape = (nc, AUG_WIDTH)
    start = (lax.broadcasted_iota(jnp.int32, shape, 0) * CMP_STRIDE).astype(F32)
    centre = jnp.full(shape, (CMP_LEN - 1) / 2.0, F32)
    kc_o[0, 0, :, 0:dh] = k.astype(BF16)
    kc_o[0, 0, :, dh:dh + AUG_WIDTH] = _position_lanes(start, centre, None, n_blk, shape)
    vcT_o[0, 0] = phi(xv_ref, pv_ref, w1v_ref, w2v_ref).T.astype(BF16)


def compress_kv(kc, vc, pos_k, pos_v, w1k, w2k, w1v, w2v, kn_cmp, *, n_blk):
    B, G, T, dh = kc.shape
    assert CMP_LEN == 2 * CMP_STRIDE
    nc = T // CMP_STRIDE
    half = CMP_STRIDE * dh
    xk = kc.reshape(B, G, nc, half)
    xv = vc.reshape(B, G, nc, half)
    x_spec = pl.BlockSpec((1, 1, nc, half), lambda b, g: (b, g, 0, 0))
    full = lambda shape: pl.BlockSpec(shape, lambda b, g: (0,) * len(shape))
    return pl.pallas_call(
        functools.partial(_compress_body, n_blk=n_blk),
        grid=(B, G),
        in_specs=[x_spec, x_spec, full((2, half)), full((2, half)),
                  full((2, half, dh)), full((dh, dh)), full((2, half, dh)), full((dh, dh)),
                  full((1, dh))],
        out_specs=[pl.BlockSpec((1, 1, nc, dh + AUG_WIDTH), lambda b, g: (b, g, 0, 0)),
                   pl.BlockSpec((1, 1, dh, nc), lambda b, g: (b, g, 0, 0))],
        out_shape=[jax.ShapeDtypeStruct((B, G, nc, dh + AUG_WIDTH), BF16),
                   jax.ShapeDtypeStruct((B, G, dh, nc), BF16)],
        compiler_params=_cparams("parallel", "parallel"),
        name="compress_kv",
    )(xk, xv, pos_k.reshape(2, half), pos_v.reshape(2, half),
      w1k.reshape(2, half, dh), w2k, w1v.reshape(2, half, dh), w2v, kn_cmp)


def _nsa_body(q_ref, gl_ref, gb_ref, qn_ref, lanes_ref, kc_ref, vcT_ref,
              ks_ref, vsT_ref, kw_ref, vwT_ref, ovT_ref, kmq_ref, o_ref,
              score_ref, gate_ref, qc_ref, qs_ref, m_ref, l_ref, acc_ref, part_ref):
    dh, hg, tq = HEAD_DIM, NSA_GROUP, Q_TILE
    cols = hg * tq
    qi = pl.program_id(2)
    t0 = qi * tq
    contract_last = (((1,), (1,)), ((), ()))
    n_blk = ovT_ref.shape[0]

    scale = dh ** -0.5 * LOG2E
    q_heads = [(_rms(q_ref[:, h * dh:(h + 1) * dh], qn_ref[...]) * scale).astype(BF16)
               for h in range(hg)]

    def stack_queries(aug_heads):
        return jnp.concatenate([jnp.concatenate([q_heads[h], aug_heads[h]], axis=1)
                                for h in range(hg)], axis=0)

    hc = kmq_ref.shape[1]
    assert cols % hc == 0 and hc % tq == 0

    def reset(b):
        m_ref[b] = jnp.full((1, cols), NEG_INF, F32)
        l_ref[b] = jnp.zeros((1, cols), F32)
        acc_ref[b] = jnp.zeros((dh, cols), F32)

    def attend(b, queries_ref, parts):
        groups = [slice(i * hc, (i + 1) * hc) for i in range(cols // hc)]
        scores = [[lax.dot_general(k, queries_ref[cs, :], contract_last, preferred_element_type=F32)
                   for cs in groups] for k, _, _, _ in parts]
        for (k, v_pieces, mask, bias), part_scores in zip(parts, scores):
            for cs, s in zip(groups, part_scores):
                if mask is not None:
                    s = jnp.where(mask(), s, NEG_INF)
                if bias is not None:
                    s = s + bias
                m_old = m_ref[b, :, cs]
                m_new = jnp.maximum(m_old, jnp.max(s, axis=0, keepdims=True))
                alpha = jnp.exp2(m_old - m_new)
                p = jnp.exp2(s - m_new)
                m_ref[b, :, cs] = m_new
                l_ref[b, :, cs] = alpha * l_ref[b, :, cs] + jnp.sum(p, axis=0, keepdims=True)
                p = p.astype(BF16)
                pv = None
                for i, vT in enumerate(v_pieces):
                    n = vT.shape[1]
                    d = jnp.dot(vT, p[i * n:(i + 1) * n, :], preferred_element_type=F32)
                    pv = d if pv is None else pv + d
                acc_ref[b, :, cs] = alpha * acc_ref[b, :, cs] + pv

    def result(b):
        return acc_ref[b] * (1.0 / l_ref[b])

    nc = kc_ref.shape[2]
    qc_ref[...] = stack_queries([jnp.broadcast_to(lanes_ref[0, hg + h:hg + h + 1, :], (tq, AUG_WIDTH)).astype(BF16)
                                 for h in range(hg)])
    s = lax.dot_general(kc_ref[0, 0], qc_ref[...], contract_last, preferred_element_type=F32)
    c_idx = lax.broadcasted_iota(jnp.int32, (nc, cols), 0)
    t_c = t0 + (lax.broadcasted_iota(jnp.int32, (nc, cols), 1) & (tq - 1))
    mask = (c_idx * CMP_STRIDE + (CMP_LEN - 1)) <= t_c
    s = jnp.where(mask, s, NEG_INF)
    m_c = jnp.max(s, axis=0, keepdims=True)
    p_c = jnp.where(mask, jnp.exp2(s - m_c), 0.0)
    l_c = jnp.sum(p_c, axis=0, keepdims=True)
    p_c = p_c * jnp.where(l_c > 0.0, 1.0 / l_c, 0.0)
    o_cmp = jnp.dot(vcT_ref[0, 0], p_c.astype(BF16), preferred_element_type=F32)

    cw = WIN_CHUNK
    assert cw == tq and WINDOW % cw == 0 and cw == kmq_ref.shape[0]
    n_back = WINDOW // cw
    WIN, SEL = 0, 1

    def win_part(c, mask=None, bias=None):
        k = kw_ref[0, 0, pl.ds(pl.multiple_of(c * cw, cw), cw), :]
        return k, [vwT_ref[0, 0, c]], mask, bias

    threshold = jnp.where(qi >= n_back, 0, cw)
    reset(WIN)
    attend(WIN, qc_ref,
           [win_part(qi, mask=lambda: kmq_ref[...] <= 0)]
           + [win_part(jnp.maximum(qi - j, 0), bias=jnp.where(qi >= j, 0.0, NEG_INF))
              for j in range(1, n_back)]
           + [win_part(jnp.maximum(qi - n_back, 0), mask=lambda: kmq_ref[...] > threshold)])

    gate_ref[...] = jax.nn.sigmoid(gl_ref[...] + gb_ref[...]).T

    def gate(h, branch):
        r = h * N_BRANCH + branch
        return gate_ref[r:r + 1, :]

    o_win = result(WIN)
    for h in range(hg):
        cs = slice(h * tq, (h + 1) * tq)
        part_ref[:, cs] = gate(h, 0) * o_cmp[:, cs] + gate(h, 2) * o_win[:, cs]

    p_sum = p_c[:, 0:tq]
    for h in range(1, hg):
        p_sum = p_sum + p_c[:, h * tq:(h + 1) * tq]
    hi = p_sum.astype(BF16)
    rem = p_sum - hi.astype(F32)
    mid = rem.astype(BF16)
    lo = (rem - mid.astype(F32)).astype(BF16)
    ovT = ovT_ref[...]
    imp = (jnp.dot(ovT, hi, preferred_element_type=F32)
           + jnp.dot(ovT, mid, preferred_element_type=F32)
           + jnp.dot(ovT, lo, preferred_element_type=F32))
    blk = lax.broadcasted_iota(jnp.int32, (n_blk, tq), 0)
    t_b = t0 + lax.broadcasted_iota(jnp.int32, (n_blk, tq), 1)
    cur = jnp.right_shift(t_b, SEL_BLK.bit_length() - 1)
    valid = blk * SEL_BLK <= t_b
    forced = (blk == 0) | (blk == cur) | (blk == cur - 1)
    score = jnp.where(valid, imp + FORCE_BONUS * forced.astype(F32), NEG_INF)
    score_ref[...] = score
    rank = jnp.zeros((n_blk, tq), F32)
    for j in range(n_blk):
        sj = score_ref[j:j + 1, :]
        beats = (sj > score) | ((sj == score) & (blk > j))
        rank = rank + beats.astype(F32)
    block_bias = jnp.where(rank < float(min(N_SEL, n_blk)), 0.0, NEG_INF)

    assert 2 * SLOPE_PIECES <= 8
    ones_rows = (lax.broadcasted_iota(jnp.int32, (8, tq), 0) < 2 * SLOPE_PIECES).astype(F32)
    extra_t = jnp.concatenate([block_bias, ones_rows,
                               jnp.zeros((AUG_WIDTH - n_blk - 8, tq), F32)], axis=0).T
    qs_ref[...] = stack_queries([(extra_t * lanes_ref[0, h:h + 1, :]).astype(BF16) for h in range(hg)])

    ck = SEL_CHUNK
    assert ck == tq and ck == kmq_ref.shape[0]

    def sel_part(c, n=1, mask=None, bias=None):
        k = ks_ref[0, 0, pl.ds(pl.multiple_of(c * ck, ck), n * ck), :]
        return k, [vsT_ref[0, 0, c + i] for i in range(n)], mask, bias

    reset(SEL)

    @pl.loop(0, qi // 2)
    def _(i):
        attend(SEL, qs_ref, [sel_part(2 * i, n=2)])

    odd = qi % 2 == 1
    attend(SEL, qs_ref,
           [sel_part(qi, mask=lambda: kmq_ref[...] <= 0),
            sel_part(jnp.maximum(qi - 1, 0), bias=jnp.where(odd, 0.0, NEG_INF))])

    o_slc = result(SEL)
    for h in range(hg):
        cs = slice(h * tq, (h + 1) * tq)
        o_h = part_ref[:, cs] + gate(h, 1) * o_slc[:, cs]
        o_ref[:, h * dh:(h + 1) * dh] = o_h.T.astype(o_ref.dtype)


def nsa_attention(z, gl, gate_b, q_norm, kcmp, vcmpT, ks, vsT, kw, vwT, *, B, T):
    G, hg, dh, tq = NSA_KV_HEADS, NSA_GROUP, HEAD_DIM, Q_TILE
    nq = T // tq
    nc = kcmp.shape[2]
    n_blk = T // SEL_BLK
    kdim = dh + AUG_WIDTH
    cols = hg * tq
    assert SEL_CHUNK == WIN_CHUNK == tq
    assert SEL_CHUNK % SEL_BLK == 0 and T % SEL_CHUNK == 0 and gl.shape[1] == G * LANES
    assert SEL_CHUNK % tq == 0 and n_blk % 8 == 0 and n_blk + 8 <= AUG_WIDTH

    slopes = np.exp2(-8.0 * (np.arange(NSA_HEADS, dtype=np.float64) + 1.0) / NSA_HEADS).reshape(G, hg)
    lanes = np.zeros((G, 2 * hg, AUG_WIDTH), np.float32)
    lanes[:, :hg, :n_blk] = 1.0
    for i, piece in enumerate(_bf16_pieces(slopes * LOG2E, SLOPE_PIECES)):
        for lane in (n_blk + i, n_blk + SLOPE_PIECES + i):
            lanes[:, :hg, lane] = piece
            lanes[:, hg:, lane] = piece
    cs = np.arange(nc)[:, None] * CMP_STRIDE
    bs = np.arange(n_blk)[None, :] * SEL_BLK
    overlap = np.clip(np.minimum(cs + CMP_LEN, bs + SEL_BLK) - np.maximum(cs, bs), 0, None) / CMP_LEN
    overlap[(T - CMP_LEN) // CMP_STRIDE + 1:] = 0.0
    ovT = jnp.asarray(overlap.T, dtype=BF16)
    hc = NSA_COLS_PER_DOT
    kmq = jnp.asarray(np.arange(SEL_CHUNK)[:, None] - (np.arange(hc)[None, :] % tq), dtype=jnp.int32)

    row = lambda b, qi: b * nq + qi
    per_bg = lambda shape: pl.BlockSpec((1, 1) + shape, lambda b, g, qi: (b, g) + (0,) * len(shape))
    const = lambda shape: pl.BlockSpec(shape, lambda b, g, qi: (0,) * len(shape))
    return pl.pallas_call(
        _nsa_body,
        grid=(B, G, nq),
        in_specs=[pl.BlockSpec((tq, hg * dh), lambda b, g, qi: (row(b, qi), g)),
                  pl.BlockSpec((tq, LANES), lambda b, g, qi: (row(b, qi), g)),
                  pl.BlockSpec((1, LANES), lambda b, g, qi: (0, g)),
                  const((1, dh)),
                  pl.BlockSpec((1, 2 * hg, AUG_WIDTH), lambda b, g, qi: (g, 0, 0)),
                  per_bg((nc, kdim)), per_bg((dh, nc)),
                  per_bg((T, kdim)), per_bg((T // SEL_CHUNK, dh, SEL_CHUNK)),
                  per_bg((T, kdim)), per_bg((T // WIN_CHUNK, dh, WIN_CHUNK)),
                  const((n_blk, nc)), const((SEL_CHUNK, hc))],
        out_specs=pl.BlockSpec((tq, hg * dh), lambda b, g, qi: (row(b, qi), g)),
        out_shape=jax.ShapeDtypeStruct((B * T, NSA_WIDTH), BF16),
        scratch_shapes=[pltpu.VMEM((n_blk, tq), F32), pltpu.VMEM((LANES, tq), F32),
                        pltpu.VMEM((cols, kdim), BF16), pltpu.VMEM((cols, kdim), BF16),
                        pltpu.VMEM((2, 1, cols), F32), pltpu.VMEM((2, 1, cols), F32),
                        pltpu.VMEM((2, dh, cols), F32), pltpu.VMEM((dh, cols), F32)],
        compiler_params=_cparams("parallel", "parallel", "arbitrary"),
        name="nsa_attention",
    )(z, gl, gate_b, q_norm, jnp.asarray(lanes), kcmp, vcmpT, ks, vsT, kw, vwT, ovT, kmq)


def _conv_body(a_ref, b_ref, ah_ref, bh_ref, w_ref, cb_ref, lg_ref, lb_ref, o_ref,
               h_ref, g_ref, y_ref, *, tt):
    C = a_ref.shape[1]
    halo = ah_ref[...] * jax.nn.sigmoid(bh_ref[...])
    h_ref[0:CONV_HALO, :] = jnp.where(pl.program_id(1) == 0, 0.0, halo)
    h_ref[CONV_HALO:CONV_HALO + tt, :] = a_ref[...] * jax.nn.sigmoid(b_ref[...])
    h_ref[CONV_HALO + tt:CONV_HALO + tt + SUBLANES, :] = jnp.zeros((SUBLANES, C), F32)
    lead = CONV_HALO - (CONV_KERNEL - 1)
    rows = tt + SUBLANES
    for s in range(C // LANES):
        cs = slice(s * LANES, (s + 1) * LANES)
        acc = None
        for r in range(SUBLANES):
            group = None
            for o in range(lead, lead + CONV_KERNEL):
                if o % SUBLANES == r:
                    term = w_ref[o - lead:o - lead + 1, cs] * h_ref[o - r:o - r + rows, cs]
                    group = term if group is None else group + term
            if r == 0:
                piece = group[0:tt, :]
            else:
                g_ref[r] = group
                piece = g_ref[r, r:r + tt, :]
            acc = piece if acc is None else acc + piece
        y_ref[:, cs] = acc + cb_ref[:, cs]
    y = y_ref[...]
    mu = jnp.mean(y, axis=-1, keepdims=True)
    var = jnp.mean(jnp.square(y - mu), axis=-1, keepdims=True)
    yn = (y - mu) * lax.rsqrt(var + LN_EPS) * lg_ref[...] + lb_ref[...]
    o_ref[...] = _silu(yn).astype(o_ref.dtype)


def conformer_conv(z, conv_w, conv_b, ln_g, ln_b, *, B, T, col0, C, tt=128):
    nt = T // tt
    ca = col0 // C
    hb = tt // CONV_HALO
    row = lambda b, t: b * nt + t
    halo_row = lambda b, t: jnp.maximum(row(b, t) * hb - 1, 0)
    const = lambda shape: pl.BlockSpec(shape, lambda b, t: (0,) * len(shape))
    return pl.pallas_call(
        functools.partial(_conv_body, tt=tt),
        grid=(B, nt),
        in_specs=[pl.BlockSpec((tt, C), lambda b, t: (row(b, t), ca)),
                  pl.BlockSpec((tt, C), lambda b, t: (row(b, t), ca + 1)),
                  pl.BlockSpec((CONV_HALO, C), lambda b, t: (halo_row(b, t), ca)),
                  pl.BlockSpec((CONV_HALO, C), lambda b, t: (halo_row(b, t), ca + 1)),
                  const((CONV_KERNEL, C)), const((1, C)), const((1, C)), const((1, C))],
        out_specs=pl.BlockSpec((tt, C), lambda b, t: (row(b, t), 0)),
        out_shape=jax.ShapeDtypeStruct((B * T, C), BF16),
        scratch_shapes=[pltpu.VMEM((CONV_HALO + tt + SUBLANES, C), F32),
                        pltpu.VMEM((SUBLANES, tt + SUBLANES, LANES), F32),
                        pltpu.VMEM((tt, C), F32)],
        compiler_params=_cparams("parallel", "parallel"),
        name="conformer_conv",
    )(z, z, z, z, conv_w, conv_b, ln_g, ln_b)


def _mem_attn_body(x_ref, gq_ref, wq_ref, kv_ref, qn_ref, kn_ref, wo_ref, gn_ref, o_ref, on_ref):
    dh = HEAD_DIM
    width = MEM_HEADS * dh
    scale = dh ** -0.5
    q_all = jnp.dot(_rms(x_ref[...], gq_ref[...]).astype(BF16), wq_ref[...], preferred_element_type=F32)
    outs = []
    for h in range(MEM_HEADS):
        cs = slice(h * dh, (h + 1) * dh)
        q = (_rms(q_all[:, cs], qn_ref[...]) * scale).astype(BF16)
        k = _rms(kv_ref[:, cs], kn_ref[...]).astype(BF16)
        v = kv_ref[:, width + h * dh:width + (h + 1) * dh].astype(BF16)
        s = lax.dot_general(q, k, (((1,), (1,)), ((), ())), preferred_element_type=F32)
        m = jnp.max(s, axis=-1, keepdims=True)
        p = jnp.exp(s - m)
        p = p / jnp.sum(p, axis=-1, keepdims=True)
        outs.append(jnp.dot(p.astype(BF16), v, preferred_element_type=F32).astype(BF16))
    o = jnp.concatenate(outs, axis=1)
    y = x_ref[...] + jnp.dot(o, wo_ref[...], preferred_element_type=F32)
    o_ref[...] = y
    on_ref[...] = _rms(y, gn_ref[...]).astype(on_ref.dtype)


def mem_attention(x, q_gain, w_mq, kvm, mq_norm, mk_norm, w_mo, next_norm, *, B, T, M, tt=256):
    D = x.shape[1]
    width = MEM_HEADS * HEAD_DIM
    nt = T // tt
    row = lambda b, t: b * nt + t
    const = lambda shape: pl.BlockSpec(shape, lambda b, t: (0,) * len(shape))
    return pl.pallas_call(
        _mem_attn_body,
        grid=(B, nt),
        in_specs=[pl.BlockSpec((tt, D), lambda b, t: (row(b, t), 0)),
                  const((1, D)), const((D, width)),
                  pl.BlockSpec((M, 2 * width), lambda b, t: (b, 0)),
                  const((1, HEAD_DIM)), const((1, HEAD_DIM)), const((width, D)),
                  const((1, D))],
        out_specs=[pl.BlockSpec((tt, D), lambda b, t: (row(b, t), 0)),
                   pl.BlockSpec((tt, D), lambda b, t: (row(b, t), 0))],
        out_shape=[jax.ShapeDtypeStruct((B * T, D), F32),
                   jax.ShapeDtypeStruct((B * T, D), BF16)],
        compiler_params=_cparams("parallel", "parallel"),
        name="mem_attention",
    )(x, q_gain, w_mq, kvm, mq_norm, mk_norm, w_mo, next_norm)


def kernel(x, mem, norm_mix, w_in, gate_b, q_norm, k_norm_cmp, k_norm_slc, k_norm_win, cmp_pos_k, cmp_pos_v, cmp_k_w1, cmp_k_w2, cmp_v_w1, cmp_v_w2, conv_w, conv_b, conv_ln_g, conv_ln_b, w_out, norm_mem_q, norm_mem_kv, w_mq, w_mk, w_mv, mq_norm, mk_norm, w_mo, norm_ffn, w_gate, w_up, w_down):
    B, T, D = x.shape
    M = mem.shape[1]
    depth = w_in.shape[0]
    G, hg = NSA_KV_HEADS, NSA_GROUP
    C = D - NSA_WIDTH
    n_gate = N_BRANCH * NSA_HEADS
    kv_all = 6 * KV_WIDTH
    assert w_in.shape[2] == NSA_WIDTH + kv_all + n_gate + 2 * C
    assert C == NSA_WIDTH, "column blocking below assumes equal NSA and conv widths"

    row1 = lambda v: v.reshape(1, -1)
    xf = x.reshape(B * T, D)
    memf = mem.reshape(B * M, D)
    u0 = NSA_WIDTH + kv_all + n_gate
    for l in range(depth):
        w = w_in[l]
        w_glu = w[:, u0:]
        wg = w[:, NSA_WIDTH + kv_all:u0].reshape(D, G, hg * N_BRANCH)
        wg = jnp.pad(wg, ((0, 0), (0, 0), (0, LANES - hg * N_BRANCH))).reshape(D, G * LANES)
        gb = jnp.pad(gate_b[l].reshape(G, hg * N_BRANCH), ((0, 0), (0, LANES - hg * N_BRANCH))).reshape(1, G * LANES)

        z, gl = in_projection(xf, row1(norm_mix[l]), w, w_glu, wg, n_q=NSA_WIDTH, n_kv=kv_all, tm=1024, tn=512)
        kv_col0 = NSA_WIDTH + 2 * C
        kc, vc, ks, vsT, kw, vwT = kv_prep(z, row1(k_norm_slc[l]), row1(k_norm_win[l]), B=B, T=T, col0=kv_col0)
        kcmp, vcmpT = compress_kv(kc, vc, cmp_pos_k[l], cmp_pos_v[l],
                                  cmp_k_w1[l].astype(BF16), cmp_k_w2[l].astype(BF16),
                                  cmp_v_w1[l].astype(BF16), cmp_v_w2[l].astype(BF16), row1(k_norm_cmp[l]),
                                  n_blk=T // SEL_BLK)
        o_nsa = nsa_attention(z, gl, gb, row1(q_norm[l]), kcmp, vcmpT, ks, vsT, kw, vwT, B=B, T=T)
        o_conv = conformer_conv(z, conv_w[l], row1(conv_b[l]), row1(conv_ln_g[l]), row1(conv_ln_b[l]),
                                B=B, T=T, col0=NSA_WIDTH, C=C)
        xf = concat_matmul_residual(o_nsa, o_conv, w_out[l], xf, tm=1024, tn=512)

        w_mkv = jnp.concatenate([w_mk[l], w_mv[l]], axis=1).astype(BF16)
        kvm = norm_matmul(memf, row1(norm_mem_kv[l]), w_mkv, tm=B * M, tn=512)
        xf, xn = mem_attention(xf, row1(norm_mem_q[l]), w_mq[l].astype(BF16), kvm,
                               row1(mq_norm[l]), row1(mk_norm[l]), w_mo[l].astype(BF16),
                               row1(norm_ffn[l]), B=B, T=T, M=M)

        hidden = swiglu_up(xn, w_gate[l], w_up[l], tm=1024, tn=512)
        xf = matmul_residual(hidden, w_down[l].astype(BF16), xf, tm=1024, tn=512)
    return xf.reshape(B, T, D)
```

```python
import functools

import ml_dtypes
import numpy as np
import jax
import jax.numpy as jnp
from jax import lax
from jax.experimental import pallas as pl
from jax.experimental.pallas import tpu as pltpu

F32 = jnp.float32
BF16 = jnp.bfloat16

HEAD_DIM = 128
NSA_HEADS = 8
NSA_KV_HEADS = 2
NSA_GROUP = NSA_HEADS // NSA_KV_HEADS
NSA_WIDTH = NSA_HEADS * HEAD_DIM
KV_WIDTH = NSA_KV_HEADS * HEAD_DIM
N_BRANCH = 3
CONV_KERNEL = 31
CMP_LEN = 32
CMP_STRIDE = 16
SEL_BLK = 64
N_SEL = 16
WINDOW = 512
MEM_HEADS = 4
NEG_INF = -1e30
FORCE_BONUS = 1e3
RMS_EPS = 1e-6
LN_EPS = 1e-5

LANES = 128
SUBLANES = 8
VMEM_LIMIT_BYTES = 56 * 1024 * 1024

Q_TILE = 256
SEL_CHUNK = 256
WIN_CHUNK = 256
NSA_COLS_PER_DOT = 512
CONV_HALO = 32


def _cparams(*sem):
    return pltpu.CompilerParams(dimension_semantics=sem, vmem_limit_bytes=VMEM_LIMIT_BYTES)


def _rms(x, g, eps=RMS_EPS):
    ms = jnp.mean(x * x, axis=-1, keepdims=True)
    return x * lax.rsqrt(ms + eps) * g


def _silu(x):
    return x * jax.nn.sigmoid(x)


def _norm_to_scratch(x_ref, g_ref, xn_ref):
    @pl.when(pl.program_id(1) == 0)
    def _():
        xn_ref[...] = _rms(x_ref[...], g_ref[...]).astype(xn_ref.dtype)


def _norm_mm_body(x_ref, g_ref, w_ref, o_ref, xn_ref):
    _norm_to_scratch(x_ref, g_ref, xn_ref)
    o_ref[...] = jnp.dot(xn_ref[...], w_ref[...], preferred_element_type=F32).astype(o_ref.dtype)


def _in_proj_body(x_ref, g_ref, wt_ref, wg_ref, o_ref, og_ref, xn_ref):
    _norm_to_scratch(x_ref, g_ref, xn_ref)

    @pl.when(pl.program_id(1) == 0)
    def _():
        og_ref[...] = jnp.dot(xn_ref[...], wg_ref[...].astype(BF16), preferred_element_type=F32)

    o_ref[...] = lax.dot_general(xn_ref[...], wt_ref[...].astype(BF16), (((1,), (1,)), ((), ())),
                                 preferred_element_type=F32)


def _swiglu_body(x_ref, wg_ref, wu_ref, o_ref):
    x = x_ref[...]
    a = jnp.dot(x, wg_ref[...].astype(BF16), preferred_element_type=F32)
    b = jnp.dot(x, wu_ref[...].astype(BF16), preferred_element_type=F32)
    o_ref[...] = (_silu(a) * b).astype(o_ref.dtype)


def norm_matmul(x, g, w, *, tm, tn, out_dtype=F32):
    M, K = x.shape
    N = w.shape[1]
    return pl.pallas_call(
        _norm_mm_body,
        grid=(M // tm, N // tn),
        in_specs=[pl.BlockSpec((tm, K), lambda i, j: (i, 0)),
                  pl.BlockSpec((1, K), lambda i, j: (0, 0)),
                  pl.BlockSpec((K, tn), lambda i, j: (0, j))],
        out_specs=pl.BlockSpec((tm, tn), lambda i, j: (i, j)),
        out_shape=jax.ShapeDtypeStruct((M, N), out_dtype),
        scratch_shapes=[pltpu.VMEM((tm, K), BF16)],
        compiler_params=_cparams("parallel", "arbitrary"),
        name="norm_matmul",
    )(x, g, w)


def in_projection(x, g, w_t, wg, *, segments, tm, tn):
    M, K = x.shape
    NG = wg.shape[1]
    assert all(start % SUBLANES == 0 and size % tn == 0 for start, size in segments)
    N = sum(size for _, size in segments)

    def row_start(i, j):
        group, first = 0, 0
        for seg_start, size in segments:
            group = jnp.where(j >= first, seg_start // SUBLANES + (j - first) * (tn // SUBLANES), group)
            first += size // tn
        return (group * SUBLANES, 0)

    return pl.pallas_call(
        _in_proj_body,
        grid=(M // tm, N // tn),
        in_specs=[pl.BlockSpec((tm, K), lambda i, j: (i, 0)),
                  pl.BlockSpec((1, K), lambda i, j: (0, 0)),
                  pl.BlockSpec((pl.Element(tn), pl.Element(K)), row_start),
                  pl.BlockSpec((K, NG), lambda i, j: (0, 0))],
        out_specs=[pl.BlockSpec((tm, tn), lambda i, j: (i, j)),
                   pl.BlockSpec((tm, NG), lambda i, j: (i, 0))],
        out_shape=[jax.ShapeDtypeStruct((M, N), F32),
                   jax.ShapeDtypeStruct((M, NG), F32)],
        scratch_shapes=[pltpu.VMEM((tm, K), BF16)],
        compiler_params=_cparams("parallel", "arbitrary"),
        name="in_proj",
    )(x, g, w_t, wg)


def swiglu_up(xn, w_gate, w_up, *, tm, tn):
    M, K = xn.shape
    N = w_gate.shape[1]
    return pl.pallas_call(
        _swiglu_body,
        grid=(M // tm, N // tn),
        in_specs=[pl.BlockSpec((tm, K), lambda i, j: (i, 0)),
                  pl.BlockSpec((K, tn), lambda i, j: (0, j)),
                  pl.BlockSpec((K, tn), lambda i, j: (0, j))],
        out_specs=pl.BlockSpec((tm, tn), lambda i, j: (i, j)),
        out_shape=jax.ShapeDtypeStruct((M, N), BF16),
        compiler_params=_cparams("parallel", "arbitrary"),
        name="ffn_up",
    )(xn, w_gate, w_up)


def _mm_res_body(a_ref, w_ref, r_ref, o_ref):
    o_ref[...] = r_ref[...] + jnp.dot(a_ref[...], w_ref[...], preferred_element_type=F32)


def matmul_residual(a, w, res, *, tm, tn):
    M, K = a.shape
    N = w.shape[1]
    return pl.pallas_call(
        _mm_res_body,
        grid=(M // tm, N // tn),
        in_specs=[pl.BlockSpec((tm, K), lambda i, j: (i, 0)),
                  pl.BlockSpec((K, tn), lambda i, j: (0, j)),
                  pl.BlockSpec((tm, tn), lambda i, j: (i, j))],
        out_specs=pl.BlockSpec((tm, tn), lambda i, j: (i, j)),
        out_shape=jax.ShapeDtypeStruct((M, N), F32),
        compiler_params=_cparams("parallel", "arbitrary"),
        name="matmul_residual",
    )(a, w, res)


def _mm2_res_body(a_ref, b_ref, wa_ref, wb_ref, r_ref, o_ref):
    acc = jnp.dot(a_ref[...], wa_ref[...].astype(BF16), preferred_element_type=F32)
    acc = acc + jnp.dot(b_ref[...], wb_ref[...].astype(BF16), preferred_element_type=F32)
    o_ref[...] = r_ref[...] + acc


def concat_matmul_residual(a, b, w, res, *, tm, tn):
    M, Ka = a.shape
    Kb = b.shape[1]
    assert Ka == Kb and w.shape[0] == Ka + Kb
    N = w.shape[1]
    return pl.pallas_call(
        _mm2_res_body,
        grid=(M // tm, N // tn),
        in_specs=[pl.BlockSpec((tm, Ka), lambda i, j: (i, 0)),
                  pl.BlockSpec((tm, Kb), lambda i, j: (i, 0)),
                  pl.BlockSpec((Ka, tn), lambda i, j: (0, j)),
                  pl.BlockSpec((Kb, tn), lambda i, j: (1, j)),
                  pl.BlockSpec((tm, tn), lambda i, j: (i, j))],
        out_specs=pl.BlockSpec((tm, tn), lambda i, j: (i, j)),
        out_shape=jax.ShapeDtypeStruct((M, N), F32),
        compiler_params=_cparams("parallel", "arbitrary"),
        name="out_proj",
    )(a, b, w, w, res)


AUG_WIDTH = LANES
SLOPE_PIECES = 4
LOG2E = float(np.log2(np.e))


def _bf16_pieces(x, n):
    out, rem = [], np.asarray(x, np.float64)
    for _ in range(n):
        piece = rem.astype(ml_dtypes.bfloat16).astype(np.float64)
        out.append(piece.astype(np.float32))
        rem = rem - piece
    return out


def _position_lanes(pos_hi, pos_lo, onehot, n_blk, shape):
    lane = lax.broadcasted_iota(jnp.int32, shape, 1)
    p = SLOPE_PIECES
    extra = jnp.where((lane >= n_blk) & (lane < n_blk + p), pos_hi,
                      jnp.where((lane >= n_blk + p) & (lane < n_blk + 2 * p), pos_lo, 0.0))
    if onehot is not None:
        extra = jnp.where(lane < n_blk, onehot, extra)
    return extra.astype(BF16)


def _kv_prep_body(c_ref, s_ref, w_ref, kns_ref, knw_ref,
                  kc_o, vc_o, ks_o, vsT_o, kw_o, vwT_o, *, tt, n_blk):
    dh = HEAD_DIM
    shape = (tt, AUG_WIDTH)
    pos = pl.program_id(1) * tt + lax.broadcasted_iota(jnp.int32, shape, 0)
    lane = lax.broadcasted_iota(jnp.int32, shape, 1)
    blk = jnp.right_shift(pos, SEL_BLK.bit_length() - 1)
    pos_hi = (blk * SEL_BLK).astype(F32)
    pos_lo = (pos & (SEL_BLK - 1)).astype(F32)
    aug_sel = _position_lanes(pos_hi, pos_lo, (blk == lane).astype(F32), n_blk, shape)
    aug_win = _position_lanes(pos_hi, pos_lo, None, n_blk, shape)
    for g in range(NSA_KV_HEADS):
        k_cols = slice(g * dh, (g + 1) * dh)
        v_cols = slice(KV_WIDTH + g * dh, KV_WIDTH + (g + 1) * dh)
        kc_o[0, g] = c_ref[:, k_cols]
        vc_o[0, g] = c_ref[:, v_cols]
        ks_o[0, g, :, 0:dh] = _rms(s_ref[:, k_cols], kns_ref[...]).astype(BF16)
        ks_o[0, g, :, dh:dh + AUG_WIDTH] = aug_sel
        kw_o[0, g, :, 0:dh] = _rms(w_ref[:, k_cols], knw_ref[...]).astype(BF16)
        kw_o[0, g, :, dh:dh + AUG_WIDTH] = aug_win
        vsT = s_ref[:, v_cols].T.astype(BF16)
        for c in range(tt // SEL_CHUNK):
            vsT_o[0, g, c] = vsT[:, c * SEL_CHUNK:(c + 1) * SEL_CHUNK]
        vwT = w_ref[:, v_cols].T.astype(BF16)
        for c in range(tt // WIN_CHUNK):
            vwT_o[0, g, c] = vwT[:, c * WIN_CHUNK:(c + 1) * WIN_CHUNK]


def kv_prep(z, kn_slc, kn_win, *, B, T, col0, tt=256):
    G, dh = NSA_KV_HEADS, HEAD_DIM
    nt = T // tt
    n_blk = T // SEL_BLK
    assert n_blk + 2 * SLOPE_PIECES <= AUG_WIDTH
    cb = col0 // (2 * KV_WIDTH)
    row = lambda b, t: b * nt + t
    kdim = dh + AUG_WIDTH
    return pl.pallas_call(
        functools.partial(_kv_prep_body, tt=tt, n_blk=n_blk),
        grid=(B, nt),
        in_specs=[pl.BlockSpec((tt, 2 * KV_WIDTH), lambda b, t: (row(b, t), cb)),
                  pl.BlockSpec((tt, 2 * KV_WIDTH), lambda b, t: (row(b, t), cb + 1)),
                  pl.BlockSpec((tt, 2 * KV_WIDTH), lambda b, t: (row(b, t), cb + 2)),
                  pl.BlockSpec((1, dh), lambda b, t: (0, 0)),
                  pl.BlockSpec((1, dh), lambda b, t: (0, 0))],
        out_specs=[pl.BlockSpec((1, G, tt, dh), lambda b, t: (b, 0, t, 0)),
                   pl.BlockSpec((1, G, tt, dh), lambda b, t: (b, 0, t, 0)),
                   pl.BlockSpec((1, G, tt, kdim), lambda b, t: (b, 0, t, 0)),
                   pl.BlockSpec((1, G, tt // SEL_CHUNK, dh, SEL_CHUNK), lambda b, t: (b, 0, t, 0, 0)),
                   pl.BlockSpec((1, G, tt, kdim), lambda b, t: (b, 0, t, 0)),
                   pl.BlockSpec((1, G, tt // WIN_CHUNK, dh, WIN_CHUNK), lambda b, t: (b, 0, t, 0, 0))],
        out_shape=[jax.ShapeDtypeStruct((B, G, T, dh), F32),
                   jax.ShapeDtypeStruct((B, G, T, dh), F32),
                   jax.ShapeDtypeStruct((B, G, T, kdim), BF16),
                   jax.ShapeDtypeStruct((B, G, T // SEL_CHUNK, dh, SEL_CHUNK), BF16),
                   jax.ShapeDtypeStruct((B, G, T, kdim), BF16),
                   jax.ShapeDtypeStruct((B, G, T // WIN_CHUNK, dh, WIN_CHUNK), BF16)],
        compiler_params=_cparams("parallel", "parallel"),
        name="kv_prep",
    )(z, z, z, kn_slc, kn_win)


def _compress_body(xk_ref, xv_ref, pk_ref, pv_ref, w1k_ref, w2k_ref, w1v_ref, w2v_ref, kn_ref,
                   kc_o, vcT_o, *, n_blk):
    dh = HEAD_DIM

    def phi(x_ref, p_ref, w1_ref, w2_ref):
        x = x_ref[0, 0]
        first = jnp.dot((x + p_ref[0:1, :]).astype(BF16), w1_ref[0], preferred_element_type=F32)
        second = jnp.dot((x + p_ref[1:2, :]).astype(BF16), w1_ref[1], preferred_element_type=F32)
        n = second.shape[0]
        h = first + pltpu.roll(second, n - 1, 0)
        return jnp.dot(_silu(h).astype(BF16), w2_ref[...], preferred_element_type=F32)

    k = _rms(phi(xk_ref, pk_ref, w1k_ref, w2k_ref), kn_ref[...])
    nc = k.shape[0]
    shape = (nc, AUG_WIDTH)
    start = (lax.broadcasted_iota(jnp.int32, shape, 0) * CMP_STRIDE).astype(F32)
    centre = jnp.full(shape, (CMP_LEN - 1) / 2.0, F32)
    kc_o[0, 0, :, 0:dh] = k.astype(BF16)
    kc_o[0, 0, :, dh:dh + AUG_WIDTH] = _position_lanes(start, centre, None, n_blk, shape)
    vcT_o[0, 0] = phi(xv_ref, pv_ref, w1v_ref, w2v_ref).T.astype(BF16)


def compress_kv(kc, vc, pos_k, pos_v, w1k, w2k, w1v, w2v, kn_cmp, *, n_blk):
    B, G, T, dh = kc.shape
    assert CMP_LEN == 2 * CMP_STRIDE
    nc = T // CMP_STRIDE
    half = CMP_STRIDE * dh
    xk = kc.reshape(B, G, nc, half)
    xv = vc.reshape(B, G, nc, half)
    x_spec = pl.BlockSpec((1, 1, nc, half), lambda b, g: (b, g, 0, 0))
    full = lambda shape: pl.BlockSpec(shape, lambda b, g: (0,) * len(shape))
    return pl.pallas_call(
        functools.partial(_compress_body, n_blk=n_blk),
        grid=(B, G),
        in_specs=[x_spec, x_spec, full((2, half)), full((2, half)),
                  full((2, half, dh)), full((dh, dh)), full((2, half, dh)), full((dh, dh)),
                  full((1, dh))],
        out_specs=[pl.BlockSpec((1, 1, nc, dh + AUG_WIDTH), lambda b, g: (b, g, 0, 0)),
                   pl.BlockSpec((1, 1, dh, nc), lambda b, g: (b, g, 0, 0))],
        out_shape=[jax.ShapeDtypeStruct((B, G, nc, dh + AUG_WIDTH), BF16),
                   jax.ShapeDtypeStruct((B, G, dh, nc), BF16)],
        compiler_params=_cparams("parallel", "parallel"),
        name="compress_kv",
    )(xk, xv, pos_k.reshape(2, half), pos_v.reshape(2, half),
      w1k.reshape(2, half, dh), w2k, w1v.reshape(2, half, dh), w2v, kn_cmp)


def _nsa_body(q_ref, gl_ref, gb_ref, qn_ref, lanes_ref, kc_ref, vcT_ref,
              ks_ref, vsT_ref, kw_ref, vwT_ref, ovT_ref, kmq_ref, o_ref,
              score_ref, gate_ref, qc_ref, qs_ref, m_ref, l_ref, acc_ref, part_ref):
    dh, hg, tq = HEAD_DIM, NSA_GROUP, Q_TILE
    cols = hg * tq
    qi = pl.program_id(2)
    t0 = qi * tq
    contract_last = (((1,), (1,)), ((), ()))
    n_blk = ovT_ref.shape[0]

    scale = dh ** -0.5 * LOG2E
    q_heads = [(_rms(q_ref[:, h * dh:(h + 1) * dh], qn_ref[...]) * scale).astype(BF16)
               for h in range(hg)]

    def stack_queries(aug_heads):
        return jnp.concatenate([jnp.concatenate([q_heads[h], aug_heads[h]], axis=1)
                                for h in range(hg)], axis=0)

    hc = kmq_ref.shape[1]
    assert cols % hc == 0 and hc % tq == 0

    def reset(b):
        m_ref[b] = jnp.full((1, cols), NEG_INF, F32)
        l_ref[b] = jnp.zeros((1, cols), F32)
        acc_ref[b] = jnp.zeros((dh, cols), F32)

    def attend(b, queries_ref, parts):
        groups = [slice(i * hc, (i + 1) * hc) for i in range(cols // hc)]
        scores = [[lax.dot_general(k, queries_ref[cs, :], contract_last, preferred_element_type=F32)
                   for cs in groups] for k, _, _, _ in parts]
        for (k, v_pieces, mask, bias), part_scores in zip(parts, scores):
            for cs, s in zip(groups, part_scores):
                if mask is not None:
                    s = jnp.where(mask(), s, NEG_INF)
                if bias is not None:
                    s = s + bias
                m_old = m_ref[b, :, cs]
                m_new = jnp.maximum(m_old, jnp.max(s, axis=0, keepdims=True))
                alpha = jnp.exp2(m_old - m_new)
                p = jnp.exp2(s - m_new)
                m_ref[b, :, cs] = m_new
                l_ref[b, :, cs] = alpha * l_ref[b, :, cs] + jnp.sum(p, axis=0, keepdims=True)
                p = p.astype(BF16)
                pv = None
                for i, vT in enumerate(v_pieces):
                    n = vT.shape[1]
                    d = jnp.dot(vT, p[i * n:(i + 1) * n, :], preferred_element_type=F32)
                    pv = d if pv is None else pv + d
                acc_ref[b, :, cs] = alpha * acc_ref[b, :, cs] + pv

    def result(b):
        return acc_ref[b] * (1.0 / l_ref[b])

    nc = kc_ref.shape[2]
    qc_ref[...] = stack_queries([jnp.broadcast_to(lanes_ref[0, hg + h:hg + h + 1, :], (tq, AUG_WIDTH)).astype(BF16)
                                 for h in range(hg)])
    s = lax.dot_general(kc_ref[0, 0], qc_ref[...], contract_last, preferred_element_type=F32)
    c_idx = lax.broadcasted_iota(jnp.int32, (nc, cols), 0)
    t_c = t0 + (lax.broadcasted_iota(jnp.int32, (nc, cols), 1) & (tq - 1))
    mask = (c_idx * CMP_STRIDE + (CMP_LEN - 1)) <= t_c
    s = jnp.where(mask, s, NEG_INF)
    m_c = jnp.max(s, axis=0, keepdims=True)
    p_c = jnp.where(mask, jnp.exp2(s - m_c), 0.0)
    l_c = jnp.sum(p_c, axis=0, keepdims=True)
    p_c = p_c * jnp.where(l_c > 0.0, 1.0 / l_c, 0.0)
    o_cmp = jnp.dot(vcT_ref[0, 0], p_c.astype(BF16), preferred_element_type=F32)

    cw = WIN_CHUNK
    assert cw == tq and WINDOW % cw == 0 and cw == kmq_ref.shape[0]
    n_back = WINDOW // cw
    WIN, SEL = 0, 1

    def win_part(c, mask=None, bias=None):
        k = kw_ref[0, 0, pl.ds(pl.multiple_of(c * cw, cw), cw), :]
        return k, [vwT_ref[0, 0, c]], mask, bias

    threshold = jnp.where(qi >= n_back, 0, cw)
    reset(WIN)
    attend(WIN, qc_ref,
           [win_part(qi, mask=lambda: kmq_ref[...] <= 0)]
           + [win_part(jnp.maximum(qi - j, 0), bias=jnp.where(qi >= j, 0.0, NEG_INF))
              for j in range(1, n_back)]
           + [win_part(jnp.maximum(qi - n_back, 0), mask=lambda: kmq_ref[...] > threshold)])

    gate_ref[...] = jax.nn.sigmoid(gl_ref[...] + gb_ref[...]).T

    def gate(h, branch):
        r = h * N_BRANCH + branch
        return gate_ref[r:r + 1, :]

    o_win = result(WIN)
    for h in range(hg):
        cs = slice(h * tq, (h + 1) * tq)
        part_ref[:, cs] = gate(h, 0) * o_cmp[:, cs] + gate(h, 2) * o_win[:, cs]

    p_sum = p_c[:, 0:tq]
    for h in range(1, hg):
        p_sum = p_sum + p_c[:, h * tq:(h + 1) * tq]
    hi = p_sum.astype(BF16)
    rem = p_sum - hi.astype(F32)
    mid = rem.astype(BF16)
    lo = (rem - mid.astype(F32)).astype(BF16)
    ovT = ovT_ref[...]
    imp = (jnp.dot(ovT, hi, preferred_element_type=F32)
           + jnp.dot(ovT, mid, preferred_element_type=F32)
           + jnp.dot(ovT, lo, preferred_element_type=F32))
    blk = lax.broadcasted_iota(jnp.int32, (n_blk, tq), 0)
    t_b = t0 + lax.broadcasted_iota(jnp.int32, (n_blk, tq), 1)
    cur = jnp.right_shift(t_b, SEL_BLK.bit_length() - 1)
    valid = blk * SEL_BLK <= t_b
    forced = (blk == 0) | (blk == cur) | (blk == cur - 1)
    score = jnp.where(valid, imp + FORCE_BONUS * forced.astype(F32), NEG_INF)
    score_ref[...] = score
    rank = jnp.zeros((n_blk, tq), F32)
    for j in range(n_blk):
        sj = score_ref[j:j + 1, :]
        beats = (sj > score) | ((sj == score) & (blk > j))
        rank = rank + beats.astype(F32)
    block_bias = jnp.where(rank < float(min(N_SEL, n_blk)), 0.0, NEG_INF)

    assert 2 * SLOPE_PIECES <= 8
    ones_rows = (lax.broadcasted_iota(jnp.int32, (8, tq), 0) < 2 * SLOPE_PIECES).astype(F32)
    extra_t = jnp.concatenate([block_bias, ones_rows,
                               jnp.zeros((AUG_WIDTH - n_blk - 8, tq), F32)], axis=0).T
    qs_ref[...] = stack_queries([(extra_t * lanes_ref[0, h:h + 1, :]).astype(BF16) for h in range(hg)])

    ck = SEL_CHUNK
    assert ck == tq and ck == kmq_ref.shape[0]

    def sel_part(c, n=1, mask=None, bias=None):
        k = ks_ref[0, 0, pl.ds(pl.multiple_of(c * ck, ck), n * ck), :]
        return k, [vsT_ref[0, 0, c + i] for i in range(n)], mask, bias

    reset(SEL)

    @pl.loop(0, qi // 2)
    def _(i):
        attend(SEL, qs_ref, [sel_part(2 * i, n=2)])

    odd = qi % 2 == 1
    attend(SEL, qs_ref,
           [sel_part(qi, mask=lambda: kmq_ref[...] <= 0),
            sel_part(jnp.maximum(qi - 1, 0), bias=jnp.where(odd, 0.0, NEG_INF))])

    o_slc = result(SEL)
    for h in range(hg):
        cs = slice(h * tq, (h + 1) * tq)
        o_h = part_ref[:, cs] + gate(h, 1) * o_slc[:, cs]
        o_ref[:, h * dh:(h + 1) * dh] = o_h.T.astype(o_ref.dtype)


def nsa_attention(z, gl, gate_b, q_norm, kcmp, vcmpT, ks, vsT, kw, vwT, *, B, T):
    G, hg, dh, tq = NSA_KV_HEADS, NSA_GROUP, HEAD_DIM, Q_TILE
    nq = T // tq
    nc = kcmp.shape[2]
    n_blk = T // SEL_BLK
    kdim = dh + AUG_WIDTH
    cols = hg * tq
    assert SEL_CHUNK == WIN_CHUNK == tq
    assert SEL_CHUNK % SEL_BLK == 0 and T % SEL_CHUNK == 0 and gl.shape[1] == G * LANES
    assert SEL_CHUNK % tq == 0 and n_blk % 8 == 0 and n_blk + 8 <= AUG_WIDTH

    slopes = np.exp2(-8.0 * (np.arange(NSA_HEADS, dtype=np.float64) + 1.0) / NSA_HEADS).reshape(G, hg)
    lanes = np.zeros((G, 2 * hg, AUG_WIDTH), np.float32)
    lanes[:, :hg, :n_blk] = 1.0
    for i, piece in enumerate(_bf16_pieces(slopes * LOG2E, SLOPE_PIECES)):
        for lane in (n_blk + i, n_blk + SLOPE_PIECES + i):
            lanes[:, :hg, lane] = piece
            lanes[:, hg:, lane] = piece
    cs = np.arange(nc)[:, None] * CMP_STRIDE
    bs = np.arange(n_blk)[None, :] * SEL_BLK
    overlap = np.clip(np.minimum(cs + CMP_LEN, bs + SEL_BLK) - np.maximum(cs, bs), 0, None) / CMP_LEN
    overlap[(T - CMP_LEN) // CMP_STRIDE + 1:] = 0.0
    ovT = jnp.asarray(overlap.T, dtype=BF16)
    hc = NSA_COLS_PER_DOT
    kmq = jnp.asarray(np.arange(SEL_CHUNK)[:, None] - (np.arange(hc)[None, :] % tq), dtype=jnp.int32)

    row = lambda b, qi: b * nq + qi
    per_bg = lambda shape: pl.BlockSpec((1, 1) + shape, lambda b, g, qi: (b, g) + (0,) * len(shape))
    const = lambda shape: pl.BlockSpec(shape, lambda b, g, qi: (0,) * len(shape))
    return pl.pallas_call(
        _nsa_body,
        grid=(B, G, nq),
        in_specs=[pl.BlockSpec((tq, hg * dh), lambda b, g, qi: (row(b, qi), g)),
                  pl.BlockSpec((tq, LANES), lambda b, g, qi: (row(b, qi), g)),
                  pl.BlockSpec((1, LANES), lambda b, g, qi: (0, g)),
                  const((1, dh)),
                  pl.BlockSpec((1, 2 * hg, AUG_WIDTH), lambda b, g, qi: (g, 0, 0)),
                  per_bg((nc, kdim)), per_bg((dh, nc)),
                  per_bg((T, kdim)), per_bg((T // SEL_CHUNK, dh, SEL_CHUNK)),
                  per_bg((T, kdim)), per_bg((T // WIN_CHUNK, dh, WIN_CHUNK)),
                  const((n_blk, nc)), const((SEL_CHUNK, hc))],
        out_specs=pl.BlockSpec((tq, hg * dh), lambda b, g, qi: (row(b, qi), g)),
        out_shape=jax.ShapeDtypeStruct((B * T, NSA_WIDTH), BF16),
        scratch_shapes=[pltpu.VMEM((n_blk, tq), F32), pltpu.VMEM((LANES, tq), F32),
                        pltpu.VMEM((cols, kdim), BF16), pltpu.VMEM((cols, kdim), BF16),
                        pltpu.VMEM((2, 1, cols), F32), pltpu.VMEM((2, 1, cols), F32),
                        pltpu.VMEM((2, dh, cols), F32), pltpu.VMEM((dh, cols), F32)],
        compiler_params=_cparams("parallel", "parallel", "arbitrary"),
        name="nsa_attention",
    )(z, gl, gate_b, q_norm, jnp.asarray(lanes), kcmp, vcmpT, ks, vsT, kw, vwT, ovT, kmq)


def _conv_body(a_ref, b_ref, ah_ref, bh_ref, w_ref, cb_ref, lg_ref, lb_ref, o_ref,
               h_ref, g_ref, y_ref, *, tt):
    C = a_ref.shape[1]
    halo = ah_ref[...] * jax.nn.sigmoid(bh_ref[...])
    h_ref[0:CONV_HALO, :] = jnp.where(pl.program_id(1) == 0, 0.0, halo)
    h_ref[CONV_HALO:CONV_HALO + tt, :] = a_ref[...] * jax.nn.sigmoid(b_ref[...])
    h_ref[CONV_HALO + tt:CONV_HALO + tt + SUBLANES, :] = jnp.zeros((SUBLANES, C), F32)
    lead = CONV_HALO - (CONV_KERNEL - 1)
    rows = tt + SUBLANES
    for s in range(C // LANES):
        cs = slice(s * LANES, (s + 1) * LANES)
        acc = None
        for r in range(SUBLANES):
            group = None
            for o in range(lead, lead + CONV_KERNEL):
                if o % SUBLANES == r:
                    term = w_ref[o - lead:o - lead + 1, cs] * h_ref[o - r:o - r + rows, cs]
                    group = term if group is None else group + term
            if r == 0:
                piece = group[0:tt, :]
            else:
                g_ref[r] = group
                piece = g_ref[r, r:r + tt, :]
            acc = piece if acc is None else acc + piece
        y_ref[:, cs] = acc + cb_ref[:, cs]
    y = y_ref[...]
    mu = jnp.mean(y, axis=-1, keepdims=True)
    var = jnp.mean(jnp.square(y - mu), axis=-1, keepdims=True)
    yn = (y - mu) * lax.rsqrt(var + LN_EPS) * lg_ref[...] + lb_ref[...]
    o_ref[...] = _silu(yn).astype(o_ref.dtype)


def conformer_conv(z, conv_w, conv_b, ln_g, ln_b, *, B, T, col0, C, tt=128):
    nt = T // tt
    ca = col0 // C
    hb = tt // CONV_HALO
    row = lambda b, t: b * nt + t
    halo_row = lambda b, t: jnp.maximum(row(b, t) * hb - 1, 0)
    const = lambda shape: pl.BlockSpec(shape, lambda b, t: (0,) * len(shape))
    return pl.pallas_call(
        functools.partial(_conv_body, tt=tt),
        grid=(B, nt),
        in_specs=[pl.BlockSpec((tt, C), lambda b, t: (row(b, t), ca)),
                  pl.BlockSpec((tt, C), lambda b, t: (row(b, t), ca + 1)),
                  pl.BlockSpec((CONV_HALO, C), lambda b, t: (halo_row(b, t), ca)),
                  pl.BlockSpec((CONV_HALO, C), lambda b, t: (halo_row(b, t), ca + 1)),
                  const((CONV_KERNEL, C)), const((1, C)), const((1, C)), const((1, C))],
        out_specs=pl.BlockSpec((tt, C), lambda b, t: (row(b, t), 0)),
        out_shape=jax.ShapeDtypeStruct((B * T, C), BF16),
        scratch_shapes=[pltpu.VMEM((CONV_HALO + tt + SUBLANES, C), F32),
                        pltpu.VMEM((SUBLANES, tt + SUBLANES, LANES), F32),
                        pltpu.VMEM((tt, C), F32)],
        compiler_params=_cparams("parallel", "parallel"),
        name="conformer_conv",
    )(z, z, z, z, conv_w, conv_b, ln_g, ln_b)


def _mem_attn_body(x_ref, gq_ref, wq_ref, kv_ref, qn_ref, kn_ref, wo_ref, gn_ref, o_ref, on_ref):
    dh = HEAD_DIM
    width = MEM_HEADS * dh
    scale = dh ** -0.5
    q_all = jnp.dot(_rms(x_ref[...], gq_ref[...]).astype(BF16), wq_ref[...], preferred_element_type=F32)
    heads = [slice(h * dh, (h + 1) * dh) for h in range(MEM_HEADS)]
    scores = []
    for cs in heads:
        q = (_rms(q_all[:, cs], qn_ref[...]) * scale).astype(BF16)
        k = _rms(kv_ref[:, cs], kn_ref[...]).astype(BF16)
        scores.append(lax.dot_general(q, k, (((1,), (1,)), ((), ())), preferred_element_type=F32))
    outs = []
    for h, s in enumerate(scores):
        v = kv_ref[:, width + h * dh:width + (h + 1) * dh].astype(BF16)
        p = jnp.exp(s - jnp.max(s, axis=-1, keepdims=True))
        inv = 1.0 / jnp.sum(p, axis=-1, keepdims=True)
        outs.append((jnp.dot(p.astype(BF16), v, preferred_element_type=F32) * inv).astype(BF16))
    o = jnp.concatenate(outs, axis=1)
    y = x_ref[...] + jnp.dot(o, wo_ref[...], preferred_element_type=F32)
    o_ref[...] = y
    on_ref[...] = _rms(y, gn_ref[...]).astype(on_ref.dtype)


def mem_attention(x, q_gain, w_mq, kvm, mq_norm, mk_norm, w_mo, next_norm, *, B, T, M, tt=256):
    D = x.shape[1]
    width = MEM_HEADS * HEAD_DIM
    nt = T // tt
    row = lambda b, t: b * nt + t
    const = lambda shape: pl.BlockSpec(shape, lambda b, t: (0,) * len(shape))
    return pl.pallas_call(
        _mem_attn_body,
        grid=(B, nt),
        in_specs=[pl.BlockSpec((tt, D), lambda b, t: (row(b, t), 0)),
                  const((1, D)), const((D, width)),
                  pl.BlockSpec((M, 2 * width), lambda b, t: (b, 0)),
                  const((1, HEAD_DIM)), const((1, HEAD_DIM)), const((width, D)),
                  const((1, D))],
        out_specs=[pl.BlockSpec((tt, D), lambda b, t: (row(b, t), 0)),
                   pl.BlockSpec((tt, D), lambda b, t: (row(b, t), 0))],
        out_shape=[jax.ShapeDtypeStruct((B * T, D), F32),
                   jax.ShapeDtypeStruct((B * T, D), BF16)],
        compiler_params=_cparams("parallel", "parallel"),
        name="mem_attention",
    )(x, q_gain, w_mq, kvm, mq_norm, mk_norm, w_mo, next_norm)


def kernel(x, mem, norm_mix, w_in, gate_b, q_norm, k_norm_cmp, k_norm_slc, k_norm_win, cmp_pos_k, cmp_pos_v, cmp_k_w1, cmp_k_w2, cmp_v_w1, cmp_v_w2, conv_w, conv_b, conv_ln_g, conv_ln_b, w_out, norm_mem_q, norm_mem_kv, w_mq, w_mk, w_mv, mq_norm, mk_norm, w_mo, norm_ffn, w_gate, w_up, w_down):
    B, T, D = x.shape
    M = mem.shape[1]
    depth = w_in.shape[0]
    G, hg = NSA_KV_HEADS, NSA_GROUP
    C = D - NSA_WIDTH
    n_gate = N_BRANCH * NSA_HEADS
    kv_all = 6 * KV_WIDTH
    assert w_in.shape[2] == NSA_WIDTH + kv_all + n_gate + 2 * C
    assert C == NSA_WIDTH, "column blocking below assumes equal NSA and conv widths"

    row1 = lambda v: v.reshape(1, -1)
    xf = x.reshape(B * T, D)
    memf = mem.reshape(B * M, D)
    u0 = NSA_WIDTH + kv_all + n_gate
    for l in range(depth):
        w = w_in[l]
        w_t = jnp.swapaxes(w, 0, 1)
        wg = w[:, NSA_WIDTH + kv_all:u0].reshape(D, G, hg * N_BRANCH)
        wg = jnp.pad(wg, ((0, 0), (0, 0), (0, LANES - hg * N_BRANCH))).reshape(D, G * LANES)
        gb = jnp.pad(gate_b[l].reshape(G, hg * N_BRANCH), ((0, 0), (0, LANES - hg * N_BRANCH))).reshape(1, G * LANES)

        z, gl = in_projection(xf, row1(norm_mix[l]), w_t, wg, tm=1024, tn=512,
                              segments=((0, NSA_WIDTH), (u0, 2 * C), (NSA_WIDTH, kv_all)))
        kv_col0 = NSA_WIDTH + 2 * C
        kc, vc, ks, vsT, kw, vwT = kv_prep(z, row1(k_norm_slc[l]), row1(k_norm_win[l]), B=B, T=T, col0=kv_col0)
        kcmp, vcmpT = compress_kv(kc, vc, cmp_pos_k[l], cmp_pos_v[l],
                                  cmp_k_w1[l].astype(BF16), cmp_k_w2[l].astype(BF16),
                                  cmp_v_w1[l].astype(BF16), cmp_v_w2[l].astype(BF16), row1(k_norm_cmp[l]),
                                  n_blk=T // SEL_BLK)
        o_nsa = nsa_attention(z, gl, gb, row1(q_norm[l]), kcmp, vcmpT, ks, vsT, kw, vwT, B=B, T=T)
        o_conv = conformer_conv(z, conv_w[l], row1(conv_b[l]), row1(conv_ln_g[l]), row1(conv_ln_b[l]),
                                B=B, T=T, col0=NSA_WIDTH, C=C)
        xf = concat_matmul_residual(o_nsa, o_conv, w_out[l], xf, tm=2048, tn=512)

        w_mkv = jnp.concatenate([w_mk[l], w_mv[l]], axis=1).astype(BF16)
        kvm = norm_matmul(memf, row1(norm_mem_kv[l]), w_mkv, tm=B * M, tn=512)
        xf, xn = mem_attention(xf, row1(norm_mem_q[l]), w_mq[l].astype(BF16), kvm,
                               row1(mq_norm[l]), row1(mk_norm[l]), w_mo[l].astype(BF16),
                               row1(norm_ffn[l]), B=B, T=T, M=M)

        hidden = swiglu_up(xn, w_gate[l], w_up[l], tm=1024, tn=512)
        xf = matmul_residual(hidden, w_down[l].astype(BF16), xf, tm=1024, tn=512)
    return xf.reshape(B, T, D)
```

```python
import functools

import ml_dtypes
import numpy as np
import jax
import jax.numpy as jnp
from jax import lax
from jax.experimental import pallas as pl
from jax.experimental.pallas import tpu as pltpu

F32 = jnp.float32
BF16 = jnp.bfloat16

HEAD_DIM = 128
NSA_HEADS = 8
NSA_KV_HEADS = 2
NSA_GROUP = NSA_HEADS // NSA_KV_HEADS
NSA_WIDTH = NSA_HEADS * HEAD_DIM
KV_WIDTH = NSA_KV_HEADS * HEAD_DIM
N_BRANCH = 3
CONV_KERNEL = 31
CMP_LEN = 32
CMP_STRIDE = 16
SEL_BLK = 64
N_SEL = 16
WINDOW = 512
MEM_HEADS = 4
NEG_INF = -1e30
FORCE_BONUS = 1e3
RMS_EPS = 1e-6
LN_EPS = 1e-5

LANES = 128
SUBLANES = 8
VMEM_LIMIT_BYTES = 56 * 1024 * 1024

Q_TILE = 256
SEL_CHUNK = 256
WIN_CHUNK = 256
NSA_COLS_PER_DOT = 512
CONV_HALO = 32


def _cparams(*sem):
    return pltpu.CompilerParams(dimension_semantics=sem, vmem_limit_bytes=VMEM_LIMIT_BYTES)


def _rms(x, g, eps=RMS_EPS):
    ms = jnp.mean(x * x, axis=-1, keepdims=True)
    return x * lax.rsqrt(ms + eps) * g


def _silu(x):
    return x * jax.nn.sigmoid(x)


def _norm_to_scratch(x_ref, g_ref, xn_ref):
    @pl.when(pl.program_id(1) == 0)
    def _():
        xn_ref[...] = _rms(x_ref[...], g_ref[...]).astype(xn_ref.dtype)


def _norm_mm_body(x_ref, g_ref, w_ref, o_ref, xn_ref):
    _norm_to_scratch(x_ref, g_ref, xn_ref)
    o_ref[...] = jnp.dot(xn_ref[...], w_ref[...], preferred_element_type=F32).astype(o_ref.dtype)


def _in_proj_body(x_ref, g_ref, wt_ref, wg_ref, o_ref, og_ref, xn_ref):
    _norm_to_scratch(x_ref, g_ref, xn_ref)

    @pl.when(pl.program_id(1) == 0)
    def _():
        og_ref[...] = jnp.dot(xn_ref[...], wg_ref[...].astype(BF16), preferred_element_type=F32)

    o_ref[...] = lax.dot_general(xn_ref[...], wt_ref[...].astype(BF16), (((1,), (1,)), ((), ())),
                                 preferred_element_type=F32)


def _swiglu_body(x_ref, wg_ref, wu_ref, o_ref):
    x = x_ref[...]
    a = jnp.dot(x, wg_ref[...].astype(BF16), preferred_element_type=F32)
    b = jnp.dot(x, wu_ref[...].astype(BF16), preferred_element_type=F32)
    o_ref[...] = (_silu(a) * b).astype(o_ref.dtype)


def norm_matmul(x, g, w, *, tm, tn, out_dtype=F32):
    M, K = x.shape
    N = w.shape[1]
    return pl.pallas_call(
        _norm_mm_body,
        grid=(M // tm, N // tn),
        in_specs=[pl.BlockSpec((tm, K), lambda i, j: (i, 0)),
                  pl.BlockSpec((1, K), lambda i, j: (0, 0)),
                  pl.BlockSpec((K, tn), lambda i, j: (0, j))],
        out_specs=pl.BlockSpec((tm, tn), lambda i, j: (i, j)),
        out_shape=jax.ShapeDtypeStruct((M, N), out_dtype),
        scratch_shapes=[pltpu.VMEM((tm, K), BF16)],
        compiler_params=_cparams("parallel", "arbitrary"),
        name="norm_matmul",
    )(x, g, w)


def in_projection(x, g, w_t, wg, *, segments, tm, tn):
    M, K = x.shape
    NG = wg.shape[1]
    assert all(start % SUBLANES == 0 and size % tn == 0 for start, size in segments)
    N = sum(size for _, size in segments)

    def row_start(i, j):
        group, first = 0, 0
        for seg_start, size in segments:
            group = jnp.where(j >= first, seg_start // SUBLANES + (j - first) * (tn // SUBLANES), group)
            first += size // tn
        return (group * SUBLANES, 0)

    return pl.pallas_call(
        _in_proj_body,
        grid=(M // tm, N // tn),
        in_specs=[pl.BlockSpec((tm, K), lambda i, j: (i, 0)),
                  pl.BlockSpec((1, K), lambda i, j: (0, 0)),
                  pl.BlockSpec((pl.Element(tn), pl.Element(K)), row_start),
                  pl.BlockSpec((K, NG), lambda i, j: (0, 0))],
        out_specs=[pl.BlockSpec((tm, tn), lambda i, j: (i, j)),
                   pl.BlockSpec((tm, NG), lambda i, j: (i, 0))],
        out_shape=[jax.ShapeDtypeStruct((M, N), F32),
                   jax.ShapeDtypeStruct((M, NG), F32)],
        scratch_shapes=[pltpu.VMEM((tm, K), BF16)],
        compiler_params=_cparams("parallel", "arbitrary"),
        name="in_proj",
    )(x, g, w_t, wg)


def swiglu_up(xn, w_gate, w_up, *, tm, tn):
    M, K = xn.shape
    N = w_gate.shape[1]
    return pl.pallas_call(
        _swiglu_body,
        grid=(M // tm, N // tn),
        in_specs=[pl.BlockSpec((tm, K), lambda i, j: (i, 0)),
                  pl.BlockSpec((K, tn), lambda i, j: (0, j)),
                  pl.BlockSpec((K, tn), lambda i, j: (0, j))],
        out_specs=pl.BlockSpec((tm, tn), lambda i, j: (i, j)),
        out_shape=jax.ShapeDtypeStruct((M, N), BF16),
        compiler_params=_cparams("parallel", "arbitrary"),
        name="ffn_up",
    )(xn, w_gate, w_up)


def _mm_res_body(a_ref, w_ref, r_ref, o_ref):
    o_ref[...] = r_ref[...] + jnp.dot(a_ref[...], w_ref[...], preferred_element_type=F32)


def matmul_residual(a, w, res, *, tm, tn):
    M, K = a.shape
    N = w.shape[1]
    return pl.pallas_call(
        _mm_res_body,
        grid=(M // tm, N // tn),
        in_specs=[pl.BlockSpec((tm, K), lambda i, j: (i, 0)),
                  pl.BlockSpec((K, tn), lambda i, j: (0, j)),
                  pl.BlockSpec((tm, tn), lambda i, j: (i, j))],
        out_specs=pl.BlockSpec((tm, tn), lambda i, j: (i, j)),
        out_shape=jax.ShapeDtypeStruct((M, N), F32),
        compiler_params=_cparams("parallel", "arbitrary"),
        name="matmul_residual",
    )(a, w, res)


def _mm2_res_body(a_ref, b_ref, wa_ref, wb_ref, r_ref, o_ref):
    acc = jnp.dot(a_ref[...], wa_ref[...].astype(BF16), preferred_element_type=F32)
    acc = acc + jnp.dot(b_ref[...], wb_ref[...].astype(BF16), preferred_element_type=F32)
    o_ref[...] = r_ref[...] + acc


def concat_matmul_residual(a, b, w, res, *, tm, tn):
    M, Ka = a.shape
    Kb = b.shape[1]
    assert Ka == Kb and w.shape[0] == Ka + Kb
    N = w.shape[1]
    return pl.pallas_call(
        _mm2_res_body,
        grid=(M // tm, N // tn),
        in_specs=[pl.BlockSpec((tm, Ka), lambda i, j: (i, 0)),
                  pl.BlockSpec((tm, Kb), lambda i, j: (i, 0)),
                  pl.BlockSpec((Ka, tn), lambda i, j: (0, j)),
                  pl.BlockSpec((Kb, tn), lambda i, j: (1, j)),
                  pl.BlockSpec((tm, tn), lambda i, j: (i, j))],
        out_specs=pl.BlockSpec((tm, tn), lambda i, j: (i, j)),
        out_shape=jax.ShapeDtypeStruct((M, N), F32),
        compiler_params=_cparams("parallel", "arbitrary"),
        name="out_proj",
    )(a, b, w, w, res)


AUG_WIDTH = LANES
SLOPE_PIECES = 4
LOG2E = float(np.log2(np.e))


def _bf16_pieces(x, n):
    out, rem = [], np.asarray(x, np.float64)
    for _ in range(n):
        piece = rem.astype(ml_dtypes.bfloat16).astype(np.float64)
        out.append(piece.astype(np.float32))
        rem = rem - piece
    return out


def _position_lanes(pos_hi, pos_lo, onehot, n_blk, shape):
    lane = lax.broadcasted_iota(jnp.int32, shape, 1)
    p = SLOPE_PIECES
    extra = jnp.where((lane >= n_blk) & (lane < n_blk + p), pos_hi,
                      jnp.where((lane >= n_blk + p) & (lane < n_blk + 2 * p), pos_lo, 0.0))
    if onehot is not None:
        extra = jnp.where(lane < n_blk, onehot, extra)
    return extra.astype(BF16)


def _kv_prep_body(c_ref, s_ref, w_ref, kns_ref, knw_ref,
                  kc_o, vc_o, ks_o, vsT_o, kw_o, vwT_o, slab_ref, *, tt, n_blk):
    dh = HEAD_DIM
    shape = (tt, AUG_WIDTH)
    pos = pl.program_id(1) * tt + lax.broadcasted_iota(jnp.int32, shape, 0)
    lane = lax.broadcasted_iota(jnp.int32, shape, 1)
    blk = jnp.right_shift(pos, SEL_BLK.bit_length() - 1)
    pos_hi = (blk * SEL_BLK).astype(F32)
    pos_lo = (pos & (SEL_BLK - 1)).astype(F32)
    aug_sel = _position_lanes(pos_hi, pos_lo, (blk == lane).astype(F32), n_blk, shape)
    aug_win = _position_lanes(pos_hi, pos_lo, None, n_blk, shape)
    for g in range(NSA_KV_HEADS):
        k_cols = slice(g * dh, (g + 1) * dh)
        v_cols = slice(KV_WIDTH + g * dh, KV_WIDTH + (g + 1) * dh)
        for slab, (cols, out) in enumerate(((k_cols, kc_o), (v_cols, vc_o))):
            slab_ref[slab] = c_ref[:, cols]
            for i in range(CMP_STRIDE):
                rows = pl.ds(i, tt // CMP_STRIDE, stride=CMP_STRIDE)
                out[0, g, :, i * dh:(i + 1) * dh] = slab_ref[slab, rows, :]
        ks_o[0, g, :, 0:dh] = _rms(s_ref[:, k_cols], kns_ref[...]).astype(BF16)
        ks_o[0, g, :, dh:dh + AUG_WIDTH] = aug_sel
        kw_o[0, g, :, 0:dh] = _rms(w_ref[:, k_cols], knw_ref[...]).astype(BF16)
        kw_o[0, g, :, dh:dh + AUG_WIDTH] = aug_win
        vsT = s_ref[:, v_cols].T.astype(BF16)
        for c in range(tt // SEL_CHUNK):
            vsT_o[0, g, c] = vsT[:, c * SEL_CHUNK:(c + 1) * SEL_CHUNK]
        vwT = w_ref[:, v_cols].T.astype(BF16)
        for c in range(tt // WIN_CHUNK):
            vwT_o[0, g, c] = vwT[:, c * WIN_CHUNK:(c + 1) * WIN_CHUNK]


def kv_prep(z, kn_slc, kn_win, *, B, T, col0, tt=256):
    G, dh = NSA_KV_HEADS, HEAD_DIM
    nt = T // tt
    n_blk = T // SEL_BLK
    assert n_blk + 2 * SLOPE_PIECES <= AUG_WIDTH
    cb = col0 // (2 * KV_WIDTH)
    row = lambda b, t: b * nt + t
    kdim = dh + AUG_WIDTH
    return pl.pallas_call(
        functools.partial(_kv_prep_body, tt=tt, n_blk=n_blk),
        grid=(B, nt),
        in_specs=[pl.BlockSpec((tt, 2 * KV_WIDTH), lambda b, t: (row(b, t), cb)),
                  pl.BlockSpec((tt, 2 * KV_WIDTH), lambda b, t: (row(b, t), cb + 1)),
                  pl.BlockSpec((tt, 2 * KV_WIDTH), lambda b, t: (row(b, t), cb + 2)),
                  pl.BlockSpec((1, dh), lambda b, t: (0, 0)),
                  pl.BlockSpec((1, dh), lambda b, t: (0, 0))],
        out_specs=[pl.BlockSpec((1, G, tt // CMP_STRIDE, CMP_STRIDE * dh), lambda b, t: (b, 0, t, 0)),
                   pl.BlockSpec((1, G, tt // CMP_STRIDE, CMP_STRIDE * dh), lambda b, t: (b, 0, t, 0)),
                   pl.BlockSpec((1, G, tt, kdim), lambda b, t: (b, 0, t, 0)),
                   pl.BlockSpec((1, G, tt // SEL_CHUNK, dh, SEL_CHUNK), lambda b, t: (b, 0, t, 0, 0)),
                   pl.BlockSpec((1, G, tt, kdim), lambda b, t: (b, 0, t, 0)),
                   pl.BlockSpec((1, G, tt // WIN_CHUNK, dh, WIN_CHUNK), lambda b, t: (b, 0, t, 0, 0))],
        out_shape=[jax.ShapeDtypeStruct((B, G, T // CMP_STRIDE, CMP_STRIDE * dh), F32),
                   jax.ShapeDtypeStruct((B, G, T // CMP_STRIDE, CMP_STRIDE * dh), F32),
                   jax.ShapeDtypeStruct((B, G, T, kdim), BF16),
                   jax.ShapeDtypeStruct((B, G, T // SEL_CHUNK, dh, SEL_CHUNK), BF16),
                   jax.ShapeDtypeStruct((B, G, T, kdim), BF16),
                   jax.ShapeDtypeStruct((B, G, T // WIN_CHUNK, dh, WIN_CHUNK), BF16)],
        scratch_shapes=[pltpu.VMEM((2, tt, dh), F32)],
        compiler_params=_cparams("parallel", "parallel"),
        name="kv_prep",
    )(z, z, z, kn_slc, kn_win)


def _compress_body(xk_ref, xv_ref, pk_ref, pv_ref, w1k_ref, w2k_ref, w1v_ref, w2v_ref, kn_ref,
                   kc_o, vcT_o, *, n_blk):
    dh = HEAD_DIM

    def phi(x_ref, p_ref, w1_ref, w2_ref):
        x = x_ref[0, 0]
        first = jnp.dot((x + p_ref[0:1, :]).astype(BF16), w1_ref[0], preferred_element_type=F32)
        second = jnp.dot((x + p_ref[1:2, :]).astype(BF16), w1_ref[1], preferred_element_type=F32)
        n = second.shape[0]
        h = first + pltpu.roll(second, n - 1, 0)
        return jnp.dot(_silu(h).astype(BF16), w2_ref[...], preferred_element_type=F32)

    k = _rms(phi(xk_ref, pk_ref, w1k_ref, w2k_ref), kn_ref[...])
    nc = k.shape[0]
    shape = (nc, AUG_WIDTH)
    start = (lax.broadcasted_iota(jnp.int32, shape, 0) * CMP_STRIDE).astype(F32)
    centre = jnp.full(shape, (CMP_LEN - 1) / 2.0, F32)
    kc_o[0, 0, :, 0:dh] = k.astype(BF16)
    kc_o[0, 0, :, dh:dh + AUG_WIDTH] = _position_lanes(start, centre, None, n_blk, shape)
    vcT_o[0, 0] = phi(xv_ref, pv_ref, w1v_ref, w2v_ref).T.astype(BF16)


def compress_kv(xk, xv, pos_k, pos_v, w1k, w2k, w1v, w2v, kn_cmp, *, n_blk):
    B, G, nc, half = xk.shape
    dh = half // CMP_STRIDE
    assert CMP_LEN == 2 * CMP_STRIDE
    x_spec = pl.BlockSpec((1, 1, nc, half), lambda b, g: (b, g, 0, 0))
    full = lambda shape: pl.BlockSpec(shape, lambda b, g: (0,) * len(shape))
    return pl.pallas_call(
        functools.partial(_compress_body, n_blk=n_blk),
        grid=(B, G),
        in_specs=[x_spec, x_spec, full((2, half)), full((2, half)),
                  full((2, half, dh)), full((dh, dh)), full((2, half, dh)), full((dh, dh)),
                  full((1, dh))],
        out_specs=[pl.BlockSpec((1, 1, nc, dh + AUG_WIDTH), lambda b, g: (b, g, 0, 0)),
                   pl.BlockSpec((1, 1, dh, nc), lambda b, g: (b, g, 0, 0))],
        out_shape=[jax.ShapeDtypeStruct((B, G, nc, dh + AUG_WIDTH), BF16),
                   jax.ShapeDtypeStruct((B, G, dh, nc), BF16)],
        compiler_params=_cparams("parallel", "parallel"),
        name="compress_kv",
    )(xk, xv, pos_k.reshape(2, half), pos_v.reshape(2, half),
      w1k.reshape(2, half, dh), w2k, w1v.reshape(2, half, dh), w2v, kn_cmp)


def _nsa_body(q_ref, gl_ref, gb_ref, qn_ref, lanes_ref, kc_ref, vcT_ref,
              ks_ref, vsT_ref, kw_ref, vwT_ref, ovT_ref, kmq_ref, o_ref,
              score_ref, gate_ref, qc_ref, qs_ref, m_ref, l_ref, acc_ref, part_ref):
    dh, hg, tq = HEAD_DIM, NSA_GROUP, Q_TILE
    cols = hg * tq
    qi = pl.program_id(2)
    t0 = qi * tq
    contract_last = (((1,), (1,)), ((), ()))
    n_blk = ovT_ref.shape[0]

    scale = dh ** -0.5 * LOG2E
    q_heads = [(_rms(q_ref[:, h * dh:(h + 1) * dh], qn_ref[...]) * scale).astype(BF16)
               for h in range(hg)]

    def stack_queries(aug_heads):
        return jnp.concatenate([jnp.concatenate([q_heads[h], aug_heads[h]], axis=1)
                                for h in range(hg)], axis=0)

    hc = kmq_ref.shape[1]
    assert cols % hc == 0 and hc % tq == 0

    def reset(b):
        m_ref[b] = jnp.full((1, cols), NEG_INF, F32)
        l_ref[b] = jnp.zeros((1, cols), F32)
        acc_ref[b] = jnp.zeros((dh, cols), F32)

    def attend(b, queries_ref, parts):
        groups = [slice(i * hc, (i + 1) * hc) for i in range(cols // hc)]
        scores = [[lax.dot_general(k, queries_ref[cs, :], contract_last, preferred_element_type=F32)
                   for cs in groups] for k, _, _, _ in parts]
        for (k, v_pieces, mask, bias), part_scores in zip(parts, scores):
            for cs, s in zip(groups, part_scores):
                if mask is not None:
                    s = jnp.where(mask(), s, NEG_INF)
                if bias is not None:
                    s = s + bias
                m_old = m_ref[b, :, cs]
                m_new = jnp.maximum(m_old, jnp.max(s, axis=0, keepdims=True))
                alpha = jnp.exp2(m_old - m_new)
                p = jnp.exp2(s - m_new)
                m_ref[b, :, cs] = m_new
                l_ref[b, :, cs] = alpha * l_ref[b, :, cs] + jnp.sum(p, axis=0, keepdims=True)
                p = p.astype(BF16)
                pv = None
                for i, vT in enumerate(v_pieces):
                    n = vT.shape[1]
                    d = jnp.dot(vT, p[i * n:(i + 1) * n, :], preferred_element_type=F32)
                    pv = d if pv is None else pv + d
                acc_ref[b, :, cs] = alpha * acc_ref[b, :, cs] + pv

    def result(b):
        return acc_ref[b] * (1.0 / l_ref[b])

    nc = kc_ref.shape[2]
    qc_ref[...] = stack_queries([jnp.broadcast_to(lanes_ref[0, hg + h:hg + h + 1, :], (tq, AUG_WIDTH)).astype(BF16)
                                 for h in range(hg)])
    s = lax.dot_general(kc_ref[0, 0], qc_ref[...], contract_last, preferred_element_type=F32)
    c_idx = lax.broadcasted_iota(jnp.int32, (nc, cols), 0)
    t_c = t0 + (lax.broadcasted_iota(jnp.int32, (nc, cols), 1) & (tq - 1))
    mask = (c_idx * CMP_STRIDE + (CMP_LEN - 1)) <= t_c
    s = jnp.where(mask, s, NEG_INF)
    m_c = jnp.max(s, axis=0, keepdims=True)
    p_c = jnp.where(mask, jnp.exp2(s - m_c), 0.0)
    l_c = jnp.sum(p_c, axis=0, keepdims=True)
    p_c = p_c * jnp.where(l_c > 0.0, 1.0 / l_c, 0.0)
    o_cmp = jnp.dot(vcT_ref[0, 0], p_c.astype(BF16), preferred_element_type=F32)

    cw = WIN_CHUNK
    assert cw == tq and WINDOW % cw == 0 and cw == kmq_ref.shape[0]
    n_back = WINDOW // cw
    WIN, SEL = 0, 1

    def win_part(c, mask=None, bias=None):
        k = kw_ref[0, 0, pl.ds(pl.multiple_of(c * cw, cw), cw), :]
        return k, [vwT_ref[0, 0, c]], mask, bias

    threshold = jnp.where(qi >= n_back, 0, cw)
    reset(WIN)
    attend(WIN, qc_ref,
           [win_part(qi, mask=lambda: kmq_ref[...] <= 0)]
           + [win_part(jnp.maximum(qi - j, 0), bias=jnp.where(qi >= j, 0.0, NEG_INF))
              for j in range(1, n_back)]
           + [win_part(jnp.maximum(qi - n_back, 0), mask=lambda: kmq_ref[...] > threshold)])

    gate_ref[...] = jax.nn.sigmoid(gl_ref[...] + gb_ref[...]).T

    def gate(h, branch):
        r = h * N_BRANCH + branch
        return gate_ref[r:r + 1, :]

    o_win = result(WIN)
    for h in range(hg):
        cs = slice(h * tq, (h + 1) * tq)
        part_ref[:, cs] = gate(h, 0) * o_cmp[:, cs] + gate(h, 2) * o_win[:, cs]

    p_sum = p_c[:, 0:tq]
    for h in range(1, hg):
        p_sum = p_sum + p_c[:, h * tq:(h + 1) * tq]
    hi = p_sum.astype(BF16)
    rem = p_sum - hi.astype(F32)
    mid = rem.astype(BF16)
    lo = (rem - mid.astype(F32)).astype(BF16)
    ovT = ovT_ref[...]
    imp = (jnp.dot(ovT, hi, preferred_element_type=F32)
           + jnp.dot(ovT, mid, preferred_element_type=F32)
           + jnp.dot(ovT, lo, preferred_element_type=F32))
    blk = lax.broadcasted_iota(jnp.int32, (n_blk, tq), 0)
    t_b = t0 + lax.broadcasted_iota(jnp.int32, (n_blk, tq), 1)
    cur = jnp.right_shift(t_b, SEL_BLK.bit_length() - 1)
    valid = blk * SEL_BLK <= t_b
    forced = (blk == 0) | (blk == cur) | (blk == cur - 1)
    score = jnp.where(valid, imp + FORCE_BONUS * forced.astype(F32), NEG_INF)
    score_ref[...] = score
    rank = jnp.zeros((n_blk, tq), F32)
    for j in range(n_blk):
        sj = score_ref[j:j + 1, :]
        beats = (sj > score) | ((sj == score) & (blk > j))
        rank = rank + beats.astype(F32)
    block_bias = jnp.where(rank < float(min(N_SEL, n_blk)), 0.0, NEG_INF)

    assert 2 * SLOPE_PIECES <= 8
    ones_rows = (lax.broadcasted_iota(jnp.int32, (8, tq), 0) < 2 * SLOPE_PIECES).astype(F32)
    extra_t = jnp.concatenate([block_bias, ones_rows,
                               jnp.zeros((AUG_WIDTH - n_blk - 8, tq), F32)], axis=0).T
    qs_ref[...] = stack_queries([(extra_t * lanes_ref[0, h:h + 1, :]).astype(BF16) for h in range(hg)])

    ck = SEL_CHUNK
    assert ck == tq and ck == kmq_ref.shape[0]

    def sel_part(c, n=1, mask=None, bias=None):
        k = ks_ref[0, 0, pl.ds(pl.multiple_of(c * ck, ck), n * ck), :]
        return k, [vsT_ref[0, 0, c + i] for i in range(n)], mask, bias

    reset(SEL)

    odd = qi % 2 == 1
    attend(SEL, qs_ref,
           [sel_part(qi, mask=lambda: kmq_ref[...] <= 0),
            sel_part(jnp.maximum(qi - 1, 0), bias=jnp.where(odd, 0.0, NEG_INF))])

    @pl.loop(0, qi // 2)
    def _(i):
        attend(SEL, qs_ref, [sel_part(2 * i, n=2)])

    o_slc = result(SEL)
    for h in range(hg):
        cs = slice(h * tq, (h + 1) * tq)
        o_h = part_ref[:, cs] + gate(h, 1) * o_slc[:, cs]
        o_ref[:, h * dh:(h + 1) * dh] = o_h.T.astype(o_ref.dtype)


def nsa_attention(z, gl, gate_b, q_norm, kcmp, vcmpT, ks, vsT, kw, vwT, *, B, T):
    G, hg, dh, tq = NSA_KV_HEADS, NSA_GROUP, HEAD_DIM, Q_TILE
    nq = T // tq
    nc = kcmp.shape[2]
    n_blk = T // SEL_BLK
    kdim = dh + AUG_WIDTH
    cols = hg * tq
    assert SEL_CHUNK == WIN_CHUNK == tq
    assert SEL_CHUNK % SEL_BLK == 0 and T % SEL_CHUNK == 0 and gl.shape[1] == G * LANES
    assert SEL_CHUNK % tq == 0 and n_blk % 8 == 0 and n_blk + 8 <= AUG_WIDTH

    slopes = np.exp2(-8.0 * (np.arange(NSA_HEADS, dtype=np.float64) + 1.0) / NSA_HEADS).reshape(G, hg)
    lanes = np.zeros((G, 2 * hg, AUG_WIDTH), np.float32)
    lanes[:, :hg, :n_blk] = 1.0
    for i, piece in enumerate(_bf16_pieces(slopes * LOG2E, SLOPE_PIECES)):
        for lane in (n_blk + i, n_blk + SLOPE_PIECES + i):
            lanes[:, :hg, lane] = piece
            lanes[:, hg:, lane] = piece
    cs = np.arange(nc)[:, None] * CMP_STRIDE
    bs = np.arange(n_blk)[None, :] * SEL_BLK
    overlap = np.clip(np.minimum(cs + CMP_LEN, bs + SEL_BLK) - np.maximum(cs, bs), 0, None) / CMP_LEN
    overlap[(T - CMP_LEN) // CMP_STRIDE + 1:] = 0.0
    ovT = jnp.asarray(overlap.T, dtype=BF16)
    hc = NSA_COLS_PER_DOT
    kmq = jnp.asarray(np.arange(SEL_CHUNK)[:, None] - (np.arange(hc)[None, :] % tq), dtype=jnp.int32)

    row = lambda b, qi: b * nq + qi
    per_bg = lambda shape: pl.BlockSpec((1, 1) + shape, lambda b, g, qi: (b, g) + (0,) * len(shape))
    const = lambda shape: pl.BlockSpec(shape, lambda b, g, qi: (0,) * len(shape))
    return pl.pallas_call(
        _nsa_body,
        grid=(B, G, nq),
        in_specs=[pl.BlockSpec((tq, hg * dh), lambda b, g, qi: (row(b, qi), g)),
                  pl.BlockSpec((tq, LANES), lambda b, g, qi: (row(b, qi), g)),
                  pl.BlockSpec((1, LANES), lambda b, g, qi: (0, g)),
                  const((1, dh)),
                  pl.BlockSpec((1, 2 * hg, AUG_WIDTH), lambda b, g, qi: (g, 0, 0)),
                  per_bg((nc, kdim)), per_bg((dh, nc)),
                  per_bg((T, kdim)), per_bg((T // SEL_CHUNK, dh, SEL_CHUNK)),
                  per_bg((T, kdim)), per_bg((T // WIN_CHUNK, dh, WIN_CHUNK)),
                  const((n_blk, nc)), const((SEL_CHUNK, hc))],
        out_specs=pl.BlockSpec((tq, hg * dh), lambda b, g, qi: (row(b, qi), g)),
        out_shape=jax.ShapeDtypeStruct((B * T, NSA_WIDTH), BF16),
        scratch_shapes=[pltpu.VMEM((n_blk, tq), F32), pltpu.VMEM((LANES, tq), F32),
                        pltpu.VMEM((cols, kdim), BF16), pltpu.VMEM((cols, kdim), BF16),
                        pltpu.VMEM((2, 1, cols), F32), pltpu.VMEM((2, 1, cols), F32),
                        pltpu.VMEM((2, dh, cols), F32), pltpu.VMEM((dh, cols), F32)],
        compiler_params=_cparams("parallel", "parallel", "arbitrary"),
        name="nsa_attention",
    )(z, gl, gate_b, q_norm, jnp.asarray(lanes), kcmp, vcmpT, ks, vsT, kw, vwT, ovT, kmq)


def _conv_body(a_ref, b_ref, ah_ref, bh_ref, w_ref, cb_ref, lg_ref, lb_ref, o_ref,
               h_ref, g_ref, y_ref, *, tt):
    C = a_ref.shape[1]
    halo = ah_ref[...] * jax.nn.sigmoid(bh_ref[...])
    h_ref[0:CONV_HALO, :] = jnp.where(pl.program_id(1) == 0, 0.0, halo)
    h_ref[CONV_HALO:CONV_HALO + tt, :] = a_ref[...] * jax.nn.sigmoid(b_ref[...])
    h_ref[CONV_HALO + tt:CONV_HALO + tt + SUBLANES, :] = jnp.zeros((SUBLANES, C), F32)
    lead = CONV_HALO - (CONV_KERNEL - 1)
    rows = tt + SUBLANES
    for s in range(C // LANES):
        cs = slice(s * LANES, (s + 1) * LANES)
        acc = None
        for r in range(SUBLANES):
            group = None
            for o in range(lead, lead + CONV_KERNEL):
                if o % SUBLANES == r:
                    term = w_ref[o - lead:o - lead + 1, cs] * h_ref[o - r:o - r + rows, cs]
                    group = term if group is None else group + term
            if r == 0:
                piece = group[0:tt, :]
            else:
                g_ref[r] = group
                piece = g_ref[r, r:r + tt, :]
            acc = piece if acc is None else acc + piece
        y_ref[:, cs] = acc + cb_ref[:, cs]
    y = y_ref[...]
    mu = jnp.mean(y, axis=-1, keepdims=True)
    var = jnp.mean(jnp.square(y - mu), axis=-1, keepdims=True)
    yn = (y - mu) * lax.rsqrt(var + LN_EPS) * lg_ref[...] + lb_ref[...]
    o_ref[...] = _silu(yn).astype(o_ref.dtype)


def conformer_conv(z, conv_w, conv_b, ln_g, ln_b, *, B, T, col0, C, tt=128):
    nt = T // tt
    ca = col0 // C
    hb = tt // CONV_HALO
    row = lambda b, t: b * nt + t
    halo_row = lambda b, t: jnp.maximum(row(b, t) * hb - 1, 0)
    const = lambda shape: pl.BlockSpec(shape, lambda b, t: (0,) * len(shape))
    return pl.pallas_call(
        functools.partial(_conv_body, tt=tt),
        grid=(B, nt),
        in_specs=[pl.BlockSpec((tt, C), lambda b, t: (row(b, t), ca)),
                  pl.BlockSpec((tt, C), lambda b, t: (row(b, t), ca + 1)),
                  pl.BlockSpec((CONV_HALO, C), lambda b, t: (halo_row(b, t), ca)),
                  pl.BlockSpec((CONV_HALO, C), lambda b, t: (halo_row(b, t), ca + 1)),
                  const((CONV_KERNEL, C)), const((1, C)), const((1, C)), const((1, C))],
        out_specs=pl.BlockSpec((tt, C), lambda b, t: (row(b, t), 0)),
        out_shape=jax.ShapeDtypeStruct((B * T, C), BF16),
        scratch_shapes=[pltpu.VMEM((CONV_HALO + tt + SUBLANES, C), F32),
                        pltpu.VMEM((SUBLANES, tt + SUBLANES, LANES), F32),
                        pltpu.VMEM((tt, C), F32)],
        compiler_params=_cparams("parallel", "parallel"),
        name="conformer_conv",
    )(z, z, z, z, conv_w, conv_b, ln_g, ln_b)


def _mem_attn_body(x_ref, gq_ref, wq_ref, kv_ref, qn_ref, kn_ref, wo_ref, gn_ref, o_ref, on_ref):
    dh = HEAD_DIM
    width = MEM_HEADS * dh
    scale = dh ** -0.5
    q_all = jnp.dot(_rms(x_ref[...], gq_ref[...]).astype(BF16), wq_ref[...], preferred_element_type=F32)
    heads = [slice(h * dh, (h + 1) * dh) for h in range(MEM_HEADS)]
    scores = []
    for cs in heads:
        q = (_rms(q_all[:, cs], qn_ref[...]) * scale).astype(BF16)
        k = _rms(kv_ref[:, cs], kn_ref[...]).astype(BF16)
        scores.append(lax.dot_general(q, k, (((1,), (1,)), ((), ())), preferred_element_type=F32))
    outs = []
    for h, s in enumerate(scores):
        v = kv_ref[:, width + h * dh:width + (h + 1) * dh].astype(BF16)
        p = jnp.exp(s - jnp.max(s, axis=-1, keepdims=True))
        inv = 1.0 / jnp.sum(p, axis=-1, keepdims=True)
        outs.append((jnp.dot(p.astype(BF16), v, preferred_element_type=F32) * inv).astype(BF16))
    o = jnp.concatenate(outs, axis=1)
    y = x_ref[...] + jnp.dot(o, wo_ref[...], preferred_element_type=F32)
    o_ref[...] = y
    on_ref[...] = _rms(y, gn_ref[...]).astype(on_ref.dtype)


def mem_attention(x, q_gain, w_mq, kvm, mq_norm, mk_norm, w_mo, next_norm, *, B, T, M, tt=256):
    D = x.shape[1]
    width = MEM_HEADS * HEAD_DIM
    nt = T // tt
    row = lambda b, t: b * nt + t
    const = lambda shape: pl.BlockSpec(shape, lambda b, t: (0,) * len(shape))
    return pl.pallas_call(
        _mem_attn_body,
        grid=(B, nt),
        in_specs=[pl.BlockSpec((tt, D), lambda b, t: (row(b, t), 0)),
                  const((1, D)), const((D, width)),
                  pl.BlockSpec((M, 2 * width), lambda b, t: (b, 0)),
                  const((1, HEAD_DIM)), const((1, HEAD_DIM)), const((width, D)),
                  const((1, D))],
        out_specs=[pl.BlockSpec((tt, D), lambda b, t: (row(b, t), 0)),
                   pl.BlockSpec((tt, D), lambda b, t: (row(b, t), 0))],
        out_shape=[jax.ShapeDtypeStruct((B * T, D), F32),
                   jax.ShapeDtypeStruct((B * T, D), BF16)],
        compiler_params=_cparams("parallel", "parallel"),
        name="mem_attention",
    )(x, q_gain, w_mq, kvm, mq_norm, mk_norm, w_mo, next_norm)


def kernel(x, mem, norm_mix, w_in, gate_b, q_norm, k_norm_cmp, k_norm_slc, k_norm_win, cmp_pos_k, cmp_pos_v, cmp_k_w1, cmp_k_w2, cmp_v_w1, cmp_v_w2, conv_w, conv_b, conv_ln_g, conv_ln_b, w_out, norm_mem_q, norm_mem_kv, w_mq, w_mk, w_mv, mq_norm, mk_norm, w_mo, norm_ffn, w_gate, w_up, w_down):
    B, T, D = x.shape
    M = mem.shape[1]
    depth = w_in.shape[0]
    G, hg = NSA_KV_HEADS, NSA_GROUP
    C = D - NSA_WIDTH
    n_gate = N_BRANCH * NSA_HEADS
    kv_all = 6 * KV_WIDTH
    assert w_in.shape[2] == NSA_WIDTH + kv_all + n_gate + 2 * C
    assert C == NSA_WIDTH, "column blocking below assumes equal NSA and conv widths"

    row1 = lambda v: v.reshape(1, -1)
    xf = x.reshape(B * T, D)
    memf = mem.reshape(B * M, D)
    u0 = NSA_WIDTH + kv_all + n_gate
    for l in range(depth):
        w = w_in[l]
        w_t = jnp.swapaxes(w, 0, 1)
        wg = w[:, NSA_WIDTH + kv_all:u0].reshape(D, G, hg * N_BRANCH)
        wg = jnp.pad(wg, ((0, 0), (0, 0), (0, LANES - hg * N_BRANCH))).reshape(D, G * LANES)
        gb = jnp.pad(gate_b[l].reshape(G, hg * N_BRANCH), ((0, 0), (0, LANES - hg * N_BRANCH))).reshape(1, G * LANES)

        z, gl = in_projection(xf, row1(norm_mix[l]), w_t, wg, tm=1024, tn=512,
                              segments=((0, NSA_WIDTH), (u0, 2 * C), (NSA_WIDTH, kv_all)))
        kv_col0 = NSA_WIDTH + 2 * C
        kc, vc, ks, vsT, kw, vwT = kv_prep(z, row1(k_norm_slc[l]), row1(k_norm_win[l]), B=B, T=T, col0=kv_col0)
        kcmp, vcmpT = compress_kv(kc, vc, cmp_pos_k[l], cmp_pos_v[l],
                                  cmp_k_w1[l].astype(BF16), cmp_k_w2[l].astype(BF16),
                                  cmp_v_w1[l].astype(BF16), cmp_v_w2[l].astype(BF16), row1(k_norm_cmp[l]),
                                  n_blk=T // SEL_BLK)
        o_nsa = nsa_attention(z, gl, gb, row1(q_norm[l]), kcmp, vcmpT, ks, vsT, kw, vwT, B=B, T=T)
        o_conv = conformer_conv(z, conv_w[l], row1(conv_b[l]), row1(conv_ln_g[l]), row1(conv_ln_b[l]),
                                B=B, T=T, col0=NSA_WIDTH, C=C)
        xf = concat_matmul_residual(o_nsa, o_conv, w_out[l], xf, tm=2048, tn=512)

        w_mkv = jnp.concatenate([w_mk[l], w_mv[l]], axis=1).astype(BF16)
        kvm = norm_matmul(memf, row1(norm_mem_kv[l]), w_mkv, tm=B * M, tn=512)
        xf, xn = mem_attention(xf, row1(norm_mem_q[l]), w_mq[l].astype(BF16), kvm,
                               row1(mq_norm[l]), row1(mk_norm[l]), w_mo[l].astype(BF16),
                               row1(norm_ffn[l]), B=B, T=T, M=M)

        hidden = swiglu_up(xn, w_gate[l], w_up[l], tm=1024, tn=512)
        xf = matmul_residual(hidden, w_down[l].astype(BF16), xf, tm=1024, tn=512)
    return xf.reshape(B, T, D)
```

```python
import functools

import ml_dtypes
import numpy as np
import jax
import jax.numpy as jnp
from jax import lax
from jax.experimental import pallas as pl
from jax.experimental.pallas import tpu as pltpu

F32 = jnp.float32
BF16 = jnp.bfloat16

HEAD_DIM = 128
NSA_HEADS = 8
NSA_KV_HEADS = 2
NSA_GROUP = NSA_HEADS // NSA_KV_HEADS
NSA_WIDTH = NSA_HEADS * HEAD_DIM
KV_WIDTH = NSA_KV_HEADS * HEAD_DIM
N_BRANCH = 3
CONV_KERNEL = 31
CMP_LEN = 32
CMP_STRIDE = 16
SEL_BLK = 64
N_SEL = 16
WINDOW = 512
MEM_HEADS = 4
NEG_INF = -1e30
FORCE_BONUS = 1e3
RMS_EPS = 1e-6
LN_EPS = 1e-5

LANES = 128
SUBLANES = 8
VMEM_LIMIT_BYTES = 56 * 1024 * 1024

Q_TILE = 256
SEL_CHUNK = 256
WIN_CHUNK = 256
NSA_COLS_PER_DOT = 512
CONV_HALO = 32


def _cparams(*sem):
    return pltpu.CompilerParams(dimension_semantics=sem, vmem_limit_bytes=VMEM_LIMIT_BYTES)


def _rms(x, g, eps=RMS_EPS):
    ms = jnp.mean(x * x, axis=-1, keepdims=True)
    return x * lax.rsqrt(ms + eps) * g


def _silu(x):
    return x * jax.nn.sigmoid(x)


def _norm_to_scratch(x_ref, g_ref, xn_ref):
    @pl.when(pl.program_id(1) == 0)
    def _():
        xn_ref[...] = _rms(x_ref[...], g_ref[...]).astype(xn_ref.dtype)


def _norm_mm_body(x_ref, g_ref, w_ref, o_ref, xn_ref):
    _norm_to_scratch(x_ref, g_ref, xn_ref)
    o_ref[...] = jnp.dot(xn_ref[...], w_ref[...], preferred_element_type=F32).astype(o_ref.dtype)


def _rmsnorm_body(x_ref, g_ref, o_ref):
    o_ref[...] = _rms(x_ref[...], g_ref[...]).astype(o_ref.dtype)


def rmsnorm_bf16(x, g, *, tm):
    M, K = x.shape
    return pl.pallas_call(
        _rmsnorm_body,
        grid=(M // tm,),
        in_specs=[pl.BlockSpec((tm, K), lambda i: (i, 0)), pl.BlockSpec((1, K), lambda i: (0, 0))],
        out_specs=pl.BlockSpec((tm, K), lambda i: (i, 0)),
        out_shape=jax.ShapeDtypeStruct((M, K), BF16),
        compiler_params=_cparams("parallel"),
        name="rmsnorm",
    )(x, g)


def _in_proj_body(xn_ref, wt_ref, wg_ref, o_ref, og_ref):
    @pl.when(pl.program_id(1) == 0)
    def _():
        og_ref[...] = jnp.dot(xn_ref[...], wg_ref[...].astype(BF16), preferred_element_type=F32)

    o_ref[...] = lax.dot_general(xn_ref[...], wt_ref[...].astype(BF16), (((1,), (1,)), ((), ())),
                                 preferred_element_type=F32)


def _swiglu_body(x_ref, wg_ref, wu_ref, o_ref):
    x = x_ref[...]
    a = jnp.dot(x, wg_ref[...].astype(BF16), preferred_element_type=F32)
    b = jnp.dot(x, wu_ref[...].astype(BF16), preferred_element_type=F32)
    o_ref[...] = (_silu(a) * b).astype(o_ref.dtype)


def norm_matmul(x, g, w, *, tm, tn, out_dtype=F32):
    M, K = x.shape
    N = w.shape[1]
    return pl.pallas_call(
        _norm_mm_body,
        grid=(M // tm, N // tn),
        in_specs=[pl.BlockSpec((tm, K), lambda i, j: (i, 0)),
                  pl.BlockSpec((1, K), lambda i, j: (0, 0)),
                  pl.BlockSpec((K, tn), lambda i, j: (0, j))],
        out_specs=pl.BlockSpec((tm, tn), lambda i, j: (i, j)),
        out_shape=jax.ShapeDtypeStruct((M, N), out_dtype),
        scratch_shapes=[pltpu.VMEM((tm, K), BF16)],
        compiler_params=_cparams("parallel", "arbitrary"),
        name="norm_matmul",
    )(x, g, w)


def in_projection(xn, w_t, wg, *, segments, tm, tn):
    M, K = xn.shape
    NG = wg.shape[1]
    assert all(start % SUBLANES == 0 and size % tn == 0 for start, size in segments)
    N = sum(size for _, size in segments)

    def row_start(i, j):
        group, first = 0, 0
        for seg_start, size in segments:
            group = jnp.where(j >= first, seg_start // SUBLANES + (j - first) * (tn // SUBLANES), group)
            first += size // tn
        return (group * SUBLANES, 0)

    return pl.pallas_call(
        _in_proj_body,
        grid=(M // tm, N // tn),
        in_specs=[pl.BlockSpec((tm, K), lambda i, j: (i, 0)),
                  pl.BlockSpec((pl.Element(tn), pl.Element(K)), row_start),
                  pl.BlockSpec((K, NG), lambda i, j: (0, 0))],
        out_specs=[pl.BlockSpec((tm, tn), lambda i, j: (i, j)),
                   pl.BlockSpec((tm, NG), lambda i, j: (i, 0))],
        out_shape=[jax.ShapeDtypeStruct((M, N), F32),
                   jax.ShapeDtypeStruct((M, NG), F32)],
        compiler_params=_cparams("parallel", "arbitrary"),
        name="in_proj",
    )(xn, w_t, wg)


def swiglu_up(xn, w_gate, w_up, *, tm, tn):
    M, K = xn.shape
    N = w_gate.shape[1]
    return pl.pallas_call(
        _swiglu_body,
        grid=(M // tm, N // tn),
        in_specs=[pl.BlockSpec((tm, K), lambda i, j: (i, 0)),
                  pl.BlockSpec((K, tn), lambda i, j: (0, j)),
                  pl.BlockSpec((K, tn), lambda i, j: (0, j))],
        out_specs=pl.BlockSpec((tm, tn), lambda i, j: (i, j)),
        out_shape=jax.ShapeDtypeStruct((M, N), BF16),
        compiler_params=_cparams("parallel", "arbitrary"),
        name="ffn_up",
    )(xn, w_gate, w_up)


def _mm_res_body(a_ref, w_ref, r_ref, o_ref):
    o_ref[...] = r_ref[...] + jnp.dot(a_ref[...], w_ref[...], preferred_element_type=F32)


def matmul_residual(a, w, res, *, tm, tn):
    M, K = a.shape
    N = w.shape[1]
    return pl.pallas_call(
        _mm_res_body,
        grid=(M // tm, N // tn),
        in_specs=[pl.BlockSpec((tm, K), lambda i, j: (i, 0)),
                  pl.BlockSpec((K, tn), lambda i, j: (0, j)),
                  pl.BlockSpec((tm, tn), lambda i, j: (i, j))],
        out_specs=pl.BlockSpec((tm, tn), lambda i, j: (i, j)),
        out_shape=jax.ShapeDtypeStruct((M, N), F32),
        compiler_params=_cparams("parallel", "arbitrary"),
        name="matmul_residual",
    )(a, w, res)


def _mm2_res_body(a_ref, b_ref, wa_ref, wb_ref, r_ref, o_ref):
    acc = jnp.dot(a_ref[...], wa_ref[...].astype(BF16), preferred_element_type=F32)
    acc = acc + jnp.dot(b_ref[...], wb_ref[...].astype(BF16), preferred_element_type=F32)
    o_ref[...] = r_ref[...] + acc


def concat_matmul_residual(a, b, w, res, *, tm, tn):
    M, Ka = a.shape
    Kb = b.shape[1]
    assert Ka == Kb and w.shape[0] == Ka + Kb
    N = w.shape[1]
    return pl.pallas_call(
        _mm2_res_body,
        grid=(M // tm, N // tn),
        in_specs=[pl.BlockSpec((tm, Ka), lambda i, j: (i, 0)),
                  pl.BlockSpec((tm, Kb), lambda i, j: (i, 0)),
                  pl.BlockSpec((Ka, tn), lambda i, j: (0, j)),
                  pl.BlockSpec((Kb, tn), lambda i, j: (1, j)),
                  pl.BlockSpec((tm, tn), lambda i, j: (i, j))],
        out_specs=pl.BlockSpec((tm, tn), lambda i, j: (i, j)),
        out_shape=jax.ShapeDtypeStruct((M, N), F32),
        compiler_params=_cparams("parallel", "arbitrary"),
        name="out_proj",
    )(a, b, w, w, res)


AUG_WIDTH = LANES
SLOPE_PIECES = 4
LOG2E = float(np.log2(np.e))


def _bf16_pieces(x, n):
    out, rem = [], np.asarray(x, np.float64)
    for _ in range(n):
        piece = rem.astype(ml_dtypes.bfloat16).astype(np.float64)
        out.append(piece.astype(np.float32))
        rem = rem - piece
    return out


def _position_lanes(pos_hi, pos_lo, onehot, n_blk, shape):
    lane = lax.broadcasted_iota(jnp.int32, shape, 1)
    p = SLOPE_PIECES
    extra = jnp.where((lane >= n_blk) & (lane < n_blk + p), pos_hi,
                      jnp.where((lane >= n_blk + p) & (lane < n_blk + 2 * p), pos_lo, 0.0))
    if onehot is not None:
        extra = jnp.where(lane < n_blk, onehot, extra)
    return extra.astype(BF16)


def _kv_prep_body(c_ref, s_ref, w_ref, kns_ref, knw_ref,
                  kc_o, vc_o, ks_o, vsT_o, kw_o, vwT_o, slab_ref, *, tt, n_blk):
    dh = HEAD_DIM
    shape = (tt, AUG_WIDTH)
    pos = pl.program_id(1) * tt + lax.broadcasted_iota(jnp.int32, shape, 0)
    lane = lax.broadcasted_iota(jnp.int32, shape, 1)
    blk = jnp.right_shift(pos, SEL_BLK.bit_length() - 1)
    pos_hi = (blk * SEL_BLK).astype(F32)
    pos_lo = (pos & (SEL_BLK - 1)).astype(F32)
    aug_sel = _position_lanes(pos_hi, pos_lo, (blk == lane).astype(F32), n_blk, shape)
    aug_win = _position_lanes(pos_hi, pos_lo, None, n_blk, shape)
    for g in range(NSA_KV_HEADS):
        k_cols = slice(g * dh, (g + 1) * dh)
        v_cols = slice(KV_WIDTH + g * dh, KV_WIDTH + (g + 1) * dh)
        for slab, (cols, out) in enumerate(((k_cols, kc_o), (v_cols, vc_o))):
            slab_ref[slab] = c_ref[:, cols]
            for i in range(CMP_STRIDE):
                rows = pl.ds(i, tt // CMP_STRIDE, stride=CMP_STRIDE)
                out[0, g, :, i * dh:(i + 1) * dh] = slab_ref[slab, rows, :]
        ks_o[0, g, :, 0:dh] = _rms(s_ref[:, k_cols], kns_ref[...]).astype(BF16)
        ks_o[0, g, :, dh:dh + AUG_WIDTH] = aug_sel
        kw_o[0, g, :, 0:dh] = _rms(w_ref[:, k_cols], knw_ref[...]).astype(BF16)
        kw_o[0, g, :, dh:dh + AUG_WIDTH] = aug_win
        vsT = s_ref[:, v_cols].T.astype(BF16)
        for c in range(tt // SEL_CHUNK):
            vsT_o[0, g, c] = vsT[:, c * SEL_CHUNK:(c + 1) * SEL_CHUNK]
        vwT = w_ref[:, v_cols].T.astype(BF16)
        for c in range(tt // WIN_CHUNK):
            vwT_o[0, g, c] = vwT[:, c * WIN_CHUNK:(c + 1) * WIN_CHUNK]


def kv_prep(z, kn_slc, kn_win, *, B, T, col0, tt=256):
    G, dh = NSA_KV_HEADS, HEAD_DIM
    nt = T // tt
    n_blk = T // SEL_BLK
    assert n_blk + 2 * SLOPE_PIECES <= AUG_WIDTH
    cb = col0 // (2 * KV_WIDTH)
    row = lambda b, t: b * nt + t
    kdim = dh + AUG_WIDTH
    return pl.pallas_call(
        functools.partial(_kv_prep_body, tt=tt, n_blk=n_blk),
        grid=(B, nt),
        in_specs=[pl.BlockSpec((tt, 2 * KV_WIDTH), lambda b, t: (row(b, t), cb)),
                  pl.BlockSpec((tt, 2 * KV_WIDTH), lambda b, t: (row(b, t), cb + 1)),
                  pl.BlockSpec((tt, 2 * KV_WIDTH), lambda b, t: (row(b, t), cb + 2)),
                  pl.BlockSpec((1, dh), lambda b, t: (0, 0)),
                  pl.BlockSpec((1, dh), lambda b, t: (0, 0))],
        out_specs=[pl.BlockSpec((1, G, tt // CMP_STRIDE, CMP_STRIDE * dh), lambda b, t: (b, 0, t, 0)),
                   pl.BlockSpec((1, G, tt // CMP_STRIDE, CMP_STRIDE * dh), lambda b, t: (b, 0, t, 0)),
                   pl.BlockSpec((1, G, tt, kdim), lambda b, t: (b, 0, t, 0)),
                   pl.BlockSpec((1, G, tt // SEL_CHUNK, dh, SEL_CHUNK), lambda b, t: (b, 0, t, 0, 0)),
                   pl.BlockSpec((1, G, tt, kdim), lambda b, t: (b, 0, t, 0)),
                   pl.BlockSpec((1, G, tt // WIN_CHUNK, dh, WIN_CHUNK), lambda b, t: (b, 0, t, 0, 0))],
        out_shape=[jax.ShapeDtypeStruct((B, G, T // CMP_STRIDE, CMP_STRIDE * dh), F32),
                   jax.ShapeDtypeStruct((B, G, T // CMP_STRIDE, CMP_STRIDE * dh), F32),
                   jax.ShapeDtypeStruct((B, G, T, kdim), BF16),
                   jax.ShapeDtypeStruct((B, G, T // SEL_CHUNK, dh, SEL_CHUNK), BF16),
                   jax.ShapeDtypeStruct((B, G, T, kdim), BF16),
                   jax.ShapeDtypeStruct((B, G, T // WIN_CHUNK, dh, WIN_CHUNK), BF16)],
        scratch_shapes=[pltpu.VMEM((2, tt, dh), F32)],
        compiler_params=_cparams("parallel", "parallel"),
        name="kv_prep",
    )(z, z, z, kn_slc, kn_win)


def _compress_body(xk_ref, xv_ref, pk_ref, pv_ref, w1k_ref, w2k_ref, w1v_ref, w2v_ref, kn_ref,
                   kc_o, vcT_o, *, n_blk):
    dh = HEAD_DIM

    def phi(x_ref, p_ref, w1_ref, w2_ref):
        x = x_ref[0, 0]
        first = jnp.dot((x + p_ref[0:1, :]).astype(BF16), w1_ref[0], preferred_element_type=F32)
        second = jnp.dot((x + p_ref[1:2, :]).astype(BF16), w1_ref[1], preferred_element_type=F32)
        n = second.shape[0]
        h = first + pltpu.roll(second, n - 1, 0)
        return jnp.dot(_silu(h).astype(BF16), w2_ref[...], preferred_element_type=F32)

    k = _rms(phi(xk_ref, pk_ref, w1k_ref, w2k_ref), kn_ref[...])
    nc = k.shape[0]
    shape = (nc, AUG_WIDTH)
    start = (lax.broadcasted_iota(jnp.int32, shape, 0) * CMP_STRIDE).astype(F32)
    centre = jnp.full(shape, (CMP_LEN - 1) / 2.0, F32)
    kc_o[0, 0, :, 0:dh] = k.astype(BF16)
    kc_o[0, 0, :, dh:dh + AUG_WIDTH] = _position_lanes(start, centre, None, n_blk, shape)
    vcT_o[0, 0] = phi(xv_ref, pv_ref, w1v_ref, w2v_ref).T.astype(BF16)


def compress_kv(xk, xv, pos_k, pos_v, w1k, w2k, w1v, w2v, kn_cmp, *, n_blk):
    B, G, nc, half = xk.shape
    dh = half // CMP_STRIDE
    assert CMP_LEN == 2 * CMP_STRIDE
    x_spec = pl.BlockSpec((1, 1, nc, half), lambda b, g: (b, g, 0, 0))
    full = lambda shape: pl.BlockSpec(shape, lambda b, g: (0,) * len(shape))
    return pl.pallas_call(
        functools.partial(_compress_body, n_blk=n_blk),
        grid=(B, G),
        in_specs=[x_spec, x_spec, full((2, half)), full((2, half)),
                  full((2, half, dh)), full((dh, dh)), full((2, half, dh)), full((dh, dh)),
                  full((1, dh))],
        out_specs=[pl.BlockSpec((1, 1, nc, dh + AUG_WIDTH), lambda b, g: (b, g, 0, 0)),
                   pl.BlockSpec((1, 1, dh, nc), lambda b, g: (b, g, 0, 0))],
        out_shape=[jax.ShapeDtypeStruct((B, G, nc, dh + AUG_WIDTH), BF16),
                   jax.ShapeDtypeStruct((B, G, dh, nc), BF16)],
        compiler_params=_cparams("parallel", "parallel"),
        name="compress_kv",
    )(xk, xv, pos_k.reshape(2, half), pos_v.reshape(2, half),
      w1k.reshape(2, half, dh), w2k, w1v.reshape(2, half, dh), w2v, kn_cmp)


def _nsa_body(q_ref, gl_ref, gb_ref, qn_ref, lanes_ref, kc_ref, vcT_ref,
              ks_ref, vsT_ref, kw_ref, vwT_ref, ovT_ref, kmq_ref, o_ref,
              score_ref, gate_ref, qc_ref, qs_ref, m_ref, l_ref, acc_ref, part_ref):
    dh, hg, tq = HEAD_DIM, NSA_GROUP, Q_TILE
    cols = hg * tq
    qi = pl.program_id(2)
    t0 = qi * tq
    contract_last = (((1,), (1,)), ((), ()))
    n_blk = ovT_ref.shape[0]

    scale = dh ** -0.5 * LOG2E
    q_heads = [(_rms(q_ref[:, h * dh:(h + 1) * dh], qn_ref[...]) * scale).astype(BF16)
               for h in range(hg)]

    def stack_queries(aug_heads):
        return jnp.concatenate([jnp.concatenate([q_heads[h], aug_heads[h]], axis=1)
                                for h in range(hg)], axis=0)

    hc = kmq_ref.shape[1]
    assert cols % hc == 0 and hc % tq == 0

    def reset(b):
        m_ref[b] = jnp.full((1, cols), NEG_INF, F32)
        l_ref[b] = jnp.zeros((1, cols), F32)
        acc_ref[b] = jnp.zeros((dh, cols), F32)

    def attend(b, queries_ref, parts):
        groups = [slice(i * hc, (i + 1) * hc) for i in range(cols // hc)]
        scores = [[lax.dot_general(k, queries_ref[cs, :], contract_last, preferred_element_type=F32)
                   for cs in groups] for k, _, _, _ in parts]
        for (k, v_pieces, mask, bias), part_scores in zip(parts, scores):
            for cs, s in zip(groups, part_scores):
                if mask is not None:
                    s = jnp.where(mask(), s, NEG_INF)
                if bias is not None:
                    s = s + bias
                m_old = m_ref[b, :, cs]
                m_new = jnp.maximum(m_old, jnp.max(s, axis=0, keepdims=True))
                alpha = jnp.exp2(m_old - m_new)
                p = jnp.exp2(s - m_new)
                m_ref[b, :, cs] = m_new
                l_ref[b, :, cs] = alpha * l_ref[b, :, cs] + jnp.sum(p, axis=0, keepdims=True)
                p = p.astype(BF16)
                pv = None
                for i, vT in enumerate(v_pieces):
                    n = vT.shape[1]
                    d = jnp.dot(vT, p[i * n:(i + 1) * n, :], preferred_element_type=F32)
                    pv = d if pv is None else pv + d
                acc_ref[b, :, cs] = alpha * acc_ref[b, :, cs] + pv

    def result(b):
        return acc_ref[b] * (1.0 / l_ref[b])

    nc = kc_ref.shape[2]
    qc_ref[...] = stack_queries([jnp.broadcast_to(lanes_ref[0, hg + h:hg + h + 1, :], (tq, AUG_WIDTH)).astype(BF16)
                                 for h in range(hg)])
    s = lax.dot_general(kc_ref[0, 0], qc_ref[...], contract_last, preferred_element_type=F32)
    c_idx = lax.broadcasted_iota(jnp.int32, (nc, cols), 0)
    t_c = t0 + (lax.broadcasted_iota(jnp.int32, (nc, cols), 1) & (tq - 1))
    mask = (c_idx * CMP_STRIDE + (CMP_LEN - 1)) <= t_c
    s = jnp.where(mask, s, NEG_INF)
    m_c = jnp.max(s, axis=0, keepdims=True)
    p_c = jnp.where(mask, jnp.exp2(s - m_c), 0.0)
    l_c = jnp.sum(p_c, axis=0, keepdims=True)
    p_c = p_c * jnp.where(l_c > 0.0, 1.0 / l_c, 0.0)
    o_cmp = jnp.dot(vcT_ref[0, 0], p_c.astype(BF16), preferred_element_type=F32)

    cw = WIN_CHUNK
    assert cw == tq and WINDOW % cw == 0 and cw == kmq_ref.shape[0]
    n_back = WINDOW // cw
    WIN, SEL = 0, 1

    def win_part(c, mask=None, bias=None):
        k = kw_ref[0, 0, pl.ds(pl.multiple_of(c * cw, cw), cw), :]
        return k, [vwT_ref[0, 0, c]], mask, bias

    threshold = jnp.where(qi >= n_back, 0, cw)
    reset(WIN)
    attend(WIN, qc_ref,
           [win_part(qi, mask=lambda: kmq_ref[...] <= 0)]
           + [win_part(jnp.maximum(qi - j, 0), bias=jnp.where(qi >= j, 0.0, NEG_INF))
              for j in range(1, n_back)]
           + [win_part(jnp.maximum(qi - n_back, 0), mask=lambda: kmq_ref[...] > threshold)])

    gate_ref[...] = jax.nn.sigmoid(gl_ref[...] + gb_ref[...]).T

    def gate(h, branch):
        r = h * N_BRANCH + branch
        return gate_ref[r:r + 1, :]

    o_win = result(WIN)
    for h in range(hg):
        cs = slice(h * tq, (h + 1) * tq)
        part_ref[:, cs] = gate(h, 0) * o_cmp[:, cs] + gate(h, 2) * o_win[:, cs]

    p_sum = p_c[:, 0:tq]
    for h in range(1, hg):
        p_sum = p_sum + p_c[:, h * tq:(h + 1) * tq]
    hi = p_sum.astype(BF16)
    rem = p_sum - hi.astype(F32)
    mid = rem.astype(BF16)
    lo = (rem - mid.astype(F32)).astype(BF16)
    ovT = ovT_ref[...]
    imp = (jnp.dot(ovT, hi, preferred_element_type=F32)
           + jnp.dot(ovT, mid, preferred_element_type=F32)
           + jnp.dot(ovT, lo, preferred_element_type=F32))
    blk = lax.broadcasted_iota(jnp.int32, (n_blk, tq), 0)
    t_b = t0 + lax.broadcasted_iota(jnp.int32, (n_blk, tq), 1)
    cur = jnp.right_shift(t_b, SEL_BLK.bit_length() - 1)
    valid = blk * SEL_BLK <= t_b
    forced = (blk == 0) | (blk == cur) | (blk == cur - 1)
    score = jnp.where(valid, imp + FORCE_BONUS * forced.astype(F32), NEG_INF)
    score_ref[...] = score
    rank = jnp.zeros((n_blk, tq), F32)
    for j in range(n_blk):
        sj = score_ref[j:j + 1, :]
        beats = (sj > score) | ((sj == score) & (blk > j))
        rank = rank + beats.astype(F32)
    block_bias = jnp.where(rank < float(min(N_SEL, n_blk)), 0.0, NEG_INF)

    assert 2 * SLOPE_PIECES <= 8
    ones_rows = (lax.broadcasted_iota(jnp.int32, (8, tq), 0) < 2 * SLOPE_PIECES).astype(F32)
    extra_t = jnp.concatenate([block_bias, ones_rows,
                               jnp.zeros((AUG_WIDTH - n_blk - 8, tq), F32)], axis=0).T
    qs_ref[...] = stack_queries([(extra_t * lanes_ref[0, h:h + 1, :]).astype(BF16) for h in range(hg)])

    ck = SEL_CHUNK
    assert ck == tq and ck == kmq_ref.shape[0]

    def sel_part(c, n=1, mask=None, bias=None):
        k = ks_ref[0, 0, pl.ds(pl.multiple_of(c * ck, ck), n * ck), :]
        return k, [vsT_ref[0, 0, c + i] for i in range(n)], mask, bias

    reset(SEL)

    odd = qi % 2 == 1
    attend(SEL, qs_ref,
           [sel_part(qi, mask=lambda: kmq_ref[...] <= 0),
            sel_part(jnp.maximum(qi - 1, 0), bias=jnp.where(odd, 0.0, NEG_INF))])

    @pl.loop(0, qi // 2)
    def _(i):
        attend(SEL, qs_ref, [sel_part(2 * i, n=2)])

    o_slc = result(SEL)
    for h in range(hg):
        cs = slice(h * tq, (h + 1) * tq)
        o_h = part_ref[:, cs] + gate(h, 1) * o_slc[:, cs]
        o_ref[:, h * dh:(h + 1) * dh] = o_h.T.astype(o_ref.dtype)


def nsa_attention(z, gl, gate_b, q_norm, kcmp, vcmpT, ks, vsT, kw, vwT, *, B, T):
    G, hg, dh, tq = NSA_KV_HEADS, NSA_GROUP, HEAD_DIM, Q_TILE
    nq = T // tq
    nc = kcmp.shape[2]
    n_blk = T // SEL_BLK
    kdim = dh + AUG_WIDTH
    cols = hg * tq
    assert SEL_CHUNK == WIN_CHUNK == tq
    assert SEL_CHUNK % SEL_BLK == 0 and T % SEL_CHUNK == 0 and gl.shape[1] == G * LANES
    assert SEL_CHUNK % tq == 0 and n_blk % 8 == 0 and n_blk + 8 <= AUG_WIDTH

    slopes = np.exp2(-8.0 * (np.arange(NSA_HEADS, dtype=np.float64) + 1.0) / NSA_HEADS).reshape(G, hg)
    lanes = np.zeros((G, 2 * hg, AUG_WIDTH), np.float32)
    lanes[:, :hg, :n_blk] = 1.0
    for i, piece in enumerate(_bf16_pieces(slopes * LOG2E, SLOPE_PIECES)):
        for lane in (n_blk + i, n_blk + SLOPE_PIECES + i):
            lanes[:, :hg, lane] = piece
            lanes[:, hg:, lane] = piece
    cs = np.arange(nc)[:, None] * CMP_STRIDE
    bs = np.arange(n_blk)[None, :] * SEL_BLK
    overlap = np.clip(np.minimum(cs + CMP_LEN, bs + SEL_BLK) - np.maximum(cs, bs), 0, None) / CMP_LEN
    overlap[(T - CMP_LEN) // CMP_STRIDE + 1:] = 0.0
    ovT = jnp.asarray(overlap.T, dtype=BF16)
    hc = NSA_COLS_PER_DOT
    kmq = jnp.asarray(np.arange(SEL_CHUNK)[:, None] - (np.arange(hc)[None, :] % tq), dtype=jnp.int32)

    row = lambda b, qi: b * nq + qi
    per_bg = lambda shape: pl.BlockSpec((1, 1) + shape, lambda b, g, qi: (b, g) + (0,) * len(shape))
    const = lambda shape: pl.BlockSpec(shape, lambda b, g, qi: (0,) * len(shape))
    return pl.pallas_call(
        _nsa_body,
        grid=(B, G, nq),
        in_specs=[pl.BlockSpec((tq, hg * dh), lambda b, g, qi: (row(b, qi), g)),
                  pl.BlockSpec((tq, LANES), lambda b, g, qi: (row(b, qi), g)),
                  pl.BlockSpec((1, LANES), lambda b, g, qi: (0, g)),
                  const((1, dh)),
                  pl.BlockSpec((1, 2 * hg, AUG_WIDTH), lambda b, g, qi: (g, 0, 0)),
                  per_bg((nc, kdim)), per_bg((dh, nc)),
                  per_bg((T, kdim)), per_bg((T // SEL_CHUNK, dh, SEL_CHUNK)),
                  per_bg((T, kdim)), per_bg((T // WIN_CHUNK, dh, WIN_CHUNK)),
                  const((n_blk, nc)), const((SEL_CHUNK, hc))],
        out_specs=pl.BlockSpec((tq, hg * dh), lambda b, g, qi: (row(b, qi), g)),
        out_shape=jax.ShapeDtypeStruct((B * T, NSA_WIDTH), BF16),
        scratch_shapes=[pltpu.VMEM((n_blk, tq), F32), pltpu.VMEM((LANES, tq), F32),
                        pltpu.VMEM((cols, kdim), BF16), pltpu.VMEM((cols, kdim), BF16),
                        pltpu.VMEM((2, 1, cols), F32), pltpu.VMEM((2, 1, cols), F32),
                        pltpu.VMEM((2, dh, cols), F32), pltpu.VMEM((dh, cols), F32)],
        compiler_params=_cparams("parallel", "parallel", "arbitrary"),
        name="nsa_attention",
    )(z, gl, gate_b, q_norm, jnp.asarray(lanes), kcmp, vcmpT, ks, vsT, kw, vwT, ovT, kmq)


def _conv_body(a_ref, b_ref, ah_ref, bh_ref, w_ref, cb_ref, lg_ref, lb_ref, o_ref,
               h_ref, g_ref, y_ref, *, tt):
    C = a_ref.shape[1]
    halo = ah_ref[...] * jax.nn.sigmoid(bh_ref[...])
    h_ref[0:CONV_HALO, :] = jnp.where(pl.program_id(1) == 0, 0.0, halo)
    h_ref[CONV_HALO:CONV_HALO + tt, :] = a_ref[...] * jax.nn.sigmoid(b_ref[...])
    h_ref[CONV_HALO + tt:CONV_HALO + tt + SUBLANES, :] = jnp.zeros((SUBLANES, C), F32)
    lead = CONV_HALO - (CONV_KERNEL - 1)
    rows = tt + SUBLANES
    for s in range(C // LANES):
        cs = slice(s * LANES, (s + 1) * LANES)
        acc = None
        for r in range(SUBLANES):
            group = None
            for o in range(lead, lead + CONV_KERNEL):
                if o % SUBLANES == r:
                    term = w_ref[o - lead:o - lead + 1, cs] * h_ref[o - r:o - r + rows, cs]
                    group = term if group is None else group + term
            if r == 0:
                piece = group[0:tt, :]
            else:
                g_ref[r] = group
                piece = g_ref[r, r:r + tt, :]
            acc = piece if acc is None else acc + piece
        y_ref[:, cs] = acc + cb_ref[:, cs]
    y = y_ref[...]
    mu = jnp.mean(y, axis=-1, keepdims=True)
    var = jnp.mean(jnp.square(y - mu), axis=-1, keepdims=True)
    yn = (y - mu) * lax.rsqrt(var + LN_EPS) * lg_ref[...] + lb_ref[...]
    o_ref[...] = _silu(yn).astype(o_ref.dtype)


def conformer_conv(z, conv_w, conv_b, ln_g, ln_b, *, B, T, col0, C, tt=128):
    nt = T // tt
    ca = col0 // C
    hb = tt // CONV_HALO
    row = lambda b, t: b * nt + t
    halo_row = lambda b, t: jnp.maximum(row(b, t) * hb - 1, 0)
    const = lambda shape: pl.BlockSpec(shape, lambda b, t: (0,) * len(shape))
    return pl.pallas_call(
        functools.partial(_conv_body, tt=tt),
        grid=(B, nt),
        in_specs=[pl.BlockSpec((tt, C), lambda b, t: (row(b, t), ca)),
                  pl.BlockSpec((tt, C), lambda b, t: (row(b, t), ca + 1)),
                  pl.BlockSpec((CONV_HALO, C), lambda b, t: (halo_row(b, t), ca)),
                  pl.BlockSpec((CONV_HALO, C), lambda b, t: (halo_row(b, t), ca + 1)),
                  const((CONV_KERNEL, C)), const((1, C)), const((1, C)), const((1, C))],
        out_specs=pl.BlockSpec((tt, C), lambda b, t: (row(b, t), 0)),
        out_shape=jax.ShapeDtypeStruct((B * T, C), BF16),
        scratch_shapes=[pltpu.VMEM((CONV_HALO + tt + SUBLANES, C), F32),
                        pltpu.VMEM((SUBLANES, tt + SUBLANES, LANES), F32),
                        pltpu.VMEM((tt, C), F32)],
        compiler_params=_cparams("parallel", "parallel"),
        name="conformer_conv",
    )(z, z, z, z, conv_w, conv_b, ln_g, ln_b)


def _mem_attn_body(x_ref, gq_ref, wq_ref, kv_ref, qn_ref, kn_ref, wo_ref, gn_ref, o_ref, on_ref):
    dh = HEAD_DIM
    width = MEM_HEADS * dh
    scale = dh ** -0.5
    q_all = jnp.dot(_rms(x_ref[...], gq_ref[...]).astype(BF16), wq_ref[...], preferred_element_type=F32)
    heads = [slice(h * dh, (h + 1) * dh) for h in range(MEM_HEADS)]
    scores = []
    for cs in heads:
        q = (_rms(q_all[:, cs], qn_ref[...]) * scale).astype(BF16)
        k = _rms(kv_ref[:, cs], kn_ref[...]).astype(BF16)
        scores.append(lax.dot_general(q, k, (((1,), (1,)), ((), ())), preferred_element_type=F32))
    outs = []
    for h, s in enumerate(scores):
        v = kv_ref[:, width + h * dh:width + (h + 1) * dh].astype(BF16)
        p = jnp.exp(s - jnp.max(s, axis=-1, keepdims=True))
        inv = 1.0 / jnp.sum(p, axis=-1, keepdims=True)
        outs.append((jnp.dot(p.astype(BF16), v, preferred_element_type=F32) * inv).astype(BF16))
    o = jnp.concatenate(outs, axis=1)
    y = x_ref[...] + jnp.dot(o, wo_ref[...], preferred_element_type=F32)
    o_ref[...] = y
    on_ref[...] = _rms(y, gn_ref[...]).astype(on_ref.dtype)


def mem_attention(x, q_gain, w_mq, kvm, mq_norm, mk_norm, w_mo, next_norm, *, B, T, M, tt=256):
    D = x.shape[1]
    width = MEM_HEADS * HEAD_DIM
    nt = T // tt
    row = lambda b, t: b * nt + t
    const = lambda shape: pl.BlockSpec(shape, lambda b, t: (0,) * len(shape))
    return pl.pallas_call(
        _mem_attn_body,
        grid=(B, nt),
        in_specs=[pl.BlockSpec((tt, D), lambda b, t: (row(b, t), 0)),
                  const((1, D)), const((D, width)),
                  pl.BlockSpec((M, 2 * width), lambda b, t: (b, 0)),
                  const((1, HEAD_DIM)), const((1, HEAD_DIM)), const((width, D)),
                  const((1, D))],
        out_specs=[pl.BlockSpec((tt, D), lambda b, t: (row(b, t), 0)),
                   pl.BlockSpec((tt, D), lambda b, t: (row(b, t), 0))],
        out_shape=[jax.ShapeDtypeStruct((B * T, D), F32),
                   jax.ShapeDtypeStruct((B * T, D), BF16)],
        compiler_params=_cparams("parallel", "parallel"),
        name="mem_attention",
    )(x, q_gain, w_mq, kvm, mq_norm, mk_norm, w_mo, next_norm)


def kernel(x, mem, norm_mix, w_in, gate_b, q_norm, k_norm_cmp, k_norm_slc, k_norm_win, cmp_pos_k, cmp_pos_v, cmp_k_w1, cmp_k_w2, cmp_v_w1, cmp_v_w2, conv_w, conv_b, conv_ln_g, conv_ln_b, w_out, norm_mem_q, norm_mem_kv, w_mq, w_mk, w_mv, mq_norm, mk_norm, w_mo, norm_ffn, w_gate, w_up, w_down):
    B, T, D = x.shape
    M = mem.shape[1]
    depth = w_in.shape[0]
    G, hg = NSA_KV_HEADS, NSA_GROUP
    C = D - NSA_WIDTH
    n_gate = N_BRANCH * NSA_HEADS
    kv_all = 6 * KV_WIDTH
    assert w_in.shape[2] == NSA_WIDTH + kv_all + n_gate + 2 * C
    assert C == NSA_WIDTH, "column blocking below assumes equal NSA and conv widths"

    row1 = lambda v: v.reshape(1, -1)
    xf = x.reshape(B * T, D)
    memf = mem.reshape(B * M, D)
    u0 = NSA_WIDTH + kv_all + n_gate
    for l in range(depth):
        w = w_in[l]
        w_t = jnp.swapaxes(w, 0, 1)
        wg = w[:, NSA_WIDTH + kv_all:u0].reshape(D, G, hg * N_BRANCH)
        wg = jnp.pad(wg, ((0, 0), (0, 0), (0, LANES - hg * N_BRANCH))).reshape(D, G * LANES)
        gb = jnp.pad(gate_b[l].reshape(G, hg * N_BRANCH), ((0, 0), (0, LANES - hg * N_BRANCH))).reshape(1, G * LANES)

        z, gl = in_projection(rmsnorm_bf16(xf, row1(norm_mix[l]), tm=512), w_t, wg, tm=2048, tn=512,
                              segments=((0, NSA_WIDTH), (u0, 2 * C), (NSA_WIDTH, kv_all)))
        kv_col0 = NSA_WIDTH + 2 * C
        kc, vc, ks, vsT, kw, vwT = kv_prep(z, row1(k_norm_slc[l]), row1(k_norm_win[l]), B=B, T=T, col0=kv_col0)
        kcmp, vcmpT = compress_kv(kc, vc, cmp_pos_k[l], cmp_pos_v[l],
                                  cmp_k_w1[l].astype(BF16), cmp_k_w2[l].astype(BF16),
                                  cmp_v_w1[l].astype(BF16), cmp_v_w2[l].astype(BF16), row1(k_norm_cmp[l]),
                                  n_blk=T // SEL_BLK)
        o_nsa = nsa_attention(z, gl, gb, row1(q_norm[l]), kcmp, vcmpT, ks, vsT, kw, vwT, B=B, T=T)
        o_conv = conformer_conv(z, conv_w[l], row1(conv_b[l]), row1(conv_ln_g[l]), row1(conv_ln_b[l]),
                                B=B, T=T, col0=NSA_WIDTH, C=C)
        xf = concat_matmul_residual(o_nsa, o_conv, w_out[l], xf, tm=2048, tn=512)

        w_mkv = jnp.concatenate([w_mk[l], w_mv[l]], axis=1).astype(BF16)
        kvm = norm_matmul(memf, row1(norm_mem_kv[l]), w_mkv, tm=B * M, tn=512)
        xf, xn = mem_attention(xf, row1(norm_mem_q[l]), w_mq[l].astype(BF16), kvm,
                               row1(mq_norm[l]), row1(mk_norm[l]), w_mo[l].astype(BF16),
                               row1(norm_ffn[l]), B=B, T=T, M=M, tt=512)

        hidden = swiglu_up(xn, w_gate[l], w_up[l], tm=2048, tn=512)
        xf = matmul_residual(hidden, w_down[l].astype(BF16), xf, tm=1024, tn=512)
    return xf.reshape(B, T, D)
```

```python
import functools

import ml_dtypes
import numpy as np
import jax
import jax.numpy as jnp
from jax import lax
from jax.experimental import pallas as pl
from jax.experimental.pallas import tpu as pltpu

F32 = jnp.float32
BF16 = jnp.bfloat16

HEAD_DIM = 128
NSA_HEADS = 8
NSA_KV_HEADS = 2
NSA_GROUP = NSA_HEADS // NSA_KV_HEADS
NSA_WIDTH = NSA_HEADS * HEAD_DIM
KV_WIDTH = NSA_KV_HEADS * HEAD_DIM
N_BRANCH = 3
CONV_KERNEL = 31
CMP_LEN = 32
CMP_STRIDE = 16
SEL_BLK = 64
N_SEL = 16
WINDOW = 512
MEM_HEADS = 4
NEG_INF = -1e30
FORCE_BONUS = 1e3
RMS_EPS = 1e-6
LN_EPS = 1e-5

LANES = 128
SUBLANES = 8
VMEM_LIMIT_BYTES = 56 * 1024 * 1024

Q_TILE = 256
SEL_CHUNK = 256
WIN_CHUNK = 256
NSA_COLS_PER_DOT = 512
CONV_HALO = 32


def _cparams(*sem):
    return pltpu.CompilerParams(dimension_semantics=sem, vmem_limit_bytes=VMEM_LIMIT_BYTES)


def _rms(x, g, eps=RMS_EPS):
    ms = jnp.mean(x * x, axis=-1, keepdims=True)
    return x * lax.rsqrt(ms + eps) * g


def _silu(x):
    return x * jax.nn.sigmoid(x)


def _norm_to_scratch(x_ref, g_ref, xn_ref):
    @pl.when(pl.program_id(1) == 0)
    def _():
        xn_ref[...] = _rms(x_ref[...], g_ref[...]).astype(xn_ref.dtype)


def _norm_mm_body(x_ref, g_ref, w_ref, o_ref, xn_ref):
    _norm_to_scratch(x_ref, g_ref, xn_ref)
    o_ref[...] = jnp.dot(xn_ref[...], w_ref[...], preferred_element_type=F32).astype(o_ref.dtype)


def _rmsnorm_body(x_ref, g_ref, o_ref):
    o_ref[...] = _rms(x_ref[...], g_ref[...]).astype(o_ref.dtype)


def rmsnorm_bf16(x, g, *, tm):
    M, K = x.shape
    return pl.pallas_call(
        _rmsnorm_body,
        grid=(M // tm,),
        in_specs=[pl.BlockSpec((tm, K), lambda i: (i, 0)), pl.BlockSpec((1, K), lambda i: (0, 0))],
        out_specs=pl.BlockSpec((tm, K), lambda i: (i, 0)),
        out_shape=jax.ShapeDtypeStruct((M, K), BF16),
        compiler_params=_cparams("parallel"),
        name="rmsnorm",
    )(x, g)


def _in_proj_body(xn_ref, wt_ref, wg_ref, o_ref, og_ref):
    @pl.when(pl.program_id(1) == 0)
    def _():
        og_ref[...] = jnp.dot(xn_ref[...], wg_ref[...].astype(BF16), preferred_element_type=F32)

    o_ref[...] = lax.dot_general(xn_ref[...], wt_ref[...].astype(BF16), (((1,), (1,)), ((), ())),
                                 preferred_element_type=F32)


def _swiglu_body(x_ref, wg_ref, wu_ref, wd_ref, o_ref, wd_o):
    x = x_ref[...]
    a = jnp.dot(x, wg_ref[...].astype(BF16), preferred_element_type=F32)
    b = jnp.dot(x, wu_ref[...].astype(BF16), preferred_element_type=F32)
    o_ref[...] = (_silu(a) * b).astype(o_ref.dtype)

    @pl.when(pl.program_id(0) == 0)
    def _():
        wd_o[...] = wd_ref[...].astype(wd_o.dtype)


def norm_matmul(x, g, w, *, tm, tn, out_dtype=F32):
    M, K = x.shape
    N = w.shape[1]
    return pl.pallas_call(
        _norm_mm_body,
        grid=(M // tm, N // tn),
        in_specs=[pl.BlockSpec((tm, K), lambda i, j: (i, 0)),
                  pl.BlockSpec((1, K), lambda i, j: (0, 0)),
                  pl.BlockSpec((K, tn), lambda i, j: (0, j))],
        out_specs=pl.BlockSpec((tm, tn), lambda i, j: (i, j)),
        out_shape=jax.ShapeDtypeStruct((M, N), out_dtype),
        scratch_shapes=[pltpu.VMEM((tm, K), BF16)],
        compiler_params=_cparams("parallel", "arbitrary"),
        name="norm_matmul",
    )(x, g, w)


def in_projection(xn, w_t, wg, *, segments, tm, tn):
    M, K = xn.shape
    NG = wg.shape[1]
    assert all(start % SUBLANES == 0 and size % tn == 0 for start, size in segments)
    N = sum(size for _, size in segments)

    def row_start(i, j):
        group, first = 0, 0
        for seg_start, size in segments:
            group = jnp.where(j >= first, seg_start // SUBLANES + (j - first) * (tn // SUBLANES), group)
            first += size // tn
        return (group * SUBLANES, 0)

    return pl.pallas_call(
        _in_proj_body,
        grid=(M // tm, N // tn),
        in_specs=[pl.BlockSpec((tm, K), lambda i, j: (i, 0)),
                  pl.BlockSpec((pl.Element(tn), pl.Element(K)), row_start),
                  pl.BlockSpec((K, NG), lambda i, j: (0, 0))],
        out_specs=[pl.BlockSpec((tm, tn), lambda i, j: (i, j)),
                   pl.BlockSpec((tm, NG), lambda i, j: (i, 0))],
        out_shape=[jax.ShapeDtypeStruct((M, N), F32),
                   jax.ShapeDtypeStruct((M, NG), F32)],
        compiler_params=_cparams("parallel", "arbitrary"),
        name="in_proj",
    )(xn, w_t, wg)


def swiglu_up(xn, w_gate, w_up, w_down, *, tm, tn):
    M, K = xn.shape
    N = w_gate.shape[1]
    D = w_down.shape[1]
    nj = N // tn
    assert w_down.shape[0] == N
    down_block = lambda i, j: (jnp.where(i == 0, j, nj - 1), 0)
    return pl.pallas_call(
        _swiglu_body,
        grid=(M // tm, nj),
        in_specs=[pl.BlockSpec((tm, K), lambda i, j: (i, 0)),
                  pl.BlockSpec((K, tn), lambda i, j: (0, j)),
                  pl.BlockSpec((K, tn), lambda i, j: (0, j)),
                  pl.BlockSpec((tn, D), down_block)],
        out_specs=[pl.BlockSpec((tm, tn), lambda i, j: (i, j)),
                   pl.BlockSpec((tn, D), down_block)],
        out_shape=[jax.ShapeDtypeStruct((M, N), BF16),
                   jax.ShapeDtypeStruct((N, D), BF16)],
        compiler_params=_cparams("arbitrary", "arbitrary"),
        name="ffn_up",
    )(xn, w_gate, w_up, w_down)


def _mm_res_body(a_ref, w_ref, r_ref, o_ref):
    o_ref[...] = r_ref[...] + jnp.dot(a_ref[...], w_ref[...], preferred_element_type=F32)


def matmul_residual(a, w, res, *, tm, tn):
    M, K = a.shape
    N = w.shape[1]
    return pl.pallas_call(
        _mm_res_body,
        grid=(M // tm, N // tn),
        in_specs=[pl.BlockSpec((tm, K), lambda i, j: (i, 0)),
                  pl.BlockSpec((K, tn), lambda i, j: (0, j)),
                  pl.BlockSpec((tm, tn), lambda i, j: (i, j))],
        out_specs=pl.BlockSpec((tm, tn), lambda i, j: (i, j)),
        out_shape=jax.ShapeDtypeStruct((M, N), F32),
        compiler_params=_cparams("parallel", "arbitrary"),
        name="matmul_residual",
    )(a, w, res)


def _mm2_res_body(a_ref, b_ref, wa_ref, wb_ref, r_ref, o_ref):
    acc = jnp.dot(a_ref[...], wa_ref[...].astype(BF16), preferred_element_type=F32)
    acc = acc + jnp.dot(b_ref[...], wb_ref[...].astype(BF16), preferred_element_type=F32)
    o_ref[...] = r_ref[...] + acc


def concat_matmul_residual(a, b, w, res, *, tm, tn):
    M, Ka = a.shape
    Kb = b.shape[1]
    assert Ka == Kb and w.shape[0] == Ka + Kb
    N = w.shape[1]
    return pl.pallas_call(
        _mm2_res_body,
        grid=(M // tm, N // tn),
        in_specs=[pl.BlockSpec((tm, Ka), lambda i, j: (i, 0)),
                  pl.BlockSpec((tm, Kb), lambda i, j: (i, 0)),
                  pl.BlockSpec((Ka, tn), lambda i, j: (0, j)),
                  pl.BlockSpec((Kb, tn), lambda i, j: (1, j)),
                  pl.BlockSpec((tm, tn), lambda i, j: (i, j))],
        out_specs=pl.BlockSpec((tm, tn), lambda i, j: (i, j)),
        out_shape=jax.ShapeDtypeStruct((M, N), F32),
        compiler_params=_cparams("parallel", "arbitrary"),
        name="out_proj",
    )(a, b, w, w, res)


AUG_WIDTH = LANES
V_EXTRA = 16
SLOPE_PIECES = 4
LOG2E = float(np.log2(np.e))


def _bf16_pieces(x, n):
    out, rem = [], np.asarray(x, np.float64)
    for _ in range(n):
        piece = rem.astype(ml_dtypes.bfloat16).astype(np.float64)
        out.append(piece.astype(np.float32))
        rem = rem - piece
    return out


def _position_lanes(pos_hi, pos_lo, onehot, n_blk, shape):
    lane = lax.broadcasted_iota(jnp.int32, shape, 1)
    p = SLOPE_PIECES
    extra = jnp.where((lane >= n_blk) & (lane < n_blk + p), pos_hi,
                      jnp.where((lane >= n_blk + p) & (lane < n_blk + 2 * p), pos_lo, 0.0))
    if onehot is not None:
        extra = jnp.where(lane < n_blk, onehot, extra)
    return extra.astype(BF16)


def _kv_prep_body(c_ref, s_ref, w_ref, kns_ref, knw_ref,
                  kc_o, vc_o, ks_o, vsT_o, kw_o, vwT_o, slab_ref, *, tt, n_blk):
    dh = HEAD_DIM
    shape = (tt, AUG_WIDTH)
    pos = pl.program_id(1) * tt + lax.broadcasted_iota(jnp.int32, shape, 0)
    lane = lax.broadcasted_iota(jnp.int32, shape, 1)
    blk = jnp.right_shift(pos, SEL_BLK.bit_length() - 1)
    pos_hi = (blk * SEL_BLK).astype(F32)
    pos_lo = (pos & (SEL_BLK - 1)).astype(F32)
    aug_sel = _position_lanes(pos_hi, pos_lo, (blk == lane).astype(F32), n_blk, shape)
    aug_win = _position_lanes(pos_hi, pos_lo, None, n_blk, shape)
    for g in range(NSA_KV_HEADS):
        k_cols = slice(g * dh, (g + 1) * dh)
        v_cols = slice(KV_WIDTH + g * dh, KV_WIDTH + (g + 1) * dh)
        for slab, (cols, out) in enumerate(((k_cols, kc_o), (v_cols, vc_o))):
            slab_ref[slab] = c_ref[:, cols]
            for i in range(CMP_STRIDE):
                rows = pl.ds(i, tt // CMP_STRIDE, stride=CMP_STRIDE)
                out[0, g, :, i * dh:(i + 1) * dh] = slab_ref[slab, rows, :]
        ks_o[0, g, :, 0:dh] = _rms(s_ref[:, k_cols], kns_ref[...]).astype(BF16)
        ks_o[0, g, :, dh:dh + AUG_WIDTH] = aug_sel
        kw_o[0, g, :, 0:dh] = _rms(w_ref[:, k_cols], knw_ref[...]).astype(BF16)
        kw_o[0, g, :, dh:dh + AUG_WIDTH] = aug_win
        for v_ref, out, chunk in ((s_ref, vsT_o, SEL_CHUNK), (w_ref, vwT_o, WIN_CHUNK)):
            vT = v_ref[:, v_cols].T.astype(BF16)
            ones_first = (lax.broadcasted_iota(jnp.int32, (V_EXTRA, chunk), 0) == 0).astype(BF16)
            for c in range(tt // chunk):
                out[0, g, c, 0:dh, :] = vT[:, c * chunk:(c + 1) * chunk]
                out[0, g, c, dh:dh + V_EXTRA, :] = ones_first


def kv_prep(z, kn_slc, kn_win, *, B, T, col0, tt=512):
    G, dh = NSA_KV_HEADS, HEAD_DIM
    nt = T // tt
    n_blk = T // SEL_BLK
    assert n_blk + 2 * SLOPE_PIECES <= AUG_WIDTH
    cb = col0 // (2 * KV_WIDTH)
    row = lambda b, t: b * nt + t
    kdim = dh + AUG_WIDTH
    return pl.pallas_call(
        functools.partial(_kv_prep_body, tt=tt, n_blk=n_blk),
        grid=(B, nt),
        in_specs=[pl.BlockSpec((tt, 2 * KV_WIDTH), lambda b, t: (row(b, t), cb)),
                  pl.BlockSpec((tt, 2 * KV_WIDTH), lambda b, t: (row(b, t), cb + 1)),
                  pl.BlockSpec((tt, 2 * KV_WIDTH), lambda b, t: (row(b, t), cb + 2)),
                  pl.BlockSpec((1, dh), lambda b, t: (0, 0)),
                  pl.BlockSpec((1, dh), lambda b, t: (0, 0))],
        out_specs=[pl.BlockSpec((1, G, tt // CMP_STRIDE, CMP_STRIDE * dh), lambda b, t: (b, 0, t, 0)),
                   pl.BlockSpec((1, G, tt // CMP_STRIDE, CMP_STRIDE * dh), lambda b, t: (b, 0, t, 0)),
                   pl.BlockSpec((1, G, tt, kdim), lambda b, t: (b, 0, t, 0)),
                   pl.BlockSpec((1, G, tt // SEL_CHUNK, dh + V_EXTRA, SEL_CHUNK), lambda b, t: (b, 0, t, 0, 0)),
                   pl.BlockSpec((1, G, tt, kdim), lambda b, t: (b, 0, t, 0)),
                   pl.BlockSpec((1, G, tt // WIN_CHUNK, dh + V_EXTRA, WIN_CHUNK), lambda b, t: (b, 0, t, 0, 0))],
        out_shape=[jax.ShapeDtypeStruct((B, G, T // CMP_STRIDE, CMP_STRIDE * dh), F32),
                   jax.ShapeDtypeStruct((B, G, T // CMP_STRIDE, CMP_STRIDE * dh), F32),
                   jax.ShapeDtypeStruct((B, G, T, kdim), BF16),
                   jax.ShapeDtypeStruct((B, G, T // SEL_CHUNK, dh + V_EXTRA, SEL_CHUNK), BF16),
                   jax.ShapeDtypeStruct((B, G, T, kdim), BF16),
                   jax.ShapeDtypeStruct((B, G, T // WIN_CHUNK, dh + V_EXTRA, WIN_CHUNK), BF16)],
        scratch_shapes=[pltpu.VMEM((2, tt, dh), F32)],
        compiler_params=_cparams("parallel", "parallel"),
        name="kv_prep",
    )(z, z, z, kn_slc, kn_win)


def _compress_body(xk_ref, xv_ref, pk_ref, pv_ref, w1k_ref, w2k_ref, w1v_ref, w2v_ref, kn_ref,
                   kc_o, vcT_o, *, n_blk):
    dh = HEAD_DIM

    def phi(x_ref, p_ref, w1_ref, w2_ref):
        x = x_ref[0, 0]
        first = jnp.dot((x + p_ref[0:1, :]).astype(BF16), w1_ref[0], preferred_element_type=F32)
        second = jnp.dot((x + p_ref[1:2, :]).astype(BF16), w1_ref[1], preferred_element_type=F32)
        n = second.shape[0]
        h = first + pltpu.roll(second, n - 1, 0)
        return jnp.dot(_silu(h).astype(BF16), w2_ref[...], preferred_element_type=F32)

    k = _rms(phi(xk_ref, pk_ref, w1k_ref, w2k_ref), kn_ref[...])
    nc = k.shape[0]
    shape = (nc, AUG_WIDTH)
    start = (lax.broadcasted_iota(jnp.int32, shape, 0) * CMP_STRIDE).astype(F32)
    centre = jnp.full(shape, (CMP_LEN - 1) / 2.0, F32)
    kc_o[0, 0, :, 0:dh] = k.astype(BF16)
    kc_o[0, 0, :, dh:dh + AUG_WIDTH] = _position_lanes(start, centre, None, n_blk, shape)
    vcT_o[0, 0] = phi(xv_ref, pv_ref, w1v_ref, w2v_ref).T.astype(BF16)


def compress_kv(xk, xv, pos_k, pos_v, w1k, w2k, w1v, w2v, kn_cmp, *, n_blk):
    B, G, nc, half = xk.shape
    dh = half // CMP_STRIDE
    assert CMP_LEN == 2 * CMP_STRIDE
    x_spec = pl.BlockSpec((1, 1, nc, half), lambda b, g: (b, g, 0, 0))
    full = lambda shape: pl.BlockSpec(shape, lambda b, g: (0,) * len(shape))
    return pl.pallas_call(
        functools.partial(_compress_body, n_blk=n_blk),
        grid=(B, G),
        in_specs=[x_spec, x_spec, full((2, half)), full((2, half)),
                  full((2, half, dh)), full((dh, dh)), full((2, half, dh)), full((dh, dh)),
                  full((1, dh))],
        out_specs=[pl.BlockSpec((1, 1, nc, dh + AUG_WIDTH), lambda b, g: (b, g, 0, 0)),
                   pl.BlockSpec((1, 1, dh, nc), lambda b, g: (b, g, 0, 0))],
        out_shape=[jax.ShapeDtypeStruct((B, G, nc, dh + AUG_WIDTH), BF16),
                   jax.ShapeDtypeStruct((B, G, dh, nc), BF16)],
        compiler_params=_cparams("parallel", "parallel"),
        name="compress_kv",
    )(xk, xv, pos_k.reshape(2, half), pos_v.reshape(2, half),
      w1k.reshape(2, half, dh), w2k, w1v.reshape(2, half, dh), w2v, kn_cmp)


def _nsa_body(q_ref, gl_ref, gb_ref, qn_ref, lanes_ref, kc_ref, vcT_ref,
              ks_ref, vsT_ref, kw_ref, vwT_ref, ovT_ref, kmq_ref, o_ref,
              score_ref, gate_ref, qc_ref, qs_ref, m_ref, acc_ref, part_ref):
    dh, hg, tq = HEAD_DIM, NSA_GROUP, Q_TILE
    cols = hg * tq
    qi = pl.program_id(2)
    t0 = qi * tq
    contract_last = (((1,), (1,)), ((), ()))
    n_blk = ovT_ref.shape[0]

    scale = dh ** -0.5 * LOG2E
    q_heads = [(_rms(q_ref[:, h * dh:(h + 1) * dh], qn_ref[...]) * scale).astype(BF16)
               for h in range(hg)]

    def stack_queries(aug_heads):
        return jnp.concatenate([jnp.concatenate([q_heads[h], aug_heads[h]], axis=1)
                                for h in range(hg)], axis=0)

    hc = kmq_ref.shape[1]
    assert cols % hc == 0 and hc % tq == 0

    def reset(b):
        m_ref[b] = jnp.full((1, cols), NEG_INF, F32)
        acc_ref[b] = jnp.zeros((dh + V_EXTRA, cols), F32)

    def attend(b, queries_ref, parts):
        groups = [slice(i * hc, (i + 1) * hc) for i in range(cols // hc)]
        scores = [[lax.dot_general(k, queries_ref[cs, :], contract_last, preferred_element_type=F32)
                   for cs in groups] for k, _, _, _ in parts]
        for (k, v_pieces, mask, bias), part_scores in zip(parts, scores):
            for cs, s in zip(groups, part_scores):
                if mask is not None:
                    s = jnp.where(mask(), s, NEG_INF)
                if bias is not None:
                    s = s + bias
                m_old = m_ref[b, :, cs]
                m_new = jnp.maximum(m_old, jnp.max(s, axis=0, keepdims=True))
                alpha = jnp.exp2(m_old - m_new)
                p = jnp.exp2(s - m_new).astype(BF16)
                m_ref[b, :, cs] = m_new
                pv = None
                for i, vT in enumerate(v_pieces):
                    n = vT.shape[1]
                    d = jnp.dot(vT, p[i * n:(i + 1) * n, :], preferred_element_type=F32)
                    pv = d if pv is None else pv + d
                acc_ref[b, :, cs] = alpha * acc_ref[b, :, cs] + pv

    def result(b):
        return acc_ref[b, 0:dh, :] * (1.0 / acc_ref[b, dh:dh + 1, :])

    nc = kc_ref.shape[2]
    qc_ref[...] = stack_queries([jnp.broadcast_to(lanes_ref[0, hg + h:hg + h + 1, :], (tq, AUG_WIDTH)).astype(BF16)
                                 for h in range(hg)])
    s = lax.dot_general(kc_ref[0, 0], qc_ref[...], contract_last, preferred_element_type=F32)
    c_idx = lax.broadcasted_iota(jnp.int32, (nc, cols), 0)
    t_c = t0 + (lax.broadcasted_iota(jnp.int32, (nc, cols), 1) & (tq - 1))
    mask = (c_idx * CMP_STRIDE + (CMP_LEN - 1)) <= t_c
    s = jnp.where(mask, s, NEG_INF)
    m_c = jnp.max(s, axis=0, keepdims=True)
    p_c = jnp.where(mask, jnp.exp2(s - m_c), 0.0)
    l_c = jnp.sum(p_c, axis=0, keepdims=True)
    p_c = p_c * jnp.where(l_c > 0.0, 1.0 / l_c, 0.0)
    o_cmp = jnp.dot(vcT_ref[0, 0], p_c.astype(BF16), preferred_element_type=F32)

    cw = WIN_CHUNK
    assert cw == tq and WINDOW % cw == 0 and cw == kmq_ref.shape[0]
    n_back = WINDOW // cw
    WIN, SEL = 0, 1

    def win_part(c, mask=None, bias=None):
        k = kw_ref[0, 0, pl.ds(pl.multiple_of(c * cw, cw), cw), :]
        return k, [vwT_ref[0, 0, c]], mask, bias

    threshold = jnp.where(qi >= n_back, 0, cw)
    reset(WIN)
    attend(WIN, qc_ref,
           [win_part(qi, mask=lambda: kmq_ref[...] <= 0)]
           + [win_part(jnp.maximum(qi - j, 0), bias=jnp.where(qi >= j, 0.0, NEG_INF))
              for j in range(1, n_back)]
           + [win_part(jnp.maximum(qi - n_back, 0), mask=lambda: kmq_ref[...] > threshold)])

    gate_ref[...] = jax.nn.sigmoid(gl_ref[...] + gb_ref[...]).T

    def gate(h, branch):
        r = h * N_BRANCH + branch
        return gate_ref[r:r + 1, :]

    o_win = result(WIN)
    for h in range(hg):
        cs = slice(h * tq, (h + 1) * tq)
        part_ref[:, cs] = gate(h, 0) * o_cmp[:, cs] + gate(h, 2) * o_win[:, cs]

    p_sum = p_c[:, 0:tq]
    for h in range(1, hg):
        p_sum = p_sum + p_c[:, h * tq:(h + 1) * tq]
    hi = p_sum.astype(BF16)
    rem = p_sum - hi.astype(F32)
    mid = rem.astype(BF16)
    lo = (rem - mid.astype(F32)).astype(BF16)
    ovT = ovT_ref[...]
    imp = (jnp.dot(ovT, hi, preferred_element_type=F32)
           + jnp.dot(ovT, mid, preferred_element_type=F32)
           + jnp.dot(ovT, lo, preferred_element_type=F32))
    blk = lax.broadcasted_iota(jnp.int32, (n_blk, tq), 0)
    t_b = t0 + lax.broadcasted_iota(jnp.int32, (n_blk, tq), 1)
    cur = jnp.right_shift(t_b, SEL_BLK.bit_length() - 1)
    valid = blk * SEL_BLK <= t_b
    forced = (blk == 0) | (blk == cur) | (blk == cur - 1)
    score = jnp.where(valid, imp + FORCE_BONUS * forced.astype(F32), NEG_INF)
    score_ref[...] = score
    rank = jnp.zeros((n_blk, tq), F32)
    for j in range(n_blk):
        sj = score_ref[j:j + 1, :]
        beats = (sj > score) | ((sj == score) & (blk > j))
        rank = rank + beats.astype(F32)
    block_bias = jnp.where(rank < float(min(N_SEL, n_blk)), 0.0, NEG_INF)

    assert 2 * SLOPE_PIECES <= 8
    ones_rows = (lax.broadcasted_iota(jnp.int32, (8, tq), 0) < 2 * SLOPE_PIECES).astype(F32)
    extra_t = jnp.concatenate([block_bias, ones_rows,
                               jnp.zeros((AUG_WIDTH - n_blk - 8, tq), F32)], axis=0).T
    qs_ref[...] = stack_queries([(extra_t * lanes_ref[0, h:h + 1, :]).astype(BF16) for h in range(hg)])

    ck = SEL_CHUNK
    assert ck == tq and ck == kmq_ref.shape[0]

    def sel_part(c, n=1, mask=None, bias=None):
        k = ks_ref[0, 0, pl.ds(pl.multiple_of(c * ck, ck), n * ck), :]
        return k, [vsT_ref[0, 0, c + i] for i in range(n)], mask, bias

    reset(SEL)

    odd = qi % 2 == 1
    attend(SEL, qs_ref,
           [sel_part(qi, mask=lambda: kmq_ref[...] <= 0),
            sel_part(jnp.maximum(qi - 1, 0), bias=jnp.where(odd, 0.0, NEG_INF))])

    @pl.loop(0, qi // 2)
    def _(i):
        attend(SEL, qs_ref, [sel_part(2 * i, n=2)])

    o_slc = result(SEL)
    for h in range(hg):
        cs = slice(h * tq, (h + 1) * tq)
        o_h = part_ref[:, cs] + gate(h, 1) * o_slc[:, cs]
        o_ref[:, h * dh:(h + 1) * dh] = o_h.T.astype(o_ref.dtype)


def nsa_attention(z, gl, gate_b, q_norm, kcmp, vcmpT, ks, vsT, kw, vwT, *, B, T):
    G, hg, dh, tq = NSA_KV_HEADS, NSA_GROUP, HEAD_DIM, Q_TILE
    nq = T // tq
    nc = kcmp.shape[2]
    n_blk = T // SEL_BLK
    kdim = dh + AUG_WIDTH
    cols = hg * tq
    assert SEL_CHUNK == WIN_CHUNK == tq
    assert SEL_CHUNK % SEL_BLK == 0 and T % SEL_CHUNK == 0 and gl.shape[1] == G * LANES
    assert SEL_CHUNK % tq == 0 and n_blk % 8 == 0 and n_blk + 8 <= AUG_WIDTH

    slopes = np.exp2(-8.0 * (np.arange(NSA_HEADS, dtype=np.float64) + 1.0) / NSA_HEADS).reshape(G, hg)
    lanes = np.zeros((G, 2 * hg, AUG_WIDTH), np.float32)
    lanes[:, :hg, :n_blk] = 1.0
    for i, piece in enumerate(_bf16_pieces(slopes * LOG2E, SLOPE_PIECES)):
        for lane in (n_blk + i, n_blk + SLOPE_PIECES + i):
            lanes[:, :hg, lane] = piece
            lanes[:, hg:, lane] = piece
    cs = np.arange(nc)[:, None] * CMP_STRIDE
    bs = np.arange(n_blk)[None, :] * SEL_BLK
    overlap = np.clip(np.minimum(cs + CMP_LEN, bs + SEL_BLK) - np.maximum(cs, bs), 0, None) / CMP_LEN
    overlap[(T - CMP_LEN) // CMP_STRIDE + 1:] = 0.0
    ovT = jnp.asarray(overlap.T, dtype=BF16)
    hc = NSA_COLS_PER_DOT
    kmq = jnp.asarray(np.arange(SEL_CHUNK)[:, None] - (np.arange(hc)[None, :] % tq), dtype=jnp.int32)

    row = lambda b, qi: b * nq + qi
    per_bg = lambda shape: pl.BlockSpec((1, 1) + shape, lambda b, g, qi: (b, g) + (0,) * len(shape))
    const = lambda shape: pl.BlockSpec(shape, lambda b, g, qi: (0,) * len(shape))
    return pl.pallas_call(
        _nsa_body,
        grid=(B, G, nq),
        in_specs=[pl.BlockSpec((tq, hg * dh), lambda b, g, qi: (row(b, qi), g)),
                  pl.BlockSpec((tq, LANES), lambda b, g, qi: (row(b, qi), g)),
                  pl.BlockSpec((1, LANES), lambda b, g, qi: (0, g)),
                  const((1, dh)),
                  pl.BlockSpec((1, 2 * hg, AUG_WIDTH), lambda b, g, qi: (g, 0, 0)),
                  per_bg((nc, kdim)), per_bg((dh, nc)),
                  per_bg((T, kdim)), per_bg((T // SEL_CHUNK, dh + V_EXTRA, SEL_CHUNK)),
                  per_bg((T, kdim)), per_bg((T // WIN_CHUNK, dh + V_EXTRA, WIN_CHUNK)),
                  const((n_blk, nc)), const((SEL_CHUNK, hc))],
        out_specs=pl.BlockSpec((tq, hg * dh), lambda b, g, qi: (row(b, qi), g)),
        out_shape=jax.ShapeDtypeStruct((B * T, NSA_WIDTH), BF16),
        scratch_shapes=[pltpu.VMEM((n_blk, tq), F32), pltpu.VMEM((LANES, tq), F32),
                        pltpu.VMEM((cols, kdim), BF16), pltpu.VMEM((cols, kdim), BF16),
                        pltpu.VMEM((2, 1, cols), F32),
                        pltpu.VMEM((2, dh + V_EXTRA, cols), F32), pltpu.VMEM((dh, cols), F32)],
        compiler_params=_cparams("parallel", "parallel", "arbitrary"),
        name="nsa_attention",
    )(z, gl, gate_b, q_norm, jnp.asarray(lanes), kcmp, vcmpT, ks, vsT, kw, vwT, ovT, kmq)


def _conv_body(a_ref, b_ref, ah_ref, bh_ref, w_ref, cb_ref, lg_ref, lb_ref, o_ref,
               h_ref, g_ref, y_ref, *, tt):
    C = a_ref.shape[1]
    halo = ah_ref[...] * jax.nn.sigmoid(bh_ref[...])
    h_ref[0:CONV_HALO, :] = jnp.where(pl.program_id(1) == 0, 0.0, halo)
    h_ref[CONV_HALO:CONV_HALO + tt, :] = a_ref[...] * jax.nn.sigmoid(b_ref[...])
    h_ref[CONV_HALO + tt:CONV_HALO + tt + SUBLANES, :] = jnp.zeros((SUBLANES, C), F32)
    lead = CONV_HALO - (CONV_KERNEL - 1)
    rows = tt + SUBLANES
    for s in range(C // LANES):
        cs = slice(s * LANES, (s + 1) * LANES)
        acc = None
        for r in range(SUBLANES):
            group = None
            for o in range(lead, lead + CONV_KERNEL):
                if o % SUBLANES == r:
                    term = w_ref[o - lead:o - lead + 1, cs] * h_ref[o - r:o - r + rows, cs]
                    group = term if group is None else group + term
            if r == 0:
                piece = group[0:tt, :]
            else:
                g_ref[r] = group
                piece = g_ref[r, r:r + tt, :]
            acc = piece if acc is None else acc + piece
        y_ref[:, cs] = acc + cb_ref[:, cs]
    y = y_ref[...]
    mu = jnp.mean(y, axis=-1, keepdims=True)
    var = jnp.mean(jnp.square(y - mu), axis=-1, keepdims=True)
    yn = (y - mu) * lax.rsqrt(var + LN_EPS) * lg_ref[...] + lb_ref[...]
    o_ref[...] = _silu(yn).astype(o_ref.dtype)


def conformer_conv(z, conv_w, conv_b, ln_g, ln_b, *, B, T, col0, C, tt=128):
    nt = T // tt
    ca = col0 // C
    hb = tt // CONV_HALO
    row = lambda b, t: b * nt + t
    halo_row = lambda b, t: jnp.maximum(row(b, t) * hb - 1, 0)
    const = lambda shape: pl.BlockSpec(shape, lambda b, t: (0,) * len(shape))
    return pl.pallas_call(
        functools.partial(_conv_body, tt=tt),
        grid=(B, nt),
        in_specs=[pl.BlockSpec((tt, C), lambda b, t: (row(b, t), ca)),
                  pl.BlockSpec((tt, C), lambda b, t: (row(b, t), ca + 1)),
                  pl.BlockSpec((CONV_HALO, C), lambda b, t: (halo_row(b, t), ca)),
                  pl.BlockSpec((CONV_HALO, C), lambda b, t: (halo_row(b, t), ca + 1)),
                  const((CONV_KERNEL, C)), const((1, C)), const((1, C)), const((1, C))],
        out_specs=pl.BlockSpec((tt, C), lambda b, t: (row(b, t), 0)),
        out_shape=jax.ShapeDtypeStruct((B * T, C), BF16),
        scratch_shapes=[pltpu.VMEM((CONV_HALO + tt + SUBLANES, C), F32),
                        pltpu.VMEM((SUBLANES, tt + SUBLANES, LANES), F32),
                        pltpu.VMEM((tt, C), F32)],
        compiler_params=_cparams("parallel", "parallel"),
        name="conformer_conv",
    )(z, z, z, z, conv_w, conv_b, ln_g, ln_b)


def _mem_attn_body(x_ref, gq_ref, wq_ref, kv_ref, qn_ref, kn_ref, wo_ref, gn_ref, o_ref, on_ref):
    dh = HEAD_DIM
    width = MEM_HEADS * dh
    scale = dh ** -0.5
    q_all = jnp.dot(_rms(x_ref[...], gq_ref[...]).astype(BF16), wq_ref[...], preferred_element_type=F32)
    heads = [slice(h * dh, (h + 1) * dh) for h in range(MEM_HEADS)]
    scores = []
    for cs in heads:
        q = (_rms(q_all[:, cs], qn_ref[...]) * scale).astype(BF16)
        k = _rms(kv_ref[:, cs], kn_ref[...]).astype(BF16)
        scores.append(lax.dot_general(q, k, (((1,), (1,)), ((), ())), preferred_element_type=F32))
    outs = []
    for h, s in enumerate(scores):
        v = kv_ref[:, width + h * dh:width + (h + 1) * dh].astype(BF16)
        p = jnp.exp(s - jnp.max(s, axis=-1, keepdims=True))
        inv = 1.0 / jnp.sum(p, axis=-1, keepdims=True)
        outs.append((jnp.dot(p.astype(BF16), v, preferred_element_type=F32) * inv).astype(BF16))
    o = jnp.concatenate(outs, axis=1)
    y = x_ref[...] + jnp.dot(o, wo_ref[...], preferred_element_type=F32)
    o_ref[...] = y
    on_ref[...] = _rms(y, gn_ref[...]).astype(on_ref.dtype)


def mem_attention(x, q_gain, w_mq, kvm, mq_norm, mk_norm, w_mo, next_norm, *, B, T, M, tt=256):
    D = x.shape[1]
    width = MEM_HEADS * HEAD_DIM
    nt = T // tt
    row = lambda b, t: b * nt + t
    const = lambda shape: pl.BlockSpec(shape, lambda b, t: (0,) * len(shape))
    return pl.pallas_call(
        _mem_attn_body,
        grid=(B, nt),
        in_specs=[pl.BlockSpec((tt, D), lambda b, t: (row(b, t), 0)),
                  const((1, D)), const((D, width)),
                  pl.BlockSpec((M, 2 * width), lambda b, t: (b, 0)),
                  const((1, HEAD_DIM)), const((1, HEAD_DIM)), const((width, D)),
                  const((1, D))],
        out_specs=[pl.BlockSpec((tt, D), lambda b, t: (row(b, t), 0)),
                   pl.BlockSpec((tt, D), lambda b, t: (row(b, t), 0))],
        out_shape=[jax.ShapeDtypeStruct((B * T, D), F32),
                   jax.ShapeDtypeStruct((B * T, D), BF16)],
        compiler_params=_cparams("parallel", "parallel"),
        name="mem_attention",
    )(x, q_gain, w_mq, kvm, mq_norm, mk_norm, w_mo, next_norm)


def kernel(x, mem, norm_mix, w_in, gate_b, q_norm, k_norm_cmp, k_norm_slc, k_norm_win, cmp_pos_k, cmp_pos_v, cmp_k_w1, cmp_k_w2, cmp_v_w1, cmp_v_w2, conv_w, conv_b, conv_ln_g, conv_ln_b, w_out, norm_mem_q, norm_mem_kv, w_mq, w_mk, w_mv, mq_norm, mk_norm, w_mo, norm_ffn, w_gate, w_up, w_down):
    B, T, D = x.shape
    M = mem.shape[1]
    depth = w_in.shape[0]
    G, hg = NSA_KV_HEADS, NSA_GROUP
    C = D - NSA_WIDTH
    n_gate = N_BRANCH * NSA_HEADS
    kv_all = 6 * KV_WIDTH
    assert w_in.shape[2] == NSA_WIDTH + kv_all + n_gate + 2 * C
    assert C == NSA_WIDTH, "column blocking below assumes equal NSA and conv widths"

    row1 = lambda v: v.reshape(1, -1)
    xf = x.reshape(B * T, D)
    memf = mem.reshape(B * M, D)
    u0 = NSA_WIDTH + kv_all + n_gate
    for l in range(depth):
        w = w_in[l]
        w_t = jnp.swapaxes(w, 0, 1)
        wg = w[:, NSA_WIDTH + kv_all:u0].reshape(D, G, hg * N_BRANCH)
        wg = jnp.pad(wg, ((0, 0), (0, 0), (0, LANES - hg * N_BRANCH))).reshape(D, G * LANES)
        gb = jnp.pad(gate_b[l].reshape(G, hg * N_BRANCH), ((0, 0), (0, LANES - hg * N_BRANCH))).reshape(1, G * LANES)

        z, gl = in_projection(rmsnorm_bf16(xf, row1(norm_mix[l]), tm=512), w_t, wg, tm=2048, tn=512,
                              segments=((0, NSA_WIDTH), (u0, 2 * C), (NSA_WIDTH, kv_all)))
        kv_col0 = NSA_WIDTH + 2 * C
        kc, vc, ks, vsT, kw, vwT = kv_prep(z, row1(k_norm_slc[l]), row1(k_norm_win[l]), B=B, T=T, col0=kv_col0)
        kcmp, vcmpT = compress_kv(kc, vc, cmp_pos_k[l], cmp_pos_v[l],
                                  cmp_k_w1[l].astype(BF16), cmp_k_w2[l].astype(BF16),
                                  cmp_v_w1[l].astype(BF16), cmp_v_w2[l].astype(BF16), row1(k_norm_cmp[l]),
                                  n_blk=T // SEL_BLK)
        o_nsa = nsa_attention(z, gl, gb, row1(q_norm[l]), kcmp, vcmpT, ks, vsT, kw, vwT, B=B, T=T)
        o_conv = conformer_conv(z, conv_w[l], row1(conv_b[l]), row1(conv_ln_g[l]), row1(conv_ln_b[l]),
                                B=B, T=T, col0=NSA_WIDTH, C=C)
        xf = concat_matmul_residual(o_nsa, o_conv, w_out[l], xf, tm=2048, tn=512)

        w_mkv = jnp.concatenate([w_mk[l], w_mv[l]], axis=1).astype(BF16)
        kvm = norm_matmul(memf, row1(norm_mem_kv[l]), w_mkv, tm=B * M, tn=512)
        xf, xn = mem_attention(xf, row1(norm_mem_q[l]), w_mq[l].astype(BF16), kvm,
                               row1(mq_norm[l]), row1(mk_norm[l]), w_mo[l].astype(BF16),
                               row1(norm_ffn[l]), B=B, T=T, M=M, tt=512)

        hidden, w_down_bf16 = swiglu_up(xn, w_gate[l], w_up[l], w_down[l], tm=1024, tn=512)
        xf = matmul_residual(hidden, w_down_bf16, xf, tm=1024, tn=512)
    return xf.reshape(B, T, D)
```

```python
import functools

import ml_dtypes
import numpy as np
import jax
import jax.numpy as jnp
from jax import lax
from jax.experimental import pallas as pl
from jax.experimental.pallas import tpu as pltpu

F32 = jnp.float32
BF16 = jnp.bfloat16

HEAD_DIM = 128
NSA_HEADS = 8
NSA_KV_HEADS = 2
NSA_GROUP = NSA_HEADS // NSA_KV_HEADS
NSA_WIDTH = NSA_HEADS * HEAD_DIM
KV_WIDTH = NSA_KV_HEADS * HEAD_DIM
N_BRANCH = 3
CONV_KERNEL = 31
CMP_LEN = 32
CMP_STRIDE = 16
SEL_BLK = 64
N_SEL = 16
WINDOW = 512
MEM_HEADS = 4
NEG_INF = -1e30
FORCE_BONUS = 1e3
RMS_EPS = 1e-6
LN_EPS = 1e-5

LANES = 128
SUBLANES = 8
VMEM_LIMIT_BYTES = 56 * 1024 * 1024

Q_TILE = 256
SEL_CHUNK = 256
WIN_CHUNK = 256
NSA_COLS_PER_DOT = 512
CONV_HALO = 32


def _cparams(*sem):
    return pltpu.CompilerParams(dimension_semantics=sem, vmem_limit_bytes=VMEM_LIMIT_BYTES)


def _rms(x, g, eps=RMS_EPS):
    ms = jnp.mean(x * x, axis=-1, keepdims=True)
    return x * lax.rsqrt(ms + eps) * g


def _silu(x):
    return x * jax.nn.sigmoid(x)


def _norm_to_scratch(x_ref, g_ref, xn_ref):
    @pl.when(pl.program_id(1) == 0)
    def _():
        xn_ref[...] = _rms(x_ref[...], g_ref[...]).astype(xn_ref.dtype)


def _norm_mm_body(x_ref, g_ref, w_ref, o_ref, xn_ref):
    _norm_to_scratch(x_ref, g_ref, xn_ref)
    o_ref[...] = jnp.dot(xn_ref[...], w_ref[...], preferred_element_type=F32).astype(o_ref.dtype)


def _rmsnorm_body(x_ref, g_ref, o_ref):
    o_ref[...] = _rms(x_ref[...], g_ref[...]).astype(o_ref.dtype)


def rmsnorm_bf16(x, g, *, tm):
    M, K = x.shape
    return pl.pallas_call(
        _rmsnorm_body,
        grid=(M // tm,),
        in_specs=[pl.BlockSpec((tm, K), lambda i: (i, 0)), pl.BlockSpec((1, K), lambda i: (0, 0))],
        out_specs=pl.BlockSpec((tm, K), lambda i: (i, 0)),
        out_shape=jax.ShapeDtypeStruct((M, K), BF16),
        compiler_params=_cparams("parallel"),
        name="rmsnorm",
    )(x, g)


def _in_proj_body(xn_ref, wt_ref, wg_ref, o_ref, og_ref):
    @pl.when(pl.program_id(1) == 0)
    def _():
        og_ref[...] = jnp.dot(xn_ref[...], wg_ref[...].astype(BF16), preferred_element_type=F32)

    o_ref[...] = lax.dot_general(xn_ref[...], wt_ref[...].astype(BF16), (((1,), (1,)), ((), ())),
                                 preferred_element_type=F32)


def _swiglu_body(x_ref, wg_ref, wu_ref, wd_ref, o_ref, wd_o):
    x = x_ref[...]
    a = jnp.dot(x, wg_ref[...].astype(BF16), preferred_element_type=F32)
    b = jnp.dot(x, wu_ref[...].astype(BF16), preferred_element_type=F32)
    o_ref[...] = (_silu(a) * b).astype(o_ref.dtype)

    @pl.when(pl.program_id(0) == 0)
    def _():
        wd_o[...] = wd_ref[...].astype(wd_o.dtype)


def norm_matmul(x, g, w, *, tm, tn, out_dtype=F32):
    M, K = x.shape
    N = w.shape[1]
    return pl.pallas_call(
        _norm_mm_body,
        grid=(M // tm, N // tn),
        in_specs=[pl.BlockSpec((tm, K), lambda i, j: (i, 0)),
                  pl.BlockSpec((1, K), lambda i, j: (0, 0)),
                  pl.BlockSpec((K, tn), lambda i, j: (0, j))],
        out_specs=pl.BlockSpec((tm, tn), lambda i, j: (i, j)),
        out_shape=jax.ShapeDtypeStruct((M, N), out_dtype),
        scratch_shapes=[pltpu.VMEM((tm, K), BF16)],
        compiler_params=_cparams("parallel", "arbitrary"),
        name="norm_matmul",
    )(x, g, w)


def in_projection(xn, w_t, wg, *, segments, tm, tn):
    M, K = xn.shape
    NG = wg.shape[1]
    assert all(start % SUBLANES == 0 and size % tn == 0 for start, size in segments)
    N = sum(size for _, size in segments)

    def row_start(i, j):
        group, first = 0, 0
        for seg_start, size in segments:
            group = jnp.where(j >= first, seg_start // SUBLANES + (j - first) * (tn // SUBLANES), group)
            first += size // tn
        return (group * SUBLANES, 0)

    return pl.pallas_call(
        _in_proj_body,
        grid=(M // tm, N // tn),
        in_specs=[pl.BlockSpec((tm, K), lambda i, j: (i, 0)),
                  pl.BlockSpec((pl.Element(tn), pl.Element(K)), row_start),
                  pl.BlockSpec((K, NG), lambda i, j: (0, 0))],
        out_specs=[pl.BlockSpec((tm, tn), lambda i, j: (i, j)),
                   pl.BlockSpec((tm, NG), lambda i, j: (i, 0))],
        out_shape=[jax.ShapeDtypeStruct((M, N), F32),
                   jax.ShapeDtypeStruct((M, NG), F32)],
        compiler_params=_cparams("parallel", "arbitrary"),
        name="in_proj",
    )(xn, w_t, wg)


def swiglu_up(xn, w_gate, w_up, w_down, *, tm, tn):
    M, K = xn.shape
    N = w_gate.shape[1]
    D = w_down.shape[1]
    nj = N // tn
    assert w_down.shape[0] == N
    down_block = lambda i, j: (jnp.where(i == 0, j, nj - 1), 0)
    return pl.pallas_call(
        _swiglu_body,
        grid=(M // tm, nj),
        in_specs=[pl.BlockSpec((tm, K), lambda i, j: (i, 0)),
                  pl.BlockSpec((K, tn), lambda i, j: (0, j)),
                  pl.BlockSpec((K, tn), lambda i, j: (0, j)),
                  pl.BlockSpec((tn, D), down_block)],
        out_specs=[pl.BlockSpec((tm, tn), lambda i, j: (i, j)),
                   pl.BlockSpec((tn, D), down_block)],
        out_shape=[jax.ShapeDtypeStruct((M, N), BF16),
                   jax.ShapeDtypeStruct((N, D), BF16)],
        compiler_params=_cparams("arbitrary", "arbitrary"),
        name="ffn_up",
    )(xn, w_gate, w_up, w_down)


def _mm_res_body(a_ref, w_ref, r_ref, o_ref):
    o_ref[...] = r_ref[...] + jnp.dot(a_ref[...], w_ref[...], preferred_element_type=F32)


def matmul_residual(a, w, res, *, tm, tn):
    M, K = a.shape
    N = w.shape[1]
    return pl.pallas_call(
        _mm_res_body,
        grid=(M // tm, N // tn),
        in_specs=[pl.BlockSpec((tm, K), lambda i, j: (i, 0)),
                  pl.BlockSpec((K, tn), lambda i, j: (0, j)),
                  pl.BlockSpec((tm, tn), lambda i, j: (i, j))],
        out_specs=pl.BlockSpec((tm, tn), lambda i, j: (i, j)),
        out_shape=jax.ShapeDtypeStruct((M, N), F32),
        compiler_params=_cparams("parallel", "arbitrary"),
        name="matmul_residual",
    )(a, w, res)


def _mm2_res_body(a_ref, b_ref, wa_ref, wb_ref, r_ref, o_ref):
    acc = jnp.dot(a_ref[...], wa_ref[...].astype(BF16), preferred_element_type=F32)
    acc = acc + jnp.dot(b_ref[...], wb_ref[...].astype(BF16), preferred_element_type=F32)
    o_ref[...] = r_ref[...] + acc


def concat_matmul_residual(a, b, w, res, *, tm, tn):
    M, Ka = a.shape
    Kb = b.shape[1]
    assert Ka == Kb and w.shape[0] == Ka + Kb
    N = w.shape[1]
    return pl.pallas_call(
        _mm2_res_body,
        grid=(M // tm, N // tn),
        in_specs=[pl.BlockSpec((tm, Ka), lambda i, j: (i, 0)),
                  pl.BlockSpec((tm, Kb), lambda i, j: (i, 0)),
                  pl.BlockSpec((Ka, tn), lambda i, j: (0, j)),
                  pl.BlockSpec((Kb, tn), lambda i, j: (1, j)),
                  pl.BlockSpec((tm, tn), lambda i, j: (i, j))],
        out_specs=pl.BlockSpec((tm, tn), lambda i, j: (i, j)),
        out_shape=jax.ShapeDtypeStruct((M, N), F32),
        compiler_params=_cparams("parallel", "arbitrary"),
        name="out_proj",
    )(a, b, w, w, res)


AUG_WIDTH = LANES
V_EXTRA = 16
SLOPE_PIECES = 4
LOG2E = float(np.log2(np.e))


def _bf16_pieces(x, n):
    out, rem = [], np.asarray(x, np.float64)
    for _ in range(n):
        piece = rem.astype(ml_dtypes.bfloat16).astype(np.float64)
        out.append(piece.astype(np.float32))
        rem = rem - piece
    return out


def _position_lanes(pos_hi, pos_lo, onehot, n_blk, shape):
    lane = lax.broadcasted_iota(jnp.int32, shape, 1)
    p = SLOPE_PIECES
    extra = jnp.where((lane >= n_blk) & (lane < n_blk + p), pos_hi,
                      jnp.where((lane >= n_blk + p) & (lane < n_blk + 2 * p), pos_lo, 0.0))
    if onehot is not None:
        extra = jnp.where(lane < n_blk, onehot, extra)
    return extra.astype(BF16)


def _kv_prep_body(c_ref, s_ref, w_ref, kns_ref, knw_ref,
                  kc_o, vc_o, ks_o, vsT_o, kw_o, vwT_o, slab_ref, *, tt, n_blk):
    dh = HEAD_DIM
    shape = (tt, AUG_WIDTH)
    pos = pl.program_id(1) * tt + lax.broadcasted_iota(jnp.int32, shape, 0)
    lane = lax.broadcasted_iota(jnp.int32, shape, 1)
    blk = jnp.right_shift(pos, SEL_BLK.bit_length() - 1)
    pos_hi = (blk * SEL_BLK).astype(F32)
    pos_lo = (pos & (SEL_BLK - 1)).astype(F32)
    aug_sel = _position_lanes(pos_hi, pos_lo, (blk == lane).astype(F32), n_blk, shape)
    aug_win = _position_lanes(pos_hi, pos_lo, None, n_blk, shape)
    for g in range(NSA_KV_HEADS):
        k_cols = slice(g * dh, (g + 1) * dh)
        v_cols = slice(KV_WIDTH + g * dh, KV_WIDTH + (g + 1) * dh)
        for slab, (cols, out) in enumerate(((k_cols, kc_o), (v_cols, vc_o))):
            slab_ref[slab] = c_ref[:, cols]
            for i in range(CMP_STRIDE):
                rows = pl.ds(i, tt // CMP_STRIDE, stride=CMP_STRIDE)
                out[0, g, :, i * dh:(i + 1) * dh] = slab_ref[slab, rows, :]
        ks_o[0, g, :, 0:dh] = _rms(s_ref[:, k_cols], kns_ref[...]).astype(BF16)
        ks_o[0, g, :, dh:dh + AUG_WIDTH] = aug_sel
        kw_o[0, g, :, 0:dh] = _rms(w_ref[:, k_cols], knw_ref[...]).astype(BF16)
        kw_o[0, g, :, dh:dh + AUG_WIDTH] = aug_win
        for v_ref, out, chunk in ((s_ref, vsT_o, SEL_CHUNK), (w_ref, vwT_o, WIN_CHUNK)):
            vT = v_ref[:, v_cols].T.astype(BF16)
            ones_first = (lax.broadcasted_iota(jnp.int32, (V_EXTRA, chunk), 0) == 0).astype(BF16)
            for c in range(tt // chunk):
                out[0, g, c, 0:dh, :] = vT[:, c * chunk:(c + 1) * chunk]
                out[0, g, c, dh:dh + V_EXTRA, :] = ones_first


def kv_prep(z, kn_slc, kn_win, *, B, T, col0, tt=512):
    G, dh = NSA_KV_HEADS, HEAD_DIM
    nt = T // tt
    n_blk = T // SEL_BLK
    assert n_blk + 2 * SLOPE_PIECES <= AUG_WIDTH
    cb = col0 // (2 * KV_WIDTH)
    row = lambda b, t: b * nt + t
    kdim = dh + AUG_WIDTH
    return pl.pallas_call(
        functools.partial(_kv_prep_body, tt=tt, n_blk=n_blk),
        grid=(B, nt),
        in_specs=[pl.BlockSpec((tt, 2 * KV_WIDTH), lambda b, t: (row(b, t), cb)),
                  pl.BlockSpec((tt, 2 * KV_WIDTH), lambda b, t: (row(b, t), cb + 1)),
                  pl.BlockSpec((tt, 2 * KV_WIDTH), lambda b, t: (row(b, t), cb + 2)),
                  pl.BlockSpec((1, dh), lambda b, t: (0, 0)),
                  pl.BlockSpec((1, dh), lambda b, t: (0, 0))],
        out_specs=[pl.BlockSpec((1, G, tt // CMP_STRIDE, CMP_STRIDE * dh), lambda b, t: (b, 0, t, 0)),
                   pl.BlockSpec((1, G, tt // CMP_STRIDE, CMP_STRIDE * dh), lambda b, t: (b, 0, t, 0)),
                   pl.BlockSpec((1, G, tt, kdim), lambda b, t: (b, 0, t, 0)),
                   pl.BlockSpec((1, G, tt // SEL_CHUNK, dh + V_EXTRA, SEL_CHUNK), lambda b, t: (b, 0, t, 0, 0)),
                   pl.BlockSpec((1, G, tt, kdim), lambda b, t: (b, 0, t, 0)),
                   pl.BlockSpec((1, G, tt // WIN_CHUNK, dh + V_EXTRA, WIN_CHUNK), lambda b, t: (b, 0, t, 0, 0))],
        out_shape=[jax.ShapeDtypeStruct((B, G, T // CMP_STRIDE, CMP_STRIDE * dh), F32),
                   jax.ShapeDtypeStruct((B, G, T // CMP_STRIDE, CMP_STRIDE * dh), F32),
                   jax.ShapeDtypeStruct((B, G, T, kdim), BF16),
                   jax.ShapeDtypeStruct((B, G, T // SEL_CHUNK, dh + V_EXTRA, SEL_CHUNK), BF16),
                   jax.ShapeDtypeStruct((B, G, T, kdim), BF16),
                   jax.ShapeDtypeStruct((B, G, T // WIN_CHUNK, dh + V_EXTRA, WIN_CHUNK), BF16)],
        scratch_shapes=[pltpu.VMEM((2, tt, dh), F32)],
        compiler_params=_cparams("parallel", "parallel"),
        name="kv_prep",
    )(z, z, z, kn_slc, kn_win)


def _compress_body(xk_ref, xv_ref, pk_ref, pv_ref, w1k_ref, w2k_ref, w1v_ref, w2v_ref, kn_ref,
                   kc_o, vcT_o, *, n_blk):
    dh = HEAD_DIM

    def phi(x_ref, p_ref, w1_ref, w2_ref):
        x = x_ref[0, 0]
        first = jnp.dot((x + p_ref[0:1, :]).astype(BF16), w1_ref[0], preferred_element_type=F32)
        second = jnp.dot((x + p_ref[1:2, :]).astype(BF16), w1_ref[1], preferred_element_type=F32)
        n = second.shape[0]
        h = first + pltpu.roll(second, n - 1, 0)
        return jnp.dot(_silu(h).astype(BF16), w2_ref[...], preferred_element_type=F32)

    k = _rms(phi(xk_ref, pk_ref, w1k_ref, w2k_ref), kn_ref[...])
    nc = k.shape[0]
    shape = (nc, AUG_WIDTH)
    start = (lax.broadcasted_iota(jnp.int32, shape, 0) * CMP_STRIDE).astype(F32)
    centre = jnp.full(shape, (CMP_LEN - 1) / 2.0, F32)
    kc_o[0, 0, :, 0:dh] = k.astype(BF16)
    kc_o[0, 0, :, dh:dh + AUG_WIDTH] = _position_lanes(start, centre, None, n_blk, shape)
    vcT_o[0, 0] = phi(xv_ref, pv_ref, w1v_ref, w2v_ref).T.astype(BF16)


def compress_kv(xk, xv, pos_k, pos_v, w1k, w2k, w1v, w2v, kn_cmp, *, n_blk):
    B, G, nc, half = xk.shape
    dh = half // CMP_STRIDE
    assert CMP_LEN == 2 * CMP_STRIDE
    x_spec = pl.BlockSpec((1, 1, nc, half), lambda b, g: (b, g, 0, 0))
    full = lambda shape: pl.BlockSpec(shape, lambda b, g: (0,) * len(shape))
    return pl.pallas_call(
        functools.partial(_compress_body, n_blk=n_blk),
        grid=(B, G),
        in_specs=[x_spec, x_spec, full((2, half)), full((2, half)),
                  full((2, half, dh)), full((dh, dh)), full((2, half, dh)), full((dh, dh)),
                  full((1, dh))],
        out_specs=[pl.BlockSpec((1, 1, nc, dh + AUG_WIDTH), lambda b, g: (b, g, 0, 0)),
                   pl.BlockSpec((1, 1, dh, nc), lambda b, g: (b, g, 0, 0))],
        out_shape=[jax.ShapeDtypeStruct((B, G, nc, dh + AUG_WIDTH), BF16),
                   jax.ShapeDtypeStruct((B, G, dh, nc), BF16)],
        compiler_params=_cparams("parallel", "parallel"),
        name="compress_kv",
    )(xk, xv, pos_k.reshape(2, half), pos_v.reshape(2, half),
      w1k.reshape(2, half, dh), w2k, w1v.reshape(2, half, dh), w2v, kn_cmp)


def _nsa_body(q_ref, gl_ref, gb_ref, qn_ref, lanes_ref, kc_ref, vcT_ref,
              ks_ref, vsT_ref, kw_ref, vwT_ref, ovT_ref, kmq_ref, o_ref,
              score_ref, gate_ref, qc_ref, qs_ref, m_ref, acc_ref, part_ref, sbuf_ref):
    dh, hg, tq = HEAD_DIM, NSA_GROUP, Q_TILE
    cols = hg * tq
    qi = pl.program_id(2)
    t0 = qi * tq
    contract_last = (((1,), (1,)), ((), ()))
    n_blk = ovT_ref.shape[0]

    scale = dh ** -0.5 * LOG2E
    q_heads = [(_rms(q_ref[:, h * dh:(h + 1) * dh], qn_ref[...]) * scale).astype(BF16)
               for h in range(hg)]

    def stack_queries(aug_heads):
        return jnp.concatenate([jnp.concatenate([q_heads[h], aug_heads[h]], axis=1)
                                for h in range(hg)], axis=0)

    hc = kmq_ref.shape[1]
    assert cols % hc == 0 and hc % tq == 0

    def reset(b):
        m_ref[b] = jnp.full((1, cols), NEG_INF, F32)
        acc_ref[b] = jnp.zeros((dh + V_EXTRA, cols), F32)

    def attend(b, queries_ref, parts):
        groups = [slice(i * hc, (i + 1) * hc) for i in range(cols // hc)]
        unit = 0
        for k, _, _, _ in parts:
            for cs in groups:
                sbuf_ref[unit, 0:k.shape[0], :] = lax.dot_general(
                    k, queries_ref[cs, :], contract_last, preferred_element_type=F32)
                unit += 1
        unit = 0
        for (k, v_pieces, mask, bias) in parts:
            for cs in groups:
                s_ref = sbuf_ref.at[unit, 0:k.shape[0], :]
                unit += 1
                if mask is not None:
                    s_ref[...] = jnp.where(mask(), s_ref[...], NEG_INF)
                if bias is not None:
                    s_ref[...] = s_ref[...] + bias
                m_old = m_ref[b, :, cs]
                m_new = jnp.maximum(m_old, jnp.max(s_ref[...], axis=0, keepdims=True))
                alpha = jnp.exp2(m_old - m_new)
                p = jnp.exp2(s_ref[...] - m_new).astype(BF16)
                m_ref[b, :, cs] = m_new
                pv = None
                for i, vT in enumerate(v_pieces):
                    n = vT.shape[1]
                    d = jnp.dot(vT, p[i * n:(i + 1) * n, :], preferred_element_type=F32)
                    pv = d if pv is None else pv + d
                acc_ref[b, :, cs] = alpha * acc_ref[b, :, cs] + pv

    def result(b):
        return acc_ref[b, 0:dh, :] * (1.0 / acc_ref[b, dh:dh + 1, :])

    nc = kc_ref.shape[2]
    qc_ref[...] = stack_queries([jnp.broadcast_to(lanes_ref[0, hg + h:hg + h + 1, :], (tq, AUG_WIDTH)).astype(BF16)
                                 for h in range(hg)])
    s = lax.dot_general(kc_ref[0, 0], qc_ref[...], contract_last, preferred_element_type=F32)
    c_idx = lax.broadcasted_iota(jnp.int32, (nc, cols), 0)
    t_c = t0 + (lax.broadcasted_iota(jnp.int32, (nc, cols), 1) & (tq - 1))
    mask = (c_idx * CMP_STRIDE + (CMP_LEN - 1)) <= t_c
    s = jnp.where(mask, s, NEG_INF)
    m_c = jnp.max(s, axis=0, keepdims=True)
    p_c = jnp.where(mask, jnp.exp2(s - m_c), 0.0)
    l_c = jnp.sum(p_c, axis=0, keepdims=True)
    p_c = p_c * jnp.where(l_c > 0.0, 1.0 / l_c, 0.0)
    o_cmp = jnp.dot(vcT_ref[0, 0], p_c.astype(BF16), preferred_element_type=F32)

    cw = WIN_CHUNK
    assert cw == tq and WINDOW % cw == 0 and cw == kmq_ref.shape[0]
    n_back = WINDOW // cw
    WIN, SEL = 0, 1

    def win_part(c, mask=None, bias=None):
        k = kw_ref[0, 0, pl.ds(pl.multiple_of(c * cw, cw), cw), :]
        return k, [vwT_ref[0, 0, c]], mask, bias

    threshold = jnp.where(qi >= n_back, 0, cw)
    reset(WIN)
    attend(WIN, qc_ref,
           [win_part(qi, mask=lambda: kmq_ref[...] <= 0)]
           + [win_part(jnp.maximum(qi - j, 0), bias=jnp.where(qi >= j, 0.0, NEG_INF))
              for j in range(1, n_back)]
           + [win_part(jnp.maximum(qi - n_back, 0), mask=lambda: kmq_ref[...] > threshold)])

    gate_ref[...] = jax.nn.sigmoid(gl_ref[...] + gb_ref[...]).T

    def gate(h, branch):
        r = h * N_BRANCH + branch
        return gate_ref[r:r + 1, :]

    o_win = result(WIN)
    for h in range(hg):
        cs = slice(h * tq, (h + 1) * tq)
        part_ref[:, cs] = gate(h, 0) * o_cmp[:, cs] + gate(h, 2) * o_win[:, cs]

    p_sum = p_c[:, 0:tq]
    for h in range(1, hg):
        p_sum = p_sum + p_c[:, h * tq:(h + 1) * tq]
    hi = p_sum.astype(BF16)
    rem = p_sum - hi.astype(F32)
    mid = rem.astype(BF16)
    lo = (rem - mid.astype(F32)).astype(BF16)
    ovT = ovT_ref[...]
    imp = (jnp.dot(ovT, hi, preferred_element_type=F32)
           + jnp.dot(ovT, mid, preferred_element_type=F32)
           + jnp.dot(ovT, lo, preferred_element_type=F32))
    blk = lax.broadcasted_iota(jnp.int32, (n_blk, tq), 0)
    t_b = t0 + lax.broadcasted_iota(jnp.int32, (n_blk, tq), 1)
    cur = jnp.right_shift(t_b, SEL_BLK.bit_length() - 1)
    valid = blk * SEL_BLK <= t_b
    forced = (blk == 0) | (blk == cur) | (blk == cur - 1)
    score = jnp.where(valid, imp + FORCE_BONUS * forced.astype(F32), NEG_INF)
    score_ref[...] = score
    rank = jnp.zeros((n_blk, tq), F32)
    for j in range(n_blk):
        sj = score_ref[j:j + 1, :]
        beats = (sj > score) | ((sj == score) & (blk > j))
        rank = rank + beats.astype(F32)
    block_bias = jnp.where(rank < float(min(N_SEL, n_blk)), 0.0, NEG_INF)

    assert 2 * SLOPE_PIECES <= 8
    ones_rows = (lax.broadcasted_iota(jnp.int32, (8, tq), 0) < 2 * SLOPE_PIECES).astype(F32)
    extra_t = jnp.concatenate([block_bias, ones_rows,
                               jnp.zeros((AUG_WIDTH - n_blk - 8, tq), F32)], axis=0).T
    qs_ref[...] = stack_queries([(extra_t * lanes_ref[0, h:h + 1, :]).astype(BF16) for h in range(hg)])

    ck = SEL_CHUNK
    assert ck == tq and ck == kmq_ref.shape[0]

    def sel_part(c, n=1, mask=None, bias=None):
        k = ks_ref[0, 0, pl.ds(pl.multiple_of(c * ck, ck), n * ck), :]
        return k, [vsT_ref[0, 0, c + i] for i in range(n)], mask, bias

    reset(SEL)

    odd = qi % 2 == 1
    attend(SEL, qs_ref,
           [sel_part(qi, mask=lambda: kmq_ref[...] <= 0),
            sel_part(jnp.maximum(qi - 1, 0), bias=jnp.where(odd, 0.0, NEG_INF))])

    @pl.loop(0, qi // 2)
    def _(i):
        attend(SEL, qs_ref, [sel_part(2 * i, n=2)])

    o_slc = result(SEL)
    for h in range(hg):
        cs = slice(h * tq, (h + 1) * tq)
        o_h = part_ref[:, cs] + gate(h, 1) * o_slc[:, cs]
        o_ref[:, h * dh:(h + 1) * dh] = o_h.T.astype(o_ref.dtype)


def nsa_attention(z, gl, gate_b, q_norm, kcmp, vcmpT, ks, vsT, kw, vwT, *, B, T):
    G, hg, dh, tq = NSA_KV_HEADS, NSA_GROUP, HEAD_DIM, Q_TILE
    nq = T // tq
    nc = kcmp.shape[2]
    n_blk = T // SEL_BLK
    kdim = dh + AUG_WIDTH
    cols = hg * tq
    assert SEL_CHUNK == WIN_CHUNK == tq
    n_units = (WINDOW // WIN_CHUNK + 1) * (cols // NSA_COLS_PER_DOT)
    assert SEL_CHUNK % SEL_BLK == 0 and T % SEL_CHUNK == 0 and gl.shape[1] == G * LANES
    assert SEL_CHUNK % tq == 0 and n_blk % 8 == 0 and n_blk + 8 <= AUG_WIDTH

    slopes = np.exp2(-8.0 * (np.arange(NSA_HEADS, dtype=np.float64) + 1.0) / NSA_HEADS).reshape(G, hg)
    lanes = np.zeros((G, 2 * hg, AUG_WIDTH), np.float32)
    lanes[:, :hg, :n_blk] = 1.0
    for i, piece in enumerate(_bf16_pieces(slopes * LOG2E, SLOPE_PIECES)):
        for lane in (n_blk + i, n_blk + SLOPE_PIECES + i):
            lanes[:, :hg, lane] = piece
            lanes[:, hg:, lane] = piece
    cs = np.arange(nc)[:, None] * CMP_STRIDE
    bs = np.arange(n_blk)[None, :] * SEL_BLK
    overlap = np.clip(np.minimum(cs + CMP_LEN, bs + SEL_BLK) - np.maximum(cs, bs), 0, None) / CMP_LEN
    overlap[(T - CMP_LEN) // CMP_STRIDE + 1:] = 0.0
    ovT = jnp.asarray(overlap.T, dtype=BF16)
    hc = NSA_COLS_PER_DOT
    kmq = jnp.asarray(np.arange(SEL_CHUNK)[:, None] - (np.arange(hc)[None, :] % tq), dtype=jnp.int32)

    row = lambda b, qi: b * nq + qi
    per_bg = lambda shape: pl.BlockSpec((1, 1) + shape, lambda b, g, qi: (b, g) + (0,) * len(shape))
    const = lambda shape: pl.BlockSpec(shape, lambda b, g, qi: (0,) * len(shape))
    return pl.pallas_call(
        _nsa_body,
        grid=(B, G, nq),
        in_specs=[pl.BlockSpec((tq, hg * dh), lambda b, g, qi: (row(b, qi), g)),
                  pl.BlockSpec((tq, LANES), lambda b, g, qi: (row(b, qi), g)),
                  pl.BlockSpec((1, LANES), lambda b, g, qi: (0, g)),
                  const((1, dh)),
                  pl.BlockSpec((1, 2 * hg, AUG_WIDTH), lambda b, g, qi: (g, 0, 0)),
                  per_bg((nc, kdim)), per_bg((dh, nc)),
                  per_bg((T, kdim)), per_bg((T // SEL_CHUNK, dh + V_EXTRA, SEL_CHUNK)),
                  per_bg((T, kdim)), per_bg((T // WIN_CHUNK, dh + V_EXTRA, WIN_CHUNK)),
                  const((n_blk, nc)), const((SEL_CHUNK, hc))],
        out_specs=pl.BlockSpec((tq, hg * dh), lambda b, g, qi: (row(b, qi), g)),
        out_shape=jax.ShapeDtypeStruct((B * T, NSA_WIDTH), BF16),
        scratch_shapes=[pltpu.VMEM((n_blk, tq), F32), pltpu.VMEM((LANES, tq), F32),
                        pltpu.VMEM((cols, kdim), BF16), pltpu.VMEM((cols, kdim), BF16),
                        pltpu.VMEM((2, 1, cols), F32),
                        pltpu.VMEM((2, dh + V_EXTRA, cols), F32), pltpu.VMEM((dh, cols), F32),
                        pltpu.VMEM((n_units, 2 * SEL_CHUNK, NSA_COLS_PER_DOT), F32)],
        compiler_params=_cparams("parallel", "parallel", "arbitrary"),
        name="nsa_attention",
    )(z, gl, gate_b, q_norm, jnp.asarray(lanes), kcmp, vcmpT, ks, vsT, kw, vwT, ovT, kmq)


def _conv_body(a_ref, b_ref, ah_ref, bh_ref, w_ref, cb_ref, lg_ref, lb_ref, o_ref,
               h_ref, g_ref, y_ref, *, tt):
    C = a_ref.shape[1]
    halo = ah_ref[...] * jax.nn.sigmoid(bh_ref[...])
    h_ref[0:CONV_HALO, :] = jnp.where(pl.program_id(1) == 0, 0.0, halo)
    h_ref[CONV_HALO:CONV_HALO + tt, :] = a_ref[...] * jax.nn.sigmoid(b_ref[...])
    h_ref[CONV_HALO + tt:CONV_HALO + tt + SUBLANES, :] = jnp.zeros((SUBLANES, C), F32)
    lead = CONV_HALO - (CONV_KERNEL - 1)
    rows = tt + SUBLANES
    for s in range(C // LANES):
        cs = slice(s * LANES, (s + 1) * LANES)
        acc = None
        for r in range(SUBLANES):
            group = None
            for o in range(lead, lead + CONV_KERNEL):
                if o % SUBLANES == r:
                    term = w_ref[o - lead:o - lead + 1, cs] * h_ref[o - r:o - r + rows, cs]
                    group = term if group is None else group + term
            if r == 0:
                piece = group[0:tt, :]
            else:
                g_ref[r] = group
                piece = g_ref[r, r:r + tt, :]
            acc = piece if acc is None else acc + piece
        y_ref[:, cs] = acc + cb_ref[:, cs]
    y = y_ref[...]
    mu = jnp.mean(y, axis=-1, keepdims=True)
    var = jnp.mean(jnp.square(y - mu), axis=-1, keepdims=True)
    yn = (y - mu) * lax.rsqrt(var + LN_EPS) * lg_ref[...] + lb_ref[...]
    o_ref[...] = _silu(yn).astype(o_ref.dtype)


def conformer_conv(z, conv_w, conv_b, ln_g, ln_b, *, B, T, col0, C, tt=128):
    nt = T // tt
    ca = col0 // C
    hb = tt // CONV_HALO
    row = lambda b, t: b * nt + t
    halo_row = lambda b, t: jnp.maximum(row(b, t) * hb - 1, 0)
    const = lambda shape: pl.BlockSpec(shape, lambda b, t: (0,) * len(shape))
    return pl.pallas_call(
        functools.partial(_conv_body, tt=tt),
        grid=(B, nt),
        in_specs=[pl.BlockSpec((tt, C), lambda b, t: (row(b, t), ca)),
                  pl.BlockSpec((tt, C), lambda b, t: (row(b, t), ca + 1)),
                  pl.BlockSpec((CONV_HALO, C), lambda b, t: (halo_row(b, t), ca)),
                  pl.BlockSpec((CONV_HALO, C), lambda b, t: (halo_row(b, t), ca + 1)),
                  const((CONV_KERNEL, C)), const((1, C)), const((1, C)), const((1, C))],
        out_specs=pl.BlockSpec((tt, C), lambda b, t: (row(b, t), 0)),
        out_shape=jax.ShapeDtypeStruct((B * T, C), BF16),
        scratch_shapes=[pltpu.VMEM((CONV_HALO + tt + SUBLANES, C), F32),
                        pltpu.VMEM((SUBLANES, tt + SUBLANES, LANES), F32),
                        pltpu.VMEM((tt, C), F32)],
        compiler_params=_cparams("parallel", "parallel"),
        name="conformer_conv",
    )(z, z, z, z, conv_w, conv_b, ln_g, ln_b)


def _mem_attn_body(x_ref, gq_ref, wq_ref, kv_ref, qn_ref, kn_ref, wo_ref, gn_ref, o_ref, on_ref):
    dh = HEAD_DIM
    width = MEM_HEADS * dh
    scale = dh ** -0.5
    q_all = jnp.dot(_rms(x_ref[...], gq_ref[...]).astype(BF16), wq_ref[...], preferred_element_type=F32)
    heads = [slice(h * dh, (h + 1) * dh) for h in range(MEM_HEADS)]
    scores = []
    for cs in heads:
        q = (_rms(q_all[:, cs], qn_ref[...]) * scale).astype(BF16)
        k = _rms(kv_ref[:, cs], kn_ref[...]).astype(BF16)
        scores.append(lax.dot_general(q, k, (((1,), (1,)), ((), ())), preferred_element_type=F32))
    outs = []
    for h, s in enumerate(scores):
        v = kv_ref[:, width + h * dh:width + (h + 1) * dh].astype(BF16)
        p = jnp.exp(s - jnp.max(s, axis=-1, keepdims=True))
        inv = 1.0 / jnp.sum(p, axis=-1, keepdims=True)
        outs.append((jnp.dot(p.astype(BF16), v, preferred_element_type=F32) * inv).astype(BF16))
    o = jnp.concatenate(outs, axis=1)
    y = x_ref[...] + jnp.dot(o, wo_ref[...], preferred_element_type=F32)
    o_ref[...] = y
    on_ref[...] = _rms(y, gn_ref[...]).astype(on_ref.dtype)


def mem_attention(x, q_gain, w_mq, kvm, mq_norm, mk_norm, w_mo, next_norm, *, B, T, M, tt=256):
    D = x.shape[1]
    width = MEM_HEADS * HEAD_DIM
    nt = T // tt
    row = lambda b, t: b * nt + t
    const = lambda shape: pl.BlockSpec(shape, lambda b, t: (0,) * len(shape))
    return pl.pallas_call(
        _mem_attn_body,
        grid=(B, nt),
        in_specs=[pl.BlockSpec((tt, D), lambda b, t: (row(b, t), 0)),
                  const((1, D)), const((D, width)),
                  pl.BlockSpec((M, 2 * width), lambda b, t: (b, 0)),
                  const((1, HEAD_DIM)), const((1, HEAD_DIM)), const((width, D)),
                  const((1, D))],
        out_specs=[pl.BlockSpec((tt, D), lambda b, t: (row(b, t), 0)),
                   pl.BlockSpec((tt, D), lambda b, t: (row(b, t), 0))],
        out_shape=[jax.ShapeDtypeStruct((B * T, D), F32),
                   jax.ShapeDtypeStruct((B * T, D), BF16)],
        compiler_params=_cparams("parallel", "parallel"),
        name="mem_attention",
    )(x, q_gain, w_mq, kvm, mq_norm, mk_norm, w_mo, next_norm)


def kernel(x, mem, norm_mix, w_in, gate_b, q_norm, k_norm_cmp, k_norm_slc, k_norm_win, cmp_pos_k, cmp_pos_v, cmp_k_w1, cmp_k_w2, cmp_v_w1, cmp_v_w2, conv_w, conv_b, conv_ln_g, conv_ln_b, w_out, norm_mem_q, norm_mem_kv, w_mq, w_mk, w_mv, mq_norm, mk_norm, w_mo, norm_ffn, w_gate, w_up, w_down):
    B, T, D = x.shape
    M = mem.shape[1]
    depth = w_in.shape[0]
    G, hg = NSA_KV_HEADS, NSA_GROUP
    C = D - NSA_WIDTH
    n_gate = N_BRANCH * NSA_HEADS
    kv_all = 6 * KV_WIDTH
    assert w_in.shape[2] == NSA_WIDTH + kv_all + n_gate + 2 * C
    assert C == NSA_WIDTH, "column blocking below assumes equal NSA and conv widths"

    row1 = lambda v: v.reshape(1, -1)
    xf = x.reshape(B * T, D)
    memf = mem.reshape(B * M, D)
    u0 = NSA_WIDTH + kv_all + n_gate
    for l in range(depth):
        w = w_in[l]
        w_t = jnp.swapaxes(w, 0, 1)
        wg = w[:, NSA_WIDTH + kv_all:u0].reshape(D, G, hg * N_BRANCH)
        wg = jnp.pad(wg, ((0, 0), (0, 0), (0, LANES - hg * N_BRANCH))).reshape(D, G * LANES)
        gb = jnp.pad(gate_b[l].reshape(G, hg * N_BRANCH), ((0, 0), (0, LANES - hg * N_BRANCH))).reshape(1, G * LANES)

        z, gl = in_projection(rmsnorm_bf16(xf, row1(norm_mix[l]), tm=512), w_t, wg, tm=2048, tn=512,
                              segments=((0, NSA_WIDTH), (u0, 2 * C), (NSA_WIDTH, kv_all)))
        kv_col0 = NSA_WIDTH + 2 * C
        kc, vc, ks, vsT, kw, vwT = kv_prep(z, row1(k_norm_slc[l]), row1(k_norm_win[l]), B=B, T=T, col0=kv_col0)
        kcmp, vcmpT = compress_kv(kc, vc, cmp_pos_k[l], cmp_pos_v[l],
                                  cmp_k_w1[l].astype(BF16), cmp_k_w2[l].astype(BF16),
                                  cmp_v_w1[l].astype(BF16), cmp_v_w2[l].astype(BF16), row1(k_norm_cmp[l]),
                                  n_blk=T // SEL_BLK)
        o_nsa = nsa_attention(z, gl, gb, row1(q_norm[l]), kcmp, vcmpT, ks, vsT, kw, vwT, B=B, T=T)
        o_conv = conformer_conv(z, conv_w[l], row1(conv_b[l]), row1(conv_ln_g[l]), row1(conv_ln_b[l]),
                                B=B, T=T, col0=NSA_WIDTH, C=C)
        xf = concat_matmul_residual(o_nsa, o_conv, w_out[l], xf, tm=2048, tn=512)

        w_mkv = jnp.concatenate([w_mk[l], w_mv[l]], axis=1).astype(BF16)
        kvm = norm_matmul(memf, row1(norm_mem_kv[l]), w_mkv, tm=B * M, tn=512)
        xf, xn = mem_attention(xf, row1(norm_mem_q[l]), w_mq[l].astype(BF16), kvm,
                               row1(mq_norm[l]), row1(mk_norm[l]), w_mo[l].astype(BF16),
                               row1(norm_ffn[l]), B=B, T=T, M=M, tt=512)

        hidden, w_down_bf16 = swiglu_up(xn, w_gate[l], w_up[l], w_down[l], tm=1024, tn=512)
        xf = matmul_residual(hidden, w_down_bf16, xf, tm=1024, tn=512)
    return xf.reshape(B, T, D)
```

```python
import functools

import ml_dtypes
import numpy as np
import jax
import jax.numpy as jnp
from jax import lax
from jax.experimental import pallas as pl
from jax.experimental.pallas import tpu as pltpu

F32 = jnp.float32
BF16 = jnp.bfloat16

HEAD_DIM = 128
NSA_HEADS = 8
NSA_KV_HEADS = 2
NSA_GROUP = NSA_HEADS // NSA_KV_HEADS
NSA_WIDTH = NSA_HEADS * HEAD_DIM
KV_WIDTH = NSA_KV_HEADS * HEAD_DIM
N_BRANCH = 3
CONV_KERNEL = 31
CMP_LEN = 32
CMP_STRIDE = 16
SEL_BLK = 64
N_SEL = 16
WINDOW = 512
MEM_HEADS = 4
NEG_INF = -1e30
FORCE_BONUS = 1e3
RMS_EPS = 1e-6
LN_EPS = 1e-5

LANES = 128
SUBLANES = 8
VMEM_LIMIT_BYTES = 56 * 1024 * 1024

Q_TILE = 256
SEL_CHUNK = 256
WIN_CHUNK = 256
NSA_COLS_PER_DOT = 512
CONV_HALO = 32


def _cparams(*sem):
    return pltpu.CompilerParams(dimension_semantics=sem, vmem_limit_bytes=VMEM_LIMIT_BYTES)


def _rms(x, g, eps=RMS_EPS):
    ms = jnp.mean(x * x, axis=-1, keepdims=True)
    return x * lax.rsqrt(ms + eps) * g


def _silu(x):
    return x * jax.nn.sigmoid(x)


def _norm_to_scratch(x_ref, g_ref, xn_ref):
    @pl.when(pl.program_id(1) == 0)
    def _():
        xn_ref[...] = _rms(x_ref[...], g_ref[...]).astype(xn_ref.dtype)


def _norm_mm_body(x_ref, g_ref, w_ref, o_ref, xn_ref):
    _norm_to_scratch(x_ref, g_ref, xn_ref)
    o_ref[...] = jnp.dot(xn_ref[...], w_ref[...], preferred_element_type=F32).astype(o_ref.dtype)


def _rmsnorm_body(x_ref, g_ref, o_ref):
    o_ref[...] = _rms(x_ref[...], g_ref[...]).astype(o_ref.dtype)


def rmsnorm_bf16(x, g, *, tm):
    M, K = x.shape
    return pl.pallas_call(
        _rmsnorm_body,
        grid=(M // tm,),
        in_specs=[pl.BlockSpec((tm, K), lambda i: (i, 0)), pl.BlockSpec((1, K), lambda i: (0, 0))],
        out_specs=pl.BlockSpec((tm, K), lambda i: (i, 0)),
        out_shape=jax.ShapeDtypeStruct((M, K), BF16),
        compiler_params=_cparams("parallel"),
        name="rmsnorm",
    )(x, g)


def _in_proj_body(xn_ref, wt_ref, wg_ref, o_ref, og_ref):
    @pl.when(pl.program_id(1) == 0)
    def _():
        og_ref[...] = jnp.dot(xn_ref[...], wg_ref[...].astype(BF16), preferred_element_type=F32)

    o_ref[...] = lax.dot_general(xn_ref[...], wt_ref[...].astype(BF16), (((1,), (1,)), ((), ())),
                                 preferred_element_type=F32)


def _swiglu_body(x_ref, wg_ref, wu_ref, wd_ref, o_ref, wd_o):
    x = x_ref[...]
    a = jnp.dot(x, wg_ref[...].astype(BF16), preferred_element_type=F32)
    b = jnp.dot(x, wu_ref[...].astype(BF16), preferred_element_type=F32)
    o_ref[...] = (_silu(a) * b).astype(o_ref.dtype)

    @pl.when(pl.program_id(0) == 0)
    def _():
        wd_o[...] = wd_ref[...].astype(wd_o.dtype)


def norm_matmul(x, g, w, *, tm, tn, out_dtype=F32):
    M, K = x.shape
    N = w.shape[1]
    return pl.pallas_call(
        _norm_mm_body,
        grid=(M // tm, N // tn),
        in_specs=[pl.BlockSpec((tm, K), lambda i, j: (i, 0)),
                  pl.BlockSpec((1, K), lambda i, j: (0, 0)),
                  pl.BlockSpec((K, tn), lambda i, j: (0, j))],
        out_specs=pl.BlockSpec((tm, tn), lambda i, j: (i, j)),
        out_shape=jax.ShapeDtypeStruct((M, N), out_dtype),
        scratch_shapes=[pltpu.VMEM((tm, K), BF16)],
        compiler_params=_cparams("parallel", "arbitrary"),
        name="norm_matmul",
    )(x, g, w)


def in_projection(xn, w_t, wg, *, segments, tm, tn):
    M, K = xn.shape
    NG = wg.shape[1]
    assert all(start % SUBLANES == 0 and size % tn == 0 for start, size in segments)
    N = sum(size for _, size in segments)

    def row_start(i, j):
        group, first = 0, 0
        for seg_start, size in segments:
            group = jnp.where(j >= first, seg_start // SUBLANES + (j - first) * (tn // SUBLANES), group)
            first += size // tn
        return (group * SUBLANES, 0)

    return pl.pallas_call(
        _in_proj_body,
        grid=(M // tm, N // tn),
        in_specs=[pl.BlockSpec((tm, K), lambda i, j: (i, 0)),
                  pl.BlockSpec((pl.Element(tn), pl.Element(K)), row_start),
                  pl.BlockSpec((K, NG), lambda i, j: (0, 0))],
        out_specs=[pl.BlockSpec((tm, tn), lambda i, j: (i, j)),
                   pl.BlockSpec((tm, NG), lambda i, j: (i, 0))],
        out_shape=[jax.ShapeDtypeStruct((M, N), F32),
                   jax.ShapeDtypeStruct((M, NG), F32)],
        compiler_params=_cparams("parallel", "arbitrary"),
        name="in_proj",
    )(xn, w_t, wg)


def swiglu_up(xn, w_gate, w_up, w_down, *, tm, tn):
    M, K = xn.shape
    N = w_gate.shape[1]
    D = w_down.shape[1]
    nj = N // tn
    assert w_down.shape[0] == N
    down_block = lambda i, j: (jnp.where(i == 0, j, nj - 1), 0)
    return pl.pallas_call(
        _swiglu_body,
        grid=(M // tm, nj),
        in_specs=[pl.BlockSpec((tm, K), lambda i, j: (i, 0)),
                  pl.BlockSpec((K, tn), lambda i, j: (0, j)),
                  pl.BlockSpec((K, tn), lambda i, j: (0, j)),
                  pl.BlockSpec((tn, D), down_block)],
        out_specs=[pl.BlockSpec((tm, tn), lambda i, j: (i, j)),
                   pl.BlockSpec((tn, D), down_block)],
        out_shape=[jax.ShapeDtypeStruct((M, N), BF16),
                   jax.ShapeDtypeStruct((N, D), BF16)],
        compiler_params=_cparams("arbitrary", "arbitrary"),
        name="ffn_up",
    )(xn, w_gate, w_up, w_down)


def _mm_res_body(a_ref, w_ref, r_ref, o_ref):
    o_ref[...] = r_ref[...] + jnp.dot(a_ref[...], w_ref[...], preferred_element_type=F32)


def matmul_residual(a, w, res, *, tm, tn):
    M, K = a.shape
    N = w.shape[1]
    return pl.pallas_call(
        _mm_res_body,
        grid=(M // tm, N // tn),
        in_specs=[pl.BlockSpec((tm, K), lambda i, j: (i, 0)),
                  pl.BlockSpec((K, tn), lambda i, j: (0, j)),
                  pl.BlockSpec((tm, tn), lambda i, j: (i, j))],
        out_specs=pl.BlockSpec((tm, tn), lambda i, j: (i, j)),
        out_shape=jax.ShapeDtypeStruct((M, N), F32),
        compiler_params=_cparams("parallel", "arbitrary"),
        name="matmul_residual",
    )(a, w, res)


def _mm2_res_body(a_ref, b_ref, wa_ref, wb_ref, r_ref, o_ref):
    acc = jnp.dot(a_ref[...], wa_ref[...].astype(BF16), preferred_element_type=F32)
    acc = acc + jnp.dot(b_ref[...], wb_ref[...].astype(BF16), preferred_element_type=F32)
    o_ref[...] = r_ref[...] + acc


def concat_matmul_residual(a, b, w, res, *, tm, tn):
    M, Ka = a.shape
    Kb = b.shape[1]
    assert Ka == Kb and w.shape[0] == Ka + Kb
    N = w.shape[1]
    return pl.pallas_call(
        _mm2_res_body,
        grid=(M // tm, N // tn),
        in_specs=[pl.BlockSpec((tm, Ka), lambda i, j: (i, 0)),
                  pl.BlockSpec((tm, Kb), lambda i, j: (i, 0)),
                  pl.BlockSpec((Ka, tn), lambda i, j: (0, j)),
                  pl.BlockSpec((Kb, tn), lambda i, j: (1, j)),
                  pl.BlockSpec((tm, tn), lambda i, j: (i, j))],
        out_specs=pl.BlockSpec((tm, tn), lambda i, j: (i, j)),
        out_shape=jax.ShapeDtypeStruct((M, N), F32),
        compiler_params=_cparams("parallel", "arbitrary"),
        name="out_proj",
    )(a, b, w, w, res)


AUG_WIDTH = LANES
V_EXTRA = 16
SLOPE_PIECES = 4
LOG2E = float(np.log2(np.e))


def _bf16_pieces(x, n):
    out, rem = [], np.asarray(x, np.float64)
    for _ in range(n):
        piece = rem.astype(ml_dtypes.bfloat16).astype(np.float64)
        out.append(piece.astype(np.float32))
        rem = rem - piece
    return out


def _position_lanes(pos_hi, pos_lo, onehot, n_blk, shape):
    lane = lax.broadcasted_iota(jnp.int32, shape, 1)
    p = SLOPE_PIECES
    extra = jnp.where((lane >= n_blk) & (lane < n_blk + p), pos_hi,
                      jnp.where((lane >= n_blk + p) & (lane < n_blk + 2 * p), pos_lo, 0.0))
    if onehot is not None:
        extra = jnp.where(lane < n_blk, onehot, extra)
    return extra.astype(BF16)


def _kv_prep_body(c_ref, s_ref, w_ref, kns_ref, knw_ref,
                  kc_o, vc_o, ks_o, vsT_o, kw_o, vwT_o, slab_ref, *, tt, n_blk):
    dh = HEAD_DIM
    shape = (tt, AUG_WIDTH)
    pos = pl.program_id(1) * tt + lax.broadcasted_iota(jnp.int32, shape, 0)
    lane = lax.broadcasted_iota(jnp.int32, shape, 1)
    blk = jnp.right_shift(pos, SEL_BLK.bit_length() - 1)
    pos_hi = (blk * SEL_BLK).astype(F32)
    pos_lo = (pos & (SEL_BLK - 1)).astype(F32)
    aug_sel = _position_lanes(pos_hi, pos_lo, (blk == lane).astype(F32), n_blk, shape)
    aug_win = _position_lanes(pos_hi, pos_lo, None, n_blk, shape)
    for g in range(NSA_KV_HEADS):
        k_cols = slice(g * dh, (g + 1) * dh)
        v_cols = slice(KV_WIDTH + g * dh, KV_WIDTH + (g + 1) * dh)
        for slab, (cols, out) in enumerate(((k_cols, kc_o), (v_cols, vc_o))):
            slab_ref[slab] = c_ref[:, cols]
            for i in range(CMP_STRIDE):
                rows = pl.ds(i, tt // CMP_STRIDE, stride=CMP_STRIDE)
                out[0, g, :, i * dh:(i + 1) * dh] = slab_ref[slab, rows, :]
        ks_o[0, g, :, 0:dh] = _rms(s_ref[:, k_cols], kns_ref[...]).astype(BF16)
        ks_o[0, g, :, dh:dh + AUG_WIDTH] = aug_sel
        kw_o[0, g, :, 0:dh] = _rms(w_ref[:, k_cols], knw_ref[...]).astype(BF16)
        kw_o[0, g, :, dh:dh + AUG_WIDTH] = aug_win
        for v_ref, out, chunk in ((s_ref, vsT_o, SEL_CHUNK), (w_ref, vwT_o, WIN_CHUNK)):
            vT = v_ref[:, v_cols].T.astype(BF16)
            ones_first = (lax.broadcasted_iota(jnp.int32, (V_EXTRA, chunk), 0) == 0).astype(BF16)
            for c in range(tt // chunk):
                out[0, g, c, 0:dh, :] = vT[:, c * chunk:(c + 1) * chunk]
                out[0, g, c, dh:dh + V_EXTRA, :] = ones_first


def kv_prep(z, kn_slc, kn_win, *, B, T, col0, tt=512):
    G, dh = NSA_KV_HEADS, HEAD_DIM
    nt = T // tt
    n_blk = T // SEL_BLK
    assert n_blk + 2 * SLOPE_PIECES <= AUG_WIDTH
    cb = col0 // (2 * KV_WIDTH)
    row = lambda b, t: b * nt + t
    kdim = dh + AUG_WIDTH
    return pl.pallas_call(
        functools.partial(_kv_prep_body, tt=tt, n_blk=n_blk),
        grid=(B, nt),
        in_specs=[pl.BlockSpec((tt, 2 * KV_WIDTH), lambda b, t: (row(b, t), cb)),
                  pl.BlockSpec((tt, 2 * KV_WIDTH), lambda b, t: (row(b, t), cb + 1)),
                  pl.BlockSpec((tt, 2 * KV_WIDTH), lambda b, t: (row(b, t), cb + 2)),
                  pl.BlockSpec((1, dh), lambda b, t: (0, 0)),
                  pl.BlockSpec((1, dh), lambda b, t: (0, 0))],
        out_specs=[pl.BlockSpec((1, G, tt // CMP_STRIDE, CMP_STRIDE * dh), lambda b, t: (b, 0, t, 0)),
                   pl.BlockSpec((1, G, tt // CMP_STRIDE, CMP_STRIDE * dh), lambda b, t: (b, 0, t, 0)),
                   pl.BlockSpec((1, G, tt, kdim), lambda b, t: (b, 0, t, 0)),
                   pl.BlockSpec((1, G, tt // SEL_CHUNK, dh + V_EXTRA, SEL_CHUNK), lambda b, t: (b, 0, t, 0, 0)),
                   pl.BlockSpec((1, G, tt, kdim), lambda b, t: (b, 0, t, 0)),
                   pl.BlockSpec((1, G, tt // WIN_CHUNK, dh + V_EXTRA, WIN_CHUNK), lambda b, t: (b, 0, t, 0, 0))],
        out_shape=[jax.ShapeDtypeStruct((B, G, T // CMP_STRIDE, CMP_STRIDE * dh), F32),
                   jax.ShapeDtypeStruct((B, G, T // CMP_STRIDE, CMP_STRIDE * dh), F32),
                   jax.ShapeDtypeStruct((B, G, T, kdim), BF16),
                   jax.ShapeDtypeStruct((B, G, T // SEL_CHUNK, dh + V_EXTRA, SEL_CHUNK), BF16),
                   jax.ShapeDtypeStruct((B, G, T, kdim), BF16),
                   jax.ShapeDtypeStruct((B, G, T // WIN_CHUNK, dh + V_EXTRA, WIN_CHUNK), BF16)],
        scratch_shapes=[pltpu.VMEM((2, tt, dh), F32)],
        compiler_params=_cparams("parallel", "parallel"),
        name="kv_prep",
    )(z, z, z, kn_slc, kn_win)


def _compress_body(xk_ref, xv_ref, pk_ref, pv_ref, w1k_ref, w2k_ref, w1v_ref, w2v_ref, kn_ref,
                   kc_o, vcT_o, *, n_blk):
    dh = HEAD_DIM

    def phi(x_ref, p_ref, w1_ref, w2_ref):
        x = x_ref[0, 0]
        first = jnp.dot((x + p_ref[0:1, :]).astype(BF16), w1_ref[0], preferred_element_type=F32)
        second = jnp.dot((x + p_ref[1:2, :]).astype(BF16), w1_ref[1], preferred_element_type=F32)
        n = second.shape[0]
        h = first + pltpu.roll(second, n - 1, 0)
        return jnp.dot(_silu(h).astype(BF16), w2_ref[...], preferred_element_type=F32)

    k = _rms(phi(xk_ref, pk_ref, w1k_ref, w2k_ref), kn_ref[...])
    nc = k.shape[0]
    shape = (nc, AUG_WIDTH)
    start = (lax.broadcasted_iota(jnp.int32, shape, 0) * CMP_STRIDE).astype(F32)
    centre = jnp.full(shape, (CMP_LEN - 1) / 2.0, F32)
    kc_o[0, 0, :, 0:dh] = k.astype(BF16)
    kc_o[0, 0, :, dh:dh + AUG_WIDTH] = _position_lanes(start, centre, None, n_blk, shape)
    vcT_o[0, 0] = phi(xv_ref, pv_ref, w1v_ref, w2v_ref).T.astype(BF16)


def compress_kv(xk, xv, pos_k, pos_v, w1k, w2k, w1v, w2v, kn_cmp, *, n_blk):
    B, G, nc, half = xk.shape
    dh = half // CMP_STRIDE
    assert CMP_LEN == 2 * CMP_STRIDE
    x_spec = pl.BlockSpec((1, 1, nc, half), lambda b, g: (b, g, 0, 0))
    full = lambda shape: pl.BlockSpec(shape, lambda b, g: (0,) * len(shape))
    return pl.pallas_call(
        functools.partial(_compress_body, n_blk=n_blk),
        grid=(B, G),
        in_specs=[x_spec, x_spec, full((2, half)), full((2, half)),
                  full((2, half, dh)), full((dh, dh)), full((2, half, dh)), full((dh, dh)),
                  full((1, dh))],
        out_specs=[pl.BlockSpec((1, 1, nc, dh + AUG_WIDTH), lambda b, g: (b, g, 0, 0)),
                   pl.BlockSpec((1, 1, dh, nc), lambda b, g: (b, g, 0, 0))],
        out_shape=[jax.ShapeDtypeStruct((B, G, nc, dh + AUG_WIDTH), BF16),
                   jax.ShapeDtypeStruct((B, G, dh, nc), BF16)],
        compiler_params=_cparams("parallel", "parallel"),
        name="compress_kv",
    )(xk, xv, pos_k.reshape(2, half), pos_v.reshape(2, half),
      w1k.reshape(2, half, dh), w2k, w1v.reshape(2, half, dh), w2v, kn_cmp)


def _nsa_body(q_ref, gl_ref, gb_ref, qn_ref, lanes_ref, kc_ref, vcT_ref,
              ks_ref, vsT_ref, kw_ref, vwT_ref, ovT_ref, kmq_ref, o_ref,
              score_ref, gate_ref, qc_ref, qs_ref, m_ref, acc_ref, part_ref, sbuf_ref, *, n_tiles):
    dh, hg, tq = HEAD_DIM, NSA_GROUP, Q_TILE
    cols = hg * tq
    qi = pl.program_id(2)
    t0 = qi * tq
    contract_last = (((1,), (1,)), ((), ()))
    n_blk = ovT_ref.shape[0]

    scale = dh ** -0.5 * LOG2E
    q_heads = [(_rms(q_ref[:, h * dh:(h + 1) * dh], qn_ref[...]) * scale).astype(BF16)
               for h in range(hg)]

    def stack_queries(aug_heads):
        return jnp.concatenate([jnp.concatenate([q_heads[h], aug_heads[h]], axis=1)
                                for h in range(hg)], axis=0)

    hc = kmq_ref.shape[1]
    assert cols % hc == 0 and hc % tq == 0

    def reset(b):
        m_ref[b] = jnp.full((1, cols), NEG_INF, F32)
        acc_ref[b] = jnp.zeros((dh + V_EXTRA, cols), F32)

    def attend(b, queries_ref, parts):
        groups = [slice(i * hc, (i + 1) * hc) for i in range(cols // hc)]
        unit = 0
        for k, _, _, _ in parts:
            for cs in groups:
                sbuf_ref[unit, 0:k.shape[0], :] = lax.dot_general(
                    k, queries_ref[cs, :], contract_last, preferred_element_type=F32)
                unit += 1
        unit = 0
        for (k, v_pieces, mask, bias) in parts:
            for cs in groups:
                s_ref = sbuf_ref.at[unit, 0:k.shape[0], :]
                unit += 1
                if mask is not None:
                    s_ref[...] = jnp.where(mask(), s_ref[...], NEG_INF)
                if bias is not None:
                    s_ref[...] = s_ref[...] + bias
                m_old = m_ref[b, :, cs]
                m_new = jnp.maximum(m_old, jnp.max(s_ref[...], axis=0, keepdims=True))
                alpha = jnp.exp2(m_old - m_new)
                p = jnp.exp2(s_ref[...] - m_new).astype(BF16)
                m_ref[b, :, cs] = m_new
                pv = None
                for i, vT in enumerate(v_pieces):
                    n = vT.shape[1]
                    d = jnp.dot(vT, p[i * n:(i + 1) * n, :], preferred_element_type=F32)
                    pv = d if pv is None else pv + d
                acc_ref[b, :, cs] = alpha * acc_ref[b, :, cs] + pv

    def result(b):
        return acc_ref[b, 0:dh, :] * (1.0 / acc_ref[b, dh:dh + 1, :])

    nc = kc_ref.shape[2]
    qc_ref[...] = stack_queries([jnp.broadcast_to(lanes_ref[0, hg + h:hg + h + 1, :], (tq, AUG_WIDTH)).astype(BF16)
                                 for h in range(hg)])
    s = lax.dot_general(kc_ref[0, 0], qc_ref[...], contract_last, preferred_element_type=F32)
    c_idx = lax.broadcasted_iota(jnp.int32, (nc, cols), 0)
    t_c = t0 + (lax.broadcasted_iota(jnp.int32, (nc, cols), 1) & (tq - 1))
    mask = (c_idx * CMP_STRIDE + (CMP_LEN - 1)) <= t_c
    s = jnp.where(mask, s, NEG_INF)
    m_c = jnp.max(s, axis=0, keepdims=True)
    p_c = jnp.where(mask, jnp.exp2(s - m_c), 0.0)
    l_c = jnp.sum(p_c, axis=0, keepdims=True)
    p_c = p_c * jnp.where(l_c > 0.0, 1.0 / l_c, 0.0)
    o_cmp = jnp.dot(vcT_ref[0, 0], p_c.astype(BF16), preferred_element_type=F32)

    cw = WIN_CHUNK
    assert cw == tq and WINDOW % cw == 0 and cw == kmq_ref.shape[0]
    n_back = WINDOW // cw
    WIN, SEL = 0, 1

    def win_part(c, mask=None, bias=None):
        k = kw_ref[0, 0, pl.ds(pl.multiple_of(c * cw, cw), cw), :]
        return k, [vwT_ref[0, 0, c]], mask, bias

    threshold = jnp.where(qi >= n_back, 0, cw)
    reset(WIN)
    attend(WIN, qc_ref,
           [win_part(qi, mask=lambda: kmq_ref[...] <= 0)]
           + [win_part(jnp.maximum(qi - j, 0), bias=jnp.where(qi >= j, 0.0, NEG_INF))
              for j in range(1, n_back)]
           + [win_part(jnp.maximum(qi - n_back, 0), mask=lambda: kmq_ref[...] > threshold)])

    gate_ref[...] = jax.nn.sigmoid(gl_ref[...] + gb_ref[...]).T

    def gate(h, branch):
        r = h * N_BRANCH + branch
        return gate_ref[r:r + 1, :]

    o_win = result(WIN)
    for h in range(hg):
        cs = slice(h * tq, (h + 1) * tq)
        part_ref[:, cs] = gate(h, 0) * o_cmp[:, cs] + gate(h, 2) * o_win[:, cs]

    p_sum = p_c[:, 0:tq]
    for h in range(1, hg):
        p_sum = p_sum + p_c[:, h * tq:(h + 1) * tq]
    hi = p_sum.astype(BF16)
    rem = p_sum - hi.astype(F32)
    mid = rem.astype(BF16)
    lo = (rem - mid.astype(F32)).astype(BF16)
    ovT = ovT_ref[...]
    imp = (jnp.dot(ovT, hi, preferred_element_type=F32)
           + jnp.dot(ovT, mid, preferred_element_type=F32)
           + jnp.dot(ovT, lo, preferred_element_type=F32))
    blk = lax.broadcasted_iota(jnp.int32, (n_blk, tq), 0)
    t_b = t0 + lax.broadcasted_iota(jnp.int32, (n_blk, tq), 1)
    cur = jnp.right_shift(t_b, SEL_BLK.bit_length() - 1)
    valid = blk * SEL_BLK <= t_b
    forced = (blk == 0) | (blk == cur) | (blk == cur - 1)
    score = jnp.where(valid, imp + FORCE_BONUS * forced.astype(F32), NEG_INF)
    score_ref[...] = score
    rank = jnp.zeros((n_blk, tq), F32)
    for j in range(n_blk):
        sj = score_ref[j:j + 1, :]
        beats = (sj > score) | ((sj == score) & (blk > j))
        rank = rank + beats.astype(F32)
    block_bias = jnp.where(rank < float(min(N_SEL, n_blk)), 0.0, NEG_INF)

    assert 2 * SLOPE_PIECES <= 8
    ones_rows = (lax.broadcasted_iota(jnp.int32, (8, tq), 0) < 2 * SLOPE_PIECES).astype(F32)
    extra_t = jnp.concatenate([block_bias, ones_rows,
                               jnp.zeros((AUG_WIDTH - n_blk - 8, tq), F32)], axis=0).T
    qs_ref[...] = stack_queries([(extra_t * lanes_ref[0, h:h + 1, :]).astype(BF16) for h in range(hg)])

    ck = SEL_CHUNK
    assert ck == tq and ck == kmq_ref.shape[0]

    def sel_part(c, n=1, mask=None):
        k = ks_ref[0, 0, c * ck:(c + n) * ck, :]
        return k, [vsT_ref[0, 0, c + i] for i in range(n)], mask, None

    reset(SEL)

    for tile in range(n_tiles):
        @pl.when(qi == tile)
        def _(tile=tile):
            parts = [sel_part(tile, mask=lambda: kmq_ref[...] <= 0)]
            parts += [sel_part(c, n=min(2, tile - c)) for c in range(0, tile, 2)]
            attend(SEL, qs_ref, parts)

    o_slc = result(SEL)
    for h in range(hg):
        cs = slice(h * tq, (h + 1) * tq)
        o_h = part_ref[:, cs] + gate(h, 1) * o_slc[:, cs]
        o_ref[:, h * dh:(h + 1) * dh] = o_h.T.astype(o_ref.dtype)


def nsa_attention(z, gl, gate_b, q_norm, kcmp, vcmpT, ks, vsT, kw, vwT, *, B, T):
    G, hg, dh, tq = NSA_KV_HEADS, NSA_GROUP, HEAD_DIM, Q_TILE
    nq = T // tq
    nc = kcmp.shape[2]
    n_blk = T // SEL_BLK
    kdim = dh + AUG_WIDTH
    cols = hg * tq
    assert SEL_CHUNK == WIN_CHUNK == tq
    n_units = max(WINDOW // WIN_CHUNK + 1, 1 + nq // 2) * (cols // NSA_COLS_PER_DOT)
    assert SEL_CHUNK % SEL_BLK == 0 and T % SEL_CHUNK == 0 and gl.shape[1] == G * LANES
    assert SEL_CHUNK % tq == 0 and n_blk % 8 == 0 and n_blk + 8 <= AUG_WIDTH

    slopes = np.exp2(-8.0 * (np.arange(NSA_HEADS, dtype=np.float64) + 1.0) / NSA_HEADS).reshape(G, hg)
    lanes = np.zeros((G, 2 * hg, AUG_WIDTH), np.float32)
    lanes[:, :hg, :n_blk] = 1.0
    for i, piece in enumerate(_bf16_pieces(slopes * LOG2E, SLOPE_PIECES)):
        for lane in (n_blk + i, n_blk + SLOPE_PIECES + i):
            lanes[:, :hg, lane] = piece
            lanes[:, hg:, lane] = piece
    cs = np.arange(nc)[:, None] * CMP_STRIDE
    bs = np.arange(n_blk)[None, :] * SEL_BLK
    overlap = np.clip(np.minimum(cs + CMP_LEN, bs + SEL_BLK) - np.maximum(cs, bs), 0, None) / CMP_LEN
    overlap[(T - CMP_LEN) // CMP_STRIDE + 1:] = 0.0
    ovT = jnp.asarray(overlap.T, dtype=BF16)
    hc = NSA_COLS_PER_DOT
    kmq = jnp.asarray(np.arange(SEL_CHUNK)[:, None] - (np.arange(hc)[None, :] % tq), dtype=jnp.int32)

    row = lambda b, qi: b * nq + qi
    per_bg = lambda shape: pl.BlockSpec((1, 1) + shape, lambda b, g, qi: (b, g) + (0,) * len(shape))
    const = lambda shape: pl.BlockSpec(shape, lambda b, g, qi: (0,) * len(shape))
    return pl.pallas_call(
        functools.partial(_nsa_body, n_tiles=nq),
        grid=(B, G, nq),
        in_specs=[pl.BlockSpec((tq, hg * dh), lambda b, g, qi: (row(b, qi), g)),
                  pl.BlockSpec((tq, LANES), lambda b, g, qi: (row(b, qi), g)),
                  pl.BlockSpec((1, LANES), lambda b, g, qi: (0, g)),
                  const((1, dh)),
                  pl.BlockSpec((1, 2 * hg, AUG_WIDTH), lambda b, g, qi: (g, 0, 0)),
                  per_bg((nc, kdim)), per_bg((dh, nc)),
                  per_bg((T, kdim)), per_bg((T // SEL_CHUNK, dh + V_EXTRA, SEL_CHUNK)),
                  per_bg((T, kdim)), per_bg((T // WIN_CHUNK, dh + V_EXTRA, WIN_CHUNK)),
                  const((n_blk, nc)), const((SEL_CHUNK, hc))],
        out_specs=pl.BlockSpec((tq, hg * dh), lambda b, g, qi: (row(b, qi), g)),
        out_shape=jax.ShapeDtypeStruct((B * T, NSA_WIDTH), BF16),
        scratch_shapes=[pltpu.VMEM((n_blk, tq), F32), pltpu.VMEM((LANES, tq), F32),
                        pltpu.VMEM((cols, kdim), BF16), pltpu.VMEM((cols, kdim), BF16),
                        pltpu.VMEM((2, 1, cols), F32),
                        pltpu.VMEM((2, dh + V_EXTRA, cols), F32), pltpu.VMEM((dh, cols), F32),
                        pltpu.VMEM((n_units, 2 * SEL_CHUNK, NSA_COLS_PER_DOT), F32)],
        compiler_params=_cparams("parallel", "parallel", "arbitrary"),
        name="nsa_attention",
    )(z, gl, gate_b, q_norm, jnp.asarray(lanes), kcmp, vcmpT, ks, vsT, kw, vwT, ovT, kmq)


def _conv_body(a_ref, b_ref, ah_ref, bh_ref, w_ref, cb_ref, lg_ref, lb_ref, o_ref,
               h_ref, g_ref, y_ref, *, tt):
    C = a_ref.shape[1]
    halo = ah_ref[...] * jax.nn.sigmoid(bh_ref[...])
    h_ref[0:CONV_HALO, :] = jnp.where(pl.program_id(1) == 0, 0.0, halo)
    h_ref[CONV_HALO:CONV_HALO + tt, :] = a_ref[...] * jax.nn.sigmoid(b_ref[...])
    h_ref[CONV_HALO + tt:CONV_HALO + tt + SUBLANES, :] = jnp.zeros((SUBLANES, C), F32)
    lead = CONV_HALO - (CONV_KERNEL - 1)
    rows = tt + SUBLANES
    for s in range(C // LANES):
        cs = slice(s * LANES, (s + 1) * LANES)
        acc = None
        for r in range(SUBLANES):
            group = None
            for o in range(lead, lead + CONV_KERNEL):
                if o % SUBLANES == r:
                    term = w_ref[o - lead:o - lead + 1, cs] * h_ref[o - r:o - r + rows, cs]
                    group = term if group is None else group + term
            if r == 0:
                piece = group[0:tt, :]
            else:
                g_ref[r] = group
                piece = g_ref[r, r:r + tt, :]
            acc = piece if acc is None else acc + piece
        y_ref[:, cs] = acc + cb_ref[:, cs]
    y = y_ref[...]
    mu = jnp.mean(y, axis=-1, keepdims=True)
    var = jnp.mean(jnp.square(y - mu), axis=-1, keepdims=True)
    yn = (y - mu) * lax.rsqrt(var + LN_EPS) * lg_ref[...] + lb_ref[...]
    o_ref[...] = _silu(yn).astype(o_ref.dtype)


def conformer_conv(z, conv_w, conv_b, ln_g, ln_b, *, B, T, col0, C, tt=128):
    nt = T // tt
    ca = col0 // C
    hb = tt // CONV_HALO
    row = lambda b, t: b * nt + t
    halo_row = lambda b, t: jnp.maximum(row(b, t) * hb - 1, 0)
    const = lambda shape: pl.BlockSpec(shape, lambda b, t: (0,) * len(shape))
    return pl.pallas_call(
        functools.partial(_conv_body, tt=tt),
        grid=(B, nt),
        in_specs=[pl.BlockSpec((tt, C), lambda b, t: (row(b, t), ca)),
                  pl.BlockSpec((tt, C), lambda b, t: (row(b, t), ca + 1)),
                  pl.BlockSpec((CONV_HALO, C), lambda b, t: (halo_row(b, t), ca)),
                  pl.BlockSpec((CONV_HALO, C), lambda b, t: (halo_row(b, t), ca + 1)),
                  const((CONV_KERNEL, C)), const((1, C)), const((1, C)), const((1, C))],
        out_specs=pl.BlockSpec((tt, C), lambda b, t: (row(b, t), 0)),
        out_shape=jax.ShapeDtypeStruct((B * T, C), BF16),
        scratch_shapes=[pltpu.VMEM((CONV_HALO + tt + SUBLANES, C), F32),
                        pltpu.VMEM((SUBLANES, tt + SUBLANES, LANES), F32),
                        pltpu.VMEM((tt, C), F32)],
        compiler_params=_cparams("parallel", "parallel"),
        name="conformer_conv",
    )(z, z, z, z, conv_w, conv_b, ln_g, ln_b)


def _mem_attn_body(x_ref, gq_ref, wq_ref, kv_ref, qn_ref, kn_ref, wo_ref, gn_ref, o_ref, on_ref):
    dh = HEAD_DIM
    width = MEM_HEADS * dh
    scale = dh ** -0.5
    q_all = jnp.dot(_rms(x_ref[...], gq_ref[...]).astype(BF16), wq_ref[...], preferred_element_type=F32)
    heads = [slice(h * dh, (h + 1) * dh) for h in range(MEM_HEADS)]
    scores = []
    for cs in heads:
        q = (_rms(q_all[:, cs], qn_ref[...]) * scale).astype(BF16)
        k = _rms(kv_ref[:, cs], kn_ref[...]).astype(BF16)
        scores.append(lax.dot_general(q, k, (((1,), (1,)), ((), ())), preferred_element_type=F32))
    outs = []
    for h, s in enumerate(scores):
        v = kv_ref[:, width + h * dh:width + (h + 1) * dh].astype(BF16)
        p = jnp.exp(s - jnp.max(s, axis=-1, keepdims=True))
        inv = 1.0 / jnp.sum(p, axis=-1, keepdims=True)
        outs.append((jnp.dot(p.astype(BF16), v, preferred_element_type=F32) * inv).astype(BF16))
    o = jnp.concatenate(outs, axis=1)
    y = x_ref[...] + jnp.dot(o, wo_ref[...], preferred_element_type=F32)
    o_ref[...] = y
    on_ref[...] = _rms(y, gn_ref[...]).astype(on_ref.dtype)


def mem_attention(x, q_gain, w_mq, kvm, mq_norm, mk_norm, w_mo, next_norm, *, B, T, M, tt=256):
    D = x.shape[1]
    width = MEM_HEADS * HEAD_DIM
    nt = T // tt
    row = lambda b, t: b * nt + t
    const = lambda shape: pl.BlockSpec(shape, lambda b, t: (0,) * len(shape))
    return pl.pallas_call(
        _mem_attn_body,
        grid=(B, nt),
        in_specs=[pl.BlockSpec((tt, D), lambda b, t: (row(b, t), 0)),
                  const((1, D)), const((D, width)),
                  pl.BlockSpec((M, 2 * width), lambda b, t: (b, 0)),
                  const((1, HEAD_DIM)), const((1, HEAD_DIM)), const((width, D)),
                  const((1, D))],
        out_specs=[pl.BlockSpec((tt, D), lambda b, t: (row(b, t), 0)),
                   pl.BlockSpec((tt, D), lambda b, t: (row(b, t), 0))],
        out_shape=[jax.ShapeDtypeStruct((B * T, D), F32),
                   jax.ShapeDtypeStruct((B * T, D), BF16)],
        compiler_params=_cparams("parallel", "parallel"),
        name="mem_attention",
    )(x, q_gain, w_mq, kvm, mq_norm, mk_norm, w_mo, next_norm)


def kernel(x, mem, norm_mix, w_in, gate_b, q_norm, k_norm_cmp, k_norm_slc, k_norm_win, cmp_pos_k, cmp_pos_v, cmp_k_w1, cmp_k_w2, cmp_v_w1, cmp_v_w2, conv_w, conv_b, conv_ln_g, conv_ln_b, w_out, norm_mem_q, norm_mem_kv, w_mq, w_mk, w_mv, mq_norm, mk_norm, w_mo, norm_ffn, w_gate, w_up, w_down):
    B, T, D = x.shape
    M = mem.shape[1]
    depth = w_in.shape[0]
    G, hg = NSA_KV_HEADS, NSA_GROUP
    C = D - NSA_WIDTH
    n_gate = N_BRANCH * NSA_HEADS
    kv_all = 6 * KV_WIDTH
    assert w_in.shape[2] == NSA_WIDTH + kv_all + n_gate + 2 * C
    assert C == NSA_WIDTH, "column blocking below assumes equal NSA and conv widths"

    row1 = lambda v: v.reshape(1, -1)
    xf = x.reshape(B * T, D)
    memf = mem.reshape(B * M, D)
    u0 = NSA_WIDTH + kv_all + n_gate
    for l in range(depth):
        w = w_in[l]
        w_t = jnp.swapaxes(w, 0, 1)
        wg = w[:, NSA_WIDTH + kv_all:u0].reshape(D, G, hg * N_BRANCH)
        wg = jnp.pad(wg, ((0, 0), (0, 0), (0, LANES - hg * N_BRANCH))).reshape(D, G * LANES)
        gb = jnp.pad(gate_b[l].reshape(G, hg * N_BRANCH), ((0, 0), (0, LANES - hg * N_BRANCH))).reshape(1, G * LANES)

        z, gl = in_projection(rmsnorm_bf16(xf, row1(norm_mix[l]), tm=512), w_t, wg, tm=2048, tn=512,
                              segments=((0, NSA_WIDTH), (u0, 2 * C), (NSA_WIDTH, kv_all)))
        kv_col0 = NSA_WIDTH + 2 * C
        kc, vc, ks, vsT, kw, vwT = kv_prep(z, row1(k_norm_slc[l]), row1(k_norm_win[l]), B=B, T=T, col0=kv_col0)
        kcmp, vcmpT = compress_kv(kc, vc, cmp_pos_k[l], cmp_pos_v[l],
                                  cmp_k_w1[l].astype(BF16), cmp_k_w2[l].astype(BF16),
                                  cmp_v_w1[l].astype(BF16), cmp_v_w2[l].astype(BF16), row1(k_norm_cmp[l]),
                                  n_blk=T // SEL_BLK)
        o_nsa = nsa_attention(z, gl, gb, row1(q_norm[l]), kcmp, vcmpT, ks, vsT, kw, vwT, B=B, T=T)
        o_conv = conformer_conv(z, conv_w[l], row1(conv_b[l]), row1(conv_ln_g[l]), row1(conv_ln_b[l]),
                                B=B, T=T, col0=NSA_WIDTH, C=C)
        xf = concat_matmul_residual(o_nsa, o_conv, w_out[l], xf, tm=2048, tn=512)

        w_mkv = jnp.concatenate([w_mk[l], w_mv[l]], axis=1).astype(BF16)
        kvm = norm_matmul(memf, row1(norm_mem_kv[l]), w_mkv, tm=B * M, tn=512)
        xf, xn = mem_attention(xf, row1(norm_mem_q[l]), w_mq[l].astype(BF16), kvm,
                               row1(mq_norm[l]), row1(mk_norm[l]), w_mo[l].astype(BF16),
                               row1(norm_ffn[l]), B=B, T=T, M=M, tt=512)

        hidden, w_down_bf16 = swiglu_up(xn, w_gate[l], w_up[l], w_down[l], tm=1024, tn=512)
        xf = matmul_residual(hidden, w_down_bf16, xf, tm=1024, tn=512)
    return xf.reshape(B, T, D)
```

```python
import functools

import ml_dtypes
import numpy as np
import jax
import jax.numpy as jnp
from jax import lax
from jax.experimental import pallas as pl
from jax.experimental.pallas import tpu as pltpu

F32 = jnp.float32
BF16 = jnp.bfloat16

HEAD_DIM = 128
NSA_HEADS = 8
NSA_KV_HEADS = 2
NSA_GROUP = NSA_HEADS // NSA_KV_HEADS
NSA_WIDTH = NSA_HEADS * HEAD_DIM
KV_WIDTH = NSA_KV_HEADS * HEAD_DIM
N_BRANCH = 3
CONV_KERNEL = 31
CMP_LEN = 32
CMP_STRIDE = 16
SEL_BLK = 64
N_SEL = 16
WINDOW = 512
MEM_HEADS = 4
NEG_INF = -1e30
FORCE_BONUS = 1e3
RMS_EPS = 1e-6
LN_EPS = 1e-5

LANES = 128
SUBLANES = 8
VMEM_LIMIT_BYTES = 56 * 1024 * 1024

Q_TILE = 256
SEL_CHUNK = 256
WIN_CHUNK = 256
NSA_COLS_PER_DOT = 512
CONV_HALO = 32

TILE_RMSNORM = 512
TILE_IN_PROJ = (2048, 512)
TILE_KV_PREP = 512
TILE_CONV = 256
TILE_OUT_PROJ = (2048, 512)
TILE_MEM_KV_COLS = 512
TILE_MEM_ATTN = 512
TILE_FFN_UP = (1024, 512)
TILE_FFN_DOWN = (1024, 512)


def _cparams(*sem):
    return pltpu.CompilerParams(dimension_semantics=sem, vmem_limit_bytes=VMEM_LIMIT_BYTES)


def _rms(x, g, eps=RMS_EPS):
    ms = jnp.mean(x * x, axis=-1, keepdims=True)
    return x * lax.rsqrt(ms + eps) * g


def _silu(x):
    return x * jax.nn.sigmoid(x)


def _norm_to_scratch(x_ref, g_ref, xn_ref):
    @pl.when(pl.program_id(1) == 0)
    def _():
        xn_ref[...] = _rms(x_ref[...], g_ref[...]).astype(xn_ref.dtype)


def _norm_mm_body(x_ref, g_ref, w_ref, o_ref, xn_ref):
    _norm_to_scratch(x_ref, g_ref, xn_ref)
    o_ref[...] = jnp.dot(xn_ref[...], w_ref[...], preferred_element_type=F32).astype(o_ref.dtype)


def _rmsnorm_body(x_ref, g_ref, o_ref):
    o_ref[...] = _rms(x_ref[...], g_ref[...]).astype(o_ref.dtype)


def rmsnorm_bf16(x, g, *, tm):
    M, K = x.shape
    return pl.pallas_call(
        _rmsnorm_body,
        grid=(M // tm,),
        in_specs=[pl.BlockSpec((tm, K), lambda i: (i, 0)), pl.BlockSpec((1, K), lambda i: (0, 0))],
        out_specs=pl.BlockSpec((tm, K), lambda i: (i, 0)),
        out_shape=jax.ShapeDtypeStruct((M, K), BF16),
        compiler_params=_cparams("parallel"),
        name="rmsnorm",
    )(x, g)


def _in_proj_body(xn_ref, wt_ref, wg_ref, o_ref, og_ref):
    @pl.when(pl.program_id(1) == 0)
    def _():
        og_ref[...] = jnp.dot(xn_ref[...], wg_ref[...].astype(BF16), preferred_element_type=F32)

    o_ref[...] = lax.dot_general(xn_ref[...], wt_ref[...].astype(BF16), (((1,), (1,)), ((), ())),
                                 preferred_element_type=F32)


def _swiglu_body(x_ref, wg_ref, wu_ref, wd_ref, o_ref, wd_o):
    x = x_ref[...]
    a = jnp.dot(x, wg_ref[...].astype(BF16), preferred_element_type=F32)
    b = jnp.dot(x, wu_ref[...].astype(BF16), preferred_element_type=F32)
    o_ref[...] = (_silu(a) * b).astype(o_ref.dtype)

    @pl.when(pl.program_id(0) == 0)
    def _():
        wd_o[...] = wd_ref[...].astype(wd_o.dtype)


def norm_matmul(x, g, w, *, tm, tn, out_dtype=F32):
    M, K = x.shape
    N = w.shape[1]
    return pl.pallas_call(
        _norm_mm_body,
        grid=(M // tm, N // tn),
        in_specs=[pl.BlockSpec((tm, K), lambda i, j: (i, 0)),
                  pl.BlockSpec((1, K), lambda i, j: (0, 0)),
                  pl.BlockSpec((K, tn), lambda i, j: (0, j))],
        out_specs=pl.BlockSpec((tm, tn), lambda i, j: (i, j)),
        out_shape=jax.ShapeDtypeStruct((M, N), out_dtype),
        scratch_shapes=[pltpu.VMEM((tm, K), BF16)],
        compiler_params=_cparams("parallel", "arbitrary"),
        name="norm_matmul",
    )(x, g, w)


def in_projection(xn, w_t, wg, *, segments, tm, tn):
    M, K = xn.shape
    NG = wg.shape[1]
    assert all(start % SUBLANES == 0 and size % tn == 0 for start, size in segments)
    N = sum(size for _, size in segments)

    def row_start(i, j):
        group, first = 0, 0
        for seg_start, size in segments:
            group = jnp.where(j >= first, seg_start // SUBLANES + (j - first) * (tn // SUBLANES), group)
            first += size // tn
        return (group * SUBLANES, 0)

    return pl.pallas_call(
        _in_proj_body,
        grid=(M // tm, N // tn),
        in_specs=[pl.BlockSpec((tm, K), lambda i, j: (i, 0)),
                  pl.BlockSpec((pl.Element(tn), pl.Element(K)), row_start),
                  pl.BlockSpec((K, NG), lambda i, j: (0, 0))],
        out_specs=[pl.BlockSpec((tm, tn), lambda i, j: (i, j)),
                   pl.BlockSpec((tm, NG), lambda i, j: (i, 0))],
        out_shape=[jax.ShapeDtypeStruct((M, N), F32),
                   jax.ShapeDtypeStruct((M, NG), F32)],
        compiler_params=_cparams("parallel", "arbitrary"),
        name="in_proj",
    )(xn, w_t, wg)


def swiglu_up(xn, w_gate, w_up, w_down, *, tm, tn):
    M, K = xn.shape
    N = w_gate.shape[1]
    D = w_down.shape[1]
    nj = N // tn
    assert w_down.shape[0] == N
    down_block = lambda i, j: (jnp.where(i == 0, j, nj - 1), 0)
    return pl.pallas_call(
        _swiglu_body,
        grid=(M // tm, nj),
        in_specs=[pl.BlockSpec((tm, K), lambda i, j: (i, 0)),
                  pl.BlockSpec((K, tn), lambda i, j: (0, j)),
                  pl.BlockSpec((K, tn), lambda i, j: (0, j)),
                  pl.BlockSpec((tn, D), down_block)],
        out_specs=[pl.BlockSpec((tm, tn), lambda i, j: (i, j)),
                   pl.BlockSpec((tn, D), down_block)],
        out_shape=[jax.ShapeDtypeStruct((M, N), BF16),
                   jax.ShapeDtypeStruct((N, D), BF16)],
        compiler_params=_cparams("arbitrary", "arbitrary"),
        name="ffn_up",
    )(xn, w_gate, w_up, w_down)


def _mm_res_body(a_ref, w_ref, r_ref, o_ref):
    o_ref[...] = r_ref[...] + jnp.dot(a_ref[...], w_ref[...], preferred_element_type=F32)


def matmul_residual(a, w, res, *, tm, tn):
    M, K = a.shape
    N = w.shape[1]
    return pl.pallas_call(
        _mm_res_body,
        grid=(M // tm, N // tn),
        in_specs=[pl.BlockSpec((tm, K), lambda i, j: (i, 0)),
                  pl.BlockSpec((K, tn), lambda i, j: (0, j)),
                  pl.BlockSpec((tm, tn), lambda i, j: (i, j))],
        out_specs=pl.BlockSpec((tm, tn), lambda i, j: (i, j)),
        out_shape=jax.ShapeDtypeStruct((M, N), F32),
        compiler_params=_cparams("parallel", "arbitrary"),
        name="matmul_residual",
    )(a, w, res)


def _mm2_res_body(a_ref, b_ref, wa_ref, wb_ref, r_ref, o_ref):
    acc = jnp.dot(a_ref[...], wa_ref[...].astype(BF16), preferred_element_type=F32)
    acc = acc + jnp.dot(b_ref[...], wb_ref[...].astype(BF16), preferred_element_type=F32)
    o_ref[...] = r_ref[...] + acc


def concat_matmul_residual(a, b, w, res, *, tm, tn):
    M, Ka = a.shape
    Kb = b.shape[1]
    assert Ka == Kb and w.shape[0] == Ka + Kb
    N = w.shape[1]
    return pl.pallas_call(
        _mm2_res_body,
        grid=(M // tm, N // tn),
        in_specs=[pl.BlockSpec((tm, Ka), lambda i, j: (i, 0)),
                  pl.BlockSpec((tm, Kb), lambda i, j: (i, 0)),
                  pl.BlockSpec((Ka, tn), lambda i, j: (0, j)),
                  pl.BlockSpec((Kb, tn), lambda i, j: (1, j)),
                  pl.BlockSpec((tm, tn), lambda i, j: (i, j))],
        out_specs=pl.BlockSpec((tm, tn), lambda i, j: (i, j)),
        out_shape=jax.ShapeDtypeStruct((M, N), F32),
        compiler_params=_cparams("parallel", "arbitrary"),
        name="out_proj",
    )(a, b, w, w, res)


AUG_WIDTH = LANES
V_EXTRA = 16
SLOPE_PIECES = 4
LOG2E = float(np.log2(np.e))


def _bf16_pieces(x, n):
    out, rem = [], np.asarray(x, np.float64)
    for _ in range(n):
        piece = rem.astype(ml_dtypes.bfloat16).astype(np.float64)
        out.append(piece.astype(np.float32))
        rem = rem - piece
    return out


def _position_lanes(pos_hi, pos_lo, onehot, n_blk, shape):
    lane = lax.broadcasted_iota(jnp.int32, shape, 1)
    p = SLOPE_PIECES
    extra = jnp.where((lane >= n_blk) & (lane < n_blk + p), pos_hi,
                      jnp.where((lane >= n_blk + p) & (lane < n_blk + 2 * p), pos_lo, 0.0))
    if onehot is not None:
        extra = jnp.where(lane < n_blk, onehot, extra)
    return extra.astype(BF16)


def _kv_prep_body(c_ref, s_ref, w_ref, kns_ref, knw_ref,
                  kc_o, vc_o, ks_o, vsT_o, kw_o, vwT_o, slab_ref, *, tt, n_blk):
    dh = HEAD_DIM
    shape = (tt, AUG_WIDTH)
    pos = pl.program_id(1) * tt + lax.broadcasted_iota(jnp.int32, shape, 0)
    lane = lax.broadcasted_iota(jnp.int32, shape, 1)
    blk = jnp.right_shift(pos, SEL_BLK.bit_length() - 1)
    pos_hi = (blk * SEL_BLK).astype(F32)
    pos_lo = (pos & (SEL_BLK - 1)).astype(F32)
    aug_sel = _position_lanes(pos_hi, pos_lo, (blk == lane).astype(F32), n_blk, shape)
    aug_win = _position_lanes(pos_hi, pos_lo, None, n_blk, shape)
    for g in range(NSA_KV_HEADS):
        k_cols = slice(g * dh, (g + 1) * dh)
        v_cols = slice(KV_WIDTH + g * dh, KV_WIDTH + (g + 1) * dh)
        for slab, (cols, out) in enumerate(((k_cols, kc_o), (v_cols, vc_o))):
            slab_ref[slab] = c_ref[:, cols]
            for i in range(CMP_STRIDE):
                rows = pl.ds(i, tt // CMP_STRIDE, stride=CMP_STRIDE)
                out[0, g, :, i * dh:(i + 1) * dh] = slab_ref[slab, rows, :]
        ks_o[0, g, :, 0:dh] = _rms(s_ref[:, k_cols], kns_ref[...]).astype(BF16)
        ks_o[0, g, :, dh:dh + AUG_WIDTH] = aug_sel
        kw_o[0, g, :, 0:dh] = _rms(w_ref[:, k_cols], knw_ref[...]).astype(BF16)
        kw_o[0, g, :, dh:dh + AUG_WIDTH] = aug_win
        for v_ref, out, chunk in ((s_ref, vsT_o, SEL_CHUNK), (w_ref, vwT_o, WIN_CHUNK)):
            vT = v_ref[:, v_cols].T.astype(BF16)
            ones_first = (lax.broadcasted_iota(jnp.int32, (V_EXTRA, chunk), 0) == 0).astype(BF16)
            for c in range(tt // chunk):
                out[0, g, c, 0:dh, :] = vT[:, c * chunk:(c + 1) * chunk]
                out[0, g, c, dh:dh + V_EXTRA, :] = ones_first


def kv_prep(z, kn_slc, kn_win, *, B, T, col0, tt):
    G, dh = NSA_KV_HEADS, HEAD_DIM
    nt = T // tt
    n_blk = T // SEL_BLK
    assert n_blk + 2 * SLOPE_PIECES <= AUG_WIDTH
    cb = col0 // (2 * KV_WIDTH)
    row = lambda b, t: b * nt + t
    kdim = dh + AUG_WIDTH
    return pl.pallas_call(
        functools.partial(_kv_prep_body, tt=tt, n_blk=n_blk),
        grid=(B, nt),
        in_specs=[pl.BlockSpec((tt, 2 * KV_WIDTH), lambda b, t: (row(b, t), cb)),
                  pl.BlockSpec((tt, 2 * KV_WIDTH), lambda b, t: (row(b, t), cb + 1)),
                  pl.BlockSpec((tt, 2 * KV_WIDTH), lambda b, t: (row(b, t), cb + 2)),
                  pl.BlockSpec((1, dh), lambda b, t: (0, 0)),
                  pl.BlockSpec((1, dh), lambda b, t: (0, 0))],
        out_specs=[pl.BlockSpec((1, G, tt // CMP_STRIDE, CMP_STRIDE * dh), lambda b, t: (b, 0, t, 0)),
                   pl.BlockSpec((1, G, tt // CMP_STRIDE, CMP_STRIDE * dh), lambda b, t: (b, 0, t, 0)),
                   pl.BlockSpec((1, G, tt, kdim), lambda b, t: (b, 0, t, 0)),
                   pl.BlockSpec((1, G, tt // SEL_CHUNK, dh + V_EXTRA, SEL_CHUNK), lambda b, t: (b, 0, t, 0, 0)),
                   pl.BlockSpec((1, G, tt, kdim), lambda b, t: (b, 0, t, 0)),
                   pl.BlockSpec((1, G, tt // WIN_CHUNK, dh + V_EXTRA, WIN_CHUNK), lambda b, t: (b, 0, t, 0, 0))],
        out_shape=[jax.ShapeDtypeStruct((B, G, T // CMP_STRIDE, CMP_STRIDE * dh), F32),
                   jax.ShapeDtypeStruct((B, G, T // CMP_STRIDE, CMP_STRIDE * dh), F32),
                   jax.ShapeDtypeStruct((B, G, T, kdim), BF16),
                   jax.ShapeDtypeStruct((B, G, T // SEL_CHUNK, dh + V_EXTRA, SEL_CHUNK), BF16),
                   jax.ShapeDtypeStruct((B, G, T, kdim), BF16),
                   jax.ShapeDtypeStruct((B, G, T // WIN_CHUNK, dh + V_EXTRA, WIN_CHUNK), BF16)],
        scratch_shapes=[pltpu.VMEM((2, tt, dh), F32)],
        compiler_params=_cparams("parallel", "parallel"),
        name="kv_prep",
    )(z, z, z, kn_slc, kn_win)


def _compress_body(xk_ref, xv_ref, pk_ref, pv_ref, w1k_ref, w2k_ref, w1v_ref, w2v_ref, kn_ref,
                   kc_o, vcT_o, *, n_blk):
    dh = HEAD_DIM

    def phi(x_ref, p_ref, w1_ref, w2_ref):
        x = x_ref[0, 0]
        first = jnp.dot((x + p_ref[0:1, :]).astype(BF16), w1_ref[0], preferred_element_type=F32)
        second = jnp.dot((x + p_ref[1:2, :]).astype(BF16), w1_ref[1], preferred_element_type=F32)
        n = second.shape[0]
        h = first + pltpu.roll(second, n - 1, 0)
        return jnp.dot(_silu(h).astype(BF16), w2_ref[...], preferred_element_type=F32)

    k = _rms(phi(xk_ref, pk_ref, w1k_ref, w2k_ref), kn_ref[...])
    nc = k.shape[0]
    shape = (nc, AUG_WIDTH)
    start = (lax.broadcasted_iota(jnp.int32, shape, 0) * CMP_STRIDE).astype(F32)
    centre = jnp.full(shape, (CMP_LEN - 1) / 2.0, F32)
    kc_o[0, 0, :, 0:dh] = k.astype(BF16)
    kc_o[0, 0, :, dh:dh + AUG_WIDTH] = _position_lanes(start, centre, None, n_blk, shape)
    vcT_o[0, 0] = phi(xv_ref, pv_ref, w1v_ref, w2v_ref).T.astype(BF16)


def compress_kv(xk, xv, pos_k, pos_v, w1k, w2k, w1v, w2v, kn_cmp, *, n_blk):
    B, G, nc, half = xk.shape
    dh = half // CMP_STRIDE
    assert CMP_LEN == 2 * CMP_STRIDE
    x_spec = pl.BlockSpec((1, 1, nc, half), lambda b, g: (b, g, 0, 0))
    full = lambda shape: pl.BlockSpec(shape, lambda b, g: (0,) * len(shape))
    return pl.pallas_call(
        functools.partial(_compress_body, n_blk=n_blk),
        grid=(B, G),
        in_specs=[x_spec, x_spec, full((2, half)), full((2, half)),
                  full((2, half, dh)), full((dh, dh)), full((2, half, dh)), full((dh, dh)),
                  full((1, dh))],
        out_specs=[pl.BlockSpec((1, 1, nc, dh + AUG_WIDTH), lambda b, g: (b, g, 0, 0)),
                   pl.BlockSpec((1, 1, dh, nc), lambda b, g: (b, g, 0, 0))],
        out_shape=[jax.ShapeDtypeStruct((B, G, nc, dh + AUG_WIDTH), BF16),
                   jax.ShapeDtypeStruct((B, G, dh, nc), BF16)],
        compiler_params=_cparams("parallel", "parallel"),
        name="compress_kv",
    )(xk, xv, pos_k.reshape(2, half), pos_v.reshape(2, half),
      w1k.reshape(2, half, dh), w2k, w1v.reshape(2, half, dh), w2v, kn_cmp)


def _nsa_body(q_ref, gl_ref, gb_ref, qn_ref, lanes_ref, kc_ref, vcT_ref,
              ks_ref, vsT_ref, kw_ref, vwT_ref, ovT_ref, kmq_ref, o_ref,
              score_ref, gate_ref, qc_ref, qs_ref, m_ref, acc_ref, part_ref, sbuf_ref, *, n_tiles):
    dh, hg, tq = HEAD_DIM, NSA_GROUP, Q_TILE
    cols = hg * tq
    qi = pl.program_id(2)
    t0 = qi * tq
    contract_last = (((1,), (1,)), ((), ()))
    n_blk = ovT_ref.shape[0]

    scale = dh ** -0.5 * LOG2E
    q_heads = [(_rms(q_ref[:, h * dh:(h + 1) * dh], qn_ref[...]) * scale).astype(BF16)
               for h in range(hg)]

    def stack_queries(aug_heads):
        return jnp.concatenate([jnp.concatenate([q_heads[h], aug_heads[h]], axis=1)
                                for h in range(hg)], axis=0)

    hc = kmq_ref.shape[1]
    assert cols % hc == 0 and hc % tq == 0

    def reset(b):
        m_ref[b] = jnp.full((1, cols), NEG_INF, F32)
        acc_ref[b] = jnp.zeros((dh + V_EXTRA, cols), F32)

    def attend(b, queries_ref, parts):
        groups = [slice(i * hc, (i + 1) * hc) for i in range(cols // hc)]
        unit = 0
        for k, _, _, _ in parts:
            for cs in groups:
                sbuf_ref[unit, 0:k.shape[0], :] = lax.dot_general(
                    k, queries_ref[cs, :], contract_last, preferred_element_type=F32)
                unit += 1
        unit = 0
        for (k, v_pieces, mask, bias) in parts:
            for cs in groups:
                s_ref = sbuf_ref.at[unit, 0:k.shape[0], :]
                unit += 1
                if mask is not None:
                    s_ref[...] = jnp.where(mask(), s_ref[...], NEG_INF)
                if bias is not None:
                    s_ref[...] = s_ref[...] + bias
                m_old = m_ref[b, :, cs]
                m_new = jnp.maximum(m_old, jnp.max(s_ref[...], axis=0, keepdims=True))
                alpha = jnp.exp2(m_old - m_new)
                p = jnp.exp2(s_ref[...] - m_new).astype(BF16)
                m_ref[b, :, cs] = m_new
                pv = None
                for i, vT in enumerate(v_pieces):
                    n = vT.shape[1]
                    d = jnp.dot(vT, p[i * n:(i + 1) * n, :], preferred_element_type=F32)
                    pv = d if pv is None else pv + d
                acc_ref[b, :, cs] = alpha * acc_ref[b, :, cs] + pv

    def result(b):
        return acc_ref[b, 0:dh, :] * (1.0 / acc_ref[b, dh:dh + 1, :])

    nc = kc_ref.shape[2]
    qc_ref[...] = stack_queries([jnp.broadcast_to(lanes_ref[0, hg + h:hg + h + 1, :], (tq, AUG_WIDTH)).astype(BF16)
                                 for h in range(hg)])
    s = lax.dot_general(kc_ref[0, 0], qc_ref[...], contract_last, preferred_element_type=F32)
    c_idx = lax.broadcasted_iota(jnp.int32, (nc, cols), 0)
    t_c = t0 + (lax.broadcasted_iota(jnp.int32, (nc, cols), 1) & (tq - 1))
    mask = (c_idx * CMP_STRIDE + (CMP_LEN - 1)) <= t_c
    s = jnp.where(mask, s, NEG_INF)
    m_c = jnp.max(s, axis=0, keepdims=True)
    p_c = jnp.where(mask, jnp.exp2(s - m_c), 0.0)
    l_c = jnp.sum(p_c, axis=0, keepdims=True)
    p_c = p_c * jnp.where(l_c > 0.0, 1.0 / l_c, 0.0)
    o_cmp = jnp.dot(vcT_ref[0, 0], p_c.astype(BF16), preferred_element_type=F32)

    p_sum = p_c[:, 0:tq]
    for h in range(1, hg):
        p_sum = p_sum + p_c[:, h * tq:(h + 1) * tq]
    hi = p_sum.astype(BF16)
    rem = p_sum - hi.astype(F32)
    mid = rem.astype(BF16)
    lo = (rem - mid.astype(F32)).astype(BF16)
    ovT = ovT_ref[...]
    imp = (jnp.dot(ovT, hi, preferred_element_type=F32)
           + jnp.dot(ovT, mid, preferred_element_type=F32)
           + jnp.dot(ovT, lo, preferred_element_type=F32))
    blk = lax.broadcasted_iota(jnp.int32, (n_blk, tq), 0)
    t_b = t0 + lax.broadcasted_iota(jnp.int32, (n_blk, tq), 1)
    cur = jnp.right_shift(t_b, SEL_BLK.bit_length() - 1)
    valid = blk * SEL_BLK <= t_b
    forced = (blk == 0) | (blk == cur) | (blk == cur - 1)
    score = jnp.where(valid, imp + FORCE_BONUS * forced.astype(F32), NEG_INF)
    score_ref[...] = score
    rank = jnp.zeros((n_blk, tq), F32)
    for j in range(n_blk):
        sj = score_ref[j:j + 1, :]
        beats = (sj > score) | ((sj == score) & (blk > j))
        rank = rank + beats.astype(F32)
    block_bias = jnp.where(rank < float(min(N_SEL, n_blk)), 0.0, NEG_INF)

    assert 2 * SLOPE_PIECES <= 8
    ones_rows = (lax.broadcasted_iota(jnp.int32, (8, tq), 0) < 2 * SLOPE_PIECES).astype(F32)
    extra_t = jnp.concatenate([block_bias, ones_rows,
                               jnp.zeros((AUG_WIDTH - n_blk - 8, tq), F32)], axis=0).T
    qs_ref[...] = stack_queries([(extra_t * lanes_ref[0, h:h + 1, :]).astype(BF16) for h in range(hg)])

    cw = WIN_CHUNK
    assert cw == tq and WINDOW % cw == 0 and cw == kmq_ref.shape[0]
    n_back = WINDOW // cw
    WIN, SEL = 0, 1

    def win_part(c, mask=None, bias=None):
        k = kw_ref[0, 0, pl.ds(pl.multiple_of(c * cw, cw), cw), :]
        return k, [vwT_ref[0, 0, c]], mask, bias

    threshold = jnp.where(qi >= n_back, 0, cw)
    reset(WIN)
    attend(WIN, qc_ref,
           [win_part(qi, mask=lambda: kmq_ref[...] <= 0)]
           + [win_part(jnp.maximum(qi - j, 0), bias=jnp.where(qi >= j, 0.0, NEG_INF))
              for j in range(1, n_back)]
           + [win_part(jnp.maximum(qi - n_back, 0), mask=lambda: kmq_ref[...] > threshold)])

    gate_ref[...] = jax.nn.sigmoid(gl_ref[...] + gb_ref[...]).T

    def gate(h, branch):
        r = h * N_BRANCH + branch
        return gate_ref[r:r + 1, :]

    o_win = result(WIN)
    for h in range(hg):
        cs = slice(h * tq, (h + 1) * tq)
        part_ref[:, cs] = gate(h, 0) * o_cmp[:, cs] + gate(h, 2) * o_win[:, cs]

    ck = SEL_CHUNK
    assert ck == tq and ck == kmq_ref.shape[0]

    def sel_part(c, n=1, mask=None):
        k = ks_ref[0, 0, c * ck:(c + n) * ck, :]
        return k, [vsT_ref[0, 0, c + i] for i in range(n)], mask, None

    reset(SEL)

    for tile in range(n_tiles):
        @pl.when(qi == tile)
        def _(tile=tile):
            parts = [sel_part(tile, mask=lambda: kmq_ref[...] <= 0)]
            parts += [sel_part(c, n=min(2, tile - c)) for c in range(0, tile, 2)]
            attend(SEL, qs_ref, parts)

    o_slc = result(SEL)
    for h in range(hg):
        cs = slice(h * tq, (h + 1) * tq)
        o_h = part_ref[:, cs] + gate(h, 1) * o_slc[:, cs]
        o_ref[:, h * dh:(h + 1) * dh] = o_h.T.astype(o_ref.dtype)


def nsa_attention(z, gl, gate_b, q_norm, kcmp, vcmpT, ks, vsT, kw, vwT, *, B, T):
    G, hg, dh, tq = NSA_KV_HEADS, NSA_GROUP, HEAD_DIM, Q_TILE
    nq = T // tq
    nc = kcmp.shape[2]
    n_blk = T // SEL_BLK
    kdim = dh + AUG_WIDTH
    cols = hg * tq
    assert SEL_CHUNK == WIN_CHUNK == tq
    n_units = max(WINDOW // WIN_CHUNK + 1, 1 + nq // 2) * (cols // NSA_COLS_PER_DOT)
    assert SEL_CHUNK % SEL_BLK == 0 and T % SEL_CHUNK == 0 and gl.shape[1] == G * LANES
    assert SEL_CHUNK % tq == 0 and n_blk % 8 == 0 and n_blk + 8 <= AUG_WIDTH

    slopes = np.exp2(-8.0 * (np.arange(NSA_HEADS, dtype=np.float64) + 1.0) / NSA_HEADS).reshape(G, hg)
    lanes = np.zeros((G, 2 * hg, AUG_WIDTH), np.float32)
    lanes[:, :hg, :n_blk] = 1.0
    for i, piece in enumerate(_bf16_pieces(slopes * LOG2E, SLOPE_PIECES)):
        for lane in (n_blk + i, n_blk + SLOPE_PIECES + i):
            lanes[:, :hg, lane] = piece
            lanes[:, hg:, lane] = piece
    cs = np.arange(nc)[:, None] * CMP_STRIDE
    bs = np.arange(n_blk)[None, :] * SEL_BLK
    overlap = np.clip(np.minimum(cs + CMP_LEN, bs + SEL_BLK) - np.maximum(cs, bs), 0, None) / CMP_LEN
    overlap[(T - CMP_LEN) // CMP_STRIDE + 1:] = 0.0
    ovT = jnp.asarray(overlap.T, dtype=BF16)
    hc = NSA_COLS_PER_DOT
    kmq = jnp.asarray(np.arange(SEL_CHUNK)[:, None] - (np.arange(hc)[None, :] % tq), dtype=jnp.int32)

    row = lambda b, qi: b * nq + qi
    per_bg = lambda shape: pl.BlockSpec((1, 1) + shape, lambda b, g, qi: (b, g) + (0,) * len(shape))
    const = lambda shape: pl.BlockSpec(shape, lambda b, g, qi: (0,) * len(shape))
    return pl.pallas_call(
        functools.partial(_nsa_body, n_tiles=nq),
        grid=(B, G, nq),
        in_specs=[pl.BlockSpec((tq, hg * dh), lambda b, g, qi: (row(b, qi), g)),
                  pl.BlockSpec((tq, LANES), lambda b, g, qi: (row(b, qi), g)),
                  pl.BlockSpec((1, LANES), lambda b, g, qi: (0, g)),
                  const((1, dh)),
                  pl.BlockSpec((1, 2 * hg, AUG_WIDTH), lambda b, g, qi: (g, 0, 0)),
                  per_bg((nc, kdim)), per_bg((dh, nc)),
                  per_bg((T, kdim)), per_bg((T // SEL_CHUNK, dh + V_EXTRA, SEL_CHUNK)),
                  per_bg((T, kdim)), per_bg((T // WIN_CHUNK, dh + V_EXTRA, WIN_CHUNK)),
                  const((n_blk, nc)), const((SEL_CHUNK, hc))],
        out_specs=pl.BlockSpec((tq, hg * dh), lambda b, g, qi: (row(b, qi), g)),
        out_shape=jax.ShapeDtypeStruct((B * T, NSA_WIDTH), BF16),
        scratch_shapes=[pltpu.VMEM((n_blk, tq), F32), pltpu.VMEM((LANES, tq), F32),
                        pltpu.VMEM((cols, kdim), BF16), pltpu.VMEM((cols, kdim), BF16),
                        pltpu.VMEM((2, 1, cols), F32),
                        pltpu.VMEM((2, dh + V_EXTRA, cols), F32), pltpu.VMEM((dh, cols), F32),
                        pltpu.VMEM((n_units, 2 * SEL_CHUNK, NSA_COLS_PER_DOT), F32)],
        compiler_params=_cparams("parallel", "parallel", "arbitrary"),
        name="nsa_attention",
    )(z, gl, gate_b, q_norm, jnp.asarray(lanes), kcmp, vcmpT, ks, vsT, kw, vwT, ovT, kmq)


def _conv_body(a_ref, b_ref, ah_ref, bh_ref, w_ref, cb_ref, lg_ref, lb_ref, o_ref,
               h_ref, g_ref, y_ref, *, tt):
    C = a_ref.shape[1]
    halo = ah_ref[...] * jax.nn.sigmoid(bh_ref[...])
    h_ref[0:CONV_HALO, :] = jnp.where(pl.program_id(1) == 0, 0.0, halo)
    h_ref[CONV_HALO:CONV_HALO + tt, :] = a_ref[...] * jax.nn.sigmoid(b_ref[...])
    h_ref[CONV_HALO + tt:CONV_HALO + tt + SUBLANES, :] = jnp.zeros((SUBLANES, C), F32)
    lead = CONV_HALO - (CONV_KERNEL - 1)
    rows = tt + SUBLANES
    for s in range(C // LANES):
        cs = slice(s * LANES, (s + 1) * LANES)
        acc = None
        for r in range(SUBLANES):
            group = None
            for o in range(lead, lead + CONV_KERNEL):
                if o % SUBLANES == r:
                    term = w_ref[o - lead:o - lead + 1, cs] * h_ref[o - r:o - r + rows, cs]
                    group = term if group is None else group + term
            if r == 0:
                piece = group[0:tt, :]
            else:
                g_ref[r] = group
                piece = g_ref[r, r:r + tt, :]
            acc = piece if acc is None else acc + piece
        y_ref[:, cs] = acc + cb_ref[:, cs]
    y = y_ref[...]
    mu = jnp.mean(y, axis=-1, keepdims=True)
    var = jnp.mean(jnp.square(y - mu), axis=-1, keepdims=True)
    yn = (y - mu) * lax.rsqrt(var + LN_EPS) * lg_ref[...] + lb_ref[...]
    o_ref[...] = _silu(yn).astype(o_ref.dtype)


def conformer_conv(z, conv_w, conv_b, ln_g, ln_b, *, B, T, col0, C, tt):
    nt = T // tt
    ca = col0 // C
    hb = tt // CONV_HALO
    row = lambda b, t: b * nt + t
    halo_row = lambda b, t: jnp.maximum(row(b, t) * hb - 1, 0)
    const = lambda shape: pl.BlockSpec(shape, lambda b, t: (0,) * len(shape))
    return pl.pallas_call(
        functools.partial(_conv_body, tt=tt),
        grid=(B, nt),
        in_specs=[pl.BlockSpec((tt, C), lambda b, t: (row(b, t), ca)),
                  pl.BlockSpec((tt, C), lambda b, t: (row(b, t), ca + 1)),
                  pl.BlockSpec((CONV_HALO, C), lambda b, t: (halo_row(b, t), ca)),
                  pl.BlockSpec((CONV_HALO, C), lambda b, t: (halo_row(b, t), ca + 1)),
                  const((CONV_KERNEL, C)), const((1, C)), const((1, C)), const((1, C))],
        out_specs=pl.BlockSpec((tt, C), lambda b, t: (row(b, t), 0)),
        out_shape=jax.ShapeDtypeStruct((B * T, C), BF16),
        scratch_shapes=[pltpu.VMEM((CONV_HALO + tt + SUBLANES, C), F32),
                        pltpu.VMEM((SUBLANES, tt + SUBLANES, LANES), F32),
                        pltpu.VMEM((tt, C), F32)],
        compiler_params=_cparams("parallel", "parallel"),
        name="conformer_conv",
    )(z, z, z, z, conv_w, conv_b, ln_g, ln_b)


def _mem_attn_body(x_ref, gq_ref, wq_ref, kv_ref, qn_ref, kn_ref, wo_ref, gn_ref, o_ref, on_ref):
    dh = HEAD_DIM
    width = MEM_HEADS * dh
    scale = dh ** -0.5
    q_all = jnp.dot(_rms(x_ref[...], gq_ref[...]).astype(BF16), wq_ref[...], preferred_element_type=F32)
    heads = [slice(h * dh, (h + 1) * dh) for h in range(MEM_HEADS)]
    scores = []
    for cs in heads:
        q = (_rms(q_all[:, cs], qn_ref[...]) * scale).astype(BF16)
        k = _rms(kv_ref[:, cs], kn_ref[...]).astype(BF16)
        scores.append(lax.dot_general(q, k, (((1,), (1,)), ((), ())), preferred_element_type=F32))
    outs = []
    for h, s in enumerate(scores):
        v = kv_ref[:, width + h * dh:width + (h + 1) * dh].astype(BF16)
        p = jnp.exp(s - jnp.max(s, axis=-1, keepdims=True))
        inv = 1.0 / jnp.sum(p, axis=-1, keepdims=True)
        outs.append((jnp.dot(p.astype(BF16), v, preferred_element_type=F32) * inv).astype(BF16))
    o = jnp.concatenate(outs, axis=1)
    y = x_ref[...] + jnp.dot(o, wo_ref[...], preferred_element_type=F32)
    o_ref[...] = y
    on_ref[...] = _rms(y, gn_ref[...]).astype(on_ref.dtype)


def mem_attention(x, q_gain, w_mq, kvm, mq_norm, mk_norm, w_mo, next_norm, *, B, T, M, tt):
    D = x.shape[1]
    width = MEM_HEADS * HEAD_DIM
    nt = T // tt
    row = lambda b, t: b * nt + t
    const = lambda shape: pl.BlockSpec(shape, lambda b, t: (0,) * len(shape))
    return pl.pallas_call(
        _mem_attn_body,
        grid=(B, nt),
        in_specs=[pl.BlockSpec((tt, D), lambda b, t: (row(b, t), 0)),
                  const((1, D)), const((D, width)),
                  pl.BlockSpec((M, 2 * width), lambda b, t: (b, 0)),
                  const((1, HEAD_DIM)), const((1, HEAD_DIM)), const((width, D)),
                  const((1, D))],
        out_specs=[pl.BlockSpec((tt, D), lambda b, t: (row(b, t), 0)),
                   pl.BlockSpec((tt, D), lambda b, t: (row(b, t), 0))],
        out_shape=[jax.ShapeDtypeStruct((B * T, D), F32),
                   jax.ShapeDtypeStruct((B * T, D), BF16)],
        compiler_params=_cparams("parallel", "parallel"),
        name="mem_attention",
    )(x, q_gain, w_mq, kvm, mq_norm, mk_norm, w_mo, next_norm)


def kernel(x, mem, norm_mix, w_in, gate_b, q_norm, k_norm_cmp, k_norm_slc, k_norm_win, cmp_pos_k, cmp_pos_v, cmp_k_w1, cmp_k_w2, cmp_v_w1, cmp_v_w2, conv_w, conv_b, conv_ln_g, conv_ln_b, w_out, norm_mem_q, norm_mem_kv, w_mq, w_mk, w_mv, mq_norm, mk_norm, w_mo, norm_ffn, w_gate, w_up, w_down):
    B, T, D = x.shape
    M = mem.shape[1]
    depth = w_in.shape[0]
    G, hg = NSA_KV_HEADS, NSA_GROUP
    C = D - NSA_WIDTH
    n_gate = N_BRANCH * NSA_HEADS
    kv_all = 6 * KV_WIDTH
    assert w_in.shape[2] == NSA_WIDTH + kv_all + n_gate + 2 * C
    assert C == NSA_WIDTH, "column blocking below assumes equal NSA and conv widths"

    row1 = lambda v: v.reshape(1, -1)
    xf = x.reshape(B * T, D)
    memf = mem.reshape(B * M, D)
    u0 = NSA_WIDTH + kv_all + n_gate
    for l in range(depth):
        w = w_in[l]
        w_t = jnp.swapaxes(w, 0, 1)
        wg = w[:, NSA_WIDTH + kv_all:u0].reshape(D, G, hg * N_BRANCH)
        wg = jnp.pad(wg, ((0, 0), (0, 0), (0, LANES - hg * N_BRANCH))).reshape(D, G * LANES)
        gb = jnp.pad(gate_b[l].reshape(G, hg * N_BRANCH), ((0, 0), (0, LANES - hg * N_BRANCH))).reshape(1, G * LANES)

        xn = rmsnorm_bf16(xf, row1(norm_mix[l]), tm=TILE_RMSNORM)
        z, gl = in_projection(xn, w_t, wg, tm=TILE_IN_PROJ[0], tn=TILE_IN_PROJ[1],
                              segments=((0, NSA_WIDTH), (u0, 2 * C), (NSA_WIDTH, kv_all)))
        kv_col0 = NSA_WIDTH + 2 * C
        kc, vc, ks, vsT, kw, vwT = kv_prep(z, row1(k_norm_slc[l]), row1(k_norm_win[l]),
                                           B=B, T=T, col0=kv_col0, tt=TILE_KV_PREP)
        kcmp, vcmpT = compress_kv(kc, vc, cmp_pos_k[l], cmp_pos_v[l],
                                  cmp_k_w1[l].astype(BF16), cmp_k_w2[l].astype(BF16),
                                  cmp_v_w1[l].astype(BF16), cmp_v_w2[l].astype(BF16), row1(k_norm_cmp[l]),
                                  n_blk=T // SEL_BLK)
        o_nsa = nsa_attention(z, gl, gb, row1(q_norm[l]), kcmp, vcmpT, ks, vsT, kw, vwT, B=B, T=T)
        o_conv = conformer_conv(z, conv_w[l], row1(conv_b[l]), row1(conv_ln_g[l]), row1(conv_ln_b[l]),
                                B=B, T=T, col0=NSA_WIDTH, C=C, tt=TILE_CONV)
        xf = concat_matmul_residual(o_nsa, o_conv, w_out[l], xf, tm=TILE_OUT_PROJ[0], tn=TILE_OUT_PROJ[1])

        w_mkv = jnp.concatenate([w_mk[l], w_mv[l]], axis=1).astype(BF16)
        kvm = norm_matmul(memf, row1(norm_mem_kv[l]), w_mkv, tm=B * M, tn=TILE_MEM_KV_COLS)
        xf, xn = mem_attention(xf, row1(norm_mem_q[l]), w_mq[l].astype(BF16), kvm,
                               row1(mq_norm[l]), row1(mk_norm[l]), w_mo[l].astype(BF16),
                               row1(norm_ffn[l]), B=B, T=T, M=M, tt=TILE_MEM_ATTN)

        hidden, w_down_bf16 = swiglu_up(xn, w_gate[l], w_up[l], w_down[l],
                                        tm=TILE_FFN_UP[0], tn=TILE_FFN_UP[1])
        xf = matmul_residual(hidden, w_down_bf16, xf, tm=TILE_FFN_DOWN[0], tn=TILE_FFN_DOWN[1])
    return xf.reshape(B, T, D)
```

```python
import functools

import ml_dtypes
import numpy as np
import jax
import jax.numpy as jnp
from jax import lax
from jax.experimental import pallas as pl
from jax.experimental.pallas import tpu as pltpu

F32 = jnp.float32
BF16 = jnp.bfloat16

HEAD_DIM = 128
NSA_HEADS = 8
NSA_KV_HEADS = 2
NSA_GROUP = NSA_HEADS // NSA_KV_HEADS
NSA_WIDTH = NSA_HEADS * HEAD_DIM
KV_WIDTH = NSA_KV_HEADS * HEAD_DIM
N_BRANCH = 3
CONV_KERNEL = 31
CMP_LEN = 32
CMP_STRIDE = 16
SEL_BLK = 64
N_SEL = 16
WINDOW = 512
MEM_HEADS = 4
NEG_INF = -1e30
FORCE_BONUS = 1e3
RMS_EPS = 1e-6
LN_EPS = 1e-5

LANES = 128
SUBLANES = 8
VMEM_LIMIT_BYTES = 56 * 1024 * 1024

Q_TILE = 256
SEL_CHUNK = 256
WIN_CHUNK = 256
NSA_COLS_PER_DOT = 512
CONV_HALO = 32

TILE_RMSNORM = 512
TILE_IN_PROJ = (2048, 512)
TILE_KV_PREP = 1024
TILE_CONV = 256
TILE_OUT_PROJ = (2048, 512)
TILE_MEM_ATTN = 512
TILE_FFN_UP = (1024, 512)
TILE_FFN_DOWN = (1024, 512)


def _cparams(*sem):
    return pltpu.CompilerParams(dimension_semantics=sem, vmem_limit_bytes=VMEM_LIMIT_BYTES)


def _rms(x, g, eps=RMS_EPS):
    ms = jnp.mean(x * x, axis=-1, keepdims=True)
    return x * lax.rsqrt(ms + eps) * g


def _silu(x):
    return x * jax.nn.sigmoid(x)


def _rmsnorm_body(x_ref, g_ref, o_ref):
    o_ref[...] = _rms(x_ref[...], g_ref[...]).astype(o_ref.dtype)


def rmsnorm_bf16(x, g, *, tm):
    M, K = x.shape
    return pl.pallas_call(
        _rmsnorm_body,
        grid=(M // tm,),
        in_specs=[pl.BlockSpec((tm, K), lambda i: (i, 0)), pl.BlockSpec((1, K), lambda i: (0, 0))],
        out_specs=pl.BlockSpec((tm, K), lambda i: (i, 0)),
        out_shape=jax.ShapeDtypeStruct((M, K), BF16),
        compiler_params=_cparams("parallel"),
        name="rmsnorm",
    )(x, g)


def _in_proj_body(xn_ref, wt_ref, wg_ref, o_ref, og_ref):
    @pl.when(pl.program_id(1) == 0)
    def _():
        og_ref[...] = jnp.dot(xn_ref[...], wg_ref[...].astype(BF16), preferred_element_type=F32)

    o_ref[...] = lax.dot_general(xn_ref[...], wt_ref[...].astype(BF16), (((1,), (1,)), ((), ())),
                                 preferred_element_type=F32)


def _swiglu_body(x_ref, wg_ref, wu_ref, wd_ref, o_ref, wd_o):
    x = x_ref[...]
    a = jnp.dot(x, wg_ref[...].astype(BF16), preferred_element_type=F32)
    b = jnp.dot(x, wu_ref[...].astype(BF16), preferred_element_type=F32)
    o_ref[...] = (_silu(a) * b).astype(o_ref.dtype)

    @pl.when(pl.program_id(0) == 0)
    def _():
        wd_o[...] = wd_ref[...].astype(wd_o.dtype)


def in_projection(xn, w_t, wg, *, segments, tm, tn):
    M, K = xn.shape
    NG = wg.shape[1]
    assert all(start % SUBLANES == 0 and size % tn == 0 for start, size in segments)
    N = sum(size for _, size in segments)

    def row_start(i, j):
        group, first = 0, 0
        for seg_start, size in segments:
            group = jnp.where(j >= first, seg_start // SUBLANES + (j - first) * (tn // SUBLANES), group)
            first += size // tn
        return (group * SUBLANES, 0)

    return pl.pallas_call(
        _in_proj_body,
        grid=(M // tm, N // tn),
        in_specs=[pl.BlockSpec((tm, K), lambda i, j: (i, 0)),
                  pl.BlockSpec((pl.Element(tn), pl.Element(K)), row_start),
                  pl.BlockSpec((K, NG), lambda i, j: (0, 0))],
        out_specs=[pl.BlockSpec((tm, tn), lambda i, j: (i, j)),
                   pl.BlockSpec((tm, NG), lambda i, j: (i, 0))],
        out_shape=[jax.ShapeDtypeStruct((M, N), F32),
                   jax.ShapeDtypeStruct((M, NG), F32)],
        compiler_params=_cparams("parallel", "arbitrary"),
        name="in_proj",
    )(xn, w_t, wg)


def swiglu_up(xn, w_gate, w_up, w_down, *, tm, tn):
    M, K = xn.shape
    N = w_gate.shape[1]
    D = w_down.shape[1]
    nj = N // tn
    assert w_down.shape[0] == N
    down_block = lambda i, j: (jnp.where(i == 0, j, nj - 1), 0)
    return pl.pallas_call(
        _swiglu_body,
        grid=(M // tm, nj),
        in_specs=[pl.BlockSpec((tm, K), lambda i, j: (i, 0)),
                  pl.BlockSpec((K, tn), lambda i, j: (0, j)),
                  pl.BlockSpec((K, tn), lambda i, j: (0, j)),
                  pl.BlockSpec((tn, D), down_block)],
        out_specs=[pl.BlockSpec((tm, tn), lambda i, j: (i, j)),
                   pl.BlockSpec((tn, D), down_block)],
        out_shape=[jax.ShapeDtypeStruct((M, N), BF16),
                   jax.ShapeDtypeStruct((N, D), BF16)],
        compiler_params=_cparams("arbitrary", "arbitrary"),
        name="ffn_up",
    )(xn, w_gate, w_up, w_down)


def _mm_res_body(a_ref, w_ref, r_ref, o_ref):
    o_ref[...] = r_ref[...] + jnp.dot(a_ref[...], w_ref[...], preferred_element_type=F32)


def matmul_residual(a, w, res, *, tm, tn):
    M, K = a.shape
    N = w.shape[1]
    return pl.pallas_call(
        _mm_res_body,
        grid=(M // tm, N // tn),
        in_specs=[pl.BlockSpec((tm, K), lambda i, j: (i, 0)),
                  pl.BlockSpec((K, tn), lambda i, j: (0, j)),
                  pl.BlockSpec((tm, tn), lambda i, j: (i, j))],
        out_specs=pl.BlockSpec((tm, tn), lambda i, j: (i, j)),
        out_shape=jax.ShapeDtypeStruct((M, N), F32),
        compiler_params=_cparams("parallel", "arbitrary"),
        name="matmul_residual",
    )(a, w, res)


def _mm2_res_body(a_ref, b_ref, wa_ref, wb_ref, r_ref, o_ref):
    acc = jnp.dot(a_ref[...], wa_ref[...].astype(BF16), preferred_element_type=F32)
    acc = acc + jnp.dot(b_ref[...], wb_ref[...].astype(BF16), preferred_element_type=F32)
    o_ref[...] = r_ref[...] + acc


def concat_matmul_residual(a, b, w, res, *, tm, tn):
    M, Ka = a.shape
    Kb = b.shape[1]
    assert Ka == Kb and w.shape[0] == Ka + Kb
    N = w.shape[1]
    return pl.pallas_call(
        _mm2_res_body,
        grid=(M // tm, N // tn),
        in_specs=[pl.BlockSpec((tm, Ka), lambda i, j: (i, 0)),
                  pl.BlockSpec((tm, Kb), lambda i, j: (i, 0)),
                  pl.BlockSpec((Ka, tn), lambda i, j: (0, j)),
                  pl.BlockSpec((Kb, tn), lambda i, j: (1, j)),
                  pl.BlockSpec((tm, tn), lambda i, j: (i, j))],
        out_specs=pl.BlockSpec((tm, tn), lambda i, j: (i, j)),
        out_shape=jax.ShapeDtypeStruct((M, N), F32),
        compiler_params=_cparams("parallel", "arbitrary"),
        name="out_proj",
    )(a, b, w, w, res)


AUG_WIDTH = LANES
V_EXTRA = 16
SLOPE_PIECES = 4
LOG2E = float(np.log2(np.e))


def _bf16_pieces(x, n):
    out, rem = [], np.asarray(x, np.float64)
    for _ in range(n):
        piece = rem.astype(ml_dtypes.bfloat16).astype(np.float64)
        out.append(piece.astype(np.float32))
        rem = rem - piece
    return out


def _position_lanes(pos_hi, pos_lo, onehot, n_blk, shape):
    lane = lax.broadcasted_iota(jnp.int32, shape, 1)
    p = SLOPE_PIECES
    extra = jnp.where((lane >= n_blk) & (lane < n_blk + p), pos_hi,
                      jnp.where((lane >= n_blk + p) & (lane < n_blk + 2 * p), pos_lo, 0.0))
    if onehot is not None:
        extra = jnp.where(lane < n_blk, onehot, extra)
    return extra.astype(BF16)


def _kv_prep_body(c_ref, s_ref, w_ref, kns_ref, knw_ref,
                  kc_o, vc_o, ks_o, vsT_o, kw_o, vwT_o, slab_ref, *, tt, n_blk):
    dh = HEAD_DIM
    shape = (tt, AUG_WIDTH)
    pos = pl.program_id(1) * tt + lax.broadcasted_iota(jnp.int32, shape, 0)
    lane = lax.broadcasted_iota(jnp.int32, shape, 1)
    blk = jnp.right_shift(pos, SEL_BLK.bit_length() - 1)
    pos_hi = (blk * SEL_BLK).astype(F32)
    pos_lo = (pos & (SEL_BLK - 1)).astype(F32)
    aug_sel = _position_lanes(pos_hi, pos_lo, (blk == lane).astype(F32), n_blk, shape)
    aug_win = _position_lanes(pos_hi, pos_lo, None, n_blk, shape)
    for g in range(NSA_KV_HEADS):
        k_cols = slice(g * dh, (g + 1) * dh)
        v_cols = slice(KV_WIDTH + g * dh, KV_WIDTH + (g + 1) * dh)
        for slab, (cols, out) in enumerate(((k_cols, kc_o), (v_cols, vc_o))):
            slab_ref[slab] = c_ref[:, cols]
            for i in range(CMP_STRIDE):
                rows = pl.ds(i, tt // CMP_STRIDE, stride=CMP_STRIDE)
                out[0, g, :, i * dh:(i + 1) * dh] = slab_ref[slab, rows, :]
        ks_o[0, g, :, 0:dh] = _rms(s_ref[:, k_cols], kns_ref[...]).astype(BF16)
        ks_o[0, g, :, dh:dh + AUG_WIDTH] = aug_sel
        kw_o[0, g, :, 0:dh] = _rms(w_ref[:, k_cols], knw_ref[...]).astype(BF16)
        kw_o[0, g, :, dh:dh + AUG_WIDTH] = aug_win
        for v_ref, out, chunk in ((s_ref, vsT_o, SEL_CHUNK), (w_ref, vwT_o, WIN_CHUNK)):
            vT = v_ref[:, v_cols].T.astype(BF16)
            ones_first = (lax.broadcasted_iota(jnp.int32, (V_EXTRA, chunk), 0) == 0).astype(BF16)
            for c in range(tt // chunk):
                out[0, g, c, 0:dh, :] = vT[:, c * chunk:(c + 1) * chunk]
                out[0, g, c, dh:dh + V_EXTRA, :] = ones_first


def kv_prep(z, kn_slc, kn_win, *, B, T, col0, tt):
    G, dh = NSA_KV_HEADS, HEAD_DIM
    nt = T // tt
    n_blk = T // SEL_BLK
    assert n_blk + 2 * SLOPE_PIECES <= AUG_WIDTH
    cb = col0 // (2 * KV_WIDTH)
    row = lambda b, t: b * nt + t
    kdim = dh + AUG_WIDTH
    return pl.pallas_call(
        functools.partial(_kv_prep_body, tt=tt, n_blk=n_blk),
        grid=(B, nt),
        in_specs=[pl.BlockSpec((tt, 2 * KV_WIDTH), lambda b, t: (row(b, t), cb)),
                  pl.BlockSpec((tt, 2 * KV_WIDTH), lambda b, t: (row(b, t), cb + 1)),
                  pl.BlockSpec((tt, 2 * KV_WIDTH), lambda b, t: (row(b, t), cb + 2)),
                  pl.BlockSpec((1, dh), lambda b, t: (0, 0)),
                  pl.BlockSpec((1, dh), lambda b, t: (0, 0))],
        out_specs=[pl.BlockSpec((1, G, tt // CMP_STRIDE, CMP_STRIDE * dh), lambda b, t: (b, 0, t, 0)),
                   pl.BlockSpec((1, G, tt // CMP_STRIDE, CMP_STRIDE * dh), lambda b, t: (b, 0, t, 0)),
                   pl.BlockSpec((1, G, tt, kdim), lambda b, t: (b, 0, t, 0)),
                   pl.BlockSpec((1, G, tt // SEL_CHUNK, dh + V_EXTRA, SEL_CHUNK), lambda b, t: (b, 0, t, 0, 0)),
                   pl.BlockSpec((1, G, tt, kdim), lambda b, t: (b, 0, t, 0)),
                   pl.BlockSpec((1, G, tt // WIN_CHUNK, dh + V_EXTRA, WIN_CHUNK), lambda b, t: (b, 0, t, 0, 0))],
        out_shape=[jax.ShapeDtypeStruct((B, G, T // CMP_STRIDE, CMP_STRIDE * dh), F32),
                   jax.ShapeDtypeStruct((B, G, T // CMP_STRIDE, CMP_STRIDE * dh), F32),
                   jax.ShapeDtypeStruct((B, G, T, kdim), BF16),
                   jax.ShapeDtypeStruct((B, G, T // SEL_CHUNK, dh + V_EXTRA, SEL_CHUNK), BF16),
                   jax.ShapeDtypeStruct((B, G, T, kdim), BF16),
                   jax.ShapeDtypeStruct((B, G, T // WIN_CHUNK, dh + V_EXTRA, WIN_CHUNK), BF16)],
        scratch_shapes=[pltpu.VMEM((2, tt, dh), F32)],
        compiler_params=_cparams("parallel", "parallel"),
        name="kv_prep",
    )(z, z, z, kn_slc, kn_win)


def _compress_body(xk_ref, xv_ref, pk_ref, pv_ref, w1k_ref, w2k_ref, w1v_ref, w2v_ref, kn_ref,
                   kc_o, vcT_o, *, n_blk):
    dh = HEAD_DIM

    def phi(x_ref, p_ref, w1_ref, w2_ref):
        x = x_ref[0, 0]
        first = jnp.dot((x + p_ref[0:1, :]).astype(BF16), w1_ref[0], preferred_element_type=F32)
        second = jnp.dot((x + p_ref[1:2, :]).astype(BF16), w1_ref[1], preferred_element_type=F32)
        n = second.shape[0]
        h = first + pltpu.roll(second, n - 1, 0)
        return jnp.dot(_silu(h).astype(BF16), w2_ref[...], preferred_element_type=F32)

    k = _rms(phi(xk_ref, pk_ref, w1k_ref, w2k_ref), kn_ref[...])
    nc = k.shape[0]
    shape = (nc, AUG_WIDTH)
    start = (lax.broadcasted_iota(jnp.int32, shape, 0) * CMP_STRIDE).astype(F32)
    centre = jnp.full(shape, (CMP_LEN - 1) / 2.0, F32)
    kc_o[0, 0, :, 0:dh] = k.astype(BF16)
    kc_o[0, 0, :, dh:dh + AUG_WIDTH] = _position_lanes(start, centre, None, n_blk, shape)
    vcT_o[0, 0] = phi(xv_ref, pv_ref, w1v_ref, w2v_ref).T.astype(BF16)


def compress_kv(xk, xv, pos_k, pos_v, w1k, w2k, w1v, w2v, kn_cmp, *, n_blk):
    B, G, nc, half = xk.shape
    dh = half // CMP_STRIDE
    assert CMP_LEN == 2 * CMP_STRIDE
    x_spec = pl.BlockSpec((1, 1, nc, half), lambda b, g: (b, g, 0, 0))
    full = lambda shape: pl.BlockSpec(shape, lambda b, g: (0,) * len(shape))
    return pl.pallas_call(
        functools.partial(_compress_body, n_blk=n_blk),
        grid=(B, G),
        in_specs=[x_spec, x_spec, full((2, half)), full((2, half)),
                  full((2, half, dh)), full((dh, dh)), full((2, half, dh)), full((dh, dh)),
                  full((1, dh))],
        out_specs=[pl.BlockSpec((1, 1, nc, dh + AUG_WIDTH), lambda b, g: (b, g, 0, 0)),
                   pl.BlockSpec((1, 1, dh, nc), lambda b, g: (b, g, 0, 0))],
        out_shape=[jax.ShapeDtypeStruct((B, G, nc, dh + AUG_WIDTH), BF16),
                   jax.ShapeDtypeStruct((B, G, dh, nc), BF16)],
        compiler_params=_cparams("parallel", "parallel"),
        name="compress_kv",
    )(xk, xv, pos_k.reshape(2, half), pos_v.reshape(2, half),
      w1k.reshape(2, half, dh), w2k, w1v.reshape(2, half, dh), w2v, kn_cmp)


def _nsa_body(q_ref, gl_ref, gb_ref, qn_ref, lanes_ref, kc_ref, vcT_ref,
              ks_ref, vsT_ref, kw_ref, vwT_ref, ovT_ref, kmq_ref, o_ref,
              score_ref, gate_ref, qc_ref, qs_ref, m_ref, acc_ref, part_ref, sbuf_ref, *, n_tiles):
    dh, hg, tq = HEAD_DIM, NSA_GROUP, Q_TILE
    cols = hg * tq
    qi = pl.program_id(2)
    t0 = qi * tq
    contract_last = (((1,), (1,)), ((), ()))
    n_blk = ovT_ref.shape[0]

    scale = dh ** -0.5 * LOG2E
    q_heads = [(_rms(q_ref[:, h * dh:(h + 1) * dh], qn_ref[...]) * scale).astype(BF16)
               for h in range(hg)]

    def stack_queries(aug_heads):
        return jnp.concatenate([jnp.concatenate([q_heads[h], aug_heads[h]], axis=1)
                                for h in range(hg)], axis=0)

    hc = kmq_ref.shape[1]
    assert cols % hc == 0 and hc % tq == 0

    def reset(b):
        m_ref[b] = jnp.full((1, cols), NEG_INF, F32)
        acc_ref[b] = jnp.zeros((dh + V_EXTRA, cols), F32)

    def attend(b, queries_ref, parts):
        groups = [slice(i * hc, (i + 1) * hc) for i in range(cols // hc)]
        unit = 0
        for k, _, _, _ in parts:
            for cs in groups:
                sbuf_ref[unit, 0:k.shape[0], :] = lax.dot_general(
                    k, queries_ref[cs, :], contract_last, preferred_element_type=F32)
                unit += 1
        unit = 0
        for (k, v_pieces, mask, bias) in parts:
            for cs in groups:
                s_ref = sbuf_ref.at[unit, 0:k.shape[0], :]
                unit += 1
                if mask is not None:
                    s_ref[...] = jnp.where(mask(), s_ref[...], NEG_INF)
                if bias is not None:
                    s_ref[...] = s_ref[...] + bias
                m_old = m_ref[b, :, cs]
                m_new = jnp.maximum(m_old, jnp.max(s_ref[...], axis=0, keepdims=True))
                alpha = jnp.exp2(m_old - m_new)
                p = jnp.exp2(s_ref[...] - m_new).astype(BF16)
                m_ref[b, :, cs] = m_new
                pv = None
                for i, vT in enumerate(v_pieces):
                    n = vT.shape[1]
                    d = jnp.dot(vT, p[i * n:(i + 1) * n, :], preferred_element_type=F32)
                    pv = d if pv is None else pv + d
                acc_ref[b, :, cs] = alpha * acc_ref[b, :, cs] + pv

    def result(b):
        return acc_ref[b, 0:dh, :] * (1.0 / acc_ref[b, dh:dh + 1, :])

    nc = kc_ref.shape[2]
    qc_ref[...] = stack_queries([jnp.broadcast_to(lanes_ref[0, hg + h:hg + h + 1, :], (tq, AUG_WIDTH)).astype(BF16)
                                 for h in range(hg)])
    s = lax.dot_general(kc_ref[0, 0], qc_ref[...], contract_last, preferred_element_type=F32)
    c_idx = lax.broadcasted_iota(jnp.int32, (nc, cols), 0)
    t_c = t0 + (lax.broadcasted_iota(jnp.int32, (nc, cols), 1) & (tq - 1))
    mask = (c_idx * CMP_STRIDE + (CMP_LEN - 1)) <= t_c
    s = jnp.where(mask, s, NEG_INF)
    m_c = jnp.max(s, axis=0, keepdims=True)
    p_c = jnp.where(mask, jnp.exp2(s - m_c), 0.0)
    l_c = jnp.sum(p_c, axis=0, keepdims=True)
    p_c = p_c * jnp.where(l_c > 0.0, 1.0 / l_c, 0.0)
    o_cmp = jnp.dot(vcT_ref[0, 0], p_c.astype(BF16), preferred_element_type=F32)

    p_sum = p_c[:, 0:tq]
    for h in range(1, hg):
        p_sum = p_sum + p_c[:, h * tq:(h + 1) * tq]
    hi = p_sum.astype(BF16)
    rem = p_sum - hi.astype(F32)
    mid = rem.astype(BF16)
    lo = (rem - mid.astype(F32)).astype(BF16)
    ovT = ovT_ref[...]
    imp = (jnp.dot(ovT, hi, preferred_element_type=F32)
           + jnp.dot(ovT, mid, preferred_element_type=F32)
           + jnp.dot(ovT, lo, preferred_element_type=F32))
    blk = lax.broadcasted_iota(jnp.int32, (n_blk, tq), 0)
    t_b = t0 + lax.broadcasted_iota(jnp.int32, (n_blk, tq), 1)
    cur = jnp.right_shift(t_b, SEL_BLK.bit_length() - 1)
    valid = blk * SEL_BLK <= t_b
    forced = (blk == 0) | (blk == cur) | (blk == cur - 1)
    score = jnp.where(valid, imp + FORCE_BONUS * forced.astype(F32), NEG_INF)
    score_ref[...] = score
    rank = jnp.zeros((n_blk, tq), F32)
    for j in range(n_blk):
        sj = score_ref[j:j + 1, :]
        beats = (sj > score) | ((sj == score) & (blk > j))
        rank = rank + beats.astype(F32)
    block_bias = jnp.where(rank < float(min(N_SEL, n_blk)), 0.0, NEG_INF)

    assert 2 * SLOPE_PIECES <= 8
    ones_rows = (lax.broadcasted_iota(jnp.int32, (8, tq), 0) < 2 * SLOPE_PIECES).astype(F32)
    extra_t = jnp.concatenate([block_bias, ones_rows,
                               jnp.zeros((AUG_WIDTH - n_blk - 8, tq), F32)], axis=0).T
    qs_ref[...] = stack_queries([(extra_t * lanes_ref[0, h:h + 1, :]).astype(BF16) for h in range(hg)])

    cw = WIN_CHUNK
    assert cw == tq and WINDOW % cw == 0 and cw == kmq_ref.shape[0]
    n_back = WINDOW // cw
    WIN, SEL = 0, 1

    def win_part(c, mask=None, bias=None):
        k = kw_ref[0, 0, pl.ds(pl.multiple_of(c * cw, cw), cw), :]
        return k, [vwT_ref[0, 0, c]], mask, bias

    threshold = jnp.where(qi >= n_back, 0, cw)
    reset(WIN)
    attend(WIN, qc_ref,
           [win_part(qi, mask=lambda: kmq_ref[...] <= 0)]
           + [win_part(jnp.maximum(qi - j, 0), bias=jnp.where(qi >= j, 0.0, NEG_INF))
              for j in range(1, n_back)]
           + [win_part(jnp.maximum(qi - n_back, 0), mask=lambda: kmq_ref[...] > threshold)])

    gate_ref[...] = jax.nn.sigmoid(gl_ref[...] + gb_ref[...]).T

    def gate(h, branch):
        r = h * N_BRANCH + branch
        return gate_ref[r:r + 1, :]

    o_win = result(WIN)
    for h in range(hg):
        cs = slice(h * tq, (h + 1) * tq)
        part_ref[:, cs] = gate(h, 0) * o_cmp[:, cs] + gate(h, 2) * o_win[:, cs]

    ck = SEL_CHUNK
    assert ck == tq and ck == kmq_ref.shape[0]

    def sel_part(c, n=1, mask=None):
        k = ks_ref[0, 0, c * ck:(c + n) * ck, :]
        return k, [vsT_ref[0, 0, c + i] for i in range(n)], mask, None

    reset(SEL)

    for tile in range(n_tiles):
        @pl.when(qi == tile)
        def _(tile=tile):
            parts = [sel_part(tile, mask=lambda: kmq_ref[...] <= 0)]
            parts += [sel_part(c, n=min(2, tile - c)) for c in range(0, tile, 2)]
            attend(SEL, qs_ref, parts)

    o_slc = result(SEL)
    for h in range(hg):
        cs = slice(h * tq, (h + 1) * tq)
        o_h = part_ref[:, cs] + gate(h, 1) * o_slc[:, cs]
        o_ref[:, h * dh:(h + 1) * dh] = o_h.T.astype(o_ref.dtype)


def nsa_attention(z, gl, gate_b, q_norm, kcmp, vcmpT, ks, vsT, kw, vwT, *, B, T):
    G, hg, dh, tq = NSA_KV_HEADS, NSA_GROUP, HEAD_DIM, Q_TILE
    nq = T // tq
    nc = kcmp.shape[2]
    n_blk = T // SEL_BLK
    kdim = dh + AUG_WIDTH
    cols = hg * tq
    assert SEL_CHUNK == WIN_CHUNK == tq
    n_units = max(WINDOW // WIN_CHUNK + 1, 1 + nq // 2) * (cols // NSA_COLS_PER_DOT)
    assert SEL_CHUNK % SEL_BLK == 0 and T % SEL_CHUNK == 0 and gl.shape[1] == G * LANES
    assert SEL_CHUNK % tq == 0 and n_blk % 8 == 0 and n_blk + 8 <= AUG_WIDTH

    slopes = np.exp2(-8.0 * (np.arange(NSA_HEADS, dtype=np.float64) + 1.0) / NSA_HEADS).reshape(G, hg)
    lanes = np.zeros((G, 2 * hg, AUG_WIDTH), np.float32)
    lanes[:, :hg, :n_blk] = 1.0
    for i, piece in enumerate(_bf16_pieces(slopes * LOG2E, SLOPE_PIECES)):
        for lane in (n_blk + i, n_blk + SLOPE_PIECES + i):
            lanes[:, :hg, lane] = piece
            lanes[:, hg:, lane] = piece
    cs = np.arange(nc)[:, None] * CMP_STRIDE
    bs = np.arange(n_blk)[None, :] * SEL_BLK
    overlap = np.clip(np.minimum(cs + CMP_LEN, bs + SEL_BLK) - np.maximum(cs, bs), 0, None) / CMP_LEN
    overlap[(T - CMP_LEN) // CMP_STRIDE + 1:] = 0.0
    ovT = jnp.asarray(overlap.T, dtype=BF16)
    hc = NSA_COLS_PER_DOT
    kmq = jnp.asarray(np.arange(SEL_CHUNK)[:, None] - (np.arange(hc)[None, :] % tq), dtype=jnp.int32)

    row = lambda b, qi: b * nq + qi
    per_bg = lambda shape: pl.BlockSpec((1, 1) + shape, lambda b, g, qi: (b, g) + (0,) * len(shape))
    const = lambda shape: pl.BlockSpec(shape, lambda b, g, qi: (0,) * len(shape))
    return pl.pallas_call(
        functools.partial(_nsa_body, n_tiles=nq),
        grid=(B, G, nq),
        in_specs=[pl.BlockSpec((tq, hg * dh), lambda b, g, qi: (row(b, qi), g)),
                  pl.BlockSpec((tq, LANES), lambda b, g, qi: (row(b, qi), g)),
                  pl.BlockSpec((1, LANES), lambda b, g, qi: (0, g)),
                  const((1, dh)),
                  pl.BlockSpec((1, 2 * hg, AUG_WIDTH), lambda b, g, qi: (g, 0, 0)),
                  per_bg((nc, kdim)), per_bg((dh, nc)),
                  per_bg((T, kdim)), per_bg((T // SEL_CHUNK, dh + V_EXTRA, SEL_CHUNK)),
                  per_bg((T, kdim)), per_bg((T // WIN_CHUNK, dh + V_EXTRA, WIN_CHUNK)),
                  const((n_blk, nc)), const((SEL_CHUNK, hc))],
        out_specs=pl.BlockSpec((tq, hg * dh), lambda b, g, qi: (row(b, qi), g)),
        out_shape=jax.ShapeDtypeStruct((B * T, NSA_WIDTH), BF16),
        scratch_shapes=[pltpu.VMEM((n_blk, tq), F32), pltpu.VMEM((LANES, tq), F32),
                        pltpu.VMEM((cols, kdim), BF16), pltpu.VMEM((cols, kdim), BF16),
                        pltpu.VMEM((2, 1, cols), F32),
                        pltpu.VMEM((2, dh + V_EXTRA, cols), F32), pltpu.VMEM((dh, cols), F32),
                        pltpu.VMEM((n_units, 2 * SEL_CHUNK, NSA_COLS_PER_DOT), F32)],
        compiler_params=_cparams("parallel", "parallel", "arbitrary"),
        name="nsa_attention",
    )(z, gl, gate_b, q_norm, jnp.asarray(lanes), kcmp, vcmpT, ks, vsT, kw, vwT, ovT, kmq)


def _conv_body(a_ref, b_ref, ah_ref, bh_ref, w_ref, cb_ref, lg_ref, lb_ref, o_ref,
               h_ref, g_ref, y_ref, *, tt):
    C = a_ref.shape[1]
    halo = ah_ref[...] * jax.nn.sigmoid(bh_ref[...])
    h_ref[0:CONV_HALO, :] = jnp.where(pl.program_id(1) == 0, 0.0, halo)
    h_ref[CONV_HALO:CONV_HALO + tt, :] = a_ref[...] * jax.nn.sigmoid(b_ref[...])
    h_ref[CONV_HALO + tt:CONV_HALO + tt + SUBLANES, :] = jnp.zeros((SUBLANES, C), F32)
    lead = CONV_HALO - (CONV_KERNEL - 1)
    rows = tt + SUBLANES
    for s in range(C // LANES):
        cs = slice(s * LANES, (s + 1) * LANES)
        acc = None
        for r in range(SUBLANES):
            group = None
            for o in range(lead, lead + CONV_KERNEL):
                if o % SUBLANES == r:
                    term = w_ref[o - lead:o - lead + 1, cs] * h_ref[o - r:o - r + rows, cs]
                    group = term if group is None else group + term
            if r == 0:
                piece = group[0:tt, :]
            else:
                g_ref[r] = group
                piece = g_ref[r, r:r + tt, :]
            acc = piece if acc is None else acc + piece
        y_ref[:, cs] = acc + cb_ref[:, cs]
    y = y_ref[...]
    mu = jnp.mean(y, axis=-1, keepdims=True)
    var = jnp.mean(jnp.square(y - mu), axis=-1, keepdims=True)
    yn = (y - mu) * lax.rsqrt(var + LN_EPS) * lg_ref[...] + lb_ref[...]
    o_ref[...] = _silu(yn).astype(o_ref.dtype)


def conformer_conv(z, conv_w, conv_b, ln_g, ln_b, *, B, T, col0, C, tt):
    nt = T // tt
    ca = col0 // C
    hb = tt // CONV_HALO
    row = lambda b, t: b * nt + t
    halo_row = lambda b, t: jnp.maximum(row(b, t) * hb - 1, 0)
    const = lambda shape: pl.BlockSpec(shape, lambda b, t: (0,) * len(shape))
    return pl.pallas_call(
        functools.partial(_conv_body, tt=tt),
        grid=(B, nt),
        in_specs=[pl.BlockSpec((tt, C), lambda b, t: (row(b, t), ca)),
                  pl.BlockSpec((tt, C), lambda b, t: (row(b, t), ca + 1)),
                  pl.BlockSpec((CONV_HALO, C), lambda b, t: (halo_row(b, t), ca)),
                  pl.BlockSpec((CONV_HALO, C), lambda b, t: (halo_row(b, t), ca + 1)),
                  const((CONV_KERNEL, C)), const((1, C)), const((1, C)), const((1, C))],
        out_specs=pl.BlockSpec((tt, C), lambda b, t: (row(b, t), 0)),
        out_shape=jax.ShapeDtypeStruct((B * T, C), BF16),
        scratch_shapes=[pltpu.VMEM((CONV_HALO + tt + SUBLANES, C), F32),
                        pltpu.VMEM((SUBLANES, tt + SUBLANES, LANES), F32),
                        pltpu.VMEM((tt, C), F32)],
        compiler_params=_cparams("parallel", "parallel"),
        name="conformer_conv",
    )(z, z, z, z, conv_w, conv_b, ln_g, ln_b)


def _mem_attn_body(x_ref, gq_ref, wq_ref, mem_ref, gkv_ref, wkv_ref, qn_ref, kn_ref, wo_ref, gn_ref,
                   o_ref, on_ref, k_scr, v_scr):
    dh = HEAD_DIM
    width = MEM_HEADS * dh
    scale = dh ** -0.5
    heads = [slice(h * dh, (h + 1) * dh) for h in range(MEM_HEADS)]

    @pl.when(pl.program_id(1) == 0)
    def _():
        kv = jnp.dot(_rms(mem_ref[...], gkv_ref[...]).astype(BF16), wkv_ref[...], preferred_element_type=F32)
        for cs in heads:
            k_scr[:, cs] = _rms(kv[:, cs], kn_ref[...]).astype(BF16)
        v_scr[...] = kv[:, width:2 * width].astype(BF16)

    q_all = jnp.dot(_rms(x_ref[...], gq_ref[...]).astype(BF16), wq_ref[...], preferred_element_type=F32)
    scores = []
    for cs in heads:
        q = (_rms(q_all[:, cs], qn_ref[...]) * scale).astype(BF16)
        scores.append(lax.dot_general(q, k_scr[:, cs], (((1,), (1,)), ((), ())), preferred_element_type=F32))
    outs = []
    for h, s in enumerate(scores):
        v = v_scr[:, heads[h]]
        p = jnp.exp(s - jnp.max(s, axis=-1, keepdims=True))
        inv = 1.0 / jnp.sum(p, axis=-1, keepdims=True)
        outs.append((jnp.dot(p.astype(BF16), v, preferred_element_type=F32) * inv).astype(BF16))
    o = jnp.concatenate(outs, axis=1)
    y = x_ref[...] + jnp.dot(o, wo_ref[...], preferred_element_type=F32)
    o_ref[...] = y
    on_ref[...] = _rms(y, gn_ref[...]).astype(on_ref.dtype)


def mem_attention(x, q_gain, w_mq, mem, kv_gain, w_mkv, mq_norm, mk_norm, w_mo, next_norm, *, B, T, M, tt):
    D = x.shape[1]
    width = MEM_HEADS * HEAD_DIM
    nt = T // tt
    row = lambda b, t: b * nt + t
    const = lambda shape: pl.BlockSpec(shape, lambda b, t: (0,) * len(shape))
    return pl.pallas_call(
        _mem_attn_body,
        grid=(B, nt),
        in_specs=[pl.BlockSpec((tt, D), lambda b, t: (row(b, t), 0)),
                  const((1, D)), const((D, width)),
                  pl.BlockSpec((M, D), lambda b, t: (b, 0)),
                  const((1, D)), const((D, 2 * width)),
                  const((1, HEAD_DIM)), const((1, HEAD_DIM)), const((width, D)),
                  const((1, D))],
        out_specs=[pl.BlockSpec((tt, D), lambda b, t: (row(b, t), 0)),
                   pl.BlockSpec((tt, D), lambda b, t: (row(b, t), 0))],
        out_shape=[jax.ShapeDtypeStruct((B * T, D), F32),
                   jax.ShapeDtypeStruct((B * T, D), BF16)],
        scratch_shapes=[pltpu.VMEM((M, width), BF16), pltpu.VMEM((M, width), BF16)],
        compiler_params=_cparams("parallel", "arbitrary"),
        name="mem_attention",
    )(x, q_gain, w_mq, mem, kv_gain, w_mkv, mq_norm, mk_norm, w_mo, next_norm)


def kernel(x, mem, norm_mix, w_in, gate_b, q_norm, k_norm_cmp, k_norm_slc, k_norm_win, cmp_pos_k, cmp_pos_v, cmp_k_w1, cmp_k_w2, cmp_v_w1, cmp_v_w2, conv_w, conv_b, conv_ln_g, conv_ln_b, w_out, norm_mem_q, norm_mem_kv, w_mq, w_mk, w_mv, mq_norm, mk_norm, w_mo, norm_ffn, w_gate, w_up, w_down):
    B, T, D = x.shape
    M = mem.shape[1]
    depth = w_in.shape[0]
    G, hg = NSA_KV_HEADS, NSA_GROUP
    C = D - NSA_WIDTH
    n_gate = N_BRANCH * NSA_HEADS
    kv_all = 6 * KV_WIDTH
    assert w_in.shape[2] == NSA_WIDTH + kv_all + n_gate + 2 * C
    assert C == NSA_WIDTH, "column blocking below assumes equal NSA and conv widths"

    row1 = lambda v: v.reshape(1, -1)
    xf = x.reshape(B * T, D)
    memf = mem.reshape(B * M, D)
    u0 = NSA_WIDTH + kv_all + n_gate
    for l in range(depth):
        w = w_in[l]
        w_t = jnp.swapaxes(w, 0, 1)
        wg = w[:, NSA_WIDTH + kv_all:u0].reshape(D, G, hg * N_BRANCH)
        wg = jnp.pad(wg, ((0, 0), (0, 0), (0, LANES - hg * N_BRANCH))).reshape(D, G * LANES)
        gb = jnp.pad(gate_b[l].reshape(G, hg * N_BRANCH), ((0, 0), (0, LANES - hg * N_BRANCH))).reshape(1, G * LANES)

        xn = rmsnorm_bf16(xf, row1(norm_mix[l]), tm=TILE_RMSNORM)
        z, gl = in_projection(xn, w_t, wg, tm=TILE_IN_PROJ[0], tn=TILE_IN_PROJ[1],
                              segments=((0, NSA_WIDTH), (u0, 2 * C), (NSA_WIDTH, kv_all)))
        kv_col0 = NSA_WIDTH + 2 * C
        kc, vc, ks, vsT, kw, vwT = kv_prep(z, row1(k_norm_slc[l]), row1(k_norm_win[l]),
                                           B=B, T=T, col0=kv_col0, tt=TILE_KV_PREP)
        kcmp, vcmpT = compress_kv(kc, vc, cmp_pos_k[l], cmp_pos_v[l],
                                  cmp_k_w1[l].astype(BF16), cmp_k_w2[l].astype(BF16),
                                  cmp_v_w1[l].astype(BF16), cmp_v_w2[l].astype(BF16), row1(k_norm_cmp[l]),
                                  n_blk=T // SEL_BLK)
        o_nsa = nsa_attention(z, gl, gb, row1(q_norm[l]), kcmp, vcmpT, ks, vsT, kw, vwT, B=B, T=T)
        o_conv = conformer_conv(z, conv_w[l], row1(conv_b[l]), row1(conv_ln_g[l]), row1(conv_ln_b[l]),
                                B=B, T=T, col0=NSA_WIDTH, C=C, tt=TILE_CONV)
        xf = concat_matmul_residual(o_nsa, o_conv, w_out[l], xf, tm=TILE_OUT_PROJ[0], tn=TILE_OUT_PROJ[1])

        w_mkv = jnp.concatenate([w_mk[l], w_mv[l]], axis=1).astype(BF16)
        xf, xn = mem_attention(xf, row1(norm_mem_q[l]), w_mq[l].astype(BF16),
                               memf, row1(norm_mem_kv[l]), w_mkv,
                               row1(mq_norm[l]), row1(mk_norm[l]), w_mo[l].astype(BF16),
                               row1(norm_ffn[l]), B=B, T=T, M=M, tt=TILE_MEM_ATTN)

        hidden, w_down_bf16 = swiglu_up(xn, w_gate[l], w_up[l], w_down[l],
                                        tm=TILE_FFN_UP[0], tn=TILE_FFN_UP[1])
        xf = matmul_residual(hidden, w_down_bf16, xf, tm=TILE_FFN_DOWN[0], tn=TILE_FFN_DOWN[1])
    return xf.reshape(B, T, D)
```

```python
import functools

import ml_dtypes
import numpy as np
import jax
import jax.numpy as jnp
from jax import lax
from jax.experimental import pallas as pl
from jax.experimental.pallas import tpu as pltpu

F32 = jnp.float32
BF16 = jnp.bfloat16

HEAD_DIM = 128
NSA_HEADS = 8
NSA_KV_HEADS = 2
NSA_GROUP = NSA_HEADS // NSA_KV_HEADS
NSA_WIDTH = NSA_HEADS * HEAD_DIM
KV_WIDTH = NSA_KV_HEADS * HEAD_DIM
N_BRANCH = 3
CONV_KERNEL = 31
CMP_LEN = 32
CMP_STRIDE = 16
SEL_BLK = 64
N_SEL = 16
WINDOW = 512
MEM_HEADS = 4
NEG_INF = -1e30
FORCE_BONUS = 1e3
RMS_EPS = 1e-6
LN_EPS = 1e-5

LANES = 128
SUBLANES = 8
VMEM_LIMIT_BYTES = 56 * 1024 * 1024

Q_TILE = 256
SEL_CHUNK = 256
WIN_CHUNK = 256
NSA_COLS_PER_DOT = 512
CONV_HALO = 32

TILE_RMSNORM = 512
TILE_IN_PROJ = (2048, 512)
TILE_KV_PREP = 1024
TILE_CONV = 256
TILE_OUT_PROJ = (2048, 512)
TILE_MEM_ATTN = 512
TILE_FFN_UP = (1024, 512)
TILE_FFN_DOWN = (1024, 512)


def _cparams(*sem):
    return pltpu.CompilerParams(dimension_semantics=sem, vmem_limit_bytes=VMEM_LIMIT_BYTES)


def _rms(x, g, eps=RMS_EPS):
    ms = jnp.mean(x * x, axis=-1, keepdims=True)
    return x * lax.rsqrt(ms + eps) * g


def _silu(x):
    return x * jax.nn.sigmoid(x)


def _rmsnorm_body(x_ref, g_ref, o_ref):
    o_ref[...] = _rms(x_ref[...], g_ref[...]).astype(o_ref.dtype)


def rmsnorm_bf16(x, g, *, tm):
    M, K = x.shape
    return pl.pallas_call(
        _rmsnorm_body,
        grid=(M // tm,),
        in_specs=[pl.BlockSpec((tm, K), lambda i: (i, 0)), pl.BlockSpec((1, K), lambda i: (0, 0))],
        out_specs=pl.BlockSpec((tm, K), lambda i: (i, 0)),
        out_shape=jax.ShapeDtypeStruct((M, K), BF16),
        compiler_params=_cparams("parallel"),
        name="rmsnorm",
    )(x, g)


def _in_proj_body(xn_ref, wt_ref, wg_ref, o_ref, og_ref):
    @pl.when(pl.program_id(1) == 0)
    def _():
        og_ref[...] = jnp.dot(xn_ref[...], wg_ref[...].astype(BF16), preferred_element_type=F32)

    o_ref[...] = lax.dot_general(xn_ref[...], wt_ref[...].astype(BF16), (((1,), (1,)), ((), ())),
                                 preferred_element_type=F32)


def _swiglu_body(x_ref, wg_ref, wu_ref, wd_ref, o_ref, wd_o):
    x = x_ref[...]
    a = jnp.dot(x, wg_ref[...].astype(BF16), preferred_element_type=F32)
    b = jnp.dot(x, wu_ref[...].astype(BF16), preferred_element_type=F32)
    o_ref[...] = (_silu(a) * b).astype(o_ref.dtype)

    @pl.when(pl.program_id(0) == 0)
    def _():
        wd_o[...] = wd_ref[...].astype(wd_o.dtype)


def in_projection(xn, w_t, wg, *, segments, tm, tn):
    M, K = xn.shape
    NG = wg.shape[1]
    assert all(start % SUBLANES == 0 and size % tn == 0 for start, size in segments)
    N = sum(size for _, size in segments)

    def row_start(i, j):
        group, first = 0, 0
        for seg_start, size in segments:
            group = jnp.where(j >= first, seg_start // SUBLANES + (j - first) * (tn // SUBLANES), group)
            first += size // tn
        return (group * SUBLANES, 0)

    return pl.pallas_call(
        _in_proj_body,
        grid=(M // tm, N // tn),
        in_specs=[pl.BlockSpec((tm, K), lambda i, j: (i, 0)),
                  pl.BlockSpec((pl.Element(tn), pl.Element(K)), row_start),
                  pl.BlockSpec((K, NG), lambda i, j: (0, 0))],
        out_specs=[pl.BlockSpec((tm, tn), lambda i, j: (i, j)),
                   pl.BlockSpec((tm, NG), lambda i, j: (i, 0))],
        out_shape=[jax.ShapeDtypeStruct((M, N), F32),
                   jax.ShapeDtypeStruct((M, NG), F32)],
        compiler_params=_cparams("parallel", "arbitrary"),
        name="in_proj",
    )(xn, w_t, wg)


def swiglu_up(xn, w_gate, w_up, w_down, *, tm, tn):
    M, K = xn.shape
    N = w_gate.shape[1]
    D = w_down.shape[1]
    nj = N // tn
    assert w_down.shape[0] == N
    down_block = lambda i, j: (jnp.where(i == 0, j, nj - 1), 0)
    return pl.pallas_call(
        _swiglu_body,
        grid=(M // tm, nj),
        in_specs=[pl.BlockSpec((tm, K), lambda i, j: (i, 0)),
                  pl.BlockSpec((K, tn), lambda i, j: (0, j)),
                  pl.BlockSpec((K, tn), lambda i, j: (0, j)),
                  pl.BlockSpec((tn, D), down_block)],
        out_specs=[pl.BlockSpec((tm, tn), lambda i, j: (i, j)),
                   pl.BlockSpec((tn, D), down_block)],
        out_shape=[jax.ShapeDtypeStruct((M, N), BF16),
                   jax.ShapeDtypeStruct((N, D), BF16)],
        compiler_params=_cparams("arbitrary", "arbitrary"),
        name="ffn_up",
    )(xn, w_gate, w_up, w_down)


def _mm_res_body(a_ref, w_ref, r_ref, o_ref):
    o_ref[...] = r_ref[...] + jnp.dot(a_ref[...], w_ref[...], preferred_element_type=F32)


def matmul_residual(a, w, res, *, tm, tn):
    M, K = a.shape
    N = w.shape[1]
    return pl.pallas_call(
        _mm_res_body,
        grid=(M // tm, N // tn),
        in_specs=[pl.BlockSpec((tm, K), lambda i, j: (i, 0)),
                  pl.BlockSpec((K, tn), lambda i, j: (0, j)),
                  pl.BlockSpec((tm, tn), lambda i, j: (i, j))],
        out_specs=pl.BlockSpec((tm, tn), lambda i, j: (i, j)),
        out_shape=jax.ShapeDtypeStruct((M, N), F32),
        compiler_params=_cparams("parallel", "arbitrary"),
        name="matmul_residual",
    )(a, w, res)


def _mm2_res_body(a_ref, b_ref, wa_ref, wb_ref, r_ref, o_ref):
    acc = jnp.dot(a_ref[...], wa_ref[...].astype(BF16), preferred_element_type=F32)
    acc = acc + jnp.dot(b_ref[...], wb_ref[...].astype(BF16), preferred_element_type=F32)
    o_ref[...] = r_ref[...] + acc


def concat_matmul_residual(a, b, w, res, *, tm, tn):
    M, Ka = a.shape
    Kb = b.shape[1]
    assert Ka == Kb and w.shape[0] == Ka + Kb
    N = w.shape[1]
    return pl.pallas_call(
        _mm2_res_body,
        grid=(M // tm, N // tn),
        in_specs=[pl.BlockSpec((tm, Ka), lambda i, j: (i, 0)),
                  pl.BlockSpec((tm, Kb), lambda i, j: (i, 0)),
                  pl.BlockSpec((Ka, tn), lambda i, j: (0, j)),
                  pl.BlockSpec((Kb, tn), lambda i, j: (1, j)),
                  pl.BlockSpec((tm, tn), lambda i, j: (i, j))],
        out_specs=pl.BlockSpec((tm, tn), lambda i, j: (i, j)),
        out_shape=jax.ShapeDtypeStruct((M, N), F32),
        compiler_params=_cparams("parallel", "arbitrary"),
        name="out_proj",
    )(a, b, w, w, res)


AUG_WIDTH = LANES
V_EXTRA = 16
SLOPE_PIECES = 4
LOG2E = float(np.log2(np.e))


def _bf16_pieces(x, n):
    out, rem = [], np.asarray(x, np.float64)
    for _ in range(n):
        piece = rem.astype(ml_dtypes.bfloat16).astype(np.float64)
        out.append(piece.astype(np.float32))
        rem = rem - piece
    return out


def _position_lanes(pos_hi, pos_lo, onehot, n_blk, shape):
    lane = lax.broadcasted_iota(jnp.int32, shape, 1)
    p = SLOPE_PIECES
    extra = jnp.where((lane >= n_blk) & (lane < n_blk + p), pos_hi,
                      jnp.where((lane >= n_blk + p) & (lane < n_blk + 2 * p), pos_lo, 0.0))
    if onehot is not None:
        extra = jnp.where(lane < n_blk, onehot, extra)
    return extra.astype(BF16)


def _kv_prep_body(c_ref, s_ref, w_ref, kns_ref, knw_ref,
                  kc_o, vc_o, ks_o, vsT_o, kw_o, vwT_o, slab_ref, *, tt, n_blk):
    dh = HEAD_DIM
    shape = (tt, AUG_WIDTH)
    pos = pl.program_id(1) * tt + lax.broadcasted_iota(jnp.int32, shape, 0)
    lane = lax.broadcasted_iota(jnp.int32, shape, 1)
    blk = jnp.right_shift(pos, SEL_BLK.bit_length() - 1)
    pos_hi = (blk * SEL_BLK).astype(F32)
    pos_lo = (pos & (SEL_BLK - 1)).astype(F32)
    aug_sel = _position_lanes(pos_hi, pos_lo, (blk == lane).astype(F32), n_blk, shape)
    aug_win = _position_lanes(pos_hi, pos_lo, None, n_blk, shape)
    for g in range(NSA_KV_HEADS):
        k_cols = slice(g * dh, (g + 1) * dh)
        v_cols = slice(KV_WIDTH + g * dh, KV_WIDTH + (g + 1) * dh)
        for slab, (cols, out) in enumerate(((k_cols, kc_o), (v_cols, vc_o))):
            slab_ref[slab] = c_ref[:, cols]
            for i in range(CMP_STRIDE):
                rows = pl.ds(i, tt // CMP_STRIDE, stride=CMP_STRIDE)
                out[0, g, :, i * dh:(i + 1) * dh] = slab_ref[slab, rows, :]
        ks_o[0, g, :, 0:dh] = _rms(s_ref[:, k_cols], kns_ref[...]).astype(BF16)
        ks_o[0, g, :, dh:dh + AUG_WIDTH] = aug_sel
        kw_o[0, g, :, 0:dh] = _rms(w_ref[:, k_cols], knw_ref[...]).astype(BF16)
        kw_o[0, g, :, dh:dh + AUG_WIDTH] = aug_win
        for v_ref, out, chunk in ((s_ref, vsT_o, SEL_CHUNK), (w_ref, vwT_o, WIN_CHUNK)):
            vT = v_ref[:, v_cols].T.astype(BF16)
            ones_first = (lax.broadcasted_iota(jnp.int32, (V_EXTRA, chunk), 0) == 0).astype(BF16)
            for c in range(tt // chunk):
                out[0, g, c, 0:dh, :] = vT[:, c * chunk:(c + 1) * chunk]
                out[0, g, c, dh:dh + V_EXTRA, :] = ones_first


def kv_prep(z, kn_slc, kn_win, *, B, T, col0, tt):
    G, dh = NSA_KV_HEADS, HEAD_DIM
    nt = T // tt
    n_blk = T // SEL_BLK
    assert n_blk + 2 * SLOPE_PIECES <= AUG_WIDTH
    cb = col0 // (2 * KV_WIDTH)
    row = lambda b, t: b * nt + t
    kdim = dh + AUG_WIDTH
    return pl.pallas_call(
        functools.partial(_kv_prep_body, tt=tt, n_blk=n_blk),
        grid=(B, nt),
        in_specs=[pl.BlockSpec((tt, 2 * KV_WIDTH), lambda b, t: (row(b, t), cb)),
                  pl.BlockSpec((tt, 2 * KV_WIDTH), lambda b, t: (row(b, t), cb + 1)),
                  pl.BlockSpec((tt, 2 * KV_WIDTH), lambda b, t: (row(b, t), cb + 2)),
                  pl.BlockSpec((1, dh), lambda b, t: (0, 0)),
                  pl.BlockSpec((1, dh), lambda b, t: (0, 0))],
        out_specs=[pl.BlockSpec((1, G, tt // CMP_STRIDE, CMP_STRIDE * dh), lambda b, t: (b, 0, t, 0)),
                   pl.BlockSpec((1, G, tt // CMP_STRIDE, CMP_STRIDE * dh), lambda b, t: (b, 0, t, 0)),
                   pl.BlockSpec((1, G, tt, kdim), lambda b, t: (b, 0, t, 0)),
                   pl.BlockSpec((1, G, tt // SEL_CHUNK, dh + V_EXTRA, SEL_CHUNK), lambda b, t: (b, 0, t, 0, 0)),
                   pl.BlockSpec((1, G, tt, kdim), lambda b, t: (b, 0, t, 0)),
                   pl.BlockSpec((1, G, tt // WIN_CHUNK, dh + V_EXTRA, WIN_CHUNK), lambda b, t: (b, 0, t, 0, 0))],
        out_shape=[jax.ShapeDtypeStruct((B, G, T // CMP_STRIDE, CMP_STRIDE * dh), F32),
                   jax.ShapeDtypeStruct((B, G, T // CMP_STRIDE, CMP_STRIDE * dh), F32),
                   jax.ShapeDtypeStruct((B, G, T, kdim), BF16),
                   jax.ShapeDtypeStruct((B, G, T // SEL_CHUNK, dh + V_EXTRA, SEL_CHUNK), BF16),
                   jax.ShapeDtypeStruct((B, G, T, kdim), BF16),
                   jax.ShapeDtypeStruct((B, G, T // WIN_CHUNK, dh + V_EXTRA, WIN_CHUNK), BF16)],
        scratch_shapes=[pltpu.VMEM((2, tt, dh), F32)],
        compiler_params=_cparams("parallel", "parallel"),
        name="kv_prep",
    )(z, z, z, kn_slc, kn_win)


def _compress_body(xk_ref, xv_ref, pk_ref, pv_ref, w1k_ref, w2k_ref, w1v_ref, w2v_ref, kn_ref,
                   kc_o, vcT_o, *, n_blk):
    dh = HEAD_DIM

    def phi(x_ref, p_ref, w1_ref, w2_ref):
        x = x_ref[0, 0]
        first = jnp.dot((x + p_ref[0:1, :]).astype(BF16), w1_ref[0], preferred_element_type=F32)
        second = jnp.dot((x + p_ref[1:2, :]).astype(BF16), w1_ref[1], preferred_element_type=F32)
        n = second.shape[0]
        h = first + pltpu.roll(second, n - 1, 0)
        return jnp.dot(_silu(h).astype(BF16), w2_ref[...], preferred_element_type=F32)

    k = _rms(phi(xk_ref, pk_ref, w1k_ref, w2k_ref), kn_ref[...])
    nc = k.shape[0]
    shape = (nc, AUG_WIDTH)
    start = (lax.broadcasted_iota(jnp.int32, shape, 0) * CMP_STRIDE).astype(F32)
    centre = jnp.full(shape, (CMP_LEN - 1) / 2.0, F32)
    kc_o[0, 0, :, 0:dh] = k.astype(BF16)
    kc_o[0, 0, :, dh:dh + AUG_WIDTH] = _position_lanes(start, centre, None, n_blk, shape)
    vcT_o[0, 0] = phi(xv_ref, pv_ref, w1v_ref, w2v_ref).T.astype(BF16)


def compress_kv(xk, xv, pos_k, pos_v, w1k, w2k, w1v, w2v, kn_cmp, *, n_blk):
    B, G, nc, half = xk.shape
    dh = half // CMP_STRIDE
    assert CMP_LEN == 2 * CMP_STRIDE
    x_spec = pl.BlockSpec((1, 1, nc, half), lambda b, g: (b, g, 0, 0))
    full = lambda shape: pl.BlockSpec(shape, lambda b, g: (0,) * len(shape))
    return pl.pallas_call(
        functools.partial(_compress_body, n_blk=n_blk),
        grid=(B, G),
        in_specs=[x_spec, x_spec, full((2, half)), full((2, half)),
                  full((2, half, dh)), full((dh, dh)), full((2, half, dh)), full((dh, dh)),
                  full((1, dh))],
        out_specs=[pl.BlockSpec((1, 1, nc, dh + AUG_WIDTH), lambda b, g: (b, g, 0, 0)),
                   pl.BlockSpec((1, 1, dh, nc), lambda b, g: (b, g, 0, 0))],
        out_shape=[jax.ShapeDtypeStruct((B, G, nc, dh + AUG_WIDTH), BF16),
                   jax.ShapeDtypeStruct((B, G, dh, nc), BF16)],
        compiler_params=_cparams("parallel", "parallel"),
        name="compress_kv",
    )(xk, xv, pos_k.reshape(2, half), pos_v.reshape(2, half),
      w1k.reshape(2, half, dh), w2k, w1v.reshape(2, half, dh), w2v, kn_cmp)


def _nsa_body(q_ref, gl_ref, gb_ref, qn_ref, lanes_ref, kc_ref, vcT_ref,
              ks_ref, vsT_ref, kw_ref, vwT_ref, ovT_ref, kmq_ref, o_ref,
              score_ref, gate_ref, qc_ref, qs_ref, m_ref, acc_ref, part_ref, sbuf_ref, *, n_tiles):
    dh, hg, tq = HEAD_DIM, NSA_GROUP, Q_TILE
    cols = hg * tq
    qi = pl.program_id(2)
    t0 = qi * tq
    contract_last = (((1,), (1,)), ((), ()))
    n_blk = ovT_ref.shape[0]

    scale = dh ** -0.5 * LOG2E
    for h in range(hg):
        q_h = (_rms(q_ref[:, h * dh:(h + 1) * dh], qn_ref[...]) * scale).astype(BF16)
        qc_ref[h * tq:(h + 1) * tq, 0:dh] = q_h
        qs_ref[h * tq:(h + 1) * tq, 0:dh] = q_h

    def set_extra_lanes(queries_ref, aug_heads):
        for h in range(hg):
            queries_ref[h * tq:(h + 1) * tq, dh:dh + AUG_WIDTH] = aug_heads[h]

    hc = kmq_ref.shape[1]
    assert cols % hc == 0 and hc % tq == 0

    def reset(b):
        m_ref[b] = jnp.full((1, cols), NEG_INF, F32)
        acc_ref[b] = jnp.zeros((dh + V_EXTRA, cols), F32)

    groups = [slice(i * hc, (i + 1) * hc) for i in range(cols // hc)]

    def issue_scores(queries_ref, parts):
        unit = 0
        for k, _, _, _ in parts:
            for cs in groups:
                sbuf_ref[unit, 0:k.shape[0], :] = lax.dot_general(
                    k, queries_ref[cs, :], contract_last, preferred_element_type=F32)
                unit += 1

    def softmax_update(b, parts):
        unit = 0
        for (k, v_pieces, mask, bias) in parts:
            for cs in groups:
                s_ref = sbuf_ref.at[unit, 0:k.shape[0], :]
                unit += 1
                if mask is not None:
                    s_ref[...] = jnp.where(mask(), s_ref[...], NEG_INF)
                if bias is not None:
                    s_ref[...] = s_ref[...] + bias
                m_old = m_ref[b, :, cs]
                m_new = jnp.maximum(m_old, jnp.max(s_ref[...], axis=0, keepdims=True))
                alpha = jnp.exp2(m_old - m_new)
                p = jnp.exp2(s_ref[...] - m_new).astype(BF16)
                m_ref[b, :, cs] = m_new
                pv = None
                for i, vT in enumerate(v_pieces):
                    n = vT.shape[1]
                    d = jnp.dot(vT, p[i * n:(i + 1) * n, :], preferred_element_type=F32)
                    pv = d if pv is None else pv + d
                acc_ref[b, :, cs] = alpha * acc_ref[b, :, cs] + pv

    def attend(b, queries_ref, parts):
        issue_scores(queries_ref, parts)
        softmax_update(b, parts)

    def result(b):
        return acc_ref[b, 0:dh, :] * (1.0 / acc_ref[b, dh:dh + 1, :])

    nc = kc_ref.shape[2]
    set_extra_lanes(qc_ref, [jnp.broadcast_to(lanes_ref[0, hg + h:hg + h + 1, :], (tq, AUG_WIDTH)).astype(BF16)
                             for h in range(hg)])
    s = lax.dot_general(kc_ref[0, 0], qc_ref[...], contract_last, preferred_element_type=F32)

    cw = WIN_CHUNK
    assert cw == tq and WINDOW % cw == 0 and cw == kmq_ref.shape[0]
    n_back = WINDOW // cw
    WIN, SEL = 0, 1

    def win_part(c, mask=None, bias=None):
        k = kw_ref[0, 0, pl.ds(pl.multiple_of(c * cw, cw), cw), :]
        return k, [vwT_ref[0, 0, c]], mask, bias

    threshold = jnp.where(qi >= n_back, 0, cw)
    win_parts = ([win_part(qi, mask=lambda: kmq_ref[...] <= 0)]
                 + [win_part(jnp.maximum(qi - j, 0), bias=jnp.where(qi >= j, 0.0, NEG_INF))
                    for j in range(1, n_back)]
                 + [win_part(jnp.maximum(qi - n_back, 0), mask=lambda: kmq_ref[...] > threshold)])
    issue_scores(qc_ref, win_parts)

    c_idx = lax.broadcasted_iota(jnp.int32, (nc, cols), 0)
    t_c = t0 + (lax.broadcasted_iota(jnp.int32, (nc, cols), 1) & (tq - 1))
    mask = (c_idx * CMP_STRIDE + (CMP_LEN - 1)) <= t_c
    s = jnp.where(mask, s, NEG_INF)
    m_c = jnp.max(s, axis=0, keepdims=True)
    p_c = jnp.where(mask, jnp.exp2(s - m_c), 0.0)
    l_c = jnp.sum(p_c, axis=0, keepdims=True)
    p_c = p_c * jnp.where(l_c > 0.0, 1.0 / l_c, 0.0)
    o_cmp = jnp.dot(vcT_ref[0, 0], p_c.astype(BF16), preferred_element_type=F32)

    p_sum = p_c[:, 0:tq]
    for h in range(1, hg):
        p_sum = p_sum + p_c[:, h * tq:(h + 1) * tq]
    hi = p_sum.astype(BF16)
    rem = p_sum - hi.astype(F32)
    mid = rem.astype(BF16)
    lo = (rem - mid.astype(F32)).astype(BF16)
    ovT = ovT_ref[...]
    imp = (jnp.dot(ovT, hi, preferred_element_type=F32)
           + jnp.dot(ovT, mid, preferred_element_type=F32)
           + jnp.dot(ovT, lo, preferred_element_type=F32))

    reset(WIN)
    softmax_update(WIN, win_parts)

    blk = lax.broadcasted_iota(jnp.int32, (n_blk, tq), 0)
    t_b = t0 + lax.broadcasted_iota(jnp.int32, (n_blk, tq), 1)
    cur = jnp.right_shift(t_b, SEL_BLK.bit_length() - 1)
    valid = blk * SEL_BLK <= t_b
    forced = (blk == 0) | (blk == cur) | (blk == cur - 1)
    score = jnp.where(valid, imp + FORCE_BONUS * forced.astype(F32), NEG_INF)
    score_ref[...] = score
    rank = jnp.zeros((n_blk, tq), F32)
    for j in range(n_blk):
        sj = score_ref[j:j + 1, :]
        beats = (sj > score) | ((sj == score) & (blk > j))
        rank = rank + beats.astype(F32)
    block_bias = jnp.where(rank < float(min(N_SEL, n_blk)), 0.0, NEG_INF)

    assert 2 * SLOPE_PIECES <= 8
    ones_rows = (lax.broadcasted_iota(jnp.int32, (8, tq), 0) < 2 * SLOPE_PIECES).astype(F32)
    extra_t = jnp.concatenate([block_bias, ones_rows,
                               jnp.zeros((AUG_WIDTH - n_blk - 8, tq), F32)], axis=0).T
    set_extra_lanes(qs_ref, [(extra_t * lanes_ref[0, h:h + 1, :]).astype(BF16) for h in range(hg)])

    gate_ref[...] = jax.nn.sigmoid(gl_ref[...] + gb_ref[...]).T

    def gate(h, branch):
        r = h * N_BRANCH + branch
        return gate_ref[r:r + 1, :]

    o_win = result(WIN)
    for h in range(hg):
        cs = slice(h * tq, (h + 1) * tq)
        part_ref[:, cs] = gate(h, 0) * o_cmp[:, cs] + gate(h, 2) * o_win[:, cs]

    ck = SEL_CHUNK
    assert ck == tq and ck == kmq_ref.shape[0]

    def sel_part(c, n=1, mask=None):
        k = ks_ref[0, 0, c * ck:(c + n) * ck, :]
        return k, [vsT_ref[0, 0, c + i] for i in range(n)], mask, None

    reset(SEL)

    for tile in range(n_tiles):
        @pl.when(qi == tile)
        def _(tile=tile):
            parts = [sel_part(tile, mask=lambda: kmq_ref[...] <= 0)]
            parts += [sel_part(c, n=min(2, tile - c)) for c in range(0, tile, 2)]
            attend(SEL, qs_ref, parts)

    o_slc = result(SEL)
    for h in range(hg):
        cs = slice(h * tq, (h + 1) * tq)
        o_h = part_ref[:, cs] + gate(h, 1) * o_slc[:, cs]
        o_ref[:, h * dh:(h + 1) * dh] = o_h.T.astype(o_ref.dtype)


def nsa_attention(z, gl, gate_b, q_norm, kcmp, vcmpT, ks, vsT, kw, vwT, *, B, T):
    G, hg, dh, tq = NSA_KV_HEADS, NSA_GROUP, HEAD_DIM, Q_TILE
    nq = T // tq
    nc = kcmp.shape[2]
    n_blk = T // SEL_BLK
    kdim = dh + AUG_WIDTH
    cols = hg * tq
    assert SEL_CHUNK == WIN_CHUNK == tq
    n_units = max(WINDOW // WIN_CHUNK + 1, 1 + nq // 2) * (cols // NSA_COLS_PER_DOT)
    assert SEL_CHUNK % SEL_BLK == 0 and T % SEL_CHUNK == 0 and gl.shape[1] == G * LANES
    assert SEL_CHUNK % tq == 0 and n_blk % 8 == 0 and n_blk + 8 <= AUG_WIDTH

    slopes = np.exp2(-8.0 * (np.arange(NSA_HEADS, dtype=np.float64) + 1.0) / NSA_HEADS).reshape(G, hg)
    lanes = np.zeros((G, 2 * hg, AUG_WIDTH), np.float32)
    lanes[:, :hg, :n_blk] = 1.0
    for i, piece in enumerate(_bf16_pieces(slopes * LOG2E, SLOPE_PIECES)):
        for lane in (n_blk + i, n_blk + SLOPE_PIECES + i):
            lanes[:, :hg, lane] = piece
            lanes[:, hg:, lane] = piece
    cs = np.arange(nc)[:, None] * CMP_STRIDE
    bs = np.arange(n_blk)[None, :] * SEL_BLK
    overlap = np.clip(np.minimum(cs + CMP_LEN, bs + SEL_BLK) - np.maximum(cs, bs), 0, None) / CMP_LEN
    overlap[(T - CMP_LEN) // CMP_STRIDE + 1:] = 0.0
    ovT = jnp.asarray(overlap.T, dtype=BF16)
    hc = NSA_COLS_PER_DOT
    kmq = jnp.asarray(np.arange(SEL_CHUNK)[:, None] - (np.arange(hc)[None, :] % tq), dtype=jnp.int32)

    row = lambda b, qi: b * nq + qi
    per_bg = lambda shape: pl.BlockSpec((1, 1) + shape, lambda b, g, qi: (b, g) + (0,) * len(shape))
    const = lambda shape: pl.BlockSpec(shape, lambda b, g, qi: (0,) * len(shape))
    return pl.pallas_call(
        functools.partial(_nsa_body, n_tiles=nq),
        grid=(B, G, nq),
        in_specs=[pl.BlockSpec((tq, hg * dh), lambda b, g, qi: (row(b, qi), g)),
                  pl.BlockSpec((tq, LANES), lambda b, g, qi: (row(b, qi), g)),
                  pl.BlockSpec((1, LANES), lambda b, g, qi: (0, g)),
                  const((1, dh)),
                  pl.BlockSpec((1, 2 * hg, AUG_WIDTH), lambda b, g, qi: (g, 0, 0)),
                  per_bg((nc, kdim)), per_bg((dh, nc)),
                  per_bg((T, kdim)), per_bg((T // SEL_CHUNK, dh + V_EXTRA, SEL_CHUNK)),
                  per_bg((T, kdim)), per_bg((T // WIN_CHUNK, dh + V_EXTRA, WIN_CHUNK)),
                  const((n_blk, nc)), const((SEL_CHUNK, hc))],
        out_specs=pl.BlockSpec((tq, hg * dh), lambda b, g, qi: (row(b, qi), g)),
        out_shape=jax.ShapeDtypeStruct((B * T, NSA_WIDTH), BF16),
        scratch_shapes=[pltpu.VMEM((n_blk, tq), F32), pltpu.VMEM((LANES, tq), F32),
                        pltpu.VMEM((cols, kdim), BF16), pltpu.VMEM((cols, kdim), BF16),
                        pltpu.VMEM((2, 1, cols), F32),
                        pltpu.VMEM((2, dh + V_EXTRA, cols), F32), pltpu.VMEM((dh, cols), F32),
                        pltpu.VMEM((n_units, 2 * SEL_CHUNK, NSA_COLS_PER_DOT), F32)],
        compiler_params=_cparams("parallel", "parallel", "arbitrary"),
        name="nsa_attention",
    )(z, gl, gate_b, q_norm, jnp.asarray(lanes), kcmp, vcmpT, ks, vsT, kw, vwT, ovT, kmq)


def _conv_body(a_ref, b_ref, ah_ref, bh_ref, w_ref, cb_ref, lg_ref, lb_ref, o_ref,
               h_ref, g_ref, y_ref, *, tt):
    C = a_ref.shape[1]
    halo = ah_ref[...] * jax.nn.sigmoid(bh_ref[...])
    h_ref[0:CONV_HALO, :] = jnp.where(pl.program_id(1) == 0, 0.0, halo)
    h_ref[CONV_HALO:CONV_HALO + tt, :] = a_ref[...] * jax.nn.sigmoid(b_ref[...])
    h_ref[CONV_HALO + tt:CONV_HALO + tt + SUBLANES, :] = jnp.zeros((SUBLANES, C), F32)
    lead = CONV_HALO - (CONV_KERNEL - 1)
    rows = tt + SUBLANES
    for s in range(C // LANES):
        cs = slice(s * LANES, (s + 1) * LANES)
        acc = None
        for r in range(SUBLANES):
            group = None
            for o in range(lead, lead + CONV_KERNEL):
                if o % SUBLANES == r:
                    term = w_ref[o - lead:o - lead + 1, cs] * h_ref[o - r:o - r + rows, cs]
                    group = term if group is None else group + term
            if r == 0:
                piece = group[0:tt, :]
            else:
                g_ref[r] = group
                piece = g_ref[r, r:r + tt, :]
            acc = piece if acc is None else acc + piece
        y_ref[:, cs] = acc + cb_ref[:, cs]
    y = y_ref[...]
    mu = jnp.mean(y, axis=-1, keepdims=True)
    var = jnp.mean(jnp.square(y - mu), axis=-1, keepdims=True)
    yn = (y - mu) * lax.rsqrt(var + LN_EPS) * lg_ref[...] + lb_ref[...]
    o_ref[...] = _silu(yn).astype(o_ref.dtype)


def conformer_conv(z, conv_w, conv_b, ln_g, ln_b, *, B, T, col0, C, tt):
    nt = T // tt
    ca = col0 // C
    hb = tt // CONV_HALO
    row = lambda b, t: b * nt + t
    halo_row = lambda b, t: jnp.maximum(row(b, t) * hb - 1, 0)
    const = lambda shape: pl.BlockSpec(shape, lambda b, t: (0,) * len(shape))
    return pl.pallas_call(
        functools.partial(_conv_body, tt=tt),
        grid=(B, nt),
        in_specs=[pl.BlockSpec((tt, C), lambda b, t: (row(b, t), ca)),
                  pl.BlockSpec((tt, C), lambda b, t: (row(b, t), ca + 1)),
                  pl.BlockSpec((CONV_HALO, C), lambda b, t: (halo_row(b, t), ca)),
                  pl.BlockSpec((CONV_HALO, C), lambda b, t: (halo_row(b, t), ca + 1)),
                  const((CONV_KERNEL, C)), const((1, C)), const((1, C)), const((1, C))],
        out_specs=pl.BlockSpec((tt, C), lambda b, t: (row(b, t), 0)),
        out_shape=jax.ShapeDtypeStruct((B * T, C), BF16),
        scratch_shapes=[pltpu.VMEM((CONV_HALO + tt + SUBLANES, C), F32),
                        pltpu.VMEM((SUBLANES, tt + SUBLANES, LANES), F32),
                        pltpu.VMEM((tt, C), F32)],
        compiler_params=_cparams("parallel", "parallel"),
        name="conformer_conv",
    )(z, z, z, z, conv_w, conv_b, ln_g, ln_b)


def _mem_attn_body(x_ref, gq_ref, wq_ref, mem_ref, gkv_ref, wkv_ref, qn_ref, kn_ref, wo_ref, gn_ref,
                   o_ref, on_ref, k_scr, v_scr):
    dh = HEAD_DIM
    width = MEM_HEADS * dh
    scale = dh ** -0.5
    heads = [slice(h * dh, (h + 1) * dh) for h in range(MEM_HEADS)]

    @pl.when(pl.program_id(1) == 0)
    def _():
        kv = jnp.dot(_rms(mem_ref[...], gkv_ref[...]).astype(BF16), wkv_ref[...], preferred_element_type=F32)
        for cs in heads:
            k_scr[:, cs] = _rms(kv[:, cs], kn_ref[...]).astype(BF16)
        v_scr[...] = kv[:, width:2 * width].astype(BF16)

    q_all = jnp.dot(_rms(x_ref[...], gq_ref[...]).astype(BF16), wq_ref[...], preferred_element_type=F32)
    scores = []
    for cs in heads:
        q = (_rms(q_all[:, cs], qn_ref[...]) * scale).astype(BF16)
        scores.append(lax.dot_general(q, k_scr[:, cs], (((1,), (1,)), ((), ())), preferred_element_type=F32))
    outs = []
    for h, s in enumerate(scores):
        v = v_scr[:, heads[h]]
        p = jnp.exp(s - jnp.max(s, axis=-1, keepdims=True))
        inv = 1.0 / jnp.sum(p, axis=-1, keepdims=True)
        outs.append((jnp.dot(p.astype(BF16), v, preferred_element_type=F32) * inv).astype(BF16))
    o = jnp.concatenate(outs, axis=1)
    y = x_ref[...] + jnp.dot(o, wo_ref[...], preferred_element_type=F32)
    o_ref[...] = y
    on_ref[...] = _rms(y, gn_ref[...]).astype(on_ref.dtype)


def mem_attention(x, q_gain, w_mq, mem, kv_gain, w_mkv, mq_norm, mk_norm, w_mo, next_norm, *, B, T, M, tt):
    D = x.shape[1]
    width = MEM_HEADS * HEAD_DIM
    nt = T // tt
    row = lambda b, t: b * nt + t
    const = lambda shape: pl.BlockSpec(shape, lambda b, t: (0,) * len(shape))
    return pl.pallas_call(
        _mem_attn_body,
        grid=(B, nt),
        in_specs=[pl.BlockSpec((tt, D), lambda b, t: (row(b, t), 0)),
                  const((1, D)), const((D, width)),
                  pl.BlockSpec((M, D), lambda b, t: (b, 0)),
                  const((1, D)), const((D, 2 * width)),
                  const((1, HEAD_DIM)), const((1, HEAD_DIM)), const((width, D)),
                  const((1, D))],
        out_specs=[pl.BlockSpec((tt, D), lambda b, t: (row(b, t), 0)),
                   pl.BlockSpec((tt, D), lambda b, t: (row(b, t), 0))],
        out_shape=[jax.ShapeDtypeStruct((B * T, D), F32),
                   jax.ShapeDtypeStruct((B * T, D), BF16)],
        scratch_shapes=[pltpu.VMEM((M, width), BF16), pltpu.VMEM((M, width), BF16)],
        compiler_params=_cparams("parallel", "arbitrary"),
        name="mem_attention",
    )(x, q_gain, w_mq, mem, kv_gain, w_mkv, mq_norm, mk_norm, w_mo, next_norm)


def kernel(x, mem, norm_mix, w_in, gate_b, q_norm, k_norm_cmp, k_norm_slc, k_norm_win, cmp_pos_k, cmp_pos_v, cmp_k_w1, cmp_k_w2, cmp_v_w1, cmp_v_w2, conv_w, conv_b, conv_ln_g, conv_ln_b, w_out, norm_mem_q, norm_mem_kv, w_mq, w_mk, w_mv, mq_norm, mk_norm, w_mo, norm_ffn, w_gate, w_up, w_down):
    B, T, D = x.shape
    M = mem.shape[1]
    depth = w_in.shape[0]
    G, hg = NSA_KV_HEADS, NSA_GROUP
    C = D - NSA_WIDTH
    n_gate = N_BRANCH * NSA_HEADS
    kv_all = 6 * KV_WIDTH
    assert w_in.shape[2] == NSA_WIDTH + kv_all + n_gate + 2 * C
    assert C == NSA_WIDTH, "column blocking below assumes equal NSA and conv widths"

    row1 = lambda v: v.reshape(1, -1)
    xf = x.reshape(B * T, D)
    memf = mem.reshape(B * M, D)
    u0 = NSA_WIDTH + kv_all + n_gate
    for l in range(depth):
        w = w_in[l]
        w_t = jnp.swapaxes(w, 0, 1)
        wg = w[:, NSA_WIDTH + kv_all:u0].reshape(D, G, hg * N_BRANCH)
        wg = jnp.pad(wg, ((0, 0), (0, 0), (0, LANES - hg * N_BRANCH))).reshape(D, G * LANES)
        gb = jnp.pad(gate_b[l].reshape(G, hg * N_BRANCH), ((0, 0), (0, LANES - hg * N_BRANCH))).reshape(1, G * LANES)

        xn = rmsnorm_bf16(xf, row1(norm_mix[l]), tm=TILE_RMSNORM)
        z, gl = in_projection(xn, w_t, wg, tm=TILE_IN_PROJ[0], tn=TILE_IN_PROJ[1],
                              segments=((0, NSA_WIDTH), (u0, 2 * C), (NSA_WIDTH, kv_all)))
        kv_col0 = NSA_WIDTH + 2 * C
        kc, vc, ks, vsT, kw, vwT = kv_prep(z, row1(k_norm_slc[l]), row1(k_norm_win[l]),
                                           B=B, T=T, col0=kv_col0, tt=TILE_KV_PREP)
        kcmp, vcmpT = compress_kv(kc, vc, cmp_pos_k[l], cmp_pos_v[l],
                                  cmp_k_w1[l].astype(BF16), cmp_k_w2[l].astype(BF16),
                                  cmp_v_w1[l].astype(BF16), cmp_v_w2[l].astype(BF16), row1(k_norm_cmp[l]),
                                  n_blk=T // SEL_BLK)
        o_nsa = nsa_attention(z, gl, gb, row1(q_norm[l]), kcmp, vcmpT, ks, vsT, kw, vwT, B=B, T=T)
        o_conv = conformer_conv(z, conv_w[l], row1(conv_b[l]), row1(conv_ln_g[l]), row1(conv_ln_b[l]),
                                B=B, T=T, col0=NSA_WIDTH, C=C, tt=TILE_CONV)
        xf = concat_matmul_residual(o_nsa, o_conv, w_out[l], xf, tm=TILE_OUT_PROJ[0], tn=TILE_OUT_PROJ[1])

        w_mkv = jnp.concatenate([w_mk[l], w_mv[l]], axis=1).astype(BF16)
        xf, xn = mem_attention(xf, row1(norm_mem_q[l]), w_mq[l].astype(BF16),
                               memf, row1(norm_mem_kv[l]), w_mkv,
                               row1(mq_norm[l]), row1(mk_norm[l]), w_mo[l].astype(BF16),
                               row1(norm_ffn[l]), B=B, T=T, M=M, tt=TILE_MEM_ATTN)

        hidden, w_down_bf16 = swiglu_up(xn, w_gate[l], w_up[l], w_down[l],
                                        tm=TILE_FFN_UP[0], tn=TILE_FFN_UP[1])
        xf = matmul_residual(hidden, w_down_bf16, xf, tm=TILE_FFN_DOWN[0], tn=TILE_FFN_DOWN[1])
    return xf.reshape(B, T, D)
```

```python
import functools

import ml_dtypes
import numpy as np
import jax
import jax.numpy as jnp
from jax import lax
from jax.experimental import pallas as pl
from jax.experimental.pallas import tpu as pltpu

F32 = jnp.float32
BF16 = jnp.bfloat16

HEAD_DIM = 128
NSA_HEADS = 8
NSA_KV_HEADS = 2
NSA_GROUP = NSA_HEADS // NSA_KV_HEADS
NSA_WIDTH = NSA_HEADS * HEAD_DIM
KV_WIDTH = NSA_KV_HEADS * HEAD_DIM
N_BRANCH = 3
CONV_KERNEL = 31
CMP_LEN = 32
CMP_STRIDE = 16
SEL_BLK = 64
N_SEL = 16
WINDOW = 512
MEM_HEADS = 4
NEG_INF = -1e30
FORCE_BONUS = 1e3
RMS_EPS = 1e-6
LN_EPS = 1e-5

LANES = 128
SUBLANES = 8
VMEM_LIMIT_BYTES = 56 * 1024 * 1024

Q_TILE = 256
SEL_CHUNK = 256
WIN_CHUNK = 256
NSA_COLS_PER_DOT = 512
CONV_HALO = 32

TILE_RMSNORM = 512
TILE_IN_PROJ = (2048, 512)
TILE_KV_PREP = 1024
TILE_CONV = 256
TILE_OUT_PROJ = (2048, 512)
TILE_MEM_ATTN = 512
TILE_FFN_UP = (1024, 512)
TILE_FFN_DOWN = (1024, 512)


def _cparams(*sem):
    return pltpu.CompilerParams(dimension_semantics=sem, vmem_limit_bytes=VMEM_LIMIT_BYTES)


def _rms(x, g, eps=RMS_EPS):
    ms = jnp.mean(x * x, axis=-1, keepdims=True)
    return x * lax.rsqrt(ms + eps) * g


def _silu(x):
    return x * jax.nn.sigmoid(x)


def _rmsnorm_body(x_ref, g_ref, o_ref):
    o_ref[...] = _rms(x_ref[...], g_ref[...]).astype(o_ref.dtype)


def rmsnorm_bf16(x, g, *, tm):
    M, K = x.shape
    return pl.pallas_call(
        _rmsnorm_body,
        grid=(M // tm,),
        in_specs=[pl.BlockSpec((tm, K), lambda i: (i, 0)), pl.BlockSpec((1, K), lambda i: (0, 0))],
        out_specs=pl.BlockSpec((tm, K), lambda i: (i, 0)),
        out_shape=jax.ShapeDtypeStruct((M, K), BF16),
        compiler_params=_cparams("parallel"),
        name="rmsnorm",
    )(x, g)


def _in_proj_body(xn_ref, wt_ref, wg_ref, o_ref, og_ref):
    @pl.when(pl.program_id(1) == 0)
    def _():
        og_ref[...] = jnp.dot(xn_ref[...], wg_ref[...].astype(BF16), preferred_element_type=F32)

    o_ref[...] = lax.dot_general(xn_ref[...], wt_ref[...].astype(BF16), (((1,), (1,)), ((), ())),
                                 preferred_element_type=F32)


def _swiglu_body(x_ref, wg_ref, wu_ref, wd_ref, o_ref, wd_o):
    x = x_ref[...]
    a = jnp.dot(x, wg_ref[...].astype(BF16), preferred_element_type=F32)
    b = jnp.dot(x, wu_ref[...].astype(BF16), preferred_element_type=F32)
    o_ref[...] = (_silu(a) * b).astype(o_ref.dtype)

    @pl.when(pl.program_id(0) == 0)
    def _():
        wd_o[...] = wd_ref[...].astype(wd_o.dtype)


def in_projection(xn, w_t, wg, *, segments, tm, tn):
    M, K = xn.shape
    NG = wg.shape[1]
    assert all(start % SUBLANES == 0 and size % tn == 0 for start, size in segments)
    N = sum(size for _, size in segments)

    def row_start(i, j):
        group, first = 0, 0
        for seg_start, size in segments:
            group = jnp.where(j >= first, seg_start // SUBLANES + (j - first) * (tn // SUBLANES), group)
            first += size // tn
        return (group * SUBLANES, 0)

    return pl.pallas_call(
        _in_proj_body,
        grid=(M // tm, N // tn),
        in_specs=[pl.BlockSpec((tm, K), lambda i, j: (i, 0)),
                  pl.BlockSpec((pl.Element(tn), pl.Element(K)), row_start),
                  pl.BlockSpec((K, NG), lambda i, j: (0, 0))],
        out_specs=[pl.BlockSpec((tm, tn), lambda i, j: (i, j)),
                   pl.BlockSpec((tm, NG), lambda i, j: (i, 0))],
        out_shape=[jax.ShapeDtypeStruct((M, N), F32),
                   jax.ShapeDtypeStruct((M, NG), F32)],
        compiler_params=_cparams("parallel", "arbitrary"),
        name="in_proj",
    )(xn, w_t, wg)


def swiglu_up(xn, w_gate, w_up, w_down, *, tm, tn):
    M, K = xn.shape
    N = w_gate.shape[1]
    D = w_down.shape[1]
    nj = N // tn
    assert w_down.shape[0] == N
    down_block = lambda i, j: (jnp.where(i == 0, j, nj - 1), 0)
    return pl.pallas_call(
        _swiglu_body,
        grid=(M // tm, nj),
        in_specs=[pl.BlockSpec((tm, K), lambda i, j: (i, 0)),
                  pl.BlockSpec((K, tn), lambda i, j: (0, j)),
                  pl.BlockSpec((K, tn), lambda i, j: (0, j)),
                  pl.BlockSpec((tn, D), down_block)],
        out_specs=[pl.BlockSpec((tm, tn), lambda i, j: (i, j)),
                   pl.BlockSpec((tn, D), down_block)],
        out_shape=[jax.ShapeDtypeStruct((M, N), BF16),
                   jax.ShapeDtypeStruct((N, D), BF16)],
        compiler_params=_cparams("arbitrary", "arbitrary"),
        name="ffn_up",
    )(xn, w_gate, w_up, w_down)


def _mm_res_body(a_ref, w_ref, r_ref, o_ref):
    o_ref[...] = r_ref[...] + jnp.dot(a_ref[...], w_ref[...], preferred_element_type=F32)


def matmul_residual(a, w, res, *, tm, tn):
    M, K = a.shape
    N = w.shape[1]
    return pl.pallas_call(
        _mm_res_body,
        grid=(M // tm, N // tn),
        in_specs=[pl.BlockSpec((tm, K), lambda i, j: (i, 0)),
                  pl.BlockSpec((K, tn), lambda i, j: (0, j)),
                  pl.BlockSpec((tm, tn), lambda i, j: (i, j))],
        out_specs=pl.BlockSpec((tm, tn), lambda i, j: (i, j)),
        out_shape=jax.ShapeDtypeStruct((M, N), F32),
        compiler_params=_cparams("parallel", "arbitrary"),
        name="matmul_residual",
    )(a, w, res)


def _mm2_res_body(a_ref, b_ref, wa_ref, wb_ref, r_ref, o_ref):
    acc = jnp.dot(a_ref[...], wa_ref[...].astype(BF16), preferred_element_type=F32)
    acc = acc + jnp.dot(b_ref[...], wb_ref[...].astype(BF16), preferred_element_type=F32)
    o_ref[...] = r_ref[...] + acc


def concat_matmul_residual(a, b, w, res, *, tm, tn):
    M, Ka = a.shape
    Kb = b.shape[1]
    assert Ka == Kb and w.shape[0] == Ka + Kb
    N = w.shape[1]
    return pl.pallas_call(
        _mm2_res_body,
        grid=(M // tm, N // tn),
        in_specs=[pl.BlockSpec((tm, Ka), lambda i, j: (i, 0)),
                  pl.BlockSpec((tm, Kb), lambda i, j: (i, 0)),
                  pl.BlockSpec((Ka, tn), lambda i, j: (0, j)),
                  pl.BlockSpec((Kb, tn), lambda i, j: (1, j)),
                  pl.BlockSpec((tm, tn), lambda i, j: (i, j))],
        out_specs=pl.BlockSpec((tm, tn), lambda i, j: (i, j)),
        out_shape=jax.ShapeDtypeStruct((M, N), F32),
        compiler_params=_cparams("parallel", "arbitrary"),
        name="out_proj",
    )(a, b, w, w, res)


AUG_WIDTH = LANES
V_EXTRA = 16
SLOPE_PIECES = 4
LOG2E = float(np.log2(np.e))


def _bf16_pieces(x, n):
    out, rem = [], np.asarray(x, np.float64)
    for _ in range(n):
        piece = rem.astype(ml_dtypes.bfloat16).astype(np.float64)
        out.append(piece.astype(np.float32))
        rem = rem - piece
    return out


def _position_lanes(pos_hi, pos_lo, onehot, n_blk, shape):
    lane = lax.broadcasted_iota(jnp.int32, shape, 1)
    p = SLOPE_PIECES
    extra = jnp.where((lane >= n_blk) & (lane < n_blk + p), pos_hi,
                      jnp.where((lane >= n_blk + p) & (lane < n_blk + 2 * p), pos_lo, 0.0))
    if onehot is not None:
        extra = jnp.where(lane < n_blk, onehot, extra)
    return extra.astype(BF16)


def _kv_prep_body(c_ref, s_ref, w_ref, kns_ref, knw_ref,
                  kc_o, vc_o, ks_o, vsT_o, kw_o, vwT_o, slab_ref, *, tt, n_blk):
    dh = HEAD_DIM
    shape = (tt, AUG_WIDTH)
    pos = pl.program_id(1) * tt + lax.broadcasted_iota(jnp.int32, shape, 0)
    lane = lax.broadcasted_iota(jnp.int32, shape, 1)
    blk = jnp.right_shift(pos, SEL_BLK.bit_length() - 1)
    pos_hi = (blk * SEL_BLK).astype(F32)
    pos_lo = (pos & (SEL_BLK - 1)).astype(F32)
    aug_sel = _position_lanes(pos_hi, pos_lo, (blk == lane).astype(F32), n_blk, shape)
    aug_win = _position_lanes(pos_hi, pos_lo, None, n_blk, shape)
    for g in range(NSA_KV_HEADS):
        k_cols = slice(g * dh, (g + 1) * dh)
        v_cols = slice(KV_WIDTH + g * dh, KV_WIDTH + (g + 1) * dh)
        for slab, (cols, out) in enumerate(((k_cols, kc_o), (v_cols, vc_o))):
            slab_ref[slab] = c_ref[:, cols]
            for i in range(CMP_STRIDE):
                rows = pl.ds(i, tt // CMP_STRIDE, stride=CMP_STRIDE)
                out[0, g, :, i * dh:(i + 1) * dh] = slab_ref[slab, rows, :]
        ks_o[0, g, :, 0:dh] = _rms(s_ref[:, k_cols], kns_ref[...]).astype(BF16)
        ks_o[0, g, :, dh:dh + AUG_WIDTH] = aug_sel
        kw_o[0, g, :, 0:dh] = _rms(w_ref[:, k_cols], knw_ref[...]).astype(BF16)
        kw_o[0, g, :, dh:dh + AUG_WIDTH] = aug_win
        for v_ref, out, chunk in ((s_ref, vsT_o, SEL_CHUNK), (w_ref, vwT_o, WIN_CHUNK)):
            vT = v_ref[:, v_cols].T.astype(BF16)
            ones_first = (lax.broadcasted_iota(jnp.int32, (V_EXTRA, chunk), 0) == 0).astype(BF16)
            for c in range(tt // chunk):
                out[0, g, c, 0:dh, :] = vT[:, c * chunk:(c + 1) * chunk]
                out[0, g, c, dh:dh + V_EXTRA, :] = ones_first


def kv_prep(z, kn_slc, kn_win, *, B, T, col0, tt):
    G, dh = NSA_KV_HEADS, HEAD_DIM
    nt = T // tt
    n_blk = T // SEL_BLK
    assert n_blk + 2 * SLOPE_PIECES <= AUG_WIDTH
    cb = col0 // (2 * KV_WIDTH)
    row = lambda b, t: b * nt + t
    kdim = dh + AUG_WIDTH
    return pl.pallas_call(
        functools.partial(_kv_prep_body, tt=tt, n_blk=n_blk),
        grid=(B, nt),
        in_specs=[pl.BlockSpec((tt, 2 * KV_WIDTH), lambda b, t: (row(b, t), cb)),
                  pl.BlockSpec((tt, 2 * KV_WIDTH), lambda b, t: (row(b, t), cb + 1)),
                  pl.BlockSpec((tt, 2 * KV_WIDTH), lambda b, t: (row(b, t), cb + 2)),
                  pl.BlockSpec((1, dh), lambda b, t: (0, 0)),
                  pl.BlockSpec((1, dh), lambda b, t: (0, 0))],
        out_specs=[pl.BlockSpec((1, G, tt // CMP_STRIDE, CMP_STRIDE * dh), lambda b, t: (b, 0, t, 0)),
                   pl.BlockSpec((1, G, tt // CMP_STRIDE, CMP_STRIDE * dh), lambda b, t: (b, 0, t, 0)),
                   pl.BlockSpec((1, G, tt, kdim), lambda b, t: (b, 0, t, 0)),
                   pl.BlockSpec((1, G, tt // SEL_CHUNK, dh + V_EXTRA, SEL_CHUNK), lambda b, t: (b, 0, t, 0, 0)),
                   pl.BlockSpec((1, G, tt, kdim), lambda b, t: (b, 0, t, 0)),
                   pl.BlockSpec((1, G, tt // WIN_CHUNK, dh + V_EXTRA, WIN_CHUNK), lambda b, t: (b, 0, t, 0, 0))],
        out_shape=[jax.ShapeDtypeStruct((B, G, T // CMP_STRIDE, CMP_STRIDE * dh), F32),
                   jax.ShapeDtypeStruct((B, G, T // CMP_STRIDE, CMP_STRIDE * dh), F32),
                   jax.ShapeDtypeStruct((B, G, T, kdim), BF16),
                   jax.ShapeDtypeStruct((B, G, T // SEL_CHUNK, dh + V_EXTRA, SEL_CHUNK), BF16),
                   jax.ShapeDtypeStruct((B, G, T, kdim), BF16),
                   jax.ShapeDtypeStruct((B, G, T // WIN_CHUNK, dh + V_EXTRA, WIN_CHUNK), BF16)],
        scratch_shapes=[pltpu.VMEM((2, tt, dh), F32)],
        compiler_params=_cparams("parallel", "parallel"),
        name="kv_prep",
    )(z, z, z, kn_slc, kn_win)


def _compress_body(xk_ref, xv_ref, pk_ref, pv_ref, w1k_ref, w2k_ref, w1v_ref, w2v_ref, kn_ref,
                   kc_o, vcT_o, *, n_blk):
    dh = HEAD_DIM

    def phi(x_ref, p_ref, w1_ref, w2_ref):
        x = x_ref[0, 0]
        first = jnp.dot((x + p_ref[0:1, :]).astype(BF16), w1_ref[0], preferred_element_type=F32)
        second = jnp.dot((x + p_ref[1:2, :]).astype(BF16), w1_ref[1], preferred_element_type=F32)
        n = second.shape[0]
        h = first + pltpu.roll(second, n - 1, 0)
        return jnp.dot(_silu(h).astype(BF16), w2_ref[...], preferred_element_type=F32)

    k = _rms(phi(xk_ref, pk_ref, w1k_ref, w2k_ref), kn_ref[...])
    nc = k.shape[0]
    shape = (nc, AUG_WIDTH)
    start = (lax.broadcasted_iota(jnp.int32, shape, 0) * CMP_STRIDE).astype(F32)
    centre = jnp.full(shape, (CMP_LEN - 1) / 2.0, F32)
    kc_o[0, 0, :, 0:dh] = k.astype(BF16)
    kc_o[0, 0, :, dh:dh + AUG_WIDTH] = _position_lanes(start, centre, None, n_blk, shape)
    vcT_o[0, 0] = phi(xv_ref, pv_ref, w1v_ref, w2v_ref).T.astype(BF16)


def compress_kv(xk, xv, pos_k, pos_v, w1k, w2k, w1v, w2v, kn_cmp, *, n_blk):
    B, G, nc, half = xk.shape
    dh = half // CMP_STRIDE
    assert CMP_LEN == 2 * CMP_STRIDE
    x_spec = pl.BlockSpec((1, 1, nc, half), lambda b, g: (b, g, 0, 0))
    full = lambda shape: pl.BlockSpec(shape, lambda b, g: (0,) * len(shape))
    return pl.pallas_call(
        functools.partial(_compress_body, n_blk=n_blk),
        grid=(B, G),
        in_specs=[x_spec, x_spec, full((2, half)), full((2, half)),
                  full((2, half, dh)), full((dh, dh)), full((2, half, dh)), full((dh, dh)),
                  full((1, dh))],
        out_specs=[pl.BlockSpec((1, 1, nc, dh + AUG_WIDTH), lambda b, g: (b, g, 0, 0)),
                   pl.BlockSpec((1, 1, dh, nc), lambda b, g: (b, g, 0, 0))],
        out_shape=[jax.ShapeDtypeStruct((B, G, nc, dh + AUG_WIDTH), BF16),
                   jax.ShapeDtypeStruct((B, G, dh, nc), BF16)],
        compiler_params=_cparams("parallel", "parallel"),
        name="compress_kv",
    )(xk, xv, pos_k.reshape(2, half), pos_v.reshape(2, half),
      w1k.reshape(2, half, dh), w2k, w1v.reshape(2, half, dh), w2v, kn_cmp)


def _nsa_body(q_ref, qnext_ref, gl_ref, gb_ref, qn_ref, lanes_ref, kc_ref, vcT_ref,
              ks_ref, vsT_ref, kw_ref, vwT_ref, ovT_ref, kmq_ref, o_ref,
              score_ref, gate_ref, qc_ref, qs_ref, m_ref, acc_ref, ocmp_ref, sbuf_ref, *, n_tiles):
    dh, hg, tq = HEAD_DIM, NSA_GROUP, Q_TILE
    cols = hg * tq
    qi = pl.program_id(2)
    contract_last = (((1,), (1,)), ((), ()))
    n_blk = ovT_ref.shape[0]

    scale = dh ** -0.5 * LOG2E
    hc = kmq_ref.shape[1]
    assert cols % hc == 0 and hc % tq == 0
    groups = [slice(i * hc, (i + 1) * hc) for i in range(cols // hc)]
    cw = ck = tq
    assert WIN_CHUNK == SEL_CHUNK == tq == kmq_ref.shape[0] and WINDOW % cw == 0
    n_back = WINDOW // cw
    nc = kc_ref.shape[2]
    WIN, SEL = 0, 1

    def reset(b):
        m_ref[b] = jnp.full((1, cols), NEG_INF, F32)
        acc_ref[b] = jnp.zeros((dh + V_EXTRA, cols), F32)

    def issue_scores(queries_ref, parts, unit):
        for k, _, _ in parts:
            for cs in groups:
                sbuf_ref[unit, 0:k.shape[0], :] = lax.dot_general(
                    k, queries_ref[cs, :], contract_last, preferred_element_type=F32)
                unit += 1

    def softmax_update(b, parts, unit):
        for (k, v_pieces, mask) in parts:
            for cs in groups:
                s_ref = sbuf_ref.at[unit, 0:k.shape[0], :]
                unit += 1
                if mask is not None:
                    s_ref[...] = jnp.where(mask(), s_ref[...], NEG_INF)
                m_old = m_ref[b, :, cs]
                m_new = jnp.maximum(m_old, jnp.max(s_ref[...], axis=0, keepdims=True))
                alpha = jnp.exp2(m_old - m_new)
                p = jnp.exp2(s_ref[...] - m_new).astype(BF16)
                m_ref[b, :, cs] = m_new
                pv = None
                for i, vT in enumerate(v_pieces):
                    n = vT.shape[1]
                    d = jnp.dot(vT, p[i * n:(i + 1) * n, :], preferred_element_type=F32)
                    pv = d if pv is None else pv + d
                acc_ref[b, :, cs] = alpha * acc_ref[b, :, cs] + pv

    def result(b):
        return acc_ref[b, 0:dh, :] * (1.0 / acc_ref[b, dh:dh + 1, :])

    def prepare(src_ref, tile, slot):
        t0 = tile * tq
        for h in range(hg):
            rows = slice(h * tq, (h + 1) * tq)
            q_h = (_rms(src_ref[:, h * dh:(h + 1) * dh], qn_ref[...]) * scale).astype(BF16)
            qc_ref[slot, rows, 0:dh] = q_h
            qs_ref[slot, rows, 0:dh] = q_h
            qc_ref[slot, rows, dh:dh + AUG_WIDTH] = jnp.broadcast_to(
                lanes_ref[0, hg + h:hg + h + 1, :], (tq, AUG_WIDTH)).astype(BF16)

        s = lax.dot_general(kc_ref[0, 0], qc_ref[slot], contract_last, preferred_element_type=F32)
        c_idx = lax.broadcasted_iota(jnp.int32, (nc, cols), 0)
        t_c = t0 + (lax.broadcasted_iota(jnp.int32, (nc, cols), 1) & (tq - 1))
        mask = (c_idx * CMP_STRIDE + (CMP_LEN - 1)) <= t_c
        s = jnp.where(mask, s, NEG_INF)
        m_c = jnp.max(s, axis=0, keepdims=True)
        p_c = jnp.where(mask, jnp.exp2(s - m_c), 0.0)
        l_c = jnp.sum(p_c, axis=0, keepdims=True)
        p_c = p_c * jnp.where(l_c > 0.0, 1.0 / l_c, 0.0)
        ocmp_ref[slot] = jnp.dot(vcT_ref[0, 0], p_c.astype(BF16), preferred_element_type=F32)

        p_sum = p_c[:, 0:tq]
        for h in range(1, hg):
            p_sum = p_sum + p_c[:, h * tq:(h + 1) * tq]
        hi = p_sum.astype(BF16)
        rem = p_sum - hi.astype(F32)
        mid = rem.astype(BF16)
        lo = (rem - mid.astype(F32)).astype(BF16)
        ovT = ovT_ref[...]
        imp = (jnp.dot(ovT, hi, preferred_element_type=F32)
               + jnp.dot(ovT, mid, preferred_element_type=F32)
               + jnp.dot(ovT, lo, preferred_element_type=F32))
        blk = lax.broadcasted_iota(jnp.int32, (n_blk, tq), 0)
        t_b = t0 + lax.broadcasted_iota(jnp.int32, (n_blk, tq), 1)
        cur = jnp.right_shift(t_b, SEL_BLK.bit_length() - 1)
        valid = blk * SEL_BLK <= t_b
        forced = (blk == 0) | (blk == cur) | (blk == cur - 1)
        score = jnp.where(valid, imp + FORCE_BONUS * forced.astype(F32), NEG_INF)
        score_ref[...] = score
        rank = jnp.zeros((n_blk, tq), F32)
        for j in range(n_blk):
            sj = score_ref[j:j + 1, :]
            beats = (sj > score) | ((sj == score) & (blk > j))
            rank = rank + beats.astype(F32)
        block_bias = jnp.where(rank < float(min(N_SEL, n_blk)), 0.0, NEG_INF)

        assert 2 * SLOPE_PIECES <= 8
        ones_rows = (lax.broadcasted_iota(jnp.int32, (8, tq), 0) < 2 * SLOPE_PIECES).astype(F32)
        extra_t = jnp.concatenate([block_bias, ones_rows,
                                   jnp.zeros((AUG_WIDTH - n_blk - 8, tq), F32)], axis=0).T
        for h in range(hg):
            qs_ref[slot, h * tq:(h + 1) * tq, dh:dh + AUG_WIDTH] = (
                extra_t * lanes_ref[0, h:h + 1, :]).astype(BF16)

    def win_part(c, mask=None):
        return kw_ref[0, 0, c * cw:(c + 1) * cw, :], [vwT_ref[0, 0, c]], mask

    def sel_part(c, n=1, mask=None):
        return ks_ref[0, 0, c * ck:(c + n) * ck, :], [vsT_ref[0, 0, c + i] for i in range(n)], mask

    def tile_step(tile):
        slot = tile % 2
        below_diagonal = lambda: kmq_ref[...] <= 0
        win_parts = [win_part(tile, below_diagonal)]
        win_parts += [win_part(tile - j) for j in range(1, n_back) if tile - j >= 0]
        if tile - n_back >= 0:
            win_parts.append(win_part(tile - n_back, lambda: kmq_ref[...] > 0))
        sel_parts = [sel_part(tile, mask=below_diagonal)]
        sel_parts += [sel_part(c, n=min(2, tile - c)) for c in range(0, tile, 2)]
        sel_unit = len(win_parts) * len(groups)
        issue_scores(qc_ref.at[slot], win_parts, 0)
        issue_scores(qs_ref.at[slot], sel_parts, sel_unit)
        if tile + 1 < n_tiles:
            prepare(qnext_ref, tile + 1, 1 - slot)
        reset(WIN)
        softmax_update(WIN, win_parts, 0)
        reset(SEL)
        softmax_update(SEL, sel_parts, sel_unit)

        gate_ref[...] = jax.nn.sigmoid(gl_ref[...] + gb_ref[...]).T
        o_win, o_slc = result(WIN), result(SEL)
        for h in range(hg):
            cs = slice(h * tq, (h + 1) * tq)
            gate = lambda branch: gate_ref[h * N_BRANCH + branch:h * N_BRANCH + branch + 1, :]
            o_h = gate(0) * ocmp_ref[slot, :, cs] + gate(1) * o_slc[:, cs] + gate(2) * o_win[:, cs]
            o_ref[:, h * dh:(h + 1) * dh] = o_h.T.astype(o_ref.dtype)

    @pl.when(qi == 0)
    def _():
        prepare(q_ref, 0, 0)

    for tile in range(n_tiles):
        pl.when(qi == tile)(functools.partial(tile_step, tile))


def nsa_attention(z, gl, gate_b, q_norm, kcmp, vcmpT, ks, vsT, kw, vwT, *, B, T):
    G, hg, dh, tq = NSA_KV_HEADS, NSA_GROUP, HEAD_DIM, Q_TILE
    nq = T // tq
    nc = kcmp.shape[2]
    n_blk = T // SEL_BLK
    kdim = dh + AUG_WIDTH
    cols = hg * tq
    assert SEL_CHUNK == WIN_CHUNK == tq
    n_units = (WINDOW // WIN_CHUNK + 1 + 1 + nq // 2) * (cols // NSA_COLS_PER_DOT)
    assert SEL_CHUNK % SEL_BLK == 0 and T % SEL_CHUNK == 0 and gl.shape[1] == G * LANES
    assert SEL_CHUNK % tq == 0 and n_blk % 8 == 0 and n_blk + 8 <= AUG_WIDTH

    slopes = np.exp2(-8.0 * (np.arange(NSA_HEADS, dtype=np.float64) + 1.0) / NSA_HEADS).reshape(G, hg)
    lanes = np.zeros((G, 2 * hg, AUG_WIDTH), np.float32)
    lanes[:, :hg, :n_blk] = 1.0
    for i, piece in enumerate(_bf16_pieces(slopes * LOG2E, SLOPE_PIECES)):
        for lane in (n_blk + i, n_blk + SLOPE_PIECES + i):
            lanes[:, :hg, lane] = piece
            lanes[:, hg:, lane] = piece
    cs = np.arange(nc)[:, None] * CMP_STRIDE
    bs = np.arange(n_blk)[None, :] * SEL_BLK
    overlap = np.clip(np.minimum(cs + CMP_LEN, bs + SEL_BLK) - np.maximum(cs, bs), 0, None) / CMP_LEN
    overlap[(T - CMP_LEN) // CMP_STRIDE + 1:] = 0.0
    ovT = jnp.asarray(overlap.T, dtype=BF16)
    hc = NSA_COLS_PER_DOT
    kmq = jnp.asarray(np.arange(SEL_CHUNK)[:, None] - (np.arange(hc)[None, :] % tq), dtype=jnp.int32)

    row = lambda b, qi: b * nq + qi
    per_bg = lambda shape: pl.BlockSpec((1, 1) + shape, lambda b, g, qi: (b, g) + (0,) * len(shape))
    const = lambda shape: pl.BlockSpec(shape, lambda b, g, qi: (0,) * len(shape))
    return pl.pallas_call(
        functools.partial(_nsa_body, n_tiles=nq),
        grid=(B, G, nq),
        in_specs=[pl.BlockSpec((tq, hg * dh), lambda b, g, qi: (row(b, qi), g)),
                  pl.BlockSpec((tq, hg * dh), lambda b, g, qi: (row(b, jnp.minimum(qi + 1, nq - 1)), g)),
                  pl.BlockSpec((tq, LANES), lambda b, g, qi: (row(b, qi), g)),
                  pl.BlockSpec((1, LANES), lambda b, g, qi: (0, g)),
                  const((1, dh)),
                  pl.BlockSpec((1, 2 * hg, AUG_WIDTH), lambda b, g, qi: (g, 0, 0)),
                  per_bg((nc, kdim)), per_bg((dh, nc)),
                  per_bg((T, kdim)), per_bg((T // SEL_CHUNK, dh + V_EXTRA, SEL_CHUNK)),
                  per_bg((T, kdim)), per_bg((T // WIN_CHUNK, dh + V_EXTRA, WIN_CHUNK)),
                  const((n_blk, nc)), const((SEL_CHUNK, hc))],
        out_specs=pl.BlockSpec((tq, hg * dh), lambda b, g, qi: (row(b, qi), g)),
        out_shape=jax.ShapeDtypeStruct((B * T, NSA_WIDTH), BF16),
        scratch_shapes=[pltpu.VMEM((n_blk, tq), F32), pltpu.VMEM((LANES, tq), F32),
                        pltpu.VMEM((2, cols, kdim), BF16), pltpu.VMEM((2, cols, kdim), BF16),
                        pltpu.VMEM((2, 1, cols), F32),
                        pltpu.VMEM((2, dh + V_EXTRA, cols), F32), pltpu.VMEM((2, dh, cols), F32),
                        pltpu.VMEM((n_units, 2 * SEL_CHUNK, NSA_COLS_PER_DOT), F32)],
        compiler_params=_cparams("parallel", "parallel", "arbitrary"),
        name="nsa_attention",
    )(z, z, gl, gate_b, q_norm, jnp.asarray(lanes), kcmp, vcmpT, ks, vsT, kw, vwT, ovT, kmq)


def _conv_body(a_ref, b_ref, ah_ref, bh_ref, w_ref, cb_ref, lg_ref, lb_ref, o_ref,
               h_ref, g_ref, y_ref, *, tt):
    C = a_ref.shape[1]
    halo = ah_ref[...] * jax.nn.sigmoid(bh_ref[...])
    h_ref[0:CONV_HALO, :] = jnp.where(pl.program_id(1) == 0, 0.0, halo)
    h_ref[CONV_HALO:CONV_HALO + tt, :] = a_ref[...] * jax.nn.sigmoid(b_ref[...])
    h_ref[CONV_HALO + tt:CONV_HALO + tt + SUBLANES, :] = jnp.zeros((SUBLANES, C), F32)
    lead = CONV_HALO - (CONV_KERNEL - 1)
    rows = tt + SUBLANES
    for s in range(C // LANES):
        cs = slice(s * LANES, (s + 1) * LANES)
        acc = None
        for r in range(SUBLANES):
            group = None
            for o in range(lead, lead + CONV_KERNEL):
                if o % SUBLANES == r:
                    term = w_ref[o - lead:o - lead + 1, cs] * h_ref[o - r:o - r + rows, cs]
                    group = term if group is None else group + term
            if r == 0:
                piece = group[0:tt, :]
            else:
                g_ref[r] = group
                piece = g_ref[r, r:r + tt, :]
            acc = piece if acc is None else acc + piece
        y_ref[:, cs] = acc + cb_ref[:, cs]
    y = y_ref[...]
    mu = jnp.mean(y, axis=-1, keepdims=True)
    var = jnp.mean(jnp.square(y - mu), axis=-1, keepdims=True)
    yn = (y - mu) * lax.rsqrt(var + LN_EPS) * lg_ref[...] + lb_ref[...]
    o_ref[...] = _silu(yn).astype(o_ref.dtype)


def conformer_conv(z, conv_w, conv_b, ln_g, ln_b, *, B, T, col0, C, tt):
    nt = T // tt
    ca = col0 // C
    hb = tt // CONV_HALO
    row = lambda b, t: b * nt + t
    halo_row = lambda b, t: jnp.maximum(row(b, t) * hb - 1, 0)
    const = lambda shape: pl.BlockSpec(shape, lambda b, t: (0,) * len(shape))
    return pl.pallas_call(
        functools.partial(_conv_body, tt=tt),
        grid=(B, nt),
        in_specs=[pl.BlockSpec((tt, C), lambda b, t: (row(b, t), ca)),
                  pl.BlockSpec((tt, C), lambda b, t: (row(b, t), ca + 1)),
                  pl.BlockSpec((CONV_HALO, C), lambda b, t: (halo_row(b, t), ca)),
                  pl.BlockSpec((CONV_HALO, C), lambda b, t: (halo_row(b, t), ca + 1)),
                  const((CONV_KERNEL, C)), const((1, C)), const((1, C)), const((1, C))],
        out_specs=pl.BlockSpec((tt, C), lambda b, t: (row(b, t), 0)),
        out_shape=jax.ShapeDtypeStruct((B * T, C), BF16),
        scratch_shapes=[pltpu.VMEM((CONV_HALO + tt + SUBLANES, C), F32),
                        pltpu.VMEM((SUBLANES, tt + SUBLANES, LANES), F32),
                        pltpu.VMEM((tt, C), F32)],
        compiler_params=_cparams("parallel", "parallel"),
        name="conformer_conv",
    )(z, z, z, z, conv_w, conv_b, ln_g, ln_b)


def _mem_attn_body(x_ref, gq_ref, wq_ref, mem_ref, gkv_ref, wkv_ref, qn_ref, kn_ref, wo_ref, gn_ref,
                   o_ref, on_ref, k_scr, v_scr):
    dh = HEAD_DIM
    width = MEM_HEADS * dh
    scale = dh ** -0.5
    heads = [slice(h * dh, (h + 1) * dh) for h in range(MEM_HEADS)]

    @pl.when(pl.program_id(1) == 0)
    def _():
        kv = jnp.dot(_rms(mem_ref[...], gkv_ref[...]).astype(BF16), wkv_ref[...], preferred_element_type=F32)
        for cs in heads:
            k_scr[:, cs] = _rms(kv[:, cs], kn_ref[...]).astype(BF16)
        v_scr[...] = kv[:, width:2 * width].astype(BF16)

    q_all = jnp.dot(_rms(x_ref[...], gq_ref[...]).astype(BF16), wq_ref[...], preferred_element_type=F32)
    scores = []
    for cs in heads:
        q = (_rms(q_all[:, cs], qn_ref[...]) * scale).astype(BF16)
        scores.append(lax.dot_general(q, k_scr[:, cs], (((1,), (1,)), ((), ())), preferred_element_type=F32))
    outs = []
    for h, s in enumerate(scores):
        v = v_scr[:, heads[h]]
        p = jnp.exp(s - jnp.max(s, axis=-1, keepdims=True))
        inv = 1.0 / jnp.sum(p, axis=-1, keepdims=True)
        outs.append((jnp.dot(p.astype(BF16), v, preferred_element_type=F32) * inv).astype(BF16))
    o = jnp.concatenate(outs, axis=1)
    y = x_ref[...] + jnp.dot(o, wo_ref[...], preferred_element_type=F32)
    o_ref[...] = y
    on_ref[...] = _rms(y, gn_ref[...]).astype(on_ref.dtype)


def mem_attention(x, q_gain, w_mq, mem, kv_gain, w_mkv, mq_norm, mk_norm, w_mo, next_norm, *, B, T, M, tt):
    D = x.shape[1]
    width = MEM_HEADS * HEAD_DIM
    nt = T // tt
    row = lambda b, t: b * nt + t
    const = lambda shape: pl.BlockSpec(shape, lambda b, t: (0,) * len(shape))
    return pl.pallas_call(
        _mem_attn_body,
        grid=(B, nt),
        in_specs=[pl.BlockSpec((tt, D), lambda b, t: (row(b, t), 0)),
                  const((1, D)), const((D, width)),
                  pl.BlockSpec((M, D), lambda b, t: (b, 0)),
                  const((1, D)), const((D, 2 * width)),
                  const((1, HEAD_DIM)), const((1, HEAD_DIM)), const((width, D)),
                  const((1, D))],
        out_specs=[pl.BlockSpec((tt, D), lambda b, t: (row(b, t), 0)),
                   pl.BlockSpec((tt, D), lambda b, t: (row(b, t), 0))],
        out_shape=[jax.ShapeDtypeStruct((B * T, D), F32),
                   jax.ShapeDtypeStruct((B * T, D), BF16)],
        scratch_shapes=[pltpu.VMEM((M, width), BF16), pltpu.VMEM((M, width), BF16)],
        compiler_params=_cparams("parallel", "arbitrary"),
        name="mem_attention",
    )(x, q_gain, w_mq, mem, kv_gain, w_mkv, mq_norm, mk_norm, w_mo, next_norm)


def kernel(x, mem, norm_mix, w_in, gate_b, q_norm, k_norm_cmp, k_norm_slc, k_norm_win, cmp_pos_k, cmp_pos_v, cmp_k_w1, cmp_k_w2, cmp_v_w1, cmp_v_w2, conv_w, conv_b, conv_ln_g, conv_ln_b, w_out, norm_mem_q, norm_mem_kv, w_mq, w_mk, w_mv, mq_norm, mk_norm, w_mo, norm_ffn, w_gate, w_up, w_down):
    B, T, D = x.shape
    M = mem.shape[1]
    depth = w_in.shape[0]
    G, hg = NSA_KV_HEADS, NSA_GROUP
    C = D - NSA_WIDTH
    n_gate = N_BRANCH * NSA_HEADS
    kv_all = 6 * KV_WIDTH
    assert w_in.shape[2] == NSA_WIDTH + kv_all + n_gate + 2 * C
    assert C == NSA_WIDTH, "column blocking below assumes equal NSA and conv widths"

    row1 = lambda v: v.reshape(1, -1)
    xf = x.reshape(B * T, D)
    memf = mem.reshape(B * M, D)
    u0 = NSA_WIDTH + kv_all + n_gate
    for l in range(depth):
        w = w_in[l]
        w_t = jnp.swapaxes(w, 0, 1)
        wg = w[:, NSA_WIDTH + kv_all:u0].reshape(D, G, hg * N_BRANCH)
        wg = jnp.pad(wg, ((0, 0), (0, 0), (0, LANES - hg * N_BRANCH))).reshape(D, G * LANES)
        gb = jnp.pad(gate_b[l].reshape(G, hg * N_BRANCH), ((0, 0), (0, LANES - hg * N_BRANCH))).reshape(1, G * LANES)

        xn = rmsnorm_bf16(xf, row1(norm_mix[l]), tm=TILE_RMSNORM)
        z, gl = in_projection(xn, w_t, wg, tm=TILE_IN_PROJ[0], tn=TILE_IN_PROJ[1],
                              segments=((0, NSA_WIDTH), (u0, 2 * C), (NSA_WIDTH, kv_all)))
        kv_col0 = NSA_WIDTH + 2 * C
        kc, vc, ks, vsT, kw, vwT = kv_prep(z, row1(k_norm_slc[l]), row1(k_norm_win[l]),
                                           B=B, T=T, col0=kv_col0, tt=TILE_KV_PREP)
        kcmp, vcmpT = compress_kv(kc, vc, cmp_pos_k[l], cmp_pos_v[l],
                                  cmp_k_w1[l].astype(BF16), cmp_k_w2[l].astype(BF16),
                                  cmp_v_w1[l].astype(BF16), cmp_v_w2[l].astype(BF16), row1(k_norm_cmp[l]),
                                  n_blk=T // SEL_BLK)
        o_nsa = nsa_attention(z, gl, gb, row1(q_norm[l]), kcmp, vcmpT, ks, vsT, kw, vwT, B=B, T=T)
        o_conv = conformer_conv(z, conv_w[l], row1(conv_b[l]), row1(conv_ln_g[l]), row1(conv_ln_b[l]),
                                B=B, T=T, col0=NSA_WIDTH, C=C, tt=TILE_CONV)
        xf = concat_matmul_residual(o_nsa, o_conv, w_out[l], xf, tm=TILE_OUT_PROJ[0], tn=TILE_OUT_PROJ[1])

        w_mkv = jnp.concatenate([w_mk[l], w_mv[l]], axis=1).astype(BF16)
        xf, xn = mem_attention(xf, row1(norm_mem_q[l]), w_mq[l].astype(BF16),
                               memf, row1(norm_mem_kv[l]), w_mkv,
                               row1(mq_norm[l]), row1(mk_norm[l]), w_mo[l].astype(BF16),
                               row1(norm_ffn[l]), B=B, T=T, M=M, tt=TILE_MEM_ATTN)

        hidden, w_down_bf16 = swiglu_up(xn, w_gate[l], w_up[l], w_down[l],
                                        tm=TILE_FFN_UP[0], tn=TILE_FFN_UP[1])
        xf = matmul_residual(hidden, w_down_bf16, xf, tm=TILE_FFN_DOWN[0], tn=TILE_FFN_DOWN[1])
    return xf.reshape(B, T, D)
```

```python
import functools

import ml_dtypes
import numpy as np
import jax
import jax.numpy as jnp
from jax import lax
from jax.experimental import pallas as pl
from jax.experimental.pallas import tpu as pltpu

F32 = jnp.float32
BF16 = jnp.bfloat16

HEAD_DIM = 128
NSA_HEADS = 8
NSA_KV_HEADS = 2
NSA_GROUP = NSA_HEADS // NSA_KV_HEADS
NSA_WIDTH = NSA_HEADS * HEAD_DIM
KV_WIDTH = NSA_KV_HEADS * HEAD_DIM
N_BRANCH = 3
CONV_KERNEL = 31
CMP_LEN = 32
CMP_STRIDE = 16
SEL_BLK = 64
N_SEL = 16
WINDOW = 512
MEM_HEADS = 4
NEG_INF = -1e30
FORCE_BONUS = 1e3
RMS_EPS = 1e-6
LN_EPS = 1e-5

LANES = 128
SUBLANES = 8
VMEM_LIMIT_BYTES = 56 * 1024 * 1024

Q_TILE = 256
SEL_CHUNK = 256
WIN_CHUNK = 256
NSA_COLS_PER_DOT = 512
CONV_HALO = 32

TILE_RMSNORM = 512
TILE_IN_PROJ = (2048, 512)
TILE_KV_PREP = 1024
TILE_CONV = 256
TILE_OUT_PROJ = (2048, 512)
TILE_MEM_ATTN = 512
TILE_FFN_UP = (1024, 512)
TILE_FFN_DOWN = (1024, 512)


def _cparams(*sem):
    return pltpu.CompilerParams(dimension_semantics=sem, vmem_limit_bytes=VMEM_LIMIT_BYTES)


def _rms(x, g, eps=RMS_EPS):
    ms = jnp.mean(x * x, axis=-1, keepdims=True)
    return x * lax.rsqrt(ms + eps) * g


def _silu(x):
    return x * jax.nn.sigmoid(x)


def _rmsnorm_body(x_ref, g_ref, o_ref):
    o_ref[...] = _rms(x_ref[...], g_ref[...]).astype(o_ref.dtype)


def rmsnorm_bf16(x, g, *, tm):
    M, K = x.shape
    return pl.pallas_call(
        _rmsnorm_body,
        grid=(M // tm,),
        in_specs=[pl.BlockSpec((tm, K), lambda i: (i, 0)), pl.BlockSpec((1, K), lambda i: (0, 0))],
        out_specs=pl.BlockSpec((tm, K), lambda i: (i, 0)),
        out_shape=jax.ShapeDtypeStruct((M, K), BF16),
        compiler_params=_cparams("parallel"),
        name="rmsnorm",
    )(x, g)


def _in_proj_body(xn_ref, wt_ref, wg_ref, o_ref, og_ref):
    @pl.when(pl.program_id(1) == 0)
    def _():
        og_ref[...] = jnp.dot(xn_ref[...], wg_ref[...].astype(BF16), preferred_element_type=F32)

    o_ref[...] = lax.dot_general(xn_ref[...], wt_ref[...].astype(BF16), (((1,), (1,)), ((), ())),
                                 preferred_element_type=F32).astype(o_ref.dtype)


def _swiglu_body(x_ref, wg_ref, wu_ref, wd_ref, o_ref, wd_o):
    x = x_ref[...]
    a = jnp.dot(x, wg_ref[...].astype(BF16), preferred_element_type=F32)
    b = jnp.dot(x, wu_ref[...].astype(BF16), preferred_element_type=F32)
    o_ref[...] = (_silu(a) * b).astype(o_ref.dtype)

    @pl.when(pl.program_id(0) == 0)
    def _():
        wd_o[...] = wd_ref[...].astype(wd_o.dtype)


def in_projection(xn, w_t, wg, *, segments, tm, tn):
    M, K = xn.shape
    NG = wg.shape[1]
    assert all(start % SUBLANES == 0 and size % tn == 0 for start, size in segments)
    N = sum(size for _, size in segments)

    def row_start(i, j):
        group, first = 0, 0
        for seg_start, size in segments:
            group = jnp.where(j >= first, seg_start // SUBLANES + (j - first) * (tn // SUBLANES), group)
            first += size // tn
        return (group * SUBLANES, 0)

    return pl.pallas_call(
        _in_proj_body,
        grid=(M // tm, N // tn),
        in_specs=[pl.BlockSpec((tm, K), lambda i, j: (i, 0)),
                  pl.BlockSpec((pl.Element(tn), pl.Element(K)), row_start),
                  pl.BlockSpec((K, NG), lambda i, j: (0, 0))],
        out_specs=[pl.BlockSpec((tm, tn), lambda i, j: (i, j)),
                   pl.BlockSpec((tm, NG), lambda i, j: (i, 0))],
        out_shape=[jax.ShapeDtypeStruct((M, N), BF16),
                   jax.ShapeDtypeStruct((M, NG), F32)],
        compiler_params=_cparams("parallel", "arbitrary"),
        name="in_proj",
    )(xn, w_t, wg)


def swiglu_up(xn, w_gate, w_up, w_down, *, tm, tn):
    M, K = xn.shape
    N = w_gate.shape[1]
    D = w_down.shape[1]
    nj = N // tn
    assert w_down.shape[0] == N
    down_block = lambda i, j: (jnp.where(i == 0, j, nj - 1), 0)
    return pl.pallas_call(
        _swiglu_body,
        grid=(M // tm, nj),
        in_specs=[pl.BlockSpec((tm, K), lambda i, j: (i, 0)),
                  pl.BlockSpec((K, tn), lambda i, j: (0, j)),
                  pl.BlockSpec((K, tn), lambda i, j: (0, j)),
                  pl.BlockSpec((tn, D), down_block)],
        out_specs=[pl.BlockSpec((tm, tn), lambda i, j: (i, j)),
                   pl.BlockSpec((tn, D), down_block)],
        out_shape=[jax.ShapeDtypeStruct((M, N), BF16),
                   jax.ShapeDtypeStruct((N, D), BF16)],
        compiler_params=_cparams("arbitrary", "arbitrary"),
        name="ffn_up",
    )(xn, w_gate, w_up, w_down)


def _mm_res_body(a_ref, w_ref, r_ref, o_ref):
    o_ref[...] = r_ref[...] + jnp.dot(a_ref[...], w_ref[...], preferred_element_type=F32)


def matmul_residual(a, w, res, *, tm, tn):
    M, K = a.shape
    N = w.shape[1]
    return pl.pallas_call(
        _mm_res_body,
        grid=(M // tm, N // tn),
        in_specs=[pl.BlockSpec((tm, K), lambda i, j: (i, 0)),
                  pl.BlockSpec((K, tn), lambda i, j: (0, j)),
                  pl.BlockSpec((tm, tn), lambda i, j: (i, j))],
        out_specs=pl.BlockSpec((tm, tn), lambda i, j: (i, j)),
        out_shape=jax.ShapeDtypeStruct((M, N), F32),
        compiler_params=_cparams("parallel", "arbitrary"),
        name="matmul_residual",
    )(a, w, res)


def _mm2_res_body(a_ref, b_ref, wa_ref, wb_ref, r_ref, o_ref):
    acc = jnp.dot(a_ref[...], wa_ref[...].astype(BF16), preferred_element_type=F32)
    acc = acc + jnp.dot(b_ref[...], wb_ref[...].astype(BF16), preferred_element_type=F32)
    o_ref[...] = r_ref[...] + acc


def concat_matmul_residual(a, b, w, res, *, tm, tn):
    M, Ka = a.shape
    Kb = b.shape[1]
    assert Ka == Kb and w.shape[0] == Ka + Kb
    N = w.shape[1]
    return pl.pallas_call(
        _mm2_res_body,
        grid=(M // tm, N // tn),
        in_specs=[pl.BlockSpec((tm, Ka), lambda i, j: (i, 0)),
                  pl.BlockSpec((tm, Kb), lambda i, j: (i, 0)),
                  pl.BlockSpec((Ka, tn), lambda i, j: (0, j)),
                  pl.BlockSpec((Kb, tn), lambda i, j: (1, j)),
                  pl.BlockSpec((tm, tn), lambda i, j: (i, j))],
        out_specs=pl.BlockSpec((tm, tn), lambda i, j: (i, j)),
        out_shape=jax.ShapeDtypeStruct((M, N), F32),
        compiler_params=_cparams("parallel", "arbitrary"),
        name="out_proj",
    )(a, b, w, w, res)


AUG_WIDTH = LANES
V_EXTRA = 16
SLOPE_PIECES = 4
LOG2E = float(np.log2(np.e))


def _bf16_pieces(x, n):
    out, rem = [], np.asarray(x, np.float64)
    for _ in range(n):
        piece = rem.astype(ml_dtypes.bfloat16).astype(np.float64)
        out.append(piece.astype(np.float32))
        rem = rem - piece
    return out


def _position_lanes(pos_hi, pos_lo, onehot, n_blk, shape):
    lane = lax.broadcasted_iota(jnp.int32, shape, 1)
    p = SLOPE_PIECES
    extra = jnp.where((lane >= n_blk) & (lane < n_blk + p), pos_hi,
                      jnp.where((lane >= n_blk + p) & (lane < n_blk + 2 * p), pos_lo, 0.0))
    if onehot is not None:
        extra = jnp.where(lane < n_blk, onehot, extra)
    return extra.astype(BF16)


def _kv_prep_body(c_ref, s_ref, w_ref, kns_ref, knw_ref,
                  kc_o, vc_o, ks_o, vsT_o, kw_o, vwT_o, slab_ref, *, tt, n_blk):
    dh = HEAD_DIM
    shape = (tt, AUG_WIDTH)
    pos = pl.program_id(1) * tt + lax.broadcasted_iota(jnp.int32, shape, 0)
    lane = lax.broadcasted_iota(jnp.int32, shape, 1)
    blk = jnp.right_shift(pos, SEL_BLK.bit_length() - 1)
    pos_hi = (blk * SEL_BLK).astype(F32)
    pos_lo = (pos & (SEL_BLK - 1)).astype(F32)
    aug_sel = _position_lanes(pos_hi, pos_lo, (blk == lane).astype(F32), n_blk, shape)
    aug_win = _position_lanes(pos_hi, pos_lo, None, n_blk, shape)
    for g in range(NSA_KV_HEADS):
        k_cols = slice(g * dh, (g + 1) * dh)
        v_cols = slice(KV_WIDTH + g * dh, KV_WIDTH + (g + 1) * dh)
        for slab, (cols, out) in enumerate(((k_cols, kc_o), (v_cols, vc_o))):
            slab_ref[slab] = c_ref[:, cols].astype(F32)
            for i in range(CMP_STRIDE):
                rows = pl.ds(i, tt // CMP_STRIDE, stride=CMP_STRIDE)
                out[0, g, :, i * dh:(i + 1) * dh] = slab_ref[slab, rows, :]
        ks_o[0, g, :, 0:dh] = _rms(s_ref[:, k_cols].astype(F32), kns_ref[...]).astype(BF16)
        ks_o[0, g, :, dh:dh + AUG_WIDTH] = aug_sel
        kw_o[0, g, :, 0:dh] = _rms(w_ref[:, k_cols].astype(F32), knw_ref[...]).astype(BF16)
        kw_o[0, g, :, dh:dh + AUG_WIDTH] = aug_win
        for v_ref, out, chunk in ((s_ref, vsT_o, SEL_CHUNK), (w_ref, vwT_o, WIN_CHUNK)):
            vT = v_ref[:, v_cols].astype(F32).T.astype(BF16)
            ones_first = (lax.broadcasted_iota(jnp.int32, (V_EXTRA, chunk), 0) == 0).astype(BF16)
            for c in range(tt // chunk):
                out[0, g, c, 0:dh, :] = vT[:, c * chunk:(c + 1) * chunk]
                out[0, g, c, dh:dh + V_EXTRA, :] = ones_first


def kv_prep(z, kn_slc, kn_win, *, B, T, col0, tt):
    G, dh = NSA_KV_HEADS, HEAD_DIM
    nt = T // tt
    n_blk = T // SEL_BLK
    assert n_blk + 2 * SLOPE_PIECES <= AUG_WIDTH
    cb = col0 // (2 * KV_WIDTH)
    row = lambda b, t: b * nt + t
    kdim = dh + AUG_WIDTH
    return pl.pallas_call(
        functools.partial(_kv_prep_body, tt=tt, n_blk=n_blk),
        grid=(B, nt),
        in_specs=[pl.BlockSpec((tt, 2 * KV_WIDTH), lambda b, t: (row(b, t), cb)),
                  pl.BlockSpec((tt, 2 * KV_WIDTH), lambda b, t: (row(b, t), cb + 1)),
                  pl.BlockSpec((tt, 2 * KV_WIDTH), lambda b, t: (row(b, t), cb + 2)),
                  pl.BlockSpec((1, dh), lambda b, t: (0, 0)),
                  pl.BlockSpec((1, dh), lambda b, t: (0, 0))],
        out_specs=[pl.BlockSpec((1, G, tt // CMP_STRIDE, CMP_STRIDE * dh), lambda b, t: (b, 0, t, 0)),
                   pl.BlockSpec((1, G, tt // CMP_STRIDE, CMP_STRIDE * dh), lambda b, t: (b, 0, t, 0)),
                   pl.BlockSpec((1, G, tt, kdim), lambda b, t: (b, 0, t, 0)),
                   pl.BlockSpec((1, G, tt // SEL_CHUNK, dh + V_EXTRA, SEL_CHUNK), lambda b, t: (b, 0, t, 0, 0)),
                   pl.BlockSpec((1, G, tt, kdim), lambda b, t: (b, 0, t, 0)),
                   pl.BlockSpec((1, G, tt // WIN_CHUNK, dh + V_EXTRA, WIN_CHUNK), lambda b, t: (b, 0, t, 0, 0))],
        out_shape=[jax.ShapeDtypeStruct((B, G, T // CMP_STRIDE, CMP_STRIDE * dh), F32),
                   jax.ShapeDtypeStruct((B, G, T // CMP_STRIDE, CMP_STRIDE * dh), F32),
                   jax.ShapeDtypeStruct((B, G, T, kdim), BF16),
                   jax.ShapeDtypeStruct((B, G, T // SEL_CHUNK, dh + V_EXTRA, SEL_CHUNK), BF16),
                   jax.ShapeDtypeStruct((B, G, T, kdim), BF16),
                   jax.ShapeDtypeStruct((B, G, T // WIN_CHUNK, dh + V_EXTRA, WIN_CHUNK), BF16)],
        scratch_shapes=[pltpu.VMEM((2, tt, dh), F32)],
        compiler_params=_cparams("parallel", "parallel"),
        name="kv_prep",
    )(z, z, z, kn_slc, kn_win)


def _compress_body(xk_ref, xv_ref, pk_ref, pv_ref, w1k_ref, w2k_ref, w1v_ref, w2v_ref, kn_ref,
                   kc_o, vcT_o, *, n_blk):
    dh = HEAD_DIM

    def phi(x_ref, p_ref, w1_ref, w2_ref):
        x = x_ref[0, 0]
        first = jnp.dot((x + p_ref[0:1, :]).astype(BF16), w1_ref[0], preferred_element_type=F32)
        second = jnp.dot((x + p_ref[1:2, :]).astype(BF16), w1_ref[1], preferred_element_type=F32)
        n = second.shape[0]
        h = first + pltpu.roll(second, n - 1, 0)
        return jnp.dot(_silu(h).astype(BF16), w2_ref[...], preferred_element_type=F32)

    k = _rms(phi(xk_ref, pk_ref, w1k_ref, w2k_ref), kn_ref[...])
    nc = k.shape[0]
    shape = (nc, AUG_WIDTH)
    start = (lax.broadcasted_iota(jnp.int32, shape, 0) * CMP_STRIDE).astype(F32)
    centre = jnp.full(shape, (CMP_LEN - 1) / 2.0, F32)
    kc_o[0, 0, :, 0:dh] = k.astype(BF16)
    kc_o[0, 0, :, dh:dh + AUG_WIDTH] = _position_lanes(start, centre, None, n_blk, shape)
    vcT_o[0, 0] = phi(xv_ref, pv_ref, w1v_ref, w2v_ref).T.astype(BF16)


def compress_kv(xk, xv, pos_k, pos_v, w1k, w2k, w1v, w2v, kn_cmp, *, n_blk):
    B, G, nc, half = xk.shape
    dh = half // CMP_STRIDE
    assert CMP_LEN == 2 * CMP_STRIDE
    x_spec = pl.BlockSpec((1, 1, nc, half), lambda b, g: (b, g, 0, 0))
    full = lambda shape: pl.BlockSpec(shape, lambda b, g: (0,) * len(shape))
    return pl.pallas_call(
        functools.partial(_compress_body, n_blk=n_blk),
        grid=(B, G),
        in_specs=[x_spec, x_spec, full((2, half)), full((2, half)),
                  full((2, half, dh)), full((dh, dh)), full((2, half, dh)), full((dh, dh)),
                  full((1, dh))],
        out_specs=[pl.BlockSpec((1, 1, nc, dh + AUG_WIDTH), lambda b, g: (b, g, 0, 0)),
                   pl.BlockSpec((1, 1, dh, nc), lambda b, g: (b, g, 0, 0))],
        out_shape=[jax.ShapeDtypeStruct((B, G, nc, dh + AUG_WIDTH), BF16),
                   jax.ShapeDtypeStruct((B, G, dh, nc), BF16)],
        compiler_params=_cparams("parallel", "parallel"),
        name="compress_kv",
    )(xk, xv, pos_k.reshape(2, half), pos_v.reshape(2, half),
      w1k.reshape(2, half, dh), w2k, w1v.reshape(2, half, dh), w2v, kn_cmp)


def _nsa_body(q_ref, qnext_ref, gl_ref, gb_ref, qn_ref, lanes_ref, kc_ref, vcT_ref,
              ks_ref, vsT_ref, kw_ref, vwT_ref, ovT_ref, kmq_ref, o_ref,
              score_ref, gate_ref, qc_ref, qs_ref, m_ref, acc_ref, ocmp_ref, sbuf_ref, *, n_tiles):
    dh, hg, tq = HEAD_DIM, NSA_GROUP, Q_TILE
    cols = hg * tq
    qi = pl.program_id(2)
    contract_last = (((1,), (1,)), ((), ()))
    n_blk = ovT_ref.shape[0]

    scale = dh ** -0.5 * LOG2E
    hc = kmq_ref.shape[1]
    assert cols % hc == 0 and hc % tq == 0
    groups = [slice(i * hc, (i + 1) * hc) for i in range(cols // hc)]
    cw = ck = tq
    assert WIN_CHUNK == SEL_CHUNK == tq == kmq_ref.shape[0] and WINDOW % cw == 0
    n_back = WINDOW // cw
    nc = kc_ref.shape[2]
    WIN, SEL = 0, 1

    def reset(b):
        m_ref[b] = jnp.full((1, cols), NEG_INF, F32)
        acc_ref[b] = jnp.zeros((dh + V_EXTRA, cols), F32)

    def issue_scores(queries_ref, parts, unit):
        for k, _, _ in parts:
            for cs in groups:
                sbuf_ref[unit, 0:k.shape[0], :] = lax.dot_general(
                    k, queries_ref[cs, :], contract_last, preferred_element_type=F32)
                unit += 1

    def softmax_update(b, parts, unit):
        for (k, v_pieces, mask) in parts:
            for cs in groups:
                s_ref = sbuf_ref.at[unit, 0:k.shape[0], :]
                unit += 1
                if mask is not None:
                    s_ref[...] = jnp.where(mask(), s_ref[...], NEG_INF)
                m_old = m_ref[b, :, cs]
                m_new = jnp.maximum(m_old, jnp.max(s_ref[...], axis=0, keepdims=True))
                alpha = jnp.exp2(m_old - m_new)
                p = jnp.exp2(s_ref[...] - m_new).astype(BF16)
                m_ref[b, :, cs] = m_new
                pv = None
                for i, vT in enumerate(v_pieces):
                    n = vT.shape[1]
                    d = jnp.dot(vT, p[i * n:(i + 1) * n, :], preferred_element_type=F32)
                    pv = d if pv is None else pv + d
                acc_ref[b, :, cs] = alpha * acc_ref[b, :, cs] + pv

    def result(b):
        return acc_ref[b, 0:dh, :] * (1.0 / acc_ref[b, dh:dh + 1, :])

    def prepare(src_ref, tile, slot):
        t0 = tile * tq
        for h in range(hg):
            rows = slice(h * tq, (h + 1) * tq)
            q_h = (_rms(src_ref[:, h * dh:(h + 1) * dh].astype(F32), qn_ref[...]) * scale).astype(BF16)
            qc_ref[slot, rows, 0:dh] = q_h
            qs_ref[slot, rows, 0:dh] = q_h
            qc_ref[slot, rows, dh:dh + AUG_WIDTH] = jnp.broadcast_to(
                lanes_ref[0, hg + h:hg + h + 1, :], (tq, AUG_WIDTH)).astype(BF16)

        s = lax.dot_general(kc_ref[0, 0], qc_ref[slot], contract_last, preferred_element_type=F32)
        c_idx = lax.broadcasted_iota(jnp.int32, (nc, cols), 0)
        t_c = t0 + (lax.broadcasted_iota(jnp.int32, (nc, cols), 1) & (tq - 1))
        mask = (c_idx * CMP_STRIDE + (CMP_LEN - 1)) <= t_c
        s = jnp.where(mask, s, NEG_INF)
        m_c = jnp.max(s, axis=0, keepdims=True)
        p_c = jnp.where(mask, jnp.exp2(s - m_c), 0.0)
        l_c = jnp.sum(p_c, axis=0, keepdims=True)
        p_c = p_c * jnp.where(l_c > 0.0, 1.0 / l_c, 0.0)
        ocmp_ref[slot] = jnp.dot(vcT_ref[0, 0], p_c.astype(BF16), preferred_element_type=F32)

        p_sum = p_c[:, 0:tq]
        for h in range(1, hg):
            p_sum = p_sum + p_c[:, h * tq:(h + 1) * tq]
        hi = p_sum.astype(BF16)
        rem = p_sum - hi.astype(F32)
        mid = rem.astype(BF16)
        lo = (rem - mid.astype(F32)).astype(BF16)
        ovT = ovT_ref[...]
        imp = (jnp.dot(ovT, hi, preferred_element_type=F32)
               + jnp.dot(ovT, mid, preferred_element_type=F32)
               + jnp.dot(ovT, lo, preferred_element_type=F32))
        blk = lax.broadcasted_iota(jnp.int32, (n_blk, tq), 0)
        t_b = t0 + lax.broadcasted_iota(jnp.int32, (n_blk, tq), 1)
        cur = jnp.right_shift(t_b, SEL_BLK.bit_length() - 1)
        valid = blk * SEL_BLK <= t_b
        forced = (blk == 0) | (blk == cur) | (blk == cur - 1)
        score = jnp.where(valid, imp + FORCE_BONUS * forced.astype(F32), NEG_INF)
        score_ref[...] = score
        rank = jnp.zeros((n_blk, tq), F32)
        for j in range(n_blk):
            sj = score_ref[j:j + 1, :]
            beats = (sj > score) | ((sj == score) & (blk > j))
            rank = rank + beats.astype(F32)
        block_bias = jnp.where(rank < float(min(N_SEL, n_blk)), 0.0, NEG_INF)

        assert 2 * SLOPE_PIECES <= 8
        ones_rows = (lax.broadcasted_iota(jnp.int32, (8, tq), 0) < 2 * SLOPE_PIECES).astype(F32)
        extra_t = jnp.concatenate([block_bias, ones_rows,
                                   jnp.zeros((AUG_WIDTH - n_blk - 8, tq), F32)], axis=0).T
        for h in range(hg):
            qs_ref[slot, h * tq:(h + 1) * tq, dh:dh + AUG_WIDTH] = (
                extra_t * lanes_ref[0, h:h + 1, :]).astype(BF16)

    def win_part(c, mask=None):
        return kw_ref[0, 0, c * cw:(c + 1) * cw, :], [vwT_ref[0, 0, c]], mask

    def sel_part(c, n=1, mask=None):
        return ks_ref[0, 0, c * ck:(c + n) * ck, :], [vsT_ref[0, 0, c + i] for i in range(n)], mask

    def tile_step(tile):
        slot = tile % 2
        below_diagonal = lambda: kmq_ref[...] <= 0
        win_parts = [win_part(tile, below_diagonal)]
        win_parts += [win_part(tile - j) for j in range(1, n_back) if tile - j >= 0]
        if tile - n_back >= 0:
            win_parts.append(win_part(tile - n_back, lambda: kmq_ref[...] > 0))
        sel_parts = [sel_part(tile, mask=below_diagonal)]
        sel_parts += [sel_part(c, n=min(2, tile - c)) for c in range(0, tile, 2)]
        sel_unit = len(win_parts) * len(groups)
        issue_scores(qc_ref.at[slot], win_parts, 0)
        issue_scores(qs_ref.at[slot], sel_parts, sel_unit)
        if tile + 1 < n_tiles:
            prepare(qnext_ref, tile + 1, 1 - slot)
        reset(WIN)
        softmax_update(WIN, win_parts, 0)
        reset(SEL)
        softmax_update(SEL, sel_parts, sel_unit)

        gate_ref[...] = jax.nn.sigmoid(gl_ref[...] + gb_ref[...]).T
        o_win, o_slc = result(WIN), result(SEL)
        for h in range(hg):
            cs = slice(h * tq, (h + 1) * tq)
            gate = lambda branch: gate_ref[h * N_BRANCH + branch:h * N_BRANCH + branch + 1, :]
            o_h = gate(0) * ocmp_ref[slot, :, cs] + gate(1) * o_slc[:, cs] + gate(2) * o_win[:, cs]
            o_ref[:, h * dh:(h + 1) * dh] = o_h.T.astype(o_ref.dtype)

    @pl.when(qi == 0)
    def _():
        prepare(q_ref, 0, 0)

    for tile in range(n_tiles):
        pl.when(qi == tile)(functools.partial(tile_step, tile))


def nsa_attention(z, gl, gate_b, q_norm, kcmp, vcmpT, ks, vsT, kw, vwT, *, B, T):
    G, hg, dh, tq = NSA_KV_HEADS, NSA_GROUP, HEAD_DIM, Q_TILE
    nq = T // tq
    nc = kcmp.shape[2]
    n_blk = T // SEL_BLK
    kdim = dh + AUG_WIDTH
    cols = hg * tq
    assert SEL_CHUNK == WIN_CHUNK == tq
    n_units = (WINDOW // WIN_CHUNK + 1 + 1 + nq // 2) * (cols // NSA_COLS_PER_DOT)
    assert SEL_CHUNK % SEL_BLK == 0 and T % SEL_CHUNK == 0 and gl.shape[1] == G * LANES
    assert SEL_CHUNK % tq == 0 and n_blk % 8 == 0 and n_blk + 8 <= AUG_WIDTH

    slopes = np.exp2(-8.0 * (np.arange(NSA_HEADS, dtype=np.float64) + 1.0) / NSA_HEADS).reshape(G, hg)
    lanes = np.zeros((G, 2 * hg, AUG_WIDTH), np.float32)
    lanes[:, :hg, :n_blk] = 1.0
    for i, piece in enumerate(_bf16_pieces(slopes * LOG2E, SLOPE_PIECES)):
        for lane in (n_blk + i, n_blk + SLOPE_PIECES + i):
            lanes[:, :hg, lane] = piece
            lanes[:, hg:, lane] = piece
    cs = np.arange(nc)[:, None] * CMP_STRIDE
    bs = np.arange(n_blk)[None, :] * SEL_BLK
    overlap = np.clip(np.minimum(cs + CMP_LEN, bs + SEL_BLK) - np.maximum(cs, bs), 0, None) / CMP_LEN
    overlap[(T - CMP_LEN) // CMP_STRIDE + 1:] = 0.0
    ovT = jnp.asarray(overlap.T, dtype=BF16)
    hc = NSA_COLS_PER_DOT
    kmq = jnp.asarray(np.arange(SEL_CHUNK)[:, None] - (np.arange(hc)[None, :] % tq), dtype=jnp.int32)

    row = lambda b, qi: b * nq + qi
    per_bg = lambda shape: pl.BlockSpec((1, 1) + shape, lambda b, g, qi: (b, g) + (0,) * len(shape))
    const = lambda shape: pl.BlockSpec(shape, lambda b, g, qi: (0,) * len(shape))
    return pl.pallas_call(
        functools.partial(_nsa_body, n_tiles=nq),
        grid=(B, G, nq),
        in_specs=[pl.BlockSpec((tq, hg * dh), lambda b, g, qi: (row(b, qi), g)),
                  pl.BlockSpec((tq, hg * dh), lambda b, g, qi: (row(b, jnp.minimum(qi + 1, nq - 1)), g)),
                  pl.BlockSpec((tq, LANES), lambda b, g, qi: (row(b, qi), g)),
                  pl.BlockSpec((1, LANES), lambda b, g, qi: (0, g)),
                  const((1, dh)),
                  pl.BlockSpec((1, 2 * hg, AUG_WIDTH), lambda b, g, qi: (g, 0, 0)),
                  per_bg((nc, kdim)), per_bg((dh, nc)),
                  per_bg((T, kdim)), per_bg((T // SEL_CHUNK, dh + V_EXTRA, SEL_CHUNK)),
                  per_bg((T, kdim)), per_bg((T // WIN_CHUNK, dh + V_EXTRA, WIN_CHUNK)),
                  const((n_blk, nc)), const((SEL_CHUNK, hc))],
        out_specs=pl.BlockSpec((tq, hg * dh), lambda b, g, qi: (row(b, qi), g)),
        out_shape=jax.ShapeDtypeStruct((B * T, NSA_WIDTH), BF16),
        scratch_shapes=[pltpu.VMEM((n_blk, tq), F32), pltpu.VMEM((LANES, tq), F32),
                        pltpu.VMEM((2, cols, kdim), BF16), pltpu.VMEM((2, cols, kdim), BF16),
                        pltpu.VMEM((2, 1, cols), F32),
                        pltpu.VMEM((2, dh + V_EXTRA, cols), F32), pltpu.VMEM((2, dh, cols), F32),
                        pltpu.VMEM((n_units, 2 * SEL_CHUNK, NSA_COLS_PER_DOT), F32)],
        compiler_params=_cparams("parallel", "parallel", "arbitrary"),
        name="nsa_attention",
    )(z, z, gl, gate_b, q_norm, jnp.asarray(lanes), kcmp, vcmpT, ks, vsT, kw, vwT, ovT, kmq)


def _conv_body(a_ref, b_ref, ah_ref, bh_ref, w_ref, cb_ref, lg_ref, lb_ref, o_ref,
               h_ref, g_ref, y_ref, *, tt):
    C = a_ref.shape[1]
    halo = ah_ref[...].astype(F32) * jax.nn.sigmoid(bh_ref[...].astype(F32))
    h_ref[0:CONV_HALO, :] = jnp.where(pl.program_id(1) == 0, 0.0, halo)
    h_ref[CONV_HALO:CONV_HALO + tt, :] = a_ref[...].astype(F32) * jax.nn.sigmoid(b_ref[...].astype(F32))
    h_ref[CONV_HALO + tt:CONV_HALO + tt + SUBLANES, :] = jnp.zeros((SUBLANES, C), F32)
    lead = CONV_HALO - (CONV_KERNEL - 1)
    rows = tt + SUBLANES
    for s in range(C // LANES):
        cs = slice(s * LANES, (s + 1) * LANES)
        acc = None
        for r in range(SUBLANES):
            group = None
            for o in range(lead, lead + CONV_KERNEL):
                if o % SUBLANES == r:
                    term = w_ref[o - lead:o - lead + 1, cs] * h_ref[o - r:o - r + rows, cs]
                    group = term if group is None else group + term
            if r == 0:
                piece = group[0:tt, :]
            else:
                g_ref[r] = group
                piece = g_ref[r, r:r + tt, :]
            acc = piece if acc is None else acc + piece
        y_ref[:, cs] = acc + cb_ref[:, cs]
    y = y_ref[...]
    mu = jnp.mean(y, axis=-1, keepdims=True)
    var = jnp.mean(jnp.square(y - mu), axis=-1, keepdims=True)
    yn = (y - mu) * lax.rsqrt(var + LN_EPS) * lg_ref[...] + lb_ref[...]
    o_ref[...] = _silu(yn).astype(o_ref.dtype)


def conformer_conv(z, conv_w, conv_b, ln_g, ln_b, *, B, T, col0, C, tt):
    nt = T // tt
    ca = col0 // C
    hb = tt // CONV_HALO
    row = lambda b, t: b * nt + t
    halo_row = lambda b, t: jnp.maximum(row(b, t) * hb - 1, 0)
    const = lambda shape: pl.BlockSpec(shape, lambda b, t: (0,) * len(shape))
    return pl.pallas_call(
        functools.partial(_conv_body, tt=tt),
        grid=(B, nt),
        in_specs=[pl.BlockSpec((tt, C), lambda b, t: (row(b, t), ca)),
                  pl.BlockSpec((tt, C), lambda b, t: (row(b, t), ca + 1)),
                  pl.BlockSpec((CONV_HALO, C), lambda b, t: (halo_row(b, t), ca)),
                  pl.BlockSpec((CONV_HALO, C), lambda b, t: (halo_row(b, t), ca + 1)),
                  const((CONV_KERNEL, C)), const((1, C)), const((1, C)), const((1, C))],
        out_specs=pl.BlockSpec((tt, C), lambda b, t: (row(b, t), 0)),
        out_shape=jax.ShapeDtypeStruct((B * T, C), BF16),
        scratch_shapes=[pltpu.VMEM((CONV_HALO + tt + SUBLANES, C), F32),
                        pltpu.VMEM((SUBLANES, tt + SUBLANES, LANES), F32),
                        pltpu.VMEM((tt, C), F32)],
        compiler_params=_cparams("parallel", "parallel"),
        name="conformer_conv",
    )(z, z, z, z, conv_w, conv_b, ln_g, ln_b)


def _mem_attn_body(x_ref, gq_ref, wq_ref, mem_ref, gkv_ref, wkv_ref, qn_ref, kn_ref, wo_ref, gn_ref,
                   o_ref, on_ref, k_scr, v_scr):
    dh = HEAD_DIM
    width = MEM_HEADS * dh
    scale = dh ** -0.5
    heads = [slice(h * dh, (h + 1) * dh) for h in range(MEM_HEADS)]

    @pl.when(pl.program_id(1) == 0)
    def _():
        kv = jnp.dot(_rms(mem_ref[...], gkv_ref[...]).astype(BF16), wkv_ref[...], preferred_element_type=F32)
        for cs in heads:
            k_scr[:, cs] = _rms(kv[:, cs], kn_ref[...]).astype(BF16)
        v_scr[...] = kv[:, width:2 * width].astype(BF16)

    q_all = jnp.dot(_rms(x_ref[...], gq_ref[...]).astype(BF16), wq_ref[...], preferred_element_type=F32)
    scores = []
    for cs in heads:
        q = (_rms(q_all[:, cs], qn_ref[...]) * scale).astype(BF16)
        scores.append(lax.dot_general(q, k_scr[:, cs], (((1,), (1,)), ((), ())), preferred_element_type=F32))
    outs = []
    for h, s in enumerate(scores):
        v = v_scr[:, heads[h]]
        p = jnp.exp(s - jnp.max(s, axis=-1, keepdims=True))
        inv = 1.0 / jnp.sum(p, axis=-1, keepdims=True)
        outs.append((jnp.dot(p.astype(BF16), v, preferred_element_type=F32) * inv).astype(BF16))
    o = jnp.concatenate(outs, axis=1)
    y = x_ref[...] + jnp.dot(o, wo_ref[...], preferred_element_type=F32)
    o_ref[...] = y
    on_ref[...] = _rms(y, gn_ref[...]).astype(on_ref.dtype)


def mem_attention(x, q_gain, w_mq, mem, kv_gain, w_mkv, mq_norm, mk_norm, w_mo, next_norm, *, B, T, M, tt):
    D = x.shape[1]
    width = MEM_HEADS * HEAD_DIM
    nt = T // tt
    row = lambda b, t: b * nt + t
    const = lambda shape: pl.BlockSpec(shape, lambda b, t: (0,) * len(shape))
    return pl.pallas_call(
        _mem_attn_body,
        grid=(B, nt),
        in_specs=[pl.BlockSpec((tt, D), lambda b, t: (row(b, t), 0)),
                  const((1, D)), const((D, width)),
                  pl.BlockSpec((M, D), lambda b, t: (b, 0)),
                  const((1, D)), const((D, 2 * width)),
                  const((1, HEAD_DIM)), const((1, HEAD_DIM)), const((width, D)),
                  const((1, D))],
        out_specs=[pl.BlockSpec((tt, D), lambda b, t: (row(b, t), 0)),
                   pl.BlockSpec((tt, D), lambda b, t: (row(b, t), 0))],
        out_shape=[jax.ShapeDtypeStruct((B * T, D), F32),
                   jax.ShapeDtypeStruct((B * T, D), BF16)],
        scratch_shapes=[pltpu.VMEM((M, width), BF16), pltpu.VMEM((M, width), BF16)],
        compiler_params=_cparams("parallel", "arbitrary"),
        name="mem_attention",
    )(x, q_gain, w_mq, mem, kv_gain, w_mkv, mq_norm, mk_norm, w_mo, next_norm)


def kernel(x, mem, norm_mix, w_in, gate_b, q_norm, k_norm_cmp, k_norm_slc, k_norm_win, cmp_pos_k, cmp_pos_v, cmp_k_w1, cmp_k_w2, cmp_v_w1, cmp_v_w2, conv_w, conv_b, conv_ln_g, conv_ln_b, w_out, norm_mem_q, norm_mem_kv, w_mq, w_mk, w_mv, mq_norm, mk_norm, w_mo, norm_ffn, w_gate, w_up, w_down):
    B, T, D = x.shape
    M = mem.shape[1]
    depth = w_in.shape[0]
    G, hg = NSA_KV_HEADS, NSA_GROUP
    C = D - NSA_WIDTH
    n_gate = N_BRANCH * NSA_HEADS
    kv_all = 6 * KV_WIDTH
    assert w_in.shape[2] == NSA_WIDTH + kv_all + n_gate + 2 * C
    assert C == NSA_WIDTH, "column blocking below assumes equal NSA and conv widths"

    row1 = lambda v: v.reshape(1, -1)
    xf = x.reshape(B * T, D)
    memf = mem.reshape(B * M, D)
    u0 = NSA_WIDTH + kv_all + n_gate
    for l in range(depth):
        w = w_in[l]
        w_t = jnp.swapaxes(w, 0, 1)
        wg = w[:, NSA_WIDTH + kv_all:u0].reshape(D, G, hg * N_BRANCH)
        wg = jnp.pad(wg, ((0, 0), (0, 0), (0, LANES - hg * N_BRANCH))).reshape(D, G * LANES)
        gb = jnp.pad(gate_b[l].reshape(G, hg * N_BRANCH), ((0, 0), (0, LANES - hg * N_BRANCH))).reshape(1, G * LANES)

        xn = rmsnorm_bf16(xf, row1(norm_mix[l]), tm=TILE_RMSNORM)
        z, gl = in_projection(xn, w_t, wg, tm=TILE_IN_PROJ[0], tn=TILE_IN_PROJ[1],
                              segments=((0, NSA_WIDTH), (u0, 2 * C), (NSA_WIDTH, kv_all)))
        kv_col0 = NSA_WIDTH + 2 * C
        kc, vc, ks, vsT, kw, vwT = kv_prep(z, row1(k_norm_slc[l]), row1(k_norm_win[l]),
                                           B=B, T=T, col0=kv_col0, tt=TILE_KV_PREP)
        kcmp, vcmpT = compress_kv(kc, vc, cmp_pos_k[l], cmp_pos_v[l],
                                  cmp_k_w1[l].astype(BF16), cmp_k_w2[l].astype(BF16),
                                  cmp_v_w1[l].astype(BF16), cmp_v_w2[l].astype(BF16), row1(k_norm_cmp[l]),
                                  n_blk=T // SEL_BLK)
        o_nsa = nsa_attention(z, gl, gb, row1(q_norm[l]), kcmp, vcmpT, ks, vsT, kw, vwT, B=B, T=T)
        o_conv = conformer_conv(z, conv_w[l], row1(conv_b[l]), row1(conv_ln_g[l]), row1(conv_ln_b[l]),
                                B=B, T=T, col0=NSA_WIDTH, C=C, tt=TILE_CONV)
        xf = concat_matmul_residual(o_nsa, o_conv, w_out[l], xf, tm=TILE_OUT_PROJ[0], tn=TILE_OUT_PROJ[1])

        w_mkv = jnp.concatenate([w_mk[l], w_mv[l]], axis=1).astype(BF16)
        xf, xn = mem_attention(xf, row1(norm_mem_q[l]), w_mq[l].astype(BF16),
                               memf, row1(norm_mem_kv[l]), w_mkv,
                               row1(mq_norm[l]), row1(mk_norm[l]), w_mo[l].astype(BF16),
                               row1(norm_ffn[l]), B=B, T=T, M=M, tt=TILE_MEM_ATTN)

        hidden, w_down_bf16 = swiglu_up(xn, w_gate[l], w_up[l], w_down[l],
                                        tm=TILE_FFN_UP[0], tn=TILE_FFN_UP[1])
        xf = matmul_residual(hidden, w_down_bf16, xf, tm=TILE_FFN_DOWN[0], tn=TILE_FFN_DOWN[1])
    return xf.reshape(B, T, D)
```

```python
import functools

import ml_dtypes
import numpy as np
import jax
import jax.numpy as jnp
from jax import lax
from jax.experimental import pallas as pl
from jax.experimental.pallas import tpu as pltpu

F32 = jnp.float32
BF16 = jnp.bfloat16

HEAD_DIM = 128
NSA_HEADS = 8
NSA_KV_HEADS = 2
NSA_GROUP = NSA_HEADS // NSA_KV_HEADS
NSA_WIDTH = NSA_HEADS * HEAD_DIM
KV_WIDTH = NSA_KV_HEADS * HEAD_DIM
N_BRANCH = 3
CONV_KERNEL = 31
CMP_LEN = 32
CMP_STRIDE = 16
SEL_BLK = 64
N_SEL = 16
WINDOW = 512
MEM_HEADS = 4
NEG_INF = -1e30
FORCE_BONUS = 1e3
RMS_EPS = 1e-6
LN_EPS = 1e-5

LANES = 128
SUBLANES = 8
VMEM_LIMIT_BYTES = 56 * 1024 * 1024

Q_TILE = 256
SEL_CHUNK = 256
WIN_CHUNK = 256
NSA_COLS_PER_DOT = 512
CONV_HALO = 32

TILE_RMSNORM = 512
TILE_IN_PROJ = (2048, 512)
TILE_KV_PREP = 1024
TILE_CONV = 256
TILE_OUT_PROJ = (2048, 512)
TILE_MEM_ATTN = 512
TILE_FFN_UP = (1024, 512)
TILE_FFN_DOWN = (1024, 512)


def _cparams(*sem):
    return pltpu.CompilerParams(dimension_semantics=sem, vmem_limit_bytes=VMEM_LIMIT_BYTES)


def _rms(x, g, eps=RMS_EPS):
    ms = jnp.mean(x * x, axis=-1, keepdims=True)
    return x * lax.rsqrt(ms + eps) * g


def _silu(x):
    return x * jax.nn.sigmoid(x)


def _rmsnorm_body(x_ref, g_ref, o_ref):
    o_ref[...] = _rms(x_ref[...], g_ref[...]).astype(o_ref.dtype)


def rmsnorm_bf16(x, g, *, tm):
    M, K = x.shape
    return pl.pallas_call(
        _rmsnorm_body,
        grid=(M // tm,),
        in_specs=[pl.BlockSpec((tm, K), lambda i: (i, 0)), pl.BlockSpec((1, K), lambda i: (0, 0))],
        out_specs=pl.BlockSpec((tm, K), lambda i: (i, 0)),
        out_shape=jax.ShapeDtypeStruct((M, K), BF16),
        compiler_params=_cparams("parallel"),
        name="rmsnorm",
    )(x, g)


def _in_proj_body(xn_ref, wt_ref, wg_ref, o_ref, og_ref):
    @pl.when(pl.program_id(1) == 0)
    def _():
        og_ref[...] = jnp.dot(xn_ref[...], wg_ref[...].astype(BF16), preferred_element_type=F32)

    o_ref[...] = lax.dot_general(xn_ref[...], wt_ref[...].astype(BF16), (((1,), (1,)), ((), ())),
                                 preferred_element_type=F32).astype(o_ref.dtype)


def _swiglu_body(x_ref, wg_ref, wu_ref, wd_ref, o_ref, wd_o):
    x = x_ref[...]
    a = jnp.dot(x, wg_ref[...], preferred_element_type=F32)
    b = jnp.dot(x, wu_ref[...], preferred_element_type=F32)
    o_ref[...] = (_silu(a) * b).astype(o_ref.dtype)

    @pl.when(pl.program_id(0) == 0)
    def _():
        wd_o[...] = wd_ref[...].astype(wd_o.dtype)


def in_projection(xn, w_t, wg, *, segments, tm, tn):
    M, K = xn.shape
    NG = wg.shape[1]
    assert all(start % SUBLANES == 0 and size % tn == 0 for start, size in segments)
    N = sum(size for _, size in segments)

    def row_start(i, j):
        group, first = 0, 0
        for seg_start, size in segments:
            group = jnp.where(j >= first, seg_start // SUBLANES + (j - first) * (tn // SUBLANES), group)
            first += size // tn
        return (group * SUBLANES, 0)

    return pl.pallas_call(
        _in_proj_body,
        grid=(M // tm, N // tn),
        in_specs=[pl.BlockSpec((tm, K), lambda i, j: (i, 0)),
                  pl.BlockSpec((pl.Element(tn), pl.Element(K)), row_start),
                  pl.BlockSpec((K, NG), lambda i, j: (0, 0))],
        out_specs=[pl.BlockSpec((tm, tn), lambda i, j: (i, j)),
                   pl.BlockSpec((tm, NG), lambda i, j: (i, 0))],
        out_shape=[jax.ShapeDtypeStruct((M, N), BF16),
                   jax.ShapeDtypeStruct((M, NG), F32)],
        compiler_params=_cparams("parallel", "arbitrary"),
        name="in_proj",
    )(xn, w_t, wg)


def swiglu_up(xn, w_gate, w_up, w_down, *, tm, tn):
    assert w_gate.dtype == w_up.dtype == BF16
    M, K = xn.shape
    N = w_gate.shape[1]
    D = w_down.shape[1]
    nj = N // tn
    assert w_down.shape[0] == N
    down_block = lambda i, j: (jnp.where(i == 0, j, nj - 1), 0)
    return pl.pallas_call(
        _swiglu_body,
        grid=(M // tm, nj),
        in_specs=[pl.BlockSpec((tm, K), lambda i, j: (i, 0)),
                  pl.BlockSpec((K, tn), lambda i, j: (0, j)),
                  pl.BlockSpec((K, tn), lambda i, j: (0, j)),
                  pl.BlockSpec((tn, D), down_block)],
        out_specs=[pl.BlockSpec((tm, tn), lambda i, j: (i, j)),
                   pl.BlockSpec((tn, D), down_block)],
        out_shape=[jax.ShapeDtypeStruct((M, N), BF16),
                   jax.ShapeDtypeStruct((N, D), BF16)],
        compiler_params=_cparams("arbitrary", "arbitrary"),
        name="ffn_up",
    )(xn, w_gate, w_up, w_down)


def _mm_res_body(a_ref, w_ref, r_ref, o_ref):
    o_ref[...] = r_ref[...] + jnp.dot(a_ref[...], w_ref[...], preferred_element_type=F32)


def matmul_residual(a, w, res, *, tm, tn):
    M, K = a.shape
    N = w.shape[1]
    return pl.pallas_call(
        _mm_res_body,
        grid=(M // tm, N // tn),
        in_specs=[pl.BlockSpec((tm, K), lambda i, j: (i, 0)),
                  pl.BlockSpec((K, tn), lambda i, j: (0, j)),
                  pl.BlockSpec((tm, tn), lambda i, j: (i, j))],
        out_specs=pl.BlockSpec((tm, tn), lambda i, j: (i, j)),
        out_shape=jax.ShapeDtypeStruct((M, N), F32),
        compiler_params=_cparams("parallel", "arbitrary"),
        name="matmul_residual",
    )(a, w, res)


def _mm2_res_body(a_ref, b_ref, wa_ref, wb_ref, r_ref, o_ref):
    acc = jnp.dot(a_ref[...], wa_ref[...].astype(BF16), preferred_element_type=F32)
    acc = acc + jnp.dot(b_ref[...], wb_ref[...].astype(BF16), preferred_element_type=F32)
    o_ref[...] = r_ref[...] + acc


def concat_matmul_residual(a, b, w, res, *, tm, tn):
    M, Ka = a.shape
    Kb = b.shape[1]
    assert Ka == Kb and w.shape[0] == Ka + Kb
    N = w.shape[1]
    return pl.pallas_call(
        _mm2_res_body,
        grid=(M // tm, N // tn),
        in_specs=[pl.BlockSpec((tm, Ka), lambda i, j: (i, 0)),
                  pl.BlockSpec((tm, Kb), lambda i, j: (i, 0)),
                  pl.BlockSpec((Ka, tn), lambda i, j: (0, j)),
                  pl.BlockSpec((Kb, tn), lambda i, j: (1, j)),
                  pl.BlockSpec((tm, tn), lambda i, j: (i, j))],
        out_specs=pl.BlockSpec((tm, tn), lambda i, j: (i, j)),
        out_shape=jax.ShapeDtypeStruct((M, N), F32),
        compiler_params=_cparams("parallel", "arbitrary"),
        name="out_proj",
    )(a, b, w, w, res)


AUG_WIDTH = LANES
V_EXTRA = 16
SLOPE_PIECES = 4
LOG2E = float(np.log2(np.e))


def _bf16_pieces(x, n):
    out, rem = [], np.asarray(x, np.float64)
    for _ in range(n):
        piece = rem.astype(ml_dtypes.bfloat16).astype(np.float64)
        out.append(piece.astype(np.float32))
        rem = rem - piece
    return out


def _position_lanes(pos_hi, pos_lo, onehot, n_blk, shape):
    lane = lax.broadcasted_iota(jnp.int32, shape, 1)
    p = SLOPE_PIECES
    extra = jnp.where((lane >= n_blk) & (lane < n_blk + p), pos_hi,
                      jnp.where((lane >= n_blk + p) & (lane < n_blk + 2 * p), pos_lo, 0.0))
    if onehot is not None:
        extra = jnp.where(lane < n_blk, onehot, extra)
    return extra.astype(BF16)


def _kv_prep_body(c_ref, s_ref, w_ref, kns_ref, knw_ref,
                  kc_o, vc_o, ks_o, vsT_o, kw_o, vwT_o, slab_ref, *, tt, n_blk):
    dh = HEAD_DIM
    shape = (tt, AUG_WIDTH)
    pos = pl.program_id(1) * tt + lax.broadcasted_iota(jnp.int32, shape, 0)
    lane = lax.broadcasted_iota(jnp.int32, shape, 1)
    blk = jnp.right_shift(pos, SEL_BLK.bit_length() - 1)
    pos_hi = (blk * SEL_BLK).astype(F32)
    pos_lo = (pos & (SEL_BLK - 1)).astype(F32)
    aug_sel = _position_lanes(pos_hi, pos_lo, (blk == lane).astype(F32), n_blk, shape)
    aug_win = _position_lanes(pos_hi, pos_lo, None, n_blk, shape)
    for g in range(NSA_KV_HEADS):
        k_cols = slice(g * dh, (g + 1) * dh)
        v_cols = slice(KV_WIDTH + g * dh, KV_WIDTH + (g + 1) * dh)
        for slab, (cols, out) in enumerate(((k_cols, kc_o), (v_cols, vc_o))):
            slab_ref[slab] = c_ref[:, cols].astype(F32)
            for i in range(CMP_STRIDE):
                rows = pl.ds(i, tt // CMP_STRIDE, stride=CMP_STRIDE)
                out[0, g, :, i * dh:(i + 1) * dh] = slab_ref[slab, rows, :]
        ks_o[0, g, :, 0:dh] = _rms(s_ref[:, k_cols].astype(F32), kns_ref[...]).astype(BF16)
        ks_o[0, g, :, dh:dh + AUG_WIDTH] = aug_sel
        kw_o[0, g, :, 0:dh] = _rms(w_ref[:, k_cols].astype(F32), knw_ref[...]).astype(BF16)
        kw_o[0, g, :, dh:dh + AUG_WIDTH] = aug_win
        for v_ref, out, chunk in ((s_ref, vsT_o, SEL_CHUNK), (w_ref, vwT_o, WIN_CHUNK)):
            vT = v_ref[:, v_cols].astype(F32).T.astype(BF16)
            ones_first = (lax.broadcasted_iota(jnp.int32, (V_EXTRA, chunk), 0) == 0).astype(BF16)
            for c in range(tt // chunk):
                out[0, g, c, 0:dh, :] = vT[:, c * chunk:(c + 1) * chunk]
                out[0, g, c, dh:dh + V_EXTRA, :] = ones_first


def kv_prep(z, kn_slc, kn_win, *, B, T, col0, tt):
    G, dh = NSA_KV_HEADS, HEAD_DIM
    nt = T // tt
    n_blk = T // SEL_BLK
    assert n_blk + 2 * SLOPE_PIECES <= AUG_WIDTH
    cb = col0 // (2 * KV_WIDTH)
    row = lambda b, t: b * nt + t
    kdim = dh + AUG_WIDTH
    return pl.pallas_call(
        functools.partial(_kv_prep_body, tt=tt, n_blk=n_blk),
        grid=(B, nt),
        in_specs=[pl.BlockSpec((tt, 2 * KV_WIDTH), lambda b, t: (row(b, t), cb)),
                  pl.BlockSpec((tt, 2 * KV_WIDTH), lambda b, t: (row(b, t), cb + 1)),
                  pl.BlockSpec((tt, 2 * KV_WIDTH), lambda b, t: (row(b, t), cb + 2)),
                  pl.BlockSpec((1, dh), lambda b, t: (0, 0)),
                  pl.BlockSpec((1, dh), lambda b, t: (0, 0))],
        out_specs=[pl.BlockSpec((1, G, tt // CMP_STRIDE, CMP_STRIDE * dh), lambda b, t: (b, 0, t, 0)),
                   pl.BlockSpec((1, G, tt // CMP_STRIDE, CMP_STRIDE * dh), lambda b, t: (b, 0, t, 0)),
                   pl.BlockSpec((1, G, tt, kdim), lambda b, t: (b, 0, t, 0)),
                   pl.BlockSpec((1, G, tt // SEL_CHUNK, dh + V_EXTRA, SEL_CHUNK), lambda b, t: (b, 0, t, 0, 0)),
                   pl.BlockSpec((1, G, tt, kdim), lambda b, t: (b, 0, t, 0)),
                   pl.BlockSpec((1, G, tt // WIN_CHUNK, dh + V_EXTRA, WIN_CHUNK), lambda b, t: (b, 0, t, 0, 0))],
        out_shape=[jax.ShapeDtypeStruct((B, G, T // CMP_STRIDE, CMP_STRIDE * dh), F32),
                   jax.ShapeDtypeStruct((B, G, T // CMP_STRIDE, CMP_STRIDE * dh), F32),
                   jax.ShapeDtypeStruct((B, G, T, kdim), BF16),
                   jax.ShapeDtypeStruct((B, G, T // SEL_CHUNK, dh + V_EXTRA, SEL_CHUNK), BF16),
                   jax.ShapeDtypeStruct((B, G, T, kdim), BF16),
                   jax.ShapeDtypeStruct((B, G, T // WIN_CHUNK, dh + V_EXTRA, WIN_CHUNK), BF16)],
        scratch_shapes=[pltpu.VMEM((2, tt, dh), F32)],
        compiler_params=_cparams("parallel", "parallel"),
        name="kv_prep",
    )(z, z, z, kn_slc, kn_win)


def _compress_body(xk_ref, xv_ref, pk_ref, pv_ref, w1k_ref, w2k_ref, w1v_ref, w2v_ref, kn_ref,
                   kc_o, vcT_o, *, n_blk):
    dh = HEAD_DIM

    def phi(x_ref, p_ref, w1_ref, w2_ref):
        x = x_ref[0, 0]
        first = jnp.dot((x + p_ref[0:1, :]).astype(BF16), w1_ref[0], preferred_element_type=F32)
        second = jnp.dot((x + p_ref[1:2, :]).astype(BF16), w1_ref[1], preferred_element_type=F32)
        n = second.shape[0]
        h = first + pltpu.roll(second, n - 1, 0)
        return jnp.dot(_silu(h).astype(BF16), w2_ref[...], preferred_element_type=F32)

    k = _rms(phi(xk_ref, pk_ref, w1k_ref, w2k_ref), kn_ref[...])
    nc = k.shape[0]
    shape = (nc, AUG_WIDTH)
    start = (lax.broadcasted_iota(jnp.int32, shape, 0) * CMP_STRIDE).astype(F32)
    centre = jnp.full(shape, (CMP_LEN - 1) / 2.0, F32)
    kc_o[0, 0, :, 0:dh] = k.astype(BF16)
    kc_o[0, 0, :, dh:dh + AUG_WIDTH] = _position_lanes(start, centre, None, n_blk, shape)
    vcT_o[0, 0] = phi(xv_ref, pv_ref, w1v_ref, w2v_ref).T.astype(BF16)


def compress_kv(xk, xv, pos_k, pos_v, w1k, w2k, w1v, w2v, kn_cmp, *, n_blk):
    B, G, nc, half = xk.shape
    dh = half // CMP_STRIDE
    assert CMP_LEN == 2 * CMP_STRIDE
    x_spec = pl.BlockSpec((1, 1, nc, half), lambda b, g: (b, g, 0, 0))
    full = lambda shape: pl.BlockSpec(shape, lambda b, g: (0,) * len(shape))
    return pl.pallas_call(
        functools.partial(_compress_body, n_blk=n_blk),
        grid=(B, G),
        in_specs=[x_spec, x_spec, full((2, half)), full((2, half)),
                  full((2, half, dh)), full((dh, dh)), full((2, half, dh)), full((dh, dh)),
                  full((1, dh))],
        out_specs=[pl.BlockSpec((1, 1, nc, dh + AUG_WIDTH), lambda b, g: (b, g, 0, 0)),
                   pl.BlockSpec((1, 1, dh, nc), lambda b, g: (b, g, 0, 0))],
        out_shape=[jax.ShapeDtypeStruct((B, G, nc, dh + AUG_WIDTH), BF16),
                   jax.ShapeDtypeStruct((B, G, dh, nc), BF16)],
        compiler_params=_cparams("parallel", "parallel"),
        name="compress_kv",
    )(xk, xv, pos_k.reshape(2, half), pos_v.reshape(2, half),
      w1k.reshape(2, half, dh), w2k, w1v.reshape(2, half, dh), w2v, kn_cmp)


def _nsa_body(q_ref, qnext_ref, gl_ref, gb_ref, qn_ref, lanes_ref, kc_ref, vcT_ref,
              ks_ref, vsT_ref, kw_ref, vwT_ref, ovT_ref, kmq_ref, cast_a_ref, cast_b_ref,
              o_ref, cast_a_o, cast_b_o,
              score_ref, gate_ref, qc_ref, qs_ref, m_ref, acc_ref, ocmp_ref, sbuf_ref, *, n_tiles):
    dh, hg, tq = HEAD_DIM, NSA_GROUP, Q_TILE
    cols = hg * tq
    qi = pl.program_id(2)
    contract_last = (((1,), (1,)), ((), ()))
    n_blk = ovT_ref.shape[0]

    scale = dh ** -0.5 * LOG2E
    hc = kmq_ref.shape[1]
    assert cols % hc == 0 and hc % tq == 0
    groups = [slice(i * hc, (i + 1) * hc) for i in range(cols // hc)]
    cw = ck = tq
    assert WIN_CHUNK == SEL_CHUNK == tq == kmq_ref.shape[0] and WINDOW % cw == 0
    n_back = WINDOW // cw
    nc = kc_ref.shape[2]
    WIN, SEL = 0, 1

    def reset(b):
        m_ref[b] = jnp.full((1, cols), NEG_INF, F32)
        acc_ref[b] = jnp.zeros((dh + V_EXTRA, cols), F32)

    def issue_scores(queries_ref, parts, unit):
        for k, _, _ in parts:
            for cs in groups:
                sbuf_ref[unit, 0:k.shape[0], :] = lax.dot_general(
                    k, queries_ref[cs, :], contract_last, preferred_element_type=F32)
                unit += 1

    def softmax_update(b, parts, unit):
        for (k, v_pieces, mask) in parts:
            for cs in groups:
                s_ref = sbuf_ref.at[unit, 0:k.shape[0], :]
                unit += 1
                if mask is not None:
                    s_ref[...] = jnp.where(mask(), s_ref[...], NEG_INF)
                m_old = m_ref[b, :, cs]
                m_new = jnp.maximum(m_old, jnp.max(s_ref[...], axis=0, keepdims=True))
                alpha = jnp.exp2(m_old - m_new)
                p = jnp.exp2(s_ref[...] - m_new).astype(BF16)
                m_ref[b, :, cs] = m_new
                pv = None
                for i, vT in enumerate(v_pieces):
                    n = vT.shape[1]
                    d = jnp.dot(vT, p[i * n:(i + 1) * n, :], preferred_element_type=F32)
                    pv = d if pv is None else pv + d
                acc_ref[b, :, cs] = alpha * acc_ref[b, :, cs] + pv

    def result(b):
        return acc_ref[b, 0:dh, :] * (1.0 / acc_ref[b, dh:dh + 1, :])

    def prepare(src_ref, tile, slot):
        t0 = tile * tq
        for h in range(hg):
            rows = slice(h * tq, (h + 1) * tq)
            q_h = (_rms(src_ref[:, h * dh:(h + 1) * dh].astype(F32), qn_ref[...]) * scale).astype(BF16)
            qc_ref[slot, rows, 0:dh] = q_h
            qs_ref[slot, rows, 0:dh] = q_h
            qc_ref[slot, rows, dh:dh + AUG_WIDTH] = jnp.broadcast_to(
                lanes_ref[0, hg + h:hg + h + 1, :], (tq, AUG_WIDTH)).astype(BF16)

        s = lax.dot_general(kc_ref[0, 0], qc_ref[slot], contract_last, preferred_element_type=F32)
        c_idx = lax.broadcasted_iota(jnp.int32, (nc, cols), 0)
        t_c = t0 + (lax.broadcasted_iota(jnp.int32, (nc, cols), 1) & (tq - 1))
        mask = (c_idx * CMP_STRIDE + (CMP_LEN - 1)) <= t_c
        s = jnp.where(mask, s, NEG_INF)
        m_c = jnp.max(s, axis=0, keepdims=True)
        p_c = jnp.where(mask, jnp.exp2(s - m_c), 0.0)
        l_c = jnp.sum(p_c, axis=0, keepdims=True)
        p_c = p_c * jnp.where(l_c > 0.0, 1.0 / l_c, 0.0)
        ocmp_ref[slot] = jnp.dot(vcT_ref[0, 0], p_c.astype(BF16), preferred_element_type=F32)

        p_sum = p_c[:, 0:tq]
        for h in range(1, hg):
            p_sum = p_sum + p_c[:, h * tq:(h + 1) * tq]
        hi = p_sum.astype(BF16)
        rem = p_sum - hi.astype(F32)
        mid = rem.astype(BF16)
        lo = (rem - mid.astype(F32)).astype(BF16)
        ovT = ovT_ref[...]
        imp = (jnp.dot(ovT, hi, preferred_element_type=F32)
               + jnp.dot(ovT, mid, preferred_element_type=F32)
               + jnp.dot(ovT, lo, preferred_element_type=F32))
        blk = lax.broadcasted_iota(jnp.int32, (n_blk, tq), 0)
        t_b = t0 + lax.broadcasted_iota(jnp.int32, (n_blk, tq), 1)
        cur = jnp.right_shift(t_b, SEL_BLK.bit_length() - 1)
        valid = blk * SEL_BLK <= t_b
        forced = (blk == 0) | (blk == cur) | (blk == cur - 1)
        score = jnp.where(valid, imp + FORCE_BONUS * forced.astype(F32), NEG_INF)
        score_ref[...] = score
        rank = jnp.zeros((n_blk, tq), F32)
        for j in range(n_blk):
            sj = score_ref[j:j + 1, :]
            beats = (sj > score) | ((sj == score) & (blk > j))
            rank = rank + beats.astype(F32)
        block_bias = jnp.where(rank < float(min(N_SEL, n_blk)), 0.0, NEG_INF)

        assert 2 * SLOPE_PIECES <= 8
        ones_rows = (lax.broadcasted_iota(jnp.int32, (8, tq), 0) < 2 * SLOPE_PIECES).astype(F32)
        extra_t = jnp.concatenate([block_bias, ones_rows,
                                   jnp.zeros((AUG_WIDTH - n_blk - 8, tq), F32)], axis=0).T
        for h in range(hg):
            qs_ref[slot, h * tq:(h + 1) * tq, dh:dh + AUG_WIDTH] = (
                extra_t * lanes_ref[0, h:h + 1, :]).astype(BF16)

    def win_part(c, mask=None):
        return kw_ref[0, 0, c * cw:(c + 1) * cw, :], [vwT_ref[0, 0, c]], mask

    def sel_part(c, n=1, mask=None):
        return ks_ref[0, 0, c * ck:(c + n) * ck, :], [vsT_ref[0, 0, c + i] for i in range(n)], mask

    def tile_step(tile):
        slot = tile % 2
        below_diagonal = lambda: kmq_ref[...] <= 0
        win_parts = [win_part(tile, below_diagonal)]
        win_parts += [win_part(tile - j) for j in range(1, n_back) if tile - j >= 0]
        if tile - n_back >= 0:
            win_parts.append(win_part(tile - n_back, lambda: kmq_ref[...] > 0))
        sel_parts = [sel_part(tile, mask=below_diagonal)]
        sel_parts += [sel_part(c, n=min(2, tile - c)) for c in range(0, tile, 2)]
        sel_unit = len(win_parts) * len(groups)
        issue_scores(qc_ref.at[slot], win_parts, 0)
        issue_scores(qs_ref.at[slot], sel_parts, sel_unit)
        if tile + 1 < n_tiles:
            prepare(qnext_ref, tile + 1, 1 - slot)
        reset(WIN)
        softmax_update(WIN, win_parts, 0)
        reset(SEL)
        softmax_update(SEL, sel_parts, sel_unit)

        gate_ref[...] = jax.nn.sigmoid(gl_ref[...] + gb_ref[...]).T
        o_win, o_slc = result(WIN), result(SEL)
        for h in range(hg):
            cs = slice(h * tq, (h + 1) * tq)
            gate = lambda branch: gate_ref[h * N_BRANCH + branch:h * N_BRANCH + branch + 1, :]
            o_h = gate(0) * ocmp_ref[slot, :, cs] + gate(1) * o_slc[:, cs] + gate(2) * o_win[:, cs]
            o_ref[:, h * dh:(h + 1) * dh] = o_h.T.astype(o_ref.dtype)

    cast_a_o[...] = cast_a_ref[...].astype(cast_a_o.dtype)
    cast_b_o[...] = cast_b_ref[...].astype(cast_b_o.dtype)

    @pl.when(qi == 0)
    def _():
        prepare(q_ref, 0, 0)

    for tile in range(n_tiles):
        pl.when(qi == tile)(functools.partial(tile_step, tile))


def nsa_attention(z, gl, gate_b, q_norm, kcmp, vcmpT, ks, vsT, kw, vwT, cast_a, cast_b, *, B, T):
    G, hg, dh, tq = NSA_KV_HEADS, NSA_GROUP, HEAD_DIM, Q_TILE
    nq = T // tq
    nc = kcmp.shape[2]
    n_blk = T // SEL_BLK
    kdim = dh + AUG_WIDTH
    cols = hg * tq
    assert SEL_CHUNK == WIN_CHUNK == tq
    n_units = (WINDOW // WIN_CHUNK + 1 + 1 + nq // 2) * (cols // NSA_COLS_PER_DOT)
    assert SEL_CHUNK % SEL_BLK == 0 and T % SEL_CHUNK == 0 and gl.shape[1] == G * LANES
    assert SEL_CHUNK % tq == 0 and n_blk % 8 == 0 and n_blk + 8 <= AUG_WIDTH

    slopes = np.exp2(-8.0 * (np.arange(NSA_HEADS, dtype=np.float64) + 1.0) / NSA_HEADS).reshape(G, hg)
    lanes = np.zeros((G, 2 * hg, AUG_WIDTH), np.float32)
    lanes[:, :hg, :n_blk] = 1.0
    for i, piece in enumerate(_bf16_pieces(slopes * LOG2E, SLOPE_PIECES)):
        for lane in (n_blk + i, n_blk + SLOPE_PIECES + i):
            lanes[:, :hg, lane] = piece
            lanes[:, hg:, lane] = piece
    cs = np.arange(nc)[:, None] * CMP_STRIDE
    bs = np.arange(n_blk)[None, :] * SEL_BLK
    overlap = np.clip(np.minimum(cs + CMP_LEN, bs + SEL_BLK) - np.maximum(cs, bs), 0, None) / CMP_LEN
    overlap[(T - CMP_LEN) // CMP_STRIDE + 1:] = 0.0
    ovT = jnp.asarray(overlap.T, dtype=BF16)
    hc = NSA_COLS_PER_DOT
    kmq = jnp.asarray(np.arange(SEL_CHUNK)[:, None] - (np.arange(hc)[None, :] % tq), dtype=jnp.int32)

    row = lambda b, qi: b * nq + qi
    per_bg = lambda shape: pl.BlockSpec((1, 1) + shape, lambda b, g, qi: (b, g) + (0,) * len(shape))
    const = lambda shape: pl.BlockSpec(shape, lambda b, g, qi: (0,) * len(shape))
    n_steps = B * G * nq
    assert cast_a.shape[0] % n_steps == 0 and cast_b.shape[0] % n_steps == 0
    slab = lambda w: pl.BlockSpec((w.shape[0] // n_steps, w.shape[1]),
                                  lambda b, g, qi: ((b * G + g) * nq + qi, 0))
    return pl.pallas_call(
        functools.partial(_nsa_body, n_tiles=nq),
        grid=(B, G, nq),
        in_specs=[pl.BlockSpec((tq, hg * dh), lambda b, g, qi: (row(b, qi), g)),
                  pl.BlockSpec((tq, hg * dh), lambda b, g, qi: (row(b, jnp.minimum(qi + 1, nq - 1)), g)),
                  pl.BlockSpec((tq, LANES), lambda b, g, qi: (row(b, qi), g)),
                  pl.BlockSpec((1, LANES), lambda b, g, qi: (0, g)),
                  const((1, dh)),
                  pl.BlockSpec((1, 2 * hg, AUG_WIDTH), lambda b, g, qi: (g, 0, 0)),
                  per_bg((nc, kdim)), per_bg((dh, nc)),
                  per_bg((T, kdim)), per_bg((T // SEL_CHUNK, dh + V_EXTRA, SEL_CHUNK)),
                  per_bg((T, kdim)), per_bg((T // WIN_CHUNK, dh + V_EXTRA, WIN_CHUNK)),
                  const((n_blk, nc)), const((SEL_CHUNK, hc)), slab(cast_a), slab(cast_b)],
        out_specs=[pl.BlockSpec((tq, hg * dh), lambda b, g, qi: (row(b, qi), g)),
                   slab(cast_a), slab(cast_b)],
        out_shape=[jax.ShapeDtypeStruct((B * T, NSA_WIDTH), BF16),
                   jax.ShapeDtypeStruct(cast_a.shape, BF16), jax.ShapeDtypeStruct(cast_b.shape, BF16)],
        scratch_shapes=[pltpu.VMEM((n_blk, tq), F32), pltpu.VMEM((LANES, tq), F32),
                        pltpu.VMEM((2, cols, kdim), BF16), pltpu.VMEM((2, cols, kdim), BF16),
                        pltpu.VMEM((2, 1, cols), F32),
                        pltpu.VMEM((2, dh + V_EXTRA, cols), F32), pltpu.VMEM((2, dh, cols), F32),
                        pltpu.VMEM((n_units, 2 * SEL_CHUNK, NSA_COLS_PER_DOT), F32)],
        compiler_params=_cparams("parallel", "parallel", "arbitrary"),
        name="nsa_attention",
    )(z, z, gl, gate_b, q_norm, jnp.asarray(lanes), kcmp, vcmpT, ks, vsT, kw, vwT, ovT, kmq, cast_a, cast_b)


def _conv_body(a_ref, b_ref, ah_ref, bh_ref, w_ref, cb_ref, lg_ref, lb_ref, o_ref,
               h_ref, g_ref, y_ref, *, tt):
    C = a_ref.shape[1]
    halo = ah_ref[...].astype(F32) * jax.nn.sigmoid(bh_ref[...].astype(F32))
    h_ref[0:CONV_HALO, :] = jnp.where(pl.program_id(1) == 0, 0.0, halo)
    h_ref[CONV_HALO:CONV_HALO + tt, :] = a_ref[...].astype(F32) * jax.nn.sigmoid(b_ref[...].astype(F32))
    h_ref[CONV_HALO + tt:CONV_HALO + tt + SUBLANES, :] = jnp.zeros((SUBLANES, C), F32)
    lead = CONV_HALO - (CONV_KERNEL - 1)
    rows = tt + SUBLANES
    for s in range(C // LANES):
        cs = slice(s * LANES, (s + 1) * LANES)
        acc = None
        for r in range(SUBLANES):
            group = None
            for o in range(lead, lead + CONV_KERNEL):
                if o % SUBLANES == r:
                    term = w_ref[o - lead:o - lead + 1, cs] * h_ref[o - r:o - r + rows, cs]
                    group = term if group is None else group + term
            if r == 0:
                piece = group[0:tt, :]
            else:
                g_ref[r] = group
                piece = g_ref[r, r:r + tt, :]
            acc = piece if acc is None else acc + piece
        y_ref[:, cs] = acc + cb_ref[:, cs]
    y = y_ref[...]
    mu = jnp.mean(y, axis=-1, keepdims=True)
    var = jnp.mean(jnp.square(y - mu), axis=-1, keepdims=True)
    yn = (y - mu) * lax.rsqrt(var + LN_EPS) * lg_ref[...] + lb_ref[...]
    o_ref[...] = _silu(yn).astype(o_ref.dtype)


def conformer_conv(z, conv_w, conv_b, ln_g, ln_b, *, B, T, col0, C, tt):
    nt = T // tt
    ca = col0 // C
    hb = tt // CONV_HALO
    row = lambda b, t: b * nt + t
    halo_row = lambda b, t: jnp.maximum(row(b, t) * hb - 1, 0)
    const = lambda shape: pl.BlockSpec(shape, lambda b, t: (0,) * len(shape))
    return pl.pallas_call(
        functools.partial(_conv_body, tt=tt),
        grid=(B, nt),
        in_specs=[pl.BlockSpec((tt, C), lambda b, t: (row(b, t), ca)),
                  pl.BlockSpec((tt, C), lambda b, t: (row(b, t), ca + 1)),
                  pl.BlockSpec((CONV_HALO, C), lambda b, t: (halo_row(b, t), ca)),
                  pl.BlockSpec((CONV_HALO, C), lambda b, t: (halo_row(b, t), ca + 1)),
                  const((CONV_KERNEL, C)), const((1, C)), const((1, C)), const((1, C))],
        out_specs=pl.BlockSpec((tt, C), lambda b, t: (row(b, t), 0)),
        out_shape=jax.ShapeDtypeStruct((B * T, C), BF16),
        scratch_shapes=[pltpu.VMEM((CONV_HALO + tt + SUBLANES, C), F32),
                        pltpu.VMEM((SUBLANES, tt + SUBLANES, LANES), F32),
                        pltpu.VMEM((tt, C), F32)],
        compiler_params=_cparams("parallel", "parallel"),
        name="conformer_conv",
    )(z, z, z, z, conv_w, conv_b, ln_g, ln_b)


def _mem_attn_body(x_ref, gq_ref, wq_ref, mem_ref, gkv_ref, wkv_ref, qn_ref, kn_ref, wo_ref, gn_ref,
                   o_ref, on_ref, k_scr, v_scr):
    dh = HEAD_DIM
    width = MEM_HEADS * dh
    scale = dh ** -0.5
    heads = [slice(h * dh, (h + 1) * dh) for h in range(MEM_HEADS)]

    @pl.when(pl.program_id(1) == 0)
    def _():
        kv = jnp.dot(_rms(mem_ref[...], gkv_ref[...]).astype(BF16), wkv_ref[...], preferred_element_type=F32)
        for cs in heads:
            k_scr[:, cs] = _rms(kv[:, cs], kn_ref[...]).astype(BF16)
        v_scr[...] = kv[:, width:2 * width].astype(BF16)

    q_all = jnp.dot(_rms(x_ref[...], gq_ref[...]).astype(BF16), wq_ref[...], preferred_element_type=F32)
    scores = []
    for cs in heads:
        q = (_rms(q_all[:, cs], qn_ref[...]) * scale).astype(BF16)
        scores.append(lax.dot_general(q, k_scr[:, cs], (((1,), (1,)), ((), ())), preferred_element_type=F32))
    outs = []
    for h, s in enumerate(scores):
        v = v_scr[:, heads[h]]
        p = jnp.exp(s - jnp.max(s, axis=-1, keepdims=True))
        inv = 1.0 / jnp.sum(p, axis=-1, keepdims=True)
        outs.append((jnp.dot(p.astype(BF16), v, preferred_element_type=F32) * inv).astype(BF16))
    o = jnp.concatenate(outs, axis=1)
    y = x_ref[...] + jnp.dot(o, wo_ref[...], preferred_element_type=F32)
    o_ref[...] = y
    on_ref[...] = _rms(y, gn_ref[...]).astype(on_ref.dtype)


def mem_attention(x, q_gain, w_mq, mem, kv_gain, w_mkv, mq_norm, mk_norm, w_mo, next_norm, *, B, T, M, tt):
    D = x.shape[1]
    width = MEM_HEADS * HEAD_DIM
    nt = T // tt
    row = lambda b, t: b * nt + t
    const = lambda shape: pl.BlockSpec(shape, lambda b, t: (0,) * len(shape))
    return pl.pallas_call(
        _mem_attn_body,
        grid=(B, nt),
        in_specs=[pl.BlockSpec((tt, D), lambda b, t: (row(b, t), 0)),
                  const((1, D)), const((D, width)),
                  pl.BlockSpec((M, D), lambda b, t: (b, 0)),
                  const((1, D)), const((D, 2 * width)),
                  const((1, HEAD_DIM)), const((1, HEAD_DIM)), const((width, D)),
                  const((1, D))],
        out_specs=[pl.BlockSpec((tt, D), lambda b, t: (row(b, t), 0)),
                   pl.BlockSpec((tt, D), lambda b, t: (row(b, t), 0))],
        out_shape=[jax.ShapeDtypeStruct((B * T, D), F32),
                   jax.ShapeDtypeStruct((B * T, D), BF16)],
        scratch_shapes=[pltpu.VMEM((M, width), BF16), pltpu.VMEM((M, width), BF16)],
        compiler_params=_cparams("parallel", "arbitrary"),
        name="mem_attention",
    )(x, q_gain, w_mq, mem, kv_gain, w_mkv, mq_norm, mk_norm, w_mo, next_norm)


def kernel(x, mem, norm_mix, w_in, gate_b, q_norm, k_norm_cmp, k_norm_slc, k_norm_win, cmp_pos_k, cmp_pos_v, cmp_k_w1, cmp_k_w2, cmp_v_w1, cmp_v_w2, conv_w, conv_b, conv_ln_g, conv_ln_b, w_out, norm_mem_q, norm_mem_kv, w_mq, w_mk, w_mv, mq_norm, mk_norm, w_mo, norm_ffn, w_gate, w_up, w_down):
    B, T, D = x.shape
    M = mem.shape[1]
    depth = w_in.shape[0]
    G, hg = NSA_KV_HEADS, NSA_GROUP
    C = D - NSA_WIDTH
    n_gate = N_BRANCH * NSA_HEADS
    kv_all = 6 * KV_WIDTH
    assert w_in.shape[2] == NSA_WIDTH + kv_all + n_gate + 2 * C
    assert C == NSA_WIDTH, "column blocking below assumes equal NSA and conv widths"

    row1 = lambda v: v.reshape(1, -1)
    xf = x.reshape(B * T, D)
    memf = mem.reshape(B * M, D)
    u0 = NSA_WIDTH + kv_all + n_gate
    for l in range(depth):
        w = w_in[l]
        w_t = jnp.swapaxes(w, 0, 1)
        wg = w[:, NSA_WIDTH + kv_all:u0].reshape(D, G, hg * N_BRANCH)
        wg = jnp.pad(wg, ((0, 0), (0, 0), (0, LANES - hg * N_BRANCH))).reshape(D, G * LANES)
        gb = jnp.pad(gate_b[l].reshape(G, hg * N_BRANCH), ((0, 0), (0, LANES - hg * N_BRANCH))).reshape(1, G * LANES)

        xn = rmsnorm_bf16(xf, row1(norm_mix[l]), tm=TILE_RMSNORM)
        z, gl = in_projection(xn, w_t, wg, tm=TILE_IN_PROJ[0], tn=TILE_IN_PROJ[1],
                              segments=((0, NSA_WIDTH), (u0, 2 * C), (NSA_WIDTH, kv_all)))
        kv_col0 = NSA_WIDTH + 2 * C
        kc, vc, ks, vsT, kw, vwT = kv_prep(z, row1(k_norm_slc[l]), row1(k_norm_win[l]),
                                           B=B, T=T, col0=kv_col0, tt=TILE_KV_PREP)
        kcmp, vcmpT = compress_kv(kc, vc, cmp_pos_k[l], cmp_pos_v[l],
                                  cmp_k_w1[l].astype(BF16), cmp_k_w2[l].astype(BF16),
                                  cmp_v_w1[l].astype(BF16), cmp_v_w2[l].astype(BF16), row1(k_norm_cmp[l]),
                                  n_blk=T // SEL_BLK)
        o_nsa, w_gate_bf16, w_up_bf16 = nsa_attention(z, gl, gb, row1(q_norm[l]), kcmp, vcmpT, ks, vsT, kw, vwT,
                                                      w_gate[l], w_up[l], B=B, T=T)
        o_conv = conformer_conv(z, conv_w[l], row1(conv_b[l]), row1(conv_ln_g[l]), row1(conv_ln_b[l]),
                                B=B, T=T, col0=NSA_WIDTH, C=C, tt=TILE_CONV)
        xf = concat_matmul_residual(o_nsa, o_conv, w_out[l], xf, tm=TILE_OUT_PROJ[0], tn=TILE_OUT_PROJ[1])

        w_mkv = jnp.concatenate([w_mk[l], w_mv[l]], axis=1).astype(BF16)
        xf, xn = mem_attention(xf, row1(norm_mem_q[l]), w_mq[l].astype(BF16),
                               memf, row1(norm_mem_kv[l]), w_mkv,
                               row1(mq_norm[l]), row1(mk_norm[l]), w_mo[l].astype(BF16),
                               row1(norm_ffn[l]), B=B, T=T, M=M, tt=TILE_MEM_ATTN)

        hidden, w_down_bf16 = swiglu_up(xn, w_gate_bf16, w_up_bf16, w_down[l],
                                        tm=TILE_FFN_UP[0], tn=TILE_FFN_UP[1])
        xf = matmul_residual(hidden, w_down_bf16, xf, tm=TILE_FFN_DOWN[0], tn=TILE_FFN_DOWN[1])
    return xf.reshape(B, T, D)
```

```python
import functools

import ml_dtypes
import numpy as np
import jax
import jax.numpy as jnp
from jax import lax
from jax.experimental import pallas as pl
from jax.experimental.pallas import tpu as pltpu

F32 = jnp.float32
BF16 = jnp.bfloat16

HEAD_DIM = 128
NSA_HEADS = 8
NSA_KV_HEADS = 2
NSA_GROUP = NSA_HEADS // NSA_KV_HEADS
NSA_WIDTH = NSA_HEADS * HEAD_DIM
KV_WIDTH = NSA_KV_HEADS * HEAD_DIM
N_BRANCH = 3
CONV_KERNEL = 31
CMP_LEN = 32
CMP_STRIDE = 16
SEL_BLK = 64
N_SEL = 16
WINDOW = 512
MEM_HEADS = 4
NEG_INF = -1e30
FORCE_BONUS = 1e3
RMS_EPS = 1e-6
LN_EPS = 1e-5

LANES = 128
SUBLANES = 8
VMEM_LIMIT_BYTES = 56 * 1024 * 1024

Q_TILE = 256
SEL_CHUNK = 256
WIN_CHUNK = 256
NSA_COLS_PER_DOT = 512
CONV_HALO = 32

TILE_RMSNORM = 512
TILE_IN_PROJ = (2048, 512)
TILE_KV_PREP = 1024
TILE_CONV = 256
TILE_OUT_PROJ = (2048, 512)
TILE_MEM_ATTN = 512
TILE_FFN_UP = (1024, 512)
TILE_FFN_DOWN = (1024, 512)


def _cparams(*sem):
    return pltpu.CompilerParams(dimension_semantics=sem, vmem_limit_bytes=VMEM_LIMIT_BYTES)


def _rms(x, g, eps=RMS_EPS):
    ms = jnp.mean(x * x, axis=-1, keepdims=True)
    return x * lax.rsqrt(ms + eps) * g


def _silu(x):
    return x * jax.nn.sigmoid(x)


def _rmsnorm_body(x_ref, g_ref, o_ref):
    o_ref[...] = _rms(x_ref[...], g_ref[...]).astype(o_ref.dtype)


def rmsnorm_bf16(x, g, *, tm):
    M, K = x.shape
    return pl.pallas_call(
        _rmsnorm_body,
        grid=(M // tm,),
        in_specs=[pl.BlockSpec((tm, K), lambda i: (i, 0)), pl.BlockSpec((1, K), lambda i: (0, 0))],
        out_specs=pl.BlockSpec((tm, K), lambda i: (i, 0)),
        out_shape=jax.ShapeDtypeStruct((M, K), BF16),
        compiler_params=_cparams("parallel"),
        name="rmsnorm",
    )(x, g)


def _in_proj_body(xn_ref, wt_ref, wg_ref, o_ref, og_ref):
    @pl.when(pl.program_id(1) == 0)
    def _():
        og_ref[...] = jnp.dot(xn_ref[...], wg_ref[...].astype(BF16), preferred_element_type=F32)

    o_ref[...] = lax.dot_general(xn_ref[...], wt_ref[...].astype(BF16), (((1,), (1,)), ((), ())),
                                 preferred_element_type=F32).astype(o_ref.dtype)


def _swiglu_body(x_ref, wg_ref, wu_ref, o_ref):
    x = x_ref[...]
    a = jnp.dot(x, wg_ref[...], preferred_element_type=F32)
    b = jnp.dot(x, wu_ref[...], preferred_element_type=F32)
    o_ref[...] = (_silu(a) * b).astype(o_ref.dtype)


def in_projection(xn, w_t, wg, *, segments, tm, tn):
    M, K = xn.shape
    NG = wg.shape[1]
    assert all(start % SUBLANES == 0 and size % tn == 0 for start, size in segments)
    N = sum(size for _, size in segments)

    def row_start(i, j):
        group, first = 0, 0
        for seg_start, size in segments:
            group = jnp.where(j >= first, seg_start // SUBLANES + (j - first) * (tn // SUBLANES), group)
            first += size // tn
        return (group * SUBLANES, 0)

    return pl.pallas_call(
        _in_proj_body,
        grid=(M // tm, N // tn),
        in_specs=[pl.BlockSpec((tm, K), lambda i, j: (i, 0)),
                  pl.BlockSpec((pl.Element(tn), pl.Element(K)), row_start),
                  pl.BlockSpec((K, NG), lambda i, j: (0, 0))],
        out_specs=[pl.BlockSpec((tm, tn), lambda i, j: (i, j)),
                   pl.BlockSpec((tm, NG), lambda i, j: (i, 0))],
        out_shape=[jax.ShapeDtypeStruct((M, N), BF16),
                   jax.ShapeDtypeStruct((M, NG), F32)],
        compiler_params=_cparams("parallel", "arbitrary"),
        name="in_proj",
    )(xn, w_t, wg)


def swiglu_up(xn, w_gate, w_up, *, tm, tn):
    assert w_gate.dtype == w_up.dtype == BF16
    M, K = xn.shape
    N = w_gate.shape[1]
    return pl.pallas_call(
        _swiglu_body,
        grid=(M // tm, N // tn),
        in_specs=[pl.BlockSpec((tm, K), lambda i, j: (i, 0)),
                  pl.BlockSpec((K, tn), lambda i, j: (0, j)),
                  pl.BlockSpec((K, tn), lambda i, j: (0, j))],
        out_specs=pl.BlockSpec((tm, tn), lambda i, j: (i, j)),
        out_shape=jax.ShapeDtypeStruct((M, N), BF16),
        compiler_params=_cparams("parallel", "arbitrary"),
        name="ffn_up",
    )(xn, w_gate, w_up)


def _mm_res_body(a_ref, w_ref, r_ref, o_ref):
    o_ref[...] = r_ref[...] + jnp.dot(a_ref[...], w_ref[...], preferred_element_type=F32)


def matmul_residual(a, w, res, *, tm, tn):
    M, K = a.shape
    N = w.shape[1]
    return pl.pallas_call(
        _mm_res_body,
        grid=(M // tm, N // tn),
        in_specs=[pl.BlockSpec((tm, K), lambda i, j: (i, 0)),
                  pl.BlockSpec((K, tn), lambda i, j: (0, j)),
                  pl.BlockSpec((tm, tn), lambda i, j: (i, j))],
        out_specs=pl.BlockSpec((tm, tn), lambda i, j: (i, j)),
        out_shape=jax.ShapeDtypeStruct((M, N), F32),
        compiler_params=_cparams("parallel", "arbitrary"),
        name="matmul_residual",
    )(a, w, res)


def _mm2_res_body(a_ref, b_ref, wa_ref, wb_ref, r_ref, o_ref):
    acc = jnp.dot(a_ref[...], wa_ref[...], preferred_element_type=F32)
    acc = acc + jnp.dot(b_ref[...], wb_ref[...], preferred_element_type=F32)
    o_ref[...] = r_ref[...] + acc


def concat_matmul_residual(a, b, w, res, *, tm, tn):
    M, Ka = a.shape
    Kb = b.shape[1]
    assert Ka == Kb and w.shape[0] == Ka + Kb
    N = w.shape[1]
    return pl.pallas_call(
        _mm2_res_body,
        grid=(M // tm, N // tn),
        in_specs=[pl.BlockSpec((tm, Ka), lambda i, j: (i, 0)),
                  pl.BlockSpec((tm, Kb), lambda i, j: (i, 0)),
                  pl.BlockSpec((Ka, tn), lambda i, j: (0, j)),
                  pl.BlockSpec((Kb, tn), lambda i, j: (1, j)),
                  pl.BlockSpec((tm, tn), lambda i, j: (i, j))],
        out_specs=pl.BlockSpec((tm, tn), lambda i, j: (i, j)),
        out_shape=jax.ShapeDtypeStruct((M, N), F32),
        compiler_params=_cparams("parallel", "arbitrary"),
        name="out_proj",
    )(a, b, w, w, res)


AUG_WIDTH = LANES
V_EXTRA = 16
SLOPE_PIECES = 4
LOG2E = float(np.log2(np.e))


def _bf16_pieces(x, n):
    out, rem = [], np.asarray(x, np.float64)
    for _ in range(n):
        piece = rem.astype(ml_dtypes.bfloat16).astype(np.float64)
        out.append(piece.astype(np.float32))
        rem = rem - piece
    return out


def _position_lanes(pos_hi, pos_lo, onehot, n_blk, shape):
    lane = lax.broadcasted_iota(jnp.int32, shape, 1)
    p = SLOPE_PIECES
    extra = jnp.where((lane >= n_blk) & (lane < n_blk + p), pos_hi,
                      jnp.where((lane >= n_blk + p) & (lane < n_blk + 2 * p), pos_lo, 0.0))
    if onehot is not None:
        extra = jnp.where(lane < n_blk, onehot, extra)
    return extra.astype(BF16)


def _kv_prep_body(c_ref, s_ref, w_ref, kns_ref, knw_ref,
                  kc_o, vc_o, ks_o, vsT_o, kw_o, vwT_o, slab_ref, *, tt, n_blk):
    dh = HEAD_DIM
    shape = (tt, AUG_WIDTH)
    pos = pl.program_id(1) * tt + lax.broadcasted_iota(jnp.int32, shape, 0)
    lane = lax.broadcasted_iota(jnp.int32, shape, 1)
    blk = jnp.right_shift(pos, SEL_BLK.bit_length() - 1)
    pos_hi = (blk * SEL_BLK).astype(F32)
    pos_lo = (pos & (SEL_BLK - 1)).astype(F32)
    aug_sel = _position_lanes(pos_hi, pos_lo, (blk == lane).astype(F32), n_blk, shape)
    aug_win = _position_lanes(pos_hi, pos_lo, None, n_blk, shape)
    for g in range(NSA_KV_HEADS):
        k_cols = slice(g * dh, (g + 1) * dh)
        v_cols = slice(KV_WIDTH + g * dh, KV_WIDTH + (g + 1) * dh)
        for slab, (cols, out) in enumerate(((k_cols, kc_o), (v_cols, vc_o))):
            slab_ref[slab] = c_ref[:, cols].astype(F32)
            for i in range(CMP_STRIDE):
                rows = pl.ds(i, tt // CMP_STRIDE, stride=CMP_STRIDE)
                out[0, g, :, i * dh:(i + 1) * dh] = slab_ref[slab, rows, :]
        ks_o[0, g, :, 0:dh] = _rms(s_ref[:, k_cols].astype(F32), kns_ref[...]).astype(BF16)
        ks_o[0, g, :, dh:dh + AUG_WIDTH] = aug_sel
        kw_o[0, g, :, 0:dh] = _rms(w_ref[:, k_cols].astype(F32), knw_ref[...]).astype(BF16)
        kw_o[0, g, :, dh:dh + AUG_WIDTH] = aug_win
        for v_ref, out, chunk in ((s_ref, vsT_o, SEL_CHUNK), (w_ref, vwT_o, WIN_CHUNK)):
            vT = v_ref[:, v_cols].astype(F32).T.astype(BF16)
            ones_first = (lax.broadcasted_iota(jnp.int32, (V_EXTRA, chunk), 0) == 0).astype(BF16)
            for c in range(tt // chunk):
                out[0, g, c, 0:dh, :] = vT[:, c * chunk:(c + 1) * chunk]
                out[0, g, c, dh:dh + V_EXTRA, :] = ones_first


def kv_prep(z, kn_slc, kn_win, *, B, T, col0, tt):
    G, dh = NSA_KV_HEADS, HEAD_DIM
    nt = T // tt
    n_blk = T // SEL_BLK
    assert n_blk + 2 * SLOPE_PIECES <= AUG_WIDTH
    cb = col0 // (2 * KV_WIDTH)
    row = lambda b, t: b * nt + t
    kdim = dh + AUG_WIDTH
    return pl.pallas_call(
        functools.partial(_kv_prep_body, tt=tt, n_blk=n_blk),
        grid=(B, nt),
        in_specs=[pl.BlockSpec((tt, 2 * KV_WIDTH), lambda b, t: (row(b, t), cb)),
                  pl.BlockSpec((tt, 2 * KV_WIDTH), lambda b, t: (row(b, t), cb + 1)),
                  pl.BlockSpec((tt, 2 * KV_WIDTH), lambda b, t: (row(b, t), cb + 2)),
                  pl.BlockSpec((1, dh), lambda b, t: (0, 0)),
                  pl.BlockSpec((1, dh), lambda b, t: (0, 0))],
        out_specs=[pl.BlockSpec((1, G, tt // CMP_STRIDE, CMP_STRIDE * dh), lambda b, t: (b, 0, t, 0)),
                   pl.BlockSpec((1, G, tt // CMP_STRIDE, CMP_STRIDE * dh), lambda b, t: (b, 0, t, 0)),
                   pl.BlockSpec((1, G, tt, kdim), lambda b, t: (b, 0, t, 0)),
                   pl.BlockSpec((1, G, tt // SEL_CHUNK, dh + V_EXTRA, SEL_CHUNK), lambda b, t: (b, 0, t, 0, 0)),
                   pl.BlockSpec((1, G, tt, kdim), lambda b, t: (b, 0, t, 0)),
                   pl.BlockSpec((1, G, tt // WIN_CHUNK, dh + V_EXTRA, WIN_CHUNK), lambda b, t: (b, 0, t, 0, 0))],
        out_shape=[jax.ShapeDtypeStruct((B, G, T // CMP_STRIDE, CMP_STRIDE * dh), F32),
                   jax.ShapeDtypeStruct((B, G, T // CMP_STRIDE, CMP_STRIDE * dh), F32),
                   jax.ShapeDtypeStruct((B, G, T, kdim), BF16),
                   jax.ShapeDtypeStruct((B, G, T // SEL_CHUNK, dh + V_EXTRA, SEL_CHUNK), BF16),
                   jax.ShapeDtypeStruct((B, G, T, kdim), BF16),
                   jax.ShapeDtypeStruct((B, G, T // WIN_CHUNK, dh + V_EXTRA, WIN_CHUNK), BF16)],
        scratch_shapes=[pltpu.VMEM((2, tt, dh), F32)],
        compiler_params=_cparams("parallel", "parallel"),
        name="kv_prep",
    )(z, z, z, kn_slc, kn_win)


def _compress_body(xk_ref, xv_ref, pk_ref, pv_ref, w1k_ref, w2k_ref, w1v_ref, w2v_ref, kn_ref,
                   kc_o, vcT_o, *, n_blk):
    dh = HEAD_DIM

    def phi(x_ref, p_ref, w1_ref, w2_ref):
        x = x_ref[0, 0]
        first = jnp.dot((x + p_ref[0:1, :]).astype(BF16), w1_ref[0], preferred_element_type=F32)
        second = jnp.dot((x + p_ref[1:2, :]).astype(BF16), w1_ref[1], preferred_element_type=F32)
        n = second.shape[0]
        h = first + pltpu.roll(second, n - 1, 0)
        return jnp.dot(_silu(h).astype(BF16), w2_ref[...], preferred_element_type=F32)

    k = _rms(phi(xk_ref, pk_ref, w1k_ref, w2k_ref), kn_ref[...])
    nc = k.shape[0]
    shape = (nc, AUG_WIDTH)
    start = (lax.broadcasted_iota(jnp.int32, shape, 0) * CMP_STRIDE).astype(F32)
    centre = jnp.full(shape, (CMP_LEN - 1) / 2.0, F32)
    kc_o[0, 0, :, 0:dh] = k.astype(BF16)
    kc_o[0, 0, :, dh:dh + AUG_WIDTH] = _position_lanes(start, centre, None, n_blk, shape)
    vcT_o[0, 0] = phi(xv_ref, pv_ref, w1v_ref, w2v_ref).T.astype(BF16)


def compress_kv(xk, xv, pos_k, pos_v, w1k, w2k, w1v, w2v, kn_cmp, *, n_blk):
    B, G, nc, half = xk.shape
    dh = half // CMP_STRIDE
    assert CMP_LEN == 2 * CMP_STRIDE
    x_spec = pl.BlockSpec((1, 1, nc, half), lambda b, g: (b, g, 0, 0))
    full = lambda shape: pl.BlockSpec(shape, lambda b, g: (0,) * len(shape))
    return pl.pallas_call(
        functools.partial(_compress_body, n_blk=n_blk),
        grid=(B, G),
        in_specs=[x_spec, x_spec, full((2, half)), full((2, half)),
                  full((2, half, dh)), full((dh, dh)), full((2, half, dh)), full((dh, dh)),
                  full((1, dh))],
        out_specs=[pl.BlockSpec((1, 1, nc, dh + AUG_WIDTH), lambda b, g: (b, g, 0, 0)),
                   pl.BlockSpec((1, 1, dh, nc), lambda b, g: (b, g, 0, 0))],
        out_shape=[jax.ShapeDtypeStruct((B, G, nc, dh + AUG_WIDTH), BF16),
                   jax.ShapeDtypeStruct((B, G, dh, nc), BF16)],
        compiler_params=_cparams("parallel", "parallel"),
        name="compress_kv",
    )(xk, xv, pos_k.reshape(2, half), pos_v.reshape(2, half),
      w1k.reshape(2, half, dh), w2k, w1v.reshape(2, half, dh), w2v, kn_cmp)


def _nsa_body(q_ref, qnext_ref, gl_ref, gb_ref, qn_ref, lanes_ref, kc_ref, vcT_ref,
              ks_ref, vsT_ref, kw_ref, vwT_ref, ovT_ref, kmq_ref, cast_a_ref, cast_b_ref,
              o_ref, cast_a_o, cast_b_o,
              score_ref, gate_ref, qc_ref, qs_ref, m_ref, acc_ref, ocmp_ref, sbuf_ref, *, n_tiles):
    dh, hg, tq = HEAD_DIM, NSA_GROUP, Q_TILE
    cols = hg * tq
    qi = pl.program_id(2)
    contract_last = (((1,), (1,)), ((), ()))
    n_blk = ovT_ref.shape[0]

    scale = dh ** -0.5 * LOG2E
    hc = kmq_ref.shape[1]
    assert cols % hc == 0 and hc % tq == 0
    groups = [slice(i * hc, (i + 1) * hc) for i in range(cols // hc)]
    cw = ck = tq
    assert WIN_CHUNK == SEL_CHUNK == tq == kmq_ref.shape[0] and WINDOW % cw == 0
    n_back = WINDOW // cw
    nc = kc_ref.shape[2]
    WIN, SEL = 0, 1

    def reset(b):
        m_ref[b] = jnp.full((1, cols), NEG_INF, F32)
        acc_ref[b] = jnp.zeros((dh + V_EXTRA, cols), F32)

    def issue_scores(queries_ref, parts, unit):
        for k, _, _ in parts:
            for cs in groups:
                sbuf_ref[unit, 0:k.shape[0], :] = lax.dot_general(
                    k, queries_ref[cs, :], contract_last, preferred_element_type=F32)
                unit += 1

    def softmax_update(b, parts, unit):
        for (k, v_pieces, mask) in parts:
            for cs in groups:
                s_ref = sbuf_ref.at[unit, 0:k.shape[0], :]
                unit += 1
                if mask is not None:
                    s_ref[...] = jnp.where(mask(), s_ref[...], NEG_INF)
                m_old = m_ref[b, :, cs]
                m_new = jnp.maximum(m_old, jnp.max(s_ref[...], axis=0, keepdims=True))
                alpha = jnp.exp2(m_old - m_new)
                p = jnp.exp2(s_ref[...] - m_new).astype(BF16)
                m_ref[b, :, cs] = m_new
                pv = None
                for i, vT in enumerate(v_pieces):
                    n = vT.shape[1]
                    d = jnp.dot(vT, p[i * n:(i + 1) * n, :], preferred_element_type=F32)
                    pv = d if pv is None else pv + d
                acc_ref[b, :, cs] = alpha * acc_ref[b, :, cs] + pv

    def result(b):
        return acc_ref[b, 0:dh, :] * (1.0 / acc_ref[b, dh:dh + 1, :])

    def prepare(src_ref, tile, slot):
        t0 = tile * tq
        for h in range(hg):
            rows = slice(h * tq, (h + 1) * tq)
            q_h = (_rms(src_ref[:, h * dh:(h + 1) * dh].astype(F32), qn_ref[...]) * scale).astype(BF16)
            qc_ref[slot, rows, 0:dh] = q_h
            qs_ref[slot, rows, 0:dh] = q_h
            qc_ref[slot, rows, dh:dh + AUG_WIDTH] = jnp.broadcast_to(
                lanes_ref[0, hg + h:hg + h + 1, :], (tq, AUG_WIDTH)).astype(BF16)

        s = lax.dot_general(kc_ref[0, 0], qc_ref[slot], contract_last, preferred_element_type=F32)
        c_idx = lax.broadcasted_iota(jnp.int32, (nc, cols), 0)
        t_c = t0 + (lax.broadcasted_iota(jnp.int32, (nc, cols), 1) & (tq - 1))
        mask = (c_idx * CMP_STRIDE + (CMP_LEN - 1)) <= t_c
        s = jnp.where(mask, s, NEG_INF)
        m_c = jnp.max(s, axis=0, keepdims=True)
        p_c = jnp.where(mask, jnp.exp2(s - m_c), 0.0)
        l_c = jnp.sum(p_c, axis=0, keepdims=True)
        p_c = p_c * jnp.where(l_c > 0.0, 1.0 / l_c, 0.0)
        ocmp_ref[slot] = jnp.dot(vcT_ref[0, 0], p_c.astype(BF16), preferred_element_type=F32)

        p_sum = p_c[:, 0:tq]
        for h in range(1, hg):
            p_sum = p_sum + p_c[:, h * tq:(h + 1) * tq]
        hi = p_sum.astype(BF16)
        rem = p_sum - hi.astype(F32)
        mid = rem.astype(BF16)
        lo = (rem - mid.astype(F32)).astype(BF16)
        ovT = ovT_ref[...]
        imp = (jnp.dot(ovT, hi, preferred_element_type=F32)
               + jnp.dot(ovT, mid, preferred_element_type=F32)
               + jnp.dot(ovT, lo, preferred_element_type=F32))
        blk = lax.broadcasted_iota(jnp.int32, (n_blk, tq), 0)
        t_b = t0 + lax.broadcasted_iota(jnp.int32, (n_blk, tq), 1)
        cur = jnp.right_shift(t_b, SEL_BLK.bit_length() - 1)
        valid = blk * SEL_BLK <= t_b
        forced = (blk == 0) | (blk == cur) | (blk == cur - 1)
        score = jnp.where(valid, imp + FORCE_BONUS * forced.astype(F32), NEG_INF)
        score_ref[...] = score
        rank = jnp.zeros((n_blk, tq), F32)
        for j in range(n_blk):
            sj = score_ref[j:j + 1, :]
            beats = (sj > score) | ((sj == score) & (blk > j))
            rank = rank + beats.astype(F32)
        block_bias = jnp.where(rank < float(min(N_SEL, n_blk)), 0.0, NEG_INF)

        assert 2 * SLOPE_PIECES <= 8
        ones_rows = (lax.broadcasted_iota(jnp.int32, (8, tq), 0) < 2 * SLOPE_PIECES).astype(F32)
        extra_t = jnp.concatenate([block_bias, ones_rows,
                                   jnp.zeros((AUG_WIDTH - n_blk - 8, tq), F32)], axis=0).T
        for h in range(hg):
            qs_ref[slot, h * tq:(h + 1) * tq, dh:dh + AUG_WIDTH] = (
                extra_t * lanes_ref[0, h:h + 1, :]).astype(BF16)

    def win_part(c, mask=None):
        return kw_ref[0, 0, c * cw:(c + 1) * cw, :], [vwT_ref[0, 0, c]], mask

    def sel_part(c, n=1, mask=None):
        return ks_ref[0, 0, c * ck:(c + n) * ck, :], [vsT_ref[0, 0, c + i] for i in range(n)], mask

    def tile_step(tile):
        slot = tile % 2
        below_diagonal = lambda: kmq_ref[...] <= 0
        win_parts = [win_part(tile, below_diagonal)]
        win_parts += [win_part(tile - j) for j in range(1, n_back) if tile - j >= 0]
        if tile - n_back >= 0:
            win_parts.append(win_part(tile - n_back, lambda: kmq_ref[...] > 0))
        sel_parts = [sel_part(tile, mask=below_diagonal)]
        sel_parts += [sel_part(c, n=min(2, tile - c)) for c in range(0, tile, 2)]
        sel_unit = len(win_parts) * len(groups)
        issue_scores(qc_ref.at[slot], win_parts, 0)
        issue_scores(qs_ref.at[slot], sel_parts, sel_unit)
        if tile + 1 < n_tiles:
            prepare(qnext_ref, tile + 1, 1 - slot)
        reset(WIN)
        softmax_update(WIN, win_parts, 0)
        reset(SEL)
        softmax_update(SEL, sel_parts, sel_unit)

        gate_ref[...] = jax.nn.sigmoid(gl_ref[...] + gb_ref[...]).T
        o_win, o_slc = result(WIN), result(SEL)
        for h in range(hg):
            cs = slice(h * tq, (h + 1) * tq)
            gate = lambda branch: gate_ref[h * N_BRANCH + branch:h * N_BRANCH + branch + 1, :]
            o_h = gate(0) * ocmp_ref[slot, :, cs] + gate(1) * o_slc[:, cs] + gate(2) * o_win[:, cs]
            o_ref[:, h * dh:(h + 1) * dh] = o_h.T.astype(o_ref.dtype)

    cast_a_o[...] = cast_a_ref[...].astype(cast_a_o.dtype)
    cast_b_o[...] = cast_b_ref[...].astype(cast_b_o.dtype)

    @pl.when(qi == 0)
    def _():
        prepare(q_ref, 0, 0)

    for tile in range(n_tiles):
        pl.when(qi == tile)(functools.partial(tile_step, tile))


def nsa_attention(z, gl, gate_b, q_norm, kcmp, vcmpT, ks, vsT, kw, vwT, cast_a, cast_b, *, B, T):
    G, hg, dh, tq = NSA_KV_HEADS, NSA_GROUP, HEAD_DIM, Q_TILE
    nq = T // tq
    nc = kcmp.shape[2]
    n_blk = T // SEL_BLK
    kdim = dh + AUG_WIDTH
    cols = hg * tq
    assert SEL_CHUNK == WIN_CHUNK == tq
    n_units = (WINDOW // WIN_CHUNK + 1 + 1 + nq // 2) * (cols // NSA_COLS_PER_DOT)
    assert SEL_CHUNK % SEL_BLK == 0 and T % SEL_CHUNK == 0 and gl.shape[1] == G * LANES
    assert SEL_CHUNK % tq == 0 and n_blk % 8 == 0 and n_blk + 8 <= AUG_WIDTH

    slopes = np.exp2(-8.0 * (np.arange(NSA_HEADS, dtype=np.float64) + 1.0) / NSA_HEADS).reshape(G, hg)
    lanes = np.zeros((G, 2 * hg, AUG_WIDTH), np.float32)
    lanes[:, :hg, :n_blk] = 1.0
    for i, piece in enumerate(_bf16_pieces(slopes * LOG2E, SLOPE_PIECES)):
        for lane in (n_blk + i, n_blk + SLOPE_PIECES + i):
            lanes[:, :hg, lane] = piece
            lanes[:, hg:, lane] = piece
    cs = np.arange(nc)[:, None] * CMP_STRIDE
    bs = np.arange(n_blk)[None, :] * SEL_BLK
    overlap = np.clip(np.minimum(cs + CMP_LEN, bs + SEL_BLK) - np.maximum(cs, bs), 0, None) / CMP_LEN
    overlap[(T - CMP_LEN) // CMP_STRIDE + 1:] = 0.0
    ovT = jnp.asarray(overlap.T, dtype=BF16)
    hc = NSA_COLS_PER_DOT
    kmq = jnp.asarray(np.arange(SEL_CHUNK)[:, None] - (np.arange(hc)[None, :] % tq), dtype=jnp.int32)

    row = lambda b, qi: b * nq + qi
    per_bg = lambda shape: pl.BlockSpec((1, 1) + shape, lambda b, g, qi: (b, g) + (0,) * len(shape))
    const = lambda shape: pl.BlockSpec(shape, lambda b, g, qi: (0,) * len(shape))
    n_steps = B * G * nq
    assert cast_a.shape[0] % n_steps == 0 and cast_b.shape[0] % n_steps == 0
    slab = lambda w: pl.BlockSpec((w.shape[0] // n_steps, w.shape[1]),
                                  lambda b, g, qi: ((b * G + g) * nq + qi, 0))
    return pl.pallas_call(
        functools.partial(_nsa_body, n_tiles=nq),
        grid=(B, G, nq),
        in_specs=[pl.BlockSpec((tq, hg * dh), lambda b, g, qi: (row(b, qi), g)),
                  pl.BlockSpec((tq, hg * dh), lambda b, g, qi: (row(b, jnp.minimum(qi + 1, nq - 1)), g)),
                  pl.BlockSpec((tq, LANES), lambda b, g, qi: (row(b, qi), g)),
                  pl.BlockSpec((1, LANES), lambda b, g, qi: (0, g)),
                  const((1, dh)),
                  pl.BlockSpec((1, 2 * hg, AUG_WIDTH), lambda b, g, qi: (g, 0, 0)),
                  per_bg((nc, kdim)), per_bg((dh, nc)),
                  per_bg((T, kdim)), per_bg((T // SEL_CHUNK, dh + V_EXTRA, SEL_CHUNK)),
                  per_bg((T, kdim)), per_bg((T // WIN_CHUNK, dh + V_EXTRA, WIN_CHUNK)),
                  const((n_blk, nc)), const((SEL_CHUNK, hc)), slab(cast_a), slab(cast_b)],
        out_specs=[pl.BlockSpec((tq, hg * dh), lambda b, g, qi: (row(b, qi), g)),
                   slab(cast_a), slab(cast_b)],
        out_shape=[jax.ShapeDtypeStruct((B * T, NSA_WIDTH), BF16),
                   jax.ShapeDtypeStruct(cast_a.shape, BF16), jax.ShapeDtypeStruct(cast_b.shape, BF16)],
        scratch_shapes=[pltpu.VMEM((n_blk, tq), F32), pltpu.VMEM((LANES, tq), F32),
                        pltpu.VMEM((2, cols, kdim), BF16), pltpu.VMEM((2, cols, kdim), BF16),
                        pltpu.VMEM((2, 1, cols), F32),
                        pltpu.VMEM((2, dh + V_EXTRA, cols), F32), pltpu.VMEM((2, dh, cols), F32),
                        pltpu.VMEM((n_units, 2 * SEL_CHUNK, NSA_COLS_PER_DOT), F32)],
        compiler_params=_cparams("parallel", "parallel", "arbitrary"),
        name="nsa_attention",
    )(z, z, gl, gate_b, q_norm, jnp.asarray(lanes), kcmp, vcmpT, ks, vsT, kw, vwT, ovT, kmq, cast_a, cast_b)


def _conv_body(*refs, tt, n_casts):
    a_ref, b_ref, ah_ref, bh_ref, w_ref, cb_ref, lg_ref, lb_ref = refs[:8]
    cast_in, (o_ref, *cast_out) = refs[8:8 + n_casts], refs[8 + n_casts:9 + 2 * n_casts]
    h_ref, g_ref, y_ref = refs[9 + 2 * n_casts:]
    for src, dst in zip(cast_in, cast_out):
        dst[...] = src[...].astype(dst.dtype)
    C = a_ref.shape[1]
    halo = ah_ref[...].astype(F32) * jax.nn.sigmoid(bh_ref[...].astype(F32))
    h_ref[0:CONV_HALO, :] = jnp.where(pl.program_id(1) == 0, 0.0, halo)
    h_ref[CONV_HALO:CONV_HALO + tt, :] = a_ref[...].astype(F32) * jax.nn.sigmoid(b_ref[...].astype(F32))
    h_ref[CONV_HALO + tt:CONV_HALO + tt + SUBLANES, :] = jnp.zeros((SUBLANES, C), F32)
    lead = CONV_HALO - (CONV_KERNEL - 1)
    rows = tt + SUBLANES
    for s in range(C // LANES):
        cs = slice(s * LANES, (s + 1) * LANES)
        acc = None
        for r in range(SUBLANES):
            group = None
            for o in range(lead, lead + CONV_KERNEL):
                if o % SUBLANES == r:
                    term = w_ref[o - lead:o - lead + 1, cs] * h_ref[o - r:o - r + rows, cs]
                    group = term if group is None else group + term
            if r == 0:
                piece = group[0:tt, :]
            else:
                g_ref[r] = group
                piece = g_ref[r, r:r + tt, :]
            acc = piece if acc is None else acc + piece
        y_ref[:, cs] = acc + cb_ref[:, cs]
    y = y_ref[...]
    mu = jnp.mean(y, axis=-1, keepdims=True)
    var = jnp.mean(jnp.square(y - mu), axis=-1, keepdims=True)
    yn = (y - mu) * lax.rsqrt(var + LN_EPS) * lg_ref[...] + lb_ref[...]
    o_ref[...] = _silu(yn).astype(o_ref.dtype)


def conformer_conv(z, conv_w, conv_b, ln_g, ln_b, casts, *, B, T, col0, C, tt):
    nt = T // tt
    ca = col0 // C
    hb = tt // CONV_HALO
    row = lambda b, t: b * nt + t
    halo_row = lambda b, t: jnp.maximum(row(b, t) * hb - 1, 0)
    const = lambda shape: pl.BlockSpec(shape, lambda b, t: (0,) * len(shape))
    n_steps = B * nt
    assert all(w.shape[0] % (n_steps * 2 * SUBLANES) == 0 for w in casts)
    slab = lambda w: pl.BlockSpec((w.shape[0] // n_steps, w.shape[1]), lambda b, t: (row(b, t), 0))
    return pl.pallas_call(
        functools.partial(_conv_body, tt=tt, n_casts=len(casts)),
        grid=(B, nt),
        in_specs=[pl.BlockSpec((tt, C), lambda b, t: (row(b, t), ca)),
                  pl.BlockSpec((tt, C), lambda b, t: (row(b, t), ca + 1)),
                  pl.BlockSpec((CONV_HALO, C), lambda b, t: (halo_row(b, t), ca)),
                  pl.BlockSpec((CONV_HALO, C), lambda b, t: (halo_row(b, t), ca + 1)),
                  const((CONV_KERNEL, C)), const((1, C)), const((1, C)), const((1, C))]
                 + [slab(w) for w in casts],
        out_specs=[pl.BlockSpec((tt, C), lambda b, t: (row(b, t), 0))] + [slab(w) for w in casts],
        out_shape=[jax.ShapeDtypeStruct((B * T, C), BF16)]
                  + [jax.ShapeDtypeStruct(w.shape, BF16) for w in casts],
        scratch_shapes=[pltpu.VMEM((CONV_HALO + tt + SUBLANES, C), F32),
                        pltpu.VMEM((SUBLANES, tt + SUBLANES, LANES), F32),
                        pltpu.VMEM((tt, C), F32)],
        compiler_params=_cparams("parallel", "parallel"),
        name="conformer_conv",
    )(z, z, z, z, conv_w, conv_b, ln_g, ln_b, *casts)


def _mem_attn_body(x_ref, gq_ref, wq_ref, mem_ref, gkv_ref, wkv_ref, qn_ref, kn_ref, wo_ref, gn_ref,
                   o_ref, on_ref, k_scr, v_scr):
    dh = HEAD_DIM
    width = MEM_HEADS * dh
    scale = dh ** -0.5
    heads = [slice(h * dh, (h + 1) * dh) for h in range(MEM_HEADS)]

    @pl.when(pl.program_id(1) == 0)
    def _():
        kv = jnp.dot(_rms(mem_ref[...], gkv_ref[...]).astype(BF16), wkv_ref[...], preferred_element_type=F32)
        for cs in heads:
            k_scr[:, cs] = _rms(kv[:, cs], kn_ref[...]).astype(BF16)
        v_scr[...] = kv[:, width:2 * width].astype(BF16)

    q_all = jnp.dot(_rms(x_ref[...], gq_ref[...]).astype(BF16), wq_ref[...], preferred_element_type=F32)
    scores = []
    for cs in heads:
        q = (_rms(q_all[:, cs], qn_ref[...]) * scale).astype(BF16)
        scores.append(lax.dot_general(q, k_scr[:, cs], (((1,), (1,)), ((), ())), preferred_element_type=F32))
    outs = []
    for h, s in enumerate(scores):
        v = v_scr[:, heads[h]]
        p = jnp.exp(s - jnp.max(s, axis=-1, keepdims=True))
        inv = 1.0 / jnp.sum(p, axis=-1, keepdims=True)
        outs.append((jnp.dot(p.astype(BF16), v, preferred_element_type=F32) * inv).astype(BF16))
    o = jnp.concatenate(outs, axis=1)
    y = x_ref[...] + jnp.dot(o, wo_ref[...], preferred_element_type=F32)
    o_ref[...] = y
    on_ref[...] = _rms(y, gn_ref[...]).astype(on_ref.dtype)


def mem_attention(x, q_gain, w_mq, mem, kv_gain, w_mkv, mq_norm, mk_norm, w_mo, next_norm, *, B, T, M, tt):
    D = x.shape[1]
    width = MEM_HEADS * HEAD_DIM
    nt = T // tt
    row = lambda b, t: b * nt + t
    const = lambda shape: pl.BlockSpec(shape, lambda b, t: (0,) * len(shape))
    return pl.pallas_call(
        _mem_attn_body,
        grid=(B, nt),
        in_specs=[pl.BlockSpec((tt, D), lambda b, t: (row(b, t), 0)),
                  const((1, D)), const((D, width)),
                  pl.BlockSpec((M, D), lambda b, t: (b, 0)),
                  const((1, D)), const((D, 2 * width)),
                  const((1, HEAD_DIM)), const((1, HEAD_DIM)), const((width, D)),
                  const((1, D))],
        out_specs=[pl.BlockSpec((tt, D), lambda b, t: (row(b, t), 0)),
                   pl.BlockSpec((tt, D), lambda b, t: (row(b, t), 0))],
        out_shape=[jax.ShapeDtypeStruct((B * T, D), F32),
                   jax.ShapeDtypeStruct((B * T, D), BF16)],
        scratch_shapes=[pltpu.VMEM((M, width), BF16), pltpu.VMEM((M, width), BF16)],
        compiler_params=_cparams("parallel", "arbitrary"),
        name="mem_attention",
    )(x, q_gain, w_mq, mem, kv_gain, w_mkv, mq_norm, mk_norm, w_mo, next_norm)


def kernel(x, mem, norm_mix, w_in, gate_b, q_norm, k_norm_cmp, k_norm_slc, k_norm_win, cmp_pos_k, cmp_pos_v, cmp_k_w1, cmp_k_w2, cmp_v_w1, cmp_v_w2, conv_w, conv_b, conv_ln_g, conv_ln_b, w_out, norm_mem_q, norm_mem_kv, w_mq, w_mk, w_mv, mq_norm, mk_norm, w_mo, norm_ffn, w_gate, w_up, w_down):
    B, T, D = x.shape
    M = mem.shape[1]
    depth = w_in.shape[0]
    G, hg = NSA_KV_HEADS, NSA_GROUP
    C = D - NSA_WIDTH
    n_gate = N_BRANCH * NSA_HEADS
    kv_all = 6 * KV_WIDTH
    assert w_in.shape[2] == NSA_WIDTH + kv_all + n_gate + 2 * C
    assert C == NSA_WIDTH, "column blocking below assumes equal NSA and conv widths"

    row1 = lambda v: v.reshape(1, -1)
    xf = x.reshape(B * T, D)
    memf = mem.reshape(B * M, D)
    u0 = NSA_WIDTH + kv_all + n_gate
    for l in range(depth):
        w = w_in[l]
        w_t = jnp.swapaxes(w, 0, 1)
        wg = w[:, NSA_WIDTH + kv_all:u0].reshape(D, G, hg * N_BRANCH)
        wg = jnp.pad(wg, ((0, 0), (0, 0), (0, LANES - hg * N_BRANCH))).reshape(D, G * LANES)
        gb = jnp.pad(gate_b[l].reshape(G, hg * N_BRANCH), ((0, 0), (0, LANES - hg * N_BRANCH))).reshape(1, G * LANES)

        xn = rmsnorm_bf16(xf, row1(norm_mix[l]), tm=TILE_RMSNORM)
        z, gl = in_projection(xn, w_t, wg, tm=TILE_IN_PROJ[0], tn=TILE_IN_PROJ[1],
                              segments=((0, NSA_WIDTH), (u0, 2 * C), (NSA_WIDTH, kv_all)))
        kv_col0 = NSA_WIDTH + 2 * C
        kc, vc, ks, vsT, kw, vwT = kv_prep(z, row1(k_norm_slc[l]), row1(k_norm_win[l]),
                                           B=B, T=T, col0=kv_col0, tt=TILE_KV_PREP)
        kcmp, vcmpT = compress_kv(kc, vc, cmp_pos_k[l], cmp_pos_v[l],
                                  cmp_k_w1[l].astype(BF16), cmp_k_w2[l].astype(BF16),
                                  cmp_v_w1[l].astype(BF16), cmp_v_w2[l].astype(BF16), row1(k_norm_cmp[l]),
                                  n_blk=T // SEL_BLK)
        o_nsa, w_gate_bf16, w_up_bf16 = nsa_attention(z, gl, gb, row1(q_norm[l]), kcmp, vcmpT, ks, vsT, kw, vwT,
                                                      w_gate[l], w_up[l], B=B, T=T)
        o_conv, w_out_bf16, w_down_bf16 = conformer_conv(
            z, conv_w[l], row1(conv_b[l]), row1(conv_ln_g[l]), row1(conv_ln_b[l]), [w_out[l], w_down[l]],
            B=B, T=T, col0=NSA_WIDTH, C=C, tt=TILE_CONV)
        xf = concat_matmul_residual(o_nsa, o_conv, w_out_bf16, xf, tm=TILE_OUT_PROJ[0], tn=TILE_OUT_PROJ[1])

        w_mkv = jnp.concatenate([w_mk[l], w_mv[l]], axis=1).astype(BF16)
        xf, xn = mem_attention(xf, row1(norm_mem_q[l]), w_mq[l].astype(BF16),
                               memf, row1(norm_mem_kv[l]), w_mkv,
                               row1(mq_norm[l]), row1(mk_norm[l]), w_mo[l].astype(BF16),
                               row1(norm_ffn[l]), B=B, T=T, M=M, tt=TILE_MEM_ATTN)

        hidden = swiglu_up(xn, w_gate_bf16, w_up_bf16, tm=TILE_FFN_UP[0], tn=TILE_FFN_UP[1])
        xf = matmul_residual(hidden, w_down_bf16, xf, tm=TILE_FFN_DOWN[0], tn=TILE_FFN_DOWN[1])
    return xf.reshape(B, T, D)
```

```python
import functools

import ml_dtypes
import numpy as np
import jax
import jax.numpy as jnp
from jax import lax
from jax.experimental import pallas as pl
from jax.experimental.pallas import tpu as pltpu

F32 = jnp.float32
BF16 = jnp.bfloat16

HEAD_DIM = 128
NSA_HEADS = 8
NSA_KV_HEADS = 2
NSA_GROUP = NSA_HEADS // NSA_KV_HEADS
NSA_WIDTH = NSA_HEADS * HEAD_DIM
KV_WIDTH = NSA_KV_HEADS * HEAD_DIM
N_BRANCH = 3
CONV_KERNEL = 31
CMP_LEN = 32
CMP_STRIDE = 16
SEL_BLK = 64
N_SEL = 16
WINDOW = 512
MEM_HEADS = 4
NEG_INF = -1e30
FORCE_BONUS = 1e3
RMS_EPS = 1e-6
LN_EPS = 1e-5

LANES = 128
SUBLANES = 8
VMEM_LIMIT_BYTES = 56 * 1024 * 1024

Q_TILE = 256
SEL_CHUNK = 256
WIN_CHUNK = 256
NSA_COLS_PER_DOT = 512
CONV_HALO = 32

TILE_RMSNORM = 512
TILE_IN_PROJ = (2048, 512)
TILE_KV_PREP = 1024
TILE_CONV = 256
TILE_OUT_PROJ = (2048, 512)
TILE_MEM_ATTN = 512
TILE_FFN_UP = (1024, 512)
TILE_FFN_DOWN = (1024, 512)


def _cparams(*sem):
    return pltpu.CompilerParams(dimension_semantics=sem, vmem_limit_bytes=VMEM_LIMIT_BYTES)


def _rms(x, g, eps=RMS_EPS):
    ms = jnp.mean(x * x, axis=-1, keepdims=True)
    return x * lax.rsqrt(ms + eps) * g


def _silu(x):
    return x * jax.nn.sigmoid(x)


def _rmsnorm_body(x_ref, g_ref, o_ref):
    o_ref[...] = _rms(x_ref[...], g_ref[...]).astype(o_ref.dtype)


def rmsnorm_bf16(x, g, *, tm):
    M, K = x.shape
    return pl.pallas_call(
        _rmsnorm_body,
        grid=(M // tm,),
        in_specs=[pl.BlockSpec((tm, K), lambda i: (i, 0)), pl.BlockSpec((1, K), lambda i: (0, 0))],
        out_specs=pl.BlockSpec((tm, K), lambda i: (i, 0)),
        out_shape=jax.ShapeDtypeStruct((M, K), BF16),
        compiler_params=_cparams("parallel"),
        name="rmsnorm",
    )(x, g)


def _in_proj_body(xn_ref, wt_ref, wg_ref, o_ref, og_ref):
    @pl.when(pl.program_id(1) == 0)
    def _():
        og_ref[...] = jnp.dot(xn_ref[...], wg_ref[...].astype(BF16), preferred_element_type=F32)

    o_ref[...] = lax.dot_general(xn_ref[...], wt_ref[...].astype(BF16), (((1,), (1,)), ((), ())),
                                 preferred_element_type=F32).astype(o_ref.dtype)


def _swiglu_body(x_ref, wg_ref, wu_ref, o_ref):
    x = x_ref[...]
    a = jnp.dot(x, wg_ref[...], preferred_element_type=F32)
    b = jnp.dot(x, wu_ref[...], preferred_element_type=F32)
    o_ref[...] = (_silu(a) * b).astype(o_ref.dtype)


def in_projection(xn, w_t, wg, *, segments, tm, tn):
    M, K = xn.shape
    NG = wg.shape[1]
    assert all(start % SUBLANES == 0 and size % tn == 0 for start, size in segments)
    N = sum(size for _, size in segments)

    def row_start(i, j):
        group, first = 0, 0
        for seg_start, size in segments:
            group = jnp.where(j >= first, seg_start // SUBLANES + (j - first) * (tn // SUBLANES), group)
            first += size // tn
        return (group * SUBLANES, 0)

    return pl.pallas_call(
        _in_proj_body,
        grid=(M // tm, N // tn),
        in_specs=[pl.BlockSpec((tm, K), lambda i, j: (i, 0)),
                  pl.BlockSpec((pl.Element(tn), pl.Element(K)), row_start),
                  pl.BlockSpec((K, NG), lambda i, j: (0, 0))],
        out_specs=[pl.BlockSpec((tm, tn), lambda i, j: (i, j)),
                   pl.BlockSpec((tm, NG), lambda i, j: (i, 0))],
        out_shape=[jax.ShapeDtypeStruct((M, N), BF16),
                   jax.ShapeDtypeStruct((M, NG), F32)],
        compiler_params=_cparams("parallel", "arbitrary"),
        name="in_proj",
    )(xn, w_t, wg)


def swiglu_up(xn, w_gate, w_up, *, tm, tn):
    assert w_gate.dtype == w_up.dtype == BF16
    M, K = xn.shape
    N = w_gate.shape[1]
    return pl.pallas_call(
        _swiglu_body,
        grid=(M // tm, N // tn),
        in_specs=[pl.BlockSpec((tm, K), lambda i, j: (i, 0)),
                  pl.BlockSpec((K, tn), lambda i, j: (0, j)),
                  pl.BlockSpec((K, tn), lambda i, j: (0, j))],
        out_specs=pl.BlockSpec((tm, tn), lambda i, j: (i, j)),
        out_shape=jax.ShapeDtypeStruct((M, N), BF16),
        compiler_params=_cparams("parallel", "arbitrary"),
        name="ffn_up",
    )(xn, w_gate, w_up)


def _mm_res_body(a_ref, w_ref, r_ref, o_ref):
    o_ref[...] = r_ref[...] + jnp.dot(a_ref[...], w_ref[...], preferred_element_type=F32)


def matmul_residual(a, w, res, *, tm, tn):
    M, K = a.shape
    N = w.shape[1]
    return pl.pallas_call(
        _mm_res_body,
        grid=(M // tm, N // tn),
        in_specs=[pl.BlockSpec((tm, K), lambda i, j: (i, 0)),
                  pl.BlockSpec((K, tn), lambda i, j: (0, j)),
                  pl.BlockSpec((tm, tn), lambda i, j: (i, j))],
        out_specs=pl.BlockSpec((tm, tn), lambda i, j: (i, j)),
        out_shape=jax.ShapeDtypeStruct((M, N), F32),
        compiler_params=_cparams("parallel", "arbitrary"),
        name="matmul_residual",
    )(a, w, res)


def _mm2_res_body(a_ref, b_ref, wa_ref, wb_ref, r_ref, o_ref):
    acc = jnp.dot(a_ref[...], wa_ref[...], preferred_element_type=F32)
    acc = acc + jnp.dot(b_ref[...], wb_ref[...], preferred_element_type=F32)
    o_ref[...] = r_ref[...] + acc


def concat_matmul_residual(a, b, w, res, *, tm, tn):
    M, Ka = a.shape
    Kb = b.shape[1]
    assert Ka == Kb and w.shape[0] == Ka + Kb
    N = w.shape[1]
    return pl.pallas_call(
        _mm2_res_body,
        grid=(M // tm, N // tn),
        in_specs=[pl.BlockSpec((tm, Ka), lambda i, j: (i, 0)),
                  pl.BlockSpec((tm, Kb), lambda i, j: (i, 0)),
                  pl.BlockSpec((Ka, tn), lambda i, j: (0, j)),
                  pl.BlockSpec((Kb, tn), lambda i, j: (1, j)),
                  pl.BlockSpec((tm, tn), lambda i, j: (i, j))],
        out_specs=pl.BlockSpec((tm, tn), lambda i, j: (i, j)),
        out_shape=jax.ShapeDtypeStruct((M, N), F32),
        compiler_params=_cparams("parallel", "arbitrary"),
        name="out_proj",
    )(a, b, w, w, res)


AUG_WIDTH = LANES
V_EXTRA = 16
SLOPE_PIECES = 4
LOG2E = float(np.log2(np.e))


def _bf16_pieces(x, n):
    out, rem = [], np.asarray(x, np.float64)
    for _ in range(n):
        piece = rem.astype(ml_dtypes.bfloat16).astype(np.float64)
        out.append(piece.astype(np.float32))
        rem = rem - piece
    return out


def _position_lanes(pos_hi, pos_lo, onehot, n_blk, shape):
    lane = lax.broadcasted_iota(jnp.int32, shape, 1)
    p = SLOPE_PIECES
    extra = jnp.where((lane >= n_blk) & (lane < n_blk + p), pos_hi,
                      jnp.where((lane >= n_blk + p) & (lane < n_blk + 2 * p), pos_lo, 0.0))
    if onehot is not None:
        extra = jnp.where(lane < n_blk, onehot, extra)
    return extra.astype(BF16)


def _kv_prep_body(c_ref, s_ref, w_ref, kns_ref, knw_ref,
                  kc_o, vc_o, ks_o, vsT_o, kw_o, vwT_o, slab_ref, *, tt, n_blk):
    dh = HEAD_DIM
    shape = (tt, AUG_WIDTH)
    pos = pl.program_id(1) * tt + lax.broadcasted_iota(jnp.int32, shape, 0)
    lane = lax.broadcasted_iota(jnp.int32, shape, 1)
    blk = jnp.right_shift(pos, SEL_BLK.bit_length() - 1)
    pos_hi = (blk * SEL_BLK).astype(F32)
    pos_lo = (pos & (SEL_BLK - 1)).astype(F32)
    aug_sel = _position_lanes(pos_hi, pos_lo, (blk == lane).astype(F32), n_blk, shape)
    aug_win = _position_lanes(pos_hi, pos_lo, None, n_blk, shape)
    for g in range(NSA_KV_HEADS):
        k_cols = slice(g * dh, (g + 1) * dh)
        v_cols = slice(KV_WIDTH + g * dh, KV_WIDTH + (g + 1) * dh)
        for slab, (cols, out) in enumerate(((k_cols, kc_o), (v_cols, vc_o))):
            slab_ref[slab] = c_ref[:, cols].astype(F32)
            for i in range(CMP_STRIDE):
                rows = pl.ds(i, tt // CMP_STRIDE, stride=CMP_STRIDE)
                out[0, g, :, i * dh:(i + 1) * dh] = slab_ref[slab, rows, :]
        ks_o[0, g, :, 0:dh] = _rms(s_ref[:, k_cols].astype(F32), kns_ref[...]).astype(BF16)
        ks_o[0, g, :, dh:dh + AUG_WIDTH] = aug_sel
        kw_o[0, g, :, 0:dh] = _rms(w_ref[:, k_cols].astype(F32), knw_ref[...]).astype(BF16)
        kw_o[0, g, :, dh:dh + AUG_WIDTH] = aug_win
        for v_ref, out, chunk in ((s_ref, vsT_o, SEL_CHUNK), (w_ref, vwT_o, WIN_CHUNK)):
            vT = v_ref[:, v_cols].astype(F32).T.astype(BF16)
            ones_first = (lax.broadcasted_iota(jnp.int32, (V_EXTRA, chunk), 0) == 0).astype(BF16)
            for c in range(tt // chunk):
                out[0, g, c, 0:dh, :] = vT[:, c * chunk:(c + 1) * chunk]
                out[0, g, c, dh:dh + V_EXTRA, :] = ones_first


def kv_prep(z, kn_slc, kn_win, *, B, T, col0, tt):
    G, dh = NSA_KV_HEADS, HEAD_DIM
    nt = T // tt
    n_blk = T // SEL_BLK
    assert n_blk + 2 * SLOPE_PIECES <= AUG_WIDTH
    cb = col0 // (2 * KV_WIDTH)
    row = lambda b, t: b * nt + t
    kdim = dh + AUG_WIDTH
    return pl.pallas_call(
        functools.partial(_kv_prep_body, tt=tt, n_blk=n_blk),
        grid=(B, nt),
        in_specs=[pl.BlockSpec((tt, 2 * KV_WIDTH), lambda b, t: (row(b, t), cb)),
                  pl.BlockSpec((tt, 2 * KV_WIDTH), lambda b, t: (row(b, t), cb + 1)),
                  pl.BlockSpec((tt, 2 * KV_WIDTH), lambda b, t: (row(b, t), cb + 2)),
                  pl.BlockSpec((1, dh), lambda b, t: (0, 0)),
                  pl.BlockSpec((1, dh), lambda b, t: (0, 0))],
        out_specs=[pl.BlockSpec((1, G, tt // CMP_STRIDE, CMP_STRIDE * dh), lambda b, t: (b, 0, t, 0)),
                   pl.BlockSpec((1, G, tt // CMP_STRIDE, CMP_STRIDE * dh), lambda b, t: (b, 0, t, 0)),
                   pl.BlockSpec((1, G, tt, kdim), lambda b, t: (b, 0, t, 0)),
                   pl.BlockSpec((1, G, tt // SEL_CHUNK, dh + V_EXTRA, SEL_CHUNK), lambda b, t: (b, 0, t, 0, 0)),
                   pl.BlockSpec((1, G, tt, kdim), lambda b, t: (b, 0, t, 0)),
                   pl.BlockSpec((1, G, tt // WIN_CHUNK, dh + V_EXTRA, WIN_CHUNK), lambda b, t: (b, 0, t, 0, 0))],
        out_shape=[jax.ShapeDtypeStruct((B, G, T // CMP_STRIDE, CMP_STRIDE * dh), F32),
                   jax.ShapeDtypeStruct((B, G, T // CMP_STRIDE, CMP_STRIDE * dh), F32),
                   jax.ShapeDtypeStruct((B, G, T, kdim), BF16),
                   jax.ShapeDtypeStruct((B, G, T // SEL_CHUNK, dh + V_EXTRA, SEL_CHUNK), BF16),
                   jax.ShapeDtypeStruct((B, G, T, kdim), BF16),
                   jax.ShapeDtypeStruct((B, G, T // WIN_CHUNK, dh + V_EXTRA, WIN_CHUNK), BF16)],
        scratch_shapes=[pltpu.VMEM((2, tt, dh), F32)],
        compiler_params=_cparams("parallel", "parallel"),
        name="kv_prep",
    )(z, z, z, kn_slc, kn_win)


def _compress_body(xk_ref, xv_ref, pk_ref, pv_ref, w1k_ref, w2k_ref, w1v_ref, w2v_ref, kn_ref,
                   kc_o, vcT_o, *, n_blk):
    dh = HEAD_DIM

    def phi(x_ref, p_ref, w1_ref, w2_ref):
        x = x_ref[0, 0]
        first = jnp.dot((x + p_ref[0:1, :]).astype(BF16), w1_ref[0].astype(BF16), preferred_element_type=F32)
        second = jnp.dot((x + p_ref[1:2, :]).astype(BF16), w1_ref[1].astype(BF16), preferred_element_type=F32)
        n = second.shape[0]
        h = first + pltpu.roll(second, n - 1, 0)
        return jnp.dot(_silu(h).astype(BF16), w2_ref[...].astype(BF16), preferred_element_type=F32)

    k = _rms(phi(xk_ref, pk_ref, w1k_ref, w2k_ref), kn_ref[...])
    nc = k.shape[0]
    shape = (nc, AUG_WIDTH)
    start = (lax.broadcasted_iota(jnp.int32, shape, 0) * CMP_STRIDE).astype(F32)
    centre = jnp.full(shape, (CMP_LEN - 1) / 2.0, F32)
    kc_o[0, 0, :, 0:dh] = k.astype(BF16)
    kc_o[0, 0, :, dh:dh + AUG_WIDTH] = _position_lanes(start, centre, None, n_blk, shape)
    vcT_o[0, 0] = phi(xv_ref, pv_ref, w1v_ref, w2v_ref).T.astype(BF16)


def compress_kv(xk, xv, pos_k, pos_v, w1k, w2k, w1v, w2v, kn_cmp, *, n_blk):
    B, G, nc, half = xk.shape
    dh = half // CMP_STRIDE
    assert CMP_LEN == 2 * CMP_STRIDE
    x_spec = pl.BlockSpec((1, 1, nc, half), lambda b, g: (b, g, 0, 0))
    full = lambda shape: pl.BlockSpec(shape, lambda b, g: (0,) * len(shape))
    return pl.pallas_call(
        functools.partial(_compress_body, n_blk=n_blk),
        grid=(B, G),
        in_specs=[x_spec, x_spec, full((2, half)), full((2, half)),
                  full((2, half, dh)), full((dh, dh)), full((2, half, dh)), full((dh, dh)),
                  full((1, dh))],
        out_specs=[pl.BlockSpec((1, 1, nc, dh + AUG_WIDTH), lambda b, g: (b, g, 0, 0)),
                   pl.BlockSpec((1, 1, dh, nc), lambda b, g: (b, g, 0, 0))],
        out_shape=[jax.ShapeDtypeStruct((B, G, nc, dh + AUG_WIDTH), BF16),
                   jax.ShapeDtypeStruct((B, G, dh, nc), BF16)],
        compiler_params=_cparams("parallel", "parallel"),
        name="compress_kv",
    )(xk, xv, pos_k.reshape(2, half), pos_v.reshape(2, half),
      w1k.reshape(2, half, dh), w2k, w1v.reshape(2, half, dh), w2v, kn_cmp)


def _nsa_body(*refs, n_tiles, n_casts):
    (q_ref, qnext_ref, gl_ref, gb_ref, qn_ref, lanes_ref, kc_ref, vcT_ref,
     ks_ref, vsT_ref, kw_ref, vwT_ref, ovT_ref, kmq_ref) = refs[:14]
    cast_in, (o_ref, *cast_out) = refs[14:14 + n_casts], refs[14 + n_casts:15 + 2 * n_casts]
    score_ref, gate_ref, qc_ref, qs_ref, m_ref, acc_ref, ocmp_ref, sbuf_ref = refs[15 + 2 * n_casts:]
    dh, hg, tq = HEAD_DIM, NSA_GROUP, Q_TILE
    cols = hg * tq
    qi = pl.program_id(2)
    contract_last = (((1,), (1,)), ((), ()))
    n_blk = ovT_ref.shape[0]

    scale = dh ** -0.5 * LOG2E
    hc = kmq_ref.shape[1]
    assert cols % hc == 0 and hc % tq == 0
    groups = [slice(i * hc, (i + 1) * hc) for i in range(cols // hc)]
    cw = ck = tq
    assert WIN_CHUNK == SEL_CHUNK == tq == kmq_ref.shape[0] and WINDOW % cw == 0
    n_back = WINDOW // cw
    nc = kc_ref.shape[2]
    WIN, SEL = 0, 1

    def reset(b):
        m_ref[b] = jnp.full((1, cols), NEG_INF, F32)
        acc_ref[b] = jnp.zeros((dh + V_EXTRA, cols), F32)

    def issue_scores(queries_ref, parts, unit):
        for k, _, _ in parts:
            for cs in groups:
                sbuf_ref[unit, 0:k.shape[0], :] = lax.dot_general(
                    k, queries_ref[cs, :], contract_last, preferred_element_type=F32)
                unit += 1

    def softmax_update(b, parts, unit):
        for (k, v_pieces, mask) in parts:
            for cs in groups:
                s_ref = sbuf_ref.at[unit, 0:k.shape[0], :]
                unit += 1
                if mask is not None:
                    s_ref[...] = jnp.where(mask(), s_ref[...], NEG_INF)
                m_old = m_ref[b, :, cs]
                m_new = jnp.maximum(m_old, jnp.max(s_ref[...], axis=0, keepdims=True))
                alpha = jnp.exp2(m_old - m_new)
                p = jnp.exp2(s_ref[...] - m_new).astype(BF16)
                m_ref[b, :, cs] = m_new
                pv = None
                for i, vT in enumerate(v_pieces):
                    n = vT.shape[1]
                    d = jnp.dot(vT, p[i * n:(i + 1) * n, :], preferred_element_type=F32)
                    pv = d if pv is None else pv + d
                acc_ref[b, :, cs] = alpha * acc_ref[b, :, cs] + pv

    def result(b):
        return acc_ref[b, 0:dh, :] * (1.0 / acc_ref[b, dh:dh + 1, :])

    def prepare(src_ref, tile, slot):
        t0 = tile * tq
        for h in range(hg):
            rows = slice(h * tq, (h + 1) * tq)
            q_h = (_rms(src_ref[:, h * dh:(h + 1) * dh].astype(F32), qn_ref[...]) * scale).astype(BF16)
            qc_ref[slot, rows, 0:dh] = q_h
            qs_ref[slot, rows, 0:dh] = q_h
            qc_ref[slot, rows, dh:dh + AUG_WIDTH] = jnp.broadcast_to(
                lanes_ref[0, hg + h:hg + h + 1, :], (tq, AUG_WIDTH)).astype(BF16)

        s = lax.dot_general(kc_ref[0, 0], qc_ref[slot], contract_last, preferred_element_type=F32)
        c_idx = lax.broadcasted_iota(jnp.int32, (nc, cols), 0)
        t_c = t0 + (lax.broadcasted_iota(jnp.int32, (nc, cols), 1) & (tq - 1))
        mask = (c_idx * CMP_STRIDE + (CMP_LEN - 1)) <= t_c
        s = jnp.where(mask, s, NEG_INF)
        m_c = jnp.max(s, axis=0, keepdims=True)
        p_c = jnp.where(mask, jnp.exp2(s - m_c), 0.0)
        l_c = jnp.sum(p_c, axis=0, keepdims=True)
        p_c = p_c * jnp.where(l_c > 0.0, 1.0 / l_c, 0.0)
        ocmp_ref[slot] = jnp.dot(vcT_ref[0, 0], p_c.astype(BF16), preferred_element_type=F32)

        p_sum = p_c[:, 0:tq]
        for h in range(1, hg):
            p_sum = p_sum + p_c[:, h * tq:(h + 1) * tq]
        hi = p_sum.astype(BF16)
        rem = p_sum - hi.astype(F32)
        mid = rem.astype(BF16)
        lo = (rem - mid.astype(F32)).astype(BF16)
        ovT = ovT_ref[...]
        imp = (jnp.dot(ovT, hi, preferred_element_type=F32)
               + jnp.dot(ovT, mid, preferred_element_type=F32)
               + jnp.dot(ovT, lo, preferred_element_type=F32))
        blk = lax.broadcasted_iota(jnp.int32, (n_blk, tq), 0)
        t_b = t0 + lax.broadcasted_iota(jnp.int32, (n_blk, tq), 1)
        cur = jnp.right_shift(t_b, SEL_BLK.bit_length() - 1)
        valid = blk * SEL_BLK <= t_b
        forced = (blk == 0) | (blk == cur) | (blk == cur - 1)
        score = jnp.where(valid, imp + FORCE_BONUS * forced.astype(F32), NEG_INF)
        score_ref[...] = score
        rank = jnp.zeros((n_blk, tq), F32)
        for j in range(n_blk):
            sj = score_ref[j:j + 1, :]
            beats = (sj > score) | ((sj == score) & (blk > j))
            rank = rank + beats.astype(F32)
        block_bias = jnp.where(rank < float(min(N_SEL, n_blk)), 0.0, NEG_INF)

        assert 2 * SLOPE_PIECES <= 8
        ones_rows = (lax.broadcasted_iota(jnp.int32, (8, tq), 0) < 2 * SLOPE_PIECES).astype(F32)
        extra_t = jnp.concatenate([block_bias, ones_rows,
                                   jnp.zeros((AUG_WIDTH - n_blk - 8, tq), F32)], axis=0).T
        for h in range(hg):
            qs_ref[slot, h * tq:(h + 1) * tq, dh:dh + AUG_WIDTH] = (
                extra_t * lanes_ref[0, h:h + 1, :]).astype(BF16)

    def win_part(c, mask=None):
        return kw_ref[0, 0, c * cw:(c + 1) * cw, :], [vwT_ref[0, 0, c]], mask

    def sel_part(c, n=1, mask=None):
        return ks_ref[0, 0, c * ck:(c + n) * ck, :], [vsT_ref[0, 0, c + i] for i in range(n)], mask

    def tile_step(tile):
        slot = tile % 2
        below_diagonal = lambda: kmq_ref[...] <= 0
        win_parts = [win_part(tile, below_diagonal)]
        win_parts += [win_part(tile - j) for j in range(1, n_back) if tile - j >= 0]
        if tile - n_back >= 0:
            win_parts.append(win_part(tile - n_back, lambda: kmq_ref[...] > 0))
        sel_parts = [sel_part(tile, mask=below_diagonal)]
        sel_parts += [sel_part(c, n=min(2, tile - c)) for c in range(0, tile, 2)]
        sel_unit = len(win_parts) * len(groups)
        issue_scores(qc_ref.at[slot], win_parts, 0)
        issue_scores(qs_ref.at[slot], sel_parts, sel_unit)
        if tile + 1 < n_tiles:
            prepare(qnext_ref, tile + 1, 1 - slot)
        reset(WIN)
        softmax_update(WIN, win_parts, 0)
        reset(SEL)
        softmax_update(SEL, sel_parts, sel_unit)

        gate_ref[...] = jax.nn.sigmoid(gl_ref[...] + gb_ref[...]).T
        o_win, o_slc = result(WIN), result(SEL)
        for h in range(hg):
            cs = slice(h * tq, (h + 1) * tq)
            gate = lambda branch: gate_ref[h * N_BRANCH + branch:h * N_BRANCH + branch + 1, :]
            o_h = gate(0) * ocmp_ref[slot, :, cs] + gate(1) * o_slc[:, cs] + gate(2) * o_win[:, cs]
            o_ref[:, h * dh:(h + 1) * dh] = o_h.T.astype(o_ref.dtype)

    for src, dst in zip(cast_in, cast_out):
        dst[...] = src[...].astype(dst.dtype)

    @pl.when(qi == 0)
    def _():
        prepare(q_ref, 0, 0)

    for tile in range(n_tiles):
        pl.when(qi == tile)(functools.partial(tile_step, tile))


def nsa_attention(z, gl, gate_b, q_norm, kcmp, vcmpT, ks, vsT, kw, vwT, casts, *, B, T):
    G, hg, dh, tq = NSA_KV_HEADS, NSA_GROUP, HEAD_DIM, Q_TILE
    nq = T // tq
    nc = kcmp.shape[2]
    n_blk = T // SEL_BLK
    kdim = dh + AUG_WIDTH
    cols = hg * tq
    assert SEL_CHUNK == WIN_CHUNK == tq
    n_units = (WINDOW // WIN_CHUNK + 1 + 1 + nq // 2) * (cols // NSA_COLS_PER_DOT)
    assert SEL_CHUNK % SEL_BLK == 0 and T % SEL_CHUNK == 0 and gl.shape[1] == G * LANES
    assert SEL_CHUNK % tq == 0 and n_blk % 8 == 0 and n_blk + 8 <= AUG_WIDTH

    slopes = np.exp2(-8.0 * (np.arange(NSA_HEADS, dtype=np.float64) + 1.0) / NSA_HEADS).reshape(G, hg)
    lanes = np.zeros((G, 2 * hg, AUG_WIDTH), np.float32)
    lanes[:, :hg, :n_blk] = 1.0
    for i, piece in enumerate(_bf16_pieces(slopes * LOG2E, SLOPE_PIECES)):
        for lane in (n_blk + i, n_blk + SLOPE_PIECES + i):
            lanes[:, :hg, lane] = piece
            lanes[:, hg:, lane] = piece
    cs = np.arange(nc)[:, None] * CMP_STRIDE
    bs = np.arange(n_blk)[None, :] * SEL_BLK
    overlap = np.clip(np.minimum(cs + CMP_LEN, bs + SEL_BLK) - np.maximum(cs, bs), 0, None) / CMP_LEN
    overlap[(T - CMP_LEN) // CMP_STRIDE + 1:] = 0.0
    ovT = jnp.asarray(overlap.T, dtype=BF16)
    hc = NSA_COLS_PER_DOT
    kmq = jnp.asarray(np.arange(SEL_CHUNK)[:, None] - (np.arange(hc)[None, :] % tq), dtype=jnp.int32)

    row = lambda b, qi: b * nq + qi
    per_bg = lambda shape: pl.BlockSpec((1, 1) + shape, lambda b, g, qi: (b, g) + (0,) * len(shape))
    const = lambda shape: pl.BlockSpec(shape, lambda b, g, qi: (0,) * len(shape))
    n_steps = B * G * nq
    assert all(w.shape[0] % (n_steps * 2 * SUBLANES) == 0 for w in casts)
    slab = lambda w: pl.BlockSpec((w.shape[0] // n_steps, w.shape[1]),
                                  lambda b, g, qi: ((b * G + g) * nq + qi, 0))
    return pl.pallas_call(
        functools.partial(_nsa_body, n_tiles=nq, n_casts=len(casts)),
        grid=(B, G, nq),
        in_specs=[pl.BlockSpec((tq, hg * dh), lambda b, g, qi: (row(b, qi), g)),
                  pl.BlockSpec((tq, hg * dh), lambda b, g, qi: (row(b, jnp.minimum(qi + 1, nq - 1)), g)),
                  pl.BlockSpec((tq, LANES), lambda b, g, qi: (row(b, qi), g)),
                  pl.BlockSpec((1, LANES), lambda b, g, qi: (0, g)),
                  const((1, dh)),
                  pl.BlockSpec((1, 2 * hg, AUG_WIDTH), lambda b, g, qi: (g, 0, 0)),
                  per_bg((nc, kdim)), per_bg((dh, nc)),
                  per_bg((T, kdim)), per_bg((T // SEL_CHUNK, dh + V_EXTRA, SEL_CHUNK)),
                  per_bg((T, kdim)), per_bg((T // WIN_CHUNK, dh + V_EXTRA, WIN_CHUNK)),
                  const((n_blk, nc)), const((SEL_CHUNK, hc))] + [slab(w) for w in casts],
        out_specs=[pl.BlockSpec((tq, hg * dh), lambda b, g, qi: (row(b, qi), g))]
                  + [slab(w) for w in casts],
        out_shape=[jax.ShapeDtypeStruct((B * T, NSA_WIDTH), BF16)]
                  + [jax.ShapeDtypeStruct(w.shape, BF16) for w in casts],
        scratch_shapes=[pltpu.VMEM((n_blk, tq), F32), pltpu.VMEM((LANES, tq), F32),
                        pltpu.VMEM((2, cols, kdim), BF16), pltpu.VMEM((2, cols, kdim), BF16),
                        pltpu.VMEM((2, 1, cols), F32),
                        pltpu.VMEM((2, dh + V_EXTRA, cols), F32), pltpu.VMEM((2, dh, cols), F32),
                        pltpu.VMEM((n_units, 2 * SEL_CHUNK, NSA_COLS_PER_DOT), F32)],
        compiler_params=_cparams("parallel", "parallel", "arbitrary"),
        name="nsa_attention",
    )(z, z, gl, gate_b, q_norm, jnp.asarray(lanes), kcmp, vcmpT, ks, vsT, kw, vwT, ovT, kmq, *casts)


def _conv_body(*refs, tt, n_casts):
    a_ref, b_ref, ah_ref, bh_ref, w_ref, cb_ref, lg_ref, lb_ref = refs[:8]
    cast_in, (o_ref, *cast_out) = refs[8:8 + n_casts], refs[8 + n_casts:9 + 2 * n_casts]
    h_ref, g_ref, y_ref = refs[9 + 2 * n_casts:]
    for src, dst in zip(cast_in, cast_out):
        dst[...] = src[...].astype(dst.dtype)
    C = a_ref.shape[1]
    halo = ah_ref[...].astype(F32) * jax.nn.sigmoid(bh_ref[...].astype(F32))
    h_ref[0:CONV_HALO, :] = jnp.where(pl.program_id(1) == 0, 0.0, halo)
    h_ref[CONV_HALO:CONV_HALO + tt, :] = a_ref[...].astype(F32) * jax.nn.sigmoid(b_ref[...].astype(F32))
    h_ref[CONV_HALO + tt:CONV_HALO + tt + SUBLANES, :] = jnp.zeros((SUBLANES, C), F32)
    lead = CONV_HALO - (CONV_KERNEL - 1)
    rows = tt + SUBLANES
    for s in range(C // LANES):
        cs = slice(s * LANES, (s + 1) * LANES)
        acc = None
        for r in range(SUBLANES):
            group = None
            for o in range(lead, lead + CONV_KERNEL):
                if o % SUBLANES == r:
                    term = w_ref[o - lead:o - lead + 1, cs] * h_ref[o - r:o - r + rows, cs]
                    group = term if group is None else group + term
            if r == 0:
                piece = group[0:tt, :]
            else:
                g_ref[r] = group
                piece = g_ref[r, r:r + tt, :]
            acc = piece if acc is None else acc + piece
        y_ref[:, cs] = acc + cb_ref[:, cs]
    y = y_ref[...]
    mu = jnp.mean(y, axis=-1, keepdims=True)
    var = jnp.mean(jnp.square(y - mu), axis=-1, keepdims=True)
    yn = (y - mu) * lax.rsqrt(var + LN_EPS) * lg_ref[...] + lb_ref[...]
    o_ref[...] = _silu(yn).astype(o_ref.dtype)


def conformer_conv(z, conv_w, conv_b, ln_g, ln_b, casts, *, B, T, col0, C, tt):
    nt = T // tt
    ca = col0 // C
    hb = tt // CONV_HALO
    row = lambda b, t: b * nt + t
    halo_row = lambda b, t: jnp.maximum(row(b, t) * hb - 1, 0)
    const = lambda shape: pl.BlockSpec(shape, lambda b, t: (0,) * len(shape))
    n_steps = B * nt
    assert all(w.shape[0] % (n_steps * 2 * SUBLANES) == 0 for w in casts)
    slab = lambda w: pl.BlockSpec((w.shape[0] // n_steps, w.shape[1]), lambda b, t: (row(b, t), 0))
    return pl.pallas_call(
        functools.partial(_conv_body, tt=tt, n_casts=len(casts)),
        grid=(B, nt),
        in_specs=[pl.BlockSpec((tt, C), lambda b, t: (row(b, t), ca)),
                  pl.BlockSpec((tt, C), lambda b, t: (row(b, t), ca + 1)),
                  pl.BlockSpec((CONV_HALO, C), lambda b, t: (halo_row(b, t), ca)),
                  pl.BlockSpec((CONV_HALO, C), lambda b, t: (halo_row(b, t), ca + 1)),
                  const((CONV_KERNEL, C)), const((1, C)), const((1, C)), const((1, C))]
                 + [slab(w) for w in casts],
        out_specs=[pl.BlockSpec((tt, C), lambda b, t: (row(b, t), 0))] + [slab(w) for w in casts],
        out_shape=[jax.ShapeDtypeStruct((B * T, C), BF16)]
                  + [jax.ShapeDtypeStruct(w.shape, BF16) for w in casts],
        scratch_shapes=[pltpu.VMEM((CONV_HALO + tt + SUBLANES, C), F32),
                        pltpu.VMEM((SUBLANES, tt + SUBLANES, LANES), F32),
                        pltpu.VMEM((tt, C), F32)],
        compiler_params=_cparams("parallel", "parallel"),
        name="conformer_conv",
    )(z, z, z, z, conv_w, conv_b, ln_g, ln_b, *casts)


def _mem_attn_body(x_ref, gq_ref, wq_ref, mem_ref, gkv_ref, wk_ref, wv_ref, qn_ref, kn_ref, wo_ref, gn_ref,
                   o_ref, on_ref, k_scr, v_scr):
    dh = HEAD_DIM
    width = MEM_HEADS * dh
    scale = dh ** -0.5
    heads = [slice(h * dh, (h + 1) * dh) for h in range(MEM_HEADS)]

    @pl.when(pl.program_id(1) == 0)
    def _():
        mem_n = _rms(mem_ref[...], gkv_ref[...]).astype(BF16)
        k = jnp.dot(mem_n, wk_ref[...], preferred_element_type=F32)
        for cs in heads:
            k_scr[:, cs] = _rms(k[:, cs], kn_ref[...]).astype(BF16)
        v_scr[...] = jnp.dot(mem_n, wv_ref[...], preferred_element_type=F32).astype(BF16)

    q_all = jnp.dot(_rms(x_ref[...], gq_ref[...]).astype(BF16), wq_ref[...], preferred_element_type=F32)
    scores = []
    for cs in heads:
        q = (_rms(q_all[:, cs], qn_ref[...]) * scale).astype(BF16)
        scores.append(lax.dot_general(q, k_scr[:, cs], (((1,), (1,)), ((), ())), preferred_element_type=F32))
    outs = []
    for h, s in enumerate(scores):
        v = v_scr[:, heads[h]]
        p = jnp.exp(s - jnp.max(s, axis=-1, keepdims=True))
        inv = 1.0 / jnp.sum(p, axis=-1, keepdims=True)
        outs.append((jnp.dot(p.astype(BF16), v, preferred_element_type=F32) * inv).astype(BF16))
    o = jnp.concatenate(outs, axis=1)
    y = x_ref[...] + jnp.dot(o, wo_ref[...], preferred_element_type=F32)
    o_ref[...] = y
    on_ref[...] = _rms(y, gn_ref[...]).astype(on_ref.dtype)


def mem_attention(x, q_gain, w_mq, mem, kv_gain, w_mk, w_mv, mq_norm, mk_norm, w_mo, next_norm, *, B, T, M, tt):
    D = x.shape[1]
    width = MEM_HEADS * HEAD_DIM
    nt = T // tt
    row = lambda b, t: b * nt + t
    const = lambda shape: pl.BlockSpec(shape, lambda b, t: (0,) * len(shape))
    return pl.pallas_call(
        _mem_attn_body,
        grid=(B, nt),
        in_specs=[pl.BlockSpec((tt, D), lambda b, t: (row(b, t), 0)),
                  const((1, D)), const((D, width)),
                  pl.BlockSpec((M, D), lambda b, t: (b, 0)),
                  const((1, D)), const((D, width)), const((D, width)),
                  const((1, HEAD_DIM)), const((1, HEAD_DIM)), const((width, D)),
                  const((1, D))],
        out_specs=[pl.BlockSpec((tt, D), lambda b, t: (row(b, t), 0)),
                   pl.BlockSpec((tt, D), lambda b, t: (row(b, t), 0))],
        out_shape=[jax.ShapeDtypeStruct((B * T, D), F32),
                   jax.ShapeDtypeStruct((B * T, D), BF16)],
        scratch_shapes=[pltpu.VMEM((M, width), BF16), pltpu.VMEM((M, width), BF16)],
        compiler_params=_cparams("parallel", "arbitrary"),
        name="mem_attention",
    )(x, q_gain, w_mq, mem, kv_gain, w_mk, w_mv, mq_norm, mk_norm, w_mo, next_norm)


def kernel(x, mem, norm_mix, w_in, gate_b, q_norm, k_norm_cmp, k_norm_slc, k_norm_win, cmp_pos_k, cmp_pos_v, cmp_k_w1, cmp_k_w2, cmp_v_w1, cmp_v_w2, conv_w, conv_b, conv_ln_g, conv_ln_b, w_out, norm_mem_q, norm_mem_kv, w_mq, w_mk, w_mv, mq_norm, mk_norm, w_mo, norm_ffn, w_gate, w_up, w_down):
    B, T, D = x.shape
    M = mem.shape[1]
    depth = w_in.shape[0]
    G, hg = NSA_KV_HEADS, NSA_GROUP
    C = D - NSA_WIDTH
    n_gate = N_BRANCH * NSA_HEADS
    kv_all = 6 * KV_WIDTH
    assert w_in.shape[2] == NSA_WIDTH + kv_all + n_gate + 2 * C
    assert C == NSA_WIDTH, "column blocking below assumes equal NSA and conv widths"

    row1 = lambda v: v.reshape(1, -1)
    xf = x.reshape(B * T, D)
    memf = mem.reshape(B * M, D)
    u0 = NSA_WIDTH + kv_all + n_gate
    for l in range(depth):
        w = w_in[l]
        w_t = jnp.swapaxes(w, 0, 1)
        wg = w[:, NSA_WIDTH + kv_all:u0].reshape(D, G, hg * N_BRANCH)
        wg = jnp.pad(wg, ((0, 0), (0, 0), (0, LANES - hg * N_BRANCH))).reshape(D, G * LANES)
        gb = jnp.pad(gate_b[l].reshape(G, hg * N_BRANCH), ((0, 0), (0, LANES - hg * N_BRANCH))).reshape(1, G * LANES)

        xn = rmsnorm_bf16(xf, row1(norm_mix[l]), tm=TILE_RMSNORM)
        z, gl = in_projection(xn, w_t, wg, tm=TILE_IN_PROJ[0], tn=TILE_IN_PROJ[1],
                              segments=((0, NSA_WIDTH), (u0, 2 * C), (NSA_WIDTH, kv_all)))
        kv_col0 = NSA_WIDTH + 2 * C
        kc, vc, ks, vsT, kw, vwT = kv_prep(z, row1(k_norm_slc[l]), row1(k_norm_win[l]),
                                           B=B, T=T, col0=kv_col0, tt=TILE_KV_PREP)
        kcmp, vcmpT = compress_kv(kc, vc, cmp_pos_k[l], cmp_pos_v[l],
                                  cmp_k_w1[l], cmp_k_w2[l], cmp_v_w1[l], cmp_v_w2[l], row1(k_norm_cmp[l]),
                                  n_blk=T // SEL_BLK)
        o_nsa, w_gate_bf16, w_up_bf16, w_mq_bf16, w_mk_bf16, w_mv_bf16 = nsa_attention(
            z, gl, gb, row1(q_norm[l]), kcmp, vcmpT, ks, vsT, kw, vwT,
            [w_gate[l], w_up[l], w_mq[l], w_mk[l], w_mv[l]], B=B, T=T)
        o_conv, w_out_bf16, w_down_bf16, w_mo_bf16 = conformer_conv(
            z, conv_w[l], row1(conv_b[l]), row1(conv_ln_g[l]), row1(conv_ln_b[l]),
            [w_out[l], w_down[l], w_mo[l]], B=B, T=T, col0=NSA_WIDTH, C=C, tt=TILE_CONV)
        xf = concat_matmul_residual(o_nsa, o_conv, w_out_bf16, xf, tm=TILE_OUT_PROJ[0], tn=TILE_OUT_PROJ[1])

        xf, xn = mem_attention(xf, row1(norm_mem_q[l]), w_mq_bf16,
                               memf, row1(norm_mem_kv[l]), w_mk_bf16, w_mv_bf16,
                               row1(mq_norm[l]), row1(mk_norm[l]), w_mo_bf16,
                               row1(norm_ffn[l]), B=B, T=T, M=M, tt=TILE_MEM_ATTN)

        hidden = swiglu_up(xn, w_gate_bf16, w_up_bf16, tm=TILE_FFN_UP[0], tn=TILE_FFN_UP[1])
        xf = matmul_residual(hidden, w_down_bf16, xf, tm=TILE_FFN_DOWN[0], tn=TILE_FFN_DOWN[1])
    return xf.reshape(B, T, D)
```

```python
import functools

import ml_dtypes
import numpy as np
import jax
import jax.numpy as jnp
from jax import lax
from jax.experimental import pallas as pl
from jax.experimental.pallas import tpu as pltpu

F32 = jnp.float32
BF16 = jnp.bfloat16

HEAD_DIM = 128
NSA_HEADS = 8
NSA_KV_HEADS = 2
NSA_GROUP = NSA_HEADS // NSA_KV_HEADS
NSA_WIDTH = NSA_HEADS * HEAD_DIM
KV_WIDTH = NSA_KV_HEADS * HEAD_DIM
N_BRANCH = 3
CONV_KERNEL = 31
CMP_LEN = 32
CMP_STRIDE = 16
SEL_BLK = 64
N_SEL = 16
WINDOW = 512
MEM_HEADS = 4
NEG_INF = -1e30
FORCE_BONUS = 1e3
RMS_EPS = 1e-6
LN_EPS = 1e-5

LANES = 128
SUBLANES = 8
VMEM_LIMIT_BYTES = 56 * 1024 * 1024

Q_TILE = 256
SEL_CHUNK = 256
WIN_CHUNK = 256
NSA_COLS_PER_DOT = 512
CONV_HALO = 32

TILE_RMSNORM = 512
TILE_IN_PROJ = (2048, 512)
TILE_KV_PREP = 1024
TILE_CONV = 256
TILE_OUT_PROJ = (2048, 512)
TILE_MEM_ATTN = 512
TILE_FFN_UP = (1024, 512)
TILE_FFN_DOWN = (1024, 512)


def _cparams(*sem):
    return pltpu.CompilerParams(dimension_semantics=sem, vmem_limit_bytes=VMEM_LIMIT_BYTES)


def _rms(x, g, eps=RMS_EPS):
    ms = jnp.mean(x * x, axis=-1, keepdims=True)
    return x * lax.rsqrt(ms + eps) * g


def _silu(x):
    return x * jax.nn.sigmoid(x)


def _rmsnorm_body(x_ref, g_ref, o_ref):
    o_ref[...] = _rms(x_ref[...], g_ref[...]).astype(o_ref.dtype)


def rmsnorm_bf16(x, g, *, tm):
    M, K = x.shape
    return pl.pallas_call(
        _rmsnorm_body,
        grid=(M // tm,),
        in_specs=[pl.BlockSpec((tm, K), lambda i: (i, 0)), pl.BlockSpec((1, K), lambda i: (0, 0))],
        out_specs=pl.BlockSpec((tm, K), lambda i: (i, 0)),
        out_shape=jax.ShapeDtypeStruct((M, K), BF16),
        compiler_params=_cparams("parallel"),
        name="rmsnorm",
    )(x, g)


def _in_proj_body(xn_ref, wt_ref, wg_ref, o_ref, og_ref):
    @pl.when(pl.program_id(1) == 0)
    def _():
        og_ref[...] = jnp.dot(xn_ref[...], wg_ref[...].astype(BF16), preferred_element_type=F32)

    o_ref[...] = lax.dot_general(xn_ref[...], wt_ref[...].astype(BF16), (((1,), (1,)), ((), ())),
                                 preferred_element_type=F32).astype(o_ref.dtype)


def _swiglu_body(x_ref, wg_ref, wu_ref, o_ref):
    x = x_ref[...]
    a = jnp.dot(x, wg_ref[...], preferred_element_type=F32)
    b = jnp.dot(x, wu_ref[...], preferred_element_type=F32)
    o_ref[...] = (_silu(a) * b).astype(o_ref.dtype)


def in_projection(xn, w_t, wg, *, segments, tm, tn):
    M, K = xn.shape
    NG = wg.shape[1]
    assert all(start % SUBLANES == 0 and size % tn == 0 for start, size in segments)
    N = sum(size for _, size in segments)

    def row_start(i, j):
        group, first = 0, 0
        for seg_start, size in segments:
            group = jnp.where(j >= first, seg_start // SUBLANES + (j - first) * (tn // SUBLANES), group)
            first += size // tn
        return (group * SUBLANES, 0)

    return pl.pallas_call(
        _in_proj_body,
        grid=(M // tm, N // tn),
        in_specs=[pl.BlockSpec((tm, K), lambda i, j: (i, 0)),
                  pl.BlockSpec((pl.Element(tn), pl.Element(K)), row_start),
                  pl.BlockSpec((K, NG), lambda i, j: (0, 0))],
        out_specs=[pl.BlockSpec((tm, tn), lambda i, j: (i, j)),
                   pl.BlockSpec((tm, NG), lambda i, j: (i, 0))],
        out_shape=[jax.ShapeDtypeStruct((M, N), BF16),
                   jax.ShapeDtypeStruct((M, NG), F32)],
        compiler_params=_cparams("parallel", "arbitrary"),
        name="in_proj",
    )(xn, w_t, wg)


def swiglu_up(xn, w_gate, w_up, *, tm, tn):
    assert w_gate.dtype == w_up.dtype == BF16
    M, K = xn.shape
    N = w_gate.shape[1]
    return pl.pallas_call(
        _swiglu_body,
        grid=(M // tm, N // tn),
        in_specs=[pl.BlockSpec((tm, K), lambda i, j: (i, 0)),
                  pl.BlockSpec((K, tn), lambda i, j: (0, j)),
                  pl.BlockSpec((K, tn), lambda i, j: (0, j))],
        out_specs=pl.BlockSpec((tm, tn), lambda i, j: (i, j)),
        out_shape=jax.ShapeDtypeStruct((M, N), BF16),
        compiler_params=_cparams("parallel", "arbitrary"),
        name="ffn_up",
    )(xn, w_gate, w_up)


def _mm_res_body(a_ref, w_ref, r_ref, o_ref):
    o_ref[...] = r_ref[...] + jnp.dot(a_ref[...], w_ref[...], preferred_element_type=F32)


def matmul_residual(a, w, res, *, tm, tn):
    M, K = a.shape
    N = w.shape[1]
    return pl.pallas_call(
        _mm_res_body,
        grid=(M // tm, N // tn),
        in_specs=[pl.BlockSpec((tm, K), lambda i, j: (i, 0)),
                  pl.BlockSpec((K, tn), lambda i, j: (0, j)),
                  pl.BlockSpec((tm, tn), lambda i, j: (i, j))],
        out_specs=pl.BlockSpec((tm, tn), lambda i, j: (i, j)),
        out_shape=jax.ShapeDtypeStruct((M, N), F32),
        compiler_params=_cparams("parallel", "arbitrary"),
        name="matmul_residual",
    )(a, w, res)


def _mm2_res_body(a_ref, b_ref, wa_ref, wb_ref, r_ref, o_ref):
    acc = jnp.dot(a_ref[...], wa_ref[...], preferred_element_type=F32)
    acc = acc + jnp.dot(b_ref[...], wb_ref[...], preferred_element_type=F32)
    o_ref[...] = r_ref[...] + acc


def concat_matmul_residual(a, b, w, res, *, tm, tn):
    M, Ka = a.shape
    Kb = b.shape[1]
    assert Ka == Kb and w.shape[0] == Ka + Kb
    N = w.shape[1]
    return pl.pallas_call(
        _mm2_res_body,
        grid=(M // tm, N // tn),
        in_specs=[pl.BlockSpec((tm, Ka), lambda i, j: (i, 0)),
                  pl.BlockSpec((tm, Kb), lambda i, j: (i, 0)),
                  pl.BlockSpec((Ka, tn), lambda i, j: (0, j)),
                  pl.BlockSpec((Kb, tn), lambda i, j: (1, j)),
                  pl.BlockSpec((tm, tn), lambda i, j: (i, j))],
        out_specs=pl.BlockSpec((tm, tn), lambda i, j: (i, j)),
        out_shape=jax.ShapeDtypeStruct((M, N), F32),
        compiler_params=_cparams("parallel", "arbitrary"),
        name="out_proj",
    )(a, b, w, w, res)


AUG_WIDTH = LANES
V_EXTRA = 16
SLOPE_PIECES = 4
LOG2E = float(np.log2(np.e))


def _bf16_pieces(x, n):
    out, rem = [], np.asarray(x, np.float64)
    for _ in range(n):
        piece = rem.astype(ml_dtypes.bfloat16).astype(np.float64)
        out.append(piece.astype(np.float32))
        rem = rem - piece
    return out


def _position_lanes(pos_hi, pos_lo, onehot, n_blk, shape):
    lane = lax.broadcasted_iota(jnp.int32, shape, 1)
    p = SLOPE_PIECES
    extra = jnp.where((lane >= n_blk) & (lane < n_blk + p), pos_hi,
                      jnp.where((lane >= n_blk + p) & (lane < n_blk + 2 * p), pos_lo, 0.0))
    if onehot is not None:
        extra = jnp.where(lane < n_blk, onehot, extra)
    return extra.astype(BF16)


def _kv_prep_body(c_ref, s_ref, w_ref, kns_ref, knw_ref,
                  kc_o, vc_o, ks_o, vsT_o, kw_o, vwT_o, slab_ref, *, tt, n_blk):
    dh = HEAD_DIM
    shape = (tt, AUG_WIDTH)
    pos = pl.program_id(1) * tt + lax.broadcasted_iota(jnp.int32, shape, 0)
    lane = lax.broadcasted_iota(jnp.int32, shape, 1)
    blk = jnp.right_shift(pos, SEL_BLK.bit_length() - 1)
    pos_hi = (blk * SEL_BLK).astype(F32)
    pos_lo = (pos & (SEL_BLK - 1)).astype(F32)
    aug_sel = _position_lanes(pos_hi, pos_lo, (blk == lane).astype(F32), n_blk, shape)
    aug_win = _position_lanes(pos_hi, pos_lo, None, n_blk, shape)
    for g in range(NSA_KV_HEADS):
        k_cols = slice(g * dh, (g + 1) * dh)
        v_cols = slice(KV_WIDTH + g * dh, KV_WIDTH + (g + 1) * dh)
        for slab, (cols, out) in enumerate(((k_cols, kc_o), (v_cols, vc_o))):
            slab_ref[slab] = c_ref[:, cols].astype(F32)
            for i in range(CMP_STRIDE):
                rows = pl.ds(i, tt // CMP_STRIDE, stride=CMP_STRIDE)
                out[0, g, :, i * dh:(i + 1) * dh] = slab_ref[slab, rows, :]
        ks_o[0, g, :, 0:dh] = _rms(s_ref[:, k_cols].astype(F32), kns_ref[...]).astype(BF16)
        ks_o[0, g, :, dh:dh + AUG_WIDTH] = aug_sel
        kw_o[0, g, :, 0:dh] = _rms(w_ref[:, k_cols].astype(F32), knw_ref[...]).astype(BF16)
        kw_o[0, g, :, dh:dh + AUG_WIDTH] = aug_win
        for v_ref, out, chunk in ((s_ref, vsT_o, SEL_CHUNK), (w_ref, vwT_o, WIN_CHUNK)):
            vT = v_ref[:, v_cols].astype(F32).T.astype(BF16)
            ones_first = (lax.broadcasted_iota(jnp.int32, (V_EXTRA, chunk), 0) == 0).astype(BF16)
            for c in range(tt // chunk):
                out[0, g, c, 0:dh, :] = vT[:, c * chunk:(c + 1) * chunk]
                out[0, g, c, dh:dh + V_EXTRA, :] = ones_first


def kv_prep(z, kn_slc, kn_win, *, B, T, col0, tt):
    G, dh = NSA_KV_HEADS, HEAD_DIM
    nt = T // tt
    n_blk = T // SEL_BLK
    assert n_blk + 2 * SLOPE_PIECES <= AUG_WIDTH
    cb = col0 // (2 * KV_WIDTH)
    row = lambda b, t: b * nt + t
    kdim = dh + AUG_WIDTH
    return pl.pallas_call(
        functools.partial(_kv_prep_body, tt=tt, n_blk=n_blk),
        grid=(B, nt),
        in_specs=[pl.BlockSpec((tt, 2 * KV_WIDTH), lambda b, t: (row(b, t), cb)),
                  pl.BlockSpec((tt, 2 * KV_WIDTH), lambda b, t: (row(b, t), cb + 1)),
                  pl.BlockSpec((tt, 2 * KV_WIDTH), lambda b, t: (row(b, t), cb + 2)),
                  pl.BlockSpec((1, dh), lambda b, t: (0, 0)),
                  pl.BlockSpec((1, dh), lambda b, t: (0, 0))],
        out_specs=[pl.BlockSpec((1, G, tt // CMP_STRIDE, CMP_STRIDE * dh), lambda b, t: (b, 0, t, 0)),
                   pl.BlockSpec((1, G, tt // CMP_STRIDE, CMP_STRIDE * dh), lambda b, t: (b, 0, t, 0)),
                   pl.BlockSpec((1, G, tt, kdim), lambda b, t: (b, 0, t, 0)),
                   pl.BlockSpec((1, G, tt // SEL_CHUNK, dh + V_EXTRA, SEL_CHUNK), lambda b, t: (b, 0, t, 0, 0)),
                   pl.BlockSpec((1, G, tt, kdim), lambda b, t: (b, 0, t, 0)),
                   pl.BlockSpec((1, G, tt // WIN_CHUNK, dh + V_EXTRA, WIN_CHUNK), lambda b, t: (b, 0, t, 0, 0))],
        out_shape=[jax.ShapeDtypeStruct((B, G, T // CMP_STRIDE, CMP_STRIDE * dh), F32),
                   jax.ShapeDtypeStruct((B, G, T // CMP_STRIDE, CMP_STRIDE * dh), F32),
                   jax.ShapeDtypeStruct((B, G, T, kdim), BF16),
                   jax.ShapeDtypeStruct((B, G, T // SEL_CHUNK, dh + V_EXTRA, SEL_CHUNK), BF16),
                   jax.ShapeDtypeStruct((B, G, T, kdim), BF16),
                   jax.ShapeDtypeStruct((B, G, T // WIN_CHUNK, dh + V_EXTRA, WIN_CHUNK), BF16)],
        scratch_shapes=[pltpu.VMEM((2, tt, dh), F32)],
        compiler_params=_cparams("parallel", "parallel"),
        name="kv_prep",
    )(z, z, z, kn_slc, kn_win)


def _compress_body(xk_ref, xv_ref, pk_ref, pv_ref, w1k_ref, w2k_ref, w1v_ref, w2v_ref, kn_ref,
                   kc_o, vcT_o, *, n_blk):
    dh = HEAD_DIM

    def phi(x_ref, p_ref, w1_ref, w2_ref):
        x = x_ref[0, 0]
        first = jnp.dot((x + p_ref[0:1, :]).astype(BF16), w1_ref[0].astype(BF16), preferred_element_type=F32)
        second = jnp.dot((x + p_ref[1:2, :]).astype(BF16), w1_ref[1].astype(BF16), preferred_element_type=F32)
        n = second.shape[0]
        h = first + pltpu.roll(second, n - 1, 0)
        return jnp.dot(_silu(h).astype(BF16), w2_ref[...].astype(BF16), preferred_element_type=F32)

    k = _rms(phi(xk_ref, pk_ref, w1k_ref, w2k_ref), kn_ref[...])
    nc = k.shape[0]
    shape = (nc, AUG_WIDTH)
    start = (lax.broadcasted_iota(jnp.int32, shape, 0) * CMP_STRIDE).astype(F32)
    centre = jnp.full(shape, (CMP_LEN - 1) / 2.0, F32)
    kc_o[0, 0, :, 0:dh] = k.astype(BF16)
    kc_o[0, 0, :, dh:dh + AUG_WIDTH] = _position_lanes(start, centre, None, n_blk, shape)
    vcT_o[0, 0] = phi(xv_ref, pv_ref, w1v_ref, w2v_ref).T.astype(BF16)


def compress_kv(xk, xv, pos_k, pos_v, w1k, w2k, w1v, w2v, kn_cmp, *, n_blk):
    B, G, nc, half = xk.shape
    dh = half // CMP_STRIDE
    assert CMP_LEN == 2 * CMP_STRIDE
    x_spec = pl.BlockSpec((1, 1, nc, half), lambda b, g: (b, g, 0, 0))
    full = lambda shape: pl.BlockSpec(shape, lambda b, g: (0,) * len(shape))
    return pl.pallas_call(
        functools.partial(_compress_body, n_blk=n_blk),
        grid=(B, G),
        in_specs=[x_spec, x_spec, full((2, half)), full((2, half)),
                  full((2, half, dh)), full((dh, dh)), full((2, half, dh)), full((dh, dh)),
                  full((1, dh))],
        out_specs=[pl.BlockSpec((1, 1, nc, dh + AUG_WIDTH), lambda b, g: (b, g, 0, 0)),
                   pl.BlockSpec((1, 1, dh, nc), lambda b, g: (b, g, 0, 0))],
        out_shape=[jax.ShapeDtypeStruct((B, G, nc, dh + AUG_WIDTH), BF16),
                   jax.ShapeDtypeStruct((B, G, dh, nc), BF16)],
        compiler_params=_cparams("parallel", "parallel"),
        name="compress_kv",
    )(xk, xv, pos_k.reshape(2, half), pos_v.reshape(2, half),
      w1k.reshape(2, half, dh), w2k, w1v.reshape(2, half, dh), w2v, kn_cmp)


def _nsa_body(*refs, n_tiles, n_casts):
    (q_ref, qnext_ref, gl_ref, gb_ref, qn_ref, lanes_ref, kc_ref, vcT_ref,
     ks_ref, vsT_ref, kw_ref, vwT_ref, ovT_ref, kmq_ref) = refs[:14]
    cast_in, (o_ref, *cast_out) = refs[14:14 + n_casts], refs[14 + n_casts:15 + 2 * n_casts]
    score_ref, gate_ref, qc_ref, qs_ref, m_ref, acc_ref, ocmp_ref, sbuf_ref = refs[15 + 2 * n_casts:]
    dh, hg, tq = HEAD_DIM, NSA_GROUP, Q_TILE
    cols = hg * tq
    qi = pl.program_id(2)
    contract_last = (((1,), (1,)), ((), ()))
    n_blk = ovT_ref.shape[0]

    scale = dh ** -0.5 * LOG2E
    hc = kmq_ref.shape[1]
    assert cols % hc == 0 and hc % tq == 0
    groups = [slice(i * hc, (i + 1) * hc) for i in range(cols // hc)]
    cw = ck = tq
    assert WIN_CHUNK == SEL_CHUNK == tq == kmq_ref.shape[0] and WINDOW % cw == 0
    n_back = WINDOW // cw
    nc = kc_ref.shape[2]
    WIN, SEL = 0, 1

    def reset(b):
        m_ref[b] = jnp.full((1, cols), NEG_INF, F32)
        acc_ref[b] = jnp.zeros((dh + V_EXTRA, cols), F32)

    def issue_scores(queries_ref, parts, unit):
        for k, _, _ in parts:
            for cs in groups:
                sbuf_ref[unit, 0:k.shape[0], :] = lax.dot_general(
                    k, queries_ref[cs, :], contract_last, preferred_element_type=F32)
                unit += 1

    def softmax_update(b, parts, unit):
        for (k, v_pieces, mask) in parts:
            for cs in groups:
                s_ref = sbuf_ref.at[unit, 0:k.shape[0], :]
                unit += 1
                if mask is not None:
                    s_ref[...] = jnp.where(mask(), s_ref[...], NEG_INF)
                m_old = m_ref[b, :, cs]
                m_new = jnp.maximum(m_old, jnp.max(s_ref[...], axis=0, keepdims=True))
                alpha = jnp.exp2(m_old - m_new)
                p = jnp.exp2(s_ref[...] - m_new).astype(BF16)
                m_ref[b, :, cs] = m_new
                pv = None
                for i, vT in enumerate(v_pieces):
                    n = vT.shape[1]
                    d = jnp.dot(vT, p[i * n:(i + 1) * n, :], preferred_element_type=F32)
                    pv = d if pv is None else pv + d
                acc_ref[b, :, cs] = alpha * acc_ref[b, :, cs] + pv

    def result(b):
        return acc_ref[b, 0:dh, :] * (1.0 / acc_ref[b, dh:dh + 1, :])

    def prepare(src_ref, tile, slot):
        t0 = tile * tq
        for h in range(hg):
            rows = slice(h * tq, (h + 1) * tq)
            q_h = (_rms(src_ref[:, h * dh:(h + 1) * dh].astype(F32), qn_ref[...]) * scale).astype(BF16)
            qc_ref[slot, rows, 0:dh] = q_h
            qs_ref[slot, rows, 0:dh] = q_h
            qc_ref[slot, rows, dh:dh + AUG_WIDTH] = jnp.broadcast_to(
                lanes_ref[0, hg + h:hg + h + 1, :], (tq, AUG_WIDTH)).astype(BF16)

        s = lax.dot_general(kc_ref[0, 0], qc_ref[slot], contract_last, preferred_element_type=F32)
        c_idx = lax.broadcasted_iota(jnp.int32, (nc, cols), 0)
        t_c = t0 + (lax.broadcasted_iota(jnp.int32, (nc, cols), 1) & (tq - 1))
        mask = (c_idx * CMP_STRIDE + (CMP_LEN - 1)) <= t_c
        s = jnp.where(mask, s, NEG_INF)
        m_c = jnp.max(s, axis=0, keepdims=True)
        p_c = jnp.where(mask, jnp.exp2(s - m_c), 0.0)
        l_c = jnp.sum(p_c, axis=0, keepdims=True)
        p_c = p_c * jnp.where(l_c > 0.0, 1.0 / l_c, 0.0)
        ocmp_ref[slot] = jnp.dot(vcT_ref[0, 0], p_c.astype(BF16), preferred_element_type=F32)

        p_sum = p_c[:, 0:tq]
        for h in range(1, hg):
            p_sum = p_sum + p_c[:, h * tq:(h + 1) * tq]
        hi = p_sum.astype(BF16)
        rem = p_sum - hi.astype(F32)
        mid = rem.astype(BF16)
        lo = (rem - mid.astype(F32)).astype(BF16)
        ovT = ovT_ref[...]
        imp = (jnp.dot(ovT, hi, preferred_element_type=F32)
               + jnp.dot(ovT, mid, preferred_element_type=F32)
               + jnp.dot(ovT, lo, preferred_element_type=F32))
        blk = lax.broadcasted_iota(jnp.int32, (n_blk, tq), 0)
        t_b = t0 + lax.broadcasted_iota(jnp.int32, (n_blk, tq), 1)
        cur = jnp.right_shift(t_b, SEL_BLK.bit_length() - 1)
        valid = blk * SEL_BLK <= t_b
        forced = (blk == 0) | (blk == cur) | (blk == cur - 1)
        score = jnp.where(valid, imp + FORCE_BONUS * forced.astype(F32), NEG_INF)
        score_ref[...] = score
        rank = jnp.zeros((n_blk, tq), F32)
        for j in range(n_blk):
            sj = score_ref[j:j + 1, :]
            beats = (sj > score) | ((sj == score) & (blk > j))
            rank = rank + beats.astype(F32)
        block_bias = jnp.where(rank < float(min(N_SEL, n_blk)), 0.0, NEG_INF)

        assert 2 * SLOPE_PIECES <= 8
        ones_rows = (lax.broadcasted_iota(jnp.int32, (8, tq), 0) < 2 * SLOPE_PIECES).astype(F32)
        extra_t = jnp.concatenate([block_bias, ones_rows,
                                   jnp.zeros((AUG_WIDTH - n_blk - 8, tq), F32)], axis=0).T
        for h in range(hg):
            qs_ref[slot, h * tq:(h + 1) * tq, dh:dh + AUG_WIDTH] = (
                extra_t * lanes_ref[0, h:h + 1, :]).astype(BF16)

    def win_part(c, mask=None):
        return kw_ref[0, 0, c * cw:(c + 1) * cw, :], [vwT_ref[0, 0, c]], mask

    def sel_part(c, n=1, mask=None):
        return ks_ref[0, 0, c * ck:(c + n) * ck, :], [vsT_ref[0, 0, c + i] for i in range(n)], mask

    def tile_step(tile):
        slot = tile % 2
        below_diagonal = lambda: kmq_ref[...] <= 0
        win_parts = [win_part(tile, below_diagonal)]
        win_parts += [win_part(tile - j) for j in range(1, n_back) if tile - j >= 0]
        if tile - n_back >= 0:
            win_parts.append(win_part(tile - n_back, lambda: kmq_ref[...] > 0))
        sel_parts = [sel_part(tile, mask=below_diagonal)]
        sel_parts += [sel_part(c, n=min(2, tile - c)) for c in range(0, tile, 2)]
        sel_unit = len(win_parts) * len(groups)
        issue_scores(qc_ref.at[slot], win_parts, 0)
        issue_scores(qs_ref.at[slot], sel_parts, sel_unit)
        if tile + 1 < n_tiles:
            prepare(qnext_ref, tile + 1, 1 - slot)
        reset(WIN)
        softmax_update(WIN, win_parts, 0)
        reset(SEL)
        softmax_update(SEL, sel_parts, sel_unit)

        gate_ref[...] = jax.nn.sigmoid(gl_ref[...] + gb_ref[...]).T
        o_win, o_slc = result(WIN), result(SEL)
        for h in range(hg):
            cs = slice(h * tq, (h + 1) * tq)
            gate = lambda branch: gate_ref[h * N_BRANCH + branch:h * N_BRANCH + branch + 1, :]
            o_h = gate(0) * ocmp_ref[slot, :, cs] + gate(1) * o_slc[:, cs] + gate(2) * o_win[:, cs]
            o_ref[:, h * dh:(h + 1) * dh] = o_h.T.astype(o_ref.dtype)

    for src, dst in zip(cast_in, cast_out):
        dst[...] = src[...].astype(dst.dtype)

    @pl.when(qi == 0)
    def _():
        prepare(q_ref, 0, 0)

    for tile in range(n_tiles):
        pl.when(qi == tile)(functools.partial(tile_step, tile))


def nsa_attention(z, gl, gate_b, q_norm, kcmp, vcmpT, ks, vsT, kw, vwT, casts, *, B, T):
    G, hg, dh, tq = NSA_KV_HEADS, NSA_GROUP, HEAD_DIM, Q_TILE
    nq = T // tq
    nc = kcmp.shape[2]
    n_blk = T // SEL_BLK
    kdim = dh + AUG_WIDTH
    cols = hg * tq
    assert SEL_CHUNK == WIN_CHUNK == tq
    n_units = (WINDOW // WIN_CHUNK + 1 + 1 + nq // 2) * (cols // NSA_COLS_PER_DOT)
    assert SEL_CHUNK % SEL_BLK == 0 and T % SEL_CHUNK == 0 and gl.shape[1] == G * LANES
    assert SEL_CHUNK % tq == 0 and n_blk % 8 == 0 and n_blk + 8 <= AUG_WIDTH

    slopes = np.exp2(-8.0 * (np.arange(NSA_HEADS, dtype=np.float64) + 1.0) / NSA_HEADS).reshape(G, hg)
    lanes = np.zeros((G, 2 * hg, AUG_WIDTH), np.float32)
    lanes[:, :hg, :n_blk] = 1.0
    for i, piece in enumerate(_bf16_pieces(slopes * LOG2E, SLOPE_PIECES)):
        for lane in (n_blk + i, n_blk + SLOPE_PIECES + i):
            lanes[:, :hg, lane] = piece
            lanes[:, hg:, lane] = piece
    cs = np.arange(nc)[:, None] * CMP_STRIDE
    bs = np.arange(n_blk)[None, :] * SEL_BLK
    overlap = np.clip(np.minimum(cs + CMP_LEN, bs + SEL_BLK) - np.maximum(cs, bs), 0, None) / CMP_LEN
    overlap[(T - CMP_LEN) // CMP_STRIDE + 1:] = 0.0
    ovT = jnp.asarray(overlap.T, dtype=BF16)
    hc = NSA_COLS_PER_DOT
    kmq = jnp.asarray(np.arange(SEL_CHUNK)[:, None] - (np.arange(hc)[None, :] % tq), dtype=jnp.int32)

    row = lambda b, qi: b * nq + qi
    per_bg = lambda shape: pl.BlockSpec((1, 1) + shape, lambda b, g, qi: (b, g) + (0,) * len(shape))
    const = lambda shape: pl.BlockSpec(shape, lambda b, g, qi: (0,) * len(shape))
    n_steps = B * G * nq
    assert all(w.shape[0] % (n_steps * 2 * SUBLANES) == 0 for w in casts)
    slab = lambda w: pl.BlockSpec((w.shape[0] // n_steps, w.shape[1]),
                                  lambda b, g, qi: ((b * G + g) * nq + qi, 0))
    return pl.pallas_call(
        functools.partial(_nsa_body, n_tiles=nq, n_casts=len(casts)),
        grid=(B, G, nq),
        in_specs=[pl.BlockSpec((tq, hg * dh), lambda b, g, qi: (row(b, qi), g)),
                  pl.BlockSpec((tq, hg * dh), lambda b, g, qi: (row(b, jnp.minimum(qi + 1, nq - 1)), g)),
                  pl.BlockSpec((tq, LANES), lambda b, g, qi: (row(b, qi), g)),
                  pl.BlockSpec((1, LANES), lambda b, g, qi: (0, g)),
                  const((1, dh)),
                  pl.BlockSpec((1, 2 * hg, AUG_WIDTH), lambda b, g, qi: (g, 0, 0)),
                  per_bg((nc, kdim)), per_bg((dh, nc)),
                  per_bg((T, kdim)), per_bg((T // SEL_CHUNK, dh + V_EXTRA, SEL_CHUNK)),
                  per_bg((T, kdim)), per_bg((T // WIN_CHUNK, dh + V_EXTRA, WIN_CHUNK)),
                  const((n_blk, nc)), const((SEL_CHUNK, hc))] + [slab(w) for w in casts],
        out_specs=[pl.BlockSpec((tq, hg * dh), lambda b, g, qi: (row(b, qi), g))]
                  + [slab(w) for w in casts],
        out_shape=[jax.ShapeDtypeStruct((B * T, NSA_WIDTH), BF16)]
                  + [jax.ShapeDtypeStruct(w.shape, BF16) for w in casts],
        scratch_shapes=[pltpu.VMEM((n_blk, tq), F32), pltpu.VMEM((LANES, tq), F32),
                        pltpu.VMEM((2, cols, kdim), BF16), pltpu.VMEM((2, cols, kdim), BF16),
                        pltpu.VMEM((2, 1, cols), F32),
                        pltpu.VMEM((2, dh + V_EXTRA, cols), F32), pltpu.VMEM((2, dh, cols), F32),
                        pltpu.VMEM((n_units, 2 * SEL_CHUNK, NSA_COLS_PER_DOT), F32)],
        compiler_params=_cparams("parallel", "parallel", "arbitrary"),
        name="nsa_attention",
    )(z, z, gl, gate_b, q_norm, jnp.asarray(lanes), kcmp, vcmpT, ks, vsT, kw, vwT, ovT, kmq, *casts)


def _conv_body(*refs, tt, n_casts):
    a_ref, b_ref, ah_ref, bh_ref, w_ref, cb_ref, lg_ref, lb_ref = refs[:8]
    cast_in, (o_ref, *cast_out) = refs[8:8 + n_casts], refs[8 + n_casts:9 + 2 * n_casts]
    h_ref, g_ref, y_ref = refs[9 + 2 * n_casts:]
    for src, dst in zip(cast_in, cast_out):
        dst[...] = src[...].astype(dst.dtype)
    C = a_ref.shape[1]
    halo = ah_ref[...].astype(F32) * jax.nn.sigmoid(bh_ref[...].astype(F32))
    h_ref[0:CONV_HALO, :] = jnp.where(pl.program_id(1) == 0, 0.0, halo)
    h_ref[CONV_HALO:CONV_HALO + tt, :] = a_ref[...].astype(F32) * jax.nn.sigmoid(b_ref[...].astype(F32))
    h_ref[CONV_HALO + tt:CONV_HALO + tt + SUBLANES, :] = jnp.zeros((SUBLANES, C), F32)
    lead = CONV_HALO - (CONV_KERNEL - 1)
    rows = tt + SUBLANES
    for s in range(C // LANES):
        cs = slice(s * LANES, (s + 1) * LANES)
        acc = None
        for r in range(SUBLANES):
            group = None
            for o in range(lead, lead + CONV_KERNEL):
                if o % SUBLANES == r:
                    term = w_ref[o - lead:o - lead + 1, cs] * h_ref[o - r:o - r + rows, cs]
                    group = term if group is None else group + term
            if r == 0:
                piece = group[0:tt, :]
            else:
                g_ref[r] = group
                piece = g_ref[r, r:r + tt, :]
            acc = piece if acc is None else acc + piece
        y_ref[:, cs] = acc + cb_ref[:, cs]
    y = y_ref[...]
    mu = jnp.mean(y, axis=-1, keepdims=True)
    var = jnp.mean(jnp.square(y - mu), axis=-1, keepdims=True)
    yn = (y - mu) * lax.rsqrt(var + LN_EPS) * lg_ref[...] + lb_ref[...]
    o_ref[...] = _silu(yn).astype(o_ref.dtype)


def conformer_conv(z, conv_w, conv_b, ln_g, ln_b, casts, *, B, T, col0, C, tt):
    nt = T // tt
    ca = col0 // C
    hb = tt // CONV_HALO
    row = lambda b, t: b * nt + t
    halo_row = lambda b, t: jnp.maximum(row(b, t) * hb - 1, 0)
    const = lambda shape: pl.BlockSpec(shape, lambda b, t: (0,) * len(shape))
    n_steps = B * nt
    assert all(w.shape[0] % (n_steps * 2 * SUBLANES) == 0 for w in casts)
    slab = lambda w: pl.BlockSpec((w.shape[0] // n_steps, w.shape[1]), lambda b, t: (row(b, t), 0))
    return pl.pallas_call(
        functools.partial(_conv_body, tt=tt, n_casts=len(casts)),
        grid=(B, nt),
        in_specs=[pl.BlockSpec((tt, C), lambda b, t: (row(b, t), ca)),
                  pl.BlockSpec((tt, C), lambda b, t: (row(b, t), ca + 1)),
                  pl.BlockSpec((CONV_HALO, C), lambda b, t: (halo_row(b, t), ca)),
                  pl.BlockSpec((CONV_HALO, C), lambda b, t: (halo_row(b, t), ca + 1)),
                  const((CONV_KERNEL, C)), const((1, C)), const((1, C)), const((1, C))]
                 + [slab(w) for w in casts],
        out_specs=[pl.BlockSpec((tt, C), lambda b, t: (row(b, t), 0))] + [slab(w) for w in casts],
        out_shape=[jax.ShapeDtypeStruct((B * T, C), BF16)]
                  + [jax.ShapeDtypeStruct(w.shape, BF16) for w in casts],
        scratch_shapes=[pltpu.VMEM((CONV_HALO + tt + SUBLANES, C), F32),
                        pltpu.VMEM((SUBLANES, tt + SUBLANES, LANES), F32),
                        pltpu.VMEM((tt, C), F32)],
        compiler_params=_cparams("parallel", "parallel"),
        name="conformer_conv",
    )(z, z, z, z, conv_w, conv_b, ln_g, ln_b, *casts)


def _mem_attn_body(x_ref, gq_ref, wq_ref, mem_ref, gkv_ref, wk_ref, wv_ref, qn_ref, kn_ref, wo_ref, gn_ref,
                   o_ref, on_ref, k_scr, v_scr):
    dh = HEAD_DIM
    width = MEM_HEADS * dh
    scale = dh ** -0.5
    heads = [slice(h * dh, (h + 1) * dh) for h in range(MEM_HEADS)]

    @pl.when(pl.program_id(1) == 0)
    def _():
        mem_n = _rms(mem_ref[...], gkv_ref[...]).astype(BF16)
        k = jnp.dot(mem_n, wk_ref[...], preferred_element_type=F32)
        for cs in heads:
            k_scr[:, cs] = _rms(k[:, cs], kn_ref[...]).astype(BF16)
        v_scr[...] = jnp.dot(mem_n, wv_ref[...], preferred_element_type=F32).astype(BF16)

    q_all = jnp.dot(_rms(x_ref[...], gq_ref[...]).astype(BF16), wq_ref[...], preferred_element_type=F32)
    scores = []
    for cs in heads:
        q = (_rms(q_all[:, cs], qn_ref[...]) * scale).astype(BF16)
        scores.append(lax.dot_general(q, k_scr[:, cs], (((1,), (1,)), ((), ())), preferred_element_type=F32))
    outs = []
    for h, s in enumerate(scores):
        v = v_scr[:, heads[h]]
        p = jnp.exp(s - jnp.max(s, axis=-1, keepdims=True))
        inv = 1.0 / jnp.sum(p, axis=-1, keepdims=True)
        outs.append((jnp.dot(p.astype(BF16), v, preferred_element_type=F32) * inv).astype(BF16))
    o = jnp.concatenate(outs, axis=1)
    y = x_ref[...] + jnp.dot(o, wo_ref[...], preferred_element_type=F32)
    o_ref[...] = y
    on_ref[...] = _rms(y, gn_ref[...]).astype(on_ref.dtype)


def mem_attention(x, q_gain, w_mq, mem, kv_gain, w_mk, w_mv, mq_norm, mk_norm, w_mo, next_norm, *, B, T, M, tt):
    D = x.shape[1]
    width = MEM_HEADS * HEAD_DIM
    nt = T // tt
    row = lambda b, t: b * nt + t
    const = lambda shape: pl.BlockSpec(shape, lambda b, t: (0,) * len(shape))
    return pl.pallas_call(
        _mem_attn_body,
        grid=(B, nt),
        in_specs=[pl.BlockSpec((tt, D), lambda b, t: (row(b, t), 0)),
                  const((1, D)), const((D, width)),
                  pl.BlockSpec((M, D), lambda b, t: (b, 0)),
                  const((1, D)), const((D, width)), const((D, width)),
                  const((1, HEAD_DIM)), const((1, HEAD_DIM)), const((width, D)),
                  const((1, D))],
        out_specs=[pl.BlockSpec((tt, D), lambda b, t: (row(b, t), 0)),
                   pl.BlockSpec((tt, D), lambda b, t: (row(b, t), 0))],
        out_shape=[jax.ShapeDtypeStruct((B * T, D), F32),
                   jax.ShapeDtypeStruct((B * T, D), BF16)],
        scratch_shapes=[pltpu.VMEM((M, width), BF16), pltpu.VMEM((M, width), BF16)],
        compiler_params=_cparams("parallel", "arbitrary"),
        name="mem_attention",
    )(x, q_gain, w_mq, mem, kv_gain, w_mk, w_mv, mq_norm, mk_norm, w_mo, next_norm)


def kernel(x, mem, norm_mix, w_in, gate_b, q_norm, k_norm_cmp, k_norm_slc, k_norm_win, cmp_pos_k, cmp_pos_v, cmp_k_w1, cmp_k_w2, cmp_v_w1, cmp_v_w2, conv_w, conv_b, conv_ln_g, conv_ln_b, w_out, norm_mem_q, norm_mem_kv, w_mq, w_mk, w_mv, mq_norm, mk_norm, w_mo, norm_ffn, w_gate, w_up, w_down):
    B, T, D = x.shape
    M = mem.shape[1]
    depth = w_in.shape[0]
    G, hg = NSA_KV_HEADS, NSA_GROUP
    C = D - NSA_WIDTH
    n_gate = N_BRANCH * NSA_HEADS
    kv_all = 6 * KV_WIDTH
    assert w_in.shape[2] == NSA_WIDTH + kv_all + n_gate + 2 * C
    assert C == NSA_WIDTH, "column blocking below assumes equal NSA and conv widths"

    row1 = lambda v: v.reshape(1, -1)
    xf = x.reshape(B * T, D)
    memf = mem.reshape(B * M, D)
    u0 = NSA_WIDTH + kv_all + n_gate
    for l in range(depth):
        w = w_in[l]
        w_t = jnp.swapaxes(w, 0, 1)
        wg = w[:, NSA_WIDTH + kv_all:u0].reshape(D, G, hg * N_BRANCH)
        wg = jnp.pad(wg, ((0, 0), (0, 0), (0, LANES - hg * N_BRANCH))).reshape(D, G * LANES)
        gb = jnp.pad(gate_b[l].reshape(G, hg * N_BRANCH), ((0, 0), (0, LANES - hg * N_BRANCH))).reshape(1, G * LANES)

        xn = rmsnorm_bf16(xf, row1(norm_mix[l]), tm=TILE_RMSNORM)
        z, gl = in_projection(xn, w_t, wg, tm=TILE_IN_PROJ[0], tn=TILE_IN_PROJ[1],
                              segments=((0, NSA_WIDTH), (u0, 2 * C), (NSA_WIDTH, kv_all)))
        kv_col0 = NSA_WIDTH + 2 * C
        kc, vc, ks, vsT, kw, vwT = kv_prep(z, row1(k_norm_slc[l]), row1(k_norm_win[l]),
                                           B=B, T=T, col0=kv_col0, tt=TILE_KV_PREP)
        kcmp, vcmpT = compress_kv(kc, vc, cmp_pos_k[l], cmp_pos_v[l],
                                  cmp_k_w1[l], cmp_k_w2[l], cmp_v_w1[l], cmp_v_w2[l], row1(k_norm_cmp[l]),
                                  n_blk=T // SEL_BLK)
        o_nsa, w_gate_bf16, w_up_bf16 = nsa_attention(
            z, gl, gb, row1(q_norm[l]), kcmp, vcmpT, ks, vsT, kw, vwT, [w_gate[l], w_up[l]], B=B, T=T)
        o_conv, w_out_bf16, w_down_bf16, w_mq_bf16, w_mk_bf16, w_mv_bf16, w_mo_bf16 = conformer_conv(
            z, conv_w[l], row1(conv_b[l]), row1(conv_ln_g[l]), row1(conv_ln_b[l]),
            [w_out[l], w_down[l], w_mq[l], w_mk[l], w_mv[l], w_mo[l]],
            B=B, T=T, col0=NSA_WIDTH, C=C, tt=TILE_CONV)
        xf = concat_matmul_residual(o_nsa, o_conv, w_out_bf16, xf, tm=TILE_OUT_PROJ[0], tn=TILE_OUT_PROJ[1])

        xf, xn = mem_attention(xf, row1(norm_mem_q[l]), w_mq_bf16,
                               memf, row1(norm_mem_kv[l]), w_mk_bf16, w_mv_bf16,
                               row1(mq_norm[l]), row1(mk_norm[l]), w_mo_bf16,
                               row1(norm_ffn[l]), B=B, T=T, M=M, tt=TILE_MEM_ATTN)

        hidden = swiglu_up(xn, w_gate_bf16, w_up_bf16, tm=TILE_FFN_UP[0], tn=TILE_FFN_UP[1])
        xf = matmul_residual(hidden, w_down_bf16, xf, tm=TILE_FFN_DOWN[0], tn=TILE_FFN_DOWN[1])
    return xf.reshape(B, T, D)
```

```python
import functools

import ml_dtypes
import numpy as np
import jax
import jax.numpy as jnp
from jax import lax
from jax.experimental import pallas as pl
from jax.experimental.pallas import tpu as pltpu

F32 = jnp.float32
BF16 = jnp.bfloat16

HEAD_DIM = 128
NSA_HEADS = 8
NSA_KV_HEADS = 2
NSA_GROUP = NSA_HEADS // NSA_KV_HEADS
NSA_WIDTH = NSA_HEADS * HEAD_DIM
KV_WIDTH = NSA_KV_HEADS * HEAD_DIM
N_BRANCH = 3
CONV_KERNEL = 31
CMP_LEN = 32
CMP_STRIDE = 16
SEL_BLK = 64
N_SEL = 16
WINDOW = 512
MEM_HEADS = 4
NEG_INF = -1e30
FORCE_BONUS = 1e3
RMS_EPS = 1e-6
LN_EPS = 1e-5

LANES = 128
SUBLANES = 8
VMEM_LIMIT_BYTES = 56 * 1024 * 1024

Q_TILE = 256
SEL_CHUNK = 256
WIN_CHUNK = 256
NSA_COLS_PER_DOT = 512
CONV_HALO = 32

TILE_RMSNORM = 1024
TILE_IN_PROJ = (2048, 512)
TILE_KV_PREP = 1024
TILE_CONV = 256
TILE_OUT_PROJ = (2048, 512)
TILE_MEM_ATTN = 512
TILE_FFN_UP = (1024, 512)
TILE_FFN_DOWN = (1024, 512)


def _cparams(*sem):
    return pltpu.CompilerParams(dimension_semantics=sem, vmem_limit_bytes=VMEM_LIMIT_BYTES)


def _rms(x, g, eps=RMS_EPS):
    ms = jnp.mean(x * x, axis=-1, keepdims=True)
    return x * lax.rsqrt(ms + eps) * g


def _silu(x):
    return x * jax.nn.sigmoid(x)


def _rmsnorm_body(x_ref, g_ref, o_ref):
    o_ref[...] = _rms(x_ref[...], g_ref[...]).astype(o_ref.dtype)


def rmsnorm_bf16(x, g, *, tm):
    M, K = x.shape
    return pl.pallas_call(
        _rmsnorm_body,
        grid=(M // tm,),
        in_specs=[pl.BlockSpec((tm, K), lambda i: (i, 0)), pl.BlockSpec((1, K), lambda i: (0, 0))],
        out_specs=pl.BlockSpec((tm, K), lambda i: (i, 0)),
        out_shape=jax.ShapeDtypeStruct((M, K), BF16),
        compiler_params=_cparams("parallel"),
        name="rmsnorm",
    )(x, g)


def _in_proj_body(xn_ref, wt_ref, wg_ref, o_ref, og_ref):
    @pl.when(pl.program_id(1) == 0)
    def _():
        og_ref[...] = jnp.dot(xn_ref[...], wg_ref[...].astype(BF16), preferred_element_type=F32)

    o_ref[...] = lax.dot_general(xn_ref[...], wt_ref[...].astype(BF16), (((1,), (1,)), ((), ())),
                                 preferred_element_type=F32).astype(o_ref.dtype)


def _swiglu_body(x_ref, wg_ref, wu_ref, o_ref):
    x = x_ref[...]
    a = jnp.dot(x, wg_ref[...], preferred_element_type=F32)
    b = jnp.dot(x, wu_ref[...], preferred_element_type=F32)
    o_ref[...] = (_silu(a) * b).astype(o_ref.dtype)


def in_projection(xn, w_t, wg, *, segments, tm, tn):
    M, K = xn.shape
    NG = wg.shape[1]
    assert all(start % SUBLANES == 0 and size % tn == 0 for start, size in segments)
    N = sum(size for _, size in segments)

    def row_start(i, j):
        group, first = 0, 0
        for seg_start, size in segments:
            group = jnp.where(j >= first, seg_start // SUBLANES + (j - first) * (tn // SUBLANES), group)
            first += size // tn
        return (group * SUBLANES, 0)

    return pl.pallas_call(
        _in_proj_body,
        grid=(M // tm, N // tn),
        in_specs=[pl.BlockSpec((tm, K), lambda i, j: (i, 0)),
                  pl.BlockSpec((pl.Element(tn), pl.Element(K)), row_start),
                  pl.BlockSpec((K, NG), lambda i, j: (0, 0))],
        out_specs=[pl.BlockSpec((tm, tn), lambda i, j: (i, j)),
                   pl.BlockSpec((tm, NG), lambda i, j: (i, 0))],
        out_shape=[jax.ShapeDtypeStruct((M, N), BF16),
                   jax.ShapeDtypeStruct((M, NG), F32)],
        compiler_params=_cparams("parallel", "arbitrary"),
        name="in_proj",
    )(xn, w_t, wg)


def swiglu_up(xn, w_gate, w_up, *, tm, tn):
    assert w_gate.dtype == w_up.dtype == BF16
    M, K = xn.shape
    N = w_gate.shape[1]
    return pl.pallas_call(
        _swiglu_body,
        grid=(M // tm, N // tn),
        in_specs=[pl.BlockSpec((tm, K), lambda i, j: (i, 0)),
                  pl.BlockSpec((K, tn), lambda i, j: (0, j)),
                  pl.BlockSpec((K, tn), lambda i, j: (0, j))],
        out_specs=pl.BlockSpec((tm, tn), lambda i, j: (i, j)),
        out_shape=jax.ShapeDtypeStruct((M, N), BF16),
        compiler_params=_cparams("parallel", "arbitrary"),
        name="ffn_up",
    )(xn, w_gate, w_up)


def _mm_res_body(a_ref, w_ref, r_ref, o_ref):
    o_ref[...] = r_ref[...] + jnp.dot(a_ref[...], w_ref[...], preferred_element_type=F32)


def matmul_residual(a, w, res, *, tm, tn):
    M, K = a.shape
    N = w.shape[1]
    return pl.pallas_call(
        _mm_res_body,
        grid=(M // tm, N // tn),
        in_specs=[pl.BlockSpec((tm, K), lambda i, j: (i, 0)),
                  pl.BlockSpec((K, tn), lambda i, j: (0, j)),
                  pl.BlockSpec((tm, tn), lambda i, j: (i, j))],
        out_specs=pl.BlockSpec((tm, tn), lambda i, j: (i, j)),
        out_shape=jax.ShapeDtypeStruct((M, N), F32),
        compiler_params=_cparams("parallel", "arbitrary"),
        name="matmul_residual",
    )(a, w, res)


def _mm2_res_body(a_ref, b_ref, wa_ref, wb_ref, r_ref, o_ref):
    acc = jnp.dot(a_ref[...], wa_ref[...], preferred_element_type=F32)
    acc = acc + jnp.dot(b_ref[...], wb_ref[...], preferred_element_type=F32)
    o_ref[...] = r_ref[...] + acc


def concat_matmul_residual(a, b, w, res, *, tm, tn):
    M, Ka = a.shape
    Kb = b.shape[1]
    assert Ka == Kb and w.shape[0] == Ka + Kb
    N = w.shape[1]
    return pl.pallas_call(
        _mm2_res_body,
        grid=(M // tm, N // tn),
        in_specs=[pl.BlockSpec((tm, Ka), lambda i, j: (i, 0)),
                  pl.BlockSpec((tm, Kb), lambda i, j: (i, 0)),
                  pl.BlockSpec((Ka, tn), lambda i, j: (0, j)),
                  pl.BlockSpec((Kb, tn), lambda i, j: (1, j)),
                  pl.BlockSpec((tm, tn), lambda i, j: (i, j))],
        out_specs=pl.BlockSpec((tm, tn), lambda i, j: (i, j)),
        out_shape=jax.ShapeDtypeStruct((M, N), F32),
        compiler_params=_cparams("parallel", "arbitrary"),
        name="out_proj",
    )(a, b, w, w, res)


AUG_WIDTH = LANES
V_EXTRA = 16
SLOPE_PIECES = 4
LOG2E = float(np.log2(np.e))


def _bf16_pieces(x, n):
    out, rem = [], np.asarray(x, np.float64)
    for _ in range(n):
        piece = rem.astype(ml_dtypes.bfloat16).astype(np.float64)
        out.append(piece.astype(np.float32))
        rem = rem - piece
    return out


def _position_lanes(pos_hi, pos_lo, onehot, n_blk, shape):
    lane = lax.broadcasted_iota(jnp.int32, shape, 1)
    p = SLOPE_PIECES
    extra = jnp.where((lane >= n_blk) & (lane < n_blk + p), pos_hi,
                      jnp.where((lane >= n_blk + p) & (lane < n_blk + 2 * p), pos_lo, 0.0))
    if onehot is not None:
        extra = jnp.where(lane < n_blk, onehot, extra)
    return extra.astype(BF16)


def _kv_prep_body(c_ref, s_ref, w_ref, kns_ref, knw_ref,
                  kc_o, vc_o, ks_o, vsT_o, kw_o, vwT_o, slab_ref, *, tt, n_blk):
    dh = HEAD_DIM
    shape = (tt, AUG_WIDTH)
    pos = pl.program_id(1) * tt + lax.broadcasted_iota(jnp.int32, shape, 0)
    lane = lax.broadcasted_iota(jnp.int32, shape, 1)
    blk = jnp.right_shift(pos, SEL_BLK.bit_length() - 1)
    pos_hi = (blk * SEL_BLK).astype(F32)
    pos_lo = (pos & (SEL_BLK - 1)).astype(F32)
    aug_sel = _position_lanes(pos_hi, pos_lo, (blk == lane).astype(F32), n_blk, shape)
    aug_win = _position_lanes(pos_hi, pos_lo, None, n_blk, shape)
    for g in range(NSA_KV_HEADS):
        k_cols = slice(g * dh, (g + 1) * dh)
        v_cols = slice(KV_WIDTH + g * dh, KV_WIDTH + (g + 1) * dh)
        for slab, (cols, out) in enumerate(((k_cols, kc_o), (v_cols, vc_o))):
            slab_ref[slab] = c_ref[:, cols].astype(F32)
            for i in range(CMP_STRIDE):
                rows = pl.ds(i, tt // CMP_STRIDE, stride=CMP_STRIDE)
                out[0, g, :, i * dh:(i + 1) * dh] = slab_ref[slab, rows, :].astype(out.dtype)
        ks_o[0, g, :, 0:dh] = _rms(s_ref[:, k_cols].astype(F32), kns_ref[...]).astype(BF16)
        ks_o[0, g, :, dh:dh + AUG_WIDTH] = aug_sel
        kw_o[0, g, :, 0:dh] = _rms(w_ref[:, k_cols].astype(F32), knw_ref[...]).astype(BF16)
        kw_o[0, g, :, dh:dh + AUG_WIDTH] = aug_win
        for v_ref, out, chunk in ((s_ref, vsT_o, SEL_CHUNK), (w_ref, vwT_o, WIN_CHUNK)):
            vT = v_ref[:, v_cols].astype(F32).T.astype(BF16)
            ones_first = (lax.broadcasted_iota(jnp.int32, (V_EXTRA, chunk), 0) == 0).astype(BF16)
            for c in range(tt // chunk):
                out[0, g, c, 0:dh, :] = vT[:, c * chunk:(c + 1) * chunk]
                out[0, g, c, dh:dh + V_EXTRA, :] = ones_first


def kv_prep(z, kn_slc, kn_win, *, B, T, col0, tt):
    G, dh = NSA_KV_HEADS, HEAD_DIM
    nt = T // tt
    n_blk = T // SEL_BLK
    assert n_blk + 2 * SLOPE_PIECES <= AUG_WIDTH
    cb = col0 // (2 * KV_WIDTH)
    row = lambda b, t: b * nt + t
    kdim = dh + AUG_WIDTH
    return pl.pallas_call(
        functools.partial(_kv_prep_body, tt=tt, n_blk=n_blk),
        grid=(B, nt),
        in_specs=[pl.BlockSpec((tt, 2 * KV_WIDTH), lambda b, t: (row(b, t), cb)),
                  pl.BlockSpec((tt, 2 * KV_WIDTH), lambda b, t: (row(b, t), cb + 1)),
                  pl.BlockSpec((tt, 2 * KV_WIDTH), lambda b, t: (row(b, t), cb + 2)),
                  pl.BlockSpec((1, dh), lambda b, t: (0, 0)),
                  pl.BlockSpec((1, dh), lambda b, t: (0, 0))],
        out_specs=[pl.BlockSpec((1, G, tt // CMP_STRIDE, CMP_STRIDE * dh), lambda b, t: (b, 0, t, 0)),
                   pl.BlockSpec((1, G, tt // CMP_STRIDE, CMP_STRIDE * dh), lambda b, t: (b, 0, t, 0)),
                   pl.BlockSpec((1, G, tt, kdim), lambda b, t: (b, 0, t, 0)),
                   pl.BlockSpec((1, G, tt // SEL_CHUNK, dh + V_EXTRA, SEL_CHUNK), lambda b, t: (b, 0, t, 0, 0)),
                   pl.BlockSpec((1, G, tt, kdim), lambda b, t: (b, 0, t, 0)),
                   pl.BlockSpec((1, G, tt // WIN_CHUNK, dh + V_EXTRA, WIN_CHUNK), lambda b, t: (b, 0, t, 0, 0))],
        out_shape=[jax.ShapeDtypeStruct((B, G, T // CMP_STRIDE, CMP_STRIDE * dh), z.dtype),
                   jax.ShapeDtypeStruct((B, G, T // CMP_STRIDE, CMP_STRIDE * dh), z.dtype),
                   jax.ShapeDtypeStruct((B, G, T, kdim), BF16),
                   jax.ShapeDtypeStruct((B, G, T // SEL_CHUNK, dh + V_EXTRA, SEL_CHUNK), BF16),
                   jax.ShapeDtypeStruct((B, G, T, kdim), BF16),
                   jax.ShapeDtypeStruct((B, G, T // WIN_CHUNK, dh + V_EXTRA, WIN_CHUNK), BF16)],
        scratch_shapes=[pltpu.VMEM((2, tt, dh), F32)],
        compiler_params=_cparams("parallel", "parallel"),
        name="kv_prep",
    )(z, z, z, kn_slc, kn_win)


def _compress_body(xk_ref, xv_ref, pk_ref, pv_ref, w1k_ref, w2k_ref, w1v_ref, w2v_ref, kn_ref,
                   kc_o, vcT_o, *, n_blk):
    dh = HEAD_DIM

    def phi(x_ref, p_ref, w1_ref, w2_ref):
        x = x_ref[0, 0].astype(F32)
        first = jnp.dot((x + p_ref[0:1, :]).astype(BF16), w1_ref[0].astype(BF16), preferred_element_type=F32)
        second = jnp.dot((x + p_ref[1:2, :]).astype(BF16), w1_ref[1].astype(BF16), preferred_element_type=F32)
        n = second.shape[0]
        h = first + pltpu.roll(second, n - 1, 0)
        return jnp.dot(_silu(h).astype(BF16), w2_ref[...].astype(BF16), preferred_element_type=F32)

    k = _rms(phi(xk_ref, pk_ref, w1k_ref, w2k_ref), kn_ref[...])
    nc = k.shape[0]
    shape = (nc, AUG_WIDTH)
    start = (lax.broadcasted_iota(jnp.int32, shape, 0) * CMP_STRIDE).astype(F32)
    centre = jnp.full(shape, (CMP_LEN - 1) / 2.0, F32)
    kc_o[0, 0, :, 0:dh] = k.astype(BF16)
    kc_o[0, 0, :, dh:dh + AUG_WIDTH] = _position_lanes(start, centre, None, n_blk, shape)
    vcT_o[0, 0] = phi(xv_ref, pv_ref, w1v_ref, w2v_ref).T.astype(BF16)


def compress_kv(xk, xv, pos_k, pos_v, w1k, w2k, w1v, w2v, kn_cmp, *, n_blk):
    B, G, nc, half = xk.shape
    dh = half // CMP_STRIDE
    assert CMP_LEN == 2 * CMP_STRIDE
    x_spec = pl.BlockSpec((1, 1, nc, half), lambda b, g: (b, g, 0, 0))
    full = lambda shape: pl.BlockSpec(shape, lambda b, g: (0,) * len(shape))
    return pl.pallas_call(
        functools.partial(_compress_body, n_blk=n_blk),
        grid=(B, G),
        in_specs=[x_spec, x_spec, full((2, half)), full((2, half)),
                  full((2, half, dh)), full((dh, dh)), full((2, half, dh)), full((dh, dh)),
                  full((1, dh))],
        out_specs=[pl.BlockSpec((1, 1, nc, dh + AUG_WIDTH), lambda b, g: (b, g, 0, 0)),
                   pl.BlockSpec((1, 1, dh, nc), lambda b, g: (b, g, 0, 0))],
        out_shape=[jax.ShapeDtypeStruct((B, G, nc, dh + AUG_WIDTH), BF16),
                   jax.ShapeDtypeStruct((B, G, dh, nc), BF16)],
        compiler_params=_cparams("parallel", "parallel"),
        name="compress_kv",
    )(xk, xv, pos_k.reshape(2, half), pos_v.reshape(2, half),
      w1k.reshape(2, half, dh), w2k, w1v.reshape(2, half, dh), w2v, kn_cmp)


def _nsa_body(*refs, n_tiles, n_casts):
    (q_ref, qnext_ref, gl_ref, gb_ref, qn_ref, lanes_ref, kc_ref, vcT_ref,
     ks_ref, vsT_ref, kw_ref, vwT_ref, ovT_ref, kmq_ref) = refs[:14]
    cast_in, (o_ref, *cast_out) = refs[14:14 + n_casts], refs[14 + n_casts:15 + 2 * n_casts]
    score_ref, gate_ref, qc_ref, qs_ref, m_ref, acc_ref, ocmp_ref, sbuf_ref = refs[15 + 2 * n_casts:]
    dh, hg, tq = HEAD_DIM, NSA_GROUP, Q_TILE
    cols = hg * tq
    qi = pl.program_id(2)
    contract_last = (((1,), (1,)), ((), ()))
    n_blk = ovT_ref.shape[0]

    scale = dh ** -0.5 * LOG2E
    hc = kmq_ref.shape[1]
    assert cols % hc == 0 and hc % tq == 0
    groups = [slice(i * hc, (i + 1) * hc) for i in range(cols // hc)]
    cw = ck = tq
    assert WIN_CHUNK == SEL_CHUNK == tq == kmq_ref.shape[0] and WINDOW % cw == 0
    n_back = WINDOW // cw
    nc = kc_ref.shape[2]
    WIN, SEL = 0, 1

    def reset(b):
        m_ref[b] = jnp.full((1, cols), NEG_INF, F32)
        acc_ref[b] = jnp.zeros((dh + V_EXTRA, cols), F32)

    def issue_scores(queries_ref, parts, unit):
        for k, _, _ in parts:
            for cs in groups:
                sbuf_ref[unit, 0:k.shape[0], :] = lax.dot_general(
                    k, queries_ref[cs, :], contract_last, preferred_element_type=F32)
                unit += 1

    def softmax_update(b, parts, unit):
        for (k, v_pieces, mask) in parts:
            for cs in groups:
                s_ref = sbuf_ref.at[unit, 0:k.shape[0], :]
                unit += 1
                if mask is not None:
                    s_ref[...] = jnp.where(mask(), s_ref[...], NEG_INF)
                m_old = m_ref[b, :, cs]
                m_new = jnp.maximum(m_old, jnp.max(s_ref[...], axis=0, keepdims=True))
                alpha = jnp.exp2(m_old - m_new)
                p = jnp.exp2(s_ref[...] - m_new).astype(BF16)
                m_ref[b, :, cs] = m_new
                pv = None
                for i, vT in enumerate(v_pieces):
                    n = vT.shape[1]
                    d = jnp.dot(vT, p[i * n:(i + 1) * n, :], preferred_element_type=F32)
                    pv = d if pv is None else pv + d
                acc_ref[b, :, cs] = alpha * acc_ref[b, :, cs] + pv

    def result(b):
        return acc_ref[b, 0:dh, :] * (1.0 / acc_ref[b, dh:dh + 1, :])

    def prepare(src_ref, tile, slot):
        t0 = tile * tq
        for h in range(hg):
            rows = slice(h * tq, (h + 1) * tq)
            q_h = (_rms(src_ref[:, h * dh:(h + 1) * dh].astype(F32), qn_ref[...]) * scale).astype(BF16)
            qc_ref[slot, rows, 0:dh] = q_h
            qs_ref[slot, rows, 0:dh] = q_h
            qc_ref[slot, rows, dh:dh + AUG_WIDTH] = jnp.broadcast_to(
                lanes_ref[0, hg + h:hg + h + 1, :], (tq, AUG_WIDTH)).astype(BF16)

        s = lax.dot_general(kc_ref[0, 0], qc_ref[slot], contract_last, preferred_element_type=F32)
        c_idx = lax.broadcasted_iota(jnp.int32, (nc, cols), 0)
        t_c = t0 + (lax.broadcasted_iota(jnp.int32, (nc, cols), 1) & (tq - 1))
        mask = (c_idx * CMP_STRIDE + (CMP_LEN - 1)) <= t_c
        s = jnp.where(mask, s, NEG_INF)
        m_c = jnp.max(s, axis=0, keepdims=True)
        p_c = jnp.where(mask, jnp.exp2(s - m_c), 0.0)
        l_c = jnp.sum(p_c, axis=0, keepdims=True)
        p_c = p_c * jnp.where(l_c > 0.0, 1.0 / l_c, 0.0)
        ocmp_ref[slot] = jnp.dot(vcT_ref[0, 0], p_c.astype(BF16), preferred_element_type=F32)

        p_sum = p_c[:, 0:tq]
        for h in range(1, hg):
            p_sum = p_sum + p_c[:, h * tq:(h + 1) * tq]
        hi = p_sum.astype(BF16)
        rem = p_sum - hi.astype(F32)
        mid = rem.astype(BF16)
        lo = (rem - mid.astype(F32)).astype(BF16)
        ovT = ovT_ref[...]
        imp = (jnp.dot(ovT, hi, preferred_element_type=F32)
               + jnp.dot(ovT, mid, preferred_element_type=F32)
               + jnp.dot(ovT, lo, preferred_element_type=F32))
        blk = lax.broadcasted_iota(jnp.int32, (n_blk, tq), 0)
        t_b = t0 + lax.broadcasted_iota(jnp.int32, (n_blk, tq), 1)
        cur = jnp.right_shift(t_b, SEL_BLK.bit_length() - 1)
        valid = blk * SEL_BLK <= t_b
        forced = (blk == 0) | (blk == cur) | (blk == cur - 1)
        score = jnp.where(valid, imp + FORCE_BONUS * forced.astype(F32), NEG_INF)
        score_ref[...] = score
        rank = jnp.zeros((n_blk, tq), F32)
        for j in range(n_blk):
            sj = score_ref[j:j + 1, :]
            beats = (sj > score) | ((sj == score) & (blk > j))
            rank = rank + beats.astype(F32)
        block_bias = jnp.where(rank < float(min(N_SEL, n_blk)), 0.0, NEG_INF)

        assert 2 * SLOPE_PIECES <= 8
        ones_rows = (lax.broadcasted_iota(jnp.int32, (8, tq), 0) < 2 * SLOPE_PIECES).astype(F32)
        extra_t = jnp.concatenate([block_bias, ones_rows,
                                   jnp.zeros((AUG_WIDTH - n_blk - 8, tq), F32)], axis=0).T
        for h in range(hg):
            qs_ref[slot, h * tq:(h + 1) * tq, dh:dh + AUG_WIDTH] = (
                extra_t * lanes_ref[0, h:h + 1, :]).astype(BF16)

    def win_part(c, mask=None):
        return kw_ref[0, 0, c * cw:(c + 1) * cw, :], [vwT_ref[0, 0, c]], mask

    def sel_part(c, n=1, mask=None):
        return ks_ref[0, 0, c * ck:(c + n) * ck, :], [vsT_ref[0, 0, c + i] for i in range(n)], mask

    def tile_step(tile):
        slot = tile % 2
        below_diagonal = lambda: kmq_ref[...] <= 0
        win_parts = [win_part(tile, below_diagonal)]
        win_parts += [win_part(tile - j) for j in range(1, n_back) if tile - j >= 0]
        if tile - n_back >= 0:
            win_parts.append(win_part(tile - n_back, lambda: kmq_ref[...] > 0))
        sel_parts = [sel_part(tile, mask=below_diagonal)]
        sel_parts += [sel_part(c, n=min(2, tile - c)) for c in range(0, tile, 2)]
        sel_unit = len(win_parts) * len(groups)
        issue_scores(qc_ref.at[slot], win_parts, 0)
        issue_scores(qs_ref.at[slot], sel_parts, sel_unit)
        if tile + 1 < n_tiles:
            prepare(qnext_ref, tile + 1, 1 - slot)
        reset(WIN)
        softmax_update(WIN, win_parts, 0)
        reset(SEL)
        softmax_update(SEL, sel_parts, sel_unit)

        gate_ref[...] = jax.nn.sigmoid(gl_ref[...] + gb_ref[...]).T
        o_win, o_slc = result(WIN), result(SEL)
        for h in range(hg):
            cs = slice(h * tq, (h + 1) * tq)
            gate = lambda branch: gate_ref[h * N_BRANCH + branch:h * N_BRANCH + branch + 1, :]
            o_h = gate(0) * ocmp_ref[slot, :, cs] + gate(1) * o_slc[:, cs] + gate(2) * o_win[:, cs]
            o_ref[:, h * dh:(h + 1) * dh] = o_h.T.astype(o_ref.dtype)

    for src, dst in zip(cast_in, cast_out):
        dst[...] = src[...].astype(dst.dtype)

    @pl.when(qi == 0)
    def _():
        prepare(q_ref, 0, 0)

    for tile in range(n_tiles):
        pl.when(qi == tile)(functools.partial(tile_step, tile))


def nsa_attention(z, gl, gate_b, q_norm, kcmp, vcmpT, ks, vsT, kw, vwT, casts, *, B, T):
    G, hg, dh, tq = NSA_KV_HEADS, NSA_GROUP, HEAD_DIM, Q_TILE
    nq = T // tq
    nc = kcmp.shape[2]
    n_blk = T // SEL_BLK
    kdim = dh + AUG_WIDTH
    cols = hg * tq
    assert SEL_CHUNK == WIN_CHUNK == tq
    n_units = (WINDOW // WIN_CHUNK + 1 + 1 + nq // 2) * (cols // NSA_COLS_PER_DOT)
    assert SEL_CHUNK % SEL_BLK == 0 and T % SEL_CHUNK == 0 and gl.shape[1] == G * LANES
    assert SEL_CHUNK % tq == 0 and n_blk % 8 == 0 and n_blk + 8 <= AUG_WIDTH

    slopes = np.exp2(-8.0 * (np.arange(NSA_HEADS, dtype=np.float64) + 1.0) / NSA_HEADS).reshape(G, hg)
    lanes = np.zeros((G, 2 * hg, AUG_WIDTH), np.float32)
    lanes[:, :hg, :n_blk] = 1.0
    for i, piece in enumerate(_bf16_pieces(slopes * LOG2E, SLOPE_PIECES)):
        for lane in (n_blk + i, n_blk + SLOPE_PIECES + i):
            lanes[:, :hg, lane] = piece
            lanes[:, hg:, lane] = piece
    cs = np.arange(nc)[:, None] * CMP_STRIDE
    bs = np.arange(n_blk)[None, :] * SEL_BLK
    overlap = np.clip(np.minimum(cs + CMP_LEN, bs + SEL_BLK) - np.maximum(cs, bs), 0, None) / CMP_LEN
    overlap[(T - CMP_LEN) // CMP_STRIDE + 1:] = 0.0
    ovT = jnp.asarray(overlap.T, dtype=BF16)
    hc = NSA_COLS_PER_DOT
    kmq = jnp.asarray(np.arange(SEL_CHUNK)[:, None] - (np.arange(hc)[None, :] % tq), dtype=jnp.int32)

    row = lambda b, qi: b * nq + qi
    per_bg = lambda shape: pl.BlockSpec((1, 1) + shape, lambda b, g, qi: (b, g) + (0,) * len(shape))
    const = lambda shape: pl.BlockSpec(shape, lambda b, g, qi: (0,) * len(shape))
    n_steps = B * G * nq
    assert all(w.shape[0] % (n_steps * 2 * SUBLANES) == 0 for w in casts)
    slab = lambda w: pl.BlockSpec((w.shape[0] // n_steps, w.shape[1]),
                                  lambda b, g, qi: ((b * G + g) * nq + qi, 0))
    return pl.pallas_call(
        functools.partial(_nsa_body, n_tiles=nq, n_casts=len(casts)),
        grid=(B, G, nq),
        in_specs=[pl.BlockSpec((tq, hg * dh), lambda b, g, qi: (row(b, qi), g)),
                  pl.BlockSpec((tq, hg * dh), lambda b, g, qi: (row(b, jnp.minimum(qi + 1, nq - 1)), g)),
                  pl.BlockSpec((tq, LANES), lambda b, g, qi: (row(b, qi), g)),
                  pl.BlockSpec((1, LANES), lambda b, g, qi: (0, g)),
                  const((1, dh)),
                  pl.BlockSpec((1, 2 * hg, AUG_WIDTH), lambda b, g, qi: (g, 0, 0)),
                  per_bg((nc, kdim)), per_bg((dh, nc)),
                  per_bg((T, kdim)), per_bg((T // SEL_CHUNK, dh + V_EXTRA, SEL_CHUNK)),
                  per_bg((T, kdim)), per_bg((T // WIN_CHUNK, dh + V_EXTRA, WIN_CHUNK)),
                  const((n_blk, nc)), const((SEL_CHUNK, hc))] + [slab(w) for w in casts],
        out_specs=[pl.BlockSpec((tq, hg * dh), lambda b, g, qi: (row(b, qi), g))]
                  + [slab(w) for w in casts],
        out_shape=[jax.ShapeDtypeStruct((B * T, NSA_WIDTH), BF16)]
                  + [jax.ShapeDtypeStruct(w.shape, BF16) for w in casts],
        scratch_shapes=[pltpu.VMEM((n_blk, tq), F32), pltpu.VMEM((LANES, tq), F32),
                        pltpu.VMEM((2, cols, kdim), BF16), pltpu.VMEM((2, cols, kdim), BF16),
                        pltpu.VMEM((2, 1, cols), F32),
                        pltpu.VMEM((2, dh + V_EXTRA, cols), F32), pltpu.VMEM((2, dh, cols), F32),
                        pltpu.VMEM((n_units, 2 * SEL_CHUNK, NSA_COLS_PER_DOT), F32)],
        compiler_params=_cparams("parallel", "parallel", "arbitrary"),
        name="nsa_attention",
    )(z, z, gl, gate_b, q_norm, jnp.asarray(lanes), kcmp, vcmpT, ks, vsT, kw, vwT, ovT, kmq, *casts)


def _conv_body(*refs, tt, n_casts):
    a_ref, b_ref, ah_ref, bh_ref, w_ref, cb_ref, lg_ref, lb_ref = refs[:8]
    cast_in, (o_ref, *cast_out) = refs[8:8 + n_casts], refs[8 + n_casts:9 + 2 * n_casts]
    h_ref, g_ref, y_ref = refs[9 + 2 * n_casts:]
    for src, dst in zip(cast_in, cast_out):
        dst[...] = src[...].astype(dst.dtype)
    C = a_ref.shape[1]
    halo = ah_ref[...].astype(F32) * jax.nn.sigmoid(bh_ref[...].astype(F32))
    h_ref[0:CONV_HALO, :] = jnp.where(pl.program_id(1) == 0, 0.0, halo)
    h_ref[CONV_HALO:CONV_HALO + tt, :] = a_ref[...].astype(F32) * jax.nn.sigmoid(b_ref[...].astype(F32))
    h_ref[CONV_HALO + tt:CONV_HALO + tt + SUBLANES, :] = jnp.zeros((SUBLANES, C), F32)
    lead = CONV_HALO - (CONV_KERNEL - 1)
    rows = tt + SUBLANES
    for s in range(C // LANES):
        cs = slice(s * LANES, (s + 1) * LANES)
        acc = None
        for r in range(SUBLANES):
            group = None
            for o in range(lead, lead + CONV_KERNEL):
                if o % SUBLANES == r:
                    term = w_ref[o - lead:o - lead + 1, cs] * h_ref[o - r:o - r + rows, cs]
                    group = term if group is None else group + term
            if r == 0:
                piece = group[0:tt, :]
            else:
                g_ref[r] = group
                piece = g_ref[r, r:r + tt, :]
            acc = piece if acc is None else acc + piece
        y_ref[:, cs] = acc + cb_ref[:, cs]
    y = y_ref[...]
    mu = jnp.mean(y, axis=-1, keepdims=True)
    var = jnp.mean(jnp.square(y - mu), axis=-1, keepdims=True)
    yn = (y - mu) * lax.rsqrt(var + LN_EPS) * lg_ref[...] + lb_ref[...]
    o_ref[...] = _silu(yn).astype(o_ref.dtype)


def conformer_conv(z, conv_w, conv_b, ln_g, ln_b, casts, *, B, T, col0, C, tt):
    nt = T // tt
    ca = col0 // C
    hb = tt // CONV_HALO
    row = lambda b, t: b * nt + t
    halo_row = lambda b, t: jnp.maximum(row(b, t) * hb - 1, 0)
    const = lambda shape: pl.BlockSpec(shape, lambda b, t: (0,) * len(shape))
    n_steps = B * nt
    assert all(w.shape[0] % (n_steps * 2 * SUBLANES) == 0 for w in casts)
    slab = lambda w: pl.BlockSpec((w.shape[0] // n_steps, w.shape[1]), lambda b, t: (row(b, t), 0))
    return pl.pallas_call(
        functools.partial(_conv_body, tt=tt, n_casts=len(casts)),
        grid=(B, nt),
        in_specs=[pl.BlockSpec((tt, C), lambda b, t: (row(b, t), ca)),
                  pl.BlockSpec((tt, C), lambda b, t: (row(b, t), ca + 1)),
                  pl.BlockSpec((CONV_HALO, C), lambda b, t: (halo_row(b, t), ca)),
                  pl.BlockSpec((CONV_HALO, C), lambda b, t: (halo_row(b, t), ca + 1)),
                  const((CONV_KERNEL, C)), const((1, C)), const((1, C)), const((1, C))]
                 + [slab(w) for w in casts],
        out_specs=[pl.BlockSpec((tt, C), lambda b, t: (row(b, t), 0))] + [slab(w) for w in casts],
        out_shape=[jax.ShapeDtypeStruct((B * T, C), BF16)]
                  + [jax.ShapeDtypeStruct(w.shape, BF16) for w in casts],
        scratch_shapes=[pltpu.VMEM((CONV_HALO + tt + SUBLANES, C), F32),
                        pltpu.VMEM((SUBLANES, tt + SUBLANES, LANES), F32),
                        pltpu.VMEM((tt, C), F32)],
        compiler_params=_cparams("parallel", "parallel"),
        name="conformer_conv",
    )(z, z, z, z, conv_w, conv_b, ln_g, ln_b, *casts)


def _mem_attn_body(x_ref, gq_ref, wq_ref, mem_ref, gkv_ref, wk_ref, wv_ref, qn_ref, kn_ref, wo_ref, gn_ref,
                   o_ref, on_ref, k_scr, v_scr):
    dh = HEAD_DIM
    width = MEM_HEADS * dh
    scale = dh ** -0.5
    heads = [slice(h * dh, (h + 1) * dh) for h in range(MEM_HEADS)]

    @pl.when(pl.program_id(1) == 0)
    def _():
        mem_n = _rms(mem_ref[...], gkv_ref[...]).astype(BF16)
        k = jnp.dot(mem_n, wk_ref[...], preferred_element_type=F32)
        for cs in heads:
            k_scr[:, cs] = _rms(k[:, cs], kn_ref[...]).astype(BF16)
        v_scr[...] = jnp.dot(mem_n, wv_ref[...], preferred_element_type=F32).astype(BF16)

    q_all = jnp.dot(_rms(x_ref[...], gq_ref[...]).astype(BF16), wq_ref[...], preferred_element_type=F32)
    scores = []
    for cs in heads:
        q = (_rms(q_all[:, cs], qn_ref[...]) * scale).astype(BF16)
        scores.append(lax.dot_general(q, k_scr[:, cs], (((1,), (1,)), ((), ())), preferred_element_type=F32))
    outs = []
    for h, s in enumerate(scores):
        v = v_scr[:, heads[h]]
        p = jnp.exp(s - jnp.max(s, axis=-1, keepdims=True))
        inv = 1.0 / jnp.sum(p, axis=-1, keepdims=True)
        outs.append((jnp.dot(p.astype(BF16), v, preferred_element_type=F32) * inv).astype(BF16))
    o = jnp.concatenate(outs, axis=1)
    y = x_ref[...] + jnp.dot(o, wo_ref[...], preferred_element_type=F32)
    o_ref[...] = y
    on_ref[...] = _rms(y, gn_ref[...]).astype(on_ref.dtype)


def mem_attention(x, q_gain, w_mq, mem, kv_gain, w_mk, w_mv, mq_norm, mk_norm, w_mo, next_norm, *, B, T, M, tt):
    D = x.shape[1]
    width = MEM_HEADS * HEAD_DIM
    nt = T // tt
    row = lambda b, t: b * nt + t
    const = lambda shape: pl.BlockSpec(shape, lambda b, t: (0,) * len(shape))
    return pl.pallas_call(
        _mem_attn_body,
        grid=(B, nt),
        in_specs=[pl.BlockSpec((tt, D), lambda b, t: (row(b, t), 0)),
                  const((1, D)), const((D, width)),
                  pl.BlockSpec((M, D), lambda b, t: (b, 0)),
                  const((1, D)), const((D, width)), const((D, width)),
                  const((1, HEAD_DIM)), const((1, HEAD_DIM)), const((width, D)),
                  const((1, D))],
        out_specs=[pl.BlockSpec((tt, D), lambda b, t: (row(b, t), 0)),
                   pl.BlockSpec((tt, D), lambda b, t: (row(b, t), 0))],
        out_shape=[jax.ShapeDtypeStruct((B * T, D), F32),
                   jax.ShapeDtypeStruct((B * T, D), BF16)],
        scratch_shapes=[pltpu.VMEM((M, width), BF16), pltpu.VMEM((M, width), BF16)],
        compiler_params=_cparams("parallel", "arbitrary"),
        name="mem_attention",
    )(x, q_gain, w_mq, mem, kv_gain, w_mk, w_mv, mq_norm, mk_norm, w_mo, next_norm)


def kernel(x, mem, norm_mix, w_in, gate_b, q_norm, k_norm_cmp, k_norm_slc, k_norm_win, cmp_pos_k, cmp_pos_v, cmp_k_w1, cmp_k_w2, cmp_v_w1, cmp_v_w2, conv_w, conv_b, conv_ln_g, conv_ln_b, w_out, norm_mem_q, norm_mem_kv, w_mq, w_mk, w_mv, mq_norm, mk_norm, w_mo, norm_ffn, w_gate, w_up, w_down):
    B, T, D = x.shape
    M = mem.shape[1]
    depth = w_in.shape[0]
    G, hg = NSA_KV_HEADS, NSA_GROUP
    C = D - NSA_WIDTH
    n_gate = N_BRANCH * NSA_HEADS
    kv_all = 6 * KV_WIDTH
    assert w_in.shape[2] == NSA_WIDTH + kv_all + n_gate + 2 * C
    assert C == NSA_WIDTH, "column blocking below assumes equal NSA and conv widths"

    row1 = lambda v: v.reshape(1, -1)
    xf = x.reshape(B * T, D)
    memf = mem.reshape(B * M, D)
    u0 = NSA_WIDTH + kv_all + n_gate
    for l in range(depth):
        w = w_in[l]
        w_t = jnp.swapaxes(w, 0, 1)
        wg = w[:, NSA_WIDTH + kv_all:u0].reshape(D, G, hg * N_BRANCH)
        wg = jnp.pad(wg, ((0, 0), (0, 0), (0, LANES - hg * N_BRANCH))).reshape(D, G * LANES)
        gb = jnp.pad(gate_b[l].reshape(G, hg * N_BRANCH), ((0, 0), (0, LANES - hg * N_BRANCH))).reshape(1, G * LANES)

        xn = rmsnorm_bf16(xf, row1(norm_mix[l]), tm=TILE_RMSNORM)
        z, gl = in_projection(xn, w_t, wg, tm=TILE_IN_PROJ[0], tn=TILE_IN_PROJ[1],
                              segments=((0, NSA_WIDTH), (u0, 2 * C), (NSA_WIDTH, kv_all)))
        kv_col0 = NSA_WIDTH + 2 * C
        kc, vc, ks, vsT, kw, vwT = kv_prep(z, row1(k_norm_slc[l]), row1(k_norm_win[l]),
                                           B=B, T=T, col0=kv_col0, tt=TILE_KV_PREP)
        kcmp, vcmpT = compress_kv(kc, vc, cmp_pos_k[l], cmp_pos_v[l],
                                  cmp_k_w1[l], cmp_k_w2[l], cmp_v_w1[l], cmp_v_w2[l], row1(k_norm_cmp[l]),
                                  n_blk=T // SEL_BLK)
        o_nsa, w_gate_bf16, w_up_bf16 = nsa_attention(
            z, gl, gb, row1(q_norm[l]), kcmp, vcmpT, ks, vsT, kw, vwT, [w_gate[l], w_up[l]], B=B, T=T)
        o_conv, w_out_bf16, w_down_bf16, w_mq_bf16, w_mk_bf16, w_mv_bf16, w_mo_bf16 = conformer_conv(
            z, conv_w[l], row1(conv_b[l]), row1(conv_ln_g[l]), row1(conv_ln_b[l]),
            [w_out[l], w_down[l], w_mq[l], w_mk[l], w_mv[l], w_mo[l]],
            B=B, T=T, col0=NSA_WIDTH, C=C, tt=TILE_CONV)
        xf = concat_matmul_residual(o_nsa, o_conv, w_out_bf16, xf, tm=TILE_OUT_PROJ[0], tn=TILE_OUT_PROJ[1])

        xf, xn = mem_attention(xf, row1(norm_mem_q[l]), w_mq_bf16,
                               memf, row1(norm_mem_kv[l]), w_mk_bf16, w_mv_bf16,
                               row1(mq_norm[l]), row1(mk_norm[l]), w_mo_bf16,
                               row1(norm_ffn[l]), B=B, T=T, M=M, tt=TILE_MEM_ATTN)

        hidden = swiglu_up(xn, w_gate_bf16, w_up_bf16, tm=TILE_FFN_UP[0], tn=TILE_FFN_UP[1])
        xf = matmul_residual(hidden, w_down_bf16, xf, tm=TILE_FFN_DOWN[0], tn=TILE_FFN_DOWN[1])
    return xf.reshape(B, T, D)
```

```python
import functools

import ml_dtypes
import numpy as np
import jax
import jax.numpy as jnp
from jax import lax
from jax.experimental import pallas as pl
from jax.experimental.pallas import tpu as pltpu

F32 = jnp.float32
BF16 = jnp.bfloat16

HEAD_DIM = 128
NSA_HEADS = 8
NSA_KV_HEADS = 2
NSA_GROUP = NSA_HEADS // NSA_KV_HEADS
NSA_WIDTH = NSA_HEADS * HEAD_DIM
KV_WIDTH = NSA_KV_HEADS * HEAD_DIM
N_BRANCH = 3
CONV_KERNEL = 31
CMP_LEN = 32
CMP_STRIDE = 16
SEL_BLK = 64
N_SEL = 16
WINDOW = 512
MEM_HEADS = 4
NEG_INF = -1e30
FORCE_BONUS = 1e3
RMS_EPS = 1e-6
LN_EPS = 1e-5

LANES = 128
SUBLANES = 8
VMEM_LIMIT_BYTES = 56 * 1024 * 1024

Q_TILE = 256
SEL_CHUNK = 256
WIN_CHUNK = 256
NSA_COLS_PER_DOT = 512
CONV_HALO = 32
CONV_LN_ROWS = 32

TILE_RMSNORM = 1024
TILE_IN_PROJ = (2048, 512)
TILE_KV_PREP = 1024
TILE_CONV = 256
TILE_OUT_PROJ = (2048, 512)
TILE_MEM_ATTN = 512
TILE_FFN_UP = (1024, 512)
TILE_FFN_DOWN = (1024, 512)


def _cparams(*sem):
    return pltpu.CompilerParams(dimension_semantics=sem, vmem_limit_bytes=VMEM_LIMIT_BYTES)


def _rms(x, g, eps=RMS_EPS):
    ms = jnp.mean(x * x, axis=-1, keepdims=True)
    return x * lax.rsqrt(ms + eps) * g


def _silu(x):
    return x * jax.nn.sigmoid(x)


def _rmsnorm_body(x_ref, g_ref, o_ref):
    o_ref[...] = _rms(x_ref[...], g_ref[...]).astype(o_ref.dtype)


def rmsnorm_bf16(x, g, *, tm):
    M, K = x.shape
    return pl.pallas_call(
        _rmsnorm_body,
        grid=(M // tm,),
        in_specs=[pl.BlockSpec((tm, K), lambda i: (i, 0)), pl.BlockSpec((1, K), lambda i: (0, 0))],
        out_specs=pl.BlockSpec((tm, K), lambda i: (i, 0)),
        out_shape=jax.ShapeDtypeStruct((M, K), BF16),
        compiler_params=_cparams("parallel"),
        name="rmsnorm",
    )(x, g)


def _in_proj_body(xn_ref, wt_ref, wg_ref, o_ref, og_ref):
    @pl.when(pl.program_id(1) == 0)
    def _():
        og_ref[...] = jnp.dot(xn_ref[...], wg_ref[...].astype(BF16), preferred_element_type=F32)

    o_ref[...] = lax.dot_general(xn_ref[...], wt_ref[...].astype(BF16), (((1,), (1,)), ((), ())),
                                 preferred_element_type=F32).astype(o_ref.dtype)


def _swiglu_body(x_ref, wg_ref, wu_ref, o_ref):
    x = x_ref[...]
    a = jnp.dot(x, wg_ref[...], preferred_element_type=F32)
    b = jnp.dot(x, wu_ref[...], preferred_element_type=F32)
    o_ref[...] = (_silu(a) * b).astype(o_ref.dtype)


def in_projection(xn, w_t, wg, *, segments, tm, tn):
    M, K = xn.shape
    NG = wg.shape[1]
    assert all(start % SUBLANES == 0 and size % tn == 0 for start, size in segments)
    N = sum(size for _, size in segments)

    def row_start(i, j):
        group, first = 0, 0
        for seg_start, size in segments:
            group = jnp.where(j >= first, seg_start // SUBLANES + (j - first) * (tn // SUBLANES), group)
            first += size // tn
        return (group * SUBLANES, 0)

    return pl.pallas_call(
        _in_proj_body,
        grid=(M // tm, N // tn),
        in_specs=[pl.BlockSpec((tm, K), lambda i, j: (i, 0)),
                  pl.BlockSpec((pl.Element(tn), pl.Element(K)), row_start),
                  pl.BlockSpec((K, NG), lambda i, j: (0, 0))],
        out_specs=[pl.BlockSpec((tm, tn), lambda i, j: (i, j)),
                   pl.BlockSpec((tm, NG), lambda i, j: (i, 0))],
        out_shape=[jax.ShapeDtypeStruct((M, N), BF16),
                   jax.ShapeDtypeStruct((M, NG), F32)],
        compiler_params=_cparams("parallel", "arbitrary"),
        name="in_proj",
    )(xn, w_t, wg)


def swiglu_up(xn, w_gate, w_up, *, tm, tn):
    assert w_gate.dtype == w_up.dtype == BF16
    M, K = xn.shape
    N = w_gate.shape[1]
    return pl.pallas_call(
        _swiglu_body,
        grid=(M // tm, N // tn),
        in_specs=[pl.BlockSpec((tm, K), lambda i, j: (i, 0)),
                  pl.BlockSpec((K, tn), lambda i, j: (0, j)),
                  pl.BlockSpec((K, tn), lambda i, j: (0, j))],
        out_specs=pl.BlockSpec((tm, tn), lambda i, j: (i, j)),
        out_shape=jax.ShapeDtypeStruct((M, N), BF16),
        compiler_params=_cparams("parallel", "arbitrary"),
        name="ffn_up",
    )(xn, w_gate, w_up)


def _mm_res_body(a_ref, w_ref, r_ref, o_ref):
    o_ref[...] = r_ref[...] + jnp.dot(a_ref[...], w_ref[...], preferred_element_type=F32)


def matmul_residual(a, w, res, *, tm, tn):
    M, K = a.shape
    N = w.shape[1]
    return pl.pallas_call(
        _mm_res_body,
        grid=(M // tm, N // tn),
        in_specs=[pl.BlockSpec((tm, K), lambda i, j: (i, 0)),
                  pl.BlockSpec((K, tn), lambda i, j: (0, j)),
                  pl.BlockSpec((tm, tn), lambda i, j: (i, j))],
        out_specs=pl.BlockSpec((tm, tn), lambda i, j: (i, j)),
        out_shape=jax.ShapeDtypeStruct((M, N), F32),
        compiler_params=_cparams("parallel", "arbitrary"),
        name="matmul_residual",
    )(a, w, res)


def _mm2_res_body(a_ref, b_ref, wa_ref, wb_ref, r_ref, o_ref):
    acc = jnp.dot(a_ref[...], wa_ref[...], preferred_element_type=F32)
    acc = acc + jnp.dot(b_ref[...], wb_ref[...], preferred_element_type=F32)
    o_ref[...] = r_ref[...] + acc


def concat_matmul_residual(a, b, w, res, *, tm, tn):
    M, Ka = a.shape
    Kb = b.shape[1]
    assert Ka == Kb and w.shape[0] == Ka + Kb
    N = w.shape[1]
    return pl.pallas_call(
        _mm2_res_body,
        grid=(M // tm, N // tn),
        in_specs=[pl.BlockSpec((tm, Ka), lambda i, j: (i, 0)),
                  pl.BlockSpec((tm, Kb), lambda i, j: (i, 0)),
                  pl.BlockSpec((Ka, tn), lambda i, j: (0, j)),
                  pl.BlockSpec((Kb, tn), lambda i, j: (1, j)),
                  pl.BlockSpec((tm, tn), lambda i, j: (i, j))],
        out_specs=pl.BlockSpec((tm, tn), lambda i, j: (i, j)),
        out_shape=jax.ShapeDtypeStruct((M, N), F32),
        compiler_params=_cparams("parallel", "arbitrary"),
        name="out_proj",
    )(a, b, w, w, res)


AUG_WIDTH = LANES
V_EXTRA = 16
SLOPE_PIECES = 4
LOG2E = float(np.log2(np.e))


def _bf16_pieces(x, n):
    out, rem = [], np.asarray(x, np.float64)
    for _ in range(n):
        piece = rem.astype(ml_dtypes.bfloat16).astype(np.float64)
        out.append(piece.astype(np.float32))
        rem = rem - piece
    return out


def _position_lanes(pos_hi, pos_lo, onehot, n_blk, shape):
    lane = lax.broadcasted_iota(jnp.int32, shape, 1)
    p = SLOPE_PIECES
    extra = jnp.where((lane >= n_blk) & (lane < n_blk + p), pos_hi,
                      jnp.where((lane >= n_blk + p) & (lane < n_blk + 2 * p), pos_lo, 0.0))
    if onehot is not None:
        extra = jnp.where(lane < n_blk, onehot, extra)
    return extra.astype(BF16)


def _kv_prep_body(c_ref, s_ref, w_ref, kns_ref, knw_ref,
                  kc_o, vc_o, ks_o, vsT_o, kw_o, vwT_o, slab_ref, *, tt, n_blk):
    dh = HEAD_DIM
    shape = (tt, AUG_WIDTH)
    pos = pl.program_id(1) * tt + lax.broadcasted_iota(jnp.int32, shape, 0)
    lane = lax.broadcasted_iota(jnp.int32, shape, 1)
    blk = jnp.right_shift(pos, SEL_BLK.bit_length() - 1)
    pos_hi = (blk * SEL_BLK).astype(F32)
    pos_lo = (pos & (SEL_BLK - 1)).astype(F32)
    aug_sel = _position_lanes(pos_hi, pos_lo, (blk == lane).astype(F32), n_blk, shape)
    aug_win = _position_lanes(pos_hi, pos_lo, None, n_blk, shape)
    for g in range(NSA_KV_HEADS):
        k_cols = slice(g * dh, (g + 1) * dh)
        v_cols = slice(KV_WIDTH + g * dh, KV_WIDTH + (g + 1) * dh)
        for slab, (cols, out) in enumerate(((k_cols, kc_o), (v_cols, vc_o))):
            slab_ref[slab] = c_ref[:, cols].astype(F32)
            for i in range(CMP_STRIDE):
                rows = pl.ds(i, tt // CMP_STRIDE, stride=CMP_STRIDE)
                out[0, g, :, i * dh:(i + 1) * dh] = slab_ref[slab, rows, :].astype(out.dtype)
        ks_o[0, g, :, 0:dh] = _rms(s_ref[:, k_cols].astype(F32), kns_ref[...]).astype(BF16)
        ks_o[0, g, :, dh:dh + AUG_WIDTH] = aug_sel
        kw_o[0, g, :, 0:dh] = _rms(w_ref[:, k_cols].astype(F32), knw_ref[...]).astype(BF16)
        kw_o[0, g, :, dh:dh + AUG_WIDTH] = aug_win
        for v_ref, out, chunk in ((s_ref, vsT_o, SEL_CHUNK), (w_ref, vwT_o, WIN_CHUNK)):
            vT = v_ref[:, v_cols].astype(F32).T.astype(BF16)
            ones_first = (lax.broadcasted_iota(jnp.int32, (V_EXTRA, chunk), 0) == 0).astype(BF16)
            for c in range(tt // chunk):
                out[0, g, c, 0:dh, :] = vT[:, c * chunk:(c + 1) * chunk]
                out[0, g, c, dh:dh + V_EXTRA, :] = ones_first


def kv_prep(z, kn_slc, kn_win, *, B, T, col0, tt):
    G, dh = NSA_KV_HEADS, HEAD_DIM
    nt = T // tt
    n_blk = T // SEL_BLK
    assert n_blk + 2 * SLOPE_PIECES <= AUG_WIDTH
    cb = col0 // (2 * KV_WIDTH)
    row = lambda b, t: b * nt + t
    kdim = dh + AUG_WIDTH
    return pl.pallas_call(
        functools.partial(_kv_prep_body, tt=tt, n_blk=n_blk),
        grid=(B, nt),
        in_specs=[pl.BlockSpec((tt, 2 * KV_WIDTH), lambda b, t: (row(b, t), cb)),
                  pl.BlockSpec((tt, 2 * KV_WIDTH), lambda b, t: (row(b, t), cb + 1)),
                  pl.BlockSpec((tt, 2 * KV_WIDTH), lambda b, t: (row(b, t), cb + 2)),
                  pl.BlockSpec((1, dh), lambda b, t: (0, 0)),
                  pl.BlockSpec((1, dh), lambda b, t: (0, 0))],
        out_specs=[pl.BlockSpec((1, G, tt // CMP_STRIDE, CMP_STRIDE * dh), lambda b, t: (b, 0, t, 0)),
                   pl.BlockSpec((1, G, tt // CMP_STRIDE, CMP_STRIDE * dh), lambda b, t: (b, 0, t, 0)),
                   pl.BlockSpec((1, G, tt, kdim), lambda b, t: (b, 0, t, 0)),
                   pl.BlockSpec((1, G, tt // SEL_CHUNK, dh + V_EXTRA, SEL_CHUNK), lambda b, t: (b, 0, t, 0, 0)),
                   pl.BlockSpec((1, G, tt, kdim), lambda b, t: (b, 0, t, 0)),
                   pl.BlockSpec((1, G, tt // WIN_CHUNK, dh + V_EXTRA, WIN_CHUNK), lambda b, t: (b, 0, t, 0, 0))],
        out_shape=[jax.ShapeDtypeStruct((B, G, T // CMP_STRIDE, CMP_STRIDE * dh), z.dtype),
                   jax.ShapeDtypeStruct((B, G, T // CMP_STRIDE, CMP_STRIDE * dh), z.dtype),
                   jax.ShapeDtypeStruct((B, G, T, kdim), BF16),
                   jax.ShapeDtypeStruct((B, G, T // SEL_CHUNK, dh + V_EXTRA, SEL_CHUNK), BF16),
                   jax.ShapeDtypeStruct((B, G, T, kdim), BF16),
                   jax.ShapeDtypeStruct((B, G, T // WIN_CHUNK, dh + V_EXTRA, WIN_CHUNK), BF16)],
        scratch_shapes=[pltpu.VMEM((2, tt, dh), F32)],
        compiler_params=_cparams("parallel", "parallel"),
        name="kv_prep",
    )(z, z, z, kn_slc, kn_win)


def _compress_body(xk_ref, xv_ref, pk_ref, pv_ref, w1k_ref, w2k_ref, w1v_ref, w2v_ref, kn_ref,
                   kc_o, vcT_o, *, n_blk):
    dh = HEAD_DIM

    def phi(x_ref, p_ref, w1_ref, w2_ref):
        x = x_ref[0, 0].astype(F32)
        first = jnp.dot((x + p_ref[0:1, :]).astype(BF16), w1_ref[0].astype(BF16), preferred_element_type=F32)
        second = jnp.dot((x + p_ref[1:2, :]).astype(BF16), w1_ref[1].astype(BF16), preferred_element_type=F32)
        n = second.shape[0]
        h = first + pltpu.roll(second, n - 1, 0)
        return jnp.dot(_silu(h).astype(BF16), w2_ref[...].astype(BF16), preferred_element_type=F32)

    k = _rms(phi(xk_ref, pk_ref, w1k_ref, w2k_ref), kn_ref[...])
    nc = k.shape[0]
    shape = (nc, AUG_WIDTH)
    start = (lax.broadcasted_iota(jnp.int32, shape, 0) * CMP_STRIDE).astype(F32)
    centre = jnp.full(shape, (CMP_LEN - 1) / 2.0, F32)
    kc_o[0, 0, :, 0:dh] = k.astype(BF16)
    kc_o[0, 0, :, dh:dh + AUG_WIDTH] = _position_lanes(start, centre, None, n_blk, shape)
    vcT_o[0, 0] = phi(xv_ref, pv_ref, w1v_ref, w2v_ref).T.astype(BF16)


def compress_kv(xk, xv, pos_k, pos_v, w1k, w2k, w1v, w2v, kn_cmp, *, n_blk):
    B, G, nc, half = xk.shape
    dh = half // CMP_STRIDE
    assert CMP_LEN == 2 * CMP_STRIDE
    x_spec = pl.BlockSpec((1, 1, nc, half), lambda b, g: (b, g, 0, 0))
    full = lambda shape: pl.BlockSpec(shape, lambda b, g: (0,) * len(shape))
    return pl.pallas_call(
        functools.partial(_compress_body, n_blk=n_blk),
        grid=(B, G),
        in_specs=[x_spec, x_spec, full((2, half)), full((2, half)),
                  full((2, half, dh)), full((dh, dh)), full((2, half, dh)), full((dh, dh)),
                  full((1, dh))],
        out_specs=[pl.BlockSpec((1, 1, nc, dh + AUG_WIDTH), lambda b, g: (b, g, 0, 0)),
                   pl.BlockSpec((1, 1, dh, nc), lambda b, g: (b, g, 0, 0))],
        out_shape=[jax.ShapeDtypeStruct((B, G, nc, dh + AUG_WIDTH), BF16),
                   jax.ShapeDtypeStruct((B, G, dh, nc), BF16)],
        compiler_params=_cparams("parallel", "parallel"),
        name="compress_kv",
    )(xk, xv, pos_k.reshape(2, half), pos_v.reshape(2, half),
      w1k.reshape(2, half, dh), w2k, w1v.reshape(2, half, dh), w2v, kn_cmp)


def _nsa_body(*refs, n_tiles, n_casts):
    (q_ref, qnext_ref, gl_ref, gb_ref, qn_ref, lanes_ref, kc_ref, vcT_ref,
     ks_ref, vsT_ref, kw_ref, vwT_ref, ovT_ref, kmq_ref) = refs[:14]
    cast_in, (o_ref, *cast_out) = refs[14:14 + n_casts], refs[14 + n_casts:15 + 2 * n_casts]
    score_ref, gate_ref, qc_ref, qs_ref, m_ref, acc_ref, ocmp_ref, sbuf_ref = refs[15 + 2 * n_casts:]
    dh, hg, tq = HEAD_DIM, NSA_GROUP, Q_TILE
    cols = hg * tq
    qi = pl.program_id(2)
    contract_last = (((1,), (1,)), ((), ()))
    n_blk = ovT_ref.shape[0]

    scale = dh ** -0.5 * LOG2E
    hc = kmq_ref.shape[1]
    assert cols % hc == 0 and hc % tq == 0
    groups = [slice(i * hc, (i + 1) * hc) for i in range(cols // hc)]
    cw = ck = tq
    assert WIN_CHUNK == SEL_CHUNK == tq == kmq_ref.shape[0] and WINDOW % cw == 0
    n_back = WINDOW // cw
    nc = kc_ref.shape[2]
    WIN, SEL = 0, 1

    def reset(b):
        m_ref[b] = jnp.full((1, cols), NEG_INF, F32)
        acc_ref[b] = jnp.zeros((dh + V_EXTRA, cols), F32)

    def issue_scores(queries_ref, parts, unit):
        for k, _, _ in parts:
            for cs in groups:
                sbuf_ref[unit, 0:k.shape[0], :] = lax.dot_general(
                    k, queries_ref[cs, :], contract_last, preferred_element_type=F32)
                unit += 1

    def softmax_update(b, parts, unit):
        for (k, v_pieces, mask) in parts:
            for cs in groups:
                s_ref = sbuf_ref.at[unit, 0:k.shape[0], :]
                unit += 1
                if mask is not None:
                    s_ref[...] = jnp.where(mask(), s_ref[...], NEG_INF)
                m_old = m_ref[b, :, cs]
                m_new = jnp.maximum(m_old, jnp.max(s_ref[...], axis=0, keepdims=True))
                alpha = jnp.exp2(m_old - m_new)
                p = jnp.exp2(s_ref[...] - m_new).astype(BF16)
                m_ref[b, :, cs] = m_new
                pv = None
                for i, vT in enumerate(v_pieces):
                    n = vT.shape[1]
                    d = jnp.dot(vT, p[i * n:(i + 1) * n, :], preferred_element_type=F32)
                    pv = d if pv is None else pv + d
                acc_ref[b, :, cs] = alpha * acc_ref[b, :, cs] + pv

    def result(b):
        return acc_ref[b, 0:dh, :] * (1.0 / acc_ref[b, dh:dh + 1, :])

    def prepare(src_ref, tile, slot):
        t0 = tile * tq
        for h in range(hg):
            rows = slice(h * tq, (h + 1) * tq)
            q_h = (_rms(src_ref[:, h * dh:(h + 1) * dh].astype(F32), qn_ref[...]) * scale).astype(BF16)
            qc_ref[slot, rows, 0:dh] = q_h
            qs_ref[slot, rows, 0:dh] = q_h
            qc_ref[slot, rows, dh:dh + AUG_WIDTH] = jnp.broadcast_to(
                lanes_ref[0, hg + h:hg + h + 1, :], (tq, AUG_WIDTH)).astype(BF16)

        s = lax.dot_general(kc_ref[0, 0], qc_ref[slot], contract_last, preferred_element_type=F32)
        c_idx = lax.broadcasted_iota(jnp.int32, (nc, cols), 0)
        t_c = t0 + (lax.broadcasted_iota(jnp.int32, (nc, cols), 1) & (tq - 1))
        mask = (c_idx * CMP_STRIDE + (CMP_LEN - 1)) <= t_c
        s = jnp.where(mask, s, NEG_INF)
        m_c = jnp.max(s, axis=0, keepdims=True)
        p_c = jnp.where(mask, jnp.exp2(s - m_c), 0.0)
        l_c = jnp.sum(p_c, axis=0, keepdims=True)
        p_c = p_c * jnp.where(l_c > 0.0, 1.0 / l_c, 0.0)
        ocmp_ref[slot] = jnp.dot(vcT_ref[0, 0], p_c.astype(BF16), preferred_element_type=F32)

        p_sum = p_c[:, 0:tq]
        for h in range(1, hg):
            p_sum = p_sum + p_c[:, h * tq:(h + 1) * tq]
        hi = p_sum.astype(BF16)
        rem = p_sum - hi.astype(F32)
        mid = rem.astype(BF16)
        lo = (rem - mid.astype(F32)).astype(BF16)
        ovT = ovT_ref[...]
        imp = (jnp.dot(ovT, hi, preferred_element_type=F32)
               + jnp.dot(ovT, mid, preferred_element_type=F32)
               + jnp.dot(ovT, lo, preferred_element_type=F32))
        blk = lax.broadcasted_iota(jnp.int32, (n_blk, tq), 0)
        t_b = t0 + lax.broadcasted_iota(jnp.int32, (n_blk, tq), 1)
        cur = jnp.right_shift(t_b, SEL_BLK.bit_length() - 1)
        valid = blk * SEL_BLK <= t_b
        forced = (blk == 0) | (blk == cur) | (blk == cur - 1)
        score = jnp.where(valid, imp + FORCE_BONUS * forced.astype(F32), NEG_INF)
        score_ref[...] = score
        rank = jnp.zeros((n_blk, tq), F32)
        for j in range(n_blk):
            sj = score_ref[j:j + 1, :]
            beats = (sj > score) | ((sj == score) & (blk > j))
            rank = rank + beats.astype(F32)
        block_bias = jnp.where(rank < float(min(N_SEL, n_blk)), 0.0, NEG_INF)

        assert 2 * SLOPE_PIECES <= 8
        ones_rows = (lax.broadcasted_iota(jnp.int32, (8, tq), 0) < 2 * SLOPE_PIECES).astype(F32)
        extra_t = jnp.concatenate([block_bias, ones_rows,
                                   jnp.zeros((AUG_WIDTH - n_blk - 8, tq), F32)], axis=0).T
        for h in range(hg):
            qs_ref[slot, h * tq:(h + 1) * tq, dh:dh + AUG_WIDTH] = (
                extra_t * lanes_ref[0, h:h + 1, :]).astype(BF16)

    def win_part(c, mask=None):
        return kw_ref[0, 0, c * cw:(c + 1) * cw, :], [vwT_ref[0, 0, c]], mask

    def sel_part(c, n=1, mask=None):
        return ks_ref[0, 0, c * ck:(c + n) * ck, :], [vsT_ref[0, 0, c + i] for i in range(n)], mask

    def tile_step(tile):
        slot = tile % 2
        below_diagonal = lambda: kmq_ref[...] <= 0
        win_parts = [win_part(tile, below_diagonal)]
        win_parts += [win_part(tile - j) for j in range(1, n_back) if tile - j >= 0]
        if tile - n_back >= 0:
            win_parts.append(win_part(tile - n_back, lambda: kmq_ref[...] > 0))
        sel_parts = [sel_part(tile, mask=below_diagonal)]
        sel_parts += [sel_part(c, n=min(2, tile - c)) for c in range(0, tile, 2)]
        sel_unit = len(win_parts) * len(groups)
        issue_scores(qc_ref.at[slot], win_parts, 0)
        issue_scores(qs_ref.at[slot], sel_parts, sel_unit)
        if tile + 1 < n_tiles:
            prepare(qnext_ref, tile + 1, 1 - slot)
        reset(WIN)
        softmax_update(WIN, win_parts, 0)
        reset(SEL)
        softmax_update(SEL, sel_parts, sel_unit)

        gate_ref[...] = jax.nn.sigmoid(gl_ref[...] + gb_ref[...]).T
        o_win, o_slc = result(WIN), result(SEL)
        for h in range(hg):
            cs = slice(h * tq, (h + 1) * tq)
            gate = lambda branch: gate_ref[h * N_BRANCH + branch:h * N_BRANCH + branch + 1, :]
            o_h = gate(0) * ocmp_ref[slot, :, cs] + gate(1) * o_slc[:, cs] + gate(2) * o_win[:, cs]
            o_ref[:, h * dh:(h + 1) * dh] = o_h.T.astype(o_ref.dtype)

    for src, dst in zip(cast_in, cast_out):
        dst[...] = src[...].astype(dst.dtype)

    @pl.when(qi == 0)
    def _():
        prepare(q_ref, 0, 0)

    for tile in range(n_tiles):
        pl.when(qi == tile)(functools.partial(tile_step, tile))


def nsa_attention(z, gl, gate_b, q_norm, kcmp, vcmpT, ks, vsT, kw, vwT, casts, *, B, T):
    G, hg, dh, tq = NSA_KV_HEADS, NSA_GROUP, HEAD_DIM, Q_TILE
    nq = T // tq
    nc = kcmp.shape[2]
    n_blk = T // SEL_BLK
    kdim = dh + AUG_WIDTH
    cols = hg * tq
    assert SEL_CHUNK == WIN_CHUNK == tq
    n_units = (WINDOW // WIN_CHUNK + 1 + 1 + nq // 2) * (cols // NSA_COLS_PER_DOT)
    assert SEL_CHUNK % SEL_BLK == 0 and T % SEL_CHUNK == 0 and gl.shape[1] == G * LANES
    assert SEL_CHUNK % tq == 0 and n_blk % 8 == 0 and n_blk + 8 <= AUG_WIDTH

    slopes = np.exp2(-8.0 * (np.arange(NSA_HEADS, dtype=np.float64) + 1.0) / NSA_HEADS).reshape(G, hg)
    lanes = np.zeros((G, 2 * hg, AUG_WIDTH), np.float32)
    lanes[:, :hg, :n_blk] = 1.0
    for i, piece in enumerate(_bf16_pieces(slopes * LOG2E, SLOPE_PIECES)):
        for lane in (n_blk + i, n_blk + SLOPE_PIECES + i):
            lanes[:, :hg, lane] = piece
            lanes[:, hg:, lane] = piece
    cs = np.arange(nc)[:, None] * CMP_STRIDE
    bs = np.arange(n_blk)[None, :] * SEL_BLK
    overlap = np.clip(np.minimum(cs + CMP_LEN, bs + SEL_BLK) - np.maximum(cs, bs), 0, None) / CMP_LEN
    overlap[(T - CMP_LEN) // CMP_STRIDE + 1:] = 0.0
    ovT = jnp.asarray(overlap.T, dtype=BF16)
    hc = NSA_COLS_PER_DOT
    kmq = jnp.asarray(np.arange(SEL_CHUNK)[:, None] - (np.arange(hc)[None, :] % tq), dtype=jnp.int32)

    row = lambda b, qi: b * nq + qi
    per_bg = lambda shape: pl.BlockSpec((1, 1) + shape, lambda b, g, qi: (b, g) + (0,) * len(shape))
    const = lambda shape: pl.BlockSpec(shape, lambda b, g, qi: (0,) * len(shape))
    n_steps = B * G * nq
    assert all(w.shape[0] % (n_steps * 2 * SUBLANES) == 0 for w in casts)
    slab = lambda w: pl.BlockSpec((w.shape[0] // n_steps, w.shape[1]),
                                  lambda b, g, qi: ((b * G + g) * nq + qi, 0))
    return pl.pallas_call(
        functools.partial(_nsa_body, n_tiles=nq, n_casts=len(casts)),
        grid=(B, G, nq),
        in_specs=[pl.BlockSpec((tq, hg * dh), lambda b, g, qi: (row(b, qi), g)),
                  pl.BlockSpec((tq, hg * dh), lambda b, g, qi: (row(b, jnp.minimum(qi + 1, nq - 1)), g)),
                  pl.BlockSpec((tq, LANES), lambda b, g, qi: (row(b, qi), g)),
                  pl.BlockSpec((1, LANES), lambda b, g, qi: (0, g)),
                  const((1, dh)),
                  pl.BlockSpec((1, 2 * hg, AUG_WIDTH), lambda b, g, qi: (g, 0, 0)),
                  per_bg((nc, kdim)), per_bg((dh, nc)),
                  per_bg((T, kdim)), per_bg((T // SEL_CHUNK, dh + V_EXTRA, SEL_CHUNK)),
                  per_bg((T, kdim)), per_bg((T // WIN_CHUNK, dh + V_EXTRA, WIN_CHUNK)),
                  const((n_blk, nc)), const((SEL_CHUNK, hc))] + [slab(w) for w in casts],
        out_specs=[pl.BlockSpec((tq, hg * dh), lambda b, g, qi: (row(b, qi), g))]
                  + [slab(w) for w in casts],
        out_shape=[jax.ShapeDtypeStruct((B * T, NSA_WIDTH), BF16)]
                  + [jax.ShapeDtypeStruct(w.shape, BF16) for w in casts],
        scratch_shapes=[pltpu.VMEM((n_blk, tq), F32), pltpu.VMEM((LANES, tq), F32),
                        pltpu.VMEM((2, cols, kdim), BF16), pltpu.VMEM((2, cols, kdim), BF16),
                        pltpu.VMEM((2, 1, cols), F32),
                        pltpu.VMEM((2, dh + V_EXTRA, cols), F32), pltpu.VMEM((2, dh, cols), F32),
                        pltpu.VMEM((n_units, 2 * SEL_CHUNK, NSA_COLS_PER_DOT), F32)],
        compiler_params=_cparams("parallel", "parallel", "arbitrary"),
        name="nsa_attention",
    )(z, z, gl, gate_b, q_norm, jnp.asarray(lanes), kcmp, vcmpT, ks, vsT, kw, vwT, ovT, kmq, *casts)


def _conv_body(*refs, tt, n_casts):
    a_ref, b_ref, ah_ref, bh_ref, w_ref, cb_ref, lg_ref, lb_ref = refs[:8]
    cast_in, (o_ref, *cast_out) = refs[8:8 + n_casts], refs[8 + n_casts:9 + 2 * n_casts]
    h_ref, g_ref, y_ref = refs[9 + 2 * n_casts:]
    for src, dst in zip(cast_in, cast_out):
        dst[...] = src[...].astype(dst.dtype)
    C = a_ref.shape[1]
    halo = ah_ref[...].astype(F32) * jax.nn.sigmoid(bh_ref[...].astype(F32))
    h_ref[0:CONV_HALO, :] = jnp.where(pl.program_id(1) == 0, 0.0, halo)
    h_ref[CONV_HALO:CONV_HALO + tt, :] = a_ref[...].astype(F32) * jax.nn.sigmoid(b_ref[...].astype(F32))
    h_ref[CONV_HALO + tt:CONV_HALO + tt + SUBLANES, :] = jnp.zeros((SUBLANES, C), F32)
    lead = CONV_HALO - (CONV_KERNEL - 1)
    rows = tt + SUBLANES
    for s in range(C // LANES):
        cs = slice(s * LANES, (s + 1) * LANES)
        acc = None
        for r in range(SUBLANES):
            group = None
            for o in range(lead, lead + CONV_KERNEL):
                if o % SUBLANES == r:
                    term = w_ref[o - lead:o - lead + 1, cs] * h_ref[o - r:o - r + rows, cs]
                    group = term if group is None else group + term
            if r == 0:
                piece = group[0:tt, :]
            else:
                g_ref[r] = group
                piece = g_ref[r, r:r + tt, :]
            acc = piece if acc is None else acc + piece
        y_ref[:, cs] = acc + cb_ref[:, cs]
    for rb in range(0, tt, CONV_LN_ROWS):
        y = y_ref[rb:rb + CONV_LN_ROWS, :]
        mu = jnp.mean(y, axis=-1, keepdims=True)
        var = jnp.mean(jnp.square(y - mu), axis=-1, keepdims=True)
        yn = (y - mu) * lax.rsqrt(var + LN_EPS) * lg_ref[...] + lb_ref[...]
        o_ref[rb:rb + CONV_LN_ROWS, :] = _silu(yn).astype(o_ref.dtype)


def conformer_conv(z, conv_w, conv_b, ln_g, ln_b, casts, *, B, T, col0, C, tt):
    nt = T // tt
    ca = col0 // C
    hb = tt // CONV_HALO
    row = lambda b, t: b * nt + t
    halo_row = lambda b, t: jnp.maximum(row(b, t) * hb - 1, 0)
    const = lambda shape: pl.BlockSpec(shape, lambda b, t: (0,) * len(shape))
    n_steps = B * nt
    assert all(w.shape[0] % (n_steps * 2 * SUBLANES) == 0 for w in casts)
    slab = lambda w: pl.BlockSpec((w.shape[0] // n_steps, w.shape[1]), lambda b, t: (row(b, t), 0))
    return pl.pallas_call(
        functools.partial(_conv_body, tt=tt, n_casts=len(casts)),
        grid=(B, nt),
        in_specs=[pl.BlockSpec((tt, C), lambda b, t: (row(b, t), ca)),
                  pl.BlockSpec((tt, C), lambda b, t: (row(b, t), ca + 1)),
                  pl.BlockSpec((CONV_HALO, C), lambda b, t: (halo_row(b, t), ca)),
                  pl.BlockSpec((CONV_HALO, C), lambda b, t: (halo_row(b, t), ca + 1)),
                  const((CONV_KERNEL, C)), const((1, C)), const((1, C)), const((1, C))]
                 + [slab(w) for w in casts],
        out_specs=[pl.BlockSpec((tt, C), lambda b, t: (row(b, t), 0))] + [slab(w) for w in casts],
        out_shape=[jax.ShapeDtypeStruct((B * T, C), BF16)]
                  + [jax.ShapeDtypeStruct(w.shape, BF16) for w in casts],
        scratch_shapes=[pltpu.VMEM((CONV_HALO + tt + SUBLANES, C), F32),
                        pltpu.VMEM((SUBLANES, tt + SUBLANES, LANES), F32),
                        pltpu.VMEM((tt, C), F32)],
        compiler_params=_cparams("parallel", "parallel"),
        name="conformer_conv",
    )(z, z, z, z, conv_w, conv_b, ln_g, ln_b, *casts)


def _mem_attn_body(x_ref, gq_ref, wq_ref, mem_ref, gkv_ref, wk_ref, wv_ref, qn_ref, kn_ref, wo_ref, gn_ref,
                   o_ref, on_ref, k_scr, v_scr):
    dh = HEAD_DIM
    width = MEM_HEADS * dh
    scale = dh ** -0.5
    heads = [slice(h * dh, (h + 1) * dh) for h in range(MEM_HEADS)]

    @pl.when(pl.program_id(1) == 0)
    def _():
        mem_n = _rms(mem_ref[...], gkv_ref[...]).astype(BF16)
        k = jnp.dot(mem_n, wk_ref[...], preferred_element_type=F32)
        for cs in heads:
            k_scr[:, cs] = _rms(k[:, cs], kn_ref[...]).astype(BF16)
        v_scr[...] = jnp.dot(mem_n, wv_ref[...], preferred_element_type=F32).astype(BF16)

    q_all = jnp.dot(_rms(x_ref[...], gq_ref[...]).astype(BF16), wq_ref[...], preferred_element_type=F32)
    scores = []
    for cs in heads:
        q = (_rms(q_all[:, cs], qn_ref[...]) * scale).astype(BF16)
        scores.append(lax.dot_general(q, k_scr[:, cs], (((1,), (1,)), ((), ())), preferred_element_type=F32))
    outs = []
    for h, s in enumerate(scores):
        v = v_scr[:, heads[h]]
        p = jnp.exp(s - jnp.max(s, axis=-1, keepdims=True))
        inv = 1.0 / jnp.sum(p, axis=-1, keepdims=True)
        outs.append((jnp.dot(p.astype(BF16), v, preferred_element_type=F32) * inv).astype(BF16))
    o = jnp.concatenate(outs, axis=1)
    y = x_ref[...] + jnp.dot(o, wo_ref[...], preferred_element_type=F32)
    o_ref[...] = y
    on_ref[...] = _rms(y, gn_ref[...]).astype(on_ref.dtype)


def mem_attention(x, q_gain, w_mq, mem, kv_gain, w_mk, w_mv, mq_norm, mk_norm, w_mo, next_norm, *, B, T, M, tt):
    D = x.shape[1]
    width = MEM_HEADS * HEAD_DIM
    nt = T // tt
    row = lambda b, t: b * nt + t
    const = lambda shape: pl.BlockSpec(shape, lambda b, t: (0,) * len(shape))
    return pl.pallas_call(
        _mem_attn_body,
        grid=(B, nt),
        in_specs=[pl.BlockSpec((tt, D), lambda b, t: (row(b, t), 0)),
                  const((1, D)), const((D, width)),
                  pl.BlockSpec((M, D), lambda b, t: (b, 0)),
                  const((1, D)), const((D, width)), const((D, width)),
                  const((1, HEAD_DIM)), const((1, HEAD_DIM)), const((width, D)),
                  const((1, D))],
        out_specs=[pl.BlockSpec((tt, D), lambda b, t: (row(b, t), 0)),
                   pl.BlockSpec((tt, D), lambda b, t: (row(b, t), 0))],
        out_shape=[jax.ShapeDtypeStruct((B * T, D), F32),
                   jax.ShapeDtypeStruct((B * T, D), BF16)],
        scratch_shapes=[pltpu.VMEM((M, width), BF16), pltpu.VMEM((M, width), BF16)],
        compiler_params=_cparams("parallel", "arbitrary"),
        name="mem_attention",
    )(x, q_gain, w_mq, mem, kv_gain, w_mk, w_mv, mq_norm, mk_norm, w_mo, next_norm)


def kernel(x, mem, norm_mix, w_in, gate_b, q_norm, k_norm_cmp, k_norm_slc, k_norm_win, cmp_pos_k, cmp_pos_v, cmp_k_w1, cmp_k_w2, cmp_v_w1, cmp_v_w2, conv_w, conv_b, conv_ln_g, conv_ln_b, w_out, norm_mem_q, norm_mem_kv, w_mq, w_mk, w_mv, mq_norm, mk_norm, w_mo, norm_ffn, w_gate, w_up, w_down):
    B, T, D = x.shape
    M = mem.shape[1]
    depth = w_in.shape[0]
    G, hg = NSA_KV_HEADS, NSA_GROUP
    C = D - NSA_WIDTH
    n_gate = N_BRANCH * NSA_HEADS
    kv_all = 6 * KV_WIDTH
    assert w_in.shape[2] == NSA_WIDTH + kv_all + n_gate + 2 * C
    assert C == NSA_WIDTH, "column blocking below assumes equal NSA and conv widths"

    row1 = lambda v: v.reshape(1, -1)
    xf = x.reshape(B * T, D)
    memf = mem.reshape(B * M, D)
    u0 = NSA_WIDTH + kv_all + n_gate
    for l in range(depth):
        w = w_in[l]
        w_t = jnp.swapaxes(w, 0, 1)
        wg = w[:, NSA_WIDTH + kv_all:u0].reshape(D, G, hg * N_BRANCH)
        wg = jnp.pad(wg, ((0, 0), (0, 0), (0, LANES - hg * N_BRANCH))).reshape(D, G * LANES)
        gb = jnp.pad(gate_b[l].reshape(G, hg * N_BRANCH), ((0, 0), (0, LANES - hg * N_BRANCH))).reshape(1, G * LANES)

        xn = rmsnorm_bf16(xf, row1(norm_mix[l]), tm=TILE_RMSNORM)
        z, gl = in_projection(xn, w_t, wg, tm=TILE_IN_PROJ[0], tn=TILE_IN_PROJ[1],
                              segments=((0, NSA_WIDTH), (u0, 2 * C), (NSA_WIDTH, kv_all)))
        kv_col0 = NSA_WIDTH + 2 * C
        kc, vc, ks, vsT, kw, vwT = kv_prep(z, row1(k_norm_slc[l]), row1(k_norm_win[l]),
                                           B=B, T=T, col0=kv_col0, tt=TILE_KV_PREP)
        kcmp, vcmpT = compress_kv(kc, vc, cmp_pos_k[l], cmp_pos_v[l],
                                  cmp_k_w1[l], cmp_k_w2[l], cmp_v_w1[l], cmp_v_w2[l], row1(k_norm_cmp[l]),
                                  n_blk=T // SEL_BLK)
        o_nsa, w_gate_bf16, w_up_bf16 = nsa_attention(
            z, gl, gb, row1(q_norm[l]), kcmp, vcmpT, ks, vsT, kw, vwT, [w_gate[l], w_up[l]], B=B, T=T)
        o_conv, w_out_bf16, w_down_bf16, w_mq_bf16, w_mk_bf16, w_mv_bf16, w_mo_bf16 = conformer_conv(
            z, conv_w[l], row1(conv_b[l]), row1(conv_ln_g[l]), row1(conv_ln_b[l]),
            [w_out[l], w_down[l], w_mq[l], w_mk[l], w_mv[l], w_mo[l]],
            B=B, T=T, col0=NSA_WIDTH, C=C, tt=TILE_CONV)
        xf = concat_matmul_residual(o_nsa, o_conv, w_out_bf16, xf, tm=TILE_OUT_PROJ[0], tn=TILE_OUT_PROJ[1])

        xf, xn = mem_attention(xf, row1(norm_mem_q[l]), w_mq_bf16,
                               memf, row1(norm_mem_kv[l]), w_mk_bf16, w_mv_bf16,
                               row1(mq_norm[l]), row1(mk_norm[l]), w_mo_bf16,
                               row1(norm_ffn[l]), B=B, T=T, M=M, tt=TILE_MEM_ATTN)

        hidden = swiglu_up(xn, w_gate_bf16, w_up_bf16, tm=TILE_FFN_UP[0], tn=TILE_FFN_UP[1])
        xf = matmul_residual(hidden, w_down_bf16, xf, tm=TILE_FFN_DOWN[0], tn=TILE_FFN_DOWN[1])
    return xf.reshape(B, T, D)
```
